```python
import math
import jax, jax.numpy as jnp
from jax import lax
import numpy as np

D_MODEL = 1024
BATCH = 2
SEQ = 8192
DEPTH = 1

GRID_W = 64
SSM_HEADS = 16
SSM_HEAD_DIM = 64
SSM_INNER = SSM_HEADS * SSM_HEAD_DIM
SSM_GROUPS = 2
SSM_STATE = 128
SSM_CONV = 5
SSM_CHUNK = 128
CONV_CH = SSM_INNER + 2 * SSM_GROUPS * SSM_STATE
ATTN_HEADS = 8
ATTN_KV_HEADS = 2
ATTN_HEAD_DIM = 128
ATTN_Q_WIDTH = ATTN_HEADS * ATTN_HEAD_DIM
ATTN_KV_WIDTH = ATTN_KV_HEADS * ATTN_HEAD_DIM
ATTN_Q_BLOCK = 128
ROPE_THETA = 10000.0
ROPE_AXIS_DIM = ATTN_HEAD_DIM // 2
N_BRANCHES = 2
Z_END = SSM_INNER
XBC_END = Z_END + CONV_CH
DTF_END = XBC_END + SSM_HEADS
DTB_END = DTF_END + SSM_HEADS
Q_END = DTB_END + ATTN_Q_WIDTH
K_END = Q_END + ATTN_KV_WIDTH
V_END = K_END + ATTN_KV_WIDTH
IN_COLS = V_END + N_BRANCHES * D_MODEL
SPLITS = [Z_END, XBC_END, DTF_END, DTB_END, Q_END, K_END, V_END]
N_EXPERTS = 32
TOP_K = 4
D_EXPERT = D_MODEL
SWIGLU_LIMIT = 7.0
SWIGLU_ALPHA = 1.702
MOE_BLOCK = 128
NORM_EPS = 1e-6

kernel_name = "hybrid_ssd_axial_gqa_moe_encoder"


def rmsnorm(x, g):
    xf = x.astype(jnp.float32)
    y = xf * lax.rsqrt(jnp.mean(xf * xf, axis=-1, keepdims=True) + NORM_EPS)
    return y.astype(x.dtype) * g


def depthwise_conv(u, w, b):
    pad = (SSM_CONV - 1) // 2
    y = lax.conv_general_dilated(u, w[:, None, :], window_strides=(1,),
                                 padding=[(pad, SSM_CONV - 1 - pad)],
                                 dimension_numbers=('NWC', 'WIO', 'NWC'),
                                 feature_group_count=u.shape[-1])
    return y + b


def ssd_scan(x, dt, A, Bm, Cm):
    b, l, h, p = x.shape
    g, n = Bm.shape[-2:]
    hg = h // g
    c = l // SSM_CHUNK
    L = SSM_CHUNK
    xr = (x * dt[..., None].astype(x.dtype)).reshape(b, c, L, g, hg, p)
    a = (dt * A.astype(jnp.float32)).reshape(b, c, L, g, hg)
    a_cs = jnp.cumsum(a, axis=2)
    Br = Bm.reshape(b, c, L, g, n)
    Cr = Cm.reshape(b, c, L, g, n)
    seg = a_cs[:, :, :, None] - a_cs[:, :, None, :]
    mask = jnp.tril(jnp.ones((L, L), dtype=bool))[:, :, None, None]
    lmat = jnp.exp(jnp.where(mask, seg, -jnp.inf))
    cb = jnp.einsum('bclgn,bcsgn->bclsg', Cr, Br)
    y_diag = jnp.einsum('bclsgh,bcsghp->bclghp', cb[..., None] * lmat, xr)
    decay_to_end = jnp.exp(a_cs[:, :, -1:] - a_cs)
    states = jnp.einsum('bclgn,bclghp->bcghpn', Br, xr * decay_to_end[..., None])
    chunk_decay = jnp.exp(a_cs[:, :, -1])

    def step(carry, inp):
        st, dec = inp
        return carry * dec[..., None, None] + st, carry

    init = jnp.zeros((b, g, hg, p, n), states.dtype)
    _, prev = lax.scan(step, init, (jnp.moveaxis(states, 1, 0), jnp.moveaxis(chunk_decay, 1, 0)))
    prev = jnp.moveaxis(prev, 0, 1)
    y_off = jnp.einsum('bclgn,bcghpn->bclghp', Cr, prev) * jnp.exp(a_cs)[..., None]
    return (y_diag + y_off).reshape(b, l, h, p).astype(x.dtype)


def rope_axis(part, ang):
    m = ang.shape[-1]
    cos = jnp.cos(ang)[None, :, None, :].astype(part.dtype)
    sin = jnp.sin(ang)[None, :, None, :].astype(part.dtype)
    p1, p2 = part[..., :m], part[..., m:]
    return jnp.concatenate([p1 * cos - p2 * sin, p2 * cos + p1 * sin], axis=-1)


def axial_rope(t, row, col):
    inv_freq = ROPE_THETA ** (-jnp.arange(0, ROPE_AXIS_DIM, 2, dtype=jnp.float32) / ROPE_AXIS_DIM)
    ang_r = row.astype(jnp.float32)[:, None] * inv_freq
    ang_c = col.astype(jnp.float32)[:, None] * inv_freq
    return jnp.concatenate([rope_axis(t[..., :ROPE_AXIS_DIM], ang_r),
                            rope_axis(t[..., ROPE_AXIS_DIM:], ang_c)], axis=-1)


def blocked_attention(q, k, v):
    b, s, hq, d = q.shape
    hkv = k.shape[2]
    grp = hq // hkv
    nb = s // ATTN_Q_BLOCK
    qb = q.reshape(b, nb, ATTN_Q_BLOCK, hkv, grp, d).transpose(1, 0, 2, 3, 4, 5)
    scale = d ** -0.5

    def one_block(qblk):
        sc = jnp.einsum('bqkgd,bskd->bkgqs', qblk, k).astype(jnp.float32) * scale
        pr = jax.nn.softmax(sc, axis=-1).astype(v.dtype)
        return jnp.einsum('bkgqs,bskd->bqkgd', pr, v)

    o = lax.map(one_block, qb)
    return o.transpose(1, 0, 2, 3, 4, 5).reshape(b, s, hq * d)


def token_mixer(h, w_in, conv_w, conv_b, dt_bias_f, dt_bias_b, a_log_f, a_log_b, d_skip,
                g_ssm, q_norm_g, k_norm_g, w_br_ssm, w_br_attn, w_out):
    b, s, _ = h.shape
    proj = h @ w_in
    z, xbc, dt_f, dt_b, q, k, v, gates = jnp.split(proj, SPLITS, axis=-1)

    xbc = jax.nn.silu(depthwise_conv(xbc, conv_w, conv_b))
    xs, Bm, Cm = jnp.split(xbc, [SSM_INNER, SSM_INNER + SSM_GROUPS * SSM_STATE], axis=-1)
    xs = xs.reshape(b, s, SSM_HEADS, SSM_HEAD_DIM)
    Bm = Bm.reshape(b, s, SSM_GROUPS, SSM_STATE)
    Cm = Cm.reshape(b, s, SSM_GROUPS, SSM_STATE)
    dtf = jax.nn.softplus((dt_f + dt_bias_f).astype(jnp.float32))
    dtb = jax.nn.softplus((dt_b + dt_bias_b).astype(jnp.float32))
    flip = lambda t: jnp.flip(t, axis=1)
    y_f = ssd_scan(xs, dtf, -jnp.exp(a_log_f.astype(jnp.float32)), Bm, Cm)
    y_b = flip(ssd_scan(flip(xs), flip(dtb), -jnp.exp(a_log_b.astype(jnp.float32)), flip(Bm), flip(Cm)))
    y = (y_f + y_b + xs * d_skip[:, None]).reshape(b, s, SSM_INNER)
    y = rmsnorm(y * jax.nn.silu(z), g_ssm)
    br_ssm = y @ w_br_ssm

    q = rmsnorm(q.reshape(b, s, ATTN_HEADS, ATTN_HEAD_DIM), q_norm_g)
    k = rmsnorm(k.reshape(b, s, ATTN_KV_HEADS, ATTN_HEAD_DIM), k_norm_g)
    v = v.reshape(b, s, ATTN_KV_HEADS, ATTN_HEAD_DIM)
    rows = s // GRID_W
    row = jnp.repeat(jnp.arange(rows, dtype=jnp.int32), GRID_W)
    col = jnp.tile(jnp.arange(GRID_W, dtype=jnp.int32), rows)
    q = axial_rope(q, row, col)
    k = axial_rope(k, row, col)
    br_attn = blocked_attention(q, k, v) @ w_br_attn

    g_s, g_a = jnp.split(jax.nn.sigmoid(gates), N_BRANCHES, axis=-1)
    return (g_s * br_ssm + g_a * br_attn) @ w_out


def moe_ffn(h, w_router, b_router, w_mlp1, b_mlp1, w_mlp2, b_mlp2):
    b, s, d = h.shape
    n = b * s
    hf = h.reshape(n, d)
    logits = (hf @ w_router + b_router).astype(jnp.float32)
    top_val, top_idx = lax.top_k(logits, TOP_K)
    top_w = jax.nn.softmax(top_val, axis=-1).astype(h.dtype)
    n_assign = n * TOP_K
    flat_e = top_idx.reshape(-1).astype(jnp.int32)
    flat_tok = jnp.arange(n_assign, dtype=jnp.int32) // TOP_K
    flat_w = top_w.reshape(-1)
    order = jnp.argsort(flat_e)
    sorted_e = flat_e[order]
    counts = jnp.bincount(flat_e, length=N_EXPERTS).astype(jnp.int32)
    padded = (counts + MOE_BLOCK - 1) // MOE_BLOCK * MOE_BLOCK
    start = jnp.cumsum(counts) - counts
    pad_end = jnp.cumsum(padded)
    pad_start = pad_end - padded
    dest = pad_start[sorted_e] + jnp.arange(n_assign, dtype=jnp.int32) - start[sorted_e]
    n_blocks = -(-n_assign // MOE_BLOCK) + N_EXPERTS
    n_rows = n_blocks * MOE_BLOCK
    row_tok = jnp.zeros((n_rows,), jnp.int32).at[dest].set(flat_tok[order])
    row_w = jnp.zeros((n_rows,), h.dtype).at[dest].set(flat_w[order])
    block_e = jnp.minimum(jnp.searchsorted(pad_end, jnp.arange(n_blocks, dtype=jnp.int32) * MOE_BLOCK,
                                           side='right'), N_EXPERTS - 1)
    xb = hf[row_tok].reshape(n_blocks, MOE_BLOCK, d)

    def expert_block(args):
        xblk, e = args
        hm = xblk @ w_mlp1[e] + b_mlp1[e]
        gate, up = hm[:, :D_EXPERT], hm[:, D_EXPERT:]
        gate = jnp.minimum(gate, SWIGLU_LIMIT)
        up = jnp.clip(up, -SWIGLU_LIMIT, SWIGLU_LIMIT)
        act = gate * jax.nn.sigmoid(SWIGLU_ALPHA * gate) * (up + 1)
        return act @ w_mlp2[e] + b_mlp2[e]

    yb = lax.map(expert_block, (xb, block_e)).reshape(n_rows, d)
    y = jax.ops.segment_sum(yb * row_w[:, None], row_tok, num_segments=n)
    return y.reshape(b, s, d)


def setup_inputs(seed: int = 0) -> dict:
    key = jax.random.key(seed)
    ks = jax.random.split(key, 26)
    nrm = lambda k, shape, sc: jax.random.normal(k, shape, jnp.float32) * sc
    lo, hi = math.log(0.001), math.log(0.1)
    dt_f = jnp.exp(jax.random.uniform(ks[5], (DEPTH, SSM_HEADS)) * (hi - lo) + lo)
    dt_b = jnp.exp(jax.random.uniform(ks[6], (DEPTH, SSM_HEADS)) * (hi - lo) + lo)
    return {
        "x": nrm(ks[0], (BATCH, SEQ, D_MODEL), 1.0),
        "g_mix": 1.0 + nrm(ks[1], (DEPTH, D_MODEL), 0.02),
        "w_in": nrm(ks[2], (DEPTH, D_MODEL, IN_COLS), D_MODEL ** -0.5),
        "conv_w": nrm(ks[3], (DEPTH, SSM_CONV, CONV_CH), SSM_CONV ** -0.5),
        "conv_b": nrm(ks[4], (DEPTH, CONV_CH), 0.02),
        "dt_bias_f": dt_f + jnp.log(-jnp.expm1(-dt_f)),
        "dt_bias_b": dt_b + jnp.log(-jnp.expm1(-dt_b)),
        "a_log_f": jnp.log(jax.random.uniform(ks[7], (DEPTH, SSM_HEADS), minval=1.0, maxval=16.0)),
        "a_log_b": jnp.log(jax.random.uniform(ks[8], (DEPTH, SSM_HEADS), minval=1.0, maxval=16.0)),
        "d_skip": 1.0 + nrm(ks[9], (DEPTH, SSM_HEADS), 0.02),
        "g_ssm": 1.0 + nrm(ks[10], (DEPTH, SSM_INNER), 0.02),
        "q_norm_g": 1.0 + nrm(ks[11], (DEPTH, ATTN_HEAD_DIM), 0.02),
        "k_norm_g": 1.0 + nrm(ks[12], (DEPTH, ATTN_HEAD_DIM), 0.02),
        "w_br_ssm": nrm(ks[13], (DEPTH, SSM_INNER, D_MODEL), SSM_INNER ** -0.5),
        "w_br_attn": nrm(ks[14], (DEPTH, ATTN_Q_WIDTH, D_MODEL), ATTN_Q_WIDTH ** -0.5),
        "w_out": nrm(ks[15], (DEPTH, D_MODEL, D_MODEL), D_MODEL ** -0.5),
        "g_ffn": 1.0 + nrm(ks[16], (DEPTH, D_MODEL), 0.02),
        "w_router": nrm(ks[17], (DEPTH, D_MODEL, N_EXPERTS), D_MODEL ** -0.5),
        "b_router": nrm(ks[18], (DEPTH, N_EXPERTS), 0.01),
        "w_mlp1": nrm(ks[19], (DEPTH, N_EXPERTS, D_MODEL, 2 * D_EXPERT), D_MODEL ** -0.5),
        "b_mlp1": nrm(ks[20], (DEPTH, N_EXPERTS, 2 * D_EXPERT), 0.02),
        "w_mlp2": nrm(ks[21], (DEPTH, N_EXPERTS, D_EXPERT, D_MODEL), D_EXPERT ** -0.5),
        "b_mlp2": nrm(ks[22], (DEPTH, N_EXPERTS, D_MODEL), 0.02),
        "g_final": 1.0 + nrm(ks[23], (D_MODEL,), 0.02),
    }


def reference(x, g_mix, w_in, conv_w, conv_b, dt_bias_f, dt_bias_b, a_log_f, a_log_b, d_skip,
              g_ssm, q_norm_g, k_norm_g, w_br_ssm, w_br_attn, w_out, g_ffn, w_router, b_router,
              w_mlp1, b_mlp1, w_mlp2, b_mlp2, g_final):
    for l in range(DEPTH):
        x = x + token_mixer(rmsnorm(x, g_mix[l]), w_in[l], conv_w[l], conv_b[l],
                            dt_bias_f[l], dt_bias_b[l], a_log_f[l], a_log_b[l], d_skip[l],
                            g_ssm[l], q_norm_g[l], k_norm_g[l], w_br_ssm[l], w_br_attn[l], w_out[l])
        x = x + moe_ffn(rmsnorm(x, g_ffn[l]), w_router[l], b_router[l],
                        w_mlp1[l], b_mlp1[l], w_mlp2[l], b_mlp2[l])
    return rmsnorm(x, g_final)
```

```python
import functools
import math

import jax
import jax.numpy as jnp
from jax import lax
from jax.experimental import pallas as pl
from jax.experimental.pallas import tpu as pltpu

F32 = jnp.float32
BF16 = jnp.bfloat16

NORM_EPS = 1e-6
GRID_W = 64
SSM_HEADS = 16
SSM_HEAD_DIM = 64
SSM_INNER = SSM_HEADS * SSM_HEAD_DIM
SSM_GROUPS = 2
SSM_STATE = 128
SSM_CONV = 5
CONV_CH = SSM_INNER + 2 * SSM_GROUPS * SSM_STATE
ATTN_HEADS = 8
ATTN_KV_HEADS = 2
ATTN_HEAD_DIM = 128
ROPE_THETA = 10000.0
N_EXPERTS = 32
TOP_K = 4
SWIGLU_LIMIT = 7.0
SWIGLU_ALPHA = 1.702

LANES = 128
BF16_SUBLANES = 16
VMEM_LIMIT = 56 * 1024 * 1024

Z_OFF, Q_OFF, GATE_OFF, XBC_OFF = 0, 1024, 2048, 4096
K_OFF, V_OFF, PROJ_COLS = 5632, 5888, 6144


def _cparams(sem):
    return pltpu.CompilerParams(dimension_semantics=sem, vmem_limit_bytes=VMEM_LIMIT)


def _sigmoid(x):
    return 1.0 / (1.0 + jnp.exp(-x))


def _softplus(x):
    return jnp.maximum(x, 0.0) + jnp.log(1.0 + jnp.exp(-jnp.abs(x)))


def _rms(x, g):
    ms = jnp.mean(x * x, axis=-1, keepdims=True)
    return x * lax.rsqrt(ms + NORM_EPS) * g


def _inproj_kernel(x_ref, g_ref, w_ref, wdt_ref, o_ref, dt_ref, h_scr):
    @pl.when(pl.program_id(1) == 0)
    def _():
        hb = _rms(x_ref[...], g_ref[...]).astype(BF16)
        h_scr[...] = hb
        dt_ref[...] = jnp.dot(hb, wdt_ref[...], preferred_element_type=F32)

    o_ref[...] = jnp.dot(h_scr[...], w_ref[...], preferred_element_type=F32).astype(o_ref.dtype)


def _in_proj(x2, g_mix, w_main, w_dt):
    n, d = x2.shape
    tm = min(1024, n)
    tn = 1024
    return pl.pallas_call(
        _inproj_kernel,
        grid=(n // tm, PROJ_COLS // tn),
        in_specs=[
            pl.BlockSpec((tm, d), lambda i, j: (i, 0)),
            pl.BlockSpec((1, d), lambda i, j: (0, 0)),
            pl.BlockSpec((d, tn), lambda i, j: (0, j)),
            pl.BlockSpec((d, LANES), lambda i, j: (0, 0)),
        ],
        out_specs=[
            pl.BlockSpec((tm, tn), lambda i, j: (i, j)),
            pl.BlockSpec((tm, LANES), lambda i, j: (i, 0)),
        ],
        out_shape=[
            jax.ShapeDtypeStruct((n, PROJ_COLS), BF16),
            jax.ShapeDtypeStruct((n, LANES), F32),
        ],
        scratch_shapes=[pltpu.VMEM((tm, d), BF16)],
        compiler_params=_cparams(("arbitrary", "arbitrary")),
    )(x2, g_mix, w_main, w_dt)


def _conv_kernel(prev_ref, cur_ref, next_ref, w_ref, b_ref, o_ref, scr):
    s = pl.program_id(1)
    ts = cur_ref.shape[1]
    halo = BF16_SUBLANES
    pad = (SSM_CONV - 1) // 2
    prev = prev_ref[0].astype(F32)
    nxt = next_ref[0].astype(F32)
    scr[0:halo, :] = jnp.where(s == 0, 0.0, prev)
    scr[halo:halo + ts, :] = cur_ref[0].astype(F32)
    scr[halo + ts:2 * halo + ts, :] = jnp.where(s == pl.num_programs(1) - 1, 0.0, nxt)
    acc = jnp.zeros((ts, cur_ref.shape[2]), F32) + b_ref[...]
    for k in range(SSM_CONV):
        acc = acc + w_ref[k:k + 1, :] * scr[halo - pad + k:halo - pad + k + ts, :]
    o_ref[0] = (acc * _sigmoid(acc)).astype(o_ref.dtype)


def _conv(proj3, conv_w, conv_b):
    b, s, _ = proj3.shape
    ts = min(512, s)
    tc = 512
    halo = BF16_SUBLANES
    hb = ts // halo
    col0 = XBC_OFF // tc
    return pl.pallas_call(
        _conv_kernel,
        grid=(b, s // ts, CONV_CH // tc),
        in_specs=[
            pl.BlockSpec((1, halo, tc), lambda bi, si, ci: (bi, jnp.maximum(si * hb - 1, 0), col0 + ci)),
            pl.BlockSpec((1, ts, tc), lambda bi, si, ci: (bi, si, col0 + ci)),
            pl.BlockSpec((1, halo, tc),
                         lambda bi, si, ci: (bi, jnp.minimum((si + 1) * hb, s // halo - 1), col0 + ci)),
            pl.BlockSpec((SSM_CONV, tc), lambda bi, si, ci: (0, ci)),
            pl.BlockSpec((1, tc), lambda bi, si, ci: (0, ci)),
        ],
        out_specs=pl.BlockSpec((1, ts, tc), lambda bi, si, ci: (bi, si, ci)),
        out_shape=jax.ShapeDtypeStruct((b, s, CONV_CH), BF16),
        scratch_shapes=[pltpu.VMEM((ts + 2 * halo, tc), F32)],
        compiler_params=_cparams(("arbitrary", "arbitrary", "arbitrary")),
    )(proj3, proj3, proj3, conv_w, conv_b)


def _ssd_kernel(xf_ref, xb_ref, dtf_ref, dtb_ref, dttf_ref, dttb_ref, brow_ref, bcol_ref,
                arow_ref, acol_ref, yf_ref, yb_ref, st_ref):
    L = xf_ref.shape[1]
    hg = SSM_HEADS // SSM_GROUPS
    pairs = hg // 2
    hi = lax.Precision.HIGHEST

    @pl.when(pl.program_id(1) == 0)
    def _():
        st_ref[...] = jnp.zeros_like(st_ref)

    rows = lax.broadcasted_iota(jnp.int32, (L, L), 0)
    cols = lax.broadcasted_iota(jnp.int32, (L, L), 1)
    lower = rows >= cols
    upper = rows <= cols
    ltri = lower.astype(F32)
    utri = upper.astype(F32)
    lane = lax.broadcasted_iota(jnp.int32, (L, LANES), 1)
    first_half = lane < SSM_HEAD_DIM
    lane1 = lax.broadcasted_iota(jnp.int32, (1, LANES), 1)
    a_row = -jnp.exp(arow_ref[...])
    a_col = -jnp.exp(acol_ref[...])

    for d in range(2):
        x_ref, dt_ref, dtt_ref, y_ref = ((xf_ref, dtf_ref, dttf_ref, yf_ref) if d == 0
                                         else (xb_ref, dtb_ref, dttb_ref, yb_ref))
        dt = _softplus(dt_ref[0] + brow_ref[...])
        a = dt * a_row
        a_t = _softplus(dtt_ref[0] + bcol_ref[...]) * a_col
        if d == 0:
            cs_col = jnp.dot(ltri, a, precision=hi, preferred_element_type=F32)
            cs_row = jnp.dot(a_t, utri, precision=hi, preferred_element_type=F32)
            tot = cs_col[L - 1:L, :]
            mask = lower
        else:
            cs_col = jnp.dot(utri, a, precision=hi, preferred_element_type=F32)
            cs_row = jnp.dot(a_t, ltri, precision=hi, preferred_element_type=F32)
            tot = cs_col[0:1, :]
            mask = upper
        decay_out = jnp.exp(tot - cs_col)
        decay_in = jnp.exp(cs_col)
        chunk_decay = jnp.exp(tot)

        for g in range(SSM_GROUPS):
            boff = SSM_INNER + g * SSM_STATE
            coff = SSM_INNER + SSM_GROUPS * SSM_STATE + g * SSM_STATE
            bm = x_ref[0, :, boff:boff + SSM_STATE]
            cm = x_ref[0, :, coff:coff + SSM_STATE]
            cb = lax.dot_general(cm, bm, (((1,), (1,)), ((), ())), preferred_element_type=F32)
            st = st_ref[d, g]
            y_off = jnp.dot(cm, st.astype(BF16), preferred_element_type=F32)
            xs_out = []
            cd_parts = []
            for pr in range(pairs):
                h0 = d * SSM_HEADS + g * hg + 2 * pr
                xoff = (g * pairs + pr) * LANES

                def pair_cols(arr, h0=h0):
                    return jnp.where(first_half, arr[:, h0:h0 + 1], arr[:, h0 + 1:h0 + 2])

                xdt = x_ref[0, :, xoff:xoff + LANES].astype(F32) * pair_cols(dt)
                ms = []
                for hh in (h0, h0 + 1):
                    seg = cs_col[:, hh:hh + 1] - cs_row[hh:hh + 1, :]
                    ms.append((cb * jnp.exp(jnp.where(mask, seg, -jnp.inf))).astype(BF16))
                lhs = jnp.concatenate(ms, axis=1)
                xb16 = xdt.astype(BF16)
                zero = jnp.zeros_like(xb16)
                rhs = jnp.concatenate([jnp.where(first_half, xb16, zero),
                                       jnp.where(first_half, zero, xb16)], axis=0)
                y = jnp.dot(lhs, rhs, preferred_element_type=F32)
                y = y + y_off[:, pr * LANES:(pr + 1) * LANES] * pair_cols(decay_in)
                y_ref[0, :, xoff:xoff + LANES] = y.astype(y_ref.dtype)
                xs_out.append((xdt * pair_cols(decay_out)).astype(BF16))
                cd_parts.append(jnp.where(lane1 < SSM_HEAD_DIM, chunk_decay[:, h0:h0 + 1],
                                          chunk_decay[:, h0 + 1:h0 + 2]))
            xs_cat = jnp.concatenate(xs_out, axis=1)
            new_st = lax.dot_general(bm, xs_cat, (((0,), (0,)), ((), ())), preferred_element_type=F32)
            st_ref[d, g] = st * jnp.concatenate(cd_parts, axis=1) + new_st


def _ssd(xbc, dt3, dtt3, bias_row, bias_col, alog_row, alog_col):
    b, s, _ = xbc.shape
    L = min(128, s)
    nc = s // L
    hg = SSM_HEADS // SSM_GROUPS
    fwd = lambda bi, ci: (bi, ci, 0)
    bwd = lambda bi, ci: (bi, nc - 1 - ci, 0)
    fwd_t = lambda bi, ci: (bi, 0, ci)
    bwd_t = lambda bi, ci: (bi, 0, nc - 1 - ci)
    const = lambda bi, ci: (0, 0)
    return pl.pallas_call(
        _ssd_kernel,
        grid=(b, nc),
        in_specs=[
            pl.BlockSpec((1, L, CONV_CH), fwd),
            pl.BlockSpec((1, L, CONV_CH), bwd),
            pl.BlockSpec((1, L, LANES), fwd),
            pl.BlockSpec((1, L, LANES), bwd),
            pl.BlockSpec((1, 2 * SSM_HEADS, L), fwd_t),
            pl.BlockSpec((1, 2 * SSM_HEADS, L), bwd_t),
            pl.BlockSpec((1, LANES), const),
            pl.BlockSpec((2 * SSM_HEADS, 1), const),
            pl.BlockSpec((1, LANES), const),
            pl.BlockSpec((2 * SSM_HEADS, 1), const),
        ],
        out_specs=[
            pl.BlockSpec((1, L, SSM_INNER), fwd),
            pl.BlockSpec((1, L, SSM_INNER), bwd),
        ],
        out_shape=[jax.ShapeDtypeStruct((b, s, SSM_INNER), BF16)] * 2,
        scratch_shapes=[pltpu.VMEM((2, SSM_GROUPS, SSM_STATE, hg * SSM_HEAD_DIM), F32)],
        compiler_params=_cparams(("arbitrary", "arbitrary")),
    )(xbc, xbc, dt3, dt3, dtt3, dtt3, bias_row, bias_col, alog_row, alog_col)


def _rope_norm(t, g, cos, sin_signed, lane):
    tn = _rms(t, g)
    swapped = jnp.where((lane & 32) == 0, pltpu.roll(tn, 96, 1), pltpu.roll(tn, 32, 1))
    return tn * cos + swapped * sin_signed


def _qkprep_kernel(q_ref, k_ref, cos_ref, sin_ref, qg_ref, kg_ref, qo_ref, ko_ref, *, q_scale):
    cos = cos_ref[...]
    sin = sin_ref[...]
    lane = lax.broadcasted_iota(jnp.int32, cos.shape, 1)
    for h in range(ATTN_HEADS):
        sl = slice(h * ATTN_HEAD_DIM, (h + 1) * ATTN_HEAD_DIM)
        r = _rope_norm(q_ref[:, sl].astype(F32), qg_ref[...], cos, sin, lane)
        qo_ref[:, sl] = (r * q_scale).astype(qo_ref.dtype)
    for h in range(ATTN_KV_HEADS):
        sl = slice(h * ATTN_HEAD_DIM, (h + 1) * ATTN_HEAD_DIM)
        r = _rope_norm(k_ref[:, sl].astype(F32), kg_ref[...], cos, sin, lane)
        ko_ref[:, sl] = r.astype(ko_ref.dtype)


def _qk_prep(proj, cos_t, sin_t, q_norm_g, k_norm_g, seq):
    n = proj.shape[0]
    tm = min(512, seq)
    qw = ATTN_HEADS * ATTN_HEAD_DIM
    kw = ATTN_KV_HEADS * ATTN_HEAD_DIM
    spt = seq // tm
    q_scale = ATTN_HEAD_DIM ** -0.5 * math.log2(math.e)
    return pl.pallas_call(
        functools.partial(_qkprep_kernel, q_scale=q_scale),
        grid=(n // tm,),
        in_specs=[
            pl.BlockSpec((tm, qw), lambda i: (i, Q_OFF // qw)),
            pl.BlockSpec((tm, kw), lambda i: (i, K_OFF // kw)),
            pl.BlockSpec((tm, ATTN_HEAD_DIM), lambda i: (i % spt, 0)),
            pl.BlockSpec((tm, ATTN_HEAD_DIM), lambda i: (i % spt, 0)),
            pl.BlockSpec((1, ATTN_HEAD_DIM), lambda i: (0, 0)),
            pl.BlockSpec((1, ATTN_HEAD_DIM), lambda i: (0, 0)),
        ],
        out_specs=[
            pl.BlockSpec((tm, qw), lambda i: (i, 0)),
            pl.BlockSpec((tm, kw), lambda i: (i, 0)),
        ],
        out_shape=[jax.ShapeDtypeStruct((n, qw), BF16), jax.ShapeDtypeStruct((n, kw), BF16)],
        compiler_params=_cparams(("arbitrary",)),
    )(proj, proj, cos_t, sin_t, q_norm_g, k_norm_g)


def _flash_kernel(q_ref, k_ref, v_ref, o_ref, *, tk):
    q = q_ref[0]
    tq = q.shape[0]
    nk = k_ref.shape[1] // tk

    def body(i, carry):
        m, l, acc = carry
        off = pl.multiple_of(i * tk, tk)
        k = k_ref[0, pl.ds(off, tk), :]
        v = v_ref[0, pl.ds(off, tk), :]
        s = lax.dot_general(q, k, (((1,), (1,)), ((), ())), preferred_element_type=F32)
        m_new = jnp.maximum(m, jnp.max(s, axis=-1, keepdims=True))
        alpha = jnp.exp2(m - m_new)
        p = jnp.exp2(s - m_new)
        l = alpha * l + jnp.sum(p, axis=-1, keepdims=True)
        acc = alpha * acc + jnp.dot(p.astype(BF16), v, preferred_element_type=F32)
        return m_new, l, acc

    init = (jnp.full((tq, 1), -jnp.inf, F32), jnp.zeros((tq, 1), F32),
            jnp.zeros((tq, ATTN_HEAD_DIM), F32))
    _, l, acc = lax.fori_loop(0, nk, body, init)
    o_ref[0] = (acc / l).astype(o_ref.dtype)


def _flash(q3, k3, proj3):
    b, s, _ = q3.shape
    tq = min(512, s)
    tk = min(512, s)
    grp = ATTN_HEADS // ATTN_KV_HEADS
    hd = ATTN_HEAD_DIM
    return pl.pallas_call(
        functools.partial(_flash_kernel, tk=tk),
        grid=(b, ATTN_HEADS, s // tq),
        in_specs=[
            pl.BlockSpec((1, tq, hd), lambda bi, h, qi: (bi, qi, h)),
            pl.BlockSpec((1, s, hd), lambda bi, h, qi: (bi, 0, h // grp)),
            pl.BlockSpec((1, s, hd), lambda bi, h, qi: (bi, 0, V_OFF // hd + h // grp)),
        ],
        out_specs=pl.BlockSpec((1, tq, hd), lambda bi, h, qi: (bi, qi, h)),
        out_shape=jax.ShapeDtypeStruct((b, s, ATTN_HEADS * hd), BF16),
        compiler_params=_cparams(("arbitrary", "arbitrary", "arbitrary")),
    )(q3, k3, proj3)


def _merge_kernel(yf_ref, yb_ref, xs_ref, z_ref, gate_ref, attn_ref, x_ref, dskip_ref, gssm_ref,
                  wbs_ref, wba_ref, wo_ref, x1_ref):
    d = x_ref.shape[1]
    xs = xs_ref[...].astype(F32)
    y = yf_ref[...].astype(F32) + yb_ref[...].astype(F32) + xs * dskip_ref[...]
    z = z_ref[...].astype(F32)
    y = _rms(y * (z * _sigmoid(z)), gssm_ref[...])
    br_ssm = jnp.dot(y.astype(BF16), wbs_ref[...], preferred_element_type=F32)
    br_attn = jnp.dot(attn_ref[...], wba_ref[...], preferred_element_type=F32)
    g_s = _sigmoid(gate_ref[:, :d].astype(F32))
    g_a = _sigmoid(gate_ref[:, d:].astype(F32))
    merged = (g_s * br_ssm + g_a * br_attn).astype(BF16)
    x1_ref[...] = x_ref[...] + jnp.dot(merged, wo_ref[...], preferred_element_type=F32)


def _merge(y_f, y_b, xbc, proj, attn, x2, dskip_row, g_ssm, w_br_ssm, w_br_attn, w_out):
    n, d = x2.shape
    tm = min(256, n)
    row = lambda i: (i, 0)
    const = lambda i: (0, 0)
    return pl.pallas_call(
        _merge_kernel,
        grid=(n // tm,),
        in_specs=[
            pl.BlockSpec((tm, d), row),
            pl.BlockSpec((tm, d), row),
            pl.BlockSpec((tm, d), row),
            pl.BlockSpec((tm, d), lambda i: (i, Z_OFF // d)),
            pl.BlockSpec((tm, 2 * d), lambda i: (i, GATE_OFF // (2 * d))),
            pl.BlockSpec((tm, d), row),
            pl.BlockSpec((tm, d), row),
            pl.BlockSpec((1, d), const),
            pl.BlockSpec((1, d), const),
            pl.BlockSpec((d, d), const),
            pl.BlockSpec((d, d), const),
            pl.BlockSpec((d, d), const),
        ],
        out_specs=pl.BlockSpec((tm, d), row),
        out_shape=jax.ShapeDtypeStruct((n, d), F32),
        compiler_params=_cparams(("arbitrary",)),
    )(y_f, y_b, xbc, proj, proj, attn, x2, dskip_row, g_ssm, w_br_ssm, w_br_attn, w_out)


def _router_kernel(x1_ref, g_ref, w_ref, b_ref, idx_ref, wt_ref):
    h = _rms(x1_ref[...], g_ref[...])
    logits = jnp.dot(h, w_ref[...], precision=lax.Precision.HIGHEST,
                     preferred_element_type=F32) + b_ref[...]
    lane = lax.broadcasted_iota(jnp.int32, logits.shape, 1)
    idx_out = jnp.zeros(logits.shape, jnp.int32)
    val_out = jnp.zeros(logits.shape, F32)
    vals = []
    for k in range(TOP_K):
        m = jnp.max(logits, axis=-1, keepdims=True)
        idx = jnp.min(jnp.where(logits == m, lane, LANES), axis=-1, keepdims=True)
        idx_out = jnp.where(lane == k, idx, idx_out)
        vals.append(m)
        logits = jnp.where(lane == idx, -jnp.inf, logits)
    es = [jnp.exp(v - vals[0]) for v in vals]
    tot = es[0] + es[1] + es[2] + es[3]
    for k in range(TOP_K):
        val_out = jnp.where(lane == k, es[k] / tot, val_out)
    idx_ref[...] = idx_out
    wt_ref[...] = val_out


def _router(x1, g_ffn, w_router_pad, b_router_pad):
    n, d = x1.shape
    tm = min(512, n)
    return pl.pallas_call(
        _router_kernel,
        grid=(n // tm,),
        in_specs=[
            pl.BlockSpec((tm, d), lambda i: (i, 0)),
            pl.BlockSpec((1, d), lambda i: (0, 0)),
            pl.BlockSpec((d, LANES), lambda i: (0, 0)),
            pl.BlockSpec((1, LANES), lambda i: (0, 0)),
        ],
        out_specs=[pl.BlockSpec((tm, LANES), lambda i: (i, 0))] * 2,
        out_shape=[jax.ShapeDtypeStruct((n, LANES), jnp.int32),
                   jax.ShapeDtypeStruct((n, LANES), F32)],
        compiler_params=_cparams(("arbitrary",)),
    )(x1, g_ffn, w_router_pad, b_router_pad)


def _expert_kernel(be_ref, nused_ref, tbl_ref, x1_hbm, g_ref, w1_ref, b1_ref, w2_ref, b2_ref,
                   yb_hbm, xbuf, obuf, gsem, ssem, *, n_tok):
    tb = xbuf.shape[0]
    de = w2_ref.shape[1]

    def gather_copy(r):
        tok = tbl_ref[0, 0, r] & (n_tok - 1)
        return pltpu.make_async_copy(x1_hbm.at[pl.ds(tok, 1)], xbuf.at[pl.ds(r, 1)], gsem)

    def scatter_copy(r):
        return pltpu.make_async_copy(obuf.at[pl.ds(r, 1)], yb_hbm.at[pl.ds(tbl_ref[0, 0, r], 1)], ssem)

    def for_rows(fn):
        def body(r, c):
            fn(r)
            return c
        lax.fori_loop(0, tb, body, 0)

    @pl.when(pl.program_id(0) == 0)
    def _():
        obuf[...] = jnp.zeros_like(obuf)
        spare = pltpu.make_async_copy(obuf, yb_hbm.at[pl.ds(TOP_K * n_tok, tb)], ssem)
        spare.start()
        spare.wait()

    @pl.when(pl.program_id(0) < nused_ref[0])
    def _():
        for_rows(lambda r: gather_copy(r).start())
        for_rows(lambda r: gather_copy(r).wait())
        h = _rms(xbuf[...], g_ref[...]).astype(BF16)
        hm = jnp.dot(h, w1_ref[0], preferred_element_type=F32) + b1_ref[0]
        gate = jnp.minimum(hm[:, :de], SWIGLU_LIMIT)
        up = jnp.clip(hm[:, de:], -SWIGLU_LIMIT, SWIGLU_LIMIT)
        act = gate * _sigmoid(SWIGLU_ALPHA * gate) * (up + 1.0)
        obuf[...] = jnp.dot(act.astype(BF16), w2_ref[0], preferred_element_type=F32) + b2_ref[0]
        for_rows(lambda r: scatter_copy(r).start())
        for_rows(lambda r: scatter_copy(r).wait())


def _experts(x1, g_ffn, tbl, block_e, n_used, w1, b1, w2, b2, tb):
    n, d = x1.shape
    nb = tbl.shape[0]
    e, _, f2 = w1.shape
    de = w2.shape[1]
    grid_spec = pltpu.PrefetchScalarGridSpec(
        num_scalar_prefetch=2,
        grid=(nb,),
        in_specs=[
            pl.BlockSpec((1, 1, tb), lambda j, be, nu: (j, 0, 0), memory_space=pltpu.SMEM),
            pl.BlockSpec(memory_space=pl.ANY),
            pl.BlockSpec((1, d), lambda j, be, nu: (0, 0)),
            pl.BlockSpec((1, d, f2), lambda j, be, nu: (be[j], 0, 0)),
            pl.BlockSpec((1, 1, f2), lambda j, be, nu: (be[j], 0, 0)),
            pl.BlockSpec((1, de, d), lambda j, be, nu: (be[j], 0, 0)),
            pl.BlockSpec((1, 1, d), lambda j, be, nu: (be[j], 0, 0)),
        ],
        out_specs=pl.BlockSpec(memory_space=pl.ANY),
        scratch_shapes=[
            pltpu.VMEM((tb, d), F32),
            pltpu.VMEM((tb, d), F32),
            pltpu.SemaphoreType.DMA(()),
            pltpu.SemaphoreType.DMA(()),
        ],
    )
    return pl.pallas_call(
        functools.partial(_expert_kernel, n_tok=n),
        grid_spec=grid_spec,
        out_shape=jax.ShapeDtypeStruct((TOP_K * n + tb, d), F32),
        compiler_params=_cparams(("arbitrary",)),
    )(block_e, n_used, tbl, x1, g_ffn, w1, b1, w2, b2)


def _combine_kernel(x1_ref, y0_ref, y1_ref, y2_ref, y3_ref, wt_ref, g_ref, o_ref):
    acc = x1_ref[...]
    for k, y_ref in enumerate((y0_ref, y1_ref, y2_ref, y3_ref)):
        acc = acc + wt_ref[:, k:k + 1] * y_ref[...]
    o_ref[...] = _rms(acc, g_ref[...])


def _combine(x1, yb, top_w, g_final):
    n, d = x1.shape
    tm = min(512, n)
    nt = n // tm
    y_spec = lambda k: pl.BlockSpec((tm, d), lambda i, k=k: (k * nt + i, 0))
    return pl.pallas_call(
        _combine_kernel,
        grid=(nt,),
        in_specs=[pl.BlockSpec((tm, d), lambda i: (i, 0)), y_spec(0), y_spec(1), y_spec(2), y_spec(3),
                  pl.BlockSpec((tm, LANES), lambda i: (i, 0)),
                  pl.BlockSpec((1, d), lambda i: (0, 0))],
        out_specs=pl.BlockSpec((tm, d), lambda i: (i, 0)),
        out_shape=jax.ShapeDtypeStruct((n, d), F32),
        compiler_params=_cparams(("arbitrary",)),
    )(x1, yb, yb, yb, yb, top_w, g_final)


def _routing_tables(top_idx, n, tb):
    n_assign = TOP_K * n
    nb = n_assign // tb + N_EXPERTS
    flat_e = top_idx[:, :TOP_K].T.reshape(-1)
    order = jnp.argsort(flat_e).astype(jnp.int32)
    counts = jnp.bincount(flat_e, length=N_EXPERTS).astype(jnp.int32)
    padded = (counts + tb - 1) // tb * tb
    start = jnp.cumsum(counts) - counts
    pad_end = jnp.cumsum(padded)
    pad_start = pad_end - padded
    n_used = (pad_end[-1] // tb).astype(jnp.int32)
    blk = jnp.arange(nb, dtype=jnp.int32)
    block_e = jnp.minimum(jnp.searchsorted(pad_end, blk * tb, side='right'), N_EXPERTS - 1).astype(jnp.int32)
    block_e = jnp.where(blk < n_used, block_e, block_e[jnp.maximum(n_used - 1, 0)])
    r = jnp.arange(nb * tb, dtype=jnp.int32)
    re = jnp.repeat(block_e, tb)
    off = r - pad_start[re]
    valid = (off < counts[re]) & (off >= 0) & (r < pad_end[-1])
    src = jnp.clip(start[re] + off, 0, n_assign - 1)
    tbl = jnp.where(valid, order[src], n_assign + r % tb).astype(jnp.int32)
    return tbl.reshape(nb, 1, tb), block_e, n_used.reshape(1)


def _rope_tables(seq):
    half = ATTN_HEAD_DIM // 2
    inv_freq = ROPE_THETA ** (-jnp.arange(0, half, 2, dtype=F32) / half)
    pos = jnp.arange(seq, dtype=jnp.int32)
    ang_r = (pos // GRID_W).astype(F32)[:, None] * inv_freq
    ang_c = (pos % GRID_W).astype(F32)[:, None] * inv_freq
    cos_t = jnp.concatenate([jnp.cos(ang_r)] * 2 + [jnp.cos(ang_c)] * 2, axis=-1)
    sin_t = jnp.concatenate([-jnp.sin(ang_r), jnp.sin(ang_r), -jnp.sin(ang_c), jnp.sin(ang_c)], axis=-1)
    return cos_t, sin_t


def _layer(x2, batch, seq, g_mix, w_in, conv_w, conv_b, dt_bias_f, dt_bias_b, a_log_f, a_log_b, d_skip,
           g_ssm, q_norm_g, k_norm_g, w_br_ssm, w_br_attn, w_out, g_ffn, w_router, b_router,
           w_mlp1, b_mlp1, w_mlp2, b_mlp2):
    n, d = x2.shape
    z_end = SSM_INNER
    xbc_end = z_end + CONV_CH
    dtf_end = xbc_end + SSM_HEADS
    dtb_end = dtf_end + SSM_HEADS
    q_end = dtb_end + ATTN_HEADS * ATTN_HEAD_DIM
    k_end = q_end + ATTN_KV_HEADS * ATTN_HEAD_DIM
    v_end = k_end + ATTN_KV_HEADS * ATTN_HEAD_DIM
    w_main = jnp.concatenate([w_in[:, :z_end], w_in[:, dtb_end:q_end], w_in[:, v_end:],
                              w_in[:, z_end:xbc_end], w_in[:, q_end:k_end], w_in[:, k_end:v_end]],
                             axis=1).astype(BF16)
    w_dt = jnp.pad(w_in[:, xbc_end:dtb_end], ((0, 0), (0, LANES - 2 * SSM_HEADS))).astype(BF16)

    proj, dt = _in_proj(x2, g_mix.reshape(1, d), w_main, w_dt)
    proj3 = proj.reshape(batch, seq, PROJ_COLS)

    xbc = _conv(proj3, conv_w, conv_b.reshape(1, CONV_CH))

    dt3 = dt.reshape(batch, seq, LANES)
    dtt3 = jnp.swapaxes(dt3[:, :, :2 * SSM_HEADS], 1, 2)
    bias = jnp.concatenate([dt_bias_f, dt_bias_b])
    alog = jnp.concatenate([a_log_f, a_log_b])
    pad_row = lambda v: jnp.pad(v, (0, LANES - 2 * SSM_HEADS)).reshape(1, LANES)
    y_f, y_b = _ssd(xbc, dt3, dtt3, pad_row(bias), bias.reshape(-1, 1), pad_row(alog), alog.reshape(-1, 1))

    cos_t, sin_t = _rope_tables(seq)
    q_rot, k_rot = _qk_prep(proj, cos_t, sin_t, q_norm_g.reshape(1, -1), k_norm_g.reshape(1, -1), seq)
    attn = _flash(q_rot.reshape(batch, seq, -1), k_rot.reshape(batch, seq, -1), proj3)

    x1 = _merge(y_f.reshape(n, -1), y_b.reshape(n, -1), xbc.reshape(n, CONV_CH), proj, attn.reshape(n, -1),
                x2, jnp.repeat(d_skip, SSM_HEAD_DIM).reshape(1, -1), g_ssm.reshape(1, -1),
                w_br_ssm.astype(BF16), w_br_attn.astype(BF16), w_out.astype(BF16))

    w_r = jnp.pad(w_router, ((0, 0), (0, LANES - N_EXPERTS)))
    b_r = jnp.pad(b_router, (0, LANES - N_EXPERTS), constant_values=-jnp.inf).reshape(1, LANES)
    g_ffn_row = g_ffn.reshape(1, d)
    top_idx, top_w = _router(x1, g_ffn_row, w_r, b_r)

    tb = min(512, n)
    tbl, block_e, n_used = _routing_tables(top_idx, n, tb)
    yb = _experts(x1, g_ffn_row, tbl, block_e, n_used, w_mlp1.astype(BF16), b_mlp1[:, None, :],
                  w_mlp2.astype(BF16), b_mlp2[:, None, :], tb)
    return x1, yb, top_w


def kernel(x, g_mix, w_in, conv_w, conv_b, dt_bias_f, dt_bias_b, a_log_f, a_log_b, d_skip, g_ssm, q_norm_g,
           k_norm_g, w_br_ssm, w_br_attn, w_out, g_ffn, w_router, b_router, w_mlp1, b_mlp1, w_mlp2, b_mlp2,
           g_final):
    batch, seq, d = x.shape
    depth = g_mix.shape[0]
    x2 = x.reshape(batch * seq, d)
    g_last = jnp.ones((1, d), F32)
    for l in range(depth):
        x1, yb, top_w = _layer(x2, batch, seq, g_mix[l], w_in[l], conv_w[l], conv_b[l], dt_bias_f[l],
                               dt_bias_b[l], a_log_f[l], a_log_b[l], d_skip[l], g_ssm[l], q_norm_g[l],
                               k_norm_g[l], w_br_ssm[l], w_br_attn[l], w_out[l], g_ffn[l], w_router[l],
                               b_router[l], w_mlp1[l], b_mlp1[l], w_mlp2[l], b_mlp2[l])
        if l == depth - 1:
            x2 = _combine(x1, yb, top_w, g_final.reshape(1, d))
        else:
            raise NotImplementedError("only the final layer fuses the closing rmsnorm")
    return x2.reshape(batch, seq, d)
```

```python
import functools
import math

import jax
import jax.numpy as jnp
from jax import lax
from jax.experimental import pallas as pl
from jax.experimental.pallas import tpu as pltpu

F32 = jnp.float32
BF16 = jnp.bfloat16

NORM_EPS = 1e-6
GRID_W = 64
SSM_HEADS = 16
SSM_HEAD_DIM = 64
SSM_INNER = SSM_HEADS * SSM_HEAD_DIM
SSM_GROUPS = 2
SSM_STATE = 128
SSM_CONV = 5
CONV_CH = SSM_INNER + 2 * SSM_GROUPS * SSM_STATE
ATTN_HEADS = 8
ATTN_KV_HEADS = 2
ATTN_HEAD_DIM = 128
ROPE_THETA = 10000.0
N_EXPERTS = 32
TOP_K = 4
SWIGLU_LIMIT = 7.0
SWIGLU_ALPHA = 1.702

LANES = 128
BF16_SUBLANES = 16
VMEM_LIMIT = 56 * 1024 * 1024

Z_OFF, Q_OFF, GATE_OFF, XBC_OFF = 0, 1024, 2048, 4096
K_OFF, V_OFF, PROJ_COLS = 5632, 5888, 6144

MOE_TILE = 512
MOE_BLOCK = 512
MOE_GRAN = BF16_SUBLANES


def _cparams(sem):
    return pltpu.CompilerParams(dimension_semantics=sem, vmem_limit_bytes=VMEM_LIMIT)


def _sigmoid(x):
    return 1.0 / (1.0 + jnp.exp(-x))


def _softplus(x):
    return jnp.maximum(x, 0.0) + jnp.log(1.0 + jnp.exp(-jnp.abs(x)))


def _rms(x, g):
    ms = jnp.mean(x * x, axis=-1, keepdims=True)
    return x * lax.rsqrt(ms + NORM_EPS) * g


def _inproj_kernel(x_ref, g_ref, w_ref, wdt_ref, o_ref, dt_ref, h_scr):
    @pl.when(pl.program_id(1) == 0)
    def _():
        hb = _rms(x_ref[...], g_ref[...]).astype(BF16)
        h_scr[...] = hb
        dt_ref[...] = jnp.dot(hb, wdt_ref[...], preferred_element_type=F32)

    o_ref[...] = jnp.dot(h_scr[...], w_ref[...], preferred_element_type=F32).astype(o_ref.dtype)


def _in_proj(x2, g_mix, w_main, w_dt):
    n, d = x2.shape
    tm = min(1024, n)
    tn = 1024
    return pl.pallas_call(
        _inproj_kernel,
        grid=(n // tm, PROJ_COLS // tn),
        in_specs=[
            pl.BlockSpec((tm, d), lambda i, j: (i, 0)),
            pl.BlockSpec((1, d), lambda i, j: (0, 0)),
            pl.BlockSpec((d, tn), lambda i, j: (0, j)),
            pl.BlockSpec((d, LANES), lambda i, j: (0, 0)),
        ],
        out_specs=[
            pl.BlockSpec((tm, tn), lambda i, j: (i, j)),
            pl.BlockSpec((tm, LANES), lambda i, j: (i, 0)),
        ],
        out_shape=[
            jax.ShapeDtypeStruct((n, PROJ_COLS), BF16),
            jax.ShapeDtypeStruct((n, LANES), F32),
        ],
        scratch_shapes=[pltpu.VMEM((tm, d), BF16)],
        compiler_params=_cparams(("arbitrary", "arbitrary")),
    )(x2, g_mix, w_main, w_dt)


def _conv_kernel(prev_ref, cur_ref, next_ref, w_ref, b_ref, o_ref, scr):
    s = pl.program_id(1)
    ts = cur_ref.shape[1]
    halo = BF16_SUBLANES
    pad = (SSM_CONV - 1) // 2
    prev = prev_ref[0].astype(F32)
    nxt = next_ref[0].astype(F32)
    scr[0:halo, :] = jnp.where(s == 0, 0.0, prev)
    scr[halo:halo + ts, :] = cur_ref[0].astype(F32)
    scr[halo + ts:2 * halo + ts, :] = jnp.where(s == pl.num_programs(1) - 1, 0.0, nxt)
    acc = jnp.zeros((ts, cur_ref.shape[2]), F32) + b_ref[...]
    for k in range(SSM_CONV):
        acc = acc + w_ref[k:k + 1, :] * scr[halo - pad + k:halo - pad + k + ts, :]
    o_ref[0] = (acc * _sigmoid(acc)).astype(o_ref.dtype)


def _conv(proj3, conv_w, conv_b):
    b, s, _ = proj3.shape
    ts = min(512, s)
    tc = 512
    halo = BF16_SUBLANES
    hb = ts // halo
    col0 = XBC_OFF // tc
    return pl.pallas_call(
        _conv_kernel,
        grid=(b, s // ts, CONV_CH // tc),
        in_specs=[
            pl.BlockSpec((1, halo, tc), lambda bi, si, ci: (bi, jnp.maximum(si * hb - 1, 0), col0 + ci)),
            pl.BlockSpec((1, ts, tc), lambda bi, si, ci: (bi, si, col0 + ci)),
            pl.BlockSpec((1, halo, tc),
                         lambda bi, si, ci: (bi, jnp.minimum((si + 1) * hb, s // halo - 1), col0 + ci)),
            pl.BlockSpec((SSM_CONV, tc), lambda bi, si, ci: (0, ci)),
            pl.BlockSpec((1, tc), lambda bi, si, ci: (0, ci)),
        ],
        out_specs=pl.BlockSpec((1, ts, tc), lambda bi, si, ci: (bi, si, ci)),
        out_shape=jax.ShapeDtypeStruct((b, s, CONV_CH), BF16),
        scratch_shapes=[pltpu.VMEM((ts + 2 * halo, tc), F32)],
        compiler_params=_cparams(("arbitrary", "arbitrary", "arbitrary")),
    )(proj3, proj3, proj3, conv_w, conv_b)


def _ssd_kernel(xf_ref, xb_ref, dtf_ref, dtb_ref, dttf_ref, dttb_ref, brow_ref, bcol_ref,
                arow_ref, acol_ref, yf_ref, yb_ref, st_ref):
    L = xf_ref.shape[1]
    hg = SSM_HEADS // SSM_GROUPS
    pairs = hg // 2
    hi = lax.Precision.HIGHEST

    @pl.when(pl.program_id(1) == 0)
    def _():
        st_ref[...] = jnp.zeros_like(st_ref)

    rows = lax.broadcasted_iota(jnp.int32, (L, L), 0)
    cols = lax.broadcasted_iota(jnp.int32, (L, L), 1)
    lower = rows >= cols
    upper = rows <= cols
    ltri = lower.astype(F32)
    utri = upper.astype(F32)
    lane = lax.broadcasted_iota(jnp.int32, (L, LANES), 1)
    first_half = lane < SSM_HEAD_DIM
    lane1 = lax.broadcasted_iota(jnp.int32, (1, LANES), 1)
    a_row = -jnp.exp(arow_ref[...])
    a_col = -jnp.exp(acol_ref[...])

    for d in range(2):
        x_ref, dt_ref, dtt_ref, y_ref = ((xf_ref, dtf_ref, dttf_ref, yf_ref) if d == 0
                                         else (xb_ref, dtb_ref, dttb_ref, yb_ref))
        dt = _softplus(dt_ref[0] + brow_ref[...])
        a = dt * a_row
        a_t = _softplus(dtt_ref[0] + bcol_ref[...]) * a_col
        if d == 0:
            cs_col = jnp.dot(ltri, a, precision=hi, preferred_element_type=F32)
            cs_row = jnp.dot(a_t, utri, precision=hi, preferred_element_type=F32)
            tot = cs_col[L - 1:L, :]
            mask = lower
        else:
            cs_col = jnp.dot(utri, a, precision=hi, preferred_element_type=F32)
            cs_row = jnp.dot(a_t, ltri, precision=hi, preferred_element_type=F32)
            tot = cs_col[0:1, :]
            mask = upper
        decay_out = jnp.exp(tot - cs_col)
        decay_in = jnp.exp(cs_col)
        chunk_decay = jnp.exp(tot)

        for g in range(SSM_GROUPS):
            boff = SSM_INNER + g * SSM_STATE
            coff = SSM_INNER + SSM_GROUPS * SSM_STATE + g * SSM_STATE
            bm = x_ref[0, :, boff:boff + SSM_STATE]
            cm = x_ref[0, :, coff:coff + SSM_STATE]
            cb = lax.dot_general(cm, bm, (((1,), (1,)), ((), ())), preferred_element_type=F32)
            st = st_ref[d, g]
            y_off = jnp.dot(cm, st.astype(BF16), preferred_element_type=F32)
            xs_out = []
            cd_parts = []
            for pr in range(pairs):
                h0 = d * SSM_HEADS + g * hg + 2 * pr
                xoff = (g * pairs + pr) * LANES

                def pair_cols(arr, h0=h0):
                    return jnp.where(first_half, arr[:, h0:h0 + 1], arr[:, h0 + 1:h0 + 2])

                xdt = x_ref[0, :, xoff:xoff + LANES].astype(F32) * pair_cols(dt)
                ms = []
                for hh in (h0, h0 + 1):
                    seg = cs_col[:, hh:hh + 1] - cs_row[hh:hh + 1, :]
                    ms.append((cb * jnp.exp(jnp.where(mask, seg, -jnp.inf))).astype(BF16))
                lhs = jnp.concatenate(ms, axis=1)
                xb16 = xdt.astype(BF16)
                zero = jnp.zeros_like(xb16)
                rhs = jnp.concatenate([jnp.where(first_half, xb16, zero),
                                       jnp.where(first_half, zero, xb16)], axis=0)
                y = jnp.dot(lhs, rhs, preferred_element_type=F32)
                y = y + y_off[:, pr * LANES:(pr + 1) * LANES] * pair_cols(decay_in)
                y_ref[0, :, xoff:xoff + LANES] = y.astype(y_ref.dtype)
                xs_out.append((xdt * pair_cols(decay_out)).astype(BF16))
                cd_parts.append(jnp.where(lane1 < SSM_HEAD_DIM, chunk_decay[:, h0:h0 + 1],
                                          chunk_decay[:, h0 + 1:h0 + 2]))
            xs_cat = jnp.concatenate(xs_out, axis=1)
            new_st = lax.dot_general(bm, xs_cat, (((0,), (0,)), ((), ())), preferred_element_type=F32)
            st_ref[d, g] = st * jnp.concatenate(cd_parts, axis=1) + new_st


def _ssd(xbc, dt3, dtt3, bias_row, bias_col, alog_row, alog_col):
    b, s, _ = xbc.shape
    L = min(128, s)
    nc = s // L
    hg = SSM_HEADS // SSM_GROUPS
    fwd = lambda bi, ci: (bi, ci, 0)
    bwd = lambda bi, ci: (bi, nc - 1 - ci, 0)
    fwd_t = lambda bi, ci: (bi, 0, ci)
    bwd_t = lambda bi, ci: (bi, 0, nc - 1 - ci)
    const = lambda bi, ci: (0, 0)
    return pl.pallas_call(
        _ssd_kernel,
        grid=(b, nc),
        in_specs=[
            pl.BlockSpec((1, L, CONV_CH), fwd),
            pl.BlockSpec((1, L, CONV_CH), bwd),
            pl.BlockSpec((1, L, LANES), fwd),
            pl.BlockSpec((1, L, LANES), bwd),
            pl.BlockSpec((1, 2 * SSM_HEADS, L), fwd_t),
            pl.BlockSpec((1, 2 * SSM_HEADS, L), bwd_t),
            pl.BlockSpec((1, LANES), const),
            pl.BlockSpec((2 * SSM_HEADS, 1), const),
            pl.BlockSpec((1, LANES), const),
            pl.BlockSpec((2 * SSM_HEADS, 1), const),
        ],
        out_specs=[
            pl.BlockSpec((1, L, SSM_INNER), fwd),
            pl.BlockSpec((1, L, SSM_INNER), bwd),
        ],
        out_shape=[jax.ShapeDtypeStruct((b, s, SSM_INNER), BF16)] * 2,
        scratch_shapes=[pltpu.VMEM((2, SSM_GROUPS, SSM_STATE, hg * SSM_HEAD_DIM), F32)],
        compiler_params=_cparams(("arbitrary", "arbitrary")),
    )(xbc, xbc, dt3, dt3, dtt3, dtt3, bias_row, bias_col, alog_row, alog_col)


def _rope_norm(t, g, cos, sin_signed, lane):
    tn = _rms(t, g)
    swapped = jnp.where((lane & 32) == 0, pltpu.roll(tn, 96, 1), pltpu.roll(tn, 32, 1))
    return tn * cos + swapped * sin_signed


def _qkprep_kernel(q_ref, k_ref, cos_ref, sin_ref, qg_ref, kg_ref, qo_ref, ko_ref, qsq_ref, ksq_ref, *,
                   q_scale):
    cos = cos_ref[...]
    sin = sin_ref[...]
    lane = lax.broadcasted_iota(jnp.int32, cos.shape, 1)

    def heads(src_ref, g_ref, dst_ref, sq_ref, n_heads, scale):
        sq = jnp.zeros(cos.shape, F32)
        for h in range(n_heads):
            sl = slice(h * ATTN_HEAD_DIM, (h + 1) * ATTN_HEAD_DIM)
            r = (_rope_norm(src_ref[:, sl].astype(F32), g_ref[...], cos, sin, lane) * scale).astype(dst_ref.dtype)
            dst_ref[:, sl] = r
            rf = r.astype(F32)
            sq = jnp.where(lane == h, jnp.sum(rf * rf, axis=-1, keepdims=True), sq)
        sq_ref[...] = sq

    heads(q_ref, qg_ref, qo_ref, qsq_ref, ATTN_HEADS, q_scale)
    heads(k_ref, kg_ref, ko_ref, ksq_ref, ATTN_KV_HEADS, 1.0)


def _qk_prep(proj, cos_t, sin_t, q_norm_g, k_norm_g, seq):
    n = proj.shape[0]
    tm = min(512, seq)
    qw = ATTN_HEADS * ATTN_HEAD_DIM
    kw = ATTN_KV_HEADS * ATTN_HEAD_DIM
    spt = seq // tm
    q_scale = ATTN_HEAD_DIM ** -0.5 * math.log2(math.e)
    return pl.pallas_call(
        functools.partial(_qkprep_kernel, q_scale=q_scale),
        grid=(n // tm,),
        in_specs=[
            pl.BlockSpec((tm, qw), lambda i: (i, Q_OFF // qw)),
            pl.BlockSpec((tm, kw), lambda i: (i, K_OFF // kw)),
            pl.BlockSpec((tm, ATTN_HEAD_DIM), lambda i: (i % spt, 0)),
            pl.BlockSpec((tm, ATTN_HEAD_DIM), lambda i: (i % spt, 0)),
            pl.BlockSpec((1, ATTN_HEAD_DIM), lambda i: (0, 0)),
            pl.BlockSpec((1, ATTN_HEAD_DIM), lambda i: (0, 0)),
        ],
        out_specs=[
            pl.BlockSpec((tm, qw), lambda i: (i, 0)),
            pl.BlockSpec((tm, kw), lambda i: (i, 0)),
            pl.BlockSpec((tm, ATTN_HEAD_DIM), lambda i: (i, 0)),
            pl.BlockSpec((tm, ATTN_HEAD_DIM), lambda i: (i, 0)),
        ],
        out_shape=[jax.ShapeDtypeStruct((n, qw), BF16), jax.ShapeDtypeStruct((n, kw), BF16),
                   jax.ShapeDtypeStruct((n, ATTN_HEAD_DIM), F32),
                   jax.ShapeDtypeStruct((n, ATTN_HEAD_DIM), F32)],
        compiler_params=_cparams(("arbitrary",)),
    )(proj, proj, cos_t, sin_t, q_norm_g, k_norm_g)


def _flash_kernel(small_ref, q_ref, k_ref, v_ref, o_ref, *, tk):
    q = q_ref[0]
    tq = q.shape[0]
    nk = k_ref.shape[1] // tk
    nt = (((1,), (1,)), ((), ()))
    bi, h, qi = pl.program_id(0), pl.program_id(1), pl.program_id(2)
    small = small_ref[(bi * pl.num_programs(1) + h) * pl.num_programs(2) + qi] != 0

    def kv(i):
        off = pl.multiple_of(i * tk, tk)
        return k_ref[0, pl.ds(off, tk), :], v_ref[0, pl.ds(off, tk), :]

    @pl.when(small)
    def _():
        def body(i, carry):
            l, acc = carry
            k, v = kv(i)
            p = jnp.exp2(lax.dot_general(q, k, nt, preferred_element_type=F32))
            for c in range(tk // LANES):
                l = l + p[:, c * LANES:(c + 1) * LANES]
            acc = acc + jnp.dot(p.astype(BF16), v, preferred_element_type=F32)
            return l, acc

        init = (jnp.zeros((tq, LANES), F32), jnp.zeros((tq, ATTN_HEAD_DIM), F32))
        l, acc = lax.fori_loop(0, nk, body, init)
        o_ref[0] = (acc / jnp.sum(l, axis=-1, keepdims=True)).astype(o_ref.dtype)

    @pl.when(jnp.logical_not(small))
    def _():
        def body(i, carry):
            m, l, acc = carry
            k, v = kv(i)
            s = lax.dot_general(q, k, nt, preferred_element_type=F32)
            m_new = jnp.maximum(m, jnp.max(s, axis=-1, keepdims=True))
            alpha = jnp.exp2(m - m_new)
            p = jnp.exp2(s - m_new)
            l = alpha * l + jnp.sum(p, axis=-1, keepdims=True)
            acc = alpha * acc + jnp.dot(p.astype(BF16), v, preferred_element_type=F32)
            return m_new, l, acc

        init = (jnp.full((tq, 1), -jnp.inf, F32), jnp.zeros((tq, 1), F32),
                jnp.zeros((tq, ATTN_HEAD_DIM), F32))
        _, l, acc = lax.fori_loop(0, nk, body, init)
        o_ref[0] = (acc / l).astype(o_ref.dtype)


SCORE_BOUND = 59.0


def _flash(q3, k3, proj3, qsq, ksq):
    b, s, _ = q3.shape
    tq = min(512, s)
    tk = min(2048, s)
    nq = s // tq
    grp = ATTN_HEADS // ATTN_KV_HEADS
    hd = ATTN_HEAD_DIM
    qmax = jnp.max(qsq.reshape(b, nq, tq, hd)[..., :ATTN_HEADS], axis=2)
    kmax = jnp.max(ksq.reshape(b, s, hd)[..., :ATTN_KV_HEADS], axis=1)
    bound_sq = jnp.swapaxes(qmax, 1, 2) * jnp.repeat(kmax, grp, axis=1)[:, :, None]
    small = (bound_sq <= SCORE_BOUND * SCORE_BOUND).astype(jnp.int32).reshape(-1)
    grid_spec = pltpu.PrefetchScalarGridSpec(
        num_scalar_prefetch=1,
        grid=(b, ATTN_HEADS, nq),
        in_specs=[
            pl.BlockSpec((1, tq, hd), lambda bi, h, qi, sm: (bi, qi, h)),
            pl.BlockSpec((1, s, hd), lambda bi, h, qi, sm: (bi, 0, h // grp)),
            pl.BlockSpec((1, s, hd), lambda bi, h, qi, sm: (bi, 0, V_OFF // hd + h // grp)),
        ],
        out_specs=pl.BlockSpec((1, tq, hd), lambda bi, h, qi, sm: (bi, qi, h)),
    )
    return pl.pallas_call(
        functools.partial(_flash_kernel, tk=tk),
        grid_spec=grid_spec,
        out_shape=jax.ShapeDtypeStruct((b, s, ATTN_HEADS * hd), BF16),
        compiler_params=_cparams(("arbitrary", "arbitrary", "arbitrary")),
    )(small, q3, k3, proj3)


def _merge_kernel(yf_ref, yb_ref, xs_ref, z_ref, gate_ref, attn_ref, x_ref, dskip_ref, gssm_ref,
                  wbs_ref, wba_ref, wo_ref, x1_ref):
    d = x_ref.shape[1]
    xs = xs_ref[...].astype(F32)
    y = yf_ref[...].astype(F32) + yb_ref[...].astype(F32) + xs * dskip_ref[...]
    z = z_ref[...].astype(F32)
    y = _rms(y * (z * _sigmoid(z)), gssm_ref[...])
    br_ssm = jnp.dot(y.astype(BF16), wbs_ref[...], preferred_element_type=F32)
    br_attn = jnp.dot(attn_ref[...], wba_ref[...], preferred_element_type=F32)
    g_s = _sigmoid(gate_ref[:, :d].astype(F32))
    g_a = _sigmoid(gate_ref[:, d:].astype(F32))
    merged = (g_s * br_ssm + g_a * br_attn).astype(BF16)
    x1_ref[...] = x_ref[...] + jnp.dot(merged, wo_ref[...], preferred_element_type=F32)


def _merge(y_f, y_b, xbc, proj, attn, x2, dskip_row, g_ssm, w_br_ssm, w_br_attn, w_out):
    n, d = x2.shape
    tm = min(256, n)
    row = lambda i: (i, 0)
    const = lambda i: (0, 0)
    return pl.pallas_call(
        _merge_kernel,
        grid=(n // tm,),
        in_specs=[
            pl.BlockSpec((tm, d), row),
            pl.BlockSpec((tm, d), row),
            pl.BlockSpec((tm, d), row),
            pl.BlockSpec((tm, d), lambda i: (i, Z_OFF // d)),
            pl.BlockSpec((tm, 2 * d), lambda i: (i, GATE_OFF // (2 * d))),
            pl.BlockSpec((tm, d), row),
            pl.BlockSpec((tm, d), row),
            pl.BlockSpec((1, d), const),
            pl.BlockSpec((1, d), const),
            pl.BlockSpec((d, d), const),
            pl.BlockSpec((d, d), const),
            pl.BlockSpec((d, d), const),
        ],
        out_specs=pl.BlockSpec((tm, d), row),
        out_shape=jax.ShapeDtypeStruct((n, d), F32),
        compiler_params=_cparams(("arbitrary",)),
    )(y_f, y_b, xbc, proj, proj, attn, x2, dskip_row, g_ssm, w_br_ssm, w_br_attn, w_out)


def _router_kernel(x1_ref, g_ref, w_ref, b_ref, idx_ref, wt_ref, cnt_ref):
    h = _rms(x1_ref[...], g_ref[...])
    logits = jnp.dot(h, w_ref[...], precision=lax.Precision.HIGHEST,
                     preferred_element_type=F32) + b_ref[...]
    lane = lax.broadcasted_iota(jnp.int32, logits.shape, 1)
    idx_out = jnp.zeros(logits.shape, jnp.int32)
    val_out = jnp.zeros(logits.shape, F32)
    chosen = jnp.zeros(logits.shape, F32)
    vals = []
    for k in range(TOP_K):
        m = jnp.max(logits, axis=-1, keepdims=True)
        idx = jnp.min(jnp.where(logits == m, lane, LANES), axis=-1, keepdims=True)
        idx_out = jnp.where(lane == k, idx, idx_out)
        vals.append(m)
        hit = lane == idx
        chosen = jnp.where(hit, 1.0, chosen)
        logits = jnp.where(hit, -jnp.inf, logits)
    es = [jnp.exp(v - vals[0]) for v in vals]
    tot = es[0] + es[1] + es[2] + es[3]
    for k in range(TOP_K):
        val_out = jnp.where(lane == k, es[k] / tot, val_out)
    idx_ref[...] = idx_out
    wt_ref[...] = val_out
    cnt_ref[0] = jnp.broadcast_to(jnp.sum(chosen, axis=0, keepdims=True), cnt_ref.shape[1:])


def _router(x1, g_ffn, w_router_pad, b_router_pad):
    n, d = x1.shape
    tm = min(MOE_TILE, n)
    return pl.pallas_call(
        _router_kernel,
        grid=(n // tm,),
        in_specs=[
            pl.BlockSpec((tm, d), lambda i: (i, 0)),
            pl.BlockSpec((1, d), lambda i: (0, 0)),
            pl.BlockSpec((d, LANES), lambda i: (0, 0)),
            pl.BlockSpec((1, LANES), lambda i: (0, 0)),
        ],
        out_specs=[pl.BlockSpec((tm, LANES), lambda i: (i, 0)),
                   pl.BlockSpec((tm, LANES), lambda i: (i, 0)),
                   pl.BlockSpec((1, 8, LANES), lambda i: (i, 0, 0))],
        out_shape=[jax.ShapeDtypeStruct((n, LANES), jnp.int32),
                   jax.ShapeDtypeStruct((n, LANES), F32),
                   jax.ShapeDtypeStruct((n // tm, 8, LANES), F32)],
        compiler_params=_cparams(("arbitrary",)),
    )(x1, g_ffn, w_router_pad, b_router_pad)


def _routing_plan(cnt, n, tm, tb):
    nt = n // tm
    gran = MOE_GRAN
    rc = (cnt + gran - 1) // gran * gran
    region = jnp.sum(rc, axis=0)
    region = (region + tb - 1) // tb * tb
    pad_end = jnp.cumsum(region)
    pad_start = pad_end - region
    seg_start = pad_start[None, :] + jnp.cumsum(rc, axis=0) - rc
    stage_off = jnp.cumsum(rc, axis=1) - rc
    n_used = pad_end[-1] // tb
    nb = (TOP_K * n + nt * N_EXPERTS * (gran - 1) + N_EXPERTS * (tb - 1) + tb - 1) // tb
    blk = jnp.arange(nb, dtype=jnp.int32)
    block_e = jnp.sum((pad_end[None, :] <= (jnp.minimum(blk, n_used - 1) * tb)[:, None]).astype(jnp.int32), axis=1)
    block_e = jnp.minimum(block_e, N_EXPERTS - 1)
    flat = lambda a: a.reshape(-1).astype(jnp.int32)
    return dict(seg_start=flat(seg_start), stage_off=flat(stage_off), ngran=flat(rc // gran),
                tot_gran=flat(jnp.sum(rc, axis=1) // gran), block_e=flat(block_e),
                n_used=flat(n_used), nb=nb, stage_off_f=stage_off.astype(F32))


def _stage_rows(tm):
    rows = TOP_K * tm + N_EXPERTS * (MOE_GRAN - 1)
    return (rows + LANES - 1) // LANES * LANES


def _segment_copies(t, soff_ref, sstart_ref, ngran_ref, make_copy):
    def per_expert(e, c):
        so = soff_ref[t * N_EXPERTS + e]
        ss = sstart_ref[t * N_EXPERTS + e]

        def per_granule(g, c2):
            off = g * MOE_GRAN
            make_copy(pl.multiple_of(so + off, MOE_GRAN), pl.multiple_of(ss + off, MOE_GRAN)).start()
            return c2

        lax.fori_loop(0, ngran_ref[t * N_EXPERTS + e], per_granule, 0)
        return c

    lax.fori_loop(0, N_EXPERTS, per_expert, 0)


def _wait_granules(count, make_copy):
    def body(g, c):
        make_copy(0, 0).wait()
        return c
    lax.fori_loop(0, count, body, 0)


def _dispatch_kernel(soff_ref, sstart_ref, ngran_ref, totg_ref, x1_ref, g_ref, idxt_ref, soffc_ref,
                     xs_in_hbm, xs_hbm, stage, sem):
    del xs_in_hbm
    t = pl.program_id(0)
    tm = x1_ref.shape[0]
    rows = stage.shape[0]
    h = _rms(x1_ref[...], g_ref[...]).astype(BF16)
    idxt = idxt_ref[...]
    expert = lax.broadcasted_iota(jnp.int32, (LANES, tm), 0)
    hots = [idxt[k:k + 1, :] == expert for k in range(TOP_K)]
    multi = jnp.zeros((LANES, tm), F32)
    for hot in hots:
        multi = jnp.where(hot, 1.0, multi)
    earlier = (lax.broadcasted_iota(jnp.int32, (tm, tm), 0)
               < lax.broadcasted_iota(jnp.int32, (tm, tm), 1)).astype(BF16)
    rank = jnp.dot(multi.astype(BF16), earlier, preferred_element_type=F32)
    pos = rank + soffc_ref[0]
    row_id = lax.broadcasted_iota(jnp.int32, (rows, tm), 0).astype(F32)
    perm = jnp.zeros((rows, tm), F32)
    for hot in hots:
        srow = jnp.sum(jnp.where(hot, pos, 0.0), axis=0, keepdims=True)
        perm = jnp.where(row_id == srow, 1.0, perm)
    stage[...] = jnp.dot(perm.astype(BF16), h, preferred_element_type=F32).astype(stage.dtype)

    def copy(stage_row, sorted_row):
        return pltpu.make_async_copy(stage.at[pl.ds(stage_row, MOE_GRAN)],
                                     xs_hbm.at[pl.ds(sorted_row, MOE_GRAN)], sem)

    _segment_copies(t, soff_ref, sstart_ref, ngran_ref, copy)
    _wait_granules(totg_ref[t], copy)


def _dispatch(x1, g_ffn, idx_t, plan, tm):
    n, d = x1.shape
    nt = n // tm
    rows = _stage_rows(tm)
    xs0 = jnp.zeros((plan["nb"] * MOE_BLOCK, d), BF16)
    soff_col = plan["stage_off_f"].reshape(nt, N_EXPERTS, 1)
    soff_col = jnp.pad(soff_col, ((0, 0), (0, LANES - N_EXPERTS), (0, 0)))
    grid_spec = pltpu.PrefetchScalarGridSpec(
        num_scalar_prefetch=4,
        grid=(nt,),
        in_specs=[
            pl.BlockSpec((tm, d), lambda t, *_: (t, 0)),
            pl.BlockSpec((1, d), lambda t, *_: (0, 0)),
            pl.BlockSpec((8, tm), lambda t, *_: (0, t)),
            pl.BlockSpec((1, LANES, 1), lambda t, *_: (t, 0, 0)),
            pl.BlockSpec(memory_space=pl.ANY),
        ],
        out_specs=pl.BlockSpec(memory_space=pl.ANY),
        scratch_shapes=[pltpu.VMEM((rows, d), BF16), pltpu.SemaphoreType.DMA(())],
    )
    return pl.pallas_call(
        _dispatch_kernel,
        grid_spec=grid_spec,
        out_shape=jax.ShapeDtypeStruct(xs0.shape, BF16),
        input_output_aliases={8: 0},
        compiler_params=_cparams(("arbitrary",)),
    )(plan["stage_off"], plan["seg_start"], plan["ngran"], plan["tot_gran"], x1, g_ffn, idx_t, soff_col, xs0)


def _expert_kernel(be_ref, nused_ref, x_ref, w1_ref, b1_ref, w2_ref, b2_ref, o_ref):
    de = w2_ref.shape[1]
    live = pl.program_id(0) < nused_ref[0]

    @pl.when(live)
    def _():
        hm = jnp.dot(x_ref[...], w1_ref[0], preferred_element_type=F32) + b1_ref[0]
        gate = jnp.minimum(hm[:, :de], SWIGLU_LIMIT)
        up = jnp.clip(hm[:, de:], -SWIGLU_LIMIT, SWIGLU_LIMIT)
        act = gate * _sigmoid(SWIGLU_ALPHA * gate) * (up + 1.0)
        y = jnp.dot(act.astype(BF16), w2_ref[0], preferred_element_type=F32) + b2_ref[0]
        o_ref[...] = y.astype(o_ref.dtype)

    @pl.when(jnp.logical_not(live))
    def _():
        o_ref[...] = jnp.zeros_like(o_ref)


def _experts(xs, plan, w1, b1, w2, b2):
    d = xs.shape[1]
    tb = MOE_BLOCK
    f2 = w1.shape[2]
    de = w2.shape[1]
    last = lambda j, nu: jnp.minimum(j, nu[0] - 1)
    grid_spec = pltpu.PrefetchScalarGridSpec(
        num_scalar_prefetch=2,
        grid=(plan["nb"],),
        in_specs=[
            pl.BlockSpec((tb, d), lambda j, be, nu: (last(j, nu), 0)),
            pl.BlockSpec((1, d, f2), lambda j, be, nu: (be[j], 0, 0)),
            pl.BlockSpec((1, 1, f2), lambda j, be, nu: (be[j], 0, 0)),
            pl.BlockSpec((1, de, d), lambda j, be, nu: (be[j], 0, 0)),
            pl.BlockSpec((1, 1, d), lambda j, be, nu: (be[j], 0, 0)),
        ],
        out_specs=pl.BlockSpec((tb, d), lambda j, be, nu: (j, 0)),
    )
    return pl.pallas_call(
        _expert_kernel,
        grid_spec=grid_spec,
        out_shape=jax.ShapeDtypeStruct(xs.shape, BF16),
        compiler_params=_cparams(("arbitrary",)),
    )(plan["block_e"], plan["n_used"], xs, w1, b1, w2, b2)


def _combine_kernel(soff_ref, sstart_ref, ngran_ref, totg_ref, x1_ref, idx_ref, wt_ref, soffr_ref, g_ref,
                    yb_hbm, o_ref, stage, sem):
    t = pl.program_id(0)
    tm = x1_ref.shape[0]
    rows = stage.shape[0]

    @pl.when(t == 0)
    def _():
        stage[...] = jnp.zeros_like(stage)

    def copy(stage_row, sorted_row):
        return pltpu.make_async_copy(yb_hbm.at[pl.ds(sorted_row, MOE_GRAN)],
                                     stage.at[pl.ds(stage_row, MOE_GRAN)], sem)

    _segment_copies(t, soff_ref, sstart_ref, ngran_ref, copy)

    idx = idx_ref[...]
    wt = wt_ref[...]
    expert = lax.broadcasted_iota(jnp.int32, (tm, LANES), 1)
    hots = [idx[:, k:k + 1] == expert for k in range(TOP_K)]
    multi = jnp.zeros((tm, LANES), F32)
    for hot in hots:
        multi = jnp.where(hot, 1.0, multi)
    earlier = (lax.broadcasted_iota(jnp.int32, (tm, tm), 0)
               > lax.broadcasted_iota(jnp.int32, (tm, tm), 1)).astype(BF16)
    rank = jnp.dot(earlier, multi.astype(BF16), preferred_element_type=F32)
    pos = rank + soffr_ref[0]
    row_id = lax.broadcasted_iota(jnp.int32, (tm, rows), 1).astype(F32)
    unsort = jnp.zeros((tm, rows), F32)
    for k, hot in enumerate(hots):
        srow = jnp.sum(jnp.where(hot, pos, 0.0), axis=-1, keepdims=True)
        unsort = jnp.where(row_id == srow, wt[:, k:k + 1], unsort)

    _wait_granules(totg_ref[t], copy)
    y = jnp.dot(unsort.astype(BF16), stage[...], preferred_element_type=F32)
    o_ref[...] = _rms(x1_ref[...] + y, g_ref[...])


def _combine(x1, yb, top_idx, top_w, plan, g_final, tm):
    n, d = x1.shape
    nt = n // tm
    rows = _stage_rows(tm)
    soff_row = jnp.pad(plan["stage_off_f"], ((0, 0), (0, LANES - N_EXPERTS))).reshape(nt, 1, LANES)
    grid_spec = pltpu.PrefetchScalarGridSpec(
        num_scalar_prefetch=4,
        grid=(nt,),
        in_specs=[
            pl.BlockSpec((tm, d), lambda t, *_: (t, 0)),
            pl.BlockSpec((tm, LANES), lambda t, *_: (t, 0)),
            pl.BlockSpec((tm, LANES), lambda t, *_: (t, 0)),
            pl.BlockSpec((1, 1, LANES), lambda t, *_: (t, 0, 0)),
            pl.BlockSpec((1, d), lambda t, *_: (0, 0)),
            pl.BlockSpec(memory_space=pl.ANY),
        ],
        out_specs=pl.BlockSpec((tm, d), lambda t, *_: (t, 0)),
        scratch_shapes=[pltpu.VMEM((rows, d), BF16), pltpu.SemaphoreType.DMA(())],
    )
    return pl.pallas_call(
        _combine_kernel,
        grid_spec=grid_spec,
        out_shape=jax.ShapeDtypeStruct((n, d), F32),
        compiler_params=_cparams(("arbitrary",)),
    )(plan["stage_off"], plan["seg_start"], plan["ngran"], plan["tot_gran"], x1, top_idx, top_w, soff_row,
      g_final, yb)


def _rope_tables(seq):
    half = ATTN_HEAD_DIM // 2
    inv_freq = ROPE_THETA ** (-jnp.arange(0, half, 2, dtype=F32) / half)
    pos = jnp.arange(seq, dtype=jnp.int32)
    ang_r = (pos // GRID_W).astype(F32)[:, None] * inv_freq
    ang_c = (pos % GRID_W).astype(F32)[:, None] * inv_freq
    cos_t = jnp.concatenate([jnp.cos(ang_r)] * 2 + [jnp.cos(ang_c)] * 2, axis=-1)
    sin_t = jnp.concatenate([-jnp.sin(ang_r), jnp.sin(ang_r), -jnp.sin(ang_c), jnp.sin(ang_c)], axis=-1)
    return cos_t, sin_t


def _token_mixer(x2, batch, seq, g_mix, w_in, conv_w, conv_b, dt_bias_f, dt_bias_b, a_log_f, a_log_b, d_skip,
                 g_ssm, q_norm_g, k_norm_g, w_br_ssm, w_br_attn, w_out):
    n, d = x2.shape
    z_end = SSM_INNER
    xbc_end = z_end + CONV_CH
    dtf_end = xbc_end + SSM_HEADS
    dtb_end = dtf_end + SSM_HEADS
    q_end = dtb_end + ATTN_HEADS * ATTN_HEAD_DIM
    k_end = q_end + ATTN_KV_HEADS * ATTN_HEAD_DIM
    v_end = k_end + ATTN_KV_HEADS * ATTN_HEAD_DIM
    w_main = jnp.concatenate([w_in[:, :z_end], w_in[:, dtb_end:q_end], w_in[:, v_end:],
                              w_in[:, z_end:xbc_end], w_in[:, q_end:k_end], w_in[:, k_end:v_end]],
                             axis=1).astype(BF16)
    w_dt = jnp.pad(w_in[:, xbc_end:dtb_end], ((0, 0), (0, LANES - 2 * SSM_HEADS))).astype(BF16)

    proj, dt = _in_proj(x2, g_mix.reshape(1, d), w_main, w_dt)
    proj3 = proj.reshape(batch, seq, PROJ_COLS)

    xbc = _conv(proj3, conv_w, conv_b.reshape(1, CONV_CH))

    dt3 = dt.reshape(batch, seq, LANES)
    dtt3 = jnp.swapaxes(dt3[:, :, :2 * SSM_HEADS], 1, 2)
    bias = jnp.concatenate([dt_bias_f, dt_bias_b])
    alog = jnp.concatenate([a_log_f, a_log_b])
    pad_row = lambda v: jnp.pad(v, (0, LANES - 2 * SSM_HEADS)).reshape(1, LANES)
    y_f, y_b = _ssd(xbc, dt3, dtt3, pad_row(bias), bias.reshape(-1, 1), pad_row(alog), alog.reshape(-1, 1))

    cos_t, sin_t = _rope_tables(seq)
    q_rot, k_rot, qsq, ksq = _qk_prep(proj, cos_t, sin_t, q_norm_g.reshape(1, -1), k_norm_g.reshape(1, -1), seq)
    attn = _flash(q_rot.reshape(batch, seq, -1), k_rot.reshape(batch, seq, -1), proj3, qsq, ksq)

    return _merge(y_f.reshape(n, -1), y_b.reshape(n, -1), xbc.reshape(n, CONV_CH), proj, attn.reshape(n, -1),
                  x2, jnp.repeat(d_skip, SSM_HEAD_DIM).reshape(1, -1), g_ssm.reshape(1, -1),
                  w_br_ssm.astype(BF16), w_br_attn.astype(BF16), w_out.astype(BF16))


def _moe_and_final_norm(x1, g_ffn, w_router, b_router, w_mlp1, b_mlp1, w_mlp2, b_mlp2, g_final):
    n, d = x1.shape
    tm = min(MOE_TILE, n)
    w_r = jnp.pad(w_router, ((0, 0), (0, LANES - N_EXPERTS)))
    b_r = jnp.pad(b_router, (0, LANES - N_EXPERTS), constant_values=-jnp.inf).reshape(1, LANES)
    g_ffn_row = g_ffn.reshape(1, d)
    top_idx, top_w, cnt = _router(x1, g_ffn_row, w_r, b_r)
    plan = _routing_plan(cnt[:, 0, :N_EXPERTS].astype(jnp.int32), n, tm, MOE_BLOCK)
    idx_t = jnp.transpose(top_idx[:, :8])
    xs = _dispatch(x1, g_ffn_row, idx_t, plan, tm)
    yb = _experts(xs, plan, w_mlp1.astype(BF16), b_mlp1[:, None, :], w_mlp2.astype(BF16), b_mlp2[:, None, :])
    return _combine(x1, yb, top_idx, top_w, plan, g_final.reshape(1, d), tm)


def kernel(x, g_mix, w_in, conv_w, conv_b, dt_bias_f, dt_bias_b, a_log_f, a_log_b, d_skip, g_ssm, q_norm_g,
           k_norm_g, w_br_ssm, w_br_attn, w_out, g_ffn, w_router, b_router, w_mlp1, b_mlp1, w_mlp2, b_mlp2,
           g_final):
    batch, seq, d = x.shape
    assert g_mix.shape[0] == 1, "single-layer model: the final rmsnorm is fused into the MoE combine"
    x2 = x.reshape(batch * seq, d)
    x1 = _token_mixer(x2, batch, seq, g_mix[0], w_in[0], conv_w[0], conv_b[0], dt_bias_f[0], dt_bias_b[0],
                      a_log_f[0], a_log_b[0], d_skip[0], g_ssm[0], q_norm_g[0], k_norm_g[0], w_br_ssm[0],
                      w_br_attn[0], w_out[0])
    out = _moe_and_final_norm(x1, g_ffn[0], w_router[0], b_router[0], w_mlp1[0], b_mlp1[0], w_mlp2[0],
                              b_mlp2[0], g_final)
    return out.reshape(batch, seq, d)
```

```python
import functools
import math

import jax
import jax.numpy as jnp
from jax import lax
from jax.experimental import pallas as pl
from jax.experimental.pallas import tpu as pltpu

F32 = jnp.float32
BF16 = jnp.bfloat16

NORM_EPS = 1e-6
GRID_W = 64
SSM_HEADS = 16
SSM_HEAD_DIM = 64
SSM_INNER = SSM_HEADS * SSM_HEAD_DIM
SSM_GROUPS = 2
SSM_STATE = 128
SSM_CONV = 5
CONV_CH = SSM_INNER + 2 * SSM_GROUPS * SSM_STATE
ATTN_HEADS = 8
ATTN_KV_HEADS = 2
ATTN_HEAD_DIM = 128
ROPE_THETA = 10000.0
N_EXPERTS = 32
TOP_K = 4
SWIGLU_LIMIT = 7.0
SWIGLU_ALPHA = 1.702

LANES = 128
BF16_SUBLANES = 16
VMEM_LIMIT = 56 * 1024 * 1024

Z_OFF, Q_OFF, GATE_OFF, XBC_OFF = 0, 1024, 2048, 4096
K_OFF, V_OFF, PROJ_COLS = 5632, 5888, 6144

MOE_TILE = 512
MOE_BLOCK = 512
MOE_GRAN = BF16_SUBLANES


def _cparams(sem):
    return pltpu.CompilerParams(dimension_semantics=sem, vmem_limit_bytes=VMEM_LIMIT)


def _sigmoid(x):
    return 1.0 / (1.0 + jnp.exp(-x))


def _softplus(x):
    return jnp.maximum(x, 0.0) + jnp.log(1.0 + jnp.exp(-jnp.abs(x)))


def _rms(x, g):
    ms = jnp.mean(x * x, axis=-1, keepdims=True)
    return x * lax.rsqrt(ms + NORM_EPS) * g


def _inproj_kernel(x_ref, g_ref, w_ref, wdt_ref, o_ref, dt_ref, h_scr):
    @pl.when(pl.program_id(1) == 0)
    def _():
        hb = _rms(x_ref[...], g_ref[...]).astype(BF16)
        h_scr[...] = hb
        dt_ref[...] = jnp.dot(hb, wdt_ref[...], preferred_element_type=F32)

    o_ref[...] = jnp.dot(h_scr[...], w_ref[...], preferred_element_type=F32).astype(o_ref.dtype)


def _in_proj(x2, g_mix, w_main, w_dt):
    n, d = x2.shape
    tm = min(1024, n)
    tn = 1024
    return pl.pallas_call(
        _inproj_kernel,
        grid=(n // tm, PROJ_COLS // tn),
        in_specs=[
            pl.BlockSpec((tm, d), lambda i, j: (i, 0)),
            pl.BlockSpec((1, d), lambda i, j: (0, 0)),
            pl.BlockSpec((d, tn), lambda i, j: (0, j)),
            pl.BlockSpec((d, LANES), lambda i, j: (0, 0)),
        ],
        out_specs=[
            pl.BlockSpec((tm, tn), lambda i, j: (i, j)),
            pl.BlockSpec((tm, LANES), lambda i, j: (i, 0)),
        ],
        out_shape=[
            jax.ShapeDtypeStruct((n, PROJ_COLS), BF16),
            jax.ShapeDtypeStruct((n, LANES), F32),
        ],
        scratch_shapes=[pltpu.VMEM((tm, d), BF16)],
        compiler_params=_cparams(("arbitrary", "arbitrary")),
    )(x2, g_mix, w_main, w_dt)


def _conv_kernel(prev_ref, cur_ref, next_ref, w_ref, b_ref, o_ref, scr):
    s = pl.program_id(1)
    ts = cur_ref.shape[1]
    halo = BF16_SUBLANES
    pad = (SSM_CONV - 1) // 2
    prev = prev_ref[0].astype(F32)
    nxt = next_ref[0].astype(F32)
    scr[0:halo, :] = jnp.where(s == 0, 0.0, prev)
    scr[halo:halo + ts, :] = cur_ref[0].astype(F32)
    scr[halo + ts:2 * halo + ts, :] = jnp.where(s == pl.num_programs(1) - 1, 0.0, nxt)
    acc = jnp.zeros((ts, cur_ref.shape[2]), F32) + b_ref[...]
    for k in range(SSM_CONV):
        acc = acc + w_ref[k:k + 1, :] * scr[halo - pad + k:halo - pad + k + ts, :]
    o_ref[0] = (acc * _sigmoid(acc)).astype(o_ref.dtype)


def _conv(proj3, conv_w, conv_b):
    b, s, _ = proj3.shape
    ts = min(512, s)
    tc = 512
    halo = BF16_SUBLANES
    hb = ts // halo
    col0 = XBC_OFF // tc
    return pl.pallas_call(
        _conv_kernel,
        grid=(b, s // ts, CONV_CH // tc),
        in_specs=[
            pl.BlockSpec((1, halo, tc), lambda bi, si, ci: (bi, jnp.maximum(si * hb - 1, 0), col0 + ci)),
            pl.BlockSpec((1, ts, tc), lambda bi, si, ci: (bi, si, col0 + ci)),
            pl.BlockSpec((1, halo, tc),
                         lambda bi, si, ci: (bi, jnp.minimum((si + 1) * hb, s // halo - 1), col0 + ci)),
            pl.BlockSpec((SSM_CONV, tc), lambda bi, si, ci: (0, ci)),
            pl.BlockSpec((1, tc), lambda bi, si, ci: (0, ci)),
        ],
        out_specs=pl.BlockSpec((1, ts, tc), lambda bi, si, ci: (bi, si, ci)),
        out_shape=jax.ShapeDtypeStruct((b, s, CONV_CH), BF16),
        scratch_shapes=[pltpu.VMEM((ts + 2 * halo, tc), F32)],
        compiler_params=_cparams(("arbitrary", "arbitrary", "arbitrary")),
    )(proj3, proj3, proj3, conv_w, conv_b)


def _ssd_kernel(xf_ref, xb_ref, dtf_ref, dtb_ref, dttf_ref, dttb_ref, brow_ref, bcol_ref,
                arow_ref, acol_ref, yf_ref, yb_ref, st_ref):
    L = xf_ref.shape[1]
    hg = SSM_HEADS // SSM_GROUPS
    pairs = hg // 2

    @pl.when(pl.program_id(1) == 0)
    def _():
        st_ref[...] = jnp.zeros_like(st_ref)

    rows = lax.broadcasted_iota(jnp.int32, (L, L), 0)
    cols = lax.broadcasted_iota(jnp.int32, (L, L), 1)
    lower = rows >= cols
    upper = rows <= cols
    ltri = lower.astype(BF16)
    utri = upper.astype(BF16)
    lane = lax.broadcasted_iota(jnp.int32, (L, LANES), 1)
    first_half = lane < SSM_HEAD_DIM
    lane1 = lax.broadcasted_iota(jnp.int32, (1, LANES), 1)
    log2e = math.log2(math.e)
    a_row = -jnp.exp(arow_ref[...]) * log2e
    a_col = -jnp.exp(acol_ref[...]) * log2e

    def split3(v):
        hi = v.astype(BF16)
        r1 = v - hi.astype(F32)
        mid = r1.astype(BF16)
        return hi, mid, (r1 - mid.astype(F32)).astype(BF16)

    def cumsum_cols(tri, v):
        return jnp.dot(jnp.concatenate([tri] * 3, axis=1), jnp.concatenate(split3(v), axis=0),
                       preferred_element_type=F32)

    def cumsum_rows(v, tri):
        return jnp.dot(jnp.concatenate(split3(v), axis=1), jnp.concatenate([tri] * 3, axis=0),
                       preferred_element_type=F32)

    for d in range(2):
        x_ref, dt_ref, dtt_ref, y_ref = ((xf_ref, dtf_ref, dttf_ref, yf_ref) if d == 0
                                         else (xb_ref, dtb_ref, dttb_ref, yb_ref))
        a = _softplus(dt_ref[0] + brow_ref[...]) * a_row
        dt_t = _softplus(dtt_ref[0] + bcol_ref[...])
        a_t = dt_t * a_col
        if d == 0:
            cs_col = cumsum_cols(ltri, a)
            cs_row = cumsum_rows(a_t, utri)
            tot = cs_col[L - 1:L, :]
            tot_t = cs_row[:, L - 1:L]
            mask = lower
        else:
            cs_col = cumsum_cols(utri, a)
            cs_row = cumsum_rows(a_t, ltri)
            tot = cs_col[0:1, :]
            tot_t = cs_row[:, 0:1]
            mask = upper
        w_t = dt_t * jnp.exp2(tot_t - cs_row)
        src_t = cs_row - jnp.log2(dt_t)
        chunk_decay = jnp.exp2(tot)

        for g in range(SSM_GROUPS):
            boff = SSM_INNER + g * SSM_STATE
            coff = SSM_INNER + SSM_GROUPS * SSM_STATE + g * SSM_STATE
            bm = x_ref[0, :, boff:boff + SSM_STATE]
            cm = x_ref[0, :, coff:coff + SSM_STATE]
            cb = lax.dot_general(cm, bm, (((1,), (1,)), ((), ())), preferred_element_type=F32)
            bt = bm.astype(F32).T
            st = st_ref[d, g]
            y_off = jnp.dot(cm, st.astype(BF16), preferred_element_type=F32)
            for pr in range(pairs):
                h0 = d * SSM_HEADS + g * hg + 2 * pr
                xoff = (g * pairs + pr) * LANES
                xs = x_ref[0, :, xoff:xoff + LANES]
                zero = jnp.zeros_like(xs)
                rhs = jnp.concatenate([jnp.where(first_half, xs, zero),
                                       jnp.where(first_half, zero, xs)], axis=0)
                ms, ws, dins = [], [], []
                for hh in (h0, h0 + 1):
                    cs_b = jnp.broadcast_to(cs_col[:, hh:hh + 1], (L, L))
                    seg = cs_b - src_t[hh:hh + 1, :]
                    m = cb * jnp.exp2(jnp.where(mask, seg, -jnp.inf))
                    ms.append(m.astype(BF16))
                    ws.append((bt * w_t[hh:hh + 1, :]).astype(BF16))
                    dins.append(jnp.exp2(cs_b))
                y = jnp.dot(jnp.concatenate(ms, axis=1), rhs, preferred_element_type=F32)
                y = y + y_off[:, pr * LANES:(pr + 1) * LANES] * jnp.where(first_half, dins[0], dins[1])
                y_ref[0, :, xoff:xoff + LANES] = y.astype(y_ref.dtype)
                new_st = jnp.dot(jnp.concatenate(ws, axis=1), rhs, preferred_element_type=F32)
                cd = jnp.where(lane1 < SSM_HEAD_DIM, chunk_decay[:, h0:h0 + 1], chunk_decay[:, h0 + 1:h0 + 2])
                st_ref[d, g, :, pr * LANES:(pr + 1) * LANES] = st[:, pr * LANES:(pr + 1) * LANES] * cd + new_st


def _ssd(xbc, dt3, dtt3, bias_row, bias_col, alog_row, alog_col):
    b, s, _ = xbc.shape
    L = min(128, s)
    nc = s // L
    hg = SSM_HEADS // SSM_GROUPS
    fwd = lambda bi, ci: (bi, ci, 0)
    bwd = lambda bi, ci: (bi, nc - 1 - ci, 0)
    fwd_t = lambda bi, ci: (bi, 0, ci)
    bwd_t = lambda bi, ci: (bi, 0, nc - 1 - ci)
    const = lambda bi, ci: (0, 0)
    return pl.pallas_call(
        _ssd_kernel,
        grid=(b, nc),
        in_specs=[
            pl.BlockSpec((1, L, CONV_CH), fwd),
            pl.BlockSpec((1, L, CONV_CH), bwd),
            pl.BlockSpec((1, L, LANES), fwd),
            pl.BlockSpec((1, L, LANES), bwd),
            pl.BlockSpec((1, 2 * SSM_HEADS, L), fwd_t),
            pl.BlockSpec((1, 2 * SSM_HEADS, L), bwd_t),
            pl.BlockSpec((1, LANES), const),
            pl.BlockSpec((2 * SSM_HEADS, 1), const),
            pl.BlockSpec((1, LANES), const),
            pl.BlockSpec((2 * SSM_HEADS, 1), const),
        ],
        out_specs=[
            pl.BlockSpec((1, L, SSM_INNER), fwd),
            pl.BlockSpec((1, L, SSM_INNER), bwd),
        ],
        out_shape=[jax.ShapeDtypeStruct((b, s, SSM_INNER), BF16)] * 2,
        scratch_shapes=[pltpu.VMEM((2, SSM_GROUPS, SSM_STATE, hg * SSM_HEAD_DIM), F32)],
        compiler_params=_cparams(("arbitrary", "arbitrary")),
    )(xbc, xbc, dt3, dt3, dtt3, dtt3, bias_row, bias_col, alog_row, alog_col)


def _rope_norm(t, g, cos, sin_signed, lane):
    tn = _rms(t, g)
    swapped = jnp.where((lane & 32) == 0, pltpu.roll(tn, 96, 1), pltpu.roll(tn, 32, 1))
    return tn * cos + swapped * sin_signed


def _qkprep_kernel(q_ref, k_ref, cos_ref, sin_ref, qg_ref, kg_ref, qo_ref, ko_ref, qsq_ref, ksq_ref, *,
                   q_scale):
    cos = cos_ref[...]
    sin = sin_ref[...]
    lane = lax.broadcasted_iota(jnp.int32, cos.shape, 1)

    def heads(src_ref, g_ref, dst_ref, sq_ref, n_heads, scale):
        sq = jnp.zeros(cos.shape, F32)
        for h in range(n_heads):
            sl = slice(h * ATTN_HEAD_DIM, (h + 1) * ATTN_HEAD_DIM)
            r = (_rope_norm(src_ref[:, sl].astype(F32), g_ref[...], cos, sin, lane) * scale).astype(dst_ref.dtype)
            dst_ref[:, sl] = r
            rf = r.astype(F32)
            sq = jnp.where(lane == h, jnp.sum(rf * rf, axis=-1, keepdims=True), sq)
        sq_ref[...] = sq

    heads(q_ref, qg_ref, qo_ref, qsq_ref, ATTN_HEADS, q_scale)
    heads(k_ref, kg_ref, ko_ref, ksq_ref, ATTN_KV_HEADS, 1.0)


def _qk_prep(proj, cos_t, sin_t, q_norm_g, k_norm_g, seq):
    n = proj.shape[0]
    tm = min(512, seq)
    qw = ATTN_HEADS * ATTN_HEAD_DIM
    kw = ATTN_KV_HEADS * ATTN_HEAD_DIM
    spt = seq // tm
    q_scale = ATTN_HEAD_DIM ** -0.5 * math.log2(math.e)
    return pl.pallas_call(
        functools.partial(_qkprep_kernel, q_scale=q_scale),
        grid=(n // tm,),
        in_specs=[
            pl.BlockSpec((tm, qw), lambda i: (i, Q_OFF // qw)),
            pl.BlockSpec((tm, kw), lambda i: (i, K_OFF // kw)),
            pl.BlockSpec((tm, ATTN_HEAD_DIM), lambda i: (i % spt, 0)),
            pl.BlockSpec((tm, ATTN_HEAD_DIM), lambda i: (i % spt, 0)),
            pl.BlockSpec((1, ATTN_HEAD_DIM), lambda i: (0, 0)),
            pl.BlockSpec((1, ATTN_HEAD_DIM), lambda i: (0, 0)),
        ],
        out_specs=[
            pl.BlockSpec((tm, qw), lambda i: (i, 0)),
            pl.BlockSpec((tm, kw), lambda i: (i, 0)),
            pl.BlockSpec((tm, ATTN_HEAD_DIM), lambda i: (i, 0)),
            pl.BlockSpec((tm, ATTN_HEAD_DIM), lambda i: (i, 0)),
        ],
        out_shape=[jax.ShapeDtypeStruct((n, qw), BF16), jax.ShapeDtypeStruct((n, kw), BF16),
                   jax.ShapeDtypeStruct((n, ATTN_HEAD_DIM), F32),
                   jax.ShapeDtypeStruct((n, ATTN_HEAD_DIM), F32)],
        compiler_params=_cparams(("arbitrary",)),
    )(proj, proj, cos_t, sin_t, q_norm_g, k_norm_g)


def _flash_kernel(small_ref, q_ref, k_ref, v_ref, o_ref, *, tk):
    q = q_ref[0]
    tq = q.shape[0]
    nk = k_ref.shape[1] // tk
    nt = (((1,), (1,)), ((), ()))
    bi, h, qi = pl.program_id(0), pl.program_id(1), pl.program_id(2)
    small = small_ref[(bi * pl.num_programs(1) + h) * pl.num_programs(2) + qi] != 0

    def kv(i):
        off = pl.multiple_of(i * tk, tk)
        return k_ref[0, pl.ds(off, tk), :], v_ref[0, pl.ds(off, tk), :]

    @pl.when(small)
    def _():
        def body(i, carry):
            l, acc = carry
            k, v = kv(i)
            p = jnp.exp2(lax.dot_general(q, k, nt, preferred_element_type=F32))
            for c in range(tk // LANES):
                l = l + p[:, c * LANES:(c + 1) * LANES]
            acc = acc + jnp.dot(p.astype(BF16), v, preferred_element_type=F32)
            return l, acc

        init = (jnp.zeros((tq, LANES), F32), jnp.zeros((tq, ATTN_HEAD_DIM), F32))
        l, acc = lax.fori_loop(0, nk, body, init)
        o_ref[0] = (acc / jnp.sum(l, axis=-1, keepdims=True)).astype(o_ref.dtype)

    @pl.when(jnp.logical_not(small))
    def _():
        def body(i, carry):
            m, l, acc = carry
            k, v = kv(i)
            s = lax.dot_general(q, k, nt, preferred_element_type=F32)
            m_new = jnp.maximum(m, jnp.max(s, axis=-1, keepdims=True))
            alpha = jnp.exp2(m - m_new)
            p = jnp.exp2(s - m_new)
            l = alpha * l + jnp.sum(p, axis=-1, keepdims=True)
            acc = alpha * acc + jnp.dot(p.astype(BF16), v, preferred_element_type=F32)
            return m_new, l, acc

        init = (jnp.full((tq, 1), -jnp.inf, F32), jnp.zeros((tq, 1), F32),
                jnp.zeros((tq, ATTN_HEAD_DIM), F32))
        _, l, acc = lax.fori_loop(0, nk, body, init)
        o_ref[0] = (acc / l).astype(o_ref.dtype)


SCORE_BOUND = 59.0


def _flash(q3, k3, proj3, qsq, ksq):
    b, s, _ = q3.shape
    tq = min(512, s)
    tk = min(8192, s)
    nq = s // tq
    grp = ATTN_HEADS // ATTN_KV_HEADS
    hd = ATTN_HEAD_DIM
    qmax = jnp.max(qsq.reshape(b, nq, tq, hd)[..., :ATTN_HEADS], axis=2)
    kmax = jnp.max(ksq.reshape(b, s, hd)[..., :ATTN_KV_HEADS], axis=1)
    bound_sq = jnp.swapaxes(qmax, 1, 2) * jnp.repeat(kmax, grp, axis=1)[:, :, None]
    small = (bound_sq <= SCORE_BOUND * SCORE_BOUND).astype(jnp.int32).reshape(-1)
    grid_spec = pltpu.PrefetchScalarGridSpec(
        num_scalar_prefetch=1,
        grid=(b, ATTN_HEADS, nq),
        in_specs=[
            pl.BlockSpec((1, tq, hd), lambda bi, h, qi, sm: (bi, qi, h)),
            pl.BlockSpec((1, s, hd), lambda bi, h, qi, sm: (bi, 0, h // grp)),
            pl.BlockSpec((1, s, hd), lambda bi, h, qi, sm: (bi, 0, V_OFF // hd + h // grp)),
        ],
        out_specs=pl.BlockSpec((1, tq, hd), lambda bi, h, qi, sm: (bi, qi, h)),
    )
    return pl.pallas_call(
        functools.partial(_flash_kernel, tk=tk),
        grid_spec=grid_spec,
        out_shape=jax.ShapeDtypeStruct((b, s, ATTN_HEADS * hd), BF16),
        compiler_params=_cparams(("arbitrary", "arbitrary", "arbitrary")),
    )(small, q3, k3, proj3)


def _merge_kernel(yf_ref, yb_ref, xs_ref, z_ref, gate_ref, attn_ref, x_ref, dskip_ref, gssm_ref,
                  wbs_ref, wba_ref, wo_ref, x1_ref):
    d = x_ref.shape[1]
    xs = xs_ref[...].astype(F32)
    y = yf_ref[...].astype(F32) + yb_ref[...].astype(F32) + xs * dskip_ref[...]
    z = z_ref[...].astype(F32)
    y = _rms(y * (z * _sigmoid(z)), gssm_ref[...])
    br_ssm = jnp.dot(y.astype(BF16), wbs_ref[...], preferred_element_type=F32)
    br_attn = jnp.dot(attn_ref[...], wba_ref[...], preferred_element_type=F32)
    g_s = _sigmoid(gate_ref[:, :d].astype(F32))
    g_a = _sigmoid(gate_ref[:, d:].astype(F32))
    merged = (g_s * br_ssm + g_a * br_attn).astype(BF16)
    x1_ref[...] = x_ref[...] + jnp.dot(merged, wo_ref[...], preferred_element_type=F32)


def _merge(y_f, y_b, xbc, proj, attn, x2, dskip_row, g_ssm, w_br_ssm, w_br_attn, w_out):
    n, d = x2.shape
    tm = min(256, n)
    row = lambda i: (i, 0)
    const = lambda i: (0, 0)
    return pl.pallas_call(
        _merge_kernel,
        grid=(n // tm,),
        in_specs=[
            pl.BlockSpec((tm, d), row),
            pl.BlockSpec((tm, d), row),
            pl.BlockSpec((tm, d), row),
            pl.BlockSpec((tm, d), lambda i: (i, Z_OFF // d)),
            pl.BlockSpec((tm, 2 * d), lambda i: (i, GATE_OFF // (2 * d))),
            pl.BlockSpec((tm, d), row),
            pl.BlockSpec((tm, d), row),
            pl.BlockSpec((1, d), const),
            pl.BlockSpec((1, d), const),
            pl.BlockSpec((d, d), const),
            pl.BlockSpec((d, d), const),
            pl.BlockSpec((d, d), const),
        ],
        out_specs=pl.BlockSpec((tm, d), row),
        out_shape=jax.ShapeDtypeStruct((n, d), F32),
        compiler_params=_cparams(("arbitrary",)),
    )(y_f, y_b, xbc, proj, proj, attn, x2, dskip_row, g_ssm, w_br_ssm, w_br_attn, w_out)


def _router_kernel(x1_ref, g_ref, w_ref, b_ref, idx_ref, wt_ref, cnt_ref):
    h = _rms(x1_ref[...], g_ref[...])
    logits = jnp.dot(h, w_ref[...], precision=lax.Precision.HIGHEST,
                     preferred_element_type=F32) + b_ref[...]
    lane = lax.broadcasted_iota(jnp.int32, logits.shape, 1)
    idx_out = jnp.zeros(logits.shape, jnp.int32)
    val_out = jnp.zeros(logits.shape, F32)
    chosen = jnp.zeros(logits.shape, F32)
    vals = []
    for k in range(TOP_K):
        m = jnp.max(logits, axis=-1, keepdims=True)
        idx = jnp.min(jnp.where(logits == m, lane, LANES), axis=-1, keepdims=True)
        idx_out = jnp.where(lane == k, idx, idx_out)
        vals.append(m)
        hit = lane == idx
        chosen = jnp.where(hit, 1.0, chosen)
        logits = jnp.where(hit, -jnp.inf, logits)
    es = [jnp.exp(v - vals[0]) for v in vals]
    tot = es[0] + es[1] + es[2] + es[3]
    for k in range(TOP_K):
        val_out = jnp.where(lane == k, es[k] / tot, val_out)
    idx_ref[...] = idx_out
    wt_ref[...] = val_out
    cnt_ref[0] = jnp.broadcast_to(jnp.sum(chosen, axis=0, keepdims=True), cnt_ref.shape[1:])


def _router(x1, g_ffn, w_router_pad, b_router_pad):
    n, d = x1.shape
    tm = min(MOE_TILE, n)
    return pl.pallas_call(
        _router_kernel,
        grid=(n // tm,),
        in_specs=[
            pl.BlockSpec((tm, d), lambda i: (i, 0)),
            pl.BlockSpec((1, d), lambda i: (0, 0)),
            pl.BlockSpec((d, LANES), lambda i: (0, 0)),
            pl.BlockSpec((1, LANES), lambda i: (0, 0)),
        ],
        out_specs=[pl.BlockSpec((tm, LANES), lambda i: (i, 0)),
                   pl.BlockSpec((tm, LANES), lambda i: (i, 0)),
                   pl.BlockSpec((1, 8, LANES), lambda i: (i, 0, 0))],
        out_shape=[jax.ShapeDtypeStruct((n, LANES), jnp.int32),
                   jax.ShapeDtypeStruct((n, LANES), F32),
                   jax.ShapeDtypeStruct((n // tm, 8, LANES), F32)],
        compiler_params=_cparams(("arbitrary",)),
    )(x1, g_ffn, w_router_pad, b_router_pad)


def _routing_plan(cnt, n, tm, tb):
    nt = n // tm
    gran = MOE_GRAN
    rc = (cnt + gran - 1) // gran * gran
    region = jnp.sum(rc, axis=0)
    region = (region + tb - 1) // tb * tb
    pad_end = jnp.cumsum(region)
    pad_start = pad_end - region
    seg_start = pad_start[None, :] + jnp.cumsum(rc, axis=0) - rc
    stage_off = jnp.cumsum(rc, axis=1) - rc
    n_used = pad_end[-1] // tb
    nb = (TOP_K * n + nt * N_EXPERTS * (gran - 1) + N_EXPERTS * (tb - 1) + tb - 1) // tb
    blk = jnp.arange(nb, dtype=jnp.int32)
    block_e = jnp.sum((pad_end[None, :] <= (jnp.minimum(blk, n_used - 1) * tb)[:, None]).astype(jnp.int32), axis=1)
    block_e = jnp.minimum(block_e, N_EXPERTS - 1)
    flat = lambda a: a.reshape(-1).astype(jnp.int32)
    return dict(seg_start=flat(seg_start), stage_off=flat(stage_off), ngran=flat(rc // gran),
                tot_gran=flat(jnp.sum(rc, axis=1) // gran), block_e=flat(block_e),
                n_used=flat(n_used), nb=nb, stage_off_f=stage_off.astype(F32))


def _stage_rows(tm):
    rows = TOP_K * tm + N_EXPERTS * (MOE_GRAN - 1)
    return (rows + LANES - 1) // LANES * LANES


def _segment_copies(t, soff_ref, sstart_ref, ngran_ref, make_copy):
    def per_expert(e, c):
        so = soff_ref[t * N_EXPERTS + e]
        ss = sstart_ref[t * N_EXPERTS + e]

        def per_granule(g, c2):
            off = g * MOE_GRAN
            make_copy(pl.multiple_of(so + off, MOE_GRAN), pl.multiple_of(ss + off, MOE_GRAN)).start()
            return c2

        lax.fori_loop(0, ngran_ref[t * N_EXPERTS + e], per_granule, 0)
        return c

    lax.fori_loop(0, N_EXPERTS, per_expert, 0)


def _wait_granules(count, make_copy):
    def body(g, c):
        make_copy(0, 0).wait()
        return c
    lax.fori_loop(0, count, body, 0)


def _dispatch_kernel(soff_ref, sstart_ref, ngran_ref, totg_ref, x1_ref, g_ref, idxt_ref, soffc_ref,
                     xs_in_hbm, xs_hbm, stage, sem):
    del xs_in_hbm
    t = pl.program_id(0)
    slot = t % 2
    tm = x1_ref.shape[0]
    rows = stage.shape[1]
    h = _rms(x1_ref[...], g_ref[...]).astype(BF16)
    idxt = idxt_ref[...]
    expert = lax.broadcasted_iota(jnp.int32, (LANES, tm), 0)
    hots = [idxt[k:k + 1, :] == expert for k in range(TOP_K)]
    multi = jnp.zeros((LANES, tm), F32)
    for hot in hots:
        multi = jnp.where(hot, 1.0, multi)
    earlier = (lax.broadcasted_iota(jnp.int32, (tm, tm), 0)
               < lax.broadcasted_iota(jnp.int32, (tm, tm), 1)).astype(BF16)
    rank = jnp.dot(multi.astype(BF16), earlier, preferred_element_type=F32)
    pos = rank + soffc_ref[0]
    row_id = lax.broadcasted_iota(jnp.int32, (rows, tm), 0).astype(F32)
    perm = jnp.zeros((rows, tm), F32)
    for hot in hots:
        srow = jnp.sum(jnp.where(hot, pos, 0.0), axis=0, keepdims=True)
        perm = jnp.where(row_id == srow, 1.0, perm)
    stage[slot] = jnp.dot(perm.astype(BF16), h, preferred_element_type=F32).astype(stage.dtype)

    def copy_from(s):
        def copy(stage_row, sorted_row):
            return pltpu.make_async_copy(stage.at[s, pl.ds(stage_row, MOE_GRAN)],
                                         xs_hbm.at[pl.ds(sorted_row, MOE_GRAN)], sem.at[s])
        return copy

    _segment_copies(t, soff_ref, sstart_ref, ngran_ref, copy_from(slot))

    @pl.when(t > 0)
    def _():
        _wait_granules(totg_ref[jnp.maximum(t - 1, 0)], copy_from(1 - slot))

    @pl.when(t == pl.num_programs(0) - 1)
    def _():
        _wait_granules(totg_ref[t], copy_from(slot))


def _dispatch(x1, g_ffn, idx_t, plan, tm):
    n, d = x1.shape
    nt = n // tm
    rows = _stage_rows(tm)
    xs0 = jnp.zeros((plan["nb"] * MOE_BLOCK, d), BF16)
    soff_col = plan["stage_off_f"].reshape(nt, N_EXPERTS, 1)
    soff_col = jnp.pad(soff_col, ((0, 0), (0, LANES - N_EXPERTS), (0, 0)))
    grid_spec = pltpu.PrefetchScalarGridSpec(
        num_scalar_prefetch=4,
        grid=(nt,),
        in_specs=[
            pl.BlockSpec((tm, d), lambda t, *_: (t, 0)),
            pl.BlockSpec((1, d), lambda t, *_: (0, 0)),
            pl.BlockSpec((8, tm), lambda t, *_: (0, t)),
            pl.BlockSpec((1, LANES, 1), lambda t, *_: (t, 0, 0)),
            pl.BlockSpec(memory_space=pl.ANY),
        ],
        out_specs=pl.BlockSpec(memory_space=pl.ANY),
        scratch_shapes=[pltpu.VMEM((2, rows, d), BF16), pltpu.SemaphoreType.DMA((2,))],
    )
    return pl.pallas_call(
        _dispatch_kernel,
        grid_spec=grid_spec,
        out_shape=jax.ShapeDtypeStruct(xs0.shape, BF16),
        input_output_aliases={8: 0},
        compiler_params=_cparams(("arbitrary",)),
    )(plan["stage_off"], plan["seg_start"], plan["ngran"], plan["tot_gran"], x1, g_ffn, idx_t, soff_col, xs0)


def _expert_kernel(be_ref, nused_ref, x_ref, w1_ref, b1_ref, w2_ref, b2_ref, o_ref, w1b, w2b):
    de = w2_ref.shape[1]
    j = pl.program_id(0)
    live = j < nused_ref[0]
    new_expert = jnp.logical_or(j == 0, be_ref[j] != be_ref[jnp.maximum(j - 1, 0)])

    @pl.when(jnp.logical_and(live, new_expert))
    def _():
        w1b[...] = w1_ref[0].astype(BF16)
        w2b[...] = w2_ref[0].astype(BF16)

    @pl.when(live)
    def _():
        hm = jnp.dot(x_ref[...], w1b[...], preferred_element_type=F32) + b1_ref[0]
        gate = jnp.minimum(hm[:, :de], SWIGLU_LIMIT)
        up = jnp.clip(hm[:, de:], -SWIGLU_LIMIT, SWIGLU_LIMIT)
        act = gate * _sigmoid(SWIGLU_ALPHA * gate) * (up + 1.0)
        y = jnp.dot(act.astype(BF16), w2b[...], preferred_element_type=F32) + b2_ref[0]
        o_ref[...] = y.astype(o_ref.dtype)

    @pl.when(jnp.logical_not(live))
    def _():
        o_ref[...] = jnp.zeros_like(o_ref)


def _experts(xs, plan, w1, b1, w2, b2):
    d = xs.shape[1]
    tb = MOE_BLOCK
    f2 = w1.shape[2]
    de = w2.shape[1]
    last = lambda j, nu: jnp.maximum(jnp.minimum(j, nu[0] - 1), 0)
    grid_spec = pltpu.PrefetchScalarGridSpec(
        num_scalar_prefetch=2,
        grid=(plan["nb"],),
        in_specs=[
            pl.BlockSpec((tb, d), lambda j, be, nu: (last(j, nu), 0)),
            pl.BlockSpec((1, d, f2), lambda j, be, nu: (be[j], 0, 0)),
            pl.BlockSpec((1, 1, f2), lambda j, be, nu: (be[j], 0, 0)),
            pl.BlockSpec((1, de, d), lambda j, be, nu: (be[j], 0, 0)),
            pl.BlockSpec((1, 1, d), lambda j, be, nu: (be[j], 0, 0)),
        ],
        out_specs=pl.BlockSpec((tb, d), lambda j, be, nu: (j, 0)),
        scratch_shapes=[pltpu.VMEM((d, f2), BF16), pltpu.VMEM((de, d), BF16)],
    )
    return pl.pallas_call(
        _expert_kernel,
        grid_spec=grid_spec,
        out_shape=jax.ShapeDtypeStruct(xs.shape, BF16),
        compiler_params=_cparams(("arbitrary",)),
    )(plan["block_e"], plan["n_used"], xs, w1, b1, w2, b2)


def _combine_kernel(soff_ref, sstart_ref, ngran_ref, totg_ref, x1_ref, idx_ref, wt_ref, soffr_ref, g_ref,
                    yb_hbm, o_ref, stage, sem):
    t = pl.program_id(0)
    slot = t % 2
    tm = x1_ref.shape[0]
    rows = stage.shape[1]

    def copy_into(s):
        def copy(stage_row, sorted_row):
            return pltpu.make_async_copy(yb_hbm.at[pl.ds(sorted_row, MOE_GRAN)],
                                         stage.at[s, pl.ds(stage_row, MOE_GRAN)], sem.at[s])
        return copy

    @pl.when(t == 0)
    def _():
        stage[...] = jnp.zeros_like(stage)
        _segment_copies(t, soff_ref, sstart_ref, ngran_ref, copy_into(slot))

    @pl.when(t + 1 < pl.num_programs(0))
    def _():
        _segment_copies(t + 1, soff_ref, sstart_ref, ngran_ref, copy_into(1 - slot))

    idx = idx_ref[...]
    wt = wt_ref[...]
    expert = lax.broadcasted_iota(jnp.int32, (tm, LANES), 1)
    hots = [idx[:, k:k + 1] == expert for k in range(TOP_K)]
    multi = jnp.zeros((tm, LANES), F32)
    for hot in hots:
        multi = jnp.where(hot, 1.0, multi)
    earlier = (lax.broadcasted_iota(jnp.int32, (tm, tm), 0)
               > lax.broadcasted_iota(jnp.int32, (tm, tm), 1)).astype(BF16)
    rank = jnp.dot(earlier, multi.astype(BF16), preferred_element_type=F32)
    pos = rank + soffr_ref[0]
    row_id = lax.broadcasted_iota(jnp.int32, (tm, rows), 1).astype(F32)
    unsort = jnp.zeros((tm, rows), F32)
    for k, hot in enumerate(hots):
        srow = jnp.sum(jnp.where(hot, pos, 0.0), axis=-1, keepdims=True)
        unsort = jnp.where(row_id == srow, wt[:, k:k + 1], unsort)

    _wait_granules(totg_ref[t], copy_into(slot))
    y = jnp.dot(unsort.astype(BF16), stage[slot], preferred_element_type=F32)
    o_ref[...] = _rms(x1_ref[...] + y, g_ref[...])


def _combine(x1, yb, top_idx, top_w, plan, g_final, tm):
    n, d = x1.shape
    nt = n // tm
    rows = _stage_rows(tm)
    soff_row = jnp.pad(plan["stage_off_f"], ((0, 0), (0, LANES - N_EXPERTS))).reshape(nt, 1, LANES)
    grid_spec = pltpu.PrefetchScalarGridSpec(
        num_scalar_prefetch=4,
        grid=(nt,),
        in_specs=[
            pl.BlockSpec((tm, d), lambda t, *_: (t, 0)),
            pl.BlockSpec((tm, LANES), lambda t, *_: (t, 0)),
            pl.BlockSpec((tm, LANES), lambda t, *_: (t, 0)),
            pl.BlockSpec((1, 1, LANES), lambda t, *_: (t, 0, 0)),
            pl.BlockSpec((1, d), lambda t, *_: (0, 0)),
            pl.BlockSpec(memory_space=pl.ANY),
        ],
        out_specs=pl.BlockSpec((tm, d), lambda t, *_: (t, 0)),
        scratch_shapes=[pltpu.VMEM((2, rows, d), BF16), pltpu.SemaphoreType.DMA((2,))],
    )
    return pl.pallas_call(
        _combine_kernel,
        grid_spec=grid_spec,
        out_shape=jax.ShapeDtypeStruct((n, d), F32),
        compiler_params=_cparams(("arbitrary",)),
    )(plan["stage_off"], plan["seg_start"], plan["ngran"], plan["tot_gran"], x1, top_idx, top_w, soff_row,
      g_final, yb)


def _rope_tables(seq):
    half = ATTN_HEAD_DIM // 2
    inv_freq = ROPE_THETA ** (-jnp.arange(0, half, 2, dtype=F32) / half)
    pos = jnp.arange(seq, dtype=jnp.int32)
    ang_r = (pos // GRID_W).astype(F32)[:, None] * inv_freq
    ang_c = (pos % GRID_W).astype(F32)[:, None] * inv_freq
    cos_t = jnp.concatenate([jnp.cos(ang_r)] * 2 + [jnp.cos(ang_c)] * 2, axis=-1)
    sin_t = jnp.concatenate([-jnp.sin(ang_r), jnp.sin(ang_r), -jnp.sin(ang_c), jnp.sin(ang_c)], axis=-1)
    return cos_t, sin_t


def _token_mixer(x2, batch, seq, g_mix, w_in, conv_w, conv_b, dt_bias_f, dt_bias_b, a_log_f, a_log_b, d_skip,
                 g_ssm, q_norm_g, k_norm_g, w_br_ssm, w_br_attn, w_out):
    n, d = x2.shape
    z_end = SSM_INNER
    xbc_end = z_end + CONV_CH
    dtf_end = xbc_end + SSM_HEADS
    dtb_end = dtf_end + SSM_HEADS
    q_end = dtb_end + ATTN_HEADS * ATTN_HEAD_DIM
    k_end = q_end + ATTN_KV_HEADS * ATTN_HEAD_DIM
    v_end = k_end + ATTN_KV_HEADS * ATTN_HEAD_DIM
    w_main = jnp.concatenate([w_in[:, :z_end], w_in[:, dtb_end:q_end], w_in[:, v_end:],
                              w_in[:, z_end:xbc_end], w_in[:, q_end:k_end], w_in[:, k_end:v_end]],
                             axis=1).astype(BF16)
    w_dt = jnp.pad(w_in[:, xbc_end:dtb_end], ((0, 0), (0, LANES - 2 * SSM_HEADS))).astype(BF16)

    proj, dt = _in_proj(x2, g_mix.reshape(1, d), w_main, w_dt)
    proj3 = proj.reshape(batch, seq, PROJ_COLS)

    xbc = _conv(proj3, conv_w, conv_b.reshape(1, CONV_CH))

    dt3 = dt.reshape(batch, seq, LANES)
    dtt3 = jnp.swapaxes(dt3[:, :, :2 * SSM_HEADS], 1, 2)
    bias = jnp.concatenate([dt_bias_f, dt_bias_b])
    alog = jnp.concatenate([a_log_f, a_log_b])
    pad_row = lambda v: jnp.pad(v, (0, LANES - 2 * SSM_HEADS)).reshape(1, LANES)
    y_f, y_b = _ssd(xbc, dt3, dtt3, pad_row(bias), bias.reshape(-1, 1), pad_row(alog), alog.reshape(-1, 1))

    cos_t, sin_t = _rope_tables(seq)
    q_rot, k_rot, qsq, ksq = _qk_prep(proj, cos_t, sin_t, q_norm_g.reshape(1, -1), k_norm_g.reshape(1, -1), seq)
    attn = _flash(q_rot.reshape(batch, seq, -1), k_rot.reshape(batch, seq, -1), proj3, qsq, ksq)

    return _merge(y_f.reshape(n, -1), y_b.reshape(n, -1), xbc.reshape(n, CONV_CH), proj, attn.reshape(n, -1),
                  x2, jnp.repeat(d_skip, SSM_HEAD_DIM).reshape(1, -1), g_ssm.reshape(1, -1),
                  w_br_ssm.astype(BF16), w_br_attn.astype(BF16), w_out.astype(BF16))


def _moe_and_final_norm(x1, g_ffn, w_router, b_router, w_mlp1, b_mlp1, w_mlp2, b_mlp2, g_final):
    n, d = x1.shape
    tm = min(MOE_TILE, n)
    w_r = jnp.pad(w_router, ((0, 0), (0, LANES - N_EXPERTS)))
    b_r = jnp.pad(b_router, (0, LANES - N_EXPERTS), constant_values=-jnp.inf).reshape(1, LANES)
    g_ffn_row = g_ffn.reshape(1, d)
    top_idx, top_w, cnt = _router(x1, g_ffn_row, w_r, b_r)
    plan = _routing_plan(cnt[:, 0, :N_EXPERTS].astype(jnp.int32), n, tm, MOE_BLOCK)
    idx_t = jnp.transpose(top_idx[:, :8])
    xs = _dispatch(x1, g_ffn_row, idx_t, plan, tm)
    yb = _experts(xs, plan, w_mlp1, b_mlp1[:, None, :], w_mlp2, b_mlp2[:, None, :])
    return _combine(x1, yb, top_idx, top_w, plan, g_final.reshape(1, d), tm)


def kernel(x, g_mix, w_in, conv_w, conv_b, dt_bias_f, dt_bias_b, a_log_f, a_log_b, d_skip, g_ssm, q_norm_g,
           k_norm_g, w_br_ssm, w_br_attn, w_out, g_ffn, w_router, b_router, w_mlp1, b_mlp1, w_mlp2, b_mlp2,
           g_final):
    batch, seq, d = x.shape
    assert g_mix.shape[0] == 1, "single-layer model: the final rmsnorm is fused into the MoE combine"
    x2 = x.reshape(batch * seq, d)
    x1 = _token_mixer(x2, batch, seq, g_mix[0], w_in[0], conv_w[0], conv_b[0], dt_bias_f[0], dt_bias_b[0],
                      a_log_f[0], a_log_b[0], d_skip[0], g_ssm[0], q_norm_g[0], k_norm_g[0], w_br_ssm[0],
                      w_br_attn[0], w_out[0])
    out = _moe_and_final_norm(x1, g_ffn[0], w_router[0], b_router[0], w_mlp1[0], b_mlp1[0], w_mlp2[0],
                              b_mlp2[0], g_final)
    return out.reshape(batch, seq, d)
```

```python
import functools
import math

import jax
import jax.numpy as jnp
from jax import lax
from jax.experimental import pallas as pl
from jax.experimental.pallas import tpu as pltpu

F32 = jnp.float32
BF16 = jnp.bfloat16

NORM_EPS = 1e-6
GRID_W = 64
SSM_HEADS = 16
SSM_HEAD_DIM = 64
SSM_INNER = SSM_HEADS * SSM_HEAD_DIM
SSM_GROUPS = 2
SSM_STATE = 128
SSM_CONV = 5
CONV_CH = SSM_INNER + 2 * SSM_GROUPS * SSM_STATE
ATTN_HEADS = 8
ATTN_KV_HEADS = 2
ATTN_HEAD_DIM = 128
ROPE_THETA = 10000.0
N_EXPERTS = 32
TOP_K = 4
SWIGLU_LIMIT = 7.0
SWIGLU_ALPHA = 1.702

LANES = 128
BF16_SUBLANES = 16
VMEM_LIMIT = 56 * 1024 * 1024

Z_OFF, Q_OFF, GATE_OFF, XBC_OFF = 0, 1024, 2048, 4096
K_OFF, V_OFF, PROJ_COLS = 5632, 5888, 6144

MOE_TILE = 512
MOE_BLOCK = 512
MOE_GRAN = BF16_SUBLANES


def _cparams(sem):
    return pltpu.CompilerParams(dimension_semantics=sem, vmem_limit_bytes=VMEM_LIMIT)


def _sigmoid(x):
    return 1.0 / (1.0 + jnp.exp(-x))


def _softplus(x):
    return jnp.maximum(x, 0.0) + jnp.log(1.0 + jnp.exp(-jnp.abs(x)))


def _rms(x, g):
    ms = jnp.mean(x * x, axis=-1, keepdims=True)
    return x * lax.rsqrt(ms + NORM_EPS) * g


def _inproj_kernel(x_ref, g_ref, w_ref, wdt_ref, o_ref, dt_ref, h_scr):
    @pl.when(pl.program_id(1) == 0)
    def _():
        hb = _rms(x_ref[...], g_ref[...]).astype(BF16)
        h_scr[...] = hb
        dt_ref[...] = jnp.dot(hb, wdt_ref[...], preferred_element_type=F32)

    o_ref[...] = jnp.dot(h_scr[...], w_ref[...], preferred_element_type=F32).astype(o_ref.dtype)


def _in_proj(x2, g_mix, w_main, w_dt):
    n, d = x2.shape
    tm = min(1024, n)
    tn = 1024
    return pl.pallas_call(
        _inproj_kernel,
        grid=(n // tm, PROJ_COLS // tn),
        in_specs=[
            pl.BlockSpec((tm, d), lambda i, j: (i, 0)),
            pl.BlockSpec((1, d), lambda i, j: (0, 0)),
            pl.BlockSpec((d, tn), lambda i, j: (0, j)),
            pl.BlockSpec((d, LANES), lambda i, j: (0, 0)),
        ],
        out_specs=[
            pl.BlockSpec((tm, tn), lambda i, j: (i, j)),
            pl.BlockSpec((tm, LANES), lambda i, j: (i, 0)),
        ],
        out_shape=[
            jax.ShapeDtypeStruct((n, PROJ_COLS), BF16),
            jax.ShapeDtypeStruct((n, LANES), F32),
        ],
        scratch_shapes=[pltpu.VMEM((tm, d), BF16)],
        compiler_params=_cparams(("arbitrary", "arbitrary")),
    )(x2, g_mix, w_main, w_dt)


def _conv_kernel(prev_ref, cur_ref, next_ref, w_ref, b_ref, o_ref, scr):
    s = pl.program_id(1)
    ts = cur_ref.shape[1]
    halo = BF16_SUBLANES
    pad = (SSM_CONV - 1) // 2
    prev = prev_ref[0].astype(F32)
    nxt = next_ref[0].astype(F32)
    scr[0:halo, :] = jnp.where(s == 0, 0.0, prev)
    scr[halo:halo + ts, :] = cur_ref[0].astype(F32)
    scr[halo + ts:2 * halo + ts, :] = jnp.where(s == pl.num_programs(1) - 1, 0.0, nxt)
    acc = jnp.zeros((ts, cur_ref.shape[2]), F32) + b_ref[...]
    for k in range(SSM_CONV):
        acc = acc + w_ref[k:k + 1, :] * scr[halo - pad + k:halo - pad + k + ts, :]
    o_ref[0] = (acc * _sigmoid(acc)).astype(o_ref.dtype)


def _conv(proj3, conv_w, conv_b):
    b, s, _ = proj3.shape
    ts = min(512, s)
    tc = 512
    halo = BF16_SUBLANES
    hb = ts // halo
    col0 = XBC_OFF // tc
    return pl.pallas_call(
        _conv_kernel,
        grid=(b, s // ts, CONV_CH // tc),
        in_specs=[
            pl.BlockSpec((1, halo, tc), lambda bi, si, ci: (bi, jnp.maximum(si * hb - 1, 0), col0 + ci)),
            pl.BlockSpec((1, ts, tc), lambda bi, si, ci: (bi, si, col0 + ci)),
            pl.BlockSpec((1, halo, tc),
                         lambda bi, si, ci: (bi, jnp.minimum((si + 1) * hb, s // halo - 1), col0 + ci)),
            pl.BlockSpec((SSM_CONV, tc), lambda bi, si, ci: (0, ci)),
            pl.BlockSpec((1, tc), lambda bi, si, ci: (0, ci)),
        ],
        out_specs=pl.BlockSpec((1, ts, tc), lambda bi, si, ci: (bi, si, ci)),
        out_shape=jax.ShapeDtypeStruct((b, s, CONV_CH), BF16),
        scratch_shapes=[pltpu.VMEM((ts + 2 * halo, tc), F32)],
        compiler_params=_cparams(("arbitrary", "arbitrary", "arbitrary")),
    )(proj3, proj3, proj3, conv_w, conv_b)


def _ssd_kernel(xf_ref, xb_ref, dtf_ref, dtb_ref, dttf_ref, dttb_ref, brow_ref, bcol_ref,
                arow_ref, acol_ref, yf_ref, yb_ref, st_ref):
    L = xf_ref.shape[1]
    hg = SSM_HEADS // SSM_GROUPS
    pairs = hg // 2

    @pl.when(pl.program_id(1) == 0)
    def _():
        st_ref[...] = jnp.zeros_like(st_ref)

    rows = lax.broadcasted_iota(jnp.int32, (L, L), 0)
    cols = lax.broadcasted_iota(jnp.int32, (L, L), 1)
    lower = rows >= cols
    upper = rows <= cols
    ltri = lower.astype(BF16)
    utri = upper.astype(BF16)
    lane = lax.broadcasted_iota(jnp.int32, (L, LANES), 1)
    first_half = lane < SSM_HEAD_DIM
    lane1 = lax.broadcasted_iota(jnp.int32, (1, LANES), 1)
    log2e = math.log2(math.e)
    a_row = -jnp.exp(arow_ref[...]) * log2e
    a_col = -jnp.exp(acol_ref[...]) * log2e

    def split3(v):
        hi = v.astype(BF16)
        r1 = v - hi.astype(F32)
        mid = r1.astype(BF16)
        return hi, mid, (r1 - mid.astype(F32)).astype(BF16)

    def cumsum_cols(tri, v):
        return jnp.dot(jnp.concatenate([tri] * 3, axis=1), jnp.concatenate(split3(v), axis=0),
                       preferred_element_type=F32)

    def cumsum_rows(v, tri):
        return jnp.dot(jnp.concatenate(split3(v), axis=1), jnp.concatenate([tri] * 3, axis=0),
                       preferred_element_type=F32)

    for d in range(2):
        x_ref, dt_ref, dtt_ref, y_ref = ((xf_ref, dtf_ref, dttf_ref, yf_ref) if d == 0
                                         else (xb_ref, dtb_ref, dttb_ref, yb_ref))
        a = _softplus(dt_ref[0] + brow_ref[...]) * a_row
        dt_t = _softplus(dtt_ref[0] + bcol_ref[...])
        a_t = dt_t * a_col
        if d == 0:
            cs_col = cumsum_cols(ltri, a)
            cs_row = cumsum_rows(a_t, utri)
            tot = cs_col[L - 1:L, :]
            tot_t = cs_row[:, L - 1:L]
            mask = lower
        else:
            cs_col = cumsum_cols(utri, a)
            cs_row = cumsum_rows(a_t, ltri)
            tot = cs_col[0:1, :]
            tot_t = cs_row[:, 0:1]
            mask = upper
        w_t = dt_t * jnp.exp2(tot_t - cs_row)
        src_t = cs_row - jnp.log2(dt_t)
        chunk_decay = jnp.exp2(tot)

        for g in range(SSM_GROUPS):
            boff = SSM_INNER + g * SSM_STATE
            coff = SSM_INNER + SSM_GROUPS * SSM_STATE + g * SSM_STATE
            bm = x_ref[0, :, boff:boff + SSM_STATE]
            cm = x_ref[0, :, coff:coff + SSM_STATE]
            cb = lax.dot_general(cm, bm, (((1,), (1,)), ((), ())), preferred_element_type=F32)
            bt = bm.astype(F32).T
            st = st_ref[d, g]
            y_off = jnp.dot(cm, st.astype(BF16), preferred_element_type=F32)
            for pr in range(pairs):
                h0 = d * SSM_HEADS + g * hg + 2 * pr
                xoff = (g * pairs + pr) * LANES
                xs = x_ref[0, :, xoff:xoff + LANES]
                zero = jnp.zeros_like(xs)
                rhs = jnp.concatenate([jnp.where(first_half, xs, zero),
                                       jnp.where(first_half, zero, xs)], axis=0)
                ms, ws, dins = [], [], []
                for hh in (h0, h0 + 1):
                    cs_b = jnp.broadcast_to(cs_col[:, hh:hh + 1], (L, L))
                    seg = cs_b - src_t[hh:hh + 1, :]
                    m = cb * jnp.exp2(jnp.where(mask, seg, -jnp.inf))
                    ms.append(m.astype(BF16))
                    ws.append((bt * w_t[hh:hh + 1, :]).astype(BF16))
                    dins.append(jnp.exp2(cs_b))
                y = jnp.dot(jnp.concatenate(ms, axis=1), rhs, preferred_element_type=F32)
                y = y + y_off[:, pr * LANES:(pr + 1) * LANES] * jnp.where(first_half, dins[0], dins[1])
                y_ref[0, :, xoff:xoff + LANES] = y.astype(y_ref.dtype)
                new_st = jnp.dot(jnp.concatenate(ws, axis=1), rhs, preferred_element_type=F32)
                cd = jnp.where(lane1 < SSM_HEAD_DIM, chunk_decay[:, h0:h0 + 1], chunk_decay[:, h0 + 1:h0 + 2])
                st_ref[d, g, :, pr * LANES:(pr + 1) * LANES] = st[:, pr * LANES:(pr + 1) * LANES] * cd + new_st


def _ssd(xbc, dt3, dtt3, bias_row, bias_col, alog_row, alog_col):
    b, s, _ = xbc.shape
    L = min(128, s)
    nc = s // L
    hg = SSM_HEADS // SSM_GROUPS
    fwd = lambda bi, ci: (bi, ci, 0)
    bwd = lambda bi, ci: (bi, nc - 1 - ci, 0)
    fwd_t = lambda bi, ci: (bi, 0, ci)
    bwd_t = lambda bi, ci: (bi, 0, nc - 1 - ci)
    const = lambda bi, ci: (0, 0)
    return pl.pallas_call(
        _ssd_kernel,
        grid=(b, nc),
        in_specs=[
            pl.BlockSpec((1, L, CONV_CH), fwd),
            pl.BlockSpec((1, L, CONV_CH), bwd),
            pl.BlockSpec((1, L, LANES), fwd),
            pl.BlockSpec((1, L, LANES), bwd),
            pl.BlockSpec((1, 2 * SSM_HEADS, L), fwd_t),
            pl.BlockSpec((1, 2 * SSM_HEADS, L), bwd_t),
            pl.BlockSpec((1, LANES), const),
            pl.BlockSpec((2 * SSM_HEADS, 1), const),
            pl.BlockSpec((1, LANES), const),
            pl.BlockSpec((2 * SSM_HEADS, 1), const),
        ],
        out_specs=[
            pl.BlockSpec((1, L, SSM_INNER), fwd),
            pl.BlockSpec((1, L, SSM_INNER), bwd),
        ],
        out_shape=[jax.ShapeDtypeStruct((b, s, SSM_INNER), BF16)] * 2,
        scratch_shapes=[pltpu.VMEM((2, SSM_GROUPS, SSM_STATE, hg * SSM_HEAD_DIM), F32)],
        compiler_params=_cparams(("arbitrary", "arbitrary")),
    )(xbc, xbc, dt3, dt3, dtt3, dtt3, bias_row, bias_col, alog_row, alog_col)


def _rope_norm(t, g, cos, sin_signed, lane):
    tn = _rms(t, g)
    swapped = jnp.where((lane & 32) == 0, pltpu.roll(tn, 96, 1), pltpu.roll(tn, 32, 1))
    return tn * cos + swapped * sin_signed


Q_SCALE = ATTN_HEAD_DIM ** -0.5 * math.log2(math.e)


def _qkprep_kernel(q_ref, k_ref, cos_ref, sin_ref, qg_ref, kg_ref, qo_ref, ko_ref):
    cos = cos_ref[...]
    sin = sin_ref[...]
    lane = lax.broadcasted_iota(jnp.int32, cos.shape, 1)

    def heads(src_ref, g_ref, dst_ref, n_heads, scale):
        for h in range(n_heads):
            sl = slice(h * ATTN_HEAD_DIM, (h + 1) * ATTN_HEAD_DIM)
            r = _rope_norm(src_ref[:, sl].astype(F32), g_ref[...], cos, sin, lane) * scale
            dst_ref[:, sl] = r.astype(dst_ref.dtype)

    heads(q_ref, qg_ref, qo_ref, ATTN_HEADS, Q_SCALE)
    heads(k_ref, kg_ref, ko_ref, ATTN_KV_HEADS, 1.0)


def _qk_prep(proj, cos_t, sin_t, q_norm_g, k_norm_g, seq):
    n = proj.shape[0]
    tm = min(512, seq)
    qw = ATTN_HEADS * ATTN_HEAD_DIM
    kw = ATTN_KV_HEADS * ATTN_HEAD_DIM
    spt = seq // tm
    return pl.pallas_call(
        _qkprep_kernel,
        grid=(n // tm,),
        in_specs=[
            pl.BlockSpec((tm, qw), lambda i: (i, Q_OFF // qw)),
            pl.BlockSpec((tm, kw), lambda i: (i, K_OFF // kw)),
            pl.BlockSpec((tm, ATTN_HEAD_DIM), lambda i: (i % spt, 0)),
            pl.BlockSpec((tm, ATTN_HEAD_DIM), lambda i: (i % spt, 0)),
            pl.BlockSpec((1, ATTN_HEAD_DIM), lambda i: (0, 0)),
            pl.BlockSpec((1, ATTN_HEAD_DIM), lambda i: (0, 0)),
        ],
        out_specs=[
            pl.BlockSpec((tm, qw), lambda i: (i, 0)),
            pl.BlockSpec((tm, kw), lambda i: (i, 0)),
        ],
        out_shape=[jax.ShapeDtypeStruct((n, qw), BF16), jax.ShapeDtypeStruct((n, kw), BF16)],
        compiler_params=_cparams(("arbitrary",)),
    )(proj, proj, cos_t, sin_t, q_norm_g, k_norm_g)


def _flash_kernel(small_ref, q_ref, k_ref, v_ref, o_ref, *, tk):
    q = q_ref[0]
    tq = q.shape[0]
    nk = k_ref.shape[1] // tk
    nt = (((1,), (1,)), ((), ()))
    small = small_ref[0] != 0

    def kv(i):
        off = pl.multiple_of(i * tk, tk)
        return k_ref[0, pl.ds(off, tk), :], v_ref[0, pl.ds(off, tk), :]

    @pl.when(small)
    def _():
        def body(i, carry):
            l, acc = carry
            k, v = kv(i)
            p = jnp.exp2(lax.dot_general(q, k, nt, preferred_element_type=F32))
            for c in range(tk // LANES):
                l = l + p[:, c * LANES:(c + 1) * LANES]
            acc = acc + jnp.dot(p.astype(BF16), v, preferred_element_type=F32)
            return l, acc

        init = (jnp.zeros((tq, LANES), F32), jnp.zeros((tq, ATTN_HEAD_DIM), F32))
        l, acc = lax.fori_loop(0, nk, body, init)
        o_ref[0] = (acc / jnp.sum(l, axis=-1, keepdims=True)).astype(o_ref.dtype)

    @pl.when(jnp.logical_not(small))
    def _():
        def body(i, carry):
            m, l, acc = carry
            k, v = kv(i)
            s = lax.dot_general(q, k, nt, preferred_element_type=F32)
            m_new = jnp.maximum(m, jnp.max(s, axis=-1, keepdims=True))
            alpha = jnp.exp2(m - m_new)
            p = jnp.exp2(s - m_new)
            l = alpha * l + jnp.sum(p, axis=-1, keepdims=True)
            acc = alpha * acc + jnp.dot(p.astype(BF16), v, preferred_element_type=F32)
            return m_new, l, acc

        init = (jnp.full((tq, 1), -jnp.inf, F32), jnp.zeros((tq, 1), F32),
                jnp.zeros((tq, ATTN_HEAD_DIM), F32))
        _, l, acc = lax.fori_loop(0, nk, body, init)
        o_ref[0] = (acc / l).astype(o_ref.dtype)


SCORE_BOUND = 59.0


def _flash(q3, k3, proj3, q_norm_g, k_norm_g):
    b, s, _ = q3.shape
    tq = min(512, s)
    tk = min(8192, s)
    nq = s // tq
    grp = ATTN_HEADS // ATTN_KV_HEADS
    hd = ATTN_HEAD_DIM
    bound = hd * Q_SCALE * jnp.max(jnp.abs(q_norm_g)) * jnp.max(jnp.abs(k_norm_g)) * 1.02
    small = (bound <= SCORE_BOUND).astype(jnp.int32).reshape(1)
    grid_spec = pltpu.PrefetchScalarGridSpec(
        num_scalar_prefetch=1,
        grid=(b, ATTN_HEADS, nq),
        in_specs=[
            pl.BlockSpec((1, tq, hd), lambda bi, h, qi, sm: (bi, qi, h)),
            pl.BlockSpec((1, s, hd), lambda bi, h, qi, sm: (bi, 0, h // grp)),
            pl.BlockSpec((1, s, hd), lambda bi, h, qi, sm: (bi, 0, V_OFF // hd + h // grp)),
        ],
        out_specs=pl.BlockSpec((1, tq, hd), lambda bi, h, qi, sm: (bi, qi, h)),
    )
    return pl.pallas_call(
        functools.partial(_flash_kernel, tk=tk),
        grid_spec=grid_spec,
        out_shape=jax.ShapeDtypeStruct((b, s, ATTN_HEADS * hd), BF16),
        compiler_params=_cparams(("arbitrary", "arbitrary", "arbitrary")),
    )(small, q3, k3, proj3)


def _merge_kernel(yf_ref, yb_ref, xs_ref, z_ref, gate_ref, attn_ref, x_ref, dskip_ref, gssm_ref,
                  wbs_ref, wba_ref, wo_ref, x1_ref):
    d = x_ref.shape[1]
    xs = xs_ref[...].astype(F32)
    y = yf_ref[...].astype(F32) + yb_ref[...].astype(F32) + xs * dskip_ref[...]
    z = z_ref[...].astype(F32)
    y = _rms(y * (z * _sigmoid(z)), gssm_ref[...])
    br_ssm = jnp.dot(y.astype(BF16), wbs_ref[...], preferred_element_type=F32)
    br_attn = jnp.dot(attn_ref[...], wba_ref[...], preferred_element_type=F32)
    g_s = _sigmoid(gate_ref[:, :d].astype(F32))
    g_a = _sigmoid(gate_ref[:, d:].astype(F32))
    merged = (g_s * br_ssm + g_a * br_attn).astype(BF16)
    x1_ref[...] = x_ref[...] + jnp.dot(merged, wo_ref[...], preferred_element_type=F32)


def _merge(y_f, y_b, xbc, proj, attn, x2, dskip_row, g_ssm, w_br_ssm, w_br_attn, w_out):
    n, d = x2.shape
    tm = min(256, n)
    row = lambda i: (i, 0)
    const = lambda i: (0, 0)
    return pl.pallas_call(
        _merge_kernel,
        grid=(n // tm,),
        in_specs=[
            pl.BlockSpec((tm, d), row),
            pl.BlockSpec((tm, d), row),
            pl.BlockSpec((tm, d), row),
            pl.BlockSpec((tm, d), lambda i: (i, Z_OFF // d)),
            pl.BlockSpec((tm, 2 * d), lambda i: (i, GATE_OFF // (2 * d))),
            pl.BlockSpec((tm, d), row),
            pl.BlockSpec((tm, d), row),
            pl.BlockSpec((1, d), const),
            pl.BlockSpec((1, d), const),
            pl.BlockSpec((d, d), const),
            pl.BlockSpec((d, d), const),
            pl.BlockSpec((d, d), const),
        ],
        out_specs=pl.BlockSpec((tm, d), row),
        out_shape=jax.ShapeDtypeStruct((n, d), F32),
        compiler_params=_cparams(("arbitrary",)),
    )(y_f, y_b, xbc, proj, proj, attn, x2, dskip_row, g_ssm, w_br_ssm, w_br_attn, w_out)


def _router_kernel(x1_ref, g_ref, w_ref, b_ref, idx_ref, wt_ref, cnt_ref):
    h = _rms(x1_ref[...], g_ref[...])
    logits = jnp.dot(h, w_ref[...], precision=lax.Precision.HIGHEST,
                     preferred_element_type=F32) + b_ref[...]
    lane = lax.broadcasted_iota(jnp.int32, logits.shape, 1)
    idx_out = jnp.zeros(logits.shape, jnp.int32)
    val_out = jnp.zeros(logits.shape, F32)
    chosen = jnp.zeros(logits.shape, F32)
    vals = []
    for k in range(TOP_K):
        m = jnp.max(logits, axis=-1, keepdims=True)
        idx = jnp.min(jnp.where(logits == m, lane, LANES), axis=-1, keepdims=True)
        idx_out = jnp.where(lane == k, idx, idx_out)
        vals.append(m)
        hit = lane == idx
        chosen = jnp.where(hit, 1.0, chosen)
        logits = jnp.where(hit, -jnp.inf, logits)
    es = [jnp.exp(v - vals[0]) for v in vals]
    tot = es[0] + es[1] + es[2] + es[3]
    for k in range(TOP_K):
        val_out = jnp.where(lane == k, es[k] / tot, val_out)
    idx_ref[...] = idx_out
    wt_ref[...] = val_out
    cnt_ref[0] = jnp.broadcast_to(jnp.sum(chosen, axis=0, keepdims=True), cnt_ref.shape[1:])


def _router(x1, g_ffn, w_router_pad, b_router_pad):
    n, d = x1.shape
    tm = min(MOE_TILE, n)
    return pl.pallas_call(
        _router_kernel,
        grid=(n // tm,),
        in_specs=[
            pl.BlockSpec((tm, d), lambda i: (i, 0)),
            pl.BlockSpec((1, d), lambda i: (0, 0)),
            pl.BlockSpec((d, LANES), lambda i: (0, 0)),
            pl.BlockSpec((1, LANES), lambda i: (0, 0)),
        ],
        out_specs=[pl.BlockSpec((tm, LANES), lambda i: (i, 0)),
                   pl.BlockSpec((tm, LANES), lambda i: (i, 0)),
                   pl.BlockSpec((1, 8, LANES), lambda i: (i, 0, 0))],
        out_shape=[jax.ShapeDtypeStruct((n, LANES), jnp.int32),
                   jax.ShapeDtypeStruct((n, LANES), F32),
                   jax.ShapeDtypeStruct((n // tm, 8, LANES), F32)],
        compiler_params=_cparams(("arbitrary",)),
    )(x1, g_ffn, w_router_pad, b_router_pad)


def _routing_plan(cnt, n, tm, tb):
    nt = n // tm
    gran = MOE_GRAN
    rc = (cnt + gran - 1) // gran * gran
    covered = jnp.sum(rc, axis=0)
    region = (covered + tb - 1) // tb * tb
    pad_end = jnp.cumsum(region)
    pad_start = pad_end - region
    seg_start = pad_start[None, :] + jnp.cumsum(rc, axis=0) - rc
    stage_off = jnp.cumsum(rc, axis=1) - rc
    n_used = pad_end[-1] // tb
    nb = (TOP_K * n + nt * N_EXPERTS * (gran - 1) + N_EXPERTS * (tb - 1) + tb - 1) // tb
    blk = jnp.arange(nb, dtype=jnp.int32)
    block_e = jnp.sum((pad_end[None, :] <= (jnp.minimum(blk, n_used - 1) * tb)[:, None]).astype(jnp.int32), axis=1)
    block_e = jnp.minimum(block_e, N_EXPERTS - 1)
    flat = lambda a: a.reshape(-1).astype(jnp.int32)
    tail_gran = (region - covered) // gran
    return dict(seg_start=flat(seg_start), stage_off=flat(stage_off), ngran=flat(rc // gran),
                tot_gran=flat(jnp.sum(rc, axis=1) // gran), block_e=flat(block_e),
                n_used=flat(n_used), nb=nb, stage_off_f=stage_off.astype(F32),
                tail_start=flat(pad_start + covered), tail_gran=flat(tail_gran),
                tot_tail=flat(jnp.sum(tail_gran)))


def _stage_rows(tm):
    rows = TOP_K * tm + N_EXPERTS * (MOE_GRAN - 1)
    return (rows + 2 * LANES - 1) // (2 * LANES) * (2 * LANES)


def _segment_copies(t, soff_ref, sstart_ref, ngran_ref, make_copy):
    def per_expert(e, c):
        so = soff_ref[t * N_EXPERTS + e]
        ss = sstart_ref[t * N_EXPERTS + e]

        def per_granule(g, c2):
            off = g * MOE_GRAN
            make_copy(pl.multiple_of(so + off, MOE_GRAN), pl.multiple_of(ss + off, MOE_GRAN)).start()
            return c2

        lax.fori_loop(0, ngran_ref[t * N_EXPERTS + e], per_granule, 0)
        return c

    lax.fori_loop(0, N_EXPERTS, per_expert, 0)


def _wait_granules(count, make_copy):
    def body(g, c):
        make_copy(0, 0).wait()
        return c
    lax.fori_loop(0, count, body, 0)


def _dispatch_kernel(soff_ref, sstart_ref, ngran_ref, totg_ref, tstart_ref, tgran_ref, misc_ref,
                     x1_ref, g_ref, idxt_ref, soffc_ref, xs_hbm, stage, zbuf, sem, zsem, *, n_blocks):
    t = pl.program_id(0)
    slot = t % 2
    tm = x1_ref.shape[0]
    rows = stage.shape[1]
    h = _rms(x1_ref[...], g_ref[...]).astype(BF16)
    idxt = idxt_ref[...]
    expert = lax.broadcasted_iota(jnp.int32, (LANES, tm), 0)
    hots = [idxt[k:k + 1, :] == expert for k in range(TOP_K)]
    multi = jnp.zeros((LANES, tm), F32)
    for hot in hots:
        multi = jnp.where(hot, 1.0, multi)
    earlier = (lax.broadcasted_iota(jnp.int32, (tm, tm), 0)
               < lax.broadcasted_iota(jnp.int32, (tm, tm), 1)).astype(BF16)
    rank = jnp.dot(multi.astype(BF16), earlier, preferred_element_type=F32)
    pos = rank + soffc_ref[0]
    srows = [jnp.sum(jnp.where(hot, pos, 0.0), axis=0, keepdims=True) for hot in hots]
    chunk = 2 * LANES
    row_id = lax.broadcasted_iota(jnp.int32, (chunk, tm), 0).astype(F32)
    for c in range(rows // chunk):
        perm = jnp.zeros((chunk, tm), F32)
        for srow in srows:
            perm = jnp.where(row_id == srow - float(c * chunk), 1.0, perm)
        stage[slot, c * chunk:(c + 1) * chunk, :] = jnp.dot(
            perm.astype(BF16), h, preferred_element_type=F32).astype(stage.dtype)

    def copy_from(s):
        def copy(stage_row, sorted_row):
            return pltpu.make_async_copy(stage.at[s, pl.ds(stage_row, MOE_GRAN)],
                                         xs_hbm.at[pl.ds(sorted_row, MOE_GRAN)], sem.at[s])
        return copy

    _segment_copies(t, soff_ref, sstart_ref, ngran_ref, copy_from(slot))

    @pl.when(t > 0)
    def _():
        _wait_granules(totg_ref[jnp.maximum(t - 1, 0)], copy_from(1 - slot))

    @pl.when(t == pl.num_programs(0) - 1)
    def _():
        _wait_granules(totg_ref[t], copy_from(slot))
        tb = zbuf.shape[0]
        n_used = misc_ref[0]
        zbuf[...] = jnp.zeros_like(zbuf)

        def zero_gran(row):
            return pltpu.make_async_copy(zbuf.at[pl.ds(0, MOE_GRAN)], xs_hbm.at[pl.ds(row, MOE_GRAN)], zsem)

        def zero_block(blk):
            return pltpu.make_async_copy(zbuf, xs_hbm.at[pl.ds(pl.multiple_of(blk * tb, tb), tb)], zsem)

        def tails(e, c):
            def one(g, c2):
                zero_gran(pl.multiple_of(tstart_ref[e] + g * MOE_GRAN, MOE_GRAN)).start()
                return c2
            lax.fori_loop(0, tgran_ref[e], one, 0)
            return c

        def start_block(blk, c):
            zero_block(blk).start()
            return c

        def wait_gran(g, c):
            zero_gran(0).wait()
            return c

        def wait_block(blk, c):
            zero_block(0).wait()
            return c

        lax.fori_loop(0, N_EXPERTS, tails, 0)
        lax.fori_loop(n_used, n_blocks, start_block, 0)
        lax.fori_loop(0, misc_ref[1], wait_gran, 0)
        lax.fori_loop(n_used, n_blocks, wait_block, 0)


def _dispatch(x1, g_ffn, idx_t, plan, tm):
    n, d = x1.shape
    nt = n // tm
    rows = _stage_rows(tm)
    soff_col = plan["stage_off_f"].reshape(nt, N_EXPERTS, 1)
    soff_col = jnp.pad(soff_col, ((0, 0), (0, LANES - N_EXPERTS), (0, 0)))
    misc = jnp.concatenate([plan["n_used"], plan["tot_tail"]])
    grid_spec = pltpu.PrefetchScalarGridSpec(
        num_scalar_prefetch=7,
        grid=(nt,),
        in_specs=[
            pl.BlockSpec((tm, d), lambda t, *_: (t, 0)),
            pl.BlockSpec((1, d), lambda t, *_: (0, 0)),
            pl.BlockSpec((8, tm), lambda t, *_: (0, t)),
            pl.BlockSpec((1, LANES, 1), lambda t, *_: (t, 0, 0)),
        ],
        out_specs=pl.BlockSpec(memory_space=pl.ANY),
        scratch_shapes=[pltpu.VMEM((2, rows, d), BF16), pltpu.VMEM((MOE_BLOCK, d), BF16),
                        pltpu.SemaphoreType.DMA((2,)), pltpu.SemaphoreType.DMA(())],
    )
    return pl.pallas_call(
        functools.partial(_dispatch_kernel, n_blocks=plan["nb"]),
        grid_spec=grid_spec,
        out_shape=jax.ShapeDtypeStruct((plan["nb"] * MOE_BLOCK, d), BF16),
        compiler_params=_cparams(("arbitrary",)),
    )(plan["stage_off"], plan["seg_start"], plan["ngran"], plan["tot_gran"], plan["tail_start"],
      plan["tail_gran"], misc, x1, g_ffn, idx_t, soff_col)


def _expert_kernel(be_ref, nused_ref, x_ref, w1_ref, b1_ref, w2_ref, b2_ref, o_ref, w1b, w2b):
    de = w2_ref.shape[1]
    j = pl.program_id(0)
    live = j < nused_ref[0]
    new_expert = jnp.logical_or(j == 0, be_ref[j] != be_ref[jnp.maximum(j - 1, 0)])

    @pl.when(jnp.logical_and(live, new_expert))
    def _():
        w1b[...] = w1_ref[0].astype(BF16)
        w2b[...] = w2_ref[0].astype(BF16)

    @pl.when(live)
    def _():
        hm = jnp.dot(x_ref[...], w1b[...], preferred_element_type=F32) + b1_ref[0]
        gate = jnp.minimum(hm[:, :de], SWIGLU_LIMIT)
        up = jnp.clip(hm[:, de:], -SWIGLU_LIMIT, SWIGLU_LIMIT)
        act = gate * _sigmoid(SWIGLU_ALPHA * gate) * (up + 1.0)
        y = jnp.dot(act.astype(BF16), w2b[...], preferred_element_type=F32) + b2_ref[0]
        o_ref[...] = y.astype(o_ref.dtype)

    @pl.when(jnp.logical_not(live))
    def _():
        o_ref[...] = jnp.zeros_like(o_ref)


def _experts(xs, plan, w1, b1, w2, b2):
    d = xs.shape[1]
    tb = MOE_BLOCK
    f2 = w1.shape[2]
    de = w2.shape[1]
    last = lambda j, nu: jnp.maximum(jnp.minimum(j, nu[0] - 1), 0)
    grid_spec = pltpu.PrefetchScalarGridSpec(
        num_scalar_prefetch=2,
        grid=(plan["nb"],),
        in_specs=[
            pl.BlockSpec((tb, d), lambda j, be, nu: (last(j, nu), 0)),
            pl.BlockSpec((1, d, f2), lambda j, be, nu: (be[j], 0, 0)),
            pl.BlockSpec((1, 1, f2), lambda j, be, nu: (be[j], 0, 0)),
            pl.BlockSpec((1, de, d), lambda j, be, nu: (be[j], 0, 0)),
            pl.BlockSpec((1, 1, d), lambda j, be, nu: (be[j], 0, 0)),
        ],
        out_specs=pl.BlockSpec((tb, d), lambda j, be, nu: (j, 0)),
        scratch_shapes=[pltpu.VMEM((d, f2), BF16), pltpu.VMEM((de, d), BF16)],
    )
    return pl.pallas_call(
        _expert_kernel,
        grid_spec=grid_spec,
        out_shape=jax.ShapeDtypeStruct(xs.shape, BF16),
        compiler_params=_cparams(("arbitrary",)),
    )(plan["block_e"], plan["n_used"], xs, w1, b1, w2, b2)


def _combine_kernel(soff_ref, sstart_ref, ngran_ref, totg_ref, x1_ref, idx_ref, wt_ref, soffr_ref, g_ref,
                    yb_hbm, o_ref, stage, sem):
    t = pl.program_id(0)
    slot = t % 2
    tm = x1_ref.shape[0]
    rows = stage.shape[1]

    def copy_into(s):
        def copy(stage_row, sorted_row):
            return pltpu.make_async_copy(yb_hbm.at[pl.ds(sorted_row, MOE_GRAN)],
                                         stage.at[s, pl.ds(stage_row, MOE_GRAN)], sem.at[s])
        return copy

    @pl.when(t == 0)
    def _():
        stage[...] = jnp.zeros_like(stage)
        _segment_copies(t, soff_ref, sstart_ref, ngran_ref, copy_into(slot))

    @pl.when(t + 1 < pl.num_programs(0))
    def _():
        _segment_copies(t + 1, soff_ref, sstart_ref, ngran_ref, copy_into(1 - slot))

    idx = idx_ref[...]
    wt = wt_ref[...]
    expert = lax.broadcasted_iota(jnp.int32, (tm, LANES), 1)
    hots = [idx[:, k:k + 1] == expert for k in range(TOP_K)]
    multi = jnp.zeros((tm, LANES), F32)
    for hot in hots:
        multi = jnp.where(hot, 1.0, multi)
    earlier = (lax.broadcasted_iota(jnp.int32, (tm, tm), 0)
               > lax.broadcasted_iota(jnp.int32, (tm, tm), 1)).astype(BF16)
    rank = jnp.dot(earlier, multi.astype(BF16), preferred_element_type=F32)
    pos = rank + soffr_ref[0]
    srows = [jnp.sum(jnp.where(hot, pos, 0.0), axis=-1, keepdims=True) for hot in hots]
    chunk = LANES
    row_id = lax.broadcasted_iota(jnp.int32, (chunk, rows), 1).astype(F32)
    for c in range(tm // chunk):
        sl = slice(c * chunk, (c + 1) * chunk)
        unsort = jnp.zeros((chunk, rows), F32)
        for k, srow in enumerate(srows):
            unsort = jnp.where(row_id == srow[sl], wt[sl, k:k + 1], unsort)
        if c == 0:
            _wait_granules(totg_ref[t], copy_into(slot))
        y = jnp.dot(unsort.astype(BF16), stage[slot], preferred_element_type=F32)
        o_ref[sl, :] = _rms(x1_ref[sl, :] + y, g_ref[...])


def _combine(x1, yb, top_idx, top_w, plan, g_final, tm):
    n, d = x1.shape
    nt = n // tm
    rows = _stage_rows(tm)
    soff_row = jnp.pad(plan["stage_off_f"], ((0, 0), (0, LANES - N_EXPERTS))).reshape(nt, 1, LANES)
    grid_spec = pltpu.PrefetchScalarGridSpec(
        num_scalar_prefetch=4,
        grid=(nt,),
        in_specs=[
            pl.BlockSpec((tm, d), lambda t, *_: (t, 0)),
            pl.BlockSpec((tm, LANES), lambda t, *_: (t, 0)),
            pl.BlockSpec((tm, LANES), lambda t, *_: (t, 0)),
            pl.BlockSpec((1, 1, LANES), lambda t, *_: (t, 0, 0)),
            pl.BlockSpec((1, d), lambda t, *_: (0, 0)),
            pl.BlockSpec(memory_space=pl.ANY),
        ],
        out_specs=pl.BlockSpec((tm, d), lambda t, *_: (t, 0)),
        scratch_shapes=[pltpu.VMEM((2, rows, d), BF16), pltpu.SemaphoreType.DMA((2,))],
    )
    return pl.pallas_call(
        _combine_kernel,
        grid_spec=grid_spec,
        out_shape=jax.ShapeDtypeStruct((n, d), F32),
        compiler_params=_cparams(("arbitrary",)),
    )(plan["stage_off"], plan["seg_start"], plan["ngran"], plan["tot_gran"], x1, top_idx, top_w, soff_row,
      g_final, yb)


def _rope_tables(seq):
    half = ATTN_HEAD_DIM // 2
    inv_freq = ROPE_THETA ** (-jnp.arange(0, half, 2, dtype=F32) / half)
    pos = jnp.arange(seq, dtype=jnp.int32)
    ang_r = (pos // GRID_W).astype(F32)[:, None] * inv_freq
    ang_c = (pos % GRID_W).astype(F32)[:, None] * inv_freq
    cos_t = jnp.concatenate([jnp.cos(ang_r)] * 2 + [jnp.cos(ang_c)] * 2, axis=-1)
    sin_t = jnp.concatenate([-jnp.sin(ang_r), jnp.sin(ang_r), -jnp.sin(ang_c), jnp.sin(ang_c)], axis=-1)
    return cos_t, sin_t


def _token_mixer(x2, batch, seq, g_mix, w_in, conv_w, conv_b, dt_bias_f, dt_bias_b, a_log_f, a_log_b, d_skip,
                 g_ssm, q_norm_g, k_norm_g, w_br_ssm, w_br_attn, w_out):
    n, d = x2.shape
    z_end = SSM_INNER
    xbc_end = z_end + CONV_CH
    dtf_end = xbc_end + SSM_HEADS
    dtb_end = dtf_end + SSM_HEADS
    q_end = dtb_end + ATTN_HEADS * ATTN_HEAD_DIM
    k_end = q_end + ATTN_KV_HEADS * ATTN_HEAD_DIM
    v_end = k_end + ATTN_KV_HEADS * ATTN_HEAD_DIM
    w_main = jnp.concatenate([w_in[:, :z_end], w_in[:, dtb_end:q_end], w_in[:, v_end:],
                              w_in[:, z_end:xbc_end], w_in[:, q_end:k_end], w_in[:, k_end:v_end]],
                             axis=1).astype(BF16)
    w_dt = jnp.pad(w_in[:, xbc_end:dtb_end], ((0, 0), (0, LANES - 2 * SSM_HEADS))).astype(BF16)

    proj, dt = _in_proj(x2, g_mix.reshape(1, d), w_main, w_dt)
    proj3 = proj.reshape(batch, seq, PROJ_COLS)

    xbc = _conv(proj3, conv_w, conv_b.reshape(1, CONV_CH))

    dt3 = dt.reshape(batch, seq, LANES)
    dtt3 = jnp.swapaxes(dt3[:, :, :2 * SSM_HEADS], 1, 2)
    bias = jnp.concatenate([dt_bias_f, dt_bias_b])
    alog = jnp.concatenate([a_log_f, a_log_b])
    pad_row = lambda v: jnp.pad(v, (0, LANES - 2 * SSM_HEADS)).reshape(1, LANES)
    y_f, y_b = _ssd(xbc, dt3, dtt3, pad_row(bias), bias.reshape(-1, 1), pad_row(alog), alog.reshape(-1, 1))

    cos_t, sin_t = _rope_tables(seq)
    q_rot, k_rot = _qk_prep(proj, cos_t, sin_t, q_norm_g.reshape(1, -1), k_norm_g.reshape(1, -1), seq)
    attn = _flash(q_rot.reshape(batch, seq, -1), k_rot.reshape(batch, seq, -1), proj3, q_norm_g, k_norm_g)

    return _merge(y_f.reshape(n, -1), y_b.reshape(n, -1), xbc.reshape(n, CONV_CH), proj, attn.reshape(n, -1),
                  x2, jnp.repeat(d_skip, SSM_HEAD_DIM).reshape(1, -1), g_ssm.reshape(1, -1),
                  w_br_ssm.astype(BF16), w_br_attn.astype(BF16), w_out.astype(BF16))


def _moe_and_final_norm(x1, g_ffn, w_router, b_router, w_mlp1, b_mlp1, w_mlp2, b_mlp2, g_final):
    n, d = x1.shape
    tm = min(MOE_TILE, n)
    w_r = jnp.pad(w_router, ((0, 0), (0, LANES - N_EXPERTS)))
    b_r = jnp.pad(b_router, (0, LANES - N_EXPERTS), constant_values=-jnp.inf).reshape(1, LANES)
    g_ffn_row = g_ffn.reshape(1, d)
    top_idx, top_w, cnt = _router(x1, g_ffn_row, w_r, b_r)
    plan = _routing_plan(cnt[:, 0, :N_EXPERTS].astype(jnp.int32), n, tm, MOE_BLOCK)
    idx_t = jnp.transpose(top_idx[:, :8])
    xs = _dispatch(x1, g_ffn_row, idx_t, plan, tm)
    yb = _experts(xs, plan, w_mlp1, b_mlp1[:, None, :], w_mlp2, b_mlp2[:, None, :])
    return _combine(x1, yb, top_idx, top_w, plan, g_final.reshape(1, d), tm)


def kernel(x, g_mix, w_in, conv_w, conv_b, dt_bias_f, dt_bias_b, a_log_f, a_log_b, d_skip, g_ssm, q_norm_g,
           k_norm_g, w_br_ssm, w_br_attn, w_out, g_ffn, w_router, b_router, w_mlp1, b_mlp1, w_mlp2, b_mlp2,
           g_final):
    batch, seq, d = x.shape
    assert g_mix.shape[0] == 1, "single-layer model: the final rmsnorm is fused into the MoE combine"
    x2 = x.reshape(batch * seq, d)
    x1 = _token_mixer(x2, batch, seq, g_mix[0], w_in[0], conv_w[0], conv_b[0], dt_bias_f[0], dt_bias_b[0],
                      a_log_f[0], a_log_b[0], d_skip[0], g_ssm[0], q_norm_g[0], k_norm_g[0], w_br_ssm[0],
                      w_br_attn[0], w_out[0])
    out = _moe_and_final_norm(x1, g_ffn[0], w_router[0], b_router[0], w_mlp1[0], b_mlp1[0], w_mlp2[0],
                              b_mlp2[0], g_final)
    return out.reshape(batch, seq, d)
```

```python
import functools
import math

import jax
import jax.numpy as jnp
from jax import lax
from jax.experimental import pallas as pl
from jax.experimental.pallas import tpu as pltpu

F32 = jnp.float32
BF16 = jnp.bfloat16

NORM_EPS = 1e-6
GRID_W = 64
SSM_HEADS = 16
SSM_HEAD_DIM = 64
SSM_INNER = SSM_HEADS * SSM_HEAD_DIM
SSM_GROUPS = 2
SSM_STATE = 128
SSM_CONV = 5
CONV_CH = SSM_INNER + 2 * SSM_GROUPS * SSM_STATE
ATTN_HEADS = 8
ATTN_KV_HEADS = 2
ATTN_HEAD_DIM = 128
ROPE_THETA = 10000.0
N_EXPERTS = 32
TOP_K = 4
SWIGLU_LIMIT = 7.0
SWIGLU_ALPHA = 1.702

LANES = 128
BF16_SUBLANES = 16
VMEM_LIMIT = 56 * 1024 * 1024

Z_OFF, Q_OFF, GATE_OFF, XBC_OFF = 0, 1024, 2048, 4096
K_OFF, V_OFF, PROJ_COLS = 5632, 5888, 6144

MOE_TILE = 512
MOE_BLOCK = 512
MOE_GRAN = BF16_SUBLANES


def _cparams(sem):
    return pltpu.CompilerParams(dimension_semantics=sem, vmem_limit_bytes=VMEM_LIMIT)


def _sigmoid(x):
    return 1.0 / (1.0 + jnp.exp(-x))


def _softplus(x):
    return jnp.maximum(x, 0.0) + jnp.log(1.0 + jnp.exp(-jnp.abs(x)))


def _rms(x, g):
    ms = jnp.mean(x * x, axis=-1, keepdims=True)
    return x * lax.rsqrt(ms + NORM_EPS) * g


def _inproj_kernel(x_ref, g_ref, w_ref, wdt_ref, o_ref, dt_ref, h_scr):
    @pl.when(pl.program_id(1) == 0)
    def _():
        hb = _rms(x_ref[...], g_ref[...]).astype(BF16)
        h_scr[...] = hb
        dt_ref[...] = jnp.dot(hb, wdt_ref[...], preferred_element_type=F32)

    o_ref[...] = jnp.dot(h_scr[...], w_ref[...], preferred_element_type=F32).astype(o_ref.dtype)


def _in_proj(x2, g_mix, w_main, w_dt):
    n, d = x2.shape
    tm = min(1024, n)
    tn = 2048
    return pl.pallas_call(
        _inproj_kernel,
        grid=(n // tm, PROJ_COLS // tn),
        in_specs=[
            pl.BlockSpec((tm, d), lambda i, j: (i, 0)),
            pl.BlockSpec((1, d), lambda i, j: (0, 0)),
            pl.BlockSpec((d, tn), lambda i, j: (0, j)),
            pl.BlockSpec((d, LANES), lambda i, j: (0, 0)),
        ],
        out_specs=[
            pl.BlockSpec((tm, tn), lambda i, j: (i, j)),
            pl.BlockSpec((tm, LANES), lambda i, j: (i, 0)),
        ],
        out_shape=[
            jax.ShapeDtypeStruct((n, PROJ_COLS), BF16),
            jax.ShapeDtypeStruct((n, LANES), F32),
        ],
        scratch_shapes=[pltpu.VMEM((tm, d), BF16)],
        compiler_params=_cparams(("arbitrary", "arbitrary")),
    )(x2, g_mix, w_main, w_dt)


def _conv_kernel(prev_ref, cur_ref, next_ref, w_ref, b_ref, o_ref, scr):
    s = pl.program_id(1)
    ts = cur_ref.shape[1]
    halo = BF16_SUBLANES
    pad = (SSM_CONV - 1) // 2
    prev = prev_ref[0].astype(F32)
    nxt = next_ref[0].astype(F32)
    scr[0:halo, :] = jnp.where(s == 0, 0.0, prev)
    scr[halo:halo + ts, :] = cur_ref[0].astype(F32)
    scr[halo + ts:2 * halo + ts, :] = jnp.where(s == pl.num_programs(1) - 1, 0.0, nxt)
    acc = jnp.zeros((ts, cur_ref.shape[2]), F32) + b_ref[...]
    for k in range(SSM_CONV):
        acc = acc + w_ref[k:k + 1, :] * scr[halo - pad + k:halo - pad + k + ts, :]
    o_ref[0] = (acc * _sigmoid(acc)).astype(o_ref.dtype)


def _conv(proj3, conv_w, conv_b):
    b, s, _ = proj3.shape
    ts = min(512, s)
    tc = 512
    halo = BF16_SUBLANES
    hb = ts // halo
    col0 = XBC_OFF // tc
    return pl.pallas_call(
        _conv_kernel,
        grid=(b, s // ts, CONV_CH // tc),
        in_specs=[
            pl.BlockSpec((1, halo, tc), lambda bi, si, ci: (bi, jnp.maximum(si * hb - 1, 0), col0 + ci)),
            pl.BlockSpec((1, ts, tc), lambda bi, si, ci: (bi, si, col0 + ci)),
            pl.BlockSpec((1, halo, tc),
                         lambda bi, si, ci: (bi, jnp.minimum((si + 1) * hb, s // halo - 1), col0 + ci)),
            pl.BlockSpec((SSM_CONV, tc), lambda bi, si, ci: (0, ci)),
            pl.BlockSpec((1, tc), lambda bi, si, ci: (0, ci)),
        ],
        out_specs=pl.BlockSpec((1, ts, tc), lambda bi, si, ci: (bi, si, ci)),
        out_shape=jax.ShapeDtypeStruct((b, s, CONV_CH), BF16),
        scratch_shapes=[pltpu.VMEM((ts + 2 * halo, tc), F32)],
        compiler_params=_cparams(("arbitrary", "arbitrary", "arbitrary")),
    )(proj3, proj3, proj3, conv_w, conv_b)


def _ssd_kernel(xf_ref, xb_ref, dtf_ref, dtb_ref, dttf_ref, dttb_ref, brow_ref, bcol_ref,
                arow_ref, acol_ref, yf_ref, yb_ref, st_ref):
    L = xf_ref.shape[1]
    hg = SSM_HEADS // SSM_GROUPS
    pairs = hg // 2

    @pl.when(pl.program_id(1) == 0)
    def _():
        st_ref[...] = jnp.zeros_like(st_ref)

    rows = lax.broadcasted_iota(jnp.int32, (L, L), 0)
    cols = lax.broadcasted_iota(jnp.int32, (L, L), 1)
    lower = rows >= cols
    upper = rows <= cols
    ltri = lower.astype(BF16)
    utri = upper.astype(BF16)
    lane = lax.broadcasted_iota(jnp.int32, (L, LANES), 1)
    first_half = lane < SSM_HEAD_DIM
    lane1 = lax.broadcasted_iota(jnp.int32, (1, LANES), 1)
    log2e = math.log2(math.e)
    a_row = -jnp.exp(arow_ref[...]) * log2e
    a_col = -jnp.exp(acol_ref[...]) * log2e

    def split3(v):
        hi = v.astype(BF16)
        r1 = v - hi.astype(F32)
        mid = r1.astype(BF16)
        return hi, mid, (r1 - mid.astype(F32)).astype(BF16)

    def cumsum_cols(tri, v):
        return jnp.dot(jnp.concatenate([tri] * 3, axis=1), jnp.concatenate(split3(v), axis=0),
                       preferred_element_type=F32)

    def cumsum_rows(v, tri):
        return jnp.dot(jnp.concatenate(split3(v), axis=1), jnp.concatenate([tri] * 3, axis=0),
                       preferred_element_type=F32)

    for d in range(2):
        x_ref, dt_ref, dtt_ref, y_ref = ((xf_ref, dtf_ref, dttf_ref, yf_ref) if d == 0
                                         else (xb_ref, dtb_ref, dttb_ref, yb_ref))
        a = _softplus(dt_ref[0] + brow_ref[...]) * a_row
        dt_t = _softplus(dtt_ref[0] + bcol_ref[...])
        a_t = dt_t * a_col
        if d == 0:
            cs_col = cumsum_cols(ltri, a)
            cs_row = cumsum_rows(a_t, utri)
            tot = cs_col[L - 1:L, :]
            tot_t = cs_row[:, L - 1:L]
            mask = lower
        else:
            cs_col = cumsum_cols(utri, a)
            cs_row = cumsum_rows(a_t, ltri)
            tot = cs_col[0:1, :]
            tot_t = cs_row[:, 0:1]
            mask = upper
        w_t = dt_t * jnp.exp2(tot_t - cs_row)
        src_t = cs_row - jnp.log2(dt_t)
        chunk_decay = jnp.exp2(tot)

        for g in range(SSM_GROUPS):
            boff = SSM_INNER + g * SSM_STATE
            coff = SSM_INNER + SSM_GROUPS * SSM_STATE + g * SSM_STATE
            bm = x_ref[0, :, boff:boff + SSM_STATE]
            cm = x_ref[0, :, coff:coff + SSM_STATE]
            cb = lax.dot_general(cm, bm, (((1,), (1,)), ((), ())), preferred_element_type=F32)
            bt = bm.astype(F32).T
            st = st_ref[d, g]
            y_off = jnp.dot(cm, st.astype(BF16), preferred_element_type=F32)
            for pr in range(pairs):
                h0 = d * SSM_HEADS + g * hg + 2 * pr
                xoff = (g * pairs + pr) * LANES
                xs = x_ref[0, :, xoff:xoff + LANES]
                zero = jnp.zeros_like(xs)
                rhs = jnp.concatenate([jnp.where(first_half, xs, zero),
                                       jnp.where(first_half, zero, xs)], axis=0)
                ms, ws, dins = [], [], []
                for hh in (h0, h0 + 1):
                    cs_b = jnp.broadcast_to(cs_col[:, hh:hh + 1], (L, L))
                    seg = cs_b - src_t[hh:hh + 1, :]
                    m = cb * jnp.exp2(jnp.where(mask, seg, -jnp.inf))
                    ms.append(m.astype(BF16))
                    ws.append((bt * w_t[hh:hh + 1, :]).astype(BF16))
                    dins.append(jnp.exp2(cs_b))
                y = jnp.dot(jnp.concatenate(ms, axis=1), rhs, preferred_element_type=F32)
                y = y + y_off[:, pr * LANES:(pr + 1) * LANES] * jnp.where(first_half, dins[0], dins[1])
                y_ref[0, :, xoff:xoff + LANES] = y.astype(y_ref.dtype)
                new_st = jnp.dot(jnp.concatenate(ws, axis=1), rhs, preferred_element_type=F32)
                cd = jnp.where(lane1 < SSM_HEAD_DIM, chunk_decay[:, h0:h0 + 1], chunk_decay[:, h0 + 1:h0 + 2])
                st_ref[d, g, :, pr * LANES:(pr + 1) * LANES] = st[:, pr * LANES:(pr + 1) * LANES] * cd + new_st


def _ssd(xbc, dt3, dtt3, bias_row, bias_col, alog_row, alog_col):
    b, s, _ = xbc.shape
    L = min(128, s)
    nc = s // L
    hg = SSM_HEADS // SSM_GROUPS
    fwd = lambda bi, ci: (bi, ci, 0)
    bwd = lambda bi, ci: (bi, nc - 1 - ci, 0)
    fwd_t = lambda bi, ci: (bi, 0, ci)
    bwd_t = lambda bi, ci: (bi, 0, nc - 1 - ci)
    const = lambda bi, ci: (0, 0)
    return pl.pallas_call(
        _ssd_kernel,
        grid=(b, nc),
        in_specs=[
            pl.BlockSpec((1, L, CONV_CH), fwd),
            pl.BlockSpec((1, L, CONV_CH), bwd),
            pl.BlockSpec((1, L, LANES), fwd),
            pl.BlockSpec((1, L, LANES), bwd),
            pl.BlockSpec((1, 2 * SSM_HEADS, L), fwd_t),
            pl.BlockSpec((1, 2 * SSM_HEADS, L), bwd_t),
            pl.BlockSpec((1, LANES), const),
            pl.BlockSpec((2 * SSM_HEADS, 1), const),
            pl.BlockSpec((1, LANES), const),
            pl.BlockSpec((2 * SSM_HEADS, 1), const),
        ],
        out_specs=[
            pl.BlockSpec((1, L, SSM_INNER), fwd),
            pl.BlockSpec((1, L, SSM_INNER), bwd),
        ],
        out_shape=[jax.ShapeDtypeStruct((b, s, SSM_INNER), BF16)] * 2,
        scratch_shapes=[pltpu.VMEM((2, SSM_GROUPS, SSM_STATE, hg * SSM_HEAD_DIM), F32)],
        compiler_params=_cparams(("arbitrary", "arbitrary")),
    )(xbc, xbc, dt3, dt3, dtt3, dtt3, bias_row, bias_col, alog_row, alog_col)


def _rope_norm(t, g, cos, sin_signed, lane):
    tn = _rms(t, g)
    swapped = jnp.where((lane & 32) == 0, pltpu.roll(tn, 96, 1), pltpu.roll(tn, 32, 1))
    return tn * cos + swapped * sin_signed


Q_SCALE = ATTN_HEAD_DIM ** -0.5 * math.log2(math.e)


def _qkprep_kernel(q_ref, k_ref, cos_ref, sin_ref, qg_ref, kg_ref, qo_ref, ko_ref):
    cos = cos_ref[...]
    sin = sin_ref[...]
    lane = lax.broadcasted_iota(jnp.int32, cos.shape, 1)

    def heads(src_ref, g_ref, dst_ref, n_heads, scale):
        for h in range(n_heads):
            sl = slice(h * ATTN_HEAD_DIM, (h + 1) * ATTN_HEAD_DIM)
            r = _rope_norm(src_ref[:, sl].astype(F32), g_ref[...], cos, sin, lane) * scale
            dst_ref[:, sl] = r.astype(dst_ref.dtype)

    heads(q_ref, qg_ref, qo_ref, ATTN_HEADS, Q_SCALE)
    heads(k_ref, kg_ref, ko_ref, ATTN_KV_HEADS, 1.0)


def _qk_prep(proj, cos_t, sin_t, q_norm_g, k_norm_g, seq):
    n = proj.shape[0]
    tm = min(512, seq)
    qw = ATTN_HEADS * ATTN_HEAD_DIM
    kw = ATTN_KV_HEADS * ATTN_HEAD_DIM
    spt = seq // tm
    return pl.pallas_call(
        _qkprep_kernel,
        grid=(n // tm,),
        in_specs=[
            pl.BlockSpec((tm, qw), lambda i: (i, Q_OFF // qw)),
            pl.BlockSpec((tm, kw), lambda i: (i, K_OFF // kw)),
            pl.BlockSpec((tm, ATTN_HEAD_DIM), lambda i: (i % spt, 0)),
            pl.BlockSpec((tm, ATTN_HEAD_DIM), lambda i: (i % spt, 0)),
            pl.BlockSpec((1, ATTN_HEAD_DIM), lambda i: (0, 0)),
            pl.BlockSpec((1, ATTN_HEAD_DIM), lambda i: (0, 0)),
        ],
        out_specs=[
            pl.BlockSpec((tm, qw), lambda i: (i, 0)),
            pl.BlockSpec((tm, kw), lambda i: (i, 0)),
        ],
        out_shape=[jax.ShapeDtypeStruct((n, qw), BF16), jax.ShapeDtypeStruct((n, kw), BF16)],
        compiler_params=_cparams(("arbitrary",)),
    )(proj, proj, cos_t, sin_t, q_norm_g, k_norm_g)


def _flash_kernel(small_ref, q_ref, k_ref, v_ref, o_ref, *, tk):
    q = q_ref[0]
    tq = q.shape[0]
    nk = k_ref.shape[1] // tk
    nt = (((1,), (1,)), ((), ()))
    small = small_ref[0] != 0

    def kv(i):
        off = pl.multiple_of(i * tk, tk)
        return k_ref[0, pl.ds(off, tk), :], v_ref[0, pl.ds(off, tk), :]

    @pl.when(small)
    def _():
        def body(i, carry):
            l, acc = carry
            k, v = kv(i)
            p = jnp.exp2(lax.dot_general(q, k, nt, preferred_element_type=F32))
            for c in range(tk // LANES):
                l = l + p[:, c * LANES:(c + 1) * LANES]
            acc = acc + jnp.dot(p.astype(BF16), v, preferred_element_type=F32)
            return l, acc

        init = (jnp.zeros((tq, LANES), F32), jnp.zeros((tq, ATTN_HEAD_DIM), F32))
        l, acc = lax.fori_loop(0, nk, body, init)
        o_ref[0] = (acc / jnp.sum(l, axis=-1, keepdims=True)).astype(o_ref.dtype)

    @pl.when(jnp.logical_not(small))
    def _():
        def body(i, carry):
            m, l, acc = carry
            k, v = kv(i)
            s = lax.dot_general(q, k, nt, preferred_element_type=F32)
            m_new = jnp.maximum(m, jnp.max(s, axis=-1, keepdims=True))
            alpha = jnp.exp2(m - m_new)
            p = jnp.exp2(s - m_new)
            l = alpha * l + jnp.sum(p, axis=-1, keepdims=True)
            acc = alpha * acc + jnp.dot(p.astype(BF16), v, preferred_element_type=F32)
            return m_new, l, acc

        init = (jnp.full((tq, 1), -jnp.inf, F32), jnp.zeros((tq, 1), F32),
                jnp.zeros((tq, ATTN_HEAD_DIM), F32))
        _, l, acc = lax.fori_loop(0, nk, body, init)
        o_ref[0] = (acc / l).astype(o_ref.dtype)


SCORE_BOUND = 59.0


def _flash(q3, k3, proj3, q_norm_g, k_norm_g):
    b, s, _ = q3.shape
    tq = min(512, s)
    tk = min(8192, s)
    nq = s // tq
    grp = ATTN_HEADS // ATTN_KV_HEADS
    hd = ATTN_HEAD_DIM
    bound = hd * Q_SCALE * jnp.max(jnp.abs(q_norm_g)) * jnp.max(jnp.abs(k_norm_g)) * 1.02
    small = (bound <= SCORE_BOUND).astype(jnp.int32).reshape(1)
    grid_spec = pltpu.PrefetchScalarGridSpec(
        num_scalar_prefetch=1,
        grid=(b, ATTN_HEADS, nq),
        in_specs=[
            pl.BlockSpec((1, tq, hd), lambda bi, h, qi, sm: (bi, qi, h)),
            pl.BlockSpec((1, s, hd), lambda bi, h, qi, sm: (bi, 0, h // grp)),
            pl.BlockSpec((1, s, hd), lambda bi, h, qi, sm: (bi, 0, V_OFF // hd + h // grp)),
        ],
        out_specs=pl.BlockSpec((1, tq, hd), lambda bi, h, qi, sm: (bi, qi, h)),
    )
    return pl.pallas_call(
        functools.partial(_flash_kernel, tk=tk),
        grid_spec=grid_spec,
        out_shape=jax.ShapeDtypeStruct((b, s, ATTN_HEADS * hd), BF16),
        compiler_params=_cparams(("arbitrary", "arbitrary", "arbitrary")),
    )(small, q3, k3, proj3)


def _merge_kernel(yf_ref, yb_ref, xs_ref, z_ref, gate_ref, attn_ref, x_ref, dskip_ref, gssm_ref,
                  wbs_ref, wba_ref, wo_ref, x1_ref):
    d = x_ref.shape[1]
    xs = xs_ref[...].astype(F32)
    y = yf_ref[...].astype(F32) + yb_ref[...].astype(F32) + xs * dskip_ref[...]
    z = z_ref[...].astype(F32)
    y = _rms(y * (z * _sigmoid(z)), gssm_ref[...])
    br_ssm = jnp.dot(y.astype(BF16), wbs_ref[...], preferred_element_type=F32)
    br_attn = jnp.dot(attn_ref[...], wba_ref[...], preferred_element_type=F32)
    g_s = _sigmoid(gate_ref[:, :d].astype(F32))
    g_a = _sigmoid(gate_ref[:, d:].astype(F32))
    merged = (g_s * br_ssm + g_a * br_attn).astype(BF16)
    x1_ref[...] = x_ref[...] + jnp.dot(merged, wo_ref[...], preferred_element_type=F32)


def _merge(y_f, y_b, xbc, proj, attn, x2, dskip_row, g_ssm, w_br_ssm, w_br_attn, w_out):
    n, d = x2.shape
    tm = min(256, n)
    row = lambda i: (i, 0)
    const = lambda i: (0, 0)
    return pl.pallas_call(
        _merge_kernel,
        grid=(n // tm,),
        in_specs=[
            pl.BlockSpec((tm, d), row),
            pl.BlockSpec((tm, d), row),
            pl.BlockSpec((tm, d), row),
            pl.BlockSpec((tm, d), lambda i: (i, Z_OFF // d)),
            pl.BlockSpec((tm, 2 * d), lambda i: (i, GATE_OFF // (2 * d))),
            pl.BlockSpec((tm, d), row),
            pl.BlockSpec((tm, d), row),
            pl.BlockSpec((1, d), const),
            pl.BlockSpec((1, d), const),
            pl.BlockSpec((d, d), const),
            pl.BlockSpec((d, d), const),
            pl.BlockSpec((d, d), const),
        ],
        out_specs=pl.BlockSpec((tm, d), row),
        out_shape=jax.ShapeDtypeStruct((n, d), F32),
        compiler_params=_cparams(("arbitrary",)),
    )(y_f, y_b, xbc, proj, proj, attn, x2, dskip_row, g_ssm, w_br_ssm, w_br_attn, w_out)


def _router_kernel(x1_ref, g_ref, w_ref, b_ref, idx_ref, wt_ref, cnt_ref):
    h = _rms(x1_ref[...], g_ref[...])
    h_hi = h.astype(BF16)
    h_lo = (h - h_hi.astype(F32)).astype(BF16)
    logits = jnp.dot(jnp.concatenate([h_hi, h_hi, h_lo], axis=1), w_ref[...],
                     preferred_element_type=F32) + b_ref[...]
    lane = lax.broadcasted_iota(jnp.int32, logits.shape, 1)
    idx_out = jnp.zeros(logits.shape, jnp.int32)
    val_out = jnp.zeros(logits.shape, F32)
    chosen = jnp.zeros(logits.shape, F32)
    vals = []
    for k in range(TOP_K):
        m = jnp.max(logits, axis=-1, keepdims=True)
        idx = jnp.min(jnp.where(logits == m, lane, LANES), axis=-1, keepdims=True)
        idx_out = jnp.where(lane == k, idx, idx_out)
        vals.append(m)
        hit = lane == idx
        chosen = jnp.where(hit, 1.0, chosen)
        logits = jnp.where(hit, -jnp.inf, logits)
    es = [jnp.exp(v - vals[0]) for v in vals]
    tot = es[0] + es[1] + es[2] + es[3]
    for k in range(TOP_K):
        val_out = jnp.where(lane == k, es[k] / tot, val_out)
    idx_ref[...] = idx_out
    wt_ref[...] = val_out
    cnt_ref[0] = jnp.broadcast_to(jnp.sum(chosen, axis=0, keepdims=True), cnt_ref.shape[1:])


def _router(x1, g_ffn, w_router_pad, b_router_pad):
    n, d = x1.shape
    tm = min(MOE_TILE, n)
    return pl.pallas_call(
        _router_kernel,
        grid=(n // tm,),
        in_specs=[
            pl.BlockSpec((tm, d), lambda i: (i, 0)),
            pl.BlockSpec((1, d), lambda i: (0, 0)),
            pl.BlockSpec((3 * d, LANES), lambda i: (0, 0)),
            pl.BlockSpec((1, LANES), lambda i: (0, 0)),
        ],
        out_specs=[pl.BlockSpec((tm, LANES), lambda i: (i, 0)),
                   pl.BlockSpec((tm, LANES), lambda i: (i, 0)),
                   pl.BlockSpec((1, 8, LANES), lambda i: (i, 0, 0))],
        out_shape=[jax.ShapeDtypeStruct((n, LANES), jnp.int32),
                   jax.ShapeDtypeStruct((n, LANES), F32),
                   jax.ShapeDtypeStruct((n // tm, 8, LANES), F32)],
        compiler_params=_cparams(("arbitrary",)),
    )(x1, g_ffn, w_router_pad, b_router_pad)


def _routing_plan(cnt, n, tm, tb):
    nt = n // tm
    gran = MOE_GRAN
    rc = (cnt + gran - 1) // gran * gran
    covered = jnp.sum(rc, axis=0)
    region = (covered + tb - 1) // tb * tb
    pad_end = jnp.cumsum(region)
    pad_start = pad_end - region
    seg_start = pad_start[None, :] + jnp.cumsum(rc, axis=0) - rc
    stage_off = jnp.cumsum(rc, axis=1) - rc
    n_used = pad_end[-1] // tb
    nb = (TOP_K * n + nt * N_EXPERTS * (gran - 1) + N_EXPERTS * (tb - 1) + tb - 1) // tb
    blk = jnp.arange(nb, dtype=jnp.int32)
    block_e = jnp.sum((pad_end[None, :] <= (jnp.minimum(blk, n_used - 1) * tb)[:, None]).astype(jnp.int32), axis=1)
    block_e = jnp.minimum(block_e, N_EXPERTS - 1)
    flat = lambda a: a.reshape(-1).astype(jnp.int32)
    tail_gran = (region - covered) // gran
    ngran = rc // gran
    g_end = jnp.cumsum(ngran, axis=1)
    gi = jnp.arange(_stage_rows(tm) // gran, dtype=jnp.int32)
    g_exp = jnp.minimum(jnp.sum((g_end[:, None, :] <= gi[None, :, None]).astype(jnp.int32), axis=2), N_EXPERTS - 1)
    gran_dst = (jnp.take_along_axis(seg_start, g_exp, axis=1)
                + gran * (gi[None, :] - jnp.take_along_axis(g_end - ngran, g_exp, axis=1)))
    return dict(gran_dst=flat(gran_dst), tot_gran=flat(g_end[:, -1]), block_e=flat(block_e),
                n_used=flat(n_used), nb=nb, stage_off_f=stage_off.astype(F32),
                tail_start=flat(pad_start + covered), tail_gran=flat(tail_gran),
                tot_tail=flat(jnp.sum(tail_gran)))


def _stage_rows(tm):
    rows = TOP_K * tm + N_EXPERTS * (MOE_GRAN - 1)
    return (rows + 2 * LANES - 1) // (2 * LANES) * (2 * LANES)


GRAN_UNROLL = 4


def _granule_copies(t, gdst_ref, totg_ref, make_copy, start):
    total = totg_ref[t]
    base = t * (gdst_ref.shape[0] // totg_ref.shape[0])
    full = lax.shift_right_logical(total, GRAN_UNROLL.bit_length() - 1)

    def one(i):
        make_copy(pl.multiple_of(i * MOE_GRAN, MOE_GRAN), pl.multiple_of(gdst_ref[base + i], MOE_GRAN),
                  MOE_GRAN).start()

    def group(q, c):
        if start:
            for u in range(GRAN_UNROLL):
                one(q * GRAN_UNROLL + u)
        else:
            make_copy(0, 0, GRAN_UNROLL * MOE_GRAN).wait()
        return c

    def rest(i, c):
        if start:
            one(i)
        else:
            make_copy(0, 0, MOE_GRAN).wait()
        return c

    lax.fori_loop(0, full, group, 0)
    lax.fori_loop(full * GRAN_UNROLL, total, rest, 0)


def _dispatch_kernel(gdst_ref, totg_ref, tstart_ref, tgran_ref, misc_ref,
                     x1_ref, g_ref, idxt_ref, soffc_ref, xs_hbm, stage, zbuf, sem, zsem, *, n_blocks):
    t = pl.program_id(0)
    slot = t % 2
    tm = x1_ref.shape[0]
    rows = stage.shape[1]
    h = _rms(x1_ref[...], g_ref[...]).astype(BF16)
    idxt = idxt_ref[...]
    expert = lax.broadcasted_iota(jnp.int32, (LANES, tm), 0)
    hots = [idxt[k:k + 1, :] == expert for k in range(TOP_K)]
    multi = jnp.zeros((LANES, tm), F32)
    for hot in hots:
        multi = jnp.where(hot, 1.0, multi)
    earlier = (lax.broadcasted_iota(jnp.int32, (tm, tm), 0)
               < lax.broadcasted_iota(jnp.int32, (tm, tm), 1)).astype(BF16)
    rank = jnp.dot(multi.astype(BF16), earlier, preferred_element_type=F32)
    pos = rank + soffc_ref[0]
    srows = [jnp.sum(jnp.where(hot, pos, 0.0), axis=0, keepdims=True) for hot in hots]
    chunk = 2 * LANES
    row_id = lax.broadcasted_iota(jnp.int32, (chunk, tm), 0).astype(F32)
    for c in range(rows // chunk):
        perm = jnp.zeros((chunk, tm), F32)
        for srow in srows:
            perm = jnp.where(row_id == srow - float(c * chunk), 1.0, perm)
        stage[slot, c * chunk:(c + 1) * chunk, :] = jnp.dot(
            perm.astype(BF16), h, preferred_element_type=F32).astype(stage.dtype)

    def copy_from(s):
        def copy(stage_row, sorted_row, nrows):
            return pltpu.make_async_copy(stage.at[s, pl.ds(stage_row, nrows)],
                                         xs_hbm.at[pl.ds(sorted_row, nrows)], sem.at[s])
        return copy

    _granule_copies(t, gdst_ref, totg_ref, copy_from(slot), start=True)

    @pl.when(t > 0)
    def _():
        _granule_copies(jnp.maximum(t - 1, 0), gdst_ref, totg_ref, copy_from(1 - slot), start=False)

    @pl.when(t == pl.num_programs(0) - 1)
    def _():
        _granule_copies(t, gdst_ref, totg_ref, copy_from(slot), start=False)
        tb = zbuf.shape[0]
        n_used = misc_ref[0]
        zbuf[...] = jnp.zeros_like(zbuf)

        def zero_gran(row):
            return pltpu.make_async_copy(zbuf.at[pl.ds(0, MOE_GRAN)], xs_hbm.at[pl.ds(row, MOE_GRAN)], zsem)

        def zero_block(blk):
            return pltpu.make_async_copy(zbuf, xs_hbm.at[pl.ds(pl.multiple_of(blk * tb, tb), tb)], zsem)

        def tails(e, c):
            def one(g, c2):
                zero_gran(pl.multiple_of(tstart_ref[e] + g * MOE_GRAN, MOE_GRAN)).start()
                return c2
            lax.fori_loop(0, tgran_ref[e], one, 0)
            return c

        def start_block(blk, c):
            zero_block(blk).start()
            return c

        def wait_gran(g, c):
            zero_gran(0).wait()
            return c

        def wait_block(blk, c):
            zero_block(0).wait()
            return c

        lax.fori_loop(0, N_EXPERTS, tails, 0)
        lax.fori_loop(n_used, n_blocks, start_block, 0)
        lax.fori_loop(0, misc_ref[1], wait_gran, 0)
        lax.fori_loop(n_used, n_blocks, wait_block, 0)


def _dispatch(x1, g_ffn, idx_t, plan, tm):
    n, d = x1.shape
    nt = n // tm
    rows = _stage_rows(tm)
    soff_col = plan["stage_off_f"].reshape(nt, N_EXPERTS, 1)
    soff_col = jnp.pad(soff_col, ((0, 0), (0, LANES - N_EXPERTS), (0, 0)))
    misc = jnp.concatenate([plan["n_used"], plan["tot_tail"]])
    grid_spec = pltpu.PrefetchScalarGridSpec(
        num_scalar_prefetch=5,
        grid=(nt,),
        in_specs=[
            pl.BlockSpec((tm, d), lambda t, *_: (t, 0)),
            pl.BlockSpec((1, d), lambda t, *_: (0, 0)),
            pl.BlockSpec((8, tm), lambda t, *_: (0, t)),
            pl.BlockSpec((1, LANES, 1), lambda t, *_: (t, 0, 0)),
        ],
        out_specs=pl.BlockSpec(memory_space=pl.ANY),
        scratch_shapes=[pltpu.VMEM((2, rows, d), BF16), pltpu.VMEM((MOE_BLOCK, d), BF16),
                        pltpu.SemaphoreType.DMA((2,)), pltpu.SemaphoreType.DMA(())],
    )
    return pl.pallas_call(
        functools.partial(_dispatch_kernel, n_blocks=plan["nb"]),
        grid_spec=grid_spec,
        out_shape=jax.ShapeDtypeStruct((plan["nb"] * MOE_BLOCK, d), BF16),
        compiler_params=_cparams(("arbitrary",)),
    )(plan["gran_dst"], plan["tot_gran"], plan["tail_start"], plan["tail_gran"], misc, x1, g_ffn, idx_t,
      soff_col)


def _expert_kernel(be_ref, nused_ref, x_ref, w1_ref, b1_ref, w2_ref, b2_ref, o_ref, w1b, w2b):
    de = w2_ref.shape[1]
    j = pl.program_id(0)
    live = j < nused_ref[0]
    new_expert = jnp.logical_or(j == 0, be_ref[j] != be_ref[jnp.maximum(j - 1, 0)])

    @pl.when(jnp.logical_and(live, new_expert))
    def _():
        w1b[...] = w1_ref[0].astype(BF16)
        w2b[...] = w2_ref[0].astype(BF16)

    @pl.when(live)
    def _():
        hm = jnp.dot(x_ref[...], w1b[...], preferred_element_type=F32) + b1_ref[0]
        gate = jnp.minimum(hm[:, :de], SWIGLU_LIMIT)
        up = jnp.clip(hm[:, de:], -SWIGLU_LIMIT, SWIGLU_LIMIT)
        act = gate * _sigmoid(SWIGLU_ALPHA * gate) * (up + 1.0)
        y = jnp.dot(act.astype(BF16), w2b[...], preferred_element_type=F32) + b2_ref[0]
        o_ref[...] = y.astype(o_ref.dtype)

    @pl.when(jnp.logical_not(live))
    def _():
        o_ref[...] = jnp.zeros_like(o_ref)


def _experts(xs, plan, w1, b1, w2, b2):
    d = xs.shape[1]
    tb = MOE_BLOCK
    f2 = w1.shape[2]
    de = w2.shape[1]
    last = lambda j, nu: jnp.maximum(jnp.minimum(j, nu[0] - 1), 0)
    grid_spec = pltpu.PrefetchScalarGridSpec(
        num_scalar_prefetch=2,
        grid=(plan["nb"],),
        in_specs=[
            pl.BlockSpec((tb, d), lambda j, be, nu: (last(j, nu), 0)),
            pl.BlockSpec((1, d, f2), lambda j, be, nu: (be[j], 0, 0)),
            pl.BlockSpec((1, 1, f2), lambda j, be, nu: (be[j], 0, 0)),
            pl.BlockSpec((1, de, d), lambda j, be, nu: (be[j], 0, 0)),
            pl.BlockSpec((1, 1, d), lambda j, be, nu: (be[j], 0, 0)),
        ],
        out_specs=pl.BlockSpec((tb, d), lambda j, be, nu: (j, 0)),
        scratch_shapes=[pltpu.VMEM((d, f2), BF16), pltpu.VMEM((de, d), BF16)],
    )
    return pl.pallas_call(
        _expert_kernel,
        grid_spec=grid_spec,
        out_shape=jax.ShapeDtypeStruct(xs.shape, BF16),
        compiler_params=_cparams(("arbitrary",)),
    )(plan["block_e"], plan["n_used"], xs, w1, b1, w2, b2)


def _combine_kernel(gdst_ref, totg_ref, x1_ref, idx_ref, wt_ref, soffr_ref, g_ref, yb_hbm, o_ref, stage, sem):
    t = pl.program_id(0)
    slot = t % 2
    tm = x1_ref.shape[0]
    rows = stage.shape[1]

    def copy_into(s):
        def copy(stage_row, sorted_row, nrows):
            return pltpu.make_async_copy(yb_hbm.at[pl.ds(sorted_row, nrows)],
                                         stage.at[s, pl.ds(stage_row, nrows)], sem.at[s])
        return copy

    @pl.when(t == 0)
    def _():
        stage[...] = jnp.zeros_like(stage)
        _granule_copies(t, gdst_ref, totg_ref, copy_into(slot), start=True)

    @pl.when(t + 1 < pl.num_programs(0))
    def _():
        _granule_copies(t + 1, gdst_ref, totg_ref, copy_into(1 - slot), start=True)

    idx = idx_ref[...]
    wt = wt_ref[...]
    expert = lax.broadcasted_iota(jnp.int32, (tm, LANES), 1)
    hots = [idx[:, k:k + 1] == expert for k in range(TOP_K)]
    multi = jnp.zeros((tm, LANES), F32)
    for hot in hots:
        multi = jnp.where(hot, 1.0, multi)
    earlier = (lax.broadcasted_iota(jnp.int32, (tm, tm), 0)
               > lax.broadcasted_iota(jnp.int32, (tm, tm), 1)).astype(BF16)
    rank = jnp.dot(earlier, multi.astype(BF16), preferred_element_type=F32)
    pos = rank + soffr_ref[0]
    srows = [jnp.sum(jnp.where(hot, pos, 0.0), axis=-1, keepdims=True) for hot in hots]
    chunk = LANES
    row_id = lax.broadcasted_iota(jnp.int32, (chunk, rows), 1).astype(F32)
    for c in range(tm // chunk):
        sl = slice(c * chunk, (c + 1) * chunk)
        unsort = jnp.zeros((chunk, rows), F32)
        for k, srow in enumerate(srows):
            unsort = jnp.where(row_id == srow[sl], wt[sl, k:k + 1], unsort)
        if c == 0:
            _granule_copies(t, gdst_ref, totg_ref, copy_into(slot), start=False)
        y = jnp.dot(unsort.astype(BF16), stage[slot], preferred_element_type=F32)
        o_ref[sl, :] = _rms(x1_ref[sl, :] + y, g_ref[...])


def _combine(x1, yb, top_idx, top_w, plan, g_final, tm):
    n, d = x1.shape
    nt = n // tm
    rows = _stage_rows(tm)
    soff_row = jnp.pad(plan["stage_off_f"], ((0, 0), (0, LANES - N_EXPERTS))).reshape(nt, 1, LANES)
    grid_spec = pltpu.PrefetchScalarGridSpec(
        num_scalar_prefetch=2,
        grid=(nt,),
        in_specs=[
            pl.BlockSpec((tm, d), lambda t, *_: (t, 0)),
            pl.BlockSpec((tm, LANES), lambda t, *_: (t, 0)),
            pl.BlockSpec((tm, LANES), lambda t, *_: (t, 0)),
            pl.BlockSpec((1, 1, LANES), lambda t, *_: (t, 0, 0)),
            pl.BlockSpec((1, d), lambda t, *_: (0, 0)),
            pl.BlockSpec(memory_space=pl.ANY),
        ],
        out_specs=pl.BlockSpec((tm, d), lambda t, *_: (t, 0)),
        scratch_shapes=[pltpu.VMEM((2, rows, d), BF16), pltpu.SemaphoreType.DMA((2,))],
    )
    return pl.pallas_call(
        _combine_kernel,
        grid_spec=grid_spec,
        out_shape=jax.ShapeDtypeStruct((n, d), F32),
        compiler_params=_cparams(("arbitrary",)),
    )(plan["gran_dst"], plan["tot_gran"], x1, top_idx, top_w, soff_row, g_final, yb)


def _rope_tables(seq):
    half = ATTN_HEAD_DIM // 2
    inv_freq = ROPE_THETA ** (-jnp.arange(0, half, 2, dtype=F32) / half)
    pos = jnp.arange(seq, dtype=jnp.int32)
    ang_r = (pos // GRID_W).astype(F32)[:, None] * inv_freq
    ang_c = (pos % GRID_W).astype(F32)[:, None] * inv_freq
    cos_t = jnp.concatenate([jnp.cos(ang_r)] * 2 + [jnp.cos(ang_c)] * 2, axis=-1)
    sin_t = jnp.concatenate([-jnp.sin(ang_r), jnp.sin(ang_r), -jnp.sin(ang_c), jnp.sin(ang_c)], axis=-1)
    return cos_t, sin_t


def _token_mixer(x2, batch, seq, g_mix, w_in, conv_w, conv_b, dt_bias_f, dt_bias_b, a_log_f, a_log_b, d_skip,
                 g_ssm, q_norm_g, k_norm_g, w_br_ssm, w_br_attn, w_out):
    n, d = x2.shape
    z_end = SSM_INNER
    xbc_end = z_end + CONV_CH
    dtf_end = xbc_end + SSM_HEADS
    dtb_end = dtf_end + SSM_HEADS
    q_end = dtb_end + ATTN_HEADS * ATTN_HEAD_DIM
    k_end = q_end + ATTN_KV_HEADS * ATTN_HEAD_DIM
    v_end = k_end + ATTN_KV_HEADS * ATTN_HEAD_DIM
    w_main = jnp.concatenate([w_in[:, :z_end], w_in[:, dtb_end:q_end], w_in[:, v_end:],
                              w_in[:, z_end:xbc_end], w_in[:, q_end:k_end], w_in[:, k_end:v_end]],
                             axis=1).astype(BF16)
    w_dt = jnp.pad(w_in[:, xbc_end:dtb_end], ((0, 0), (0, LANES - 2 * SSM_HEADS))).astype(BF16)

    proj, dt = _in_proj(x2, g_mix.reshape(1, d), w_main, w_dt)
    proj3 = proj.reshape(batch, seq, PROJ_COLS)

    xbc = _conv(proj3, conv_w, conv_b.reshape(1, CONV_CH))

    dt3 = dt.reshape(batch, seq, LANES)
    dtt3 = jnp.swapaxes(dt3[:, :, :2 * SSM_HEADS], 1, 2)
    bias = jnp.concatenate([dt_bias_f, dt_bias_b])
    alog = jnp.concatenate([a_log_f, a_log_b])
    pad_row = lambda v: jnp.pad(v, (0, LANES - 2 * SSM_HEADS)).reshape(1, LANES)
    y_f, y_b = _ssd(xbc, dt3, dtt3, pad_row(bias), bias.reshape(-1, 1), pad_row(alog), alog.reshape(-1, 1))

    cos_t, sin_t = _rope_tables(seq)
    q_rot, k_rot = _qk_prep(proj, cos_t, sin_t, q_norm_g.reshape(1, -1), k_norm_g.reshape(1, -1), seq)
    attn = _flash(q_rot.reshape(batch, seq, -1), k_rot.reshape(batch, seq, -1), proj3, q_norm_g, k_norm_g)

    return _merge(y_f.reshape(n, -1), y_b.reshape(n, -1), xbc.reshape(n, CONV_CH), proj, attn.reshape(n, -1),
                  x2, jnp.repeat(d_skip, SSM_HEAD_DIM).reshape(1, -1), g_ssm.reshape(1, -1),
                  w_br_ssm.astype(BF16), w_br_attn.astype(BF16), w_out.astype(BF16))


def _moe_and_final_norm(x1, g_ffn, w_router, b_router, w_mlp1, b_mlp1, w_mlp2, b_mlp2, g_final):
    n, d = x1.shape
    tm = min(MOE_TILE, n)
    w_r = jnp.pad(w_router, ((0, 0), (0, LANES - N_EXPERTS)))
    w_hi = w_r.astype(BF16)
    w_lo = (w_r - w_hi.astype(F32)).astype(BF16)
    w_r = jnp.concatenate([w_hi, w_lo, w_hi], axis=0)
    b_r = jnp.pad(b_router, (0, LANES - N_EXPERTS), constant_values=-jnp.inf).reshape(1, LANES)
    g_ffn_row = g_ffn.reshape(1, d)
    top_idx, top_w, cnt = _router(x1, g_ffn_row, w_r, b_r)
    plan = _routing_plan(cnt[:, 0, :N_EXPERTS].astype(jnp.int32), n, tm, MOE_BLOCK)
    idx_t = jnp.transpose(top_idx[:, :8])
    xs = _dispatch(x1, g_ffn_row, idx_t, plan, tm)
    yb = _experts(xs, plan, w_mlp1, b_mlp1[:, None, :], w_mlp2, b_mlp2[:, None, :])
    return _combine(x1, yb, top_idx, top_w, plan, g_final.reshape(1, d), tm)


def kernel(x, g_mix, w_in, conv_w, conv_b, dt_bias_f, dt_bias_b, a_log_f, a_log_b, d_skip, g_ssm, q_norm_g,
           k_norm_g, w_br_ssm, w_br_attn, w_out, g_ffn, w_router, b_router, w_mlp1, b_mlp1, w_mlp2, b_mlp2,
           g_final):
    batch, seq, d = x.shape
    assert g_mix.shape[0] == 1, "single-layer model: the final rmsnorm is fused into the MoE combine"
    x2 = x.reshape(batch * seq, d)
    x1 = _token_mixer(x2, batch, seq, g_mix[0], w_in[0], conv_w[0], conv_b[0], dt_bias_f[0], dt_bias_b[0],
                      a_log_f[0], a_log_b[0], d_skip[0], g_ssm[0], q_norm_g[0], k_norm_g[0], w_br_ssm[0],
                      w_br_attn[0], w_out[0])
    out = _moe_and_final_norm(x1, g_ffn[0], w_router[0], b_router[0], w_mlp1[0], b_mlp1[0], w_mlp2[0],
                              b_mlp2[0], g_final)
    return out.reshape(batch, seq, d)
```

```python
import functools
import math

import jax
import jax.numpy as jnp
from jax import lax
from jax.experimental import pallas as pl
from jax.experimental.pallas import tpu as pltpu

F32 = jnp.float32
BF16 = jnp.bfloat16

NORM_EPS = 1e-6
GRID_W = 64
SSM_HEADS = 16
SSM_HEAD_DIM = 64
SSM_INNER = SSM_HEADS * SSM_HEAD_DIM
SSM_GROUPS = 2
SSM_STATE = 128
SSM_CONV = 5
CONV_CH = SSM_INNER + 2 * SSM_GROUPS * SSM_STATE
ATTN_HEADS = 8
ATTN_KV_HEADS = 2
ATTN_HEAD_DIM = 128
ROPE_THETA = 10000.0
N_EXPERTS = 32
TOP_K = 4
SWIGLU_LIMIT = 7.0
SWIGLU_ALPHA = 1.702

LANES = 128
BF16_SUBLANES = 16
VMEM_LIMIT = 56 * 1024 * 1024

Z_OFF, Q_OFF, GATE_OFF, XBC_OFF = 0, 1024, 2048, 4096
K_OFF, V_OFF, PROJ_COLS = 5632, 5888, 6144

MOE_TILE = 512
MOE_BLOCK = 512
MOE_GRAN = BF16_SUBLANES


def _cparams(sem):
    return pltpu.CompilerParams(dimension_semantics=sem, vmem_limit_bytes=VMEM_LIMIT)


def _sigmoid(x):
    return 1.0 / (1.0 + jnp.exp(-x))


def _softplus(x):
    return jnp.maximum(x, 0.0) + jnp.log(1.0 + jnp.exp(-jnp.abs(x)))


def _rms(x, g):
    ms = jnp.mean(x * x, axis=-1, keepdims=True)
    return x * lax.rsqrt(ms + NORM_EPS) * g


def _inproj_kernel(x_ref, g_ref, w_ref, wdt_ref, o_ref, dt_ref, h_scr):
    @pl.when(pl.program_id(1) == 0)
    def _():
        hb = _rms(x_ref[...], g_ref[...]).astype(BF16)
        h_scr[...] = hb
        dt_ref[...] = jnp.dot(hb, wdt_ref[...], preferred_element_type=F32)

    o_ref[...] = jnp.dot(h_scr[...], w_ref[...], preferred_element_type=F32).astype(o_ref.dtype)


def _in_proj(x2, g_mix, w_main, w_dt):
    n, d = x2.shape
    tm = min(1024, n)
    tn = 2048
    return pl.pallas_call(
        _inproj_kernel,
        grid=(n // tm, PROJ_COLS // tn),
        in_specs=[
            pl.BlockSpec((tm, d), lambda i, j: (i, 0)),
            pl.BlockSpec((1, d), lambda i, j: (0, 0)),
            pl.BlockSpec((d, tn), lambda i, j: (0, j)),
            pl.BlockSpec((d, LANES), lambda i, j: (0, 0)),
        ],
        out_specs=[
            pl.BlockSpec((tm, tn), lambda i, j: (i, j)),
            pl.BlockSpec((tm, LANES), lambda i, j: (i, 0)),
        ],
        out_shape=[
            jax.ShapeDtypeStruct((n, PROJ_COLS), BF16),
            jax.ShapeDtypeStruct((n, LANES), F32),
        ],
        scratch_shapes=[pltpu.VMEM((tm, d), BF16)],
        compiler_params=_cparams(("arbitrary", "arbitrary")),
    )(x2, g_mix, w_main, w_dt)


def _conv_kernel(prev_ref, cur_ref, next_ref, w_ref, b_ref, o_ref, scr):
    s = pl.program_id(1)
    ts = cur_ref.shape[1]
    halo = BF16_SUBLANES
    pad = (SSM_CONV - 1) // 2
    prev = prev_ref[0].astype(F32)
    nxt = next_ref[0].astype(F32)
    scr[0:halo, :] = jnp.where(s == 0, 0.0, prev)
    scr[halo:halo + ts, :] = cur_ref[0].astype(F32)
    scr[halo + ts:2 * halo + ts, :] = jnp.where(s == pl.num_programs(1) - 1, 0.0, nxt)
    acc = jnp.zeros((ts, cur_ref.shape[2]), F32) + b_ref[...]
    for k in range(SSM_CONV):
        acc = acc + w_ref[k:k + 1, :] * scr[halo - pad + k:halo - pad + k + ts, :]
    o_ref[0] = (acc * _sigmoid(acc)).astype(o_ref.dtype)


def _conv(proj3, conv_w, conv_b):
    b, s, _ = proj3.shape
    ts = min(512, s)
    tc = 512
    halo = BF16_SUBLANES
    hb = ts // halo
    col0 = XBC_OFF // tc
    return pl.pallas_call(
        _conv_kernel,
        grid=(b, s // ts, CONV_CH // tc),
        in_specs=[
            pl.BlockSpec((1, halo, tc), lambda bi, si, ci: (bi, jnp.maximum(si * hb - 1, 0), col0 + ci)),
            pl.BlockSpec((1, ts, tc), lambda bi, si, ci: (bi, si, col0 + ci)),
            pl.BlockSpec((1, halo, tc),
                         lambda bi, si, ci: (bi, jnp.minimum((si + 1) * hb, s // halo - 1), col0 + ci)),
            pl.BlockSpec((SSM_CONV, tc), lambda bi, si, ci: (0, ci)),
            pl.BlockSpec((1, tc), lambda bi, si, ci: (0, ci)),
        ],
        out_specs=pl.BlockSpec((1, ts, tc), lambda bi, si, ci: (bi, si, ci)),
        out_shape=jax.ShapeDtypeStruct((b, s, CONV_CH), BF16),
        scratch_shapes=[pltpu.VMEM((ts + 2 * halo, tc), F32)],
        compiler_params=_cparams(("arbitrary", "arbitrary", "arbitrary")),
    )(proj3, proj3, proj3, conv_w, conv_b)


def _ssd_kernel(xf_ref, xb_ref, dtf_ref, dtb_ref, dttf_ref, dttb_ref, brow_ref, bcol_ref,
                arow_ref, acol_ref, yf_ref, yb_ref, st_ref):
    L = xf_ref.shape[1]
    hg = SSM_HEADS // SSM_GROUPS
    pairs = hg // 2

    @pl.when(pl.program_id(1) == 0)
    def _():
        st_ref[...] = jnp.zeros_like(st_ref)

    rows = lax.broadcasted_iota(jnp.int32, (L, L), 0)
    cols = lax.broadcasted_iota(jnp.int32, (L, L), 1)
    lower = rows >= cols
    upper = rows <= cols
    ltri = lower.astype(BF16)
    utri = upper.astype(BF16)
    lane = lax.broadcasted_iota(jnp.int32, (L, LANES), 1)
    first_half = lane < SSM_HEAD_DIM
    lane1 = lax.broadcasted_iota(jnp.int32, (1, LANES), 1)
    log2e = math.log2(math.e)
    a_row = -jnp.exp(arow_ref[...]) * log2e
    a_col = -jnp.exp(acol_ref[...]) * log2e

    def split3(v):
        hi = v.astype(BF16)
        r1 = v - hi.astype(F32)
        mid = r1.astype(BF16)
        return hi, mid, (r1 - mid.astype(F32)).astype(BF16)

    def cumsum_cols(tri, v):
        return jnp.dot(jnp.concatenate([tri] * 3, axis=1), jnp.concatenate(split3(v), axis=0),
                       preferred_element_type=F32)

    def cumsum_rows(v, tri):
        return jnp.dot(jnp.concatenate(split3(v), axis=1), jnp.concatenate([tri] * 3, axis=0),
                       preferred_element_type=F32)

    for d in range(2):
        x_ref, dt_ref, dtt_ref, y_ref = ((xf_ref, dtf_ref, dttf_ref, yf_ref) if d == 0
                                         else (xb_ref, dtb_ref, dttb_ref, yb_ref))
        a = _softplus(dt_ref[0] + brow_ref[...]) * a_row
        dt_t = _softplus(dtt_ref[0] + bcol_ref[...])
        a_t = dt_t * a_col
        if d == 0:
            cs_col = cumsum_cols(ltri, a)
            cs_row = cumsum_rows(a_t, utri)
            tot = cs_col[L - 1:L, :]
            tot_t = cs_row[:, L - 1:L]
            mask = lower
        else:
            cs_col = cumsum_cols(utri, a)
            cs_row = cumsum_rows(a_t, ltri)
            tot = cs_col[0:1, :]
            tot_t = cs_row[:, 0:1]
            mask = upper
        w_t = dt_t * jnp.exp2(tot_t - cs_row)
        src_t = cs_row - jnp.log2(dt_t)
        chunk_decay = jnp.exp2(tot)

        for g in range(SSM_GROUPS):
            boff = SSM_INNER + g * SSM_STATE
            coff = SSM_INNER + SSM_GROUPS * SSM_STATE + g * SSM_STATE
            bm = x_ref[0, :, boff:boff + SSM_STATE]
            cm = x_ref[0, :, coff:coff + SSM_STATE]
            cb = lax.dot_general(cm, bm, (((1,), (1,)), ((), ())), preferred_element_type=F32)
            bt = bm.astype(F32).T
            st = st_ref[d, g]
            y_off = jnp.dot(cm, st.astype(BF16), preferred_element_type=F32)
            for pr in range(pairs):
                h0 = d * SSM_HEADS + g * hg + 2 * pr
                xoff = (g * pairs + pr) * LANES
                xs = x_ref[0, :, xoff:xoff + LANES]
                zero = jnp.zeros_like(xs)
                rhs = jnp.concatenate([jnp.where(first_half, xs, zero),
                                       jnp.where(first_half, zero, xs)], axis=0)
                ms, ws, dins = [], [], []
                for hh in (h0, h0 + 1):
                    cs_b = jnp.broadcast_to(cs_col[:, hh:hh + 1], (L, L))
                    seg = cs_b - src_t[hh:hh + 1, :]
                    m = cb * jnp.exp2(jnp.where(mask, seg, -jnp.inf))
                    ms.append(m.astype(BF16))
                    ws.append((bt * w_t[hh:hh + 1, :]).astype(BF16))
                    dins.append(jnp.exp2(cs_b))
                y = jnp.dot(jnp.concatenate(ms, axis=1), rhs, preferred_element_type=F32)
                y = y + y_off[:, pr * LANES:(pr + 1) * LANES] * jnp.where(first_half, dins[0], dins[1])
                y_ref[0, :, xoff:xoff + LANES] = y.astype(y_ref.dtype)
                new_st = jnp.dot(jnp.concatenate(ws, axis=1), rhs, preferred_element_type=F32)
                cd = jnp.where(lane1 < SSM_HEAD_DIM, chunk_decay[:, h0:h0 + 1], chunk_decay[:, h0 + 1:h0 + 2])
                st_ref[d, g, :, pr * LANES:(pr + 1) * LANES] = st[:, pr * LANES:(pr + 1) * LANES] * cd + new_st


def _ssd(xbc, dt3, dtt3, bias_row, bias_col, alog_row, alog_col):
    b, s, _ = xbc.shape
    L = min(128, s)
    nc = s // L
    hg = SSM_HEADS // SSM_GROUPS
    fwd = lambda bi, ci: (bi, ci, 0)
    bwd = lambda bi, ci: (bi, nc - 1 - ci, 0)
    fwd_t = lambda bi, ci: (bi, 0, ci)
    bwd_t = lambda bi, ci: (bi, 0, nc - 1 - ci)
    const = lambda bi, ci: (0, 0)
    return pl.pallas_call(
        _ssd_kernel,
        grid=(b, nc),
        in_specs=[
            pl.BlockSpec((1, L, CONV_CH), fwd),
            pl.BlockSpec((1, L, CONV_CH), bwd),
            pl.BlockSpec((1, L, LANES), fwd),
            pl.BlockSpec((1, L, LANES), bwd),
            pl.BlockSpec((1, 2 * SSM_HEADS, L), fwd_t),
            pl.BlockSpec((1, 2 * SSM_HEADS, L), bwd_t),
            pl.BlockSpec((1, LANES), const),
            pl.BlockSpec((2 * SSM_HEADS, 1), const),
            pl.BlockSpec((1, LANES), const),
            pl.BlockSpec((2 * SSM_HEADS, 1), const),
        ],
        out_specs=[
            pl.BlockSpec((1, L, SSM_INNER), fwd),
            pl.BlockSpec((1, L, SSM_INNER), bwd),
        ],
        out_shape=[jax.ShapeDtypeStruct((b, s, SSM_INNER), BF16)] * 2,
        scratch_shapes=[pltpu.VMEM((2, SSM_GROUPS, SSM_STATE, hg * SSM_HEAD_DIM), F32)],
        compiler_params=_cparams(("arbitrary", "arbitrary")),
    )(xbc, xbc, dt3, dt3, dtt3, dtt3, bias_row, bias_col, alog_row, alog_col)


def _rope_head_order(a):
    q4 = ATTN_HEAD_DIM // 4
    return jnp.concatenate([a[..., 0:q4], a[..., 2 * q4:3 * q4], a[..., q4:2 * q4], a[..., 3 * q4:]], axis=-1)


def _rope_norm(t, g, cos, sin_signed, ones):
    sq = t * t
    hi = sq.astype(BF16)
    lo = (sq - hi.astype(F32)).astype(BF16)
    ms = jnp.dot(jnp.concatenate([hi, lo], axis=1), ones, preferred_element_type=F32) * (1.0 / ATTN_HEAD_DIM)
    tn = t * lax.rsqrt(ms + NORM_EPS) * g
    return tn * cos + pltpu.roll(tn, ATTN_HEAD_DIM // 2, 1) * sin_signed


Q_SCALE = ATTN_HEAD_DIM ** -0.5 * math.log2(math.e)


def _qkprep_kernel(q_ref, k_ref, cos_ref, sin_ref, qg_ref, kg_ref, qo_ref, ko_ref):
    cos = cos_ref[...]
    sin = sin_ref[...]
    ones = jnp.ones((2 * ATTN_HEAD_DIM, ATTN_HEAD_DIM), BF16)

    def heads(src_ref, g_ref, dst_ref, n_heads, scale):
        for h in range(n_heads):
            sl = slice(h * ATTN_HEAD_DIM, (h + 1) * ATTN_HEAD_DIM)
            r = _rope_norm(src_ref[:, sl].astype(F32), g_ref[...], cos, sin, ones) * scale
            dst_ref[:, sl] = r.astype(dst_ref.dtype)

    heads(q_ref, qg_ref, qo_ref, ATTN_HEADS, Q_SCALE)
    heads(k_ref, kg_ref, ko_ref, ATTN_KV_HEADS, 1.0)


def _qk_prep(proj, cos_t, sin_t, q_norm_g, k_norm_g, seq):
    n = proj.shape[0]
    tm = min(512, seq)
    qw = ATTN_HEADS * ATTN_HEAD_DIM
    kw = ATTN_KV_HEADS * ATTN_HEAD_DIM
    spt = seq // tm
    return pl.pallas_call(
        _qkprep_kernel,
        grid=(n // tm,),
        in_specs=[
            pl.BlockSpec((tm, qw), lambda i: (i, Q_OFF // qw)),
            pl.BlockSpec((tm, kw), lambda i: (i, K_OFF // kw)),
            pl.BlockSpec((tm, ATTN_HEAD_DIM), lambda i: (i % spt, 0)),
            pl.BlockSpec((tm, ATTN_HEAD_DIM), lambda i: (i % spt, 0)),
            pl.BlockSpec((1, ATTN_HEAD_DIM), lambda i: (0, 0)),
            pl.BlockSpec((1, ATTN_HEAD_DIM), lambda i: (0, 0)),
        ],
        out_specs=[
            pl.BlockSpec((tm, qw), lambda i: (i, 0)),
            pl.BlockSpec((tm, kw), lambda i: (i, 0)),
        ],
        out_shape=[jax.ShapeDtypeStruct((n, qw), BF16), jax.ShapeDtypeStruct((n, kw), BF16)],
        compiler_params=_cparams(("arbitrary",)),
    )(proj, proj, cos_t, sin_t, q_norm_g, k_norm_g)


def _flash_kernel(small_ref, q_ref, k_ref, v_ref, o_ref, *, tk):
    q = q_ref[0]
    tq = q.shape[0]
    nk = k_ref.shape[1] // tk
    nt = (((1,), (1,)), ((), ()))
    small = small_ref[0] != 0

    def kv(i):
        off = pl.multiple_of(i * tk, tk)
        return k_ref[0, pl.ds(off, tk), :], v_ref[0, pl.ds(off, tk), :]

    @pl.when(small)
    def _():
        def body(i, carry):
            l, acc = carry
            k, v = kv(i)
            p = jnp.exp2(lax.dot_general(q, k, nt, preferred_element_type=F32))
            for c in range(tk // LANES):
                l = l + p[:, c * LANES:(c + 1) * LANES]
            acc = acc + jnp.dot(p.astype(BF16), v, preferred_element_type=F32)
            return l, acc

        init = (jnp.zeros((tq, LANES), F32), jnp.zeros((tq, ATTN_HEAD_DIM), F32))
        l, acc = lax.fori_loop(0, nk, body, init)
        o_ref[0] = (acc / jnp.sum(l, axis=-1, keepdims=True)).astype(o_ref.dtype)

    @pl.when(jnp.logical_not(small))
    def _():
        def body(i, carry):
            m, l, acc = carry
            k, v = kv(i)
            s = lax.dot_general(q, k, nt, preferred_element_type=F32)
            m_new = jnp.maximum(m, jnp.max(s, axis=-1, keepdims=True))
            alpha = jnp.exp2(m - m_new)
            p = jnp.exp2(s - m_new)
            l = alpha * l + jnp.sum(p, axis=-1, keepdims=True)
            acc = alpha * acc + jnp.dot(p.astype(BF16), v, preferred_element_type=F32)
            return m_new, l, acc

        init = (jnp.full((tq, 1), -jnp.inf, F32), jnp.zeros((tq, 1), F32),
                jnp.zeros((tq, ATTN_HEAD_DIM), F32))
        _, l, acc = lax.fori_loop(0, nk, body, init)
        o_ref[0] = (acc / l).astype(o_ref.dtype)


SCORE_BOUND = 59.0


def _flash(q3, k3, proj3, q_norm_g, k_norm_g):
    b, s, _ = q3.shape
    tq = min(512, s)
    tk = min(8192, s)
    nq = s // tq
    grp = ATTN_HEADS // ATTN_KV_HEADS
    hd = ATTN_HEAD_DIM
    bound = hd * Q_SCALE * jnp.max(jnp.abs(q_norm_g)) * jnp.max(jnp.abs(k_norm_g)) * 1.02
    small = (bound <= SCORE_BOUND).astype(jnp.int32).reshape(1)
    grid_spec = pltpu.PrefetchScalarGridSpec(
        num_scalar_prefetch=1,
        grid=(b, ATTN_HEADS, nq),
        in_specs=[
            pl.BlockSpec((1, tq, hd), lambda bi, h, qi, sm: (bi, qi, h)),
            pl.BlockSpec((1, s, hd), lambda bi, h, qi, sm: (bi, 0, h // grp)),
            pl.BlockSpec((1, s, hd), lambda bi, h, qi, sm: (bi, 0, V_OFF // hd + h // grp)),
        ],
        out_specs=pl.BlockSpec((1, tq, hd), lambda bi, h, qi, sm: (bi, qi, h)),
    )
    return pl.pallas_call(
        functools.partial(_flash_kernel, tk=tk),
        grid_spec=grid_spec,
        out_shape=jax.ShapeDtypeStruct((b, s, ATTN_HEADS * hd), BF16),
        compiler_params=_cparams(("arbitrary", "arbitrary", "arbitrary")),
    )(small, q3, k3, proj3)


def _merge_kernel(yf_ref, yb_ref, xs_ref, z_ref, gate_ref, attn_ref, x_ref, dskip_ref, gssm_ref,
                  wbs_ref, wba_ref, wo_ref, x1_ref):
    d = x_ref.shape[1]
    xs = xs_ref[...].astype(F32)
    y = yf_ref[...].astype(F32) + yb_ref[...].astype(F32) + xs * dskip_ref[...]
    z = z_ref[...].astype(F32)
    y = _rms(y * (z * _sigmoid(z)), gssm_ref[...])
    br_ssm = jnp.dot(y.astype(BF16), wbs_ref[...], preferred_element_type=F32)
    br_attn = jnp.dot(attn_ref[...], wba_ref[...], preferred_element_type=F32)
    g_s = _sigmoid(gate_ref[:, :d].astype(F32))
    g_a = _sigmoid(gate_ref[:, d:].astype(F32))
    merged = (g_s * br_ssm + g_a * br_attn).astype(BF16)
    x1_ref[...] = x_ref[...] + jnp.dot(merged, wo_ref[...], preferred_element_type=F32)


def _merge(y_f, y_b, xbc, proj, attn, x2, dskip_row, g_ssm, w_br_ssm, w_br_attn, w_out):
    n, d = x2.shape
    tm = min(256, n)
    row = lambda i: (i, 0)
    const = lambda i: (0, 0)
    return pl.pallas_call(
        _merge_kernel,
        grid=(n // tm,),
        in_specs=[
            pl.BlockSpec((tm, d), row),
            pl.BlockSpec((tm, d), row),
            pl.BlockSpec((tm, d), row),
            pl.BlockSpec((tm, d), lambda i: (i, Z_OFF // d)),
            pl.BlockSpec((tm, 2 * d), lambda i: (i, GATE_OFF // (2 * d))),
            pl.BlockSpec((tm, d), row),
            pl.BlockSpec((tm, d), row),
            pl.BlockSpec((1, d), const),
            pl.BlockSpec((1, d), const),
            pl.BlockSpec((d, d), const),
            pl.BlockSpec((d, d), const),
            pl.BlockSpec((d, d), const),
        ],
        out_specs=pl.BlockSpec((tm, d), row),
        out_shape=jax.ShapeDtypeStruct((n, d), F32),
        compiler_params=_cparams(("arbitrary",)),
    )(y_f, y_b, xbc, proj, proj, attn, x2, dskip_row, g_ssm, w_br_ssm, w_br_attn, w_out)


def _router_kernel(x1_ref, g_ref, w_ref, b_ref, idx_ref, wt_ref, cnt_ref):
    h = _rms(x1_ref[...], g_ref[...])
    h_hi = h.astype(BF16)
    h_lo = (h - h_hi.astype(F32)).astype(BF16)
    logits = jnp.dot(jnp.concatenate([h_hi, h_hi, h_lo], axis=1), w_ref[...],
                     preferred_element_type=F32) + b_ref[...]
    lane = lax.broadcasted_iota(jnp.int32, logits.shape, 1)
    idx_out = jnp.zeros(logits.shape, jnp.int32)
    val_out = jnp.zeros(logits.shape, F32)
    chosen = jnp.zeros(logits.shape, F32)
    vals = []
    for k in range(TOP_K):
        m = jnp.max(logits, axis=-1, keepdims=True)
        idx = jnp.min(jnp.where(logits == m, lane, LANES), axis=-1, keepdims=True)
        idx_out = jnp.where(lane == k, idx, idx_out)
        vals.append(m)
        hit = lane == idx
        chosen = jnp.where(hit, 1.0, chosen)
        logits = jnp.where(hit, -jnp.inf, logits)
    es = [jnp.exp(v - vals[0]) for v in vals]
    tot = es[0] + es[1] + es[2] + es[3]
    for k in range(TOP_K):
        val_out = jnp.where(lane == k, es[k] / tot, val_out)
    idx_ref[...] = idx_out
    wt_ref[...] = val_out
    cnt_ref[0] = jnp.broadcast_to(jnp.sum(chosen, axis=0, keepdims=True), cnt_ref.shape[1:])


def _router(x1, g_ffn, w_router_pad, b_router_pad):
    n, d = x1.shape
    tm = min(MOE_TILE, n)
    return pl.pallas_call(
        _router_kernel,
        grid=(n // tm,),
        in_specs=[
            pl.BlockSpec((tm, d), lambda i: (i, 0)),
            pl.BlockSpec((1, d), lambda i: (0, 0)),
            pl.BlockSpec((3 * d, LANES), lambda i: (0, 0)),
            pl.BlockSpec((1, LANES), lambda i: (0, 0)),
        ],
        out_specs=[pl.BlockSpec((tm, LANES), lambda i: (i, 0)),
                   pl.BlockSpec((tm, LANES), lambda i: (i, 0)),
                   pl.BlockSpec((1, 8, LANES), lambda i: (i, 0, 0))],
        out_shape=[jax.ShapeDtypeStruct((n, LANES), jnp.int32),
                   jax.ShapeDtypeStruct((n, LANES), F32),
                   jax.ShapeDtypeStruct((n // tm, 8, LANES), F32)],
        compiler_params=_cparams(("arbitrary",)),
    )(x1, g_ffn, w_router_pad, b_router_pad)


def _routing_plan(cnt, n, tm, tb):
    nt = n // tm
    gran = MOE_GRAN
    rc = (cnt + gran - 1) // gran * gran
    covered = jnp.sum(rc, axis=0)
    region = (covered + tb - 1) // tb * tb
    pad_end = jnp.cumsum(region)
    pad_start = pad_end - region
    seg_start = pad_start[None, :] + jnp.cumsum(rc, axis=0) - rc
    stage_off = jnp.cumsum(rc, axis=1) - rc
    n_used = pad_end[-1] // tb
    nb = (TOP_K * n + nt * N_EXPERTS * (gran - 1) + N_EXPERTS * (tb - 1) + tb - 1) // tb
    blk = jnp.arange(nb, dtype=jnp.int32)
    block_e = jnp.sum((pad_end[None, :] <= (jnp.minimum(blk, n_used - 1) * tb)[:, None]).astype(jnp.int32), axis=1)
    block_e = jnp.minimum(block_e, N_EXPERTS - 1)
    flat = lambda a: a.reshape(-1).astype(jnp.int32)
    tail_gran = (region - covered) // gran
    ngran = rc // gran
    g_end = jnp.cumsum(ngran, axis=1)
    gi = jnp.arange(_stage_rows(tm) // gran, dtype=jnp.int32)
    g_exp = jnp.minimum(jnp.sum((g_end[:, None, :] <= gi[None, :, None]).astype(jnp.int32), axis=2), N_EXPERTS - 1)
    mine = g_exp[:, :, None] == jnp.arange(N_EXPERTS, dtype=jnp.int32)[None, None, :]
    seg_base = seg_start - gran * (g_end - ngran)
    gran_dst = jnp.sum(jnp.where(mine, seg_base[:, None, :], 0), axis=2) + gran * gi[None, :]
    ids = jnp.arange(N_EXPERTS, dtype=jnp.int32)
    later = jnp.where((ids[None, :] > ids[:, None]) & (region[None, :] > 0), ids[None, :], N_EXPERTS)
    nxt = jnp.min(later, axis=1)
    next_expert = jnp.where(nxt < N_EXPERTS, nxt, ids)
    return dict(gran_dst=flat(gran_dst), tot_gran=flat(g_end[:, -1]), block_e=flat(block_e),
                next_expert=flat(next_expert),
                n_used=flat(n_used), nb=nb, stage_off_f=stage_off.astype(F32),
                tail_start=flat(pad_start + covered), tail_gran=flat(tail_gran),
                tot_tail=flat(jnp.sum(tail_gran)))


def _stage_rows(tm):
    rows = TOP_K * tm + N_EXPERTS * (MOE_GRAN - 1)
    return (rows + 2 * LANES - 1) // (2 * LANES) * (2 * LANES)


GRAN_UNROLL = 4


def _granule_copies(t, gdst_ref, totg_ref, make_copy, start):
    total = totg_ref[t]
    base = t * (gdst_ref.shape[0] // totg_ref.shape[0])
    full = lax.shift_right_logical(total, GRAN_UNROLL.bit_length() - 1)

    def one(i):
        make_copy(pl.multiple_of(i * MOE_GRAN, MOE_GRAN), pl.multiple_of(gdst_ref[base + i], MOE_GRAN),
                  MOE_GRAN).start()

    def group(q, c):
        if start:
            for u in range(GRAN_UNROLL):
                one(q * GRAN_UNROLL + u)
        else:
            make_copy(0, 0, GRAN_UNROLL * MOE_GRAN).wait()
        return c

    def rest(i, c):
        if start:
            one(i)
        else:
            make_copy(0, 0, MOE_GRAN).wait()
        return c

    lax.fori_loop(0, full, group, 0)
    lax.fori_loop(full * GRAN_UNROLL, total, rest, 0)


def _dispatch_kernel(gdst_ref, totg_ref, tstart_ref, tgran_ref, misc_ref,
                     x1_ref, g_ref, idxt_ref, soffc_ref, xs_hbm, stage, zbuf, sem, zsem, *, n_blocks):
    t = pl.program_id(0)
    slot = t % 2
    tm = x1_ref.shape[0]
    rows = stage.shape[1]
    h = _rms(x1_ref[...], g_ref[...]).astype(BF16)
    idxt = idxt_ref[...]
    expert = lax.broadcasted_iota(jnp.int32, (LANES, tm), 0)
    hots = [idxt[k:k + 1, :] == expert for k in range(TOP_K)]
    multi = jnp.zeros((LANES, tm), F32)
    for hot in hots:
        multi = jnp.where(hot, 1.0, multi)
    earlier = (lax.broadcasted_iota(jnp.int32, (tm, tm), 0)
               < lax.broadcasted_iota(jnp.int32, (tm, tm), 1)).astype(BF16)
    rank = jnp.dot(multi.astype(BF16), earlier, preferred_element_type=F32)
    pos = rank + soffc_ref[0]
    srows = [jnp.sum(jnp.where(hot, pos, 0.0), axis=0, keepdims=True) for hot in hots]
    chunk = 2 * LANES
    row_id = lax.broadcasted_iota(jnp.int32, (chunk, tm), 0).astype(F32)
    for c in range(rows // chunk):
        perm = jnp.zeros((chunk, tm), F32)
        for srow in srows:
            perm = jnp.where(row_id == srow - float(c * chunk), 1.0, perm)
        stage[slot, c * chunk:(c + 1) * chunk, :] = jnp.dot(
            perm.astype(BF16), h, preferred_element_type=F32).astype(stage.dtype)

    def copy_from(s):
        def copy(stage_row, sorted_row, nrows):
            return pltpu.make_async_copy(stage.at[s, pl.ds(stage_row, nrows)],
                                         xs_hbm.at[pl.ds(sorted_row, nrows)], sem.at[s])
        return copy

    _granule_copies(t, gdst_ref, totg_ref, copy_from(slot), start=True)

    @pl.when(t > 0)
    def _():
        _granule_copies(jnp.maximum(t - 1, 0), gdst_ref, totg_ref, copy_from(1 - slot), start=False)

    @pl.when(t == pl.num_programs(0) - 1)
    def _():
        _granule_copies(t, gdst_ref, totg_ref, copy_from(slot), start=False)
        tb = zbuf.shape[0]
        n_used = misc_ref[0]
        zbuf[...] = jnp.zeros_like(zbuf)

        def zero_gran(row):
            return pltpu.make_async_copy(zbuf.at[pl.ds(0, MOE_GRAN)], xs_hbm.at[pl.ds(row, MOE_GRAN)], zsem)

        def zero_block(blk):
            return pltpu.make_async_copy(zbuf, xs_hbm.at[pl.ds(pl.multiple_of(blk * tb, tb), tb)], zsem)

        def tails(e, c):
            def one(g, c2):
                zero_gran(pl.multiple_of(tstart_ref[e] + g * MOE_GRAN, MOE_GRAN)).start()
                return c2
            lax.fori_loop(0, tgran_ref[e], one, 0)
            return c

        def start_block(blk, c):
            zero_block(blk).start()
            return c

        def wait_gran(g, c):
            zero_gran(0).wait()
            return c

        def wait_block(blk, c):
            zero_block(0).wait()
            return c

        lax.fori_loop(0, N_EXPERTS, tails, 0)
        lax.fori_loop(n_used, n_blocks, start_block, 0)
        lax.fori_loop(0, misc_ref[1], wait_gran, 0)
        lax.fori_loop(n_used, n_blocks, wait_block, 0)


def _dispatch(x1, g_ffn, idx_t, plan, tm):
    n, d = x1.shape
    nt = n // tm
    rows = _stage_rows(tm)
    soff_col = plan["stage_off_f"].reshape(nt, N_EXPERTS, 1)
    soff_col = jnp.pad(soff_col, ((0, 0), (0, LANES - N_EXPERTS), (0, 0)))
    misc = jnp.concatenate([plan["n_used"], plan["tot_tail"]])
    grid_spec = pltpu.PrefetchScalarGridSpec(
        num_scalar_prefetch=5,
        grid=(nt,),
        in_specs=[
            pl.BlockSpec((tm, d), lambda t, *_: (t, 0)),
            pl.BlockSpec((1, d), lambda t, *_: (0, 0)),
            pl.BlockSpec((8, tm), lambda t, *_: (0, t)),
            pl.BlockSpec((1, LANES, 1), lambda t, *_: (t, 0, 0)),
        ],
        out_specs=pl.BlockSpec(memory_space=pl.ANY),
        scratch_shapes=[pltpu.VMEM((2, rows, d), BF16), pltpu.VMEM((MOE_BLOCK, d), BF16),
                        pltpu.SemaphoreType.DMA((2,)), pltpu.SemaphoreType.DMA(())],
    )
    return pl.pallas_call(
        functools.partial(_dispatch_kernel, n_blocks=plan["nb"]),
        grid_spec=grid_spec,
        out_shape=jax.ShapeDtypeStruct((plan["nb"] * MOE_BLOCK, d), BF16),
        compiler_params=_cparams(("arbitrary",)),
    )(plan["gran_dst"], plan["tot_gran"], plan["tail_start"], plan["tail_gran"], misc, x1, g_ffn, idx_t,
      soff_col)


def _expert_kernel(be_ref, nused_ref, nexte_ref, x_ref, w1_hbm, b1_ref, w2_hbm, b2_ref, o_ref,
                   w1f, w2f, w1b, w2b, sem):
    de = w2f.shape[0]
    j = pl.program_id(0)
    live = j < nused_ref[0]
    e = be_ref[j]
    new_expert = jnp.logical_or(j == 0, e != be_ref[jnp.maximum(j - 1, 0)])

    def fetch(ex):
        return (pltpu.make_async_copy(w1_hbm.at[ex], w1f, sem.at[0]),
                pltpu.make_async_copy(w2_hbm.at[ex], w2f, sem.at[1]))

    @pl.when(jnp.logical_and(live, j == 0))
    def _():
        for c in fetch(e):
            c.start()

    @pl.when(jnp.logical_and(live, new_expert))
    def _():
        for c in fetch(e):
            c.wait()
        w1b[...] = w1f[...].astype(BF16)
        w2b[...] = w2f[...].astype(BF16)
        nxt = nexte_ref[e]

        @pl.when(nxt != e)
        def _():
            for c in fetch(nxt):
                c.start()

    @pl.when(live)
    def _():
        hm = jnp.dot(x_ref[...], w1b[...], preferred_element_type=F32) + b1_ref[0]
        gate = jnp.minimum(hm[:, :de], SWIGLU_LIMIT)
        up = jnp.clip(hm[:, de:], -SWIGLU_LIMIT, SWIGLU_LIMIT)
        act = gate * _sigmoid(SWIGLU_ALPHA * gate) * (up + 1.0)
        y = jnp.dot(act.astype(BF16), w2b[...], preferred_element_type=F32) + b2_ref[0]
        o_ref[...] = y.astype(o_ref.dtype)

    @pl.when(jnp.logical_not(live))
    def _():
        o_ref[...] = jnp.zeros_like(o_ref)


def _experts(xs, plan, w1, b1, w2, b2):
    d = xs.shape[1]
    tb = MOE_BLOCK
    f2 = w1.shape[2]
    de = w2.shape[1]
    last = lambda j, nu: jnp.maximum(jnp.minimum(j, nu[0] - 1), 0)
    grid_spec = pltpu.PrefetchScalarGridSpec(
        num_scalar_prefetch=3,
        grid=(plan["nb"],),
        in_specs=[
            pl.BlockSpec((tb, d), lambda j, be, nu, ne: (last(j, nu), 0)),
            pl.BlockSpec(memory_space=pl.ANY),
            pl.BlockSpec((1, 1, f2), lambda j, be, nu, ne: (be[j], 0, 0)),
            pl.BlockSpec(memory_space=pl.ANY),
            pl.BlockSpec((1, 1, d), lambda j, be, nu, ne: (be[j], 0, 0)),
        ],
        out_specs=pl.BlockSpec((tb, d), lambda j, be, nu, ne: (j, 0)),
        scratch_shapes=[pltpu.VMEM((d, f2), F32), pltpu.VMEM((de, d), F32),
                        pltpu.VMEM((d, f2), BF16), pltpu.VMEM((de, d), BF16),
                        pltpu.SemaphoreType.DMA((2,))],
    )
    return pl.pallas_call(
        _expert_kernel,
        grid_spec=grid_spec,
        out_shape=jax.ShapeDtypeStruct(xs.shape, BF16),
        compiler_params=_cparams(("arbitrary",)),
    )(plan["block_e"], plan["n_used"], plan["next_expert"], xs, w1, b1, w2, b2)


def _combine_kernel(gdst_ref, totg_ref, x1_ref, idx_ref, wt_ref, soffr_ref, g_ref, yb_hbm, o_ref, stage, sem):
    t = pl.program_id(0)
    slot = t % 2
    tm = x1_ref.shape[0]
    rows = stage.shape[1]

    def copy_into(s):
        def copy(stage_row, sorted_row, nrows):
            return pltpu.make_async_copy(yb_hbm.at[pl.ds(sorted_row, nrows)],
                                         stage.at[s, pl.ds(stage_row, nrows)], sem.at[s])
        return copy

    @pl.when(t == 0)
    def _():
        stage[...] = jnp.zeros_like(stage)
        _granule_copies(t, gdst_ref, totg_ref, copy_into(slot), start=True)

    @pl.when(t + 1 < pl.num_programs(0))
    def _():
        _granule_copies(t + 1, gdst_ref, totg_ref, copy_into(1 - slot), start=True)

    idx = idx_ref[...]
    wt = wt_ref[...]
    expert = lax.broadcasted_iota(jnp.int32, (tm, LANES), 1)
    hots = [idx[:, k:k + 1] == expert for k in range(TOP_K)]
    multi = jnp.zeros((tm, LANES), F32)
    for hot in hots:
        multi = jnp.where(hot, 1.0, multi)
    earlier = (lax.broadcasted_iota(jnp.int32, (tm, tm), 0)
               > lax.broadcasted_iota(jnp.int32, (tm, tm), 1)).astype(BF16)
    rank = jnp.dot(earlier, multi.astype(BF16), preferred_element_type=F32)
    pos = rank + soffr_ref[0]
    srows = [jnp.sum(jnp.where(hot, pos, 0.0), axis=-1, keepdims=True) for hot in hots]
    chunk = LANES
    row_id = lax.broadcasted_iota(jnp.int32, (chunk, rows), 1).astype(F32)
    for c in range(tm // chunk):
        sl = slice(c * chunk, (c + 1) * chunk)
        unsort = jnp.zeros((chunk, rows), F32)
        for k, srow in enumerate(srows):
            unsort = jnp.where(row_id == srow[sl], wt[sl, k:k + 1], unsort)
        if c == 0:
            _granule_copies(t, gdst_ref, totg_ref, copy_into(slot), start=False)
        y = jnp.dot(unsort.astype(BF16), stage[slot], preferred_element_type=F32)
        o_ref[sl, :] = _rms(x1_ref[sl, :] + y, g_ref[...])


def _combine(x1, yb, top_idx, top_w, plan, g_final, tm):
    n, d = x1.shape
    nt = n // tm
    rows = _stage_rows(tm)
    soff_row = jnp.pad(plan["stage_off_f"], ((0, 0), (0, LANES - N_EXPERTS))).reshape(nt, 1, LANES)
    grid_spec = pltpu.PrefetchScalarGridSpec(
        num_scalar_prefetch=2,
        grid=(nt,),
        in_specs=[
            pl.BlockSpec((tm, d), lambda t, *_: (t, 0)),
            pl.BlockSpec((tm, LANES), lambda t, *_: (t, 0)),
            pl.BlockSpec((tm, LANES), lambda t, *_: (t, 0)),
            pl.BlockSpec((1, 1, LANES), lambda t, *_: (t, 0, 0)),
            pl.BlockSpec((1, d), lambda t, *_: (0, 0)),
            pl.BlockSpec(memory_space=pl.ANY),
        ],
        out_specs=pl.BlockSpec((tm, d), lambda t, *_: (t, 0)),
        scratch_shapes=[pltpu.VMEM((2, rows, d), BF16), pltpu.SemaphoreType.DMA((2,))],
    )
    return pl.pallas_call(
        _combine_kernel,
        grid_spec=grid_spec,
        out_shape=jax.ShapeDtypeStruct((n, d), F32),
        compiler_params=_cparams(("arbitrary",)),
    )(plan["gran_dst"], plan["tot_gran"], x1, top_idx, top_w, soff_row, g_final, yb)


def _rope_tables(seq):
    half = ATTN_HEAD_DIM // 2
    inv_freq = ROPE_THETA ** (-jnp.arange(0, half, 2, dtype=F32) / half)
    pos = jnp.arange(seq, dtype=jnp.int32)
    ang_r = (pos // GRID_W).astype(F32)[:, None] * inv_freq
    ang_c = (pos % GRID_W).astype(F32)[:, None] * inv_freq
    cos_t = jnp.concatenate([jnp.cos(ang_r), jnp.cos(ang_c)] * 2, axis=-1)
    sin_t = jnp.concatenate([-jnp.sin(ang_r), -jnp.sin(ang_c), jnp.sin(ang_r), jnp.sin(ang_c)], axis=-1)
    return cos_t, sin_t


def _token_mixer(x2, batch, seq, g_mix, w_in, conv_w, conv_b, dt_bias_f, dt_bias_b, a_log_f, a_log_b, d_skip,
                 g_ssm, q_norm_g, k_norm_g, w_br_ssm, w_br_attn, w_out):
    n, d = x2.shape
    z_end = SSM_INNER
    xbc_end = z_end + CONV_CH
    dtf_end = xbc_end + SSM_HEADS
    dtb_end = dtf_end + SSM_HEADS
    q_end = dtb_end + ATTN_HEADS * ATTN_HEAD_DIM
    k_end = q_end + ATTN_KV_HEADS * ATTN_HEAD_DIM
    v_end = k_end + ATTN_KV_HEADS * ATTN_HEAD_DIM
    head_cols = lambda w: _rope_head_order(w.reshape(d, -1, ATTN_HEAD_DIM)).reshape(d, -1)
    w_main = jnp.concatenate([w_in[:, :z_end], head_cols(w_in[:, dtb_end:q_end]), w_in[:, v_end:],
                              w_in[:, z_end:xbc_end], head_cols(w_in[:, q_end:k_end]), w_in[:, k_end:v_end]],
                             axis=1).astype(BF16)
    w_dt = jnp.pad(w_in[:, xbc_end:dtb_end], ((0, 0), (0, LANES - 2 * SSM_HEADS))).astype(BF16)

    proj, dt = _in_proj(x2, g_mix.reshape(1, d), w_main, w_dt)
    proj3 = proj.reshape(batch, seq, PROJ_COLS)

    xbc = _conv(proj3, conv_w, conv_b.reshape(1, CONV_CH))

    dt3 = dt.reshape(batch, seq, LANES)
    dtt3 = jnp.swapaxes(dt3[:, :, :2 * SSM_HEADS], 1, 2)
    bias = jnp.concatenate([dt_bias_f, dt_bias_b])
    alog = jnp.concatenate([a_log_f, a_log_b])
    pad_row = lambda v: jnp.pad(v, (0, LANES - 2 * SSM_HEADS)).reshape(1, LANES)
    y_f, y_b = _ssd(xbc, dt3, dtt3, pad_row(bias), bias.reshape(-1, 1), pad_row(alog), alog.reshape(-1, 1))

    cos_t, sin_t = _rope_tables(seq)
    q_rot, k_rot = _qk_prep(proj, cos_t, sin_t, _rope_head_order(q_norm_g).reshape(1, -1),
                            _rope_head_order(k_norm_g).reshape(1, -1), seq)
    attn = _flash(q_rot.reshape(batch, seq, -1), k_rot.reshape(batch, seq, -1), proj3, q_norm_g, k_norm_g)

    return _merge(y_f.reshape(n, -1), y_b.reshape(n, -1), xbc.reshape(n, CONV_CH), proj, attn.reshape(n, -1),
                  x2, jnp.repeat(d_skip, SSM_HEAD_DIM).reshape(1, -1), g_ssm.reshape(1, -1),
                  w_br_ssm.astype(BF16), w_br_attn.astype(BF16), w_out.astype(BF16))


def _moe_and_final_norm(x1, g_ffn, w_router, b_router, w_mlp1, b_mlp1, w_mlp2, b_mlp2, g_final):
    n, d = x1.shape
    tm = min(MOE_TILE, n)
    w_r = jnp.pad(w_router, ((0, 0), (0, LANES - N_EXPERTS)))
    w_hi = w_r.astype(BF16)
    w_lo = (w_r - w_hi.astype(F32)).astype(BF16)
    w_r = jnp.concatenate([w_hi, w_lo, w_hi], axis=0)
    b_r = jnp.pad(b_router, (0, LANES - N_EXPERTS), constant_values=-jnp.inf).reshape(1, LANES)
    g_ffn_row = g_ffn.reshape(1, d)
    top_idx, top_w, cnt = _router(x1, g_ffn_row, w_r, b_r)
    plan = _routing_plan(cnt[:, 0, :N_EXPERTS].astype(jnp.int32), n, tm, MOE_BLOCK)
    idx_t = jnp.transpose(top_idx[:, :8])
    xs = _dispatch(x1, g_ffn_row, idx_t, plan, tm)
    yb = _experts(xs, plan, w_mlp1, b_mlp1[:, None, :], w_mlp2, b_mlp2[:, None, :])
    return _combine(x1, yb, top_idx, top_w, plan, g_final.reshape(1, d), tm)


def kernel(x, g_mix, w_in, conv_w, conv_b, dt_bias_f, dt_bias_b, a_log_f, a_log_b, d_skip, g_ssm, q_norm_g,
           k_norm_g, w_br_ssm, w_br_attn, w_out, g_ffn, w_router, b_router, w_mlp1, b_mlp1, w_mlp2, b_mlp2,
           g_final):
    batch, seq, d = x.shape
    assert g_mix.shape[0] == 1, "single-layer model: the final rmsnorm is fused into the MoE combine"
    x2 = x.reshape(batch * seq, d)
    x1 = _token_mixer(x2, batch, seq, g_mix[0], w_in[0], conv_w[0], conv_b[0], dt_bias_f[0], dt_bias_b[0],
                      a_log_f[0], a_log_b[0], d_skip[0], g_ssm[0], q_norm_g[0], k_norm_g[0], w_br_ssm[0],
                      w_br_attn[0], w_out[0])
    out = _moe_and_final_norm(x1, g_ffn[0], w_router[0], b_router[0], w_mlp1[0], b_mlp1[0], w_mlp2[0],
                              b_mlp2[0], g_final)
    return out.reshape(batch, seq, d)
```

```python
import functools
import math

import jax
import jax.numpy as jnp
import numpy as np
from jax import lax
from jax.experimental import pallas as pl
from jax.experimental.pallas import tpu as pltpu

F32 = jnp.float32
BF16 = jnp.bfloat16

NORM_EPS = 1e-6
GRID_W = 64
SSM_HEADS = 16
SSM_HEAD_DIM = 64
SSM_INNER = SSM_HEADS * SSM_HEAD_DIM
SSM_GROUPS = 2
SSM_STATE = 128
SSM_CONV = 5
CONV_CH = SSM_INNER + 2 * SSM_GROUPS * SSM_STATE
ATTN_HEADS = 8
ATTN_KV_HEADS = 2
ATTN_HEAD_DIM = 128
ROPE_THETA = 10000.0
N_EXPERTS = 32
TOP_K = 4
SWIGLU_LIMIT = 7.0
SWIGLU_ALPHA = 1.702

LANES = 128
BF16_SUBLANES = 16
VMEM_LIMIT = 56 * 1024 * 1024

Z_OFF, Q_OFF, GATE_OFF, XBC_OFF = 0, 1024, 2048, 4096
K_OFF, V_OFF, PROJ_COLS = 5632, 5888, 6144

MOE_TILE = 512
MOE_BLOCK = 512
MOE_GRAN = BF16_SUBLANES


def _cparams(sem):
    return pltpu.CompilerParams(dimension_semantics=sem, vmem_limit_bytes=VMEM_LIMIT)


def _sigmoid(x):
    return 1.0 / (1.0 + jnp.exp(-x))


def _softplus(x):
    return jnp.maximum(x, 0.0) + jnp.log(1.0 + jnp.exp(-jnp.abs(x)))


def _rms(x, g):
    ms = jnp.mean(x * x, axis=-1, keepdims=True)
    return x * lax.rsqrt(ms + NORM_EPS) * g


def _inproj_kernel(x_ref, g_ref, w_ref, wdt_ref, o_ref, dt_ref, h_scr):
    @pl.when(pl.program_id(1) == 0)
    def _():
        hb = _rms(x_ref[...], g_ref[...]).astype(BF16)
        h_scr[...] = hb
        dt_ref[...] = jnp.dot(hb, wdt_ref[...], preferred_element_type=F32)

    o_ref[...] = jnp.dot(h_scr[...], w_ref[...], preferred_element_type=F32).astype(o_ref.dtype)


def _in_proj(x2, g_mix, w_main, w_dt):
    n, d = x2.shape
    tm = min(1024, n)
    tn = 2048
    return pl.pallas_call(
        _inproj_kernel,
        grid=(n // tm, PROJ_COLS // tn),
        in_specs=[
            pl.BlockSpec((tm, d), lambda i, j: (i, 0)),
            pl.BlockSpec((1, d), lambda i, j: (0, 0)),
            pl.BlockSpec((d, tn), lambda i, j: (0, j)),
            pl.BlockSpec((d, LANES), lambda i, j: (0, 0)),
        ],
        out_specs=[
            pl.BlockSpec((tm, tn), lambda i, j: (i, j)),
            pl.BlockSpec((tm, LANES), lambda i, j: (i, 0)),
        ],
        out_shape=[
            jax.ShapeDtypeStruct((n, PROJ_COLS), BF16),
            jax.ShapeDtypeStruct((n, LANES), F32),
        ],
        scratch_shapes=[pltpu.VMEM((tm, d), BF16)],
        compiler_params=_cparams(("arbitrary", "arbitrary")),
    )(x2, g_mix, w_main, w_dt)


def _conv_kernel(prev_ref, cur_ref, next_ref, w_ref, b_ref, o_ref, scr):
    s = pl.program_id(1)
    ts = cur_ref.shape[1]
    halo = BF16_SUBLANES
    pad = (SSM_CONV - 1) // 2
    prev = prev_ref[0].astype(F32)
    nxt = next_ref[0].astype(F32)
    scr[0:halo, :] = jnp.where(s == 0, 0.0, prev)
    scr[halo:halo + ts, :] = cur_ref[0].astype(F32)
    scr[halo + ts:2 * halo + ts, :] = jnp.where(s == pl.num_programs(1) - 1, 0.0, nxt)
    acc = jnp.zeros((ts, cur_ref.shape[2]), F32) + b_ref[...]
    for k in range(SSM_CONV):
        acc = acc + w_ref[k:k + 1, :] * scr[halo - pad + k:halo - pad + k + ts, :]
    o_ref[0] = (acc * _sigmoid(acc)).astype(o_ref.dtype)


def _conv(proj3, conv_w, conv_b):
    b, s, _ = proj3.shape
    ts = min(512, s)
    tc = 512
    halo = BF16_SUBLANES
    hb = ts // halo
    col0 = XBC_OFF // tc
    return pl.pallas_call(
        _conv_kernel,
        grid=(b, s // ts, CONV_CH // tc),
        in_specs=[
            pl.BlockSpec((1, halo, tc), lambda bi, si, ci: (bi, jnp.maximum(si * hb - 1, 0), col0 + ci)),
            pl.BlockSpec((1, ts, tc), lambda bi, si, ci: (bi, si, col0 + ci)),
            pl.BlockSpec((1, halo, tc),
                         lambda bi, si, ci: (bi, jnp.minimum((si + 1) * hb, s // halo - 1), col0 + ci)),
            pl.BlockSpec((SSM_CONV, tc), lambda bi, si, ci: (0, ci)),
            pl.BlockSpec((1, tc), lambda bi, si, ci: (0, ci)),
        ],
        out_specs=pl.BlockSpec((1, ts, tc), lambda bi, si, ci: (bi, si, ci)),
        out_shape=jax.ShapeDtypeStruct((b, s, CONV_CH), BF16),
        scratch_shapes=[pltpu.VMEM((ts + 2 * halo, tc), F32)],
        compiler_params=_cparams(("arbitrary", "arbitrary", "arbitrary")),
    )(proj3, proj3, proj3, conv_w, conv_b)


def _ssd_kernel(xf_ref, xb_ref, dtf_ref, dtb_ref, dttf_ref, dttb_ref, brow_ref, bcol_ref,
                arow_ref, acol_ref, yf_ref, yb_ref, st_ref):
    L = xf_ref.shape[1]
    hg = SSM_HEADS // SSM_GROUPS
    pairs = hg // 2

    @pl.when(pl.program_id(1) == 0)
    def _():
        st_ref[...] = jnp.zeros_like(st_ref)

    rows = lax.broadcasted_iota(jnp.int32, (L, L), 0)
    cols = lax.broadcasted_iota(jnp.int32, (L, L), 1)
    lower = rows >= cols
    upper = rows <= cols
    ltri = lower.astype(BF16)
    utri = upper.astype(BF16)
    lane = lax.broadcasted_iota(jnp.int32, (L, LANES), 1)
    first_half = lane < SSM_HEAD_DIM
    lane1 = lax.broadcasted_iota(jnp.int32, (1, LANES), 1)
    log2e = math.log2(math.e)
    a_row = -jnp.exp(arow_ref[...]) * log2e
    a_col = -jnp.exp(acol_ref[...]) * log2e

    def split3(v):
        hi = v.astype(BF16)
        r1 = v - hi.astype(F32)
        mid = r1.astype(BF16)
        return hi, mid, (r1 - mid.astype(F32)).astype(BF16)

    def cumsum_cols(tri, v):
        return jnp.dot(jnp.concatenate([tri] * 3, axis=1), jnp.concatenate(split3(v), axis=0),
                       preferred_element_type=F32)

    def cumsum_rows(v, tri):
        return jnp.dot(jnp.concatenate(split3(v), axis=1), jnp.concatenate([tri] * 3, axis=0),
                       preferred_element_type=F32)

    for d in range(2):
        x_ref, dt_ref, dtt_ref, y_ref = ((xf_ref, dtf_ref, dttf_ref, yf_ref) if d == 0
                                         else (xb_ref, dtb_ref, dttb_ref, yb_ref))
        a = _softplus(dt_ref[0] + brow_ref[...]) * a_row
        dt_t = _softplus(dtt_ref[0] + bcol_ref[...])
        a_t = dt_t * a_col
        if d == 0:
            cs_col = cumsum_cols(ltri, a)
            cs_row = cumsum_rows(a_t, utri)
            tot = cs_col[L - 1:L, :]
            tot_t = cs_row[:, L - 1:L]
            mask = lower
        else:
            cs_col = cumsum_cols(utri, a)
            cs_row = cumsum_rows(a_t, ltri)
            tot = cs_col[0:1, :]
            tot_t = cs_row[:, 0:1]
            mask = upper
        w_t = dt_t * jnp.exp2(tot_t - cs_row)
        src_t = cs_row - jnp.log2(dt_t)
        chunk_decay = jnp.exp2(tot)

        for g in range(SSM_GROUPS):
            boff = SSM_INNER + g * SSM_STATE
            coff = SSM_INNER + SSM_GROUPS * SSM_STATE + g * SSM_STATE
            bm = x_ref[0, :, boff:boff + SSM_STATE]
            cm = x_ref[0, :, coff:coff + SSM_STATE]
            cb = lax.dot_general(cm, bm, (((1,), (1,)), ((), ())), preferred_element_type=F32)
            bt = bm.astype(F32).T
            st = st_ref[d, g]
            y_off = jnp.dot(cm, st.astype(BF16), preferred_element_type=F32)
            for pr in range(pairs):
                h0 = d * SSM_HEADS + g * hg + 2 * pr
                xoff = (g * pairs + pr) * LANES
                xs = x_ref[0, :, xoff:xoff + LANES]
                zero = jnp.zeros_like(xs)
                rhs = jnp.concatenate([jnp.where(first_half, xs, zero),
                                       jnp.where(first_half, zero, xs)], axis=0)
                ms, ws, dins = [], [], []
                for hh in (h0, h0 + 1):
                    cs_b = jnp.broadcast_to(cs_col[:, hh:hh + 1], (L, L))
                    seg = cs_b - src_t[hh:hh + 1, :]
                    m = cb * jnp.exp2(jnp.where(mask, seg, -jnp.inf))
                    ms.append(m.astype(BF16))
                    ws.append((bt * w_t[hh:hh + 1, :]).astype(BF16))
                    dins.append(jnp.exp2(cs_b))
                y = jnp.dot(jnp.concatenate(ms, axis=1), rhs, preferred_element_type=F32)
                y = y + y_off[:, pr * LANES:(pr + 1) * LANES] * jnp.where(first_half, dins[0], dins[1])
                y_ref[0, :, xoff:xoff + LANES] = y.astype(y_ref.dtype)
                new_st = jnp.dot(jnp.concatenate(ws, axis=1), rhs, preferred_element_type=F32)
                cd = jnp.where(lane1 < SSM_HEAD_DIM, chunk_decay[:, h0:h0 + 1], chunk_decay[:, h0 + 1:h0 + 2])
                st_ref[d, g, :, pr * LANES:(pr + 1) * LANES] = st[:, pr * LANES:(pr + 1) * LANES] * cd + new_st


def _ssd(xbc, dt3, dtt3, bias_row, bias_col, alog_row, alog_col):
    b, s, _ = xbc.shape
    L = min(128, s)
    nc = s // L
    hg = SSM_HEADS // SSM_GROUPS
    fwd = lambda bi, ci: (bi, ci, 0)
    bwd = lambda bi, ci: (bi, nc - 1 - ci, 0)
    fwd_t = lambda bi, ci: (bi, 0, ci)
    bwd_t = lambda bi, ci: (bi, 0, nc - 1 - ci)
    const = lambda bi, ci: (0, 0)
    return pl.pallas_call(
        _ssd_kernel,
        grid=(b, nc),
        in_specs=[
            pl.BlockSpec((1, L, CONV_CH), fwd),
            pl.BlockSpec((1, L, CONV_CH), bwd),
            pl.BlockSpec((1, L, LANES), fwd),
            pl.BlockSpec((1, L, LANES), bwd),
            pl.BlockSpec((1, 2 * SSM_HEADS, L), fwd_t),
            pl.BlockSpec((1, 2 * SSM_HEADS, L), bwd_t),
            pl.BlockSpec((1, LANES), const),
            pl.BlockSpec((2 * SSM_HEADS, 1), const),
            pl.BlockSpec((1, LANES), const),
            pl.BlockSpec((2 * SSM_HEADS, 1), const),
        ],
        out_specs=[
            pl.BlockSpec((1, L, SSM_INNER), fwd),
            pl.BlockSpec((1, L, SSM_INNER), bwd),
        ],
        out_shape=[jax.ShapeDtypeStruct((b, s, SSM_INNER), BF16)] * 2,
        scratch_shapes=[pltpu.VMEM((2, SSM_GROUPS, SSM_STATE, hg * SSM_HEAD_DIM), F32)],
        compiler_params=_cparams(("arbitrary", "arbitrary")),
    )(xbc, xbc, dt3, dt3, dtt3, dtt3, bias_row, bias_col, alog_row, alog_col)


def _rope_head_order(a):
    q4 = ATTN_HEAD_DIM // 4
    return jnp.concatenate([a[..., 0:q4], a[..., 2 * q4:3 * q4], a[..., q4:2 * q4], a[..., 3 * q4:]], axis=-1)


def _rope_norm(t, g, cos, sin_signed, ones):
    sq = t * t
    hi = sq.astype(BF16)
    lo = (sq - hi.astype(F32)).astype(BF16)
    ms = jnp.dot(jnp.concatenate([hi, lo], axis=1), ones, preferred_element_type=F32) * (1.0 / ATTN_HEAD_DIM)
    tn = t * lax.rsqrt(ms + NORM_EPS) * g
    return tn * cos + pltpu.roll(tn, ATTN_HEAD_DIM // 2, 1) * sin_signed


Q_SCALE = ATTN_HEAD_DIM ** -0.5 * math.log2(math.e)


def _qkprep_kernel(q_ref, k_ref, cos_ref, sin_ref, qg_ref, kg_ref, qo_ref, ko_ref):
    cos = cos_ref[...]
    sin = sin_ref[...]
    ones = jnp.ones((2 * ATTN_HEAD_DIM, ATTN_HEAD_DIM), BF16)

    def heads(src_ref, g_ref, dst_ref, n_heads, scale):
        for h in range(n_heads):
            sl = slice(h * ATTN_HEAD_DIM, (h + 1) * ATTN_HEAD_DIM)
            r = _rope_norm(src_ref[:, sl].astype(F32), g_ref[...], cos, sin, ones) * scale
            dst_ref[:, sl] = r.astype(dst_ref.dtype)

    heads(q_ref, qg_ref, qo_ref, ATTN_HEADS, Q_SCALE)
    heads(k_ref, kg_ref, ko_ref, ATTN_KV_HEADS, 1.0)


def _qk_prep(proj, cos_t, sin_t, q_norm_g, k_norm_g, seq):
    n = proj.shape[0]
    tm = min(512, seq)
    qw = ATTN_HEADS * ATTN_HEAD_DIM
    kw = ATTN_KV_HEADS * ATTN_HEAD_DIM
    spt = seq // tm
    return pl.pallas_call(
        _qkprep_kernel,
        grid=(n // tm,),
        in_specs=[
            pl.BlockSpec((tm, qw), lambda i: (i, Q_OFF // qw)),
            pl.BlockSpec((tm, kw), lambda i: (i, K_OFF // kw)),
            pl.BlockSpec((tm, ATTN_HEAD_DIM), lambda i: (i % spt, 0)),
            pl.BlockSpec((tm, ATTN_HEAD_DIM), lambda i: (i % spt, 0)),
            pl.BlockSpec((1, ATTN_HEAD_DIM), lambda i: (0, 0)),
            pl.BlockSpec((1, ATTN_HEAD_DIM), lambda i: (0, 0)),
        ],
        out_specs=[
            pl.BlockSpec((tm, qw), lambda i: (i, 0)),
            pl.BlockSpec((tm, kw), lambda i: (i, 0)),
        ],
        out_shape=[jax.ShapeDtypeStruct((n, qw), BF16), jax.ShapeDtypeStruct((n, kw), BF16)],
        compiler_params=_cparams(("arbitrary",)),
    )(proj, proj, cos_t, sin_t, q_norm_g, k_norm_g)


def _flash_kernel(small_ref, q_ref, k_ref, vt_ref, o_ref, *, tk):
    q = q_ref[0]
    tq = q.shape[0]
    seq = k_ref.shape[1]
    nt = (((1,), (1,)), ((), ()))
    small = small_ref[0] != 0

    @pl.when(small)
    def _():
        p_t = jnp.exp2(lax.dot_general(k_ref[0], q, nt, preferred_element_type=F32))
        l = jnp.sum(p_t, axis=0, keepdims=True)
        acc_t = jnp.dot(vt_ref[0], p_t.astype(BF16), preferred_element_type=F32)
        o_ref[0] = (acc_t / l).T.astype(o_ref.dtype)

    @pl.when(jnp.logical_not(small))
    def _():
        def body(i, carry):
            m, l, acc_t = carry
            off = pl.multiple_of(i * tk, tk)
            s_t = lax.dot_general(k_ref[0, pl.ds(off, tk), :], q, nt, preferred_element_type=F32)
            m_new = jnp.maximum(m, jnp.max(s_t, axis=0, keepdims=True))
            alpha = jnp.exp2(m - m_new)
            p_t = jnp.exp2(s_t - m_new)
            l = alpha * l + jnp.sum(p_t, axis=0, keepdims=True)
            acc_t = alpha * acc_t + jnp.dot(vt_ref[0, :, pl.ds(off, tk)], p_t.astype(BF16),
                                            preferred_element_type=F32)
            return m_new, l, acc_t

        init = (jnp.full((1, tq), -jnp.inf, F32), jnp.zeros((1, tq), F32),
                jnp.zeros((ATTN_HEAD_DIM, tq), F32))
        _, l, acc_t = lax.fori_loop(0, seq // tk, body, init)
        o_ref[0] = (acc_t / l).T.astype(o_ref.dtype)


SCORE_BOUND = 59.0


def _flash(q3, k3, vt3, q_norm_g, k_norm_g):
    b, s, _ = q3.shape
    tq = min(512, s)
    tk = min(2048, s)
    nq = s // tq
    grp = ATTN_HEADS // ATTN_KV_HEADS
    hd = ATTN_HEAD_DIM
    bound = hd * Q_SCALE * jnp.max(jnp.abs(q_norm_g)) * jnp.max(jnp.abs(k_norm_g)) * 1.02
    small = (bound <= SCORE_BOUND).astype(jnp.int32).reshape(1)
    grid_spec = pltpu.PrefetchScalarGridSpec(
        num_scalar_prefetch=1,
        grid=(b, ATTN_HEADS, nq),
        in_specs=[
            pl.BlockSpec((1, tq, hd), lambda bi, h, qi, sm: (bi, qi, h)),
            pl.BlockSpec((1, s, hd), lambda bi, h, qi, sm: (bi, 0, h // grp)),
            pl.BlockSpec((1, hd, s), lambda bi, h, qi, sm: (bi, h // grp, 0)),
        ],
        out_specs=pl.BlockSpec((1, tq, hd), lambda bi, h, qi, sm: (bi, qi, h)),
    )
    return pl.pallas_call(
        functools.partial(_flash_kernel, tk=tk),
        grid_spec=grid_spec,
        out_shape=jax.ShapeDtypeStruct((b, s, ATTN_HEADS * hd), BF16),
        compiler_params=_cparams(("arbitrary", "arbitrary", "arbitrary")),
    )(small, q3, k3, vt3)


def _merge_kernel(yf_ref, yb_ref, xs_ref, z_ref, gate_ref, attn_ref, x_ref, dskip_ref, gssm_ref,
                  wbs_ref, wba_ref, wo_ref, x1_ref):
    d = x_ref.shape[1]
    xs = xs_ref[...].astype(F32)
    y = yf_ref[...].astype(F32) + yb_ref[...].astype(F32) + xs * dskip_ref[...]
    z = z_ref[...].astype(F32)
    y = _rms(y * (z * _sigmoid(z)), gssm_ref[...])
    br_ssm = jnp.dot(y.astype(BF16), wbs_ref[...], preferred_element_type=F32)
    br_attn = jnp.dot(attn_ref[...], wba_ref[...], preferred_element_type=F32)
    g_s = _sigmoid(gate_ref[:, :d].astype(F32))
    g_a = _sigmoid(gate_ref[:, d:].astype(F32))
    merged = (g_s * br_ssm + g_a * br_attn).astype(BF16)
    x1_ref[...] = x_ref[...] + jnp.dot(merged, wo_ref[...], preferred_element_type=F32)


def _merge(y_f, y_b, xbc, proj, attn, x2, dskip_row, g_ssm, w_br_ssm, w_br_attn, w_out):
    n, d = x2.shape
    tm = min(256, n)
    row = lambda i: (i, 0)
    const = lambda i: (0, 0)
    return pl.pallas_call(
        _merge_kernel,
        grid=(n // tm,),
        in_specs=[
            pl.BlockSpec((tm, d), row),
            pl.BlockSpec((tm, d), row),
            pl.BlockSpec((tm, d), row),
            pl.BlockSpec((tm, d), lambda i: (i, Z_OFF // d)),
            pl.BlockSpec((tm, 2 * d), lambda i: (i, GATE_OFF // (2 * d))),
            pl.BlockSpec((tm, d), row),
            pl.BlockSpec((tm, d), row),
            pl.BlockSpec((1, d), const),
            pl.BlockSpec((1, d), const),
            pl.BlockSpec((d, d), const),
            pl.BlockSpec((d, d), const),
            pl.BlockSpec((d, d), const),
        ],
        out_specs=pl.BlockSpec((tm, d), row),
        out_shape=jax.ShapeDtypeStruct((n, d), F32),
        compiler_params=_cparams(("arbitrary",)),
    )(y_f, y_b, xbc, proj, proj, attn, x2, dskip_row, g_ssm, w_br_ssm, w_br_attn, w_out)


def _router_kernel(x1_ref, g_ref, w_ref, b_ref, idx_ref, wt_ref, cnt_ref):
    h = _rms(x1_ref[...], g_ref[...])
    h_hi = h.astype(BF16)
    h_lo = (h - h_hi.astype(F32)).astype(BF16)
    logits = jnp.dot(jnp.concatenate([h_hi, h_hi, h_lo], axis=1), w_ref[...],
                     preferred_element_type=F32) + b_ref[...]
    lane = lax.broadcasted_iota(jnp.int32, logits.shape, 1)
    idx_out = jnp.zeros(logits.shape, jnp.int32)
    val_out = jnp.zeros(logits.shape, F32)
    chosen = jnp.zeros(logits.shape, F32)
    vals = []
    for k in range(TOP_K):
        m = jnp.max(logits, axis=-1, keepdims=True)
        idx = jnp.min(jnp.where(logits == m, lane, LANES), axis=-1, keepdims=True)
        idx_out = jnp.where(lane == k, idx, idx_out)
        vals.append(m)
        hit = lane == idx
        chosen = jnp.where(hit, 1.0, chosen)
        logits = jnp.where(hit, -jnp.inf, logits)
    es = [jnp.exp(v - vals[0]) for v in vals]
    tot = es[0] + es[1] + es[2] + es[3]
    for k in range(TOP_K):
        val_out = jnp.where(lane == k, es[k] / tot, val_out)
    idx_ref[...] = idx_out
    wt_ref[...] = val_out
    cnt_ref[0] = jnp.broadcast_to(jnp.sum(chosen, axis=0, keepdims=True), cnt_ref.shape[1:])


def _router(x1, g_ffn, w_router_pad, b_router_pad):
    n, d = x1.shape
    tm = min(MOE_TILE, n)
    return pl.pallas_call(
        _router_kernel,
        grid=(n // tm,),
        in_specs=[
            pl.BlockSpec((tm, d), lambda i: (i, 0)),
            pl.BlockSpec((1, d), lambda i: (0, 0)),
            pl.BlockSpec((3 * d, LANES), lambda i: (0, 0)),
            pl.BlockSpec((1, LANES), lambda i: (0, 0)),
        ],
        out_specs=[pl.BlockSpec((tm, LANES), lambda i: (i, 0)),
                   pl.BlockSpec((tm, LANES), lambda i: (i, 0)),
                   pl.BlockSpec((1, 8, LANES), lambda i: (i, 0, 0))],
        out_shape=[jax.ShapeDtypeStruct((n, LANES), jnp.int32),
                   jax.ShapeDtypeStruct((n, LANES), F32),
                   jax.ShapeDtypeStruct((n // tm, 8, LANES), F32)],
        compiler_params=_cparams(("arbitrary",)),
    )(x1, g_ffn, w_router_pad, b_router_pad)


def _routing_plan(cnt, n, tm, tb):
    nt = n // tm
    gran = MOE_GRAN
    rc = (cnt + gran - 1) // gran * gran
    covered = jnp.sum(rc, axis=0)
    region = (covered + tb - 1) // tb * tb
    pad_end = jnp.cumsum(region)
    pad_start = pad_end - region
    seg_start = pad_start[None, :] + jnp.cumsum(rc, axis=0) - rc
    stage_off = jnp.cumsum(rc, axis=1) - rc
    n_used = pad_end[-1] // tb
    nb = (TOP_K * n + nt * N_EXPERTS * (gran - 1) + N_EXPERTS * (tb - 1) + tb - 1) // tb
    blk = jnp.arange(nb, dtype=jnp.int32)
    block_e = jnp.sum((pad_end[None, :] <= (jnp.minimum(blk, n_used - 1) * tb)[:, None]).astype(jnp.int32), axis=1)
    block_e = jnp.minimum(block_e, N_EXPERTS - 1)
    flat = lambda a: a.reshape(-1).astype(jnp.int32)
    tail_gran = (region - covered) // gran
    ngran = rc // gran
    g_end = jnp.cumsum(ngran, axis=1)
    gi = jnp.arange(_stage_rows(tm) // gran, dtype=jnp.int32)
    g_exp = jnp.minimum(jnp.sum((g_end[:, None, :] <= gi[None, :, None]).astype(jnp.int32), axis=2), N_EXPERTS - 1)
    mine = g_exp[:, :, None] == jnp.arange(N_EXPERTS, dtype=jnp.int32)[None, None, :]
    seg_base = seg_start - gran * (g_end - ngran)
    gran_dst = jnp.sum(jnp.where(mine, seg_base[:, None, :], 0), axis=2) + gran * gi[None, :]
    ids = jnp.arange(N_EXPERTS, dtype=jnp.int32)
    later = jnp.where((ids[None, :] > ids[:, None]) & (region[None, :] > 0), ids[None, :], N_EXPERTS)
    nxt = jnp.min(later, axis=1)
    next_expert = jnp.where(nxt < N_EXPERTS, nxt, ids)
    return dict(gran_dst=flat(gran_dst), tot_gran=flat(g_end[:, -1]), block_e=flat(block_e),
                next_expert=flat(next_expert),
                n_used=flat(n_used), nb=nb, stage_off_f=stage_off.astype(F32),
                tail_start=flat(pad_start + covered), tail_gran=flat(tail_gran),
                tot_tail=flat(jnp.sum(tail_gran)))


def _stage_rows(tm):
    rows = TOP_K * tm + N_EXPERTS * (MOE_GRAN - 1)
    return (rows + 2 * LANES - 1) // (2 * LANES) * (2 * LANES)


GRAN_UNROLL = 4


def _granule_copies(t, gdst_ref, totg_ref, make_copy, start):
    total = totg_ref[t]
    base = t * (gdst_ref.shape[0] // totg_ref.shape[0])
    full = lax.shift_right_logical(total, GRAN_UNROLL.bit_length() - 1)

    def one(i):
        make_copy(pl.multiple_of(i * MOE_GRAN, MOE_GRAN), pl.multiple_of(gdst_ref[base + i], MOE_GRAN),
                  MOE_GRAN).start()

    def group(q, c):
        if start:
            for u in range(GRAN_UNROLL):
                one(q * GRAN_UNROLL + u)
        else:
            make_copy(0, 0, GRAN_UNROLL * MOE_GRAN).wait()
        return c

    def rest(i, c):
        if start:
            one(i)
        else:
            make_copy(0, 0, MOE_GRAN).wait()
        return c

    lax.fori_loop(0, full, group, 0)
    lax.fori_loop(full * GRAN_UNROLL, total, rest, 0)


def _dispatch_kernel(gdst_ref, totg_ref, tstart_ref, tgran_ref, misc_ref,
                     x1_ref, g_ref, idxt_ref, soffc_ref, xs_hbm, stage, zbuf, sem, zsem, *, n_blocks):
    t = pl.program_id(0)
    slot = t % 2
    tm = x1_ref.shape[0]
    rows = stage.shape[1]
    h = _rms(x1_ref[...], g_ref[...]).astype(BF16)
    idxt = idxt_ref[...]
    expert = lax.broadcasted_iota(jnp.int32, (LANES, tm), 0)
    hots = [idxt[k:k + 1, :] == expert for k in range(TOP_K)]
    multi = jnp.zeros((LANES, tm), F32)
    for hot in hots:
        multi = jnp.where(hot, 1.0, multi)
    earlier = (lax.broadcasted_iota(jnp.int32, (tm, tm), 0)
               < lax.broadcasted_iota(jnp.int32, (tm, tm), 1)).astype(BF16)
    rank = jnp.dot(multi.astype(BF16), earlier, preferred_element_type=F32)
    pos = rank + soffc_ref[0]
    srows = [jnp.sum(jnp.where(hot, pos, 0.0), axis=0, keepdims=True) for hot in hots]
    chunk = 2 * LANES
    row_id = lax.broadcasted_iota(jnp.int32, (chunk, tm), 0).astype(F32)
    for c in range(rows // chunk):
        perm = jnp.zeros((chunk, tm), F32)
        for srow in srows:
            perm = jnp.where(row_id == srow - float(c * chunk), 1.0, perm)
        stage[slot, c * chunk:(c + 1) * chunk, :] = jnp.dot(
            perm.astype(BF16), h, preferred_element_type=F32).astype(stage.dtype)

    def copy_from(s):
        def copy(stage_row, sorted_row, nrows):
            return pltpu.make_async_copy(stage.at[s, pl.ds(stage_row, nrows)],
                                         xs_hbm.at[pl.ds(sorted_row, nrows)], sem.at[s])
        return copy

    _granule_copies(t, gdst_ref, totg_ref, copy_from(slot), start=True)

    @pl.when(t > 0)
    def _():
        _granule_copies(jnp.maximum(t - 1, 0), gdst_ref, totg_ref, copy_from(1 - slot), start=False)

    @pl.when(t == pl.num_programs(0) - 1)
    def _():
        _granule_copies(t, gdst_ref, totg_ref, copy_from(slot), start=False)
        tb = zbuf.shape[0]
        n_used = misc_ref[0]
        zbuf[...] = jnp.zeros_like(zbuf)

        def zero_gran(row):
            return pltpu.make_async_copy(zbuf.at[pl.ds(0, MOE_GRAN)], xs_hbm.at[pl.ds(row, MOE_GRAN)], zsem)

        def zero_block(blk):
            return pltpu.make_async_copy(zbuf, xs_hbm.at[pl.ds(pl.multiple_of(blk * tb, tb), tb)], zsem)

        def tails(e, c):
            def one(g, c2):
                zero_gran(pl.multiple_of(tstart_ref[e] + g * MOE_GRAN, MOE_GRAN)).start()
                return c2
            lax.fori_loop(0, tgran_ref[e], one, 0)
            return c

        def start_block(blk, c):
            zero_block(blk).start()
            return c

        def wait_gran(g, c):
            zero_gran(0).wait()
            return c

        def wait_block(blk, c):
            zero_block(0).wait()
            return c

        lax.fori_loop(0, N_EXPERTS, tails, 0)
        lax.fori_loop(n_used, n_blocks, start_block, 0)
        lax.fori_loop(0, misc_ref[1], wait_gran, 0)
        lax.fori_loop(n_used, n_blocks, wait_block, 0)


def _dispatch(x1, g_ffn, idx_t, plan, tm):
    n, d = x1.shape
    nt = n // tm
    rows = _stage_rows(tm)
    soff_col = plan["stage_off_f"].reshape(nt, N_EXPERTS, 1)
    soff_col = jnp.pad(soff_col, ((0, 0), (0, LANES - N_EXPERTS), (0, 0)))
    misc = jnp.concatenate([plan["n_used"], plan["tot_tail"]])
    grid_spec = pltpu.PrefetchScalarGridSpec(
        num_scalar_prefetch=5,
        grid=(nt,),
        in_specs=[
            pl.BlockSpec((tm, d), lambda t, *_: (t, 0)),
            pl.BlockSpec((1, d), lambda t, *_: (0, 0)),
            pl.BlockSpec((8, tm), lambda t, *_: (0, t)),
            pl.BlockSpec((1, LANES, 1), lambda t, *_: (t, 0, 0)),
        ],
        out_specs=pl.BlockSpec(memory_space=pl.ANY),
        scratch_shapes=[pltpu.VMEM((2, rows, d), BF16), pltpu.VMEM((MOE_BLOCK, d), BF16),
                        pltpu.SemaphoreType.DMA((2,)), pltpu.SemaphoreType.DMA(())],
    )
    return pl.pallas_call(
        functools.partial(_dispatch_kernel, n_blocks=plan["nb"]),
        grid_spec=grid_spec,
        out_shape=jax.ShapeDtypeStruct((plan["nb"] * MOE_BLOCK, d), BF16),
        compiler_params=_cparams(("arbitrary",)),
    )(plan["gran_dst"], plan["tot_gran"], plan["tail_start"], plan["tail_gran"], misc, x1, g_ffn, idx_t,
      soff_col)


def _expert_kernel(be_ref, nused_ref, nexte_ref, x_ref, w1_hbm, b1_ref, w2_hbm, b2_ref, o_ref,
                   w1f, w2f, w1b, w2b, sem):
    de = w2f.shape[0]
    j = pl.program_id(0)
    live = j < nused_ref[0]
    e = be_ref[j]
    new_expert = jnp.logical_or(j == 0, e != be_ref[jnp.maximum(j - 1, 0)])

    def fetch(ex):
        return (pltpu.make_async_copy(w1_hbm.at[ex], w1f, sem.at[0]),
                pltpu.make_async_copy(w2_hbm.at[ex], w2f, sem.at[1]))

    @pl.when(jnp.logical_and(live, j == 0))
    def _():
        for c in fetch(e):
            c.start()

    @pl.when(jnp.logical_and(live, new_expert))
    def _():
        for c in fetch(e):
            c.wait()
        w1b[...] = w1f[...].astype(BF16)
        w2b[...] = w2f[...].astype(BF16)
        nxt = nexte_ref[e]

        @pl.when(nxt != e)
        def _():
            for c in fetch(nxt):
                c.start()

    @pl.when(live)
    def _():
        hm = jnp.dot(x_ref[...], w1b[...], preferred_element_type=F32) + b1_ref[0]
        gate = jnp.minimum(hm[:, :de], SWIGLU_LIMIT)
        up = jnp.clip(hm[:, de:], -SWIGLU_LIMIT, SWIGLU_LIMIT)
        act = gate * _sigmoid(SWIGLU_ALPHA * gate) * (up + 1.0)
        y = jnp.dot(act.astype(BF16), w2b[...], preferred_element_type=F32) + b2_ref[0]
        o_ref[...] = y.astype(o_ref.dtype)

    @pl.when(jnp.logical_not(live))
    def _():
        o_ref[...] = jnp.zeros_like(o_ref)


def _experts(xs, plan, w1, b1, w2, b2):
    d = xs.shape[1]
    tb = MOE_BLOCK
    f2 = w1.shape[2]
    de = w2.shape[1]
    last = lambda j, nu: jnp.maximum(jnp.minimum(j, nu[0] - 1), 0)
    grid_spec = pltpu.PrefetchScalarGridSpec(
        num_scalar_prefetch=3,
        grid=(plan["nb"],),
        in_specs=[
            pl.BlockSpec((tb, d), lambda j, be, nu, ne: (last(j, nu), 0)),
            pl.BlockSpec(memory_space=pl.ANY),
            pl.BlockSpec((1, 1, f2), lambda j, be, nu, ne: (be[j], 0, 0)),
            pl.BlockSpec(memory_space=pl.ANY),
            pl.BlockSpec((1, 1, d), lambda j, be, nu, ne: (be[j], 0, 0)),
        ],
        out_specs=pl.BlockSpec((tb, d), lambda j, be, nu, ne: (j, 0)),
        scratch_shapes=[pltpu.VMEM((d, f2), F32), pltpu.VMEM((de, d), F32),
                        pltpu.VMEM((d, f2), BF16), pltpu.VMEM((de, d), BF16),
                        pltpu.SemaphoreType.DMA((2,))],
    )
    return pl.pallas_call(
        _expert_kernel,
        grid_spec=grid_spec,
        out_shape=jax.ShapeDtypeStruct(xs.shape, BF16),
        compiler_params=_cparams(("arbitrary",)),
    )(plan["block_e"], plan["n_used"], plan["next_expert"], xs, w1, b1, w2, b2)


def _combine_kernel(gdst_ref, totg_ref, x1_ref, idx_ref, wt_ref, soffr_ref, g_ref, yb_hbm, o_ref, stage, sem):
    t = pl.program_id(0)
    slot = t % 2
    tm = x1_ref.shape[0]
    rows = stage.shape[1]

    def copy_into(s):
        def copy(stage_row, sorted_row, nrows):
            return pltpu.make_async_copy(yb_hbm.at[pl.ds(sorted_row, nrows)],
                                         stage.at[s, pl.ds(stage_row, nrows)], sem.at[s])
        return copy

    @pl.when(t == 0)
    def _():
        stage[...] = jnp.zeros_like(stage)
        _granule_copies(t, gdst_ref, totg_ref, copy_into(slot), start=True)

    @pl.when(t + 1 < pl.num_programs(0))
    def _():
        _granule_copies(t + 1, gdst_ref, totg_ref, copy_into(1 - slot), start=True)

    idx = idx_ref[...]
    wt = wt_ref[...]
    expert = lax.broadcasted_iota(jnp.int32, (tm, LANES), 1)
    hots = [idx[:, k:k + 1] == expert for k in range(TOP_K)]
    multi = jnp.zeros((tm, LANES), F32)
    for hot in hots:
        multi = jnp.where(hot, 1.0, multi)
    earlier = (lax.broadcasted_iota(jnp.int32, (tm, tm), 0)
               > lax.broadcasted_iota(jnp.int32, (tm, tm), 1)).astype(BF16)
    rank = jnp.dot(earlier, multi.astype(BF16), preferred_element_type=F32)
    pos = rank + soffr_ref[0]
    srows = [jnp.sum(jnp.where(hot, pos, 0.0), axis=-1, keepdims=True) for hot in hots]
    chunk = LANES
    row_id = lax.broadcasted_iota(jnp.int32, (chunk, rows), 1).astype(F32)
    for c in range(tm // chunk):
        sl = slice(c * chunk, (c + 1) * chunk)
        unsort = jnp.zeros((chunk, rows), F32)
        for k, srow in enumerate(srows):
            unsort = jnp.where(row_id == srow[sl], wt[sl, k:k + 1], unsort)
        if c == 0:
            _granule_copies(t, gdst_ref, totg_ref, copy_into(slot), start=False)
        y = jnp.dot(unsort.astype(BF16), stage[slot], preferred_element_type=F32)
        o_ref[sl, :] = _rms(x1_ref[sl, :] + y, g_ref[...])


def _combine(x1, yb, top_idx, top_w, plan, g_final, tm):
    n, d = x1.shape
    nt = n // tm
    rows = _stage_rows(tm)
    soff_row = jnp.pad(plan["stage_off_f"], ((0, 0), (0, LANES - N_EXPERTS))).reshape(nt, 1, LANES)
    grid_spec = pltpu.PrefetchScalarGridSpec(
        num_scalar_prefetch=2,
        grid=(nt,),
        in_specs=[
            pl.BlockSpec((tm, d), lambda t, *_: (t, 0)),
            pl.BlockSpec((tm, LANES), lambda t, *_: (t, 0)),
            pl.BlockSpec((tm, LANES), lambda t, *_: (t, 0)),
            pl.BlockSpec((1, 1, LANES), lambda t, *_: (t, 0, 0)),
            pl.BlockSpec((1, d), lambda t, *_: (0, 0)),
            pl.BlockSpec(memory_space=pl.ANY),
        ],
        out_specs=pl.BlockSpec((tm, d), lambda t, *_: (t, 0)),
        scratch_shapes=[pltpu.VMEM((2, rows, d), BF16), pltpu.SemaphoreType.DMA((2,))],
    )
    return pl.pallas_call(
        _combine_kernel,
        grid_spec=grid_spec,
        out_shape=jax.ShapeDtypeStruct((n, d), F32),
        compiler_params=_cparams(("arbitrary",)),
    )(plan["gran_dst"], plan["tot_gran"], x1, top_idx, top_w, soff_row, g_final, yb)


def _rope_tables(seq):
    half = ATTN_HEAD_DIM // 2
    inv_freq = np.float32(ROPE_THETA) ** (-np.arange(0, half, 2, dtype=np.float32) / np.float32(half))
    ang_row = np.arange(seq // GRID_W, dtype=np.float32)[:, None] * inv_freq
    ang_col = np.arange(GRID_W, dtype=np.float32)[:, None] * inv_freq
    by_row = lambda t: jnp.repeat(jnp.asarray(t, F32), GRID_W, axis=0)
    by_col = lambda t: jnp.tile(jnp.asarray(t, F32), (seq // GRID_W, 1))
    cos_r, sin_r = by_row(np.cos(ang_row)), by_row(np.sin(ang_row))
    cos_c, sin_c = by_col(np.cos(ang_col)), by_col(np.sin(ang_col))
    cos_t = jnp.concatenate([cos_r, cos_c] * 2, axis=-1)
    sin_t = jnp.concatenate([-sin_r, -sin_c, sin_r, sin_c], axis=-1)
    return cos_t, sin_t


def _token_mixer(x2, batch, seq, g_mix, w_in, conv_w, conv_b, dt_bias_f, dt_bias_b, a_log_f, a_log_b, d_skip,
                 g_ssm, q_norm_g, k_norm_g, w_br_ssm, w_br_attn, w_out):
    n, d = x2.shape
    z_end = SSM_INNER
    xbc_end = z_end + CONV_CH
    dtf_end = xbc_end + SSM_HEADS
    dtb_end = dtf_end + SSM_HEADS
    q_end = dtb_end + ATTN_HEADS * ATTN_HEAD_DIM
    k_end = q_end + ATTN_KV_HEADS * ATTN_HEAD_DIM
    v_end = k_end + ATTN_KV_HEADS * ATTN_HEAD_DIM
    head_cols = lambda w: _rope_head_order(w.reshape(d, -1, ATTN_HEAD_DIM)).reshape(d, -1)
    w_main = jnp.concatenate([w_in[:, :z_end], head_cols(w_in[:, dtb_end:q_end]), w_in[:, v_end:],
                              w_in[:, z_end:xbc_end], head_cols(w_in[:, q_end:k_end]), w_in[:, k_end:v_end]],
                             axis=1).astype(BF16)
    w_dt = jnp.pad(w_in[:, xbc_end:dtb_end], ((0, 0), (0, LANES - 2 * SSM_HEADS))).astype(BF16)

    proj, dt = _in_proj(x2, g_mix.reshape(1, d), w_main, w_dt)
    proj3 = proj.reshape(batch, seq, PROJ_COLS)

    xbc = _conv(proj3, conv_w, conv_b.reshape(1, CONV_CH))

    dt3 = dt.reshape(batch, seq, LANES)
    dtt3 = jnp.swapaxes(dt3[:, :, :2 * SSM_HEADS], 1, 2)
    bias = jnp.concatenate([dt_bias_f, dt_bias_b])
    alog = jnp.concatenate([a_log_f, a_log_b])
    pad_row = lambda v: jnp.pad(v, (0, LANES - 2 * SSM_HEADS)).reshape(1, LANES)
    y_f, y_b = _ssd(xbc, dt3, dtt3, pad_row(bias), bias.reshape(-1, 1), pad_row(alog), alog.reshape(-1, 1))

    cos_t, sin_t = _rope_tables(seq)
    q_rot, k_rot = _qk_prep(proj, cos_t, sin_t, _rope_head_order(q_norm_g).reshape(1, -1),
                            _rope_head_order(k_norm_g).reshape(1, -1), seq)
    vt3 = jnp.swapaxes(proj3[:, :, V_OFF:V_OFF + ATTN_KV_HEADS * ATTN_HEAD_DIM], 1, 2)
    attn = _flash(q_rot.reshape(batch, seq, -1), k_rot.reshape(batch, seq, -1), vt3, q_norm_g, k_norm_g)

    return _merge(y_f.reshape(n, -1), y_b.reshape(n, -1), xbc.reshape(n, CONV_CH), proj, attn.reshape(n, -1),
                  x2, jnp.repeat(d_skip, SSM_HEAD_DIM).reshape(1, -1), g_ssm.reshape(1, -1),
                  w_br_ssm.astype(BF16), w_br_attn.astype(BF16), w_out.astype(BF16))


def _moe_and_final_norm(x1, g_ffn, w_router, b_router, w_mlp1, b_mlp1, w_mlp2, b_mlp2, g_final):
    n, d = x1.shape
    tm = min(MOE_TILE, n)
    w_r = jnp.pad(w_router, ((0, 0), (0, LANES - N_EXPERTS)))
    w_hi = w_r.astype(BF16)
    w_lo = (w_r - w_hi.astype(F32)).astype(BF16)
    w_r = jnp.concatenate([w_hi, w_lo, w_hi], axis=0)
    b_r = jnp.pad(b_router, (0, LANES - N_EXPERTS), constant_values=-jnp.inf).reshape(1, LANES)
    g_ffn_row = g_ffn.reshape(1, d)
    top_idx, top_w, cnt = _router(x1, g_ffn_row, w_r, b_r)
    plan = _routing_plan(cnt[:, 0, :N_EXPERTS].astype(jnp.int32), n, tm, MOE_BLOCK)
    idx_t = jnp.transpose(top_idx[:, :8])
    xs = _dispatch(x1, g_ffn_row, idx_t, plan, tm)
    yb = _experts(xs, plan, w_mlp1, b_mlp1[:, None, :], w_mlp2, b_mlp2[:, None, :])
    return _combine(x1, yb, top_idx, top_w, plan, g_final.reshape(1, d), tm)


def kernel(x, g_mix, w_in, conv_w, conv_b, dt_bias_f, dt_bias_b, a_log_f, a_log_b, d_skip, g_ssm, q_norm_g,
           k_norm_g, w_br_ssm, w_br_attn, w_out, g_ffn, w_router, b_router, w_mlp1, b_mlp1, w_mlp2, b_mlp2,
           g_final):
    batch, seq, d = x.shape
    assert g_mix.shape[0] == 1, "single-layer model: the final rmsnorm is fused into the MoE combine"
    x2 = x.reshape(batch * seq, d)
    x1 = _token_mixer(x2, batch, seq, g_mix[0], w_in[0], conv_w[0], conv_b[0], dt_bias_f[0], dt_bias_b[0],
                      a_log_f[0], a_log_b[0], d_skip[0], g_ssm[0], q_norm_g[0], k_norm_g[0], w_br_ssm[0],
                      w_br_attn[0], w_out[0])
    out = _moe_and_final_norm(x1, g_ffn[0], w_router[0], b_router[0], w_mlp1[0], b_mlp1[0], w_mlp2[0],
                              b_mlp2[0], g_final)
    return out.reshape(batch, seq, d)
```

```python
import functools
import math

import jax
import jax.numpy as jnp
import numpy as np
from jax import lax
from jax.experimental import pallas as pl
from jax.experimental.pallas import tpu as pltpu

F32 = jnp.float32
BF16 = jnp.bfloat16

NORM_EPS = 1e-6
GRID_W = 64
SSM_HEADS = 16
SSM_HEAD_DIM = 64
SSM_INNER = SSM_HEADS * SSM_HEAD_DIM
SSM_GROUPS = 2
SSM_STATE = 128
SSM_CONV = 5
CONV_CH = SSM_INNER + 2 * SSM_GROUPS * SSM_STATE
ATTN_HEADS = 8
ATTN_KV_HEADS = 2
ATTN_HEAD_DIM = 128
ROPE_THETA = 10000.0
N_EXPERTS = 32
TOP_K = 4
SWIGLU_LIMIT = 7.0
SWIGLU_ALPHA = 1.702

LANES = 128
BF16_SUBLANES = 16
VMEM_LIMIT = 56 * 1024 * 1024

Z_OFF, Q_OFF, GATE_OFF, XBC_OFF = 0, 1024, 2048, 4096
K_OFF, V_OFF, PROJ_COLS = 5632, 5888, 6144

MOE_TILE = 512
MOE_BLOCK = 512
MOE_GRAN = BF16_SUBLANES


def _cparams(sem):
    return pltpu.CompilerParams(dimension_semantics=sem, vmem_limit_bytes=VMEM_LIMIT)


def _sigmoid(x):
    return 1.0 / (1.0 + jnp.exp(-x))


def _softplus(x):
    return jnp.maximum(x, 0.0) + jnp.log(1.0 + jnp.exp(-jnp.abs(x)))


def _rms(x, g):
    ms = jnp.mean(x * x, axis=-1, keepdims=True)
    return x * lax.rsqrt(ms + NORM_EPS) * g


def _inproj_kernel(x_ref, g_ref, w_ref, wdt_ref, o_ref, dt_ref, h_scr):
    @pl.when(pl.program_id(1) == 0)
    def _():
        hb = _rms(x_ref[...], g_ref[...]).astype(BF16)
        h_scr[...] = hb
        dt_ref[...] = jnp.dot(hb, wdt_ref[...], preferred_element_type=F32)

    o_ref[...] = jnp.dot(h_scr[...], w_ref[...], preferred_element_type=F32).astype(o_ref.dtype)


def _in_proj(x2, g_mix, w_main, w_dt):
    n, d = x2.shape
    tm = min(1024, n)
    tn = 2048
    return pl.pallas_call(
        _inproj_kernel,
        grid=(n // tm, PROJ_COLS // tn),
        in_specs=[
            pl.BlockSpec((tm, d), lambda i, j: (i, 0)),
            pl.BlockSpec((1, d), lambda i, j: (0, 0)),
            pl.BlockSpec((d, tn), lambda i, j: (0, j)),
            pl.BlockSpec((d, LANES), lambda i, j: (0, 0)),
        ],
        out_specs=[
            pl.BlockSpec((tm, tn), lambda i, j: (i, j)),
            pl.BlockSpec((tm, LANES), lambda i, j: (i, 0)),
        ],
        out_shape=[
            jax.ShapeDtypeStruct((n, PROJ_COLS), BF16),
            jax.ShapeDtypeStruct((n, LANES), F32),
        ],
        scratch_shapes=[pltpu.VMEM((tm, d), BF16)],
        compiler_params=_cparams(("arbitrary", "arbitrary")),
    )(x2, g_mix, w_main, w_dt)


def _conv_kernel(prev_ref, cur_ref, next_ref, w_ref, b_ref, o_ref, scr):
    s = pl.program_id(1)
    ts = cur_ref.shape[1]
    halo = BF16_SUBLANES
    pad = (SSM_CONV - 1) // 2
    prev = prev_ref[0].astype(F32)
    nxt = next_ref[0].astype(F32)
    scr[0:halo, :] = jnp.where(s == 0, 0.0, prev)
    scr[halo:halo + ts, :] = cur_ref[0].astype(F32)
    scr[halo + ts:2 * halo + ts, :] = jnp.where(s == pl.num_programs(1) - 1, 0.0, nxt)
    acc = jnp.zeros((ts, cur_ref.shape[2]), F32) + b_ref[...]
    for k in range(SSM_CONV):
        acc = acc + w_ref[k:k + 1, :] * scr[halo - pad + k:halo - pad + k + ts, :]
    o_ref[0] = (acc * _sigmoid(acc)).astype(o_ref.dtype)


def _conv(proj3, conv_w, conv_b):
    b, s, _ = proj3.shape
    ts = min(1024, s)
    tc = 512
    halo = BF16_SUBLANES
    hb = ts // halo
    col0 = XBC_OFF // tc
    return pl.pallas_call(
        _conv_kernel,
        grid=(b, s // ts, CONV_CH // tc),
        in_specs=[
            pl.BlockSpec((1, halo, tc), lambda bi, si, ci: (bi, jnp.maximum(si * hb - 1, 0), col0 + ci)),
            pl.BlockSpec((1, ts, tc), lambda bi, si, ci: (bi, si, col0 + ci)),
            pl.BlockSpec((1, halo, tc),
                         lambda bi, si, ci: (bi, jnp.minimum((si + 1) * hb, s // halo - 1), col0 + ci)),
            pl.BlockSpec((SSM_CONV, tc), lambda bi, si, ci: (0, ci)),
            pl.BlockSpec((1, tc), lambda bi, si, ci: (0, ci)),
        ],
        out_specs=pl.BlockSpec((1, ts, tc), lambda bi, si, ci: (bi, si, ci)),
        out_shape=jax.ShapeDtypeStruct((b, s, CONV_CH), BF16),
        scratch_shapes=[pltpu.VMEM((ts + 2 * halo, tc), F32)],
        compiler_params=_cparams(("arbitrary", "arbitrary", "arbitrary")),
    )(proj3, proj3, proj3, conv_w, conv_b)


def _ssd_kernel(xf_ref, xb_ref, dtf_ref, dtb_ref, dttf_ref, dttb_ref, brow_ref, bcol_ref,
                arow_ref, acol_ref, yf_ref, yb_ref, st_ref):
    L = xf_ref.shape[1]
    hg = SSM_HEADS // SSM_GROUPS
    pairs = hg // 2

    @pl.when(pl.program_id(1) == 0)
    def _():
        st_ref[...] = jnp.zeros_like(st_ref)

    rows = lax.broadcasted_iota(jnp.int32, (L, L), 0)
    cols = lax.broadcasted_iota(jnp.int32, (L, L), 1)
    lower = rows >= cols
    upper = rows <= cols
    ltri = lower.astype(BF16)
    utri = upper.astype(BF16)
    lane = lax.broadcasted_iota(jnp.int32, (L, LANES), 1)
    first_half = lane < SSM_HEAD_DIM
    lane1 = lax.broadcasted_iota(jnp.int32, (1, LANES), 1)
    log2e = math.log2(math.e)
    a_row = -jnp.exp(arow_ref[...]) * log2e
    a_col = -jnp.exp(acol_ref[...]) * log2e

    def split3(v):
        hi = v.astype(BF16)
        r1 = v - hi.astype(F32)
        mid = r1.astype(BF16)
        return hi, mid, (r1 - mid.astype(F32)).astype(BF16)

    def cumsum_cols(tri, v):
        return jnp.dot(jnp.concatenate([tri] * 3, axis=1), jnp.concatenate(split3(v), axis=0),
                       preferred_element_type=F32)

    def cumsum_rows(v, tri):
        return jnp.dot(jnp.concatenate(split3(v), axis=1), jnp.concatenate([tri] * 3, axis=0),
                       preferred_element_type=F32)

    for d in range(2):
        x_ref, dt_ref, dtt_ref, y_ref = ((xf_ref, dtf_ref, dttf_ref, yf_ref) if d == 0
                                         else (xb_ref, dtb_ref, dttb_ref, yb_ref))
        a = _softplus(dt_ref[0] + brow_ref[...]) * a_row
        dt_t = _softplus(dtt_ref[0] + bcol_ref[...])
        a_t = dt_t * a_col
        if d == 0:
            cs_col = cumsum_cols(ltri, a)
            cs_row = cumsum_rows(a_t, utri)
            tot = cs_col[L - 1:L, :]
            tot_t = cs_row[:, L - 1:L]
            mask = lower
        else:
            cs_col = cumsum_cols(utri, a)
            cs_row = cumsum_rows(a_t, ltri)
            tot = cs_col[0:1, :]
            tot_t = cs_row[:, 0:1]
            mask = upper
        w_t = dt_t * jnp.exp2(tot_t - cs_row)
        src_t = cs_row - jnp.log2(dt_t)
        chunk_decay = jnp.exp2(tot)

        for g in range(SSM_GROUPS):
            boff = SSM_INNER + g * SSM_STATE
            coff = SSM_INNER + SSM_GROUPS * SSM_STATE + g * SSM_STATE
            bm = x_ref[0, :, boff:boff + SSM_STATE]
            cm = x_ref[0, :, coff:coff + SSM_STATE]
            cb = lax.dot_general(cm, bm, (((1,), (1,)), ((), ())), preferred_element_type=F32)
            bt = bm.astype(F32).T
            st = st_ref[d, g]
            y_off = jnp.dot(cm, st.astype(BF16), preferred_element_type=F32)
            for pr in range(pairs):
                h0 = d * SSM_HEADS + g * hg + 2 * pr
                xoff = (g * pairs + pr) * LANES
                xs = x_ref[0, :, xoff:xoff + LANES]
                zero = jnp.zeros_like(xs)
                rhs = jnp.concatenate([jnp.where(first_half, xs, zero),
                                       jnp.where(first_half, zero, xs)], axis=0)
                ms, ws, dins = [], [], []
                for hh in (h0, h0 + 1):
                    cs_b = jnp.broadcast_to(cs_col[:, hh:hh + 1], (L, L))
                    seg = cs_b - src_t[hh:hh + 1, :]
                    m = cb * jnp.exp2(jnp.where(mask, seg, -jnp.inf))
                    ms.append(m.astype(BF16))
                    ws.append((bt * w_t[hh:hh + 1, :]).astype(BF16))
                    dins.append(jnp.exp2(cs_b))
                y = jnp.dot(jnp.concatenate(ms, axis=1), rhs, preferred_element_type=F32)
                y = y + y_off[:, pr * LANES:(pr + 1) * LANES] * jnp.where(first_half, dins[0], dins[1])
                y_ref[0, :, xoff:xoff + LANES] = y.astype(y_ref.dtype)
                new_st = jnp.dot(jnp.concatenate(ws, axis=1), rhs, preferred_element_type=F32)
                cd = jnp.where(lane1 < SSM_HEAD_DIM, chunk_decay[:, h0:h0 + 1], chunk_decay[:, h0 + 1:h0 + 2])
                st_ref[d, g, :, pr * LANES:(pr + 1) * LANES] = st[:, pr * LANES:(pr + 1) * LANES] * cd + new_st


def _ssd(xbc, dt3, dtt3, bias_row, bias_col, alog_row, alog_col):
    b, s, _ = xbc.shape
    L = min(128, s)
    nc = s // L
    hg = SSM_HEADS // SSM_GROUPS
    fwd = lambda bi, ci: (bi, ci, 0)
    bwd = lambda bi, ci: (bi, nc - 1 - ci, 0)
    fwd_t = lambda bi, ci: (bi, 0, ci)
    bwd_t = lambda bi, ci: (bi, 0, nc - 1 - ci)
    const = lambda bi, ci: (0, 0)
    return pl.pallas_call(
        _ssd_kernel,
        grid=(b, nc),
        in_specs=[
            pl.BlockSpec((1, L, CONV_CH), fwd),
            pl.BlockSpec((1, L, CONV_CH), bwd),
            pl.BlockSpec((1, L, LANES), fwd),
            pl.BlockSpec((1, L, LANES), bwd),
            pl.BlockSpec((1, 2 * SSM_HEADS, L), fwd_t),
            pl.BlockSpec((1, 2 * SSM_HEADS, L), bwd_t),
            pl.BlockSpec((1, LANES), const),
            pl.BlockSpec((2 * SSM_HEADS, 1), const),
            pl.BlockSpec((1, LANES), const),
            pl.BlockSpec((2 * SSM_HEADS, 1), const),
        ],
        out_specs=[
            pl.BlockSpec((1, L, SSM_INNER), fwd),
            pl.BlockSpec((1, L, SSM_INNER), bwd),
        ],
        out_shape=[jax.ShapeDtypeStruct((b, s, SSM_INNER), BF16)] * 2,
        scratch_shapes=[pltpu.VMEM((2, SSM_GROUPS, SSM_STATE, hg * SSM_HEAD_DIM), F32)],
        compiler_params=_cparams(("arbitrary", "arbitrary")),
    )(xbc, xbc, dt3, dt3, dtt3, dtt3, bias_row, bias_col, alog_row, alog_col)


def _rope_head_order(a):
    q4 = ATTN_HEAD_DIM // 4
    return jnp.concatenate([a[..., 0:q4], a[..., 2 * q4:3 * q4], a[..., q4:2 * q4], a[..., 3 * q4:]], axis=-1)


def _rope_norm(t, g, cos, sin_signed, ones):
    sq = t * t
    hi = sq.astype(BF16)
    lo = (sq - hi.astype(F32)).astype(BF16)
    ms = jnp.dot(jnp.concatenate([hi, lo], axis=1), ones, preferred_element_type=F32) * (1.0 / ATTN_HEAD_DIM)
    tn = t * lax.rsqrt(ms + NORM_EPS) * g
    return tn * cos + pltpu.roll(tn, ATTN_HEAD_DIM // 2, 1) * sin_signed


Q_SCALE = ATTN_HEAD_DIM ** -0.5 * math.log2(math.e)


def _qkprep_kernel(q_ref, k_ref, cos_ref, sin_ref, qg_ref, kg_ref, qo_ref, ko_ref):
    cos = cos_ref[...]
    sin = sin_ref[...]
    ones = jnp.ones((2 * ATTN_HEAD_DIM, ATTN_HEAD_DIM), BF16)

    def heads(src_ref, g_ref, dst_ref, n_heads, scale):
        for h in range(n_heads):
            sl = slice(h * ATTN_HEAD_DIM, (h + 1) * ATTN_HEAD_DIM)
            r = _rope_norm(src_ref[:, sl].astype(F32), g_ref[...], cos, sin, ones) * scale
            dst_ref[:, sl] = r.astype(dst_ref.dtype)

    heads(q_ref, qg_ref, qo_ref, ATTN_HEADS, Q_SCALE)
    heads(k_ref, kg_ref, ko_ref, ATTN_KV_HEADS, 1.0)


def _qk_prep(proj, cos_t, sin_t, q_norm_g, k_norm_g, seq):
    n = proj.shape[0]
    tm = min(512, seq)
    qw = ATTN_HEADS * ATTN_HEAD_DIM
    kw = ATTN_KV_HEADS * ATTN_HEAD_DIM
    spt = seq // tm
    return pl.pallas_call(
        _qkprep_kernel,
        grid=(n // tm,),
        in_specs=[
            pl.BlockSpec((tm, qw), lambda i: (i, Q_OFF // qw)),
            pl.BlockSpec((tm, kw), lambda i: (i, K_OFF // kw)),
            pl.BlockSpec((tm, ATTN_HEAD_DIM), lambda i: (i % spt, 0)),
            pl.BlockSpec((tm, ATTN_HEAD_DIM), lambda i: (i % spt, 0)),
            pl.BlockSpec((1, ATTN_HEAD_DIM), lambda i: (0, 0)),
            pl.BlockSpec((1, ATTN_HEAD_DIM), lambda i: (0, 0)),
        ],
        out_specs=[
            pl.BlockSpec((tm, qw), lambda i: (i, 0)),
            pl.BlockSpec((tm, kw), lambda i: (i, 0)),
        ],
        out_shape=[jax.ShapeDtypeStruct((n, qw), BF16), jax.ShapeDtypeStruct((n, kw), BF16)],
        compiler_params=_cparams(("arbitrary",)),
    )(proj, proj, cos_t, sin_t, q_norm_g, k_norm_g)


def _flash_kernel(small_ref, q_ref, k_ref, vt_ref, o_ref, *, tk):
    q = q_ref[0]
    tq = q.shape[0]
    seq = k_ref.shape[1]
    nt = (((1,), (1,)), ((), ()))
    small = small_ref[0] != 0

    @pl.when(small)
    def _():
        p_t = jnp.exp2(lax.dot_general(k_ref[0], q, nt, preferred_element_type=F32))
        l = jnp.sum(p_t, axis=0, keepdims=True)
        acc_t = jnp.dot(vt_ref[0], p_t.astype(BF16), preferred_element_type=F32)
        o_ref[0] = (acc_t / l).T.astype(o_ref.dtype)

    @pl.when(jnp.logical_not(small))
    def _():
        def body(i, carry):
            m, l, acc_t = carry
            off = pl.multiple_of(i * tk, tk)
            s_t = lax.dot_general(k_ref[0, pl.ds(off, tk), :], q, nt, preferred_element_type=F32)
            m_new = jnp.maximum(m, jnp.max(s_t, axis=0, keepdims=True))
            alpha = jnp.exp2(m - m_new)
            p_t = jnp.exp2(s_t - m_new)
            l = alpha * l + jnp.sum(p_t, axis=0, keepdims=True)
            acc_t = alpha * acc_t + jnp.dot(vt_ref[0, :, pl.ds(off, tk)], p_t.astype(BF16),
                                            preferred_element_type=F32)
            return m_new, l, acc_t

        init = (jnp.full((1, tq), -jnp.inf, F32), jnp.zeros((1, tq), F32),
                jnp.zeros((ATTN_HEAD_DIM, tq), F32))
        _, l, acc_t = lax.fori_loop(0, seq // tk, body, init)
        o_ref[0] = (acc_t / l).T.astype(o_ref.dtype)


SCORE_BOUND = 59.0


def _flash(q3, k3, vt3, q_norm_g, k_norm_g):
    b, s, _ = q3.shape
    tq = min(512, s)
    tk = min(2048, s)
    nq = s // tq
    grp = ATTN_HEADS // ATTN_KV_HEADS
    hd = ATTN_HEAD_DIM
    bound = hd * Q_SCALE * jnp.max(jnp.abs(q_norm_g)) * jnp.max(jnp.abs(k_norm_g)) * 1.02
    small = (bound <= SCORE_BOUND).astype(jnp.int32).reshape(1)
    grid_spec = pltpu.PrefetchScalarGridSpec(
        num_scalar_prefetch=1,
        grid=(b, ATTN_HEADS, nq),
        in_specs=[
            pl.BlockSpec((1, tq, hd), lambda bi, h, qi, sm: (bi, qi, h)),
            pl.BlockSpec((1, s, hd), lambda bi, h, qi, sm: (bi, 0, h // grp)),
            pl.BlockSpec((1, hd, s), lambda bi, h, qi, sm: (bi, h // grp, 0)),
        ],
        out_specs=pl.BlockSpec((1, tq, hd), lambda bi, h, qi, sm: (bi, qi, h)),
    )
    return pl.pallas_call(
        functools.partial(_flash_kernel, tk=tk),
        grid_spec=grid_spec,
        out_shape=jax.ShapeDtypeStruct((b, s, ATTN_HEADS * hd), BF16),
        compiler_params=_cparams(("arbitrary", "arbitrary", "arbitrary")),
    )(small, q3, k3, vt3)


def _merge_kernel(yf_ref, yb_ref, xs_ref, z_ref, gate_ref, attn_ref, x_ref, dskip_ref, gssm_ref,
                  wbs_ref, wba_ref, wo_ref, x1_ref):
    d = x_ref.shape[1]
    xs = xs_ref[...].astype(F32)
    y = yf_ref[...].astype(F32) + yb_ref[...].astype(F32) + xs * dskip_ref[...]
    z = z_ref[...].astype(F32)
    y = _rms(y * (z * _sigmoid(z)), gssm_ref[...])
    br_ssm = jnp.dot(y.astype(BF16), wbs_ref[...], preferred_element_type=F32)
    br_attn = jnp.dot(attn_ref[...], wba_ref[...], preferred_element_type=F32)
    g_s = _sigmoid(gate_ref[:, :d].astype(F32))
    g_a = _sigmoid(gate_ref[:, d:].astype(F32))
    merged = (g_s * br_ssm + g_a * br_attn).astype(BF16)
    x1_ref[...] = x_ref[...] + jnp.dot(merged, wo_ref[...], preferred_element_type=F32)


def _merge(y_f, y_b, xbc, proj, attn, x2, dskip_row, g_ssm, w_br_ssm, w_br_attn, w_out):
    n, d = x2.shape
    tm = min(512, n)
    row = lambda i: (i, 0)
    const = lambda i: (0, 0)
    return pl.pallas_call(
        _merge_kernel,
        grid=(n // tm,),
        in_specs=[
            pl.BlockSpec((tm, d), row),
            pl.BlockSpec((tm, d), row),
            pl.BlockSpec((tm, d), row),
            pl.BlockSpec((tm, d), lambda i: (i, Z_OFF // d)),
            pl.BlockSpec((tm, 2 * d), lambda i: (i, GATE_OFF // (2 * d))),
            pl.BlockSpec((tm, d), row),
            pl.BlockSpec((tm, d), row),
            pl.BlockSpec((1, d), const),
            pl.BlockSpec((1, d), const),
            pl.BlockSpec((d, d), const),
            pl.BlockSpec((d, d), const),
            pl.BlockSpec((d, d), const),
        ],
        out_specs=pl.BlockSpec((tm, d), row),
        out_shape=jax.ShapeDtypeStruct((n, d), F32),
        compiler_params=_cparams(("arbitrary",)),
    )(y_f, y_b, xbc, proj, proj, attn, x2, dskip_row, g_ssm, w_br_ssm, w_br_attn, w_out)


def _router_kernel(x1_ref, g_ref, w_ref, b_ref, idx_ref, wt_ref, cnt_ref):
    h = _rms(x1_ref[...], g_ref[...])
    h_hi = h.astype(BF16)
    h_lo = (h - h_hi.astype(F32)).astype(BF16)
    logits = jnp.dot(jnp.concatenate([h_hi, h_hi, h_lo], axis=1), w_ref[...],
                     preferred_element_type=F32) + b_ref[...]
    lane = lax.broadcasted_iota(jnp.int32, logits.shape, 1)
    idx_out = jnp.zeros(logits.shape, jnp.int32)
    val_out = jnp.zeros(logits.shape, F32)
    chosen = jnp.zeros(logits.shape, F32)
    vals = []
    for k in range(TOP_K):
        m = jnp.max(logits, axis=-1, keepdims=True)
        idx = jnp.min(jnp.where(logits == m, lane, LANES), axis=-1, keepdims=True)
        idx_out = jnp.where(lane == k, idx, idx_out)
        vals.append(m)
        hit = lane == idx
        chosen = jnp.where(hit, 1.0, chosen)
        logits = jnp.where(hit, -jnp.inf, logits)
    es = [jnp.exp(v - vals[0]) for v in vals]
    tot = es[0] + es[1] + es[2] + es[3]
    for k in range(TOP_K):
        val_out = jnp.where(lane == k, es[k] / tot, val_out)
    idx_ref[...] = idx_out
    wt_ref[...] = val_out
    cnt_ref[0] = jnp.broadcast_to(jnp.sum(chosen, axis=0, keepdims=True), cnt_ref.shape[1:])


def _router(x1, g_ffn, w_router_pad, b_router_pad):
    n, d = x1.shape
    tm = min(MOE_TILE, n)
    return pl.pallas_call(
        _router_kernel,
        grid=(n // tm,),
        in_specs=[
            pl.BlockSpec((tm, d), lambda i: (i, 0)),
            pl.BlockSpec((1, d), lambda i: (0, 0)),
            pl.BlockSpec((3 * d, LANES), lambda i: (0, 0)),
            pl.BlockSpec((1, LANES), lambda i: (0, 0)),
        ],
        out_specs=[pl.BlockSpec((tm, LANES), lambda i: (i, 0)),
                   pl.BlockSpec((tm, LANES), lambda i: (i, 0)),
                   pl.BlockSpec((1, 8, LANES), lambda i: (i, 0, 0))],
        out_shape=[jax.ShapeDtypeStruct((n, LANES), jnp.int32),
                   jax.ShapeDtypeStruct((n, LANES), F32),
                   jax.ShapeDtypeStruct((n // tm, 8, LANES), F32)],
        compiler_params=_cparams(("arbitrary",)),
    )(x1, g_ffn, w_router_pad, b_router_pad)


def _routing_plan(cnt, n, tm, tb):
    nt = n // tm
    gran = MOE_GRAN
    rc = (cnt + gran - 1) // gran * gran
    covered = jnp.sum(rc, axis=0)
    region = (covered + tb - 1) // tb * tb
    pad_end = jnp.cumsum(region)
    pad_start = pad_end - region
    seg_start = pad_start[None, :] + jnp.cumsum(rc, axis=0) - rc
    stage_off = jnp.cumsum(rc, axis=1) - rc
    n_used = pad_end[-1] // tb
    nb = (TOP_K * n + nt * N_EXPERTS * (gran - 1) + N_EXPERTS * (tb - 1) + tb - 1) // tb
    blk = jnp.arange(nb, dtype=jnp.int32)
    block_e = jnp.sum((pad_end[None, :] <= (jnp.minimum(blk, n_used - 1) * tb)[:, None]).astype(jnp.int32), axis=1)
    block_e = jnp.minimum(block_e, N_EXPERTS - 1)
    flat = lambda a: a.reshape(-1).astype(jnp.int32)
    tail_gran = (region - covered) // gran
    ngran = rc // gran
    g_end = jnp.cumsum(ngran, axis=1)
    gi = jnp.arange(_stage_rows(tm) // gran, dtype=jnp.int32)
    g_exp = jnp.minimum(jnp.sum((g_end[:, None, :] <= gi[None, :, None]).astype(jnp.int32), axis=2), N_EXPERTS - 1)
    mine = g_exp[:, :, None] == jnp.arange(N_EXPERTS, dtype=jnp.int32)[None, None, :]
    seg_base = seg_start - gran * (g_end - ngran)
    gran_dst = jnp.sum(jnp.where(mine, seg_base[:, None, :], 0), axis=2) + gran * gi[None, :]
    ids = jnp.arange(N_EXPERTS, dtype=jnp.int32)
    later = jnp.where((ids[None, :] > ids[:, None]) & (region[None, :] > 0), ids[None, :], N_EXPERTS)
    nxt = jnp.min(later, axis=1)
    next_expert = jnp.where(nxt < N_EXPERTS, nxt, ids)
    return dict(gran_dst=flat(gran_dst), tot_gran=flat(g_end[:, -1]), block_e=flat(block_e),
                next_expert=flat(next_expert),
                n_used=flat(n_used), nb=nb, stage_off_f=stage_off.astype(F32),
                tail_start=flat(pad_start + covered), tail_gran=flat(tail_gran),
                tot_tail=flat(jnp.sum(tail_gran)))


def _stage_rows(tm):
    rows = TOP_K * tm + N_EXPERTS * (MOE_GRAN - 1)
    return (rows + 2 * LANES - 1) // (2 * LANES) * (2 * LANES)


GRAN_UNROLL = 4


def _granule_copies(t, gdst_ref, totg_ref, make_copy, start):
    total = totg_ref[t]
    base = t * (gdst_ref.shape[0] // totg_ref.shape[0])
    full = lax.shift_right_logical(total, GRAN_UNROLL.bit_length() - 1)

    def one(i):
        make_copy(pl.multiple_of(i * MOE_GRAN, MOE_GRAN), pl.multiple_of(gdst_ref[base + i], MOE_GRAN),
                  MOE_GRAN).start()

    def group(q, c):
        if start:
            for u in range(GRAN_UNROLL):
                one(q * GRAN_UNROLL + u)
        else:
            make_copy(0, 0, GRAN_UNROLL * MOE_GRAN).wait()
        return c

    def rest(i, c):
        if start:
            one(i)
        else:
            make_copy(0, 0, MOE_GRAN).wait()
        return c

    lax.fori_loop(0, full, group, 0)
    lax.fori_loop(full * GRAN_UNROLL, total, rest, 0)


def _dispatch_kernel(gdst_ref, totg_ref, tstart_ref, tgran_ref, misc_ref,
                     x1_ref, g_ref, idxt_ref, soffc_ref, xs_hbm, stage, zbuf, sem, zsem, *, n_blocks):
    t = pl.program_id(0)
    slot = t % 2
    tm = x1_ref.shape[0]
    rows = stage.shape[1]
    h = _rms(x1_ref[...], g_ref[...]).astype(BF16)
    idxt = idxt_ref[...]
    expert = lax.broadcasted_iota(jnp.int32, (LANES, tm), 0)
    hots = [idxt[k:k + 1, :] == expert for k in range(TOP_K)]
    multi = jnp.zeros((LANES, tm), F32)
    for hot in hots:
        multi = jnp.where(hot, 1.0, multi)
    earlier = (lax.broadcasted_iota(jnp.int32, (tm, tm), 0)
               < lax.broadcasted_iota(jnp.int32, (tm, tm), 1)).astype(BF16)
    rank = jnp.dot(multi.astype(BF16), earlier, preferred_element_type=F32)
    pos = rank + soffc_ref[0]
    srows = [jnp.sum(jnp.where(hot, pos, 0.0), axis=0, keepdims=True) for hot in hots]
    chunk = 2 * LANES
    row_id = lax.broadcasted_iota(jnp.int32, (chunk, tm), 0).astype(F32)
    for c in range(rows // chunk):
        perm = jnp.zeros((chunk, tm), F32)
        for srow in srows:
            perm = jnp.where(row_id == srow - float(c * chunk), 1.0, perm)
        stage[slot, c * chunk:(c + 1) * chunk, :] = jnp.dot(
            perm.astype(BF16), h, preferred_element_type=F32).astype(stage.dtype)

    def copy_from(s):
        def copy(stage_row, sorted_row, nrows):
            return pltpu.make_async_copy(stage.at[s, pl.ds(stage_row, nrows)],
                                         xs_hbm.at[pl.ds(sorted_row, nrows)], sem.at[s])
        return copy

    _granule_copies(t, gdst_ref, totg_ref, copy_from(slot), start=True)

    @pl.when(t > 0)
    def _():
        _granule_copies(jnp.maximum(t - 1, 0), gdst_ref, totg_ref, copy_from(1 - slot), start=False)

    @pl.when(t == pl.num_programs(0) - 1)
    def _():
        _granule_copies(t, gdst_ref, totg_ref, copy_from(slot), start=False)
        tb = zbuf.shape[0]
        n_used = misc_ref[0]
        zbuf[...] = jnp.zeros_like(zbuf)

        def zero_gran(row):
            return pltpu.make_async_copy(zbuf.at[pl.ds(0, MOE_GRAN)], xs_hbm.at[pl.ds(row, MOE_GRAN)], zsem)

        def zero_block(blk):
            return pltpu.make_async_copy(zbuf, xs_hbm.at[pl.ds(pl.multiple_of(blk * tb, tb), tb)], zsem)

        def tails(e, c):
            def one(g, c2):
                zero_gran(pl.multiple_of(tstart_ref[e] + g * MOE_GRAN, MOE_GRAN)).start()
                return c2
            lax.fori_loop(0, tgran_ref[e], one, 0)
            return c

        def start_block(blk, c):
            zero_block(blk).start()
            return c

        def wait_gran(g, c):
            zero_gran(0).wait()
            return c

        def wait_block(blk, c):
            zero_block(0).wait()
            return c

        lax.fori_loop(0, N_EXPERTS, tails, 0)
        lax.fori_loop(n_used, n_blocks, start_block, 0)
        lax.fori_loop(0, misc_ref[1], wait_gran, 0)
        lax.fori_loop(n_used, n_blocks, wait_block, 0)


def _dispatch(x1, g_ffn, idx_t, plan, tm):
    n, d = x1.shape
    nt = n // tm
    rows = _stage_rows(tm)
    soff_col = plan["stage_off_f"].reshape(nt, N_EXPERTS, 1)
    soff_col = jnp.pad(soff_col, ((0, 0), (0, LANES - N_EXPERTS), (0, 0)))
    misc = jnp.concatenate([plan["n_used"], plan["tot_tail"]])
    grid_spec = pltpu.PrefetchScalarGridSpec(
        num_scalar_prefetch=5,
        grid=(nt,),
        in_specs=[
            pl.BlockSpec((tm, d), lambda t, *_: (t, 0)),
            pl.BlockSpec((1, d), lambda t, *_: (0, 0)),
            pl.BlockSpec((8, tm), lambda t, *_: (0, t)),
            pl.BlockSpec((1, LANES, 1), lambda t, *_: (t, 0, 0)),
        ],
        out_specs=pl.BlockSpec(memory_space=pl.ANY),
        scratch_shapes=[pltpu.VMEM((2, rows, d), BF16), pltpu.VMEM((MOE_BLOCK, d), BF16),
                        pltpu.SemaphoreType.DMA((2,)), pltpu.SemaphoreType.DMA(())],
    )
    return pl.pallas_call(
        functools.partial(_dispatch_kernel, n_blocks=plan["nb"]),
        grid_spec=grid_spec,
        out_shape=jax.ShapeDtypeStruct((plan["nb"] * MOE_BLOCK, d), BF16),
        compiler_params=_cparams(("arbitrary",)),
    )(plan["gran_dst"], plan["tot_gran"], plan["tail_start"], plan["tail_gran"], misc, x1, g_ffn, idx_t,
      soff_col)


def _expert_kernel(be_ref, nused_ref, nexte_ref, x_ref, w1_hbm, b1_ref, w2_hbm, b2_ref, o_ref,
                   w1f, w2f, w1b, w2b, sem):
    de = w2f.shape[0]
    j = pl.program_id(0)
    live = j < nused_ref[0]
    e = be_ref[j]
    new_expert = jnp.logical_or(j == 0, e != be_ref[jnp.maximum(j - 1, 0)])

    def fetch(ex):
        return (pltpu.make_async_copy(w1_hbm.at[ex], w1f, sem.at[0]),
                pltpu.make_async_copy(w2_hbm.at[ex], w2f, sem.at[1]))

    @pl.when(jnp.logical_and(live, j == 0))
    def _():
        for c in fetch(e):
            c.start()

    @pl.when(jnp.logical_and(live, new_expert))
    def _():
        for c in fetch(e):
            c.wait()
        w1b[...] = w1f[...].astype(BF16)
        w2b[...] = w2f[...].astype(BF16)
        nxt = nexte_ref[e]

        @pl.when(nxt != e)
        def _():
            for c in fetch(nxt):
                c.start()

    @pl.when(live)
    def _():
        hm = jnp.dot(x_ref[...], w1b[...], preferred_element_type=F32) + b1_ref[0]
        gate = jnp.minimum(hm[:, :de], SWIGLU_LIMIT)
        up = jnp.clip(hm[:, de:], -SWIGLU_LIMIT, SWIGLU_LIMIT)
        act = gate * _sigmoid(SWIGLU_ALPHA * gate) * (up + 1.0)
        y = jnp.dot(act.astype(BF16), w2b[...], preferred_element_type=F32) + b2_ref[0]
        o_ref[...] = y.astype(o_ref.dtype)

    @pl.when(jnp.logical_not(live))
    def _():
        o_ref[...] = jnp.zeros_like(o_ref)


def _experts(xs, plan, w1, b1, w2, b2):
    d = xs.shape[1]
    tb = MOE_BLOCK
    f2 = w1.shape[2]
    de = w2.shape[1]
    last = lambda j, nu: jnp.maximum(jnp.minimum(j, nu[0] - 1), 0)
    grid_spec = pltpu.PrefetchScalarGridSpec(
        num_scalar_prefetch=3,
        grid=(plan["nb"],),
        in_specs=[
            pl.BlockSpec((tb, d), lambda j, be, nu, ne: (last(j, nu), 0)),
            pl.BlockSpec(memory_space=pl.ANY),
            pl.BlockSpec((1, 1, f2), lambda j, be, nu, ne: (be[j], 0, 0)),
            pl.BlockSpec(memory_space=pl.ANY),
            pl.BlockSpec((1, 1, d), lambda j, be, nu, ne: (be[j], 0, 0)),
        ],
        out_specs=pl.BlockSpec((tb, d), lambda j, be, nu, ne: (j, 0)),
        scratch_shapes=[pltpu.VMEM((d, f2), F32), pltpu.VMEM((de, d), F32),
                        pltpu.VMEM((d, f2), BF16), pltpu.VMEM((de, d), BF16),
                        pltpu.SemaphoreType.DMA((2,))],
    )
    return pl.pallas_call(
        _expert_kernel,
        grid_spec=grid_spec,
        out_shape=jax.ShapeDtypeStruct(xs.shape, BF16),
        compiler_params=_cparams(("arbitrary",)),
    )(plan["block_e"], plan["n_used"], plan["next_expert"], xs, w1, b1, w2, b2)


def _combine_kernel(gdst_ref, totg_ref, x1_ref, idx_ref, wt_ref, soffr_ref, g_ref, yb_hbm, o_ref, stage, sem):
    t = pl.program_id(0)
    slot = t % 2
    tm = x1_ref.shape[0]
    rows = stage.shape[1]

    def copy_into(s):
        def copy(stage_row, sorted_row, nrows):
            return pltpu.make_async_copy(yb_hbm.at[pl.ds(sorted_row, nrows)],
                                         stage.at[s, pl.ds(stage_row, nrows)], sem.at[s])
        return copy

    @pl.when(t == 0)
    def _():
        stage[...] = jnp.zeros_like(stage)
        _granule_copies(t, gdst_ref, totg_ref, copy_into(slot), start=True)

    @pl.when(t + 1 < pl.num_programs(0))
    def _():
        _granule_copies(t + 1, gdst_ref, totg_ref, copy_into(1 - slot), start=True)

    idx = idx_ref[...]
    wt = wt_ref[...]
    expert = lax.broadcasted_iota(jnp.int32, (tm, LANES), 1)
    hots = [idx[:, k:k + 1] == expert for k in range(TOP_K)]
    multi = jnp.zeros((tm, LANES), F32)
    for hot in hots:
        multi = jnp.where(hot, 1.0, multi)
    earlier = (lax.broadcasted_iota(jnp.int32, (tm, tm), 0)
               > lax.broadcasted_iota(jnp.int32, (tm, tm), 1)).astype(BF16)
    rank = jnp.dot(earlier, multi.astype(BF16), preferred_element_type=F32)
    pos = rank + soffr_ref[0]
    srows = [jnp.sum(jnp.where(hot, pos, 0.0), axis=-1, keepdims=True) for hot in hots]
    chunk = LANES
    row_id = lax.broadcasted_iota(jnp.int32, (chunk, rows), 1).astype(F32)
    for c in range(tm // chunk):
        sl = slice(c * chunk, (c + 1) * chunk)
        unsort = jnp.zeros((chunk, rows), F32)
        for k, srow in enumerate(srows):
            unsort = jnp.where(row_id == srow[sl], wt[sl, k:k + 1], unsort)
        if c == 0:
            _granule_copies(t, gdst_ref, totg_ref, copy_into(slot), start=False)
        y = jnp.dot(unsort.astype(BF16), stage[slot], preferred_element_type=F32)
        o_ref[sl, :] = _rms(x1_ref[sl, :] + y, g_ref[...])


def _combine(x1, yb, top_idx, top_w, plan, g_final, tm):
    n, d = x1.shape
    nt = n // tm
    rows = _stage_rows(tm)
    soff_row = jnp.pad(plan["stage_off_f"], ((0, 0), (0, LANES - N_EXPERTS))).reshape(nt, 1, LANES)
    grid_spec = pltpu.PrefetchScalarGridSpec(
        num_scalar_prefetch=2,
        grid=(nt,),
        in_specs=[
            pl.BlockSpec((tm, d), lambda t, *_: (t, 0)),
            pl.BlockSpec((tm, LANES), lambda t, *_: (t, 0)),
            pl.BlockSpec((tm, LANES), lambda t, *_: (t, 0)),
            pl.BlockSpec((1, 1, LANES), lambda t, *_: (t, 0, 0)),
            pl.BlockSpec((1, d), lambda t, *_: (0, 0)),
            pl.BlockSpec(memory_space=pl.ANY),
        ],
        out_specs=pl.BlockSpec((tm, d), lambda t, *_: (t, 0)),
        scratch_shapes=[pltpu.VMEM((2, rows, d), BF16), pltpu.SemaphoreType.DMA((2,))],
    )
    return pl.pallas_call(
        _combine_kernel,
        grid_spec=grid_spec,
        out_shape=jax.ShapeDtypeStruct((n, d), F32),
        compiler_params=_cparams(("arbitrary",)),
    )(plan["gran_dst"], plan["tot_gran"], x1, top_idx, top_w, soff_row, g_final, yb)


def _rope_tables(seq):
    half = ATTN_HEAD_DIM // 2
    inv_freq = np.float32(ROPE_THETA) ** (-np.arange(0, half, 2, dtype=np.float32) / np.float32(half))
    ang_row = np.arange(seq // GRID_W, dtype=np.float32)[:, None] * inv_freq
    ang_col = np.arange(GRID_W, dtype=np.float32)[:, None] * inv_freq
    by_row = lambda t: jnp.repeat(jnp.asarray(t, F32), GRID_W, axis=0)
    by_col = lambda t: jnp.tile(jnp.asarray(t, F32), (seq // GRID_W, 1))
    cos_r, sin_r = by_row(np.cos(ang_row)), by_row(np.sin(ang_row))
    cos_c, sin_c = by_col(np.cos(ang_col)), by_col(np.sin(ang_col))
    cos_t = jnp.concatenate([cos_r, cos_c] * 2, axis=-1)
    sin_t = jnp.concatenate([-sin_r, -sin_c, sin_r, sin_c], axis=-1)
    return cos_t, sin_t


def _token_mixer(x2, batch, seq, g_mix, w_in, conv_w, conv_b, dt_bias_f, dt_bias_b, a_log_f, a_log_b, d_skip,
                 g_ssm, q_norm_g, k_norm_g, w_br_ssm, w_br_attn, w_out):
    n, d = x2.shape
    z_end = SSM_INNER
    xbc_end = z_end + CONV_CH
    dtf_end = xbc_end + SSM_HEADS
    dtb_end = dtf_end + SSM_HEADS
    q_end = dtb_end + ATTN_HEADS * ATTN_HEAD_DIM
    k_end = q_end + ATTN_KV_HEADS * ATTN_HEAD_DIM
    v_end = k_end + ATTN_KV_HEADS * ATTN_HEAD_DIM
    head_cols = lambda w: _rope_head_order(w.reshape(d, -1, ATTN_HEAD_DIM)).reshape(d, -1)
    w_main = jnp.concatenate([w_in[:, :z_end], head_cols(w_in[:, dtb_end:q_end]), w_in[:, v_end:],
                              w_in[:, z_end:xbc_end], head_cols(w_in[:, q_end:k_end]), w_in[:, k_end:v_end]],
                             axis=1).astype(BF16)
    w_dt = jnp.pad(w_in[:, xbc_end:dtb_end], ((0, 0), (0, LANES - 2 * SSM_HEADS))).astype(BF16)

    proj, dt = _in_proj(x2, g_mix.reshape(1, d), w_main, w_dt)
    proj3 = proj.reshape(batch, seq, PROJ_COLS)

    xbc = _conv(proj3, conv_w, conv_b.reshape(1, CONV_CH))

    dt3 = dt.reshape(batch, seq, LANES)
    dtt3 = jnp.swapaxes(dt3[:, :, :2 * SSM_HEADS], 1, 2)
    bias = jnp.concatenate([dt_bias_f, dt_bias_b])
    alog = jnp.concatenate([a_log_f, a_log_b])
    pad_row = lambda v: jnp.pad(v, (0, LANES - 2 * SSM_HEADS)).reshape(1, LANES)
    y_f, y_b = _ssd(xbc, dt3, dtt3, pad_row(bias), bias.reshape(-1, 1), pad_row(alog), alog.reshape(-1, 1))

    cos_t, sin_t = _rope_tables(seq)
    q_rot, k_rot = _qk_prep(proj, cos_t, sin_t, _rope_head_order(q_norm_g).reshape(1, -1),
                            _rope_head_order(k_norm_g).reshape(1, -1), seq)
    vt3 = jnp.swapaxes(proj3[:, :, V_OFF:V_OFF + ATTN_KV_HEADS * ATTN_HEAD_DIM], 1, 2)
    attn = _flash(q_rot.reshape(batch, seq, -1), k_rot.reshape(batch, seq, -1), vt3, q_norm_g, k_norm_g)

    return _merge(y_f.reshape(n, -1), y_b.reshape(n, -1), xbc.reshape(n, CONV_CH), proj, attn.reshape(n, -1),
                  x2, jnp.repeat(d_skip, SSM_HEAD_DIM).reshape(1, -1), g_ssm.reshape(1, -1),
                  w_br_ssm.astype(BF16), w_br_attn.astype(BF16), w_out.astype(BF16))


def _moe_and_final_norm(x1, g_ffn, w_router, b_router, w_mlp1, b_mlp1, w_mlp2, b_mlp2, g_final):
    n, d = x1.shape
    tm = min(MOE_TILE, n)
    w_r = jnp.pad(w_router, ((0, 0), (0, LANES - N_EXPERTS)))
    w_hi = w_r.astype(BF16)
    w_lo = (w_r - w_hi.astype(F32)).astype(BF16)
    w_r = jnp.concatenate([w_hi, w_lo, w_hi], axis=0)
    b_r = jnp.pad(b_router, (0, LANES - N_EXPERTS), constant_values=-jnp.inf).reshape(1, LANES)
    g_ffn_row = g_ffn.reshape(1, d)
    top_idx, top_w, cnt = _router(x1, g_ffn_row, w_r, b_r)
    plan = _routing_plan(cnt[:, 0, :N_EXPERTS].astype(jnp.int32), n, tm, MOE_BLOCK)
    idx_t = jnp.transpose(top_idx[:, :8])
    xs = _dispatch(x1, g_ffn_row, idx_t, plan, tm)
    yb = _experts(xs, plan, w_mlp1, b_mlp1[:, None, :], w_mlp2, b_mlp2[:, None, :])
    return _combine(x1, yb, top_idx, top_w, plan, g_final.reshape(1, d), tm)


def kernel(x, g_mix, w_in, conv_w, conv_b, dt_bias_f, dt_bias_b, a_log_f, a_log_b, d_skip, g_ssm, q_norm_g,
           k_norm_g, w_br_ssm, w_br_attn, w_out, g_ffn, w_router, b_router, w_mlp1, b_mlp1, w_mlp2, b_mlp2,
           g_final):
    batch, seq, d = x.shape
    assert g_mix.shape[0] == 1, "single-layer model: the final rmsnorm is fused into the MoE combine"
    x2 = x.reshape(batch * seq, d)
    x1 = _token_mixer(x2, batch, seq, g_mix[0], w_in[0], conv_w[0], conv_b[0], dt_bias_f[0], dt_bias_b[0],
                      a_log_f[0], a_log_b[0], d_skip[0], g_ssm[0], q_norm_g[0], k_norm_g[0], w_br_ssm[0],
                      w_br_attn[0], w_out[0])
    out = _moe_and_final_norm(x1, g_ffn[0], w_router[0], b_router[0], w_mlp1[0], b_mlp1[0], w_mlp2[0],
                              b_mlp2[0], g_final)
    return out.reshape(batch, seq, d)
```

```python
import functools
import math

import jax
import jax.numpy as jnp
import numpy as np
from jax import lax
from jax.experimental import pallas as pl
from jax.experimental.pallas import tpu as pltpu

F32 = jnp.float32
BF16 = jnp.bfloat16

NORM_EPS = 1e-6
GRID_W = 64
SSM_HEADS = 16
SSM_HEAD_DIM = 64
SSM_INNER = SSM_HEADS * SSM_HEAD_DIM
SSM_GROUPS = 2
SSM_STATE = 128
SSM_CONV = 5
CONV_CH = SSM_INNER + 2 * SSM_GROUPS * SSM_STATE
ATTN_HEADS = 8
ATTN_KV_HEADS = 2
ATTN_HEAD_DIM = 128
ROPE_THETA = 10000.0
N_EXPERTS = 32
TOP_K = 4
SWIGLU_LIMIT = 7.0
SWIGLU_ALPHA = 1.702

LANES = 128
BF16_SUBLANES = 16
VMEM_LIMIT = 56 * 1024 * 1024

Z_OFF, Q_OFF, GATE_OFF, XBC_OFF = 0, 1024, 2048, 4096
K_OFF, V_OFF, PROJ_COLS = 5632, 5888, 6144

MOE_TILE = 512
MOE_BLOCK = 512
MOE_GRAN = BF16_SUBLANES


def _cparams(sem):
    return pltpu.CompilerParams(dimension_semantics=sem, vmem_limit_bytes=VMEM_LIMIT)


def _sigmoid(x):
    return 1.0 / (1.0 + jnp.exp(-x))


def _softplus(x):
    return jnp.maximum(x, 0.0) + jnp.log(1.0 + jnp.exp(-jnp.abs(x)))


def _rms(x, g):
    ms = jnp.mean(x * x, axis=-1, keepdims=True)
    return x * lax.rsqrt(ms + NORM_EPS) * g


def _inproj_kernel(x_ref, g_ref, w_ref, wdt_ref, o_ref, dt_ref, h_scr):
    @pl.when(pl.program_id(1) == 0)
    def _():
        hb = _rms(x_ref[...], g_ref[...]).astype(BF16)
        h_scr[...] = hb
        dt_ref[...] = jnp.dot(hb, wdt_ref[...], preferred_element_type=F32)

    o_ref[...] = jnp.dot(h_scr[...], w_ref[...], preferred_element_type=F32).astype(o_ref.dtype)


def _in_proj(x2, g_mix, w_main, w_dt):
    n, d = x2.shape
    tm = min(1024, n)
    tn = 2048
    return pl.pallas_call(
        _inproj_kernel,
        grid=(n // tm, PROJ_COLS // tn),
        in_specs=[
            pl.BlockSpec((tm, d), lambda i, j: (i, 0)),
            pl.BlockSpec((1, d), lambda i, j: (0, 0)),
            pl.BlockSpec((d, tn), lambda i, j: (0, j)),
            pl.BlockSpec((d, LANES), lambda i, j: (0, 0)),
        ],
        out_specs=[
            pl.BlockSpec((tm, tn), lambda i, j: (i, j)),
            pl.BlockSpec((tm, LANES), lambda i, j: (i, 0)),
        ],
        out_shape=[
            jax.ShapeDtypeStruct((n, PROJ_COLS), BF16),
            jax.ShapeDtypeStruct((n, LANES), F32),
        ],
        scratch_shapes=[pltpu.VMEM((tm, d), BF16)],
        compiler_params=_cparams(("arbitrary", "arbitrary")),
    )(x2, g_mix, w_main, w_dt)


def _conv_kernel(prev_ref, cur_ref, next_ref, w_ref, b_ref, o_ref, scr):
    s = pl.program_id(1)
    ts = cur_ref.shape[1]
    halo = BF16_SUBLANES
    pad = (SSM_CONV - 1) // 2
    prev = prev_ref[0].astype(F32)
    nxt = next_ref[0].astype(F32)
    scr[0:halo, :] = jnp.where(s == 0, 0.0, prev)
    scr[halo:halo + ts, :] = cur_ref[0].astype(F32)
    scr[halo + ts:2 * halo + ts, :] = jnp.where(s == pl.num_programs(1) - 1, 0.0, nxt)
    acc = jnp.zeros((ts, cur_ref.shape[2]), F32) + b_ref[...]
    for k in range(SSM_CONV):
        acc = acc + w_ref[k:k + 1, :] * scr[halo - pad + k:halo - pad + k + ts, :]
    o_ref[0] = (acc * _sigmoid(acc)).astype(o_ref.dtype)


def _conv(proj3, conv_w, conv_b):
    b, s, _ = proj3.shape
    ts = min(1024, s)
    tc = 512
    halo = BF16_SUBLANES
    hb = ts // halo
    col0 = XBC_OFF // tc
    return pl.pallas_call(
        _conv_kernel,
        grid=(b, s // ts, CONV_CH // tc),
        in_specs=[
            pl.BlockSpec((1, halo, tc), lambda bi, si, ci: (bi, jnp.maximum(si * hb - 1, 0), col0 + ci)),
            pl.BlockSpec((1, ts, tc), lambda bi, si, ci: (bi, si, col0 + ci)),
            pl.BlockSpec((1, halo, tc),
                         lambda bi, si, ci: (bi, jnp.minimum((si + 1) * hb, s // halo - 1), col0 + ci)),
            pl.BlockSpec((SSM_CONV, tc), lambda bi, si, ci: (0, ci)),
            pl.BlockSpec((1, tc), lambda bi, si, ci: (0, ci)),
        ],
        out_specs=pl.BlockSpec((1, ts, tc), lambda bi, si, ci: (bi, si, ci)),
        out_shape=jax.ShapeDtypeStruct((b, s, CONV_CH), BF16),
        scratch_shapes=[pltpu.VMEM((ts + 2 * halo, tc), F32)],
        compiler_params=_cparams(("arbitrary", "arbitrary", "arbitrary")),
    )(proj3, proj3, proj3, conv_w, conv_b)


def _ssd_kernel(xf_ref, xb_ref, dtf_ref, dtb_ref, dttf_ref, dttb_ref, brow_ref, bcol_ref,
                arow_ref, acol_ref, yf_ref, yb_ref, st_ref):
    L = xf_ref.shape[1]
    hg = SSM_HEADS // SSM_GROUPS
    pairs = hg // 2

    @pl.when(pl.program_id(1) == 0)
    def _():
        st_ref[...] = jnp.zeros_like(st_ref)

    rows = lax.broadcasted_iota(jnp.int32, (L, L), 0)
    cols = lax.broadcasted_iota(jnp.int32, (L, L), 1)
    lower = rows >= cols
    upper = rows <= cols
    ltri = lower.astype(BF16)
    utri = upper.astype(BF16)
    lane = lax.broadcasted_iota(jnp.int32, (L, LANES), 1)
    first_half = lane < SSM_HEAD_DIM
    lane1 = lax.broadcasted_iota(jnp.int32, (1, LANES), 1)
    log2e = math.log2(math.e)
    a_row = -jnp.exp(arow_ref[...]) * log2e
    a_col = -jnp.exp(acol_ref[...]) * log2e

    def split3(v):
        hi = v.astype(BF16)
        r1 = v - hi.astype(F32)
        mid = r1.astype(BF16)
        return hi, mid, (r1 - mid.astype(F32)).astype(BF16)

    def cumsum_cols(tri, v):
        return jnp.dot(jnp.concatenate([tri] * 3, axis=1), jnp.concatenate(split3(v), axis=0),
                       preferred_element_type=F32)

    def cumsum_rows(v, tri):
        return jnp.dot(jnp.concatenate(split3(v), axis=1), jnp.concatenate([tri] * 3, axis=0),
                       preferred_element_type=F32)

    for d in range(2):
        x_ref, dt_ref, dtt_ref, y_ref = ((xf_ref, dtf_ref, dttf_ref, yf_ref) if d == 0
                                         else (xb_ref, dtb_ref, dttb_ref, yb_ref))
        a = _softplus(dt_ref[0] + brow_ref[...]) * a_row
        dt_t = _softplus(dtt_ref[0] + bcol_ref[...])
        a_t = dt_t * a_col
        if d == 0:
            cs_col = cumsum_cols(ltri, a)
            cs_row = cumsum_rows(a_t, utri)
            tot = cs_col[L - 1:L, :]
            tot_t = cs_row[:, L - 1:L]
            mask = lower
        else:
            cs_col = cumsum_cols(utri, a)
            cs_row = cumsum_rows(a_t, ltri)
            tot = cs_col[0:1, :]
            tot_t = cs_row[:, 0:1]
            mask = upper
        w_t = dt_t * jnp.exp2(tot_t - cs_row)
        src_t = cs_row - jnp.log2(dt_t)
        chunk_decay = jnp.exp2(tot)

        for g in range(SSM_GROUPS):
            boff = SSM_INNER + g * SSM_STATE
            coff = SSM_INNER + SSM_GROUPS * SSM_STATE + g * SSM_STATE
            bm = x_ref[0, :, boff:boff + SSM_STATE]
            cm = x_ref[0, :, coff:coff + SSM_STATE]
            cb = lax.dot_general(cm, bm, (((1,), (1,)), ((), ())), preferred_element_type=F32)
            bt = bm.astype(F32).T
            st = st_ref[d, g]
            y_off = jnp.dot(cm, st.astype(BF16), preferred_element_type=F32)
            for pr in range(pairs):
                h0 = d * SSM_HEADS + g * hg + 2 * pr
                xoff = (g * pairs + pr) * LANES
                xs = x_ref[0, :, xoff:xoff + LANES]
                zero = jnp.zeros_like(xs)
                rhs = jnp.concatenate([jnp.where(first_half, xs, zero),
                                       jnp.where(first_half, zero, xs)], axis=0)
                ms, ws, dins = [], [], []
                for hh in (h0, h0 + 1):
                    cs_b = jnp.broadcast_to(cs_col[:, hh:hh + 1], (L, L))
                    seg = cs_b - src_t[hh:hh + 1, :]
                    m = cb * jnp.exp2(jnp.where(mask, seg, -jnp.inf))
                    ms.append(m.astype(BF16))
                    ws.append((bt * w_t[hh:hh + 1, :]).astype(BF16))
                    dins.append(jnp.exp2(cs_b))
                y = jnp.dot(jnp.concatenate(ms, axis=1), rhs, preferred_element_type=F32)
                y = y + y_off[:, pr * LANES:(pr + 1) * LANES] * jnp.where(first_half, dins[0], dins[1])
                y_ref[0, :, xoff:xoff + LANES] = y.astype(y_ref.dtype)
                new_st = jnp.dot(jnp.concatenate(ws, axis=1), rhs, preferred_element_type=F32)
                cd = jnp.where(lane1 < SSM_HEAD_DIM, chunk_decay[:, h0:h0 + 1], chunk_decay[:, h0 + 1:h0 + 2])
                st_ref[d, g, :, pr * LANES:(pr + 1) * LANES] = st[:, pr * LANES:(pr + 1) * LANES] * cd + new_st


def _ssd(xbc, dt3, dtt3, bias_row, bias_col, alog_row, alog_col):
    b, s, _ = xbc.shape
    L = min(128, s)
    nc = s // L
    hg = SSM_HEADS // SSM_GROUPS
    fwd = lambda bi, ci: (bi, ci, 0)
    bwd = lambda bi, ci: (bi, nc - 1 - ci, 0)
    fwd_t = lambda bi, ci: (bi, 0, ci)
    bwd_t = lambda bi, ci: (bi, 0, nc - 1 - ci)
    const = lambda bi, ci: (0, 0)
    return pl.pallas_call(
        _ssd_kernel,
        grid=(b, nc),
        in_specs=[
            pl.BlockSpec((1, L, CONV_CH), fwd),
            pl.BlockSpec((1, L, CONV_CH), bwd),
            pl.BlockSpec((1, L, LANES), fwd),
            pl.BlockSpec((1, L, LANES), bwd),
            pl.BlockSpec((1, 2 * SSM_HEADS, L), fwd_t),
            pl.BlockSpec((1, 2 * SSM_HEADS, L), bwd_t),
            pl.BlockSpec((1, LANES), const),
            pl.BlockSpec((2 * SSM_HEADS, 1), const),
            pl.BlockSpec((1, LANES), const),
            pl.BlockSpec((2 * SSM_HEADS, 1), const),
        ],
        out_specs=[
            pl.BlockSpec((1, L, SSM_INNER), fwd),
            pl.BlockSpec((1, L, SSM_INNER), bwd),
        ],
        out_shape=[jax.ShapeDtypeStruct((b, s, SSM_INNER), BF16)] * 2,
        scratch_shapes=[pltpu.VMEM((2, SSM_GROUPS, SSM_STATE, hg * SSM_HEAD_DIM), F32)],
        compiler_params=_cparams(("arbitrary", "arbitrary")),
    )(xbc, xbc, dt3, dt3, dtt3, dtt3, bias_row, bias_col, alog_row, alog_col)


def _rope_head_order(a):
    q4 = ATTN_HEAD_DIM // 4
    return jnp.concatenate([a[..., 0:q4], a[..., 2 * q4:3 * q4], a[..., q4:2 * q4], a[..., 3 * q4:]], axis=-1)


def _rope_norm(t, g, cos, sin_signed, ones):
    sq = t * t
    hi = sq.astype(BF16)
    lo = (sq - hi.astype(F32)).astype(BF16)
    ms = jnp.dot(jnp.concatenate([hi, lo], axis=1), ones, preferred_element_type=F32) * (1.0 / ATTN_HEAD_DIM)
    tn = t * lax.rsqrt(ms + NORM_EPS) * g
    return tn * cos + pltpu.roll(tn, ATTN_HEAD_DIM // 2, 1) * sin_signed


Q_SCALE = ATTN_HEAD_DIM ** -0.5 * math.log2(math.e)


def _qkprep_kernel(q_ref, k_ref, cos_ref, sin_ref, qg_ref, kg_ref, qo_ref, ko_ref):
    cos = cos_ref[...]
    sin = sin_ref[...]
    ones = jnp.ones((2 * ATTN_HEAD_DIM, ATTN_HEAD_DIM), BF16)

    def heads(src_ref, g_ref, dst_ref, n_heads, scale):
        for h in range(n_heads):
            sl = slice(h * ATTN_HEAD_DIM, (h + 1) * ATTN_HEAD_DIM)
            r = _rope_norm(src_ref[:, sl].astype(F32), g_ref[...], cos, sin, ones) * scale
            dst_ref[:, sl] = r.astype(dst_ref.dtype)

    heads(q_ref, qg_ref, qo_ref, ATTN_HEADS, Q_SCALE)
    heads(k_ref, kg_ref, ko_ref, ATTN_KV_HEADS, 1.0)


def _qk_prep(proj, cos_t, sin_t, q_norm_g, k_norm_g, seq):
    n = proj.shape[0]
    tm = min(512, seq)
    qw = ATTN_HEADS * ATTN_HEAD_DIM
    kw = ATTN_KV_HEADS * ATTN_HEAD_DIM
    spt = seq // tm
    return pl.pallas_call(
        _qkprep_kernel,
        grid=(n // tm,),
        in_specs=[
            pl.BlockSpec((tm, qw), lambda i: (i, Q_OFF // qw)),
            pl.BlockSpec((tm, kw), lambda i: (i, K_OFF // kw)),
            pl.BlockSpec((tm, ATTN_HEAD_DIM), lambda i: (i % spt, 0)),
            pl.BlockSpec((tm, ATTN_HEAD_DIM), lambda i: (i % spt, 0)),
            pl.BlockSpec((1, ATTN_HEAD_DIM), lambda i: (0, 0)),
            pl.BlockSpec((1, ATTN_HEAD_DIM), lambda i: (0, 0)),
        ],
        out_specs=[
            pl.BlockSpec((tm, qw), lambda i: (i, 0)),
            pl.BlockSpec((tm, kw), lambda i: (i, 0)),
        ],
        out_shape=[jax.ShapeDtypeStruct((n, qw), BF16), jax.ShapeDtypeStruct((n, kw), BF16)],
        compiler_params=_cparams(("arbitrary",)),
    )(proj, proj, cos_t, sin_t, q_norm_g, k_norm_g)


def _flash_kernel(small_ref, q_ref, k_ref, vt_ref, o_ref, *, tk, th):
    q = q_ref[0]
    tq = q.shape[0]
    seq = k_ref.shape[1]
    nt = (((1,), (1,)), ((), ()))
    small = small_ref[0] != 0

    @pl.when(small)
    def _():
        l = acc_t = None
        for c in range(seq // th):
            ks = slice(c * th, (c + 1) * th)
            p_t = jnp.exp2(lax.dot_general(k_ref[0, ks, :], q, nt, preferred_element_type=F32))
            l_c = jnp.sum(p_t, axis=0, keepdims=True)
            a_c = jnp.dot(vt_ref[0, :, ks], p_t.astype(BF16), preferred_element_type=F32)
            l = l_c if l is None else l + l_c
            acc_t = a_c if acc_t is None else acc_t + a_c
        o_ref[0] = (acc_t / l).T.astype(o_ref.dtype)

    @pl.when(jnp.logical_not(small))
    def _():
        def body(i, carry):
            m, l, acc_t = carry
            off = pl.multiple_of(i * tk, tk)
            s_t = lax.dot_general(k_ref[0, pl.ds(off, tk), :], q, nt, preferred_element_type=F32)
            m_new = jnp.maximum(m, jnp.max(s_t, axis=0, keepdims=True))
            alpha = jnp.exp2(m - m_new)
            p_t = jnp.exp2(s_t - m_new)
            l = alpha * l + jnp.sum(p_t, axis=0, keepdims=True)
            acc_t = alpha * acc_t + jnp.dot(vt_ref[0, :, pl.ds(off, tk)], p_t.astype(BF16),
                                            preferred_element_type=F32)
            return m_new, l, acc_t

        init = (jnp.full((1, tq), -jnp.inf, F32), jnp.zeros((1, tq), F32),
                jnp.zeros((ATTN_HEAD_DIM, tq), F32))
        _, l, acc_t = lax.fori_loop(0, seq // tk, body, init)
        o_ref[0] = (acc_t / l).T.astype(o_ref.dtype)


SCORE_BOUND = 59.0


def _flash(q3, k3, vt3, q_norm_g, k_norm_g):
    b, s, _ = q3.shape
    tq = min(1024, s)
    tk = min(2048, s)
    th = min(4096, s)
    nq = s // tq
    grp = ATTN_HEADS // ATTN_KV_HEADS
    hd = ATTN_HEAD_DIM
    bound = hd * Q_SCALE * jnp.max(jnp.abs(q_norm_g)) * jnp.max(jnp.abs(k_norm_g)) * 1.02
    small = (bound <= SCORE_BOUND).astype(jnp.int32).reshape(1)
    grid_spec = pltpu.PrefetchScalarGridSpec(
        num_scalar_prefetch=1,
        grid=(b, ATTN_HEADS, nq),
        in_specs=[
            pl.BlockSpec((1, tq, hd), lambda bi, h, qi, sm: (bi, qi, h)),
            pl.BlockSpec((1, s, hd), lambda bi, h, qi, sm: (bi, 0, h // grp)),
            pl.BlockSpec((1, hd, s), lambda bi, h, qi, sm: (bi, h // grp, 0)),
        ],
        out_specs=pl.BlockSpec((1, tq, hd), lambda bi, h, qi, sm: (bi, qi, h)),
    )
    return pl.pallas_call(
        functools.partial(_flash_kernel, tk=tk, th=th),
        grid_spec=grid_spec,
        out_shape=jax.ShapeDtypeStruct((b, s, ATTN_HEADS * hd), BF16),
        compiler_params=_cparams(("arbitrary", "arbitrary", "arbitrary")),
    )(small, q3, k3, vt3)


def _merge_kernel(yf_ref, yb_ref, xs_ref, z_ref, gate_ref, attn_ref, x_ref, dskip_ref, gssm_ref,
                  wbs_ref, wba_ref, wo_ref, x1_ref):
    d = x_ref.shape[1]
    xs = xs_ref[...].astype(F32)
    y = yf_ref[...].astype(F32) + yb_ref[...].astype(F32) + xs * dskip_ref[...]
    z = z_ref[...].astype(F32)
    y = _rms(y * (z * _sigmoid(z)), gssm_ref[...])
    br_ssm = jnp.dot(y.astype(BF16), wbs_ref[...], preferred_element_type=F32)
    br_attn = jnp.dot(attn_ref[...], wba_ref[...], preferred_element_type=F32)
    g_s = _sigmoid(gate_ref[:, :d].astype(F32))
    g_a = _sigmoid(gate_ref[:, d:].astype(F32))
    merged = (g_s * br_ssm + g_a * br_attn).astype(BF16)
    x1_ref[...] = x_ref[...] + jnp.dot(merged, wo_ref[...], preferred_element_type=F32)


def _merge(y_f, y_b, xbc, proj, attn, x2, dskip_row, g_ssm, w_br_ssm, w_br_attn, w_out):
    n, d = x2.shape
    tm = min(512, n)
    row = lambda i: (i, 0)
    const = lambda i: (0, 0)
    return pl.pallas_call(
        _merge_kernel,
        grid=(n // tm,),
        in_specs=[
            pl.BlockSpec((tm, d), row),
            pl.BlockSpec((tm, d), row),
            pl.BlockSpec((tm, d), row),
            pl.BlockSpec((tm, d), lambda i: (i, Z_OFF // d)),
            pl.BlockSpec((tm, 2 * d), lambda i: (i, GATE_OFF // (2 * d))),
            pl.BlockSpec((tm, d), row),
            pl.BlockSpec((tm, d), row),
            pl.BlockSpec((1, d), const),
            pl.BlockSpec((1, d), const),
            pl.BlockSpec((d, d), const),
            pl.BlockSpec((d, d), const),
            pl.BlockSpec((d, d), const),
        ],
        out_specs=pl.BlockSpec((tm, d), row),
        out_shape=jax.ShapeDtypeStruct((n, d), F32),
        compiler_params=_cparams(("arbitrary",)),
    )(y_f, y_b, xbc, proj, proj, attn, x2, dskip_row, g_ssm, w_br_ssm, w_br_attn, w_out)


def _router_kernel(x1_ref, g_ref, w_ref, b_ref, idx_ref, wt_ref, cnt_ref):
    h = _rms(x1_ref[...], g_ref[...])
    h_hi = h.astype(BF16)
    h_lo = (h - h_hi.astype(F32)).astype(BF16)
    logits = jnp.dot(jnp.concatenate([h_hi, h_hi, h_lo], axis=1), w_ref[...],
                     preferred_element_type=F32) + b_ref[...]
    lane = lax.broadcasted_iota(jnp.int32, logits.shape, 1)
    idx_out = jnp.zeros(logits.shape, jnp.int32)
    val_out = jnp.zeros(logits.shape, F32)
    chosen = jnp.zeros(logits.shape, F32)
    vals = []
    for k in range(TOP_K):
        m = jnp.max(logits, axis=-1, keepdims=True)
        idx = jnp.min(jnp.where(logits == m, lane, LANES), axis=-1, keepdims=True)
        idx_out = jnp.where(lane == k, idx, idx_out)
        vals.append(m)
        hit = lane == idx
        chosen = jnp.where(hit, 1.0, chosen)
        logits = jnp.where(hit, -jnp.inf, logits)
    es = [jnp.exp(v - vals[0]) for v in vals]
    tot = es[0] + es[1] + es[2] + es[3]
    for k in range(TOP_K):
        val_out = jnp.where(lane == k, es[k] / tot, val_out)
    idx_ref[...] = idx_out
    wt_ref[...] = val_out
    cnt_ref[0] = jnp.broadcast_to(jnp.sum(chosen, axis=0, keepdims=True), cnt_ref.shape[1:])


def _router(x1, g_ffn, w_router_pad, b_router_pad):
    n, d = x1.shape
    tm = min(MOE_TILE, n)
    return pl.pallas_call(
        _router_kernel,
        grid=(n // tm,),
        in_specs=[
            pl.BlockSpec((tm, d), lambda i: (i, 0)),
            pl.BlockSpec((1, d), lambda i: (0, 0)),
            pl.BlockSpec((3 * d, LANES), lambda i: (0, 0)),
            pl.BlockSpec((1, LANES), lambda i: (0, 0)),
        ],
        out_specs=[pl.BlockSpec((tm, LANES), lambda i: (i, 0)),
                   pl.BlockSpec((tm, LANES), lambda i: (i, 0)),
                   pl.BlockSpec((1, 8, LANES), lambda i: (i, 0, 0))],
        out_shape=[jax.ShapeDtypeStruct((n, LANES), jnp.int32),
                   jax.ShapeDtypeStruct((n, LANES), F32),
                   jax.ShapeDtypeStruct((n // tm, 8, LANES), F32)],
        compiler_params=_cparams(("arbitrary",)),
    )(x1, g_ffn, w_router_pad, b_router_pad)


def _routing_plan(cnt, n, tm, tb):
    nt = n // tm
    gran = MOE_GRAN
    rc = (cnt + gran - 1) // gran * gran
    covered = jnp.sum(rc, axis=0)
    region = (covered + tb - 1) // tb * tb
    pad_end = jnp.cumsum(region)
    pad_start = pad_end - region
    seg_start = pad_start[None, :] + jnp.cumsum(rc, axis=0) - rc
    stage_off = jnp.cumsum(rc, axis=1) - rc
    n_used = pad_end[-1] // tb
    nb = (TOP_K * n + nt * N_EXPERTS * (gran - 1) + N_EXPERTS * (tb - 1) + tb - 1) // tb
    blk = jnp.arange(nb, dtype=jnp.int32)
    block_e = jnp.sum((pad_end[None, :] <= (jnp.minimum(blk, n_used - 1) * tb)[:, None]).astype(jnp.int32), axis=1)
    block_e = jnp.minimum(block_e, N_EXPERTS - 1)
    flat = lambda a: a.reshape(-1).astype(jnp.int32)
    tail_gran = (region - covered) // gran
    ngran = rc // gran
    g_end = jnp.cumsum(ngran, axis=1)
    gi = jnp.arange(_stage_rows(tm) // gran, dtype=jnp.int32)
    g_exp = jnp.minimum(jnp.sum((g_end[:, None, :] <= gi[None, :, None]).astype(jnp.int32), axis=2), N_EXPERTS - 1)
    mine = g_exp[:, :, None] == jnp.arange(N_EXPERTS, dtype=jnp.int32)[None, None, :]
    seg_base = seg_start - gran * (g_end - ngran)
    gran_dst = jnp.sum(jnp.where(mine, seg_base[:, None, :], 0), axis=2) + gran * gi[None, :]
    ids = jnp.arange(N_EXPERTS, dtype=jnp.int32)
    later = jnp.where((ids[None, :] > ids[:, None]) & (region[None, :] > 0), ids[None, :], N_EXPERTS)
    nxt = jnp.min(later, axis=1)
    next_expert = jnp.where(nxt < N_EXPERTS, nxt, ids)
    return dict(gran_dst=flat(gran_dst), tot_gran=flat(g_end[:, -1]), block_e=flat(block_e),
                next_expert=flat(next_expert),
                n_used=flat(n_used), nb=nb, stage_off_f=stage_off.astype(F32),
                tail_start=flat(pad_start + covered), tail_gran=flat(tail_gran),
                tot_tail=flat(jnp.sum(tail_gran)))


def _stage_rows(tm):
    rows = TOP_K * tm + N_EXPERTS * (MOE_GRAN - 1)
    return (rows + 2 * LANES - 1) // (2 * LANES) * (2 * LANES)


GRAN_UNROLL = 4


def _granule_copies(t, gdst_ref, totg_ref, make_copy, start):
    total = totg_ref[t]
    base = t * (gdst_ref.shape[0] // totg_ref.shape[0])
    full = lax.shift_right_logical(total, GRAN_UNROLL.bit_length() - 1)

    def one(i):
        make_copy(pl.multiple_of(i * MOE_GRAN, MOE_GRAN), pl.multiple_of(gdst_ref[base + i], MOE_GRAN),
                  MOE_GRAN).start()

    def group(q, c):
        if start:
            for u in range(GRAN_UNROLL):
                one(q * GRAN_UNROLL + u)
        else:
            make_copy(0, 0, GRAN_UNROLL * MOE_GRAN).wait()
        return c

    def rest(i, c):
        if start:
            one(i)
        else:
            make_copy(0, 0, MOE_GRAN).wait()
        return c

    lax.fori_loop(0, full, group, 0)
    lax.fori_loop(full * GRAN_UNROLL, total, rest, 0)


def _dispatch_kernel(gdst_ref, totg_ref, tstart_ref, tgran_ref, misc_ref,
                     x1_ref, g_ref, idxt_ref, soffc_ref, xs_hbm, stage, zbuf, sem, zsem, *, n_blocks):
    t = pl.program_id(0)
    slot = t % 2
    tm = x1_ref.shape[0]
    rows = stage.shape[1]
    h = _rms(x1_ref[...], g_ref[...]).astype(BF16)
    idxt = idxt_ref[...]
    expert = lax.broadcasted_iota(jnp.int32, (LANES, tm), 0)
    hots = [idxt[k:k + 1, :] == expert for k in range(TOP_K)]
    multi = jnp.zeros((LANES, tm), F32)
    for hot in hots:
        multi = jnp.where(hot, 1.0, multi)
    earlier = (lax.broadcasted_iota(jnp.int32, (tm, tm), 0)
               < lax.broadcasted_iota(jnp.int32, (tm, tm), 1)).astype(BF16)
    rank = jnp.dot(multi.astype(BF16), earlier, preferred_element_type=F32)
    pos = rank + soffc_ref[0]
    srows = [jnp.sum(jnp.where(hot, pos, 0.0), axis=0, keepdims=True) for hot in hots]
    chunk = 2 * LANES
    row_id = lax.broadcasted_iota(jnp.int32, (chunk, tm), 0).astype(F32)
    for c in range(rows // chunk):
        perm = jnp.zeros((chunk, tm), F32)
        for srow in srows:
            perm = jnp.where(row_id == srow - float(c * chunk), 1.0, perm)
        stage[slot, c * chunk:(c + 1) * chunk, :] = jnp.dot(
            perm.astype(BF16), h, preferred_element_type=F32).astype(stage.dtype)

    def copy_from(s):
        def copy(stage_row, sorted_row, nrows):
            return pltpu.make_async_copy(stage.at[s, pl.ds(stage_row, nrows)],
                                         xs_hbm.at[pl.ds(sorted_row, nrows)], sem.at[s])
        return copy

    _granule_copies(t, gdst_ref, totg_ref, copy_from(slot), start=True)

    @pl.when(t > 0)
    def _():
        _granule_copies(jnp.maximum(t - 1, 0), gdst_ref, totg_ref, copy_from(1 - slot), start=False)

    @pl.when(t == pl.num_programs(0) - 1)
    def _():
        _granule_copies(t, gdst_ref, totg_ref, copy_from(slot), start=False)
        tb = zbuf.shape[0]
        n_used = misc_ref[0]
        zbuf[...] = jnp.zeros_like(zbuf)

        def zero_gran(row):
            return pltpu.make_async_copy(zbuf.at[pl.ds(0, MOE_GRAN)], xs_hbm.at[pl.ds(row, MOE_GRAN)], zsem)

        def zero_block(blk):
            return pltpu.make_async_copy(zbuf, xs_hbm.at[pl.ds(pl.multiple_of(blk * tb, tb), tb)], zsem)

        def tails(e, c):
            def one(g, c2):
                zero_gran(pl.multiple_of(tstart_ref[e] + g * MOE_GRAN, MOE_GRAN)).start()
                return c2
            lax.fori_loop(0, tgran_ref[e], one, 0)
            return c

        def start_block(blk, c):
            zero_block(blk).start()
            return c

        def wait_gran(g, c):
            zero_gran(0).wait()
            return c

        def wait_block(blk, c):
            zero_block(0).wait()
            return c

        lax.fori_loop(0, N_EXPERTS, tails, 0)
        lax.fori_loop(n_used, n_blocks, start_block, 0)
        lax.fori_loop(0, misc_ref[1], wait_gran, 0)
        lax.fori_loop(n_used, n_blocks, wait_block, 0)


def _dispatch(x1, g_ffn, idx_t, plan, tm):
    n, d = x1.shape
    nt = n // tm
    rows = _stage_rows(tm)
    soff_col = plan["stage_off_f"].reshape(nt, N_EXPERTS, 1)
    soff_col = jnp.pad(soff_col, ((0, 0), (0, LANES - N_EXPERTS), (0, 0)))
    misc = jnp.concatenate([plan["n_used"], plan["tot_tail"]])
    grid_spec = pltpu.PrefetchScalarGridSpec(
        num_scalar_prefetch=5,
        grid=(nt,),
        in_specs=[
            pl.BlockSpec((tm, d), lambda t, *_: (t, 0)),
            pl.BlockSpec((1, d), lambda t, *_: (0, 0)),
            pl.BlockSpec((8, tm), lambda t, *_: (0, t)),
            pl.BlockSpec((1, LANES, 1), lambda t, *_: (t, 0, 0)),
        ],
        out_specs=pl.BlockSpec(memory_space=pl.ANY),
        scratch_shapes=[pltpu.VMEM((2, rows, d), BF16), pltpu.VMEM((MOE_BLOCK, d), BF16),
                        pltpu.SemaphoreType.DMA((2,)), pltpu.SemaphoreType.DMA(())],
    )
    return pl.pallas_call(
        functools.partial(_dispatch_kernel, n_blocks=plan["nb"]),
        grid_spec=grid_spec,
        out_shape=jax.ShapeDtypeStruct((plan["nb"] * MOE_BLOCK, d), BF16),
        compiler_params=_cparams(("arbitrary",)),
    )(plan["gran_dst"], plan["tot_gran"], plan["tail_start"], plan["tail_gran"], misc, x1, g_ffn, idx_t,
      soff_col)


def _expert_kernel(be_ref, nused_ref, nexte_ref, x_ref, w1_hbm, b1_ref, w2_hbm, b2_ref, o_ref,
                   w1f, w2f, w1b, w2b, sem):
    de = w2f.shape[0]
    j = pl.program_id(0)
    live = j < nused_ref[0]
    e = be_ref[j]
    new_expert = jnp.logical_or(j == 0, e != be_ref[jnp.maximum(j - 1, 0)])

    def fetch(ex):
        return (pltpu.make_async_copy(w1_hbm.at[ex], w1f, sem.at[0]),
                pltpu.make_async_copy(w2_hbm.at[ex], w2f, sem.at[1]))

    @pl.when(jnp.logical_and(live, j == 0))
    def _():
        for c in fetch(e):
            c.start()

    @pl.when(jnp.logical_and(live, new_expert))
    def _():
        for c in fetch(e):
            c.wait()
        w1b[...] = w1f[...].astype(BF16)
        w2b[...] = w2f[...].astype(BF16)
        nxt = nexte_ref[e]

        @pl.when(nxt != e)
        def _():
            for c in fetch(nxt):
                c.start()

    @pl.when(live)
    def _():
        hm = jnp.dot(x_ref[...], w1b[...], preferred_element_type=F32) + b1_ref[0]
        gate = jnp.minimum(hm[:, :de], SWIGLU_LIMIT)
        up = jnp.clip(hm[:, de:], -SWIGLU_LIMIT, SWIGLU_LIMIT)
        act = gate * _sigmoid(SWIGLU_ALPHA * gate) * (up + 1.0)
        y = jnp.dot(act.astype(BF16), w2b[...], preferred_element_type=F32) + b2_ref[0]
        o_ref[...] = y.astype(o_ref.dtype)

    @pl.when(jnp.logical_not(live))
    def _():
        o_ref[...] = jnp.zeros_like(o_ref)


def _experts(xs, plan, w1, b1, w2, b2):
    d = xs.shape[1]
    tb = MOE_BLOCK
    f2 = w1.shape[2]
    de = w2.shape[1]
    last = lambda j, nu: jnp.maximum(jnp.minimum(j, nu[0] - 1), 0)
    grid_spec = pltpu.PrefetchScalarGridSpec(
        num_scalar_prefetch=3,
        grid=(plan["nb"],),
        in_specs=[
            pl.BlockSpec((tb, d), lambda j, be, nu, ne: (last(j, nu), 0)),
            pl.BlockSpec(memory_space=pl.ANY),
            pl.BlockSpec((1, 1, f2), lambda j, be, nu, ne: (be[j], 0, 0)),
            pl.BlockSpec(memory_space=pl.ANY),
            pl.BlockSpec((1, 1, d), lambda j, be, nu, ne: (be[j], 0, 0)),
        ],
        out_specs=pl.BlockSpec((tb, d), lambda j, be, nu, ne: (j, 0)),
        scratch_shapes=[pltpu.VMEM((d, f2), F32), pltpu.VMEM((de, d), F32),
                        pltpu.VMEM((d, f2), BF16), pltpu.VMEM((de, d), BF16),
                        pltpu.SemaphoreType.DMA((2,))],
    )
    return pl.pallas_call(
        _expert_kernel,
        grid_spec=grid_spec,
        out_shape=jax.ShapeDtypeStruct(xs.shape, BF16),
        compiler_params=_cparams(("arbitrary",)),
    )(plan["block_e"], plan["n_used"], plan["next_expert"], xs, w1, b1, w2, b2)


def _combine_kernel(gdst_ref, totg_ref, x1_ref, idx_ref, wt_ref, soffr_ref, g_ref, yb_hbm, o_ref, stage, sem):
    t = pl.program_id(0)
    slot = t % 2
    tm = x1_ref.shape[0]
    rows = stage.shape[1]

    def copy_into(s):
        def copy(stage_row, sorted_row, nrows):
            return pltpu.make_async_copy(yb_hbm.at[pl.ds(sorted_row, nrows)],
                                         stage.at[s, pl.ds(stage_row, nrows)], sem.at[s])
        return copy

    @pl.when(t == 0)
    def _():
        stage[...] = jnp.zeros_like(stage)
        _granule_copies(t, gdst_ref, totg_ref, copy_into(slot), start=True)

    @pl.when(t + 1 < pl.num_programs(0))
    def _():
        _granule_copies(t + 1, gdst_ref, totg_ref, copy_into(1 - slot), start=True)

    idx = idx_ref[...]
    wt = wt_ref[...]
    expert = lax.broadcasted_iota(jnp.int32, (tm, LANES), 1)
    hots = [idx[:, k:k + 1] == expert for k in range(TOP_K)]
    multi = jnp.zeros((tm, LANES), F32)
    for hot in hots:
        multi = jnp.where(hot, 1.0, multi)
    earlier = (lax.broadcasted_iota(jnp.int32, (tm, tm), 0)
               > lax.broadcasted_iota(jnp.int32, (tm, tm), 1)).astype(BF16)
    rank = jnp.dot(earlier, multi.astype(BF16), preferred_element_type=F32)
    pos = rank + soffr_ref[0]
    srows = [jnp.sum(jnp.where(hot, pos, 0.0), axis=-1, keepdims=True) for hot in hots]
    chunk = LANES
    row_id = lax.broadcasted_iota(jnp.int32, (chunk, rows), 1).astype(F32)
    for c in range(tm // chunk):
        sl = slice(c * chunk, (c + 1) * chunk)
        unsort = jnp.zeros((chunk, rows), F32)
        for k, srow in enumerate(srows):
            unsort = jnp.where(row_id == srow[sl], wt[sl, k:k + 1], unsort)
        if c == 0:
            _granule_copies(t, gdst_ref, totg_ref, copy_into(slot), start=False)
        y = jnp.dot(unsort.astype(BF16), stage[slot], preferred_element_type=F32)
        o_ref[sl, :] = _rms(x1_ref[sl, :] + y, g_ref[...])


def _combine(x1, yb, top_idx, top_w, plan, g_final, tm):
    n, d = x1.shape
    nt = n // tm
    rows = _stage_rows(tm)
    soff_row = jnp.pad(plan["stage_off_f"], ((0, 0), (0, LANES - N_EXPERTS))).reshape(nt, 1, LANES)
    grid_spec = pltpu.PrefetchScalarGridSpec(
        num_scalar_prefetch=2,
        grid=(nt,),
        in_specs=[
            pl.BlockSpec((tm, d), lambda t, *_: (t, 0)),
            pl.BlockSpec((tm, LANES), lambda t, *_: (t, 0)),
            pl.BlockSpec((tm, LANES), lambda t, *_: (t, 0)),
            pl.BlockSpec((1, 1, LANES), lambda t, *_: (t, 0, 0)),
            pl.BlockSpec((1, d), lambda t, *_: (0, 0)),
            pl.BlockSpec(memory_space=pl.ANY),
        ],
        out_specs=pl.BlockSpec((tm, d), lambda t, *_: (t, 0)),
        scratch_shapes=[pltpu.VMEM((2, rows, d), BF16), pltpu.SemaphoreType.DMA((2,))],
    )
    return pl.pallas_call(
        _combine_kernel,
        grid_spec=grid_spec,
        out_shape=jax.ShapeDtypeStruct((n, d), F32),
        compiler_params=_cparams(("arbitrary",)),
    )(plan["gran_dst"], plan["tot_gran"], x1, top_idx, top_w, soff_row, g_final, yb)


def _rope_tables(seq):
    half = ATTN_HEAD_DIM // 2
    inv_freq = np.float32(ROPE_THETA) ** (-np.arange(0, half, 2, dtype=np.float32) / np.float32(half))
    ang_row = np.arange(seq // GRID_W, dtype=np.float32)[:, None] * inv_freq
    ang_col = np.arange(GRID_W, dtype=np.float32)[:, None] * inv_freq
    by_row = lambda t: jnp.repeat(jnp.asarray(t, F32), GRID_W, axis=0)
    by_col = lambda t: jnp.tile(jnp.asarray(t, F32), (seq // GRID_W, 1))
    cos_r, sin_r = by_row(np.cos(ang_row)), by_row(np.sin(ang_row))
    cos_c, sin_c = by_col(np.cos(ang_col)), by_col(np.sin(ang_col))
    cos_t = jnp.concatenate([cos_r, cos_c] * 2, axis=-1)
    sin_t = jnp.concatenate([-sin_r, -sin_c, sin_r, sin_c], axis=-1)
    return cos_t, sin_t


def _token_mixer(x2, batch, seq, g_mix, w_in, conv_w, conv_b, dt_bias_f, dt_bias_b, a_log_f, a_log_b, d_skip,
                 g_ssm, q_norm_g, k_norm_g, w_br_ssm, w_br_attn, w_out):
    n, d = x2.shape
    z_end = SSM_INNER
    xbc_end = z_end + CONV_CH
    dtf_end = xbc_end + SSM_HEADS
    dtb_end = dtf_end + SSM_HEADS
    q_end = dtb_end + ATTN_HEADS * ATTN_HEAD_DIM
    k_end = q_end + ATTN_KV_HEADS * ATTN_HEAD_DIM
    v_end = k_end + ATTN_KV_HEADS * ATTN_HEAD_DIM
    head_cols = lambda w: _rope_head_order(w.reshape(d, -1, ATTN_HEAD_DIM)).reshape(d, -1)
    w_main = jnp.concatenate([w_in[:, :z_end], head_cols(w_in[:, dtb_end:q_end]), w_in[:, v_end:],
                              w_in[:, z_end:xbc_end], head_cols(w_in[:, q_end:k_end]), w_in[:, k_end:v_end]],
                             axis=1).astype(BF16)
    w_dt = jnp.pad(w_in[:, xbc_end:dtb_end], ((0, 0), (0, LANES - 2 * SSM_HEADS))).astype(BF16)

    proj, dt = _in_proj(x2, g_mix.reshape(1, d), w_main, w_dt)
    proj3 = proj.reshape(batch, seq, PROJ_COLS)

    xbc = _conv(proj3, conv_w, conv_b.reshape(1, CONV_CH))

    dt3 = dt.reshape(batch, seq, LANES)
    dtt3 = jnp.swapaxes(dt3[:, :, :2 * SSM_HEADS], 1, 2)
    bias = jnp.concatenate([dt_bias_f, dt_bias_b])
    alog = jnp.concatenate([a_log_f, a_log_b])
    pad_row = lambda v: jnp.pad(v, (0, LANES - 2 * SSM_HEADS)).reshape(1, LANES)
    y_f, y_b = _ssd(xbc, dt3, dtt3, pad_row(bias), bias.reshape(-1, 1), pad_row(alog), alog.reshape(-1, 1))

    cos_t, sin_t = _rope_tables(seq)
    q_rot, k_rot = _qk_prep(proj, cos_t, sin_t, _rope_head_order(q_norm_g).reshape(1, -1),
                            _rope_head_order(k_norm_g).reshape(1, -1), seq)
    vt3 = jnp.swapaxes(proj3[:, :, V_OFF:V_OFF + ATTN_KV_HEADS * ATTN_HEAD_DIM], 1, 2)
    attn = _flash(q_rot.reshape(batch, seq, -1), k_rot.reshape(batch, seq, -1), vt3, q_norm_g, k_norm_g)

    return _merge(y_f.reshape(n, -1), y_b.reshape(n, -1), xbc.reshape(n, CONV_CH), proj, attn.reshape(n, -1),
                  x2, jnp.repeat(d_skip, SSM_HEAD_DIM).reshape(1, -1), g_ssm.reshape(1, -1),
                  w_br_ssm.astype(BF16), w_br_attn.astype(BF16), w_out.astype(BF16))


def _moe_and_final_norm(x1, g_ffn, w_router, b_router, w_mlp1, b_mlp1, w_mlp2, b_mlp2, g_final):
    n, d = x1.shape
    tm = min(MOE_TILE, n)
    w_r = jnp.pad(w_router, ((0, 0), (0, LANES - N_EXPERTS)))
    w_hi = w_r.astype(BF16)
    w_lo = (w_r - w_hi.astype(F32)).astype(BF16)
    w_r = jnp.concatenate([w_hi, w_lo, w_hi], axis=0)
    b_r = jnp.pad(b_router, (0, LANES - N_EXPERTS), constant_values=-jnp.inf).reshape(1, LANES)
    g_ffn_row = g_ffn.reshape(1, d)
    top_idx, top_w, cnt = _router(x1, g_ffn_row, w_r, b_r)
    plan = _routing_plan(cnt[:, 0, :N_EXPERTS].astype(jnp.int32), n, tm, MOE_BLOCK)
    idx_t = jnp.transpose(top_idx[:, :8])
    xs = _dispatch(x1, g_ffn_row, idx_t, plan, tm)
    yb = _experts(xs, plan, w_mlp1, b_mlp1[:, None, :], w_mlp2, b_mlp2[:, None, :])
    return _combine(x1, yb, top_idx, top_w, plan, g_final.reshape(1, d), tm)


def kernel(x, g_mix, w_in, conv_w, conv_b, dt_bias_f, dt_bias_b, a_log_f, a_log_b, d_skip, g_ssm, q_norm_g,
           k_norm_g, w_br_ssm, w_br_attn, w_out, g_ffn, w_router, b_router, w_mlp1, b_mlp1, w_mlp2, b_mlp2,
           g_final):
    batch, seq, d = x.shape
    assert g_mix.shape[0] == 1, "single-layer model: the final rmsnorm is fused into the MoE combine"
    x2 = x.reshape(batch * seq, d)
    x1 = _token_mixer(x2, batch, seq, g_mix[0], w_in[0], conv_w[0], conv_b[0], dt_bias_f[0], dt_bias_b[0],
                      a_log_f[0], a_log_b[0], d_skip[0], g_ssm[0], q_norm_g[0], k_norm_g[0], w_br_ssm[0],
                      w_br_attn[0], w_out[0])
    out = _moe_and_final_norm(x1, g_ffn[0], w_router[0], b_router[0], w_mlp1[0], b_mlp1[0], w_mlp2[0],
                              b_mlp2[0], g_final)
    return out.reshape(batch, seq, d)
```

```python
import functools
import math

import jax
import jax.numpy as jnp
import numpy as np
from jax import lax
from jax.experimental import pallas as pl
from jax.experimental.pallas import tpu as pltpu

F32 = jnp.float32
BF16 = jnp.bfloat16

NORM_EPS = 1e-6
GRID_W = 64
SSM_HEADS = 16
SSM_HEAD_DIM = 64
SSM_INNER = SSM_HEADS * SSM_HEAD_DIM
SSM_GROUPS = 2
SSM_STATE = 128
SSM_CONV = 5
CONV_CH = SSM_INNER + 2 * SSM_GROUPS * SSM_STATE
ATTN_HEADS = 8
ATTN_KV_HEADS = 2
ATTN_HEAD_DIM = 128
ROPE_THETA = 10000.0
N_EXPERTS = 32
TOP_K = 4
SWIGLU_LIMIT = 7.0
SWIGLU_ALPHA = 1.702

LANES = 128
BF16_SUBLANES = 16
VMEM_LIMIT = 56 * 1024 * 1024

Z_OFF, Q_OFF, GATE_OFF, XBC_OFF = 0, 1024, 2048, 4096
K_OFF, V_OFF, PROJ_COLS = 5632, 5888, 6144

MOE_TILE = 512
MOE_BLOCK = 512
MOE_GRAN = BF16_SUBLANES


def _cparams(sem):
    return pltpu.CompilerParams(dimension_semantics=sem, vmem_limit_bytes=VMEM_LIMIT)


def _sigmoid(x):
    return 1.0 / (1.0 + jnp.exp(-x))


def _softplus(x):
    return jnp.maximum(x, 0.0) + jnp.log(1.0 + jnp.exp(-jnp.abs(x)))


def _rms(x, g):
    ms = jnp.mean(x * x, axis=-1, keepdims=True)
    return x * lax.rsqrt(ms + NORM_EPS) * g


def _inproj_kernel(x_ref, g_ref, w_ref, wdt_ref, o_ref, dt_ref, h_scr):
    @pl.when(pl.program_id(1) == 0)
    def _():
        hb = _rms(x_ref[...], g_ref[...]).astype(BF16)
        h_scr[...] = hb
        dt_ref[...] = jnp.dot(hb, wdt_ref[...], preferred_element_type=F32)

    o_ref[...] = jnp.dot(h_scr[...], w_ref[...], preferred_element_type=F32).astype(o_ref.dtype)


def _in_proj(x2, g_mix, w_main, w_dt):
    n, d = x2.shape
    tm = min(1024, n)
    tn = 2048
    return pl.pallas_call(
        _inproj_kernel,
        grid=(n // tm, PROJ_COLS // tn),
        in_specs=[
            pl.BlockSpec((tm, d), lambda i, j: (i, 0)),
            pl.BlockSpec((1, d), lambda i, j: (0, 0)),
            pl.BlockSpec((d, tn), lambda i, j: (0, j)),
            pl.BlockSpec((d, LANES), lambda i, j: (0, 0)),
        ],
        out_specs=[
            pl.BlockSpec((tm, tn), lambda i, j: (i, j)),
            pl.BlockSpec((tm, LANES), lambda i, j: (i, 0)),
        ],
        out_shape=[
            jax.ShapeDtypeStruct((n, PROJ_COLS), BF16),
            jax.ShapeDtypeStruct((n, LANES), F32),
        ],
        scratch_shapes=[pltpu.VMEM((tm, d), BF16)],
        compiler_params=_cparams(("arbitrary", "arbitrary")),
    )(x2, g_mix, w_main, w_dt)


def _conv_kernel(prev_ref, cur_ref, next_ref, w_ref, b_ref, o_ref, scr):
    s = pl.program_id(1)
    ts = cur_ref.shape[1]
    halo = BF16_SUBLANES
    pad = (SSM_CONV - 1) // 2
    prev = prev_ref[0].astype(F32)
    nxt = next_ref[0].astype(F32)
    scr[0:halo, :] = jnp.where(s == 0, 0.0, prev)
    scr[halo:halo + ts, :] = cur_ref[0].astype(F32)
    scr[halo + ts:2 * halo + ts, :] = jnp.where(s == pl.num_programs(1) - 1, 0.0, nxt)
    acc = jnp.zeros((ts, cur_ref.shape[2]), F32) + b_ref[...]
    for k in range(SSM_CONV):
        acc = acc + w_ref[k:k + 1, :] * scr[halo - pad + k:halo - pad + k + ts, :]
    o_ref[0] = (acc * _sigmoid(acc)).astype(o_ref.dtype)


def _conv(proj3, conv_w, conv_b):
    b, s, _ = proj3.shape
    ts = min(1024, s)
    tc = 512
    halo = BF16_SUBLANES
    hb = ts // halo
    col0 = XBC_OFF // tc
    return pl.pallas_call(
        _conv_kernel,
        grid=(b, s // ts, CONV_CH // tc),
        in_specs=[
            pl.BlockSpec((1, halo, tc), lambda bi, si, ci: (bi, jnp.maximum(si * hb - 1, 0), col0 + ci)),
            pl.BlockSpec((1, ts, tc), lambda bi, si, ci: (bi, si, col0 + ci)),
            pl.BlockSpec((1, halo, tc),
                         lambda bi, si, ci: (bi, jnp.minimum((si + 1) * hb, s // halo - 1), col0 + ci)),
            pl.BlockSpec((SSM_CONV, tc), lambda bi, si, ci: (0, ci)),
            pl.BlockSpec((1, tc), lambda bi, si, ci: (0, ci)),
        ],
        out_specs=pl.BlockSpec((1, ts, tc), lambda bi, si, ci: (bi, si, ci)),
        out_shape=jax.ShapeDtypeStruct((b, s, CONV_CH), BF16),
        scratch_shapes=[pltpu.VMEM((ts + 2 * halo, tc), F32)],
        compiler_params=_cparams(("arbitrary", "arbitrary", "arbitrary")),
    )(proj3, proj3, proj3, conv_w, conv_b)


def _ssd_chunk(xf_ref, xb_ref, dtf_ref, dtb_ref, dttf_ref, dttb_ref, brow_ref, bcol_ref,
               arow_ref, acol_ref, yf_ref, yb_ref, st_ref):
    L = xf_ref.shape[1]
    hg = SSM_HEADS // SSM_GROUPS
    pairs = hg // 2

    rows = lax.broadcasted_iota(jnp.int32, (L, L), 0)
    cols = lax.broadcasted_iota(jnp.int32, (L, L), 1)
    lower = rows >= cols
    upper = rows <= cols
    ltri = lower.astype(BF16)
    utri = upper.astype(BF16)
    lane = lax.broadcasted_iota(jnp.int32, (L, LANES), 1)
    first_half = lane < SSM_HEAD_DIM
    lane1 = lax.broadcasted_iota(jnp.int32, (1, LANES), 1)
    log2e = math.log2(math.e)
    a_row = -jnp.exp(arow_ref[...]) * log2e
    a_col = -jnp.exp(acol_ref[...]) * log2e

    def split3(v):
        hi = v.astype(BF16)
        r1 = v - hi.astype(F32)
        mid = r1.astype(BF16)
        return hi, mid, (r1 - mid.astype(F32)).astype(BF16)

    def cumsum_cols(tri, v):
        return jnp.dot(jnp.concatenate([tri] * 3, axis=1), jnp.concatenate(split3(v), axis=0),
                       preferred_element_type=F32)

    def cumsum_rows(v, tri):
        return jnp.dot(jnp.concatenate(split3(v), axis=1), jnp.concatenate([tri] * 3, axis=0),
                       preferred_element_type=F32)

    for d in range(2):
        x_ref, dt_ref, dtt_ref, y_ref = ((xf_ref, dtf_ref, dttf_ref, yf_ref) if d == 0
                                         else (xb_ref, dtb_ref, dttb_ref, yb_ref))
        a = _softplus(dt_ref[0] + brow_ref[...]) * a_row
        dt_t = _softplus(dtt_ref[0] + bcol_ref[...])
        a_t = dt_t * a_col
        if d == 0:
            cs_col = cumsum_cols(ltri, a)
            cs_row = cumsum_rows(a_t, utri)
            tot = cs_col[L - 1:L, :]
            tot_t = cs_row[:, L - 1:L]
            mask = lower
        else:
            cs_col = cumsum_cols(utri, a)
            cs_row = cumsum_rows(a_t, ltri)
            tot = cs_col[0:1, :]
            tot_t = cs_row[:, 0:1]
            mask = upper
        w_t = dt_t * jnp.exp2(tot_t - cs_row)
        src_t = cs_row - jnp.log2(dt_t)
        chunk_decay = jnp.exp2(tot)

        for g in range(SSM_GROUPS):
            boff = SSM_INNER + g * SSM_STATE
            coff = SSM_INNER + SSM_GROUPS * SSM_STATE + g * SSM_STATE
            bm = x_ref[0, :, boff:boff + SSM_STATE]
            cm = x_ref[0, :, coff:coff + SSM_STATE]
            cb = lax.dot_general(cm, bm, (((1,), (1,)), ((), ())), preferred_element_type=F32)
            bt = bm.astype(F32).T
            st = st_ref[d, g]
            y_off = jnp.dot(cm, st.astype(BF16), preferred_element_type=F32)
            for pr in range(pairs):
                h0 = d * SSM_HEADS + g * hg + 2 * pr
                xoff = (g * pairs + pr) * LANES
                xs = x_ref[0, :, xoff:xoff + LANES]
                zero = jnp.zeros_like(xs)
                rhs = jnp.concatenate([jnp.where(first_half, xs, zero),
                                       jnp.where(first_half, zero, xs)], axis=0)
                ms, ws, dins = [], [], []
                for hh in (h0, h0 + 1):
                    cs_b = jnp.broadcast_to(cs_col[:, hh:hh + 1], (L, L))
                    seg = cs_b - src_t[hh:hh + 1, :]
                    m = cb * jnp.exp2(jnp.where(mask, seg, -jnp.inf))
                    ms.append(m.astype(BF16))
                    ws.append((bt * w_t[hh:hh + 1, :]).astype(BF16))
                    dins.append(jnp.exp2(cs_b))
                y = jnp.dot(jnp.concatenate(ms, axis=1), rhs, preferred_element_type=F32)
                y = y + y_off[:, pr * LANES:(pr + 1) * LANES] * jnp.where(first_half, dins[0], dins[1])
                y_ref[0, :, xoff:xoff + LANES] = y.astype(y_ref.dtype)
                new_st = jnp.dot(jnp.concatenate(ws, axis=1), rhs, preferred_element_type=F32)
                cd = jnp.where(lane1 < SSM_HEAD_DIM, chunk_decay[:, h0:h0 + 1], chunk_decay[:, h0 + 1:h0 + 2])
                st_ref[d, g, :, pr * LANES:(pr + 1) * LANES] = st[:, pr * LANES:(pr + 1) * LANES] * cd + new_st


def _rope_head_order(a):
    q4 = ATTN_HEAD_DIM // 4
    return jnp.concatenate([a[..., 0:q4], a[..., 2 * q4:3 * q4], a[..., q4:2 * q4], a[..., 3 * q4:]], axis=-1)


def _rope_norm(t, g, cos, sin_signed, ones):
    sq = t * t
    hi = sq.astype(BF16)
    lo = (sq - hi.astype(F32)).astype(BF16)
    ms = jnp.dot(jnp.concatenate([hi, lo], axis=1), ones, preferred_element_type=F32) * (1.0 / ATTN_HEAD_DIM)
    tn = t * lax.rsqrt(ms + NORM_EPS) * g
    return tn * cos + pltpu.roll(tn, ATTN_HEAD_DIM // 2, 1) * sin_signed


Q_SCALE = ATTN_HEAD_DIM ** -0.5 * math.log2(math.e)


def _qkprep_kernel(q_ref, k_ref, cos_ref, sin_ref, qg_ref, kg_ref, qo_ref, ko_ref):
    cos = cos_ref[...]
    sin = sin_ref[...]
    ones = jnp.ones((2 * ATTN_HEAD_DIM, ATTN_HEAD_DIM), BF16)

    def heads(src_ref, g_ref, dst_ref, n_heads, scale):
        for h in range(n_heads):
            sl = slice(h * ATTN_HEAD_DIM, (h + 1) * ATTN_HEAD_DIM)
            r = _rope_norm(src_ref[:, sl].astype(F32), g_ref[...], cos, sin, ones) * scale
            dst_ref[:, sl] = r.astype(dst_ref.dtype)

    heads(q_ref, qg_ref, qo_ref, ATTN_HEADS, Q_SCALE)
    heads(k_ref, kg_ref, ko_ref, ATTN_KV_HEADS, 1.0)


def _qk_prep(proj, cos_t, sin_t, q_norm_g, k_norm_g, seq):
    n = proj.shape[0]
    tm = min(512, seq)
    qw = ATTN_HEADS * ATTN_HEAD_DIM
    kw = ATTN_KV_HEADS * ATTN_HEAD_DIM
    spt = seq // tm
    return pl.pallas_call(
        _qkprep_kernel,
        grid=(n // tm,),
        in_specs=[
            pl.BlockSpec((tm, qw), lambda i: (i, Q_OFF // qw)),
            pl.BlockSpec((tm, kw), lambda i: (i, K_OFF // kw)),
            pl.BlockSpec((tm, ATTN_HEAD_DIM), lambda i: (i % spt, 0)),
            pl.BlockSpec((tm, ATTN_HEAD_DIM), lambda i: (i % spt, 0)),
            pl.BlockSpec((1, ATTN_HEAD_DIM), lambda i: (0, 0)),
            pl.BlockSpec((1, ATTN_HEAD_DIM), lambda i: (0, 0)),
        ],
        out_specs=[
            pl.BlockSpec((tm, qw), lambda i: (i, 0)),
            pl.BlockSpec((tm, kw), lambda i: (i, 0)),
        ],
        out_shape=[jax.ShapeDtypeStruct((n, qw), BF16), jax.ShapeDtypeStruct((n, kw), BF16)],
        compiler_params=_cparams(("arbitrary",)),
    )(proj, proj, cos_t, sin_t, q_norm_g, k_norm_g)


def _attn_ssd_kernel(small_ref, q_ref, k_ref, vt_ref, *rest, tk, th):
    ssd_refs, o_ref = rest[:10], rest[10]
    yf_ref, yb_ref, st_ref = rest[11:]

    @pl.when(jnp.logical_and(pl.program_id(1) == 0, pl.program_id(2) == 0))
    def _():
        st_ref[...] = jnp.zeros_like(st_ref)

    _flash_tile(small_ref, q_ref, k_ref, vt_ref, o_ref, tk=tk, th=th,
                side_work=lambda: _ssd_chunk(*ssd_refs, yf_ref, yb_ref, st_ref))


def _flash_tile(small_ref, q_ref, k_ref, vt_ref, o_ref, *, tk, th, side_work):
    q = q_ref[0]
    tq = q.shape[0]
    seq = k_ref.shape[1]
    nt = (((1,), (1,)), ((), ()))
    small = small_ref[0] != 0

    @pl.when(small)
    def _():
        side_work()
        l = acc_t = None
        for c in range(seq // th):
            ks = slice(c * th, (c + 1) * th)
            p_t = jnp.exp2(lax.dot_general(k_ref[0, ks, :], q, nt, preferred_element_type=F32))
            l_c = jnp.sum(p_t, axis=0, keepdims=True)
            a_c = jnp.dot(vt_ref[0, :, ks], p_t.astype(BF16), preferred_element_type=F32)
            l = l_c if l is None else l + l_c
            acc_t = a_c if acc_t is None else acc_t + a_c
        o_ref[0] = (acc_t / l).T.astype(o_ref.dtype)

    @pl.when(jnp.logical_not(small))
    def _():
        side_work()

        def body(i, carry):
            m, l, acc_t = carry
            off = pl.multiple_of(i * tk, tk)
            s_t = lax.dot_general(k_ref[0, pl.ds(off, tk), :], q, nt, preferred_element_type=F32)
            m_new = jnp.maximum(m, jnp.max(s_t, axis=0, keepdims=True))
            alpha = jnp.exp2(m - m_new)
            p_t = jnp.exp2(s_t - m_new)
            l = alpha * l + jnp.sum(p_t, axis=0, keepdims=True)
            acc_t = alpha * acc_t + jnp.dot(vt_ref[0, :, pl.ds(off, tk)], p_t.astype(BF16),
                                            preferred_element_type=F32)
            return m_new, l, acc_t

        init = (jnp.full((1, tq), -jnp.inf, F32), jnp.zeros((1, tq), F32),
                jnp.zeros((ATTN_HEAD_DIM, tq), F32))
        _, l, acc_t = lax.fori_loop(0, seq // tk, body, init)
        o_ref[0] = (acc_t / l).T.astype(o_ref.dtype)


SCORE_BOUND = 59.0


def _attn_ssd(q3, k3, vt3, q_norm_g, k_norm_g, xbc, dt3, dtt3, bias_row, bias_col, alog_row, alog_col):
    b, s, _ = q3.shape
    tq = min(1024, s)
    tk = min(2048, s)
    th = min(4096, s)
    nq = s // tq
    grp = ATTN_HEADS // ATTN_KV_HEADS
    hd = ATTN_HEAD_DIM
    L = SSM_STATE
    nc = s // L
    assert ATTN_HEADS * nq == nc, "one SSD chunk per attention step"
    hgd = SSM_HEADS // SSM_GROUPS * SSM_HEAD_DIM
    chunk = lambda h, qi: h * nq + qi
    fwd = lambda bi, h, qi, sm: (bi, chunk(h, qi), 0)
    bwd = lambda bi, h, qi, sm: (bi, nc - 1 - chunk(h, qi), 0)
    fwd_t = lambda bi, h, qi, sm: (bi, 0, chunk(h, qi))
    bwd_t = lambda bi, h, qi, sm: (bi, 0, nc - 1 - chunk(h, qi))
    const = lambda bi, h, qi, sm: (0, 0)
    bound = hd * Q_SCALE * jnp.max(jnp.abs(q_norm_g)) * jnp.max(jnp.abs(k_norm_g)) * 1.02
    small = (bound <= SCORE_BOUND).astype(jnp.int32).reshape(1)
    grid_spec = pltpu.PrefetchScalarGridSpec(
        num_scalar_prefetch=1,
        grid=(b, ATTN_HEADS, nq),
        in_specs=[
            pl.BlockSpec((1, tq, hd), lambda bi, h, qi, sm: (bi, qi, h)),
            pl.BlockSpec((1, s, hd), lambda bi, h, qi, sm: (bi, 0, h // grp)),
            pl.BlockSpec((1, hd, s), lambda bi, h, qi, sm: (bi, h // grp, 0)),
            pl.BlockSpec((1, L, CONV_CH), fwd),
            pl.BlockSpec((1, L, CONV_CH), bwd),
            pl.BlockSpec((1, L, LANES), fwd),
            pl.BlockSpec((1, L, LANES), bwd),
            pl.BlockSpec((1, 2 * SSM_HEADS, L), fwd_t),
            pl.BlockSpec((1, 2 * SSM_HEADS, L), bwd_t),
            pl.BlockSpec((1, LANES), const),
            pl.BlockSpec((2 * SSM_HEADS, 1), const),
            pl.BlockSpec((1, LANES), const),
            pl.BlockSpec((2 * SSM_HEADS, 1), const),
        ],
        out_specs=[
            pl.BlockSpec((1, tq, hd), lambda bi, h, qi, sm: (bi, qi, h)),
            pl.BlockSpec((1, L, SSM_INNER), fwd),
            pl.BlockSpec((1, L, SSM_INNER), bwd),
        ],
        scratch_shapes=[pltpu.VMEM((2, SSM_GROUPS, SSM_STATE, hgd), F32)],
    )
    return pl.pallas_call(
        functools.partial(_attn_ssd_kernel, tk=tk, th=th),
        grid_spec=grid_spec,
        out_shape=[jax.ShapeDtypeStruct((b, s, ATTN_HEADS * hd), BF16),
                   jax.ShapeDtypeStruct((b, s, SSM_INNER), BF16),
                   jax.ShapeDtypeStruct((b, s, SSM_INNER), BF16)],
        compiler_params=_cparams(("arbitrary", "arbitrary", "arbitrary")),
    )(small, q3, k3, vt3, xbc, xbc, dt3, dt3, dtt3, dtt3, bias_row, bias_col, alog_row, alog_col)


def _merge_kernel(yf_ref, yb_ref, xs_ref, z_ref, gate_ref, attn_ref, x_ref, dskip_ref, gssm_ref,
                  wbs_ref, wba_ref, wo_ref, x1_ref):
    d = x_ref.shape[1]
    xs = xs_ref[...].astype(F32)
    y = yf_ref[...].astype(F32) + yb_ref[...].astype(F32) + xs * dskip_ref[...]
    z = z_ref[...].astype(F32)
    y = _rms(y * (z * _sigmoid(z)), gssm_ref[...])
    br_ssm = jnp.dot(y.astype(BF16), wbs_ref[...], preferred_element_type=F32)
    br_attn = jnp.dot(attn_ref[...], wba_ref[...], preferred_element_type=F32)
    g_s = _sigmoid(gate_ref[:, :d].astype(F32))
    g_a = _sigmoid(gate_ref[:, d:].astype(F32))
    merged = (g_s * br_ssm + g_a * br_attn).astype(BF16)
    x1_ref[...] = x_ref[...] + jnp.dot(merged, wo_ref[...], preferred_element_type=F32)


def _merge(y_f, y_b, xbc, proj, attn, x2, dskip_row, g_ssm, w_br_ssm, w_br_attn, w_out):
    n, d = x2.shape
    tm = min(512, n)
    row = lambda i: (i, 0)
    const = lambda i: (0, 0)
    return pl.pallas_call(
        _merge_kernel,
        grid=(n // tm,),
        in_specs=[
            pl.BlockSpec((tm, d), row),
            pl.BlockSpec((tm, d), row),
            pl.BlockSpec((tm, d), row),
            pl.BlockSpec((tm, d), lambda i: (i, Z_OFF // d)),
            pl.BlockSpec((tm, 2 * d), lambda i: (i, GATE_OFF // (2 * d))),
            pl.BlockSpec((tm, d), row),
            pl.BlockSpec((tm, d), row),
            pl.BlockSpec((1, d), const),
            pl.BlockSpec((1, d), const),
            pl.BlockSpec((d, d), const),
            pl.BlockSpec((d, d), const),
            pl.BlockSpec((d, d), const),
        ],
        out_specs=pl.BlockSpec((tm, d), row),
        out_shape=jax.ShapeDtypeStruct((n, d), F32),
        compiler_params=_cparams(("arbitrary",)),
    )(y_f, y_b, xbc, proj, proj, attn, x2, dskip_row, g_ssm, w_br_ssm, w_br_attn, w_out)


def _router_kernel(x1_ref, g_ref, w_ref, b_ref, idx_ref, wt_ref, cnt_ref):
    h = _rms(x1_ref[...], g_ref[...])
    h_hi = h.astype(BF16)
    h_lo = (h - h_hi.astype(F32)).astype(BF16)
    logits = jnp.dot(jnp.concatenate([h_hi, h_hi, h_lo], axis=1), w_ref[...],
                     preferred_element_type=F32) + b_ref[...]
    lane = lax.broadcasted_iota(jnp.int32, logits.shape, 1)
    idx_out = jnp.zeros(logits.shape, jnp.int32)
    val_out = jnp.zeros(logits.shape, F32)
    chosen = jnp.zeros(logits.shape, F32)
    vals = []
    for k in range(TOP_K):
        m = jnp.max(logits, axis=-1, keepdims=True)
        idx = jnp.min(jnp.where(logits == m, lane, LANES), axis=-1, keepdims=True)
        idx_out = jnp.where(lane == k, idx, idx_out)
        vals.append(m)
        hit = lane == idx
        chosen = jnp.where(hit, 1.0, chosen)
        logits = jnp.where(hit, -jnp.inf, logits)
    es = [jnp.exp(v - vals[0]) for v in vals]
    tot = es[0] + es[1] + es[2] + es[3]
    for k in range(TOP_K):
        val_out = jnp.where(lane == k, es[k] / tot, val_out)
    idx_ref[...] = idx_out
    wt_ref[...] = val_out
    cnt_ref[0] = jnp.broadcast_to(jnp.sum(chosen, axis=0, keepdims=True), cnt_ref.shape[1:])


def _router(x1, g_ffn, w_router_pad, b_router_pad):
    n, d = x1.shape
    tm = min(MOE_TILE, n)
    return pl.pallas_call(
        _router_kernel,
        grid=(n // tm,),
        in_specs=[
            pl.BlockSpec((tm, d), lambda i: (i, 0)),
            pl.BlockSpec((1, d), lambda i: (0, 0)),
            pl.BlockSpec((3 * d, LANES), lambda i: (0, 0)),
            pl.BlockSpec((1, LANES), lambda i: (0, 0)),
        ],
        out_specs=[pl.BlockSpec((tm, LANES), lambda i: (i, 0)),
                   pl.BlockSpec((tm, LANES), lambda i: (i, 0)),
                   pl.BlockSpec((1, 8, LANES), lambda i: (i, 0, 0))],
        out_shape=[jax.ShapeDtypeStruct((n, LANES), jnp.int32),
                   jax.ShapeDtypeStruct((n, LANES), F32),
                   jax.ShapeDtypeStruct((n // tm, 8, LANES), F32)],
        compiler_params=_cparams(("arbitrary",)),
    )(x1, g_ffn, w_router_pad, b_router_pad)


def _routing_plan(cnt, n, tm, tb):
    nt = n // tm
    gran = MOE_GRAN
    rc = (cnt + gran - 1) // gran * gran
    covered = jnp.sum(rc, axis=0)
    region = (covered + tb - 1) // tb * tb
    pad_end = jnp.cumsum(region)
    pad_start = pad_end - region
    seg_start = pad_start[None, :] + jnp.cumsum(rc, axis=0) - rc
    stage_off = jnp.cumsum(rc, axis=1) - rc
    n_used = pad_end[-1] // tb
    nb = (TOP_K * n + nt * N_EXPERTS * (gran - 1) + N_EXPERTS * (tb - 1) + tb - 1) // tb
    blk = jnp.arange(nb, dtype=jnp.int32)
    block_e = jnp.sum((pad_end[None, :] <= (jnp.minimum(blk, n_used - 1) * tb)[:, None]).astype(jnp.int32), axis=1)
    block_e = jnp.minimum(block_e, N_EXPERTS - 1)
    flat = lambda a: a.reshape(-1).astype(jnp.int32)
    tail_gran = (region - covered) // gran
    ngran = rc // gran
    g_end = jnp.cumsum(ngran, axis=1)
    gi = jnp.arange(_stage_rows(tm) // gran, dtype=jnp.int32)
    g_exp = jnp.minimum(jnp.sum((g_end[:, None, :] <= gi[None, :, None]).astype(jnp.int32), axis=2), N_EXPERTS - 1)
    mine = g_exp[:, :, None] == jnp.arange(N_EXPERTS, dtype=jnp.int32)[None, None, :]
    seg_base = seg_start - gran * (g_end - ngran)
    gran_dst = jnp.sum(jnp.where(mine, seg_base[:, None, :], 0), axis=2) + gran * gi[None, :]
    ids = jnp.arange(N_EXPERTS, dtype=jnp.int32)
    later = jnp.where((ids[None, :] > ids[:, None]) & (region[None, :] > 0), ids[None, :], N_EXPERTS)
    nxt = jnp.min(later, axis=1)
    next_expert = jnp.where(nxt < N_EXPERTS, nxt, ids)
    return dict(gran_dst=flat(gran_dst), tot_gran=flat(g_end[:, -1]), block_e=flat(block_e),
                next_expert=flat(next_expert),
                n_used=flat(n_used), nb=nb, stage_off_f=stage_off.astype(F32),
                tail_start=flat(pad_start + covered), tail_gran=flat(tail_gran),
                tot_tail=flat(jnp.sum(tail_gran)))


def _stage_rows(tm):
    rows = TOP_K * tm + N_EXPERTS * (MOE_GRAN - 1)
    return (rows + 2 * LANES - 1) // (2 * LANES) * (2 * LANES)


GRAN_UNROLL = 4


def _granule_copies(t, gdst_ref, totg_ref, make_copy, start):
    total = totg_ref[t]
    base = t * (gdst_ref.shape[0] // totg_ref.shape[0])
    full = lax.shift_right_logical(total, GRAN_UNROLL.bit_length() - 1)

    def one(i):
        make_copy(pl.multiple_of(i * MOE_GRAN, MOE_GRAN), pl.multiple_of(gdst_ref[base + i], MOE_GRAN),
                  MOE_GRAN).start()

    def group(q, c):
        if start:
            for u in range(GRAN_UNROLL):
                one(q * GRAN_UNROLL + u)
        else:
            make_copy(0, 0, GRAN_UNROLL * MOE_GRAN).wait()
        return c

    def rest(i, c):
        if start:
            one(i)
        else:
            make_copy(0, 0, MOE_GRAN).wait()
        return c

    lax.fori_loop(0, full, group, 0)
    lax.fori_loop(full * GRAN_UNROLL, total, rest, 0)


def _dispatch_kernel(gdst_ref, totg_ref, tstart_ref, tgran_ref, misc_ref,
                     x1_ref, g_ref, idxt_ref, soffc_ref, xs_hbm, stage, zbuf, sem, zsem, *, n_blocks):
    t = pl.program_id(0)
    slot = t % 2
    tm = x1_ref.shape[0]
    rows = stage.shape[1]
    h = _rms(x1_ref[...], g_ref[...]).astype(BF16)
    idxt = idxt_ref[...]
    expert = lax.broadcasted_iota(jnp.int32, (LANES, tm), 0)
    hots = [idxt[k:k + 1, :] == expert for k in range(TOP_K)]
    multi = jnp.zeros((LANES, tm), F32)
    for hot in hots:
        multi = jnp.where(hot, 1.0, multi)
    earlier = (lax.broadcasted_iota(jnp.int32, (tm, tm), 0)
               < lax.broadcasted_iota(jnp.int32, (tm, tm), 1)).astype(BF16)
    rank = jnp.dot(multi.astype(BF16), earlier, preferred_element_type=F32)
    pos = rank + soffc_ref[0]
    srows = [jnp.sum(jnp.where(hot, pos, 0.0), axis=0, keepdims=True) for hot in hots]
    chunk = 2 * LANES
    row_id = lax.broadcasted_iota(jnp.int32, (chunk, tm), 0).astype(F32)
    for c in range(rows // chunk):
        perm = jnp.zeros((chunk, tm), F32)
        for srow in srows:
            perm = jnp.where(row_id == srow - float(c * chunk), 1.0, perm)
        stage[slot, c * chunk:(c + 1) * chunk, :] = jnp.dot(
            perm.astype(BF16), h, preferred_element_type=F32).astype(stage.dtype)

    def copy_from(s):
        def copy(stage_row, sorted_row, nrows):
            return pltpu.make_async_copy(stage.at[s, pl.ds(stage_row, nrows)],
                                         xs_hbm.at[pl.ds(sorted_row, nrows)], sem.at[s])
        return copy

    _granule_copies(t, gdst_ref, totg_ref, copy_from(slot), start=True)

    @pl.when(t > 0)
    def _():
        _granule_copies(jnp.maximum(t - 1, 0), gdst_ref, totg_ref, copy_from(1 - slot), start=False)

    @pl.when(t == pl.num_programs(0) - 1)
    def _():
        _granule_copies(t, gdst_ref, totg_ref, copy_from(slot), start=False)
        tb = zbuf.shape[0]
        n_used = misc_ref[0]
        zbuf[...] = jnp.zeros_like(zbuf)

        def zero_gran(row):
            return pltpu.make_async_copy(zbuf.at[pl.ds(0, MOE_GRAN)], xs_hbm.at[pl.ds(row, MOE_GRAN)], zsem)

        def zero_block(blk):
            return pltpu.make_async_copy(zbuf, xs_hbm.at[pl.ds(pl.multiple_of(blk * tb, tb), tb)], zsem)

        def tails(e, c):
            def one(g, c2):
                zero_gran(pl.multiple_of(tstart_ref[e] + g * MOE_GRAN, MOE_GRAN)).start()
                return c2
            lax.fori_loop(0, tgran_ref[e], one, 0)
            return c

        def start_block(blk, c):
            zero_block(blk).start()
            return c

        def wait_gran(g, c):
            zero_gran(0).wait()
            return c

        def wait_block(blk, c):
            zero_block(0).wait()
            return c

        lax.fori_loop(0, N_EXPERTS, tails, 0)
        lax.fori_loop(n_used, n_blocks, start_block, 0)
        lax.fori_loop(0, misc_ref[1], wait_gran, 0)
        lax.fori_loop(n_used, n_blocks, wait_block, 0)


def _dispatch(x1, g_ffn, idx_t, plan, tm):
    n, d = x1.shape
    nt = n // tm
    rows = _stage_rows(tm)
    soff_col = plan["stage_off_f"].reshape(nt, N_EXPERTS, 1)
    soff_col = jnp.pad(soff_col, ((0, 0), (0, LANES - N_EXPERTS), (0, 0)))
    misc = jnp.concatenate([plan["n_used"], plan["tot_tail"]])
    grid_spec = pltpu.PrefetchScalarGridSpec(
        num_scalar_prefetch=5,
        grid=(nt,),
        in_specs=[
            pl.BlockSpec((tm, d), lambda t, *_: (t, 0)),
            pl.BlockSpec((1, d), lambda t, *_: (0, 0)),
            pl.BlockSpec((8, tm), lambda t, *_: (0, t)),
            pl.BlockSpec((1, LANES, 1), lambda t, *_: (t, 0, 0)),
        ],
        out_specs=pl.BlockSpec(memory_space=pl.ANY),
        scratch_shapes=[pltpu.VMEM((2, rows, d), BF16), pltpu.VMEM((MOE_BLOCK, d), BF16),
                        pltpu.SemaphoreType.DMA((2,)), pltpu.SemaphoreType.DMA(())],
    )
    return pl.pallas_call(
        functools.partial(_dispatch_kernel, n_blocks=plan["nb"]),
        grid_spec=grid_spec,
        out_shape=jax.ShapeDtypeStruct((plan["nb"] * MOE_BLOCK, d), BF16),
        compiler_params=_cparams(("arbitrary",)),
    )(plan["gran_dst"], plan["tot_gran"], plan["tail_start"], plan["tail_gran"], misc, x1, g_ffn, idx_t,
      soff_col)


def _expert_kernel(be_ref, nused_ref, nexte_ref, x_ref, w1_hbm, b1_ref, w2_hbm, b2_ref, o_ref,
                   w1f, w2f, w1b, w2b, sem):
    de = w2f.shape[0]
    j = pl.program_id(0)
    live = j < nused_ref[0]
    e = be_ref[j]
    new_expert = jnp.logical_or(j == 0, e != be_ref[jnp.maximum(j - 1, 0)])

    def fetch(ex):
        return (pltpu.make_async_copy(w1_hbm.at[ex], w1f, sem.at[0]),
                pltpu.make_async_copy(w2_hbm.at[ex], w2f, sem.at[1]))

    @pl.when(jnp.logical_and(live, j == 0))
    def _():
        for c in fetch(e):
            c.start()

    @pl.when(jnp.logical_and(live, new_expert))
    def _():
        for c in fetch(e):
            c.wait()
        w1b[...] = w1f[...].astype(BF16)
        w2b[...] = w2f[...].astype(BF16)
        nxt = nexte_ref[e]

        @pl.when(nxt != e)
        def _():
            for c in fetch(nxt):
                c.start()

    @pl.when(live)
    def _():
        hm = jnp.dot(x_ref[...], w1b[...], preferred_element_type=F32) + b1_ref[0]
        gate = jnp.minimum(hm[:, :de], SWIGLU_LIMIT)
        up = jnp.clip(hm[:, de:], -SWIGLU_LIMIT, SWIGLU_LIMIT)
        act = gate * _sigmoid(SWIGLU_ALPHA * gate) * (up + 1.0)
        y = jnp.dot(act.astype(BF16), w2b[...], preferred_element_type=F32) + b2_ref[0]
        o_ref[...] = y.astype(o_ref.dtype)

    @pl.when(jnp.logical_not(live))
    def _():
        o_ref[...] = jnp.zeros_like(o_ref)


def _experts(xs, plan, w1, b1, w2, b2):
    d = xs.shape[1]
    tb = MOE_BLOCK
    f2 = w1.shape[2]
    de = w2.shape[1]
    last = lambda j, nu: jnp.maximum(jnp.minimum(j, nu[0] - 1), 0)
    grid_spec = pltpu.PrefetchScalarGridSpec(
        num_scalar_prefetch=3,
        grid=(plan["nb"],),
        in_specs=[
            pl.BlockSpec((tb, d), lambda j, be, nu, ne: (last(j, nu), 0)),
            pl.BlockSpec(memory_space=pl.ANY),
            pl.BlockSpec((1, 1, f2), lambda j, be, nu, ne: (be[j], 0, 0)),
            pl.BlockSpec(memory_space=pl.ANY),
            pl.BlockSpec((1, 1, d), lambda j, be, nu, ne: (be[j], 0, 0)),
        ],
        out_specs=pl.BlockSpec((tb, d), lambda j, be, nu, ne: (j, 0)),
        scratch_shapes=[pltpu.VMEM((d, f2), F32), pltpu.VMEM((de, d), F32),
                        pltpu.VMEM((d, f2), BF16), pltpu.VMEM((de, d), BF16),
                        pltpu.SemaphoreType.DMA((2,))],
    )
    return pl.pallas_call(
        _expert_kernel,
        grid_spec=grid_spec,
        out_shape=jax.ShapeDtypeStruct(xs.shape, BF16),
        compiler_params=_cparams(("arbitrary",)),
    )(plan["block_e"], plan["n_used"], plan["next_expert"], xs, w1, b1, w2, b2)


def _combine_kernel(gdst_ref, totg_ref, x1_ref, idx_ref, wt_ref, soffr_ref, g_ref, yb_hbm, o_ref, stage, sem):
    t = pl.program_id(0)
    slot = t % 2
    tm = x1_ref.shape[0]
    rows = stage.shape[1]

    def copy_into(s):
        def copy(stage_row, sorted_row, nrows):
            return pltpu.make_async_copy(yb_hbm.at[pl.ds(sorted_row, nrows)],
                                         stage.at[s, pl.ds(stage_row, nrows)], sem.at[s])
        return copy

    @pl.when(t == 0)
    def _():
        stage[...] = jnp.zeros_like(stage)
        _granule_copies(t, gdst_ref, totg_ref, copy_into(slot), start=True)

    @pl.when(t + 1 < pl.num_programs(0))
    def _():
        _granule_copies(t + 1, gdst_ref, totg_ref, copy_into(1 - slot), start=True)

    idx = idx_ref[...]
    wt = wt_ref[...]
    expert = lax.broadcasted_iota(jnp.int32, (tm, LANES), 1)
    hots = [idx[:, k:k + 1] == expert for k in range(TOP_K)]
    multi = jnp.zeros((tm, LANES), F32)
    for hot in hots:
        multi = jnp.where(hot, 1.0, multi)
    earlier = (lax.broadcasted_iota(jnp.int32, (tm, tm), 0)
               > lax.broadcasted_iota(jnp.int32, (tm, tm), 1)).astype(BF16)
    rank = jnp.dot(earlier, multi.astype(BF16), preferred_element_type=F32)
    pos = rank + soffr_ref[0]
    srows = [jnp.sum(jnp.where(hot, pos, 0.0), axis=-1, keepdims=True) for hot in hots]
    chunk = LANES
    row_id = lax.broadcasted_iota(jnp.int32, (chunk, rows), 1).astype(F32)
    for c in range(tm // chunk):
        sl = slice(c * chunk, (c + 1) * chunk)
        unsort = jnp.zeros((chunk, rows), F32)
        for k, srow in enumerate(srows):
            unsort = jnp.where(row_id == srow[sl], wt[sl, k:k + 1], unsort)
        if c == 0:
            _granule_copies(t, gdst_ref, totg_ref, copy_into(slot), start=False)
        y = jnp.dot(unsort.astype(BF16), stage[slot], preferred_element_type=F32)
        o_ref[sl, :] = _rms(x1_ref[sl, :] + y, g_ref[...])


def _combine(x1, yb, top_idx, top_w, plan, g_final, tm):
    n, d = x1.shape
    nt = n // tm
    rows = _stage_rows(tm)
    soff_row = jnp.pad(plan["stage_off_f"], ((0, 0), (0, LANES - N_EXPERTS))).reshape(nt, 1, LANES)
    grid_spec = pltpu.PrefetchScalarGridSpec(
        num_scalar_prefetch=2,
        grid=(nt,),
        in_specs=[
            pl.BlockSpec((tm, d), lambda t, *_: (t, 0)),
            pl.BlockSpec((tm, LANES), lambda t, *_: (t, 0)),
            pl.BlockSpec((tm, LANES), lambda t, *_: (t, 0)),
            pl.BlockSpec((1, 1, LANES), lambda t, *_: (t, 0, 0)),
            pl.BlockSpec((1, d), lambda t, *_: (0, 0)),
            pl.BlockSpec(memory_space=pl.ANY),
        ],
        out_specs=pl.BlockSpec((tm, d), lambda t, *_: (t, 0)),
        scratch_shapes=[pltpu.VMEM((2, rows, d), BF16), pltpu.SemaphoreType.DMA((2,))],
    )
    return pl.pallas_call(
        _combine_kernel,
        grid_spec=grid_spec,
        out_shape=jax.ShapeDtypeStruct((n, d), F32),
        compiler_params=_cparams(("arbitrary",)),
    )(plan["gran_dst"], plan["tot_gran"], x1, top_idx, top_w, soff_row, g_final, yb)


def _rope_tables(seq):
    half = ATTN_HEAD_DIM // 2
    inv_freq = np.float32(ROPE_THETA) ** (-np.arange(0, half, 2, dtype=np.float32) / np.float32(half))
    ang_row = np.arange(seq // GRID_W, dtype=np.float32)[:, None] * inv_freq
    ang_col = np.arange(GRID_W, dtype=np.float32)[:, None] * inv_freq
    by_row = lambda t: jnp.repeat(jnp.asarray(t, F32), GRID_W, axis=0)
    by_col = lambda t: jnp.tile(jnp.asarray(t, F32), (seq // GRID_W, 1))
    cos_r, sin_r = by_row(np.cos(ang_row)), by_row(np.sin(ang_row))
    cos_c, sin_c = by_col(np.cos(ang_col)), by_col(np.sin(ang_col))
    cos_t = jnp.concatenate([cos_r, cos_c] * 2, axis=-1)
    sin_t = jnp.concatenate([-sin_r, -sin_c, sin_r, sin_c], axis=-1)
    return cos_t, sin_t


def _token_mixer(x2, batch, seq, g_mix, w_in, conv_w, conv_b, dt_bias_f, dt_bias_b, a_log_f, a_log_b, d_skip,
                 g_ssm, q_norm_g, k_norm_g, w_br_ssm, w_br_attn, w_out):
    n, d = x2.shape
    z_end = SSM_INNER
    xbc_end = z_end + CONV_CH
    dtf_end = xbc_end + SSM_HEADS
    dtb_end = dtf_end + SSM_HEADS
    q_end = dtb_end + ATTN_HEADS * ATTN_HEAD_DIM
    k_end = q_end + ATTN_KV_HEADS * ATTN_HEAD_DIM
    v_end = k_end + ATTN_KV_HEADS * ATTN_HEAD_DIM
    head_cols = lambda w: _rope_head_order(w.reshape(d, -1, ATTN_HEAD_DIM)).reshape(d, -1)
    w_main = jnp.concatenate([w_in[:, :z_end], head_cols(w_in[:, dtb_end:q_end]), w_in[:, v_end:],
                              w_in[:, z_end:xbc_end], head_cols(w_in[:, q_end:k_end]), w_in[:, k_end:v_end]],
                             axis=1).astype(BF16)
    w_dt = jnp.pad(w_in[:, xbc_end:dtb_end], ((0, 0), (0, LANES - 2 * SSM_HEADS))).astype(BF16)

    proj, dt = _in_proj(x2, g_mix.reshape(1, d), w_main, w_dt)
    proj3 = proj.reshape(batch, seq, PROJ_COLS)

    xbc = _conv(proj3, conv_w, conv_b.reshape(1, CONV_CH))

    dt3 = dt.reshape(batch, seq, LANES)
    dtt3 = jnp.swapaxes(dt3[:, :, :2 * SSM_HEADS], 1, 2)
    bias = jnp.concatenate([dt_bias_f, dt_bias_b])
    alog = jnp.concatenate([a_log_f, a_log_b])
    pad_row = lambda v: jnp.pad(v, (0, LANES - 2 * SSM_HEADS)).reshape(1, LANES)

    cos_t, sin_t = _rope_tables(seq)
    q_rot, k_rot = _qk_prep(proj, cos_t, sin_t, _rope_head_order(q_norm_g).reshape(1, -1),
                            _rope_head_order(k_norm_g).reshape(1, -1), seq)
    vt3 = jnp.swapaxes(proj3[:, :, V_OFF:V_OFF + ATTN_KV_HEADS * ATTN_HEAD_DIM], 1, 2)
    attn, y_f, y_b = _attn_ssd(q_rot.reshape(batch, seq, -1), k_rot.reshape(batch, seq, -1), vt3, q_norm_g,
                               k_norm_g, xbc, dt3, dtt3, pad_row(bias), bias.reshape(-1, 1), pad_row(alog),
                               alog.reshape(-1, 1))

    return _merge(y_f.reshape(n, -1), y_b.reshape(n, -1), xbc.reshape(n, CONV_CH), proj, attn.reshape(n, -1),
                  x2, jnp.repeat(d_skip, SSM_HEAD_DIM).reshape(1, -1), g_ssm.reshape(1, -1),
                  w_br_ssm.astype(BF16), w_br_attn.astype(BF16), w_out.astype(BF16))


def _moe_and_final_norm(x1, g_ffn, w_router, b_router, w_mlp1, b_mlp1, w_mlp2, b_mlp2, g_final):
    n, d = x1.shape
    tm = min(MOE_TILE, n)
    w_r = jnp.pad(w_router, ((0, 0), (0, LANES - N_EXPERTS)))
    w_hi = w_r.astype(BF16)
    w_lo = (w_r - w_hi.astype(F32)).astype(BF16)
    w_r = jnp.concatenate([w_hi, w_lo, w_hi], axis=0)
    b_r = jnp.pad(b_router, (0, LANES - N_EXPERTS), constant_values=-jnp.inf).reshape(1, LANES)
    g_ffn_row = g_ffn.reshape(1, d)
    top_idx, top_w, cnt = _router(x1, g_ffn_row, w_r, b_r)
    plan = _routing_plan(cnt[:, 0, :N_EXPERTS].astype(jnp.int32), n, tm, MOE_BLOCK)
    idx_t = jnp.transpose(top_idx[:, :8])
    xs = _dispatch(x1, g_ffn_row, idx_t, plan, tm)
    yb = _experts(xs, plan, w_mlp1, b_mlp1[:, None, :], w_mlp2, b_mlp2[:, None, :])
    return _combine(x1, yb, top_idx, top_w, plan, g_final.reshape(1, d), tm)


def kernel(x, g_mix, w_in, conv_w, conv_b, dt_bias_f, dt_bias_b, a_log_f, a_log_b, d_skip, g_ssm, q_norm_g,
           k_norm_g, w_br_ssm, w_br_attn, w_out, g_ffn, w_router, b_router, w_mlp1, b_mlp1, w_mlp2, b_mlp2,
           g_final):
    batch, seq, d = x.shape
    assert g_mix.shape[0] == 1, "single-layer model: the final rmsnorm is fused into the MoE combine"
    x2 = x.reshape(batch * seq, d)
    x1 = _token_mixer(x2, batch, seq, g_mix[0], w_in[0], conv_w[0], conv_b[0], dt_bias_f[0], dt_bias_b[0],
                      a_log_f[0], a_log_b[0], d_skip[0], g_ssm[0], q_norm_g[0], k_norm_g[0], w_br_ssm[0],
                      w_br_attn[0], w_out[0])
    out = _moe_and_final_norm(x1, g_ffn[0], w_router[0], b_router[0], w_mlp1[0], b_mlp1[0], w_mlp2[0],
                              b_mlp2[0], g_final)
    return out.reshape(batch, seq, d)
```

```python
import functools
import math

import jax
import jax.numpy as jnp
import numpy as np
from jax import lax
from jax.experimental import pallas as pl
from jax.experimental.pallas import tpu as pltpu

F32 = jnp.float32
BF16 = jnp.bfloat16

NORM_EPS = 1e-6
GRID_W = 64
SSM_HEADS = 16
SSM_HEAD_DIM = 64
SSM_INNER = SSM_HEADS * SSM_HEAD_DIM
SSM_GROUPS = 2
SSM_STATE = 128
SSM_CONV = 5
CONV_CH = SSM_INNER + 2 * SSM_GROUPS * SSM_STATE
ATTN_HEADS = 8
ATTN_KV_HEADS = 2
ATTN_HEAD_DIM = 128
ROPE_THETA = 10000.0
N_EXPERTS = 32
TOP_K = 4
SWIGLU_LIMIT = 7.0
SWIGLU_ALPHA = 1.702

LANES = 128
BF16_SUBLANES = 16
VMEM_LIMIT = 56 * 1024 * 1024

Z_OFF, Q_OFF, GATE_OFF, XBC_OFF = 0, 1024, 2048, 4096
K_OFF, V_OFF, PROJ_COLS = 5632, 5888, 6144

MOE_TILE = 512
MOE_BLOCK = 512
MOE_GRAN = BF16_SUBLANES


def _cparams(sem):
    return pltpu.CompilerParams(dimension_semantics=sem, vmem_limit_bytes=VMEM_LIMIT)


def _sigmoid(x):
    return 1.0 / (1.0 + jnp.exp(-x))


def _softplus(x):
    return jnp.maximum(x, 0.0) + jnp.log(1.0 + jnp.exp(-jnp.abs(x)))


def _rms(x, g):
    ms = jnp.mean(x * x, axis=-1, keepdims=True)
    return x * lax.rsqrt(ms + NORM_EPS) * g


def _inproj_kernel(x_ref, g_ref, w_ref, wdt_ref, o_ref, dt_ref, h_scr):
    @pl.when(pl.program_id(1) == 0)
    def _():
        hb = _rms(x_ref[...], g_ref[...]).astype(BF16)
        h_scr[...] = hb
        dt_ref[...] = jnp.dot(hb, wdt_ref[...], preferred_element_type=F32)

    o_ref[...] = jnp.dot(h_scr[...], w_ref[...], preferred_element_type=F32).astype(o_ref.dtype)


def _in_proj(x2, g_mix, w_main, w_dt):
    n, d = x2.shape
    tm = min(1024, n)
    tn = 2048
    return pl.pallas_call(
        _inproj_kernel,
        grid=(n // tm, PROJ_COLS // tn),
        in_specs=[
            pl.BlockSpec((tm, d), lambda i, j: (i, 0)),
            pl.BlockSpec((1, d), lambda i, j: (0, 0)),
            pl.BlockSpec((d, tn), lambda i, j: (0, j)),
            pl.BlockSpec((d, LANES), lambda i, j: (0, 0)),
        ],
        out_specs=[
            pl.BlockSpec((tm, tn), lambda i, j: (i, j)),
            pl.BlockSpec((tm, LANES), lambda i, j: (i, 0)),
        ],
        out_shape=[
            jax.ShapeDtypeStruct((n, PROJ_COLS), BF16),
            jax.ShapeDtypeStruct((n, LANES), F32),
        ],
        scratch_shapes=[pltpu.VMEM((tm, d), BF16)],
        compiler_params=_cparams(("arbitrary", "arbitrary")),
    )(x2, g_mix, w_main, w_dt)


CONV_HALO = 64
CONV_ROWS = 128


def _conv_kernel(prev_ref, cur_ref, next_ref, shift_ref, w_ref, b_ref, o_ref):
    s = pl.program_id(1)
    ts = cur_ref.shape[1]
    prev = prev_ref[0]
    nxt = next_ref[0]
    zero = jnp.zeros_like(prev)
    ext = jnp.concatenate([jnp.where(s == 0, zero, prev), cur_ref[0],
                           jnp.where(s == pl.num_programs(1) - 1, zero, nxt)], axis=0)
    shift = shift_ref[...]
    for rb in range(ts // CONV_ROWS):
        lo = rb * CONV_ROWS
        taps = jnp.dot(shift, ext[lo:lo + CONV_ROWS + 2 * CONV_HALO], preferred_element_type=F32)
        acc = b_ref[...] + w_ref[0:1, :] * taps[0:CONV_ROWS]
        for k in range(1, SSM_CONV):
            acc = acc + w_ref[k:k + 1, :] * taps[k * CONV_ROWS:(k + 1) * CONV_ROWS]
        o_ref[0, lo:lo + CONV_ROWS, :] = (acc * _sigmoid(acc)).astype(o_ref.dtype)


def _conv(proj3, conv_w, conv_b):
    b, s, _ = proj3.shape
    ts = min(1024, s)
    tc = 512
    halo = CONV_HALO
    hb = ts // halo
    col0 = XBC_OFF // tc
    pad = (SSM_CONV - 1) // 2
    out_row = np.arange(SSM_CONV * CONV_ROWS)
    src = out_row % CONV_ROWS + halo + out_row // CONV_ROWS - pad
    shift = jnp.asarray(src[:, None] == np.arange(CONV_ROWS + 2 * halo)[None, :], BF16)
    return pl.pallas_call(
        _conv_kernel,
        grid=(b, s // ts, CONV_CH // tc),
        in_specs=[
            pl.BlockSpec((1, halo, tc), lambda bi, si, ci: (bi, jnp.maximum(si * hb - 1, 0), col0 + ci)),
            pl.BlockSpec((1, ts, tc), lambda bi, si, ci: (bi, si, col0 + ci)),
            pl.BlockSpec((1, halo, tc),
                         lambda bi, si, ci: (bi, jnp.minimum((si + 1) * hb, s // halo - 1), col0 + ci)),
            pl.BlockSpec(shift.shape, lambda bi, si, ci: (0, 0)),
            pl.BlockSpec((SSM_CONV, tc), lambda bi, si, ci: (0, ci)),
            pl.BlockSpec((1, tc), lambda bi, si, ci: (0, ci)),
        ],
        out_specs=pl.BlockSpec((1, ts, tc), lambda bi, si, ci: (bi, si, ci)),
        out_shape=jax.ShapeDtypeStruct((b, s, CONV_CH), BF16),
        compiler_params=_cparams(("arbitrary", "arbitrary", "arbitrary")),
    )(proj3, proj3, proj3, shift, conv_w, conv_b)


def _ssd_kernel(xf_ref, xb_ref, dtf_ref, dtb_ref, dttf_ref, dttb_ref, brow_ref, bcol_ref,
                arow_ref, acol_ref, yf_ref, yb_ref, st_ref):
    L = xf_ref.shape[1]
    hg = SSM_HEADS // SSM_GROUPS
    pairs = hg // 2

    @pl.when(pl.program_id(1) == 0)
    def _():
        st_ref[...] = jnp.zeros_like(st_ref)

    rows = lax.broadcasted_iota(jnp.int32, (L, L), 0)
    cols = lax.broadcasted_iota(jnp.int32, (L, L), 1)
    lower = rows >= cols
    upper = rows <= cols
    ltri = lower.astype(BF16)
    utri = upper.astype(BF16)
    lane = lax.broadcasted_iota(jnp.int32, (L, LANES), 1)
    first_half = lane < SSM_HEAD_DIM
    lane1 = lax.broadcasted_iota(jnp.int32, (1, LANES), 1)
    log2e = math.log2(math.e)
    a_row = -jnp.exp(arow_ref[...]) * log2e
    a_col = -jnp.exp(acol_ref[...]) * log2e

    def split3(v):
        hi = v.astype(BF16)
        r1 = v - hi.astype(F32)
        mid = r1.astype(BF16)
        return hi, mid, (r1 - mid.astype(F32)).astype(BF16)

    def cumsum_cols(tri, v):
        return jnp.dot(jnp.concatenate([tri] * 3, axis=1), jnp.concatenate(split3(v), axis=0),
                       preferred_element_type=F32)

    def cumsum_rows(v, tri):
        return jnp.dot(jnp.concatenate(split3(v), axis=1), jnp.concatenate([tri] * 3, axis=0),
                       preferred_element_type=F32)

    for d in range(2):
        x_ref, dt_ref, dtt_ref, y_ref = ((xf_ref, dtf_ref, dttf_ref, yf_ref) if d == 0
                                         else (xb_ref, dtb_ref, dttb_ref, yb_ref))
        a = _softplus(dt_ref[0] + brow_ref[...]) * a_row
        dt_t = _softplus(dtt_ref[0] + bcol_ref[...])
        a_t = dt_t * a_col
        if d == 0:
            cs_col = cumsum_cols(ltri, a)
            cs_row = cumsum_rows(a_t, utri)
            tot = cs_col[L - 1:L, :]
            tot_t = cs_row[:, L - 1:L]
            mask = lower
        else:
            cs_col = cumsum_cols(utri, a)
            cs_row = cumsum_rows(a_t, ltri)
            tot = cs_col[0:1, :]
            tot_t = cs_row[:, 0:1]
            mask = upper
        w_t = dt_t * jnp.exp2(tot_t - cs_row)
        src_t = cs_row - jnp.log2(dt_t)
        chunk_decay = jnp.exp2(tot)

        for g in range(SSM_GROUPS):
            boff = SSM_INNER + g * SSM_STATE
            coff = SSM_INNER + SSM_GROUPS * SSM_STATE + g * SSM_STATE
            bm = x_ref[0, :, boff:boff + SSM_STATE]
            cm = x_ref[0, :, coff:coff + SSM_STATE]
            cb = lax.dot_general(cm, bm, (((1,), (1,)), ((), ())), preferred_element_type=F32)
            bt = bm.astype(F32).T
            st = st_ref[d, g]
            y_off = jnp.dot(cm, st.astype(BF16), preferred_element_type=F32)
            for pr in range(pairs):
                h0 = d * SSM_HEADS + g * hg + 2 * pr
                xoff = (g * pairs + pr) * LANES
                xs = x_ref[0, :, xoff:xoff + LANES]
                zero = jnp.zeros_like(xs)
                rhs = jnp.concatenate([jnp.where(first_half, xs, zero),
                                       jnp.where(first_half, zero, xs)], axis=0)
                ms, ws, dins = [], [], []
                for hh in (h0, h0 + 1):
                    cs_b = jnp.broadcast_to(cs_col[:, hh:hh + 1], (L, L))
                    seg = cs_b - src_t[hh:hh + 1, :]
                    m = cb * jnp.exp2(jnp.where(mask, seg, -jnp.inf))
                    ms.append(m.astype(BF16))
                    ws.append((bt * w_t[hh:hh + 1, :]).astype(BF16))
                    dins.append(jnp.exp2(cs_b))
                y = jnp.dot(jnp.concatenate(ms, axis=1), rhs, preferred_element_type=F32)
                y = y + y_off[:, pr * LANES:(pr + 1) * LANES] * jnp.where(first_half, dins[0], dins[1])
                y_ref[0, :, xoff:xoff + LANES] = y.astype(y_ref.dtype)
                new_st = jnp.dot(jnp.concatenate(ws, axis=1), rhs, preferred_element_type=F32)
                cd = jnp.where(lane1 < SSM_HEAD_DIM, chunk_decay[:, h0:h0 + 1], chunk_decay[:, h0 + 1:h0 + 2])
                st_ref[d, g, :, pr * LANES:(pr + 1) * LANES] = st[:, pr * LANES:(pr + 1) * LANES] * cd + new_st


def _ssd(xbc, dt3, dtt3, bias_row, bias_col, alog_row, alog_col):
    b, s, _ = xbc.shape
    L = min(128, s)
    nc = s // L
    hg = SSM_HEADS // SSM_GROUPS
    fwd = lambda bi, ci: (bi, ci, 0)
    bwd = lambda bi, ci: (bi, nc - 1 - ci, 0)
    fwd_t = lambda bi, ci: (bi, 0, ci)
    bwd_t = lambda bi, ci: (bi, 0, nc - 1 - ci)
    const = lambda bi, ci: (0, 0)
    return pl.pallas_call(
        _ssd_kernel,
        grid=(b, nc),
        in_specs=[
            pl.BlockSpec((1, L, CONV_CH), fwd),
            pl.BlockSpec((1, L, CONV_CH), bwd),
            pl.BlockSpec((1, L, LANES), fwd),
            pl.BlockSpec((1, L, LANES), bwd),
            pl.BlockSpec((1, 2 * SSM_HEADS, L), fwd_t),
            pl.BlockSpec((1, 2 * SSM_HEADS, L), bwd_t),
            pl.BlockSpec((1, LANES), const),
            pl.BlockSpec((2 * SSM_HEADS, 1), const),
            pl.BlockSpec((1, LANES), const),
            pl.BlockSpec((2 * SSM_HEADS, 1), const),
        ],
        out_specs=[
            pl.BlockSpec((1, L, SSM_INNER), fwd),
            pl.BlockSpec((1, L, SSM_INNER), bwd),
        ],
        out_shape=[jax.ShapeDtypeStruct((b, s, SSM_INNER), BF16)] * 2,
        scratch_shapes=[pltpu.VMEM((2, SSM_GROUPS, SSM_STATE, hg * SSM_HEAD_DIM), F32)],
        compiler_params=_cparams(("arbitrary", "arbitrary")),
    )(xbc, xbc, dt3, dt3, dtt3, dtt3, bias_row, bias_col, alog_row, alog_col)


def _rope_head_order(a):
    q4 = ATTN_HEAD_DIM // 4
    return jnp.concatenate([a[..., 0:q4], a[..., 2 * q4:3 * q4], a[..., q4:2 * q4], a[..., 3 * q4:]], axis=-1)


def _rope_norm(t, g, cos, sin_signed, ones):
    sq = t * t
    hi = sq.astype(BF16)
    lo = (sq - hi.astype(F32)).astype(BF16)
    ms = jnp.dot(jnp.concatenate([hi, lo], axis=1), ones, preferred_element_type=F32) * (1.0 / ATTN_HEAD_DIM)
    tn = t * lax.rsqrt(ms + NORM_EPS) * g
    return tn * cos + pltpu.roll(tn, ATTN_HEAD_DIM // 2, 1) * sin_signed


Q_SCALE = ATTN_HEAD_DIM ** -0.5 * math.log2(math.e)


def _qkprep_kernel(q_ref, k_ref, cos_ref, sin_ref, qg_ref, kg_ref, qo_ref, ko_ref):
    cos = cos_ref[...]
    sin = sin_ref[...]
    ones = jnp.ones((2 * ATTN_HEAD_DIM, ATTN_HEAD_DIM), BF16)

    def heads(src_ref, g_ref, dst_ref, n_heads, scale):
        for h in range(n_heads):
            sl = slice(h * ATTN_HEAD_DIM, (h + 1) * ATTN_HEAD_DIM)
            r = _rope_norm(src_ref[:, sl].astype(F32), g_ref[...], cos, sin, ones) * scale
            dst_ref[:, sl] = r.astype(dst_ref.dtype)

    heads(q_ref, qg_ref, qo_ref, ATTN_HEADS, Q_SCALE)
    heads(k_ref, kg_ref, ko_ref, ATTN_KV_HEADS, 1.0)


def _qk_prep(proj, cos_t, sin_t, q_norm_g, k_norm_g, seq):
    n = proj.shape[0]
    tm = min(512, seq)
    qw = ATTN_HEADS * ATTN_HEAD_DIM
    kw = ATTN_KV_HEADS * ATTN_HEAD_DIM
    spt = seq // tm
    return pl.pallas_call(
        _qkprep_kernel,
        grid=(n // tm,),
        in_specs=[
            pl.BlockSpec((tm, qw), lambda i: (i, Q_OFF // qw)),
            pl.BlockSpec((tm, kw), lambda i: (i, K_OFF // kw)),
            pl.BlockSpec((tm, ATTN_HEAD_DIM), lambda i: (i % spt, 0)),
            pl.BlockSpec((tm, ATTN_HEAD_DIM), lambda i: (i % spt, 0)),
            pl.BlockSpec((1, ATTN_HEAD_DIM), lambda i: (0, 0)),
            pl.BlockSpec((1, ATTN_HEAD_DIM), lambda i: (0, 0)),
        ],
        out_specs=[
            pl.BlockSpec((tm, qw), lambda i: (i, 0)),
            pl.BlockSpec((tm, kw), lambda i: (i, 0)),
        ],
        out_shape=[jax.ShapeDtypeStruct((n, qw), BF16), jax.ShapeDtypeStruct((n, kw), BF16)],
        compiler_params=_cparams(("arbitrary",)),
    )(proj, proj, cos_t, sin_t, q_norm_g, k_norm_g)


def _flash_kernel(small_ref, q_ref, k_ref, vt_ref, o_ref, *, tk, th):
    q = q_ref[0]
    tq = q.shape[0]
    seq = k_ref.shape[1]
    nt = (((1,), (1,)), ((), ()))
    small = small_ref[0] != 0

    @pl.when(small)
    def _():
        l = acc_t = None
        for c in range(seq // th):
            ks = slice(c * th, (c + 1) * th)
            p_t = jnp.exp2(lax.dot_general(k_ref[0, ks, :], q, nt, preferred_element_type=F32))
            l_c = jnp.sum(p_t, axis=0, keepdims=True)
            a_c = jnp.dot(vt_ref[0, :, ks], p_t.astype(BF16), preferred_element_type=F32)
            l = l_c if l is None else l + l_c
            acc_t = a_c if acc_t is None else acc_t + a_c
        o_ref[0] = (acc_t / l).T.astype(o_ref.dtype)

    @pl.when(jnp.logical_not(small))
    def _():
        def body(i, carry):
            m, l, acc_t = carry
            off = pl.multiple_of(i * tk, tk)
            s_t = lax.dot_general(k_ref[0, pl.ds(off, tk), :], q, nt, preferred_element_type=F32)
            m_new = jnp.maximum(m, jnp.max(s_t, axis=0, keepdims=True))
            alpha = jnp.exp2(m - m_new)
            p_t = jnp.exp2(s_t - m_new)
            l = alpha * l + jnp.sum(p_t, axis=0, keepdims=True)
            acc_t = alpha * acc_t + jnp.dot(vt_ref[0, :, pl.ds(off, tk)], p_t.astype(BF16),
                                            preferred_element_type=F32)
            return m_new, l, acc_t

        init = (jnp.full((1, tq), -jnp.inf, F32), jnp.zeros((1, tq), F32),
                jnp.zeros((ATTN_HEAD_DIM, tq), F32))
        _, l, acc_t = lax.fori_loop(0, seq // tk, body, init)
        o_ref[0] = (acc_t / l).T.astype(o_ref.dtype)


SCORE_BOUND = 59.0


def _flash(q3, k3, vt3, q_norm_g, k_norm_g):
    b, s, _ = q3.shape
    tq = min(1024, s)
    tk = min(2048, s)
    th = min(4096, s)
    nq = s // tq
    grp = ATTN_HEADS // ATTN_KV_HEADS
    hd = ATTN_HEAD_DIM
    bound = hd * Q_SCALE * jnp.max(jnp.abs(q_norm_g)) * jnp.max(jnp.abs(k_norm_g)) * 1.02
    small = (bound <= SCORE_BOUND).astype(jnp.int32).reshape(1)
    grid_spec = pltpu.PrefetchScalarGridSpec(
        num_scalar_prefetch=1,
        grid=(b, ATTN_HEADS, nq),
        in_specs=[
            pl.BlockSpec((1, tq, hd), lambda bi, h, qi, sm: (bi, qi, h)),
            pl.BlockSpec((1, s, hd), lambda bi, h, qi, sm: (bi, 0, h // grp)),
            pl.BlockSpec((1, hd, s), lambda bi, h, qi, sm: (bi, h // grp, 0)),
        ],
        out_specs=pl.BlockSpec((1, tq, hd), lambda bi, h, qi, sm: (bi, qi, h)),
    )
    return pl.pallas_call(
        functools.partial(_flash_kernel, tk=tk, th=th),
        grid_spec=grid_spec,
        out_shape=jax.ShapeDtypeStruct((b, s, ATTN_HEADS * hd), BF16),
        compiler_params=_cparams(("arbitrary", "arbitrary", "arbitrary")),
    )(small, q3, k3, vt3)


def _merge_kernel(yf_ref, yb_ref, xs_ref, z_ref, gate_ref, attn_ref, x_ref, dskip_ref, gssm_ref,
                  wbs_ref, wba_ref, wo_ref, x1_ref):
    d = x_ref.shape[1]
    xs = xs_ref[...].astype(F32)
    y = yf_ref[...].astype(F32) + yb_ref[...].astype(F32) + xs * dskip_ref[...]
    z = z_ref[...].astype(F32)
    y = _rms(y * (z * _sigmoid(z)), gssm_ref[...])
    br_ssm = jnp.dot(y.astype(BF16), wbs_ref[...], preferred_element_type=F32)
    br_attn = jnp.dot(attn_ref[...], wba_ref[...], preferred_element_type=F32)
    g_s = _sigmoid(gate_ref[:, :d].astype(F32))
    g_a = _sigmoid(gate_ref[:, d:].astype(F32))
    merged = (g_s * br_ssm + g_a * br_attn).astype(BF16)
    x1_ref[...] = x_ref[...] + jnp.dot(merged, wo_ref[...], preferred_element_type=F32)


def _merge(y_f, y_b, xbc, proj, attn, x2, dskip_row, g_ssm, w_br_ssm, w_br_attn, w_out):
    n, d = x2.shape
    tm = min(512, n)
    row = lambda i: (i, 0)
    const = lambda i: (0, 0)
    return pl.pallas_call(
        _merge_kernel,
        grid=(n // tm,),
        in_specs=[
            pl.BlockSpec((tm, d), row),
            pl.BlockSpec((tm, d), row),
            pl.BlockSpec((tm, d), row),
            pl.BlockSpec((tm, d), lambda i: (i, Z_OFF // d)),
            pl.BlockSpec((tm, 2 * d), lambda i: (i, GATE_OFF // (2 * d))),
            pl.BlockSpec((tm, d), row),
            pl.BlockSpec((tm, d), row),
            pl.BlockSpec((1, d), const),
            pl.BlockSpec((1, d), const),
            pl.BlockSpec((d, d), const),
            pl.BlockSpec((d, d), const),
            pl.BlockSpec((d, d), const),
        ],
        out_specs=pl.BlockSpec((tm, d), row),
        out_shape=jax.ShapeDtypeStruct((n, d), F32),
        compiler_params=_cparams(("arbitrary",)),
    )(y_f, y_b, xbc, proj, proj, attn, x2, dskip_row, g_ssm, w_br_ssm, w_br_attn, w_out)


def _router_kernel(x1_ref, g_ref, w_ref, b_ref, idx_ref, wt_ref, cnt_ref):
    h = _rms(x1_ref[...], g_ref[...])
    h_hi = h.astype(BF16)
    h_lo = (h - h_hi.astype(F32)).astype(BF16)
    logits = lax.dot_general(w_ref[...], jnp.concatenate([h_hi, h_hi, h_lo], axis=1), (((1,), (1,)), ((), ())),
                             preferred_element_type=F32) + b_ref[...]
    expert = lax.broadcasted_iota(jnp.int32, logits.shape, 0)
    slot = lax.broadcasted_iota(jnp.int32, idx_ref.shape, 0)
    idx_out = jnp.zeros(idx_ref.shape, jnp.int32)
    val_out = jnp.zeros(wt_ref.shape, F32)
    chosen = jnp.zeros(logits.shape, F32)
    vals = []
    for k in range(TOP_K):
        m = jnp.max(logits, axis=0, keepdims=True)
        idx = jnp.min(jnp.where(logits == m, expert, N_EXPERTS), axis=0, keepdims=True)
        idx_out = jnp.where(slot == k, idx, idx_out)
        vals.append(m)
        hit = expert == idx
        chosen = jnp.where(hit, 1.0, chosen)
        logits = jnp.where(hit, -jnp.inf, logits)
    es = [jnp.exp(v - vals[0]) for v in vals]
    tot = es[0] + es[1] + es[2] + es[3]
    for k in range(TOP_K):
        val_out = jnp.where(slot == k, es[k] / tot, val_out)
    idx_ref[...] = idx_out
    wt_ref[...] = val_out
    cnt_ref[0] = jnp.broadcast_to(jnp.sum(chosen, axis=1, keepdims=True), cnt_ref.shape[1:])


def _router(x1, g_ffn, w_router_t, b_router_col):
    n, d = x1.shape
    tm = min(MOE_TILE, n)
    return pl.pallas_call(
        _router_kernel,
        grid=(n // tm,),
        in_specs=[
            pl.BlockSpec((tm, d), lambda i: (i, 0)),
            pl.BlockSpec((1, d), lambda i: (0, 0)),
            pl.BlockSpec((N_EXPERTS, 3 * d), lambda i: (0, 0)),
            pl.BlockSpec((N_EXPERTS, 1), lambda i: (0, 0)),
        ],
        out_specs=[pl.BlockSpec((8, tm), lambda i: (0, i)),
                   pl.BlockSpec((8, tm), lambda i: (0, i)),
                   pl.BlockSpec((1, N_EXPERTS, LANES), lambda i: (i, 0, 0))],
        out_shape=[jax.ShapeDtypeStruct((8, n), jnp.int32),
                   jax.ShapeDtypeStruct((8, n), F32),
                   jax.ShapeDtypeStruct((n // tm, N_EXPERTS, LANES), F32)],
        compiler_params=_cparams(("arbitrary",)),
    )(x1, g_ffn, w_router_t, b_router_col)


def _routing_plan(cnt, n, tm, tb):
    nt = n // tm
    gran = MOE_GRAN
    rc = (cnt + gran - 1) // gran * gran
    covered = jnp.sum(rc, axis=0)
    region = (covered + tb - 1) // tb * tb
    pad_end = jnp.cumsum(region)
    pad_start = pad_end - region
    seg_start = pad_start[None, :] + jnp.cumsum(rc, axis=0) - rc
    stage_off = jnp.cumsum(rc, axis=1) - rc
    n_used = pad_end[-1] // tb
    nb = (TOP_K * n + nt * N_EXPERTS * (gran - 1) + N_EXPERTS * (tb - 1) + tb - 1) // tb
    blk = jnp.arange(nb, dtype=jnp.int32)
    block_e = jnp.sum((pad_end[None, :] <= (jnp.minimum(blk, n_used - 1) * tb)[:, None]).astype(jnp.int32), axis=1)
    block_e = jnp.minimum(block_e, N_EXPERTS - 1)
    flat = lambda a: a.reshape(-1).astype(jnp.int32)
    tail_gran = (region - covered) // gran
    ngran = rc // gran
    g_end = jnp.cumsum(ngran, axis=1)
    gi = jnp.arange(_stage_rows(tm) // gran, dtype=jnp.int32)
    g_exp = jnp.minimum(jnp.sum((g_end[:, None, :] <= gi[None, :, None]).astype(jnp.int32), axis=2), N_EXPERTS - 1)
    mine = g_exp[:, :, None] == jnp.arange(N_EXPERTS, dtype=jnp.int32)[None, None, :]
    seg_base = seg_start - gran * (g_end - ngran)
    gran_dst = jnp.sum(jnp.where(mine, seg_base[:, None, :], 0), axis=2) + gran * gi[None, :]
    ids = jnp.arange(N_EXPERTS, dtype=jnp.int32)
    later = jnp.where((ids[None, :] > ids[:, None]) & (region[None, :] > 0), ids[None, :], N_EXPERTS)
    nxt = jnp.min(later, axis=1)
    next_expert = jnp.where(nxt < N_EXPERTS, nxt, ids)
    return dict(gran_dst=flat(gran_dst), tot_gran=flat(g_end[:, -1]), block_e=flat(block_e),
                next_expert=flat(next_expert),
                n_used=flat(n_used), nb=nb, stage_off_f=stage_off.astype(F32),
                tail_start=flat(pad_start + covered), tail_gran=flat(tail_gran),
                tot_tail=flat(jnp.sum(tail_gran)))


def _stage_rows(tm):
    rows = TOP_K * tm + N_EXPERTS * (MOE_GRAN - 1)
    return (rows + 2 * LANES - 1) // (2 * LANES) * (2 * LANES)


GRAN_UNROLL = 4


def _granule_copies(t, gdst_ref, totg_ref, make_copy, start):
    total = totg_ref[t]
    base = t * (gdst_ref.shape[0] // totg_ref.shape[0])
    full = lax.shift_right_logical(total, GRAN_UNROLL.bit_length() - 1)

    def one(i):
        make_copy(pl.multiple_of(i * MOE_GRAN, MOE_GRAN), pl.multiple_of(gdst_ref[base + i], MOE_GRAN),
                  MOE_GRAN).start()

    def group(q, c):
        if start:
            for u in range(GRAN_UNROLL):
                one(q * GRAN_UNROLL + u)
        else:
            make_copy(0, 0, GRAN_UNROLL * MOE_GRAN).wait()
        return c

    def rest(i, c):
        if start:
            one(i)
        else:
            make_copy(0, 0, MOE_GRAN).wait()
        return c

    lax.fori_loop(0, full, group, 0)
    lax.fori_loop(full * GRAN_UNROLL, total, rest, 0)


def _dispatch_kernel(gdst_ref, totg_ref, tstart_ref, tgran_ref, misc_ref,
                     x1_ref, g_ref, idxt_ref, soffc_ref, xs_hbm, stage, zbuf, sem, zsem, *, n_blocks):
    t = pl.program_id(0)
    slot = t % 2
    tm = x1_ref.shape[0]
    rows = stage.shape[1]
    h = _rms(x1_ref[...], g_ref[...]).astype(BF16)
    idxt = idxt_ref[...]
    expert = lax.broadcasted_iota(jnp.int32, (LANES, tm), 0)
    hots = [idxt[k:k + 1, :] == expert for k in range(TOP_K)]
    multi = jnp.zeros((LANES, tm), F32)
    for hot in hots:
        multi = jnp.where(hot, 1.0, multi)
    earlier = (lax.broadcasted_iota(jnp.int32, (tm, tm), 0)
               < lax.broadcasted_iota(jnp.int32, (tm, tm), 1)).astype(BF16)
    rank = jnp.dot(multi.astype(BF16), earlier, preferred_element_type=F32)
    pos = rank + soffc_ref[0]
    srows = [jnp.sum(jnp.where(hot, pos, 0.0), axis=0, keepdims=True) for hot in hots]
    chunk = 2 * LANES
    row_id = lax.broadcasted_iota(jnp.int32, (chunk, tm), 0).astype(F32)
    for c in range(rows // chunk):
        perm = jnp.zeros((chunk, tm), F32)
        for srow in srows:
            perm = jnp.where(row_id == srow - float(c * chunk), 1.0, perm)
        stage[slot, c * chunk:(c + 1) * chunk, :] = jnp.dot(
            perm.astype(BF16), h, preferred_element_type=F32).astype(stage.dtype)

    def copy_from(s):
        def copy(stage_row, sorted_row, nrows):
            return pltpu.make_async_copy(stage.at[s, pl.ds(stage_row, nrows)],
                                         xs_hbm.at[pl.ds(sorted_row, nrows)], sem.at[s])
        return copy

    _granule_copies(t, gdst_ref, totg_ref, copy_from(slot), start=True)

    @pl.when(t > 0)
    def _():
        _granule_copies(jnp.maximum(t - 1, 0), gdst_ref, totg_ref, copy_from(1 - slot), start=False)

    @pl.when(t == pl.num_programs(0) - 1)
    def _():
        _granule_copies(t, gdst_ref, totg_ref, copy_from(slot), start=False)
        tb = zbuf.shape[0]
        n_used = misc_ref[0]
        zbuf[...] = jnp.zeros_like(zbuf)

        def zero_gran(row):
            return pltpu.make_async_copy(zbuf.at[pl.ds(0, MOE_GRAN)], xs_hbm.at[pl.ds(row, MOE_GRAN)], zsem)

        def zero_block(blk):
            return pltpu.make_async_copy(zbuf, xs_hbm.at[pl.ds(pl.multiple_of(blk * tb, tb), tb)], zsem)

        def tails(e, c):
            def one(g, c2):
                zero_gran(pl.multiple_of(tstart_ref[e] + g * MOE_GRAN, MOE_GRAN)).start()
                return c2
            lax.fori_loop(0, tgran_ref[e], one, 0)
            return c

        def start_block(blk, c):
            zero_block(blk).start()
            return c

        def wait_gran(g, c):
            zero_gran(0).wait()
            return c

        def wait_block(blk, c):
            zero_block(0).wait()
            return c

        lax.fori_loop(0, N_EXPERTS, tails, 0)
        lax.fori_loop(n_used, n_blocks, start_block, 0)
        lax.fori_loop(0, misc_ref[1], wait_gran, 0)
        lax.fori_loop(n_used, n_blocks, wait_block, 0)


def _dispatch(x1, g_ffn, idx_t, plan, tm):
    n, d = x1.shape
    nt = n // tm
    rows = _stage_rows(tm)
    soff_col = plan["stage_off_f"].reshape(nt, N_EXPERTS, 1)
    soff_col = jnp.pad(soff_col, ((0, 0), (0, LANES - N_EXPERTS), (0, 0)))
    misc = jnp.concatenate([plan["n_used"], plan["tot_tail"]])
    grid_spec = pltpu.PrefetchScalarGridSpec(
        num_scalar_prefetch=5,
        grid=(nt,),
        in_specs=[
            pl.BlockSpec((tm, d), lambda t, *_: (t, 0)),
            pl.BlockSpec((1, d), lambda t, *_: (0, 0)),
            pl.BlockSpec((8, tm), lambda t, *_: (0, t)),
            pl.BlockSpec((1, LANES, 1), lambda t, *_: (t, 0, 0)),
        ],
        out_specs=pl.BlockSpec(memory_space=pl.ANY),
        scratch_shapes=[pltpu.VMEM((2, rows, d), BF16), pltpu.VMEM((MOE_BLOCK, d), BF16),
                        pltpu.SemaphoreType.DMA((2,)), pltpu.SemaphoreType.DMA(())],
    )
    return pl.pallas_call(
        functools.partial(_dispatch_kernel, n_blocks=plan["nb"]),
        grid_spec=grid_spec,
        out_shape=jax.ShapeDtypeStruct((plan["nb"] * MOE_BLOCK, d), BF16),
        compiler_params=_cparams(("arbitrary",)),
    )(plan["gran_dst"], plan["tot_gran"], plan["tail_start"], plan["tail_gran"], misc, x1, g_ffn, idx_t,
      soff_col)


def _expert_kernel(be_ref, nused_ref, nexte_ref, x_ref, w1_hbm, b1_ref, w2_hbm, b2_ref, o_ref,
                   w1f, w2f, w1b, w2b, sem):
    de = w2f.shape[0]
    j = pl.program_id(0)
    live = j < nused_ref[0]
    e = be_ref[j]
    new_expert = jnp.logical_or(j == 0, e != be_ref[jnp.maximum(j - 1, 0)])

    def fetch(ex):
        return (pltpu.make_async_copy(w1_hbm.at[ex], w1f, sem.at[0]),
                pltpu.make_async_copy(w2_hbm.at[ex], w2f, sem.at[1]))

    @pl.when(jnp.logical_and(live, j == 0))
    def _():
        for c in fetch(e):
            c.start()

    @pl.when(jnp.logical_and(live, new_expert))
    def _():
        for c in fetch(e):
            c.wait()
        w1b[...] = w1f[...].astype(BF16)
        w2b[...] = w2f[...].astype(BF16)
        nxt = nexte_ref[e]

        @pl.when(nxt != e)
        def _():
            for c in fetch(nxt):
                c.start()

    @pl.when(live)
    def _():
        hm = jnp.dot(x_ref[...], w1b[...], preferred_element_type=F32) + b1_ref[0]
        gate = jnp.minimum(hm[:, :de], SWIGLU_LIMIT)
        up = jnp.clip(hm[:, de:], -SWIGLU_LIMIT, SWIGLU_LIMIT)
        act = gate * _sigmoid(SWIGLU_ALPHA * gate) * (up + 1.0)
        y = jnp.dot(act.astype(BF16), w2b[...], preferred_element_type=F32) + b2_ref[0]
        o_ref[...] = y.astype(o_ref.dtype)

    @pl.when(jnp.logical_not(live))
    def _():
        o_ref[...] = jnp.zeros_like(o_ref)


def _experts(xs, plan, w1, b1, w2, b2):
    d = xs.shape[1]
    tb = MOE_BLOCK
    f2 = w1.shape[2]
    de = w2.shape[1]
    last = lambda j, nu: jnp.maximum(jnp.minimum(j, nu[0] - 1), 0)
    grid_spec = pltpu.PrefetchScalarGridSpec(
        num_scalar_prefetch=3,
        grid=(plan["nb"],),
        in_specs=[
            pl.BlockSpec((tb, d), lambda j, be, nu, ne: (last(j, nu), 0)),
            pl.BlockSpec(memory_space=pl.ANY),
            pl.BlockSpec((1, 1, f2), lambda j, be, nu, ne: (be[j], 0, 0)),
            pl.BlockSpec(memory_space=pl.ANY),
            pl.BlockSpec((1, 1, d), lambda j, be, nu, ne: (be[j], 0, 0)),
        ],
        out_specs=pl.BlockSpec((tb, d), lambda j, be, nu, ne: (j, 0)),
        scratch_shapes=[pltpu.VMEM((d, f2), F32), pltpu.VMEM((de, d), F32),
                        pltpu.VMEM((d, f2), BF16), pltpu.VMEM((de, d), BF16),
                        pltpu.SemaphoreType.DMA((2,))],
    )
    return pl.pallas_call(
        _expert_kernel,
        grid_spec=grid_spec,
        out_shape=jax.ShapeDtypeStruct(xs.shape, BF16),
        compiler_params=_cparams(("arbitrary",)),
    )(plan["block_e"], plan["n_used"], plan["next_expert"], xs, w1, b1, w2, b2)


def _combine_kernel(gdst_ref, totg_ref, x1_ref, idx_ref, wt_ref, soffr_ref, g_ref, yb_hbm, o_ref, stage, sem):
    t = pl.program_id(0)
    slot = t % 2
    tm = x1_ref.shape[0]
    rows = stage.shape[1]

    def copy_into(s):
        def copy(stage_row, sorted_row, nrows):
            return pltpu.make_async_copy(yb_hbm.at[pl.ds(sorted_row, nrows)],
                                         stage.at[s, pl.ds(stage_row, nrows)], sem.at[s])
        return copy

    @pl.when(t == 0)
    def _():
        stage[...] = jnp.zeros_like(stage)
        _granule_copies(t, gdst_ref, totg_ref, copy_into(slot), start=True)

    @pl.when(t + 1 < pl.num_programs(0))
    def _():
        _granule_copies(t + 1, gdst_ref, totg_ref, copy_into(1 - slot), start=True)

    idx = idx_ref[...]
    wt = wt_ref[...]
    expert = lax.broadcasted_iota(jnp.int32, (tm, LANES), 1)
    hots = [idx[:, k:k + 1] == expert for k in range(TOP_K)]
    multi = jnp.zeros((tm, LANES), F32)
    for hot in hots:
        multi = jnp.where(hot, 1.0, multi)
    earlier = (lax.broadcasted_iota(jnp.int32, (tm, tm), 0)
               > lax.broadcasted_iota(jnp.int32, (tm, tm), 1)).astype(BF16)
    rank = jnp.dot(earlier, multi.astype(BF16), preferred_element_type=F32)
    pos = rank + soffr_ref[0]
    srows = [jnp.sum(jnp.where(hot, pos, 0.0), axis=-1, keepdims=True) for hot in hots]
    chunk = LANES
    row_id = lax.broadcasted_iota(jnp.int32, (chunk, rows), 1).astype(F32)
    for c in range(tm // chunk):
        sl = slice(c * chunk, (c + 1) * chunk)
        unsort = jnp.zeros((chunk, rows), F32)
        for k, srow in enumerate(srows):
            unsort = jnp.where(row_id == srow[sl], wt[sl, k:k + 1], unsort)
        if c == 0:
            _granule_copies(t, gdst_ref, totg_ref, copy_into(slot), start=False)
        y = jnp.dot(unsort.astype(BF16), stage[slot], preferred_element_type=F32)
        o_ref[sl, :] = _rms(x1_ref[sl, :] + y, g_ref[...])


def _combine(x1, yb, top_idx, top_w, plan, g_final, tm):
    n, d = x1.shape
    nt = n // tm
    rows = _stage_rows(tm)
    soff_row = jnp.pad(plan["stage_off_f"], ((0, 0), (0, LANES - N_EXPERTS))).reshape(nt, 1, LANES)
    grid_spec = pltpu.PrefetchScalarGridSpec(
        num_scalar_prefetch=2,
        grid=(nt,),
        in_specs=[
            pl.BlockSpec((tm, d), lambda t, *_: (t, 0)),
            pl.BlockSpec((tm, LANES), lambda t, *_: (t, 0)),
            pl.BlockSpec((tm, LANES), lambda t, *_: (t, 0)),
            pl.BlockSpec((1, 1, LANES), lambda t, *_: (t, 0, 0)),
            pl.BlockSpec((1, d), lambda t, *_: (0, 0)),
            pl.BlockSpec(memory_space=pl.ANY),
        ],
        out_specs=pl.BlockSpec((tm, d), lambda t, *_: (t, 0)),
        scratch_shapes=[pltpu.VMEM((2, rows, d), BF16), pltpu.SemaphoreType.DMA((2,))],
    )
    return pl.pallas_call(
        _combine_kernel,
        grid_spec=grid_spec,
        out_shape=jax.ShapeDtypeStruct((n, d), F32),
        compiler_params=_cparams(("arbitrary",)),
    )(plan["gran_dst"], plan["tot_gran"], x1, top_idx, top_w, soff_row, g_final, yb)


def _rope_tables(seq):
    half = ATTN_HEAD_DIM // 2
    inv_freq = np.float32(ROPE_THETA) ** (-np.arange(0, half, 2, dtype=np.float32) / np.float32(half))
    ang_row = np.arange(seq // GRID_W, dtype=np.float32)[:, None] * inv_freq
    ang_col = np.arange(GRID_W, dtype=np.float32)[:, None] * inv_freq
    by_row = lambda t: jnp.repeat(jnp.asarray(t, F32), GRID_W, axis=0)
    by_col = lambda t: jnp.tile(jnp.asarray(t, F32), (seq // GRID_W, 1))
    cos_r, sin_r = by_row(np.cos(ang_row)), by_row(np.sin(ang_row))
    cos_c, sin_c = by_col(np.cos(ang_col)), by_col(np.sin(ang_col))
    cos_t = jnp.concatenate([cos_r, cos_c] * 2, axis=-1)
    sin_t = jnp.concatenate([-sin_r, -sin_c, sin_r, sin_c], axis=-1)
    return cos_t, sin_t


def _token_mixer(x2, batch, seq, g_mix, w_in, conv_w, conv_b, dt_bias_f, dt_bias_b, a_log_f, a_log_b, d_skip,
                 g_ssm, q_norm_g, k_norm_g, w_br_ssm, w_br_attn, w_out):
    n, d = x2.shape
    z_end = SSM_INNER
    xbc_end = z_end + CONV_CH
    dtf_end = xbc_end + SSM_HEADS
    dtb_end = dtf_end + SSM_HEADS
    q_end = dtb_end + ATTN_HEADS * ATTN_HEAD_DIM
    k_end = q_end + ATTN_KV_HEADS * ATTN_HEAD_DIM
    v_end = k_end + ATTN_KV_HEADS * ATTN_HEAD_DIM
    head_cols = lambda w: _rope_head_order(w.reshape(d, -1, ATTN_HEAD_DIM)).reshape(d, -1)
    w_main = jnp.concatenate([w_in[:, :z_end], head_cols(w_in[:, dtb_end:q_end]), w_in[:, v_end:],
                              w_in[:, z_end:xbc_end], head_cols(w_in[:, q_end:k_end]), w_in[:, k_end:v_end]],
                             axis=1).astype(BF16)
    w_dt = jnp.pad(w_in[:, xbc_end:dtb_end], ((0, 0), (0, LANES - 2 * SSM_HEADS))).astype(BF16)

    proj, dt = _in_proj(x2, g_mix.reshape(1, d), w_main, w_dt)
    proj3 = proj.reshape(batch, seq, PROJ_COLS)

    xbc = _conv(proj3, conv_w, conv_b.reshape(1, CONV_CH))

    dt3 = dt.reshape(batch, seq, LANES)
    dtt3 = jnp.swapaxes(dt3[:, :, :2 * SSM_HEADS], 1, 2)
    bias = jnp.concatenate([dt_bias_f, dt_bias_b])
    alog = jnp.concatenate([a_log_f, a_log_b])
    pad_row = lambda v: jnp.pad(v, (0, LANES - 2 * SSM_HEADS)).reshape(1, LANES)
    y_f, y_b = _ssd(xbc, dt3, dtt3, pad_row(bias), bias.reshape(-1, 1), pad_row(alog), alog.reshape(-1, 1))

    cos_t, sin_t = _rope_tables(seq)
    q_rot, k_rot = _qk_prep(proj, cos_t, sin_t, _rope_head_order(q_norm_g).reshape(1, -1),
                            _rope_head_order(k_norm_g).reshape(1, -1), seq)
    vt3 = jnp.swapaxes(proj3[:, :, V_OFF:V_OFF + ATTN_KV_HEADS * ATTN_HEAD_DIM], 1, 2)
    attn = _flash(q_rot.reshape(batch, seq, -1), k_rot.reshape(batch, seq, -1), vt3, q_norm_g, k_norm_g)

    return _merge(y_f.reshape(n, -1), y_b.reshape(n, -1), xbc.reshape(n, CONV_CH), proj, attn.reshape(n, -1),
                  x2, jnp.repeat(d_skip, SSM_HEAD_DIM).reshape(1, -1), g_ssm.reshape(1, -1),
                  w_br_ssm.astype(BF16), w_br_attn.astype(BF16), w_out.astype(BF16))


def _moe_and_final_norm(x1, g_ffn, w_router, b_router, w_mlp1, b_mlp1, w_mlp2, b_mlp2, g_final):
    n, d = x1.shape
    tm = min(MOE_TILE, n)
    w_t = w_router.T
    w_hi = w_t.astype(BF16)
    w_lo = (w_t - w_hi.astype(F32)).astype(BF16)
    g_ffn_row = g_ffn.reshape(1, d)
    idx_t, wt_t, cnt = _router(x1, g_ffn_row, jnp.concatenate([w_hi, w_lo, w_hi], axis=1),
                               b_router.reshape(N_EXPERTS, 1))
    plan = _routing_plan(cnt[:, :, 0].astype(jnp.int32), n, tm, MOE_BLOCK)
    lanes_0_3 = lambda a: jnp.pad(a[:TOP_K].T, ((0, 0), (0, LANES - TOP_K)))
    top_idx, top_w = lanes_0_3(idx_t), lanes_0_3(wt_t)
    xs = _dispatch(x1, g_ffn_row, idx_t, plan, tm)
    yb = _experts(xs, plan, w_mlp1, b_mlp1[:, None, :], w_mlp2, b_mlp2[:, None, :])
    return _combine(x1, yb, top_idx, top_w, plan, g_final.reshape(1, d), tm)


def kernel(x, g_mix, w_in, conv_w, conv_b, dt_bias_f, dt_bias_b, a_log_f, a_log_b, d_skip, g_ssm, q_norm_g,
           k_norm_g, w_br_ssm, w_br_attn, w_out, g_ffn, w_router, b_router, w_mlp1, b_mlp1, w_mlp2, b_mlp2,
           g_final):
    batch, seq, d = x.shape
    assert g_mix.shape[0] == 1, "single-layer model: the final rmsnorm is fused into the MoE combine"
    x2 = x.reshape(batch * seq, d)
    x1 = _token_mixer(x2, batch, seq, g_mix[0], w_in[0], conv_w[0], conv_b[0], dt_bias_f[0], dt_bias_b[0],
                      a_log_f[0], a_log_b[0], d_skip[0], g_ssm[0], q_norm_g[0], k_norm_g[0], w_br_ssm[0],
                      w_br_attn[0], w_out[0])
    out = _moe_and_final_norm(x1, g_ffn[0], w_router[0], b_router[0], w_mlp1[0], b_mlp1[0], w_mlp2[0],
                              b_mlp2[0], g_final)
    return out.reshape(batch, seq, d)
```

```python
import functools
import math

import jax
import jax.numpy as jnp
import numpy as np
from jax import lax
from jax.experimental import pallas as pl
from jax.experimental.pallas import tpu as pltpu

F32 = jnp.float32
BF16 = jnp.bfloat16

NORM_EPS = 1e-6
GRID_W = 64
SSM_HEADS = 16
SSM_HEAD_DIM = 64
SSM_INNER = SSM_HEADS * SSM_HEAD_DIM
SSM_GROUPS = 2
SSM_STATE = 128
SSM_CONV = 5
CONV_CH = SSM_INNER + 2 * SSM_GROUPS * SSM_STATE
ATTN_HEADS = 8
ATTN_KV_HEADS = 2
ATTN_HEAD_DIM = 128
ROPE_THETA = 10000.0
N_EXPERTS = 32
TOP_K = 4
SWIGLU_LIMIT = 7.0
SWIGLU_ALPHA = 1.702

LANES = 128
BF16_SUBLANES = 16
VMEM_LIMIT = 56 * 1024 * 1024

Z_OFF, Q_OFF, GATE_OFF, XBC_OFF = 0, 1024, 2048, 4096
K_OFF, V_OFF, PROJ_COLS = 5632, 5888, 6144

MOE_TILE = 512
MOE_BLOCK = 512
MOE_GRAN = BF16_SUBLANES


def _cparams(sem):
    return pltpu.CompilerParams(dimension_semantics=sem, vmem_limit_bytes=VMEM_LIMIT)


def _sigmoid(x):
    return 1.0 / (1.0 + jnp.exp(-x))


def _softplus(x):
    return jnp.maximum(x, 0.0) + jnp.log(1.0 + jnp.exp(-jnp.abs(x)))


def _rms(x, g):
    ms = jnp.mean(x * x, axis=-1, keepdims=True)
    return x * lax.rsqrt(ms + NORM_EPS) * g


def _inproj_kernel(x_ref, g_ref, w_ref, wdt_ref, o_ref, dt_ref, h_scr):
    @pl.when(pl.program_id(1) == 0)
    def _():
        hb = _rms(x_ref[...], g_ref[...]).astype(BF16)
        h_scr[...] = hb
        dt_ref[...] = jnp.dot(hb, wdt_ref[...], preferred_element_type=F32)

    o_ref[...] = jnp.dot(h_scr[...], w_ref[...], preferred_element_type=F32).astype(o_ref.dtype)


def _in_proj(x2, g_mix, w_main, w_dt):
    n, d = x2.shape
    tm = min(1024, n)
    tn = 2048
    return pl.pallas_call(
        _inproj_kernel,
        grid=(n // tm, PROJ_COLS // tn),
        in_specs=[
            pl.BlockSpec((tm, d), lambda i, j: (i, 0)),
            pl.BlockSpec((1, d), lambda i, j: (0, 0)),
            pl.BlockSpec((d, tn), lambda i, j: (0, j)),
            pl.BlockSpec((d, LANES), lambda i, j: (0, 0)),
        ],
        out_specs=[
            pl.BlockSpec((tm, tn), lambda i, j: (i, j)),
            pl.BlockSpec((tm, LANES), lambda i, j: (i, 0)),
        ],
        out_shape=[
            jax.ShapeDtypeStruct((n, PROJ_COLS), BF16),
            jax.ShapeDtypeStruct((n, LANES), F32),
        ],
        scratch_shapes=[pltpu.VMEM((tm, d), BF16)],
        compiler_params=_cparams(("arbitrary", "arbitrary")),
    )(x2, g_mix, w_main, w_dt)


CONV_HALO = 64
CONV_ROWS = 128


def _conv_kernel(prev_ref, cur_ref, next_ref, shift_ref, w_ref, b_ref, o_ref):
    s = pl.program_id(1)
    ts = cur_ref.shape[1]
    prev = prev_ref[0]
    nxt = next_ref[0]
    zero = jnp.zeros_like(prev)
    ext = jnp.concatenate([jnp.where(s == 0, zero, prev), cur_ref[0],
                           jnp.where(s == pl.num_programs(1) - 1, zero, nxt)], axis=0)
    shift = shift_ref[...]
    for rb in range(ts // CONV_ROWS):
        lo = rb * CONV_ROWS
        taps = jnp.dot(shift, ext[lo:lo + CONV_ROWS + 2 * CONV_HALO], preferred_element_type=F32)
        acc = b_ref[...] + w_ref[0:1, :] * taps[0:CONV_ROWS]
        for k in range(1, SSM_CONV):
            acc = acc + w_ref[k:k + 1, :] * taps[k * CONV_ROWS:(k + 1) * CONV_ROWS]
        o_ref[0, lo:lo + CONV_ROWS, :] = (acc * _sigmoid(acc)).astype(o_ref.dtype)


def _conv(proj3, conv_w, conv_b):
    b, s, _ = proj3.shape
    ts = min(1024, s)
    tc = 512
    halo = CONV_HALO
    hb = ts // halo
    col0 = XBC_OFF // tc
    pad = (SSM_CONV - 1) // 2
    out_row = np.arange(SSM_CONV * CONV_ROWS)
    src = out_row % CONV_ROWS + halo + out_row // CONV_ROWS - pad
    shift = jnp.asarray(src[:, None] == np.arange(CONV_ROWS + 2 * halo)[None, :], BF16)
    return pl.pallas_call(
        _conv_kernel,
        grid=(b, s // ts, CONV_CH // tc),
        in_specs=[
            pl.BlockSpec((1, halo, tc), lambda bi, si, ci: (bi, jnp.maximum(si * hb - 1, 0), col0 + ci)),
            pl.BlockSpec((1, ts, tc), lambda bi, si, ci: (bi, si, col0 + ci)),
            pl.BlockSpec((1, halo, tc),
                         lambda bi, si, ci: (bi, jnp.minimum((si + 1) * hb, s // halo - 1), col0 + ci)),
            pl.BlockSpec(shift.shape, lambda bi, si, ci: (0, 0)),
            pl.BlockSpec((SSM_CONV, tc), lambda bi, si, ci: (0, ci)),
            pl.BlockSpec((1, tc), lambda bi, si, ci: (0, ci)),
        ],
        out_specs=pl.BlockSpec((1, ts, tc), lambda bi, si, ci: (bi, si, ci)),
        out_shape=jax.ShapeDtypeStruct((b, s, CONV_CH), BF16),
        compiler_params=_cparams(("arbitrary", "arbitrary", "arbitrary")),
    )(proj3, proj3, proj3, shift, conv_w, conv_b)


def _ssd_kernel(xf_ref, xb_ref, dtf_ref, dtb_ref, dttf_ref, dttb_ref, brow_ref, bcol_ref,
                arow_ref, acol_ref, yf_ref, yb_ref, st_ref):
    L = xf_ref.shape[1]
    hg = SSM_HEADS // SSM_GROUPS
    pairs = hg // 2

    @pl.when(pl.program_id(1) == 0)
    def _():
        st_ref[...] = jnp.zeros_like(st_ref)

    rows = lax.broadcasted_iota(jnp.int32, (L, L), 0)
    cols = lax.broadcasted_iota(jnp.int32, (L, L), 1)
    lower = rows >= cols
    upper = rows <= cols
    ltri = lower.astype(BF16)
    utri = upper.astype(BF16)
    lane = lax.broadcasted_iota(jnp.int32, (L, LANES), 1)
    first_half = lane < SSM_HEAD_DIM
    lane1 = lax.broadcasted_iota(jnp.int32, (1, LANES), 1)
    log2e = math.log2(math.e)
    a_row = -jnp.exp(arow_ref[...]) * log2e
    a_col = -jnp.exp(acol_ref[...]) * log2e

    def split3(v):
        hi = v.astype(BF16)
        r1 = v - hi.astype(F32)
        mid = r1.astype(BF16)
        return hi, mid, (r1 - mid.astype(F32)).astype(BF16)

    def cumsum_cols(tri, v):
        return jnp.dot(jnp.concatenate([tri] * 3, axis=1), jnp.concatenate(split3(v), axis=0),
                       preferred_element_type=F32)

    def cumsum_rows(v, tri):
        return jnp.dot(jnp.concatenate(split3(v), axis=1), jnp.concatenate([tri] * 3, axis=0),
                       preferred_element_type=F32)

    for d in range(2):
        x_ref, dt_ref, dtt_ref, y_ref = ((xf_ref, dtf_ref, dttf_ref, yf_ref) if d == 0
                                         else (xb_ref, dtb_ref, dttb_ref, yb_ref))
        a = _softplus(dt_ref[0] + brow_ref[...]) * a_row
        dt_t = _softplus(dtt_ref[0] + bcol_ref[...])
        a_t = dt_t * a_col
        if d == 0:
            cs_col = cumsum_cols(ltri, a)
            cs_row = cumsum_rows(a_t, utri)
            tot = cs_col[L - 1:L, :]
            tot_t = cs_row[:, L - 1:L]
            mask = lower
        else:
            cs_col = cumsum_cols(utri, a)
            cs_row = cumsum_rows(a_t, ltri)
            tot = cs_col[0:1, :]
            tot_t = cs_row[:, 0:1]
            mask = upper
        w_t = dt_t * jnp.exp2(tot_t - cs_row)
        src_t = cs_row - jnp.log2(dt_t)
        chunk_decay = jnp.exp2(tot)

        for g in range(SSM_GROUPS):
            boff = SSM_INNER + g * SSM_STATE
            coff = SSM_INNER + SSM_GROUPS * SSM_STATE + g * SSM_STATE
            bm = x_ref[0, :, boff:boff + SSM_STATE]
            cm = x_ref[0, :, coff:coff + SSM_STATE]
            cb = lax.dot_general(cm, bm, (((1,), (1,)), ((), ())), preferred_element_type=F32)
            bt = bm.astype(F32).T
            st = st_ref[d, g]
            y_off = jnp.dot(cm, st.astype(BF16), preferred_element_type=F32)
            for pr in range(pairs):
                h0 = d * SSM_HEADS + g * hg + 2 * pr
                xoff = (g * pairs + pr) * LANES
                xs = x_ref[0, :, xoff:xoff + LANES]
                zero = jnp.zeros_like(xs)
                rhs = jnp.concatenate([jnp.where(first_half, xs, zero),
                                       jnp.where(first_half, zero, xs)], axis=0)
                ms, ws, dins = [], [], []
                for hh in (h0, h0 + 1):
                    cs_b = jnp.broadcast_to(cs_col[:, hh:hh + 1], (L, L))
                    seg = cs_b - src_t[hh:hh + 1, :]
                    m = cb * jnp.exp2(jnp.where(mask, seg, -jnp.inf))
                    ms.append(m.astype(BF16))
                    ws.append((bt * w_t[hh:hh + 1, :]).astype(BF16))
                    dins.append(jnp.exp2(cs_b))
                y = jnp.dot(jnp.concatenate(ms, axis=1), rhs, preferred_element_type=F32)
                y = y + y_off[:, pr * LANES:(pr + 1) * LANES] * jnp.where(first_half, dins[0], dins[1])
                y_ref[0, :, xoff:xoff + LANES] = y.astype(y_ref.dtype)
                new_st = jnp.dot(jnp.concatenate(ws, axis=1), rhs, preferred_element_type=F32)
                cd = jnp.where(lane1 < SSM_HEAD_DIM, chunk_decay[:, h0:h0 + 1], chunk_decay[:, h0 + 1:h0 + 2])
                st_ref[d, g, :, pr * LANES:(pr + 1) * LANES] = st[:, pr * LANES:(pr + 1) * LANES] * cd + new_st


def _ssd(xbc, dt3, dtt3, bias_row, bias_col, alog_row, alog_col):
    b, s, _ = xbc.shape
    L = min(128, s)
    nc = s // L
    hg = SSM_HEADS // SSM_GROUPS
    fwd = lambda bi, ci: (bi, ci, 0)
    bwd = lambda bi, ci: (bi, nc - 1 - ci, 0)
    fwd_t = lambda bi, ci: (bi, 0, ci)
    bwd_t = lambda bi, ci: (bi, 0, nc - 1 - ci)
    const = lambda bi, ci: (0, 0)
    return pl.pallas_call(
        _ssd_kernel,
        grid=(b, nc),
        in_specs=[
            pl.BlockSpec((1, L, CONV_CH), fwd),
            pl.BlockSpec((1, L, CONV_CH), bwd),
            pl.BlockSpec((1, L, LANES), fwd),
            pl.BlockSpec((1, L, LANES), bwd),
            pl.BlockSpec((1, 2 * SSM_HEADS, L), fwd_t),
            pl.BlockSpec((1, 2 * SSM_HEADS, L), bwd_t),
            pl.BlockSpec((1, LANES), const),
            pl.BlockSpec((2 * SSM_HEADS, 1), const),
            pl.BlockSpec((1, LANES), const),
            pl.BlockSpec((2 * SSM_HEADS, 1), const),
        ],
        out_specs=[
            pl.BlockSpec((1, L, SSM_INNER), fwd),
            pl.BlockSpec((1, L, SSM_INNER), bwd),
        ],
        out_shape=[jax.ShapeDtypeStruct((b, s, SSM_INNER), BF16)] * 2,
        scratch_shapes=[pltpu.VMEM((2, SSM_GROUPS, SSM_STATE, hg * SSM_HEAD_DIM), F32)],
        compiler_params=_cparams(("arbitrary", "arbitrary")),
    )(xbc, xbc, dt3, dt3, dtt3, dtt3, bias_row, bias_col, alog_row, alog_col)


def _rope_head_order(a):
    q4 = ATTN_HEAD_DIM // 4
    return jnp.concatenate([a[..., 0:q4], a[..., 2 * q4:3 * q4], a[..., q4:2 * q4], a[..., 3 * q4:]], axis=-1)


def _rope_norm(t, g, cos, sin_signed, ones):
    sq = t * t
    hi = sq.astype(BF16)
    lo = (sq - hi.astype(F32)).astype(BF16)
    ms = jnp.dot(jnp.concatenate([hi, lo], axis=1), ones, preferred_element_type=F32) * (1.0 / ATTN_HEAD_DIM)
    tn = t * lax.rsqrt(ms + NORM_EPS) * g
    return tn * cos + pltpu.roll(tn, ATTN_HEAD_DIM // 2, 1) * sin_signed


Q_SCALE = ATTN_HEAD_DIM ** -0.5 * math.log2(math.e)


def _qkprep_kernel(q_ref, k_ref, cos_ref, sin_ref, qg_ref, kg_ref, qo_ref, ko_ref):
    cos = cos_ref[...]
    sin = sin_ref[...]
    ones = jnp.ones((2 * ATTN_HEAD_DIM, ATTN_HEAD_DIM), BF16)

    def heads(src_ref, g_ref, dst_ref, n_heads, scale):
        for h in range(n_heads):
            sl = slice(h * ATTN_HEAD_DIM, (h + 1) * ATTN_HEAD_DIM)
            r = _rope_norm(src_ref[:, sl].astype(F32), g_ref[...], cos, sin, ones) * scale
            dst_ref[:, sl] = r.astype(dst_ref.dtype)

    heads(q_ref, qg_ref, qo_ref, ATTN_HEADS, Q_SCALE)
    heads(k_ref, kg_ref, ko_ref, ATTN_KV_HEADS, 1.0)


def _qk_prep(proj, cos_t, sin_t, q_norm_g, k_norm_g, seq):
    n = proj.shape[0]
    tm = min(512, seq)
    qw = ATTN_HEADS * ATTN_HEAD_DIM
    kw = ATTN_KV_HEADS * ATTN_HEAD_DIM
    spt = seq // tm
    return pl.pallas_call(
        _qkprep_kernel,
        grid=(n // tm,),
        in_specs=[
            pl.BlockSpec((tm, qw), lambda i: (i, Q_OFF // qw)),
            pl.BlockSpec((tm, kw), lambda i: (i, K_OFF // kw)),
            pl.BlockSpec((tm, ATTN_HEAD_DIM), lambda i: (i % spt, 0)),
            pl.BlockSpec((tm, ATTN_HEAD_DIM), lambda i: (i % spt, 0)),
            pl.BlockSpec((1, ATTN_HEAD_DIM), lambda i: (0, 0)),
            pl.BlockSpec((1, ATTN_HEAD_DIM), lambda i: (0, 0)),
        ],
        out_specs=[
            pl.BlockSpec((tm, qw), lambda i: (i, 0)),
            pl.BlockSpec((tm, kw), lambda i: (i, 0)),
        ],
        out_shape=[jax.ShapeDtypeStruct((n, qw), BF16), jax.ShapeDtypeStruct((n, kw), BF16)],
        compiler_params=_cparams(("arbitrary",)),
    )(proj, proj, cos_t, sin_t, q_norm_g, k_norm_g)


def _flash_kernel(small_ref, q_ref, k_ref, vt_ref, o_ref, *, tk, th):
    q = q_ref[0]
    tq = q.shape[0]
    seq = k_ref.shape[1]
    nt = (((1,), (1,)), ((), ()))
    small = small_ref[0] != 0

    @pl.when(small)
    def _():
        l = acc_t = None
        for c in range(seq // th):
            ks = slice(c * th, (c + 1) * th)
            p_t = jnp.exp2(lax.dot_general(k_ref[0, ks, :], q, nt, preferred_element_type=F32))
            l_c = jnp.sum(p_t, axis=0, keepdims=True)
            a_c = jnp.dot(vt_ref[0, :, ks], p_t.astype(BF16), preferred_element_type=F32)
            l = l_c if l is None else l + l_c
            acc_t = a_c if acc_t is None else acc_t + a_c
        o_ref[0] = (acc_t / l).T.astype(o_ref.dtype)

    @pl.when(jnp.logical_not(small))
    def _():
        def body(i, carry):
            m, l, acc_t = carry
            off = pl.multiple_of(i * tk, tk)
            s_t = lax.dot_general(k_ref[0, pl.ds(off, tk), :], q, nt, preferred_element_type=F32)
            m_new = jnp.maximum(m, jnp.max(s_t, axis=0, keepdims=True))
            alpha = jnp.exp2(m - m_new)
            p_t = jnp.exp2(s_t - m_new)
            l = alpha * l + jnp.sum(p_t, axis=0, keepdims=True)
            acc_t = alpha * acc_t + jnp.dot(vt_ref[0, :, pl.ds(off, tk)], p_t.astype(BF16),
                                            preferred_element_type=F32)
            return m_new, l, acc_t

        init = (jnp.full((1, tq), -jnp.inf, F32), jnp.zeros((1, tq), F32),
                jnp.zeros((ATTN_HEAD_DIM, tq), F32))
        _, l, acc_t = lax.fori_loop(0, seq // tk, body, init)
        o_ref[0] = (acc_t / l).T.astype(o_ref.dtype)


SCORE_BOUND = 59.0


def _flash(q3, k3, vt3, q_norm_g, k_norm_g):
    b, s, _ = q3.shape
    tq = min(1024, s)
    tk = min(2048, s)
    th = min(4096, s)
    nq = s // tq
    grp = ATTN_HEADS // ATTN_KV_HEADS
    hd = ATTN_HEAD_DIM
    bound = hd * Q_SCALE * jnp.max(jnp.abs(q_norm_g)) * jnp.max(jnp.abs(k_norm_g)) * 1.02
    small = (bound <= SCORE_BOUND).astype(jnp.int32).reshape(1)
    grid_spec = pltpu.PrefetchScalarGridSpec(
        num_scalar_prefetch=1,
        grid=(b, ATTN_HEADS, nq),
        in_specs=[
            pl.BlockSpec((1, tq, hd), lambda bi, h, qi, sm: (bi, qi, h)),
            pl.BlockSpec((1, s, hd), lambda bi, h, qi, sm: (bi, 0, h // grp)),
            pl.BlockSpec((1, hd, s), lambda bi, h, qi, sm: (bi, h // grp, 0)),
        ],
        out_specs=pl.BlockSpec((1, tq, hd), lambda bi, h, qi, sm: (bi, qi, h)),
    )
    return pl.pallas_call(
        functools.partial(_flash_kernel, tk=tk, th=th),
        grid_spec=grid_spec,
        out_shape=jax.ShapeDtypeStruct((b, s, ATTN_HEADS * hd), BF16),
        compiler_params=_cparams(("arbitrary", "arbitrary", "arbitrary")),
    )(small, q3, k3, vt3)


def _merge_kernel(yf_ref, yb_ref, xs_ref, z_ref, gate_ref, attn_ref, x_ref, dskip_ref, gssm_ref,
                  wbs_ref, wba_ref, wo_ref, x1_ref):
    d = x_ref.shape[1]
    xs = xs_ref[...].astype(F32)
    y = yf_ref[...].astype(F32) + yb_ref[...].astype(F32) + xs * dskip_ref[...]
    z = z_ref[...].astype(F32)
    y = _rms(y * (z * _sigmoid(z)), gssm_ref[...])
    br_ssm = jnp.dot(y.astype(BF16), wbs_ref[...], preferred_element_type=F32)
    br_attn = jnp.dot(attn_ref[...], wba_ref[...], preferred_element_type=F32)
    g_s = _sigmoid(gate_ref[:, :d].astype(F32))
    g_a = _sigmoid(gate_ref[:, d:].astype(F32))
    merged = (g_s * br_ssm + g_a * br_attn).astype(BF16)
    x1_ref[...] = x_ref[...] + jnp.dot(merged, wo_ref[...], preferred_element_type=F32)


def _merge(y_f, y_b, xbc, proj, attn, x2, dskip_row, g_ssm, w_br_ssm, w_br_attn, w_out):
    n, d = x2.shape
    tm = min(512, n)
    row = lambda i: (i, 0)
    const = lambda i: (0, 0)
    return pl.pallas_call(
        _merge_kernel,
        grid=(n // tm,),
        in_specs=[
            pl.BlockSpec((tm, d), row),
            pl.BlockSpec((tm, d), row),
            pl.BlockSpec((tm, d), row),
            pl.BlockSpec((tm, d), lambda i: (i, Z_OFF // d)),
            pl.BlockSpec((tm, 2 * d), lambda i: (i, GATE_OFF // (2 * d))),
            pl.BlockSpec((tm, d), row),
            pl.BlockSpec((tm, d), row),
            pl.BlockSpec((1, d), const),
            pl.BlockSpec((1, d), const),
            pl.BlockSpec((d, d), const),
            pl.BlockSpec((d, d), const),
            pl.BlockSpec((d, d), const),
        ],
        out_specs=pl.BlockSpec((tm, d), row),
        out_shape=jax.ShapeDtypeStruct((n, d), F32),
        compiler_params=_cparams(("arbitrary",)),
    )(y_f, y_b, xbc, proj, proj, attn, x2, dskip_row, g_ssm, w_br_ssm, w_br_attn, w_out)


def _router_kernel(x1_ref, g_ref, w_ref, b_ref, h_ref, srow_ref, wt_ref, cnt_ref):
    tm = x1_ref.shape[0]
    h = _rms(x1_ref[...], g_ref[...])
    h_hi = h.astype(BF16)
    h_lo = (h - h_hi.astype(F32)).astype(BF16)
    h_ref[...] = h_hi
    logits = lax.dot_general(w_ref[...], jnp.concatenate([h_hi, h_hi, h_lo], axis=1), (((1,), (1,)), ((), ())),
                             preferred_element_type=F32) + b_ref[...]
    expert = lax.broadcasted_iota(jnp.int32, logits.shape, 0)
    slot = lax.broadcasted_iota(jnp.int32, wt_ref.shape, 0)
    chosen = jnp.zeros(logits.shape, F32)
    vals, hits = [], []
    for k in range(TOP_K):
        m = jnp.max(logits, axis=0, keepdims=True)
        idx = jnp.min(jnp.where(logits == m, expert, N_EXPERTS), axis=0, keepdims=True)
        vals.append(m)
        hit = expert == idx
        hits.append(hit)
        chosen = jnp.where(hit, 1.0, chosen)
        logits = jnp.where(hit, -jnp.inf, logits)
    es = [jnp.exp(v - vals[0]) for v in vals]
    tot = es[0] + es[1] + es[2] + es[3]

    chosen16 = chosen.astype(BF16)
    cnt_col = jnp.sum(chosen, axis=1, keepdims=True)
    cnt_row = lax.dot_general(jnp.ones((8, tm), BF16), chosen16, (((1,), (1,)), ((), ())),
                              preferred_element_type=F32)[0:1, :]
    seg_row = jnp.ceil(cnt_row * (1.0 / MOE_GRAN)) * MOE_GRAN
    before = (lax.broadcasted_iota(jnp.int32, (N_EXPERTS, N_EXPERTS), 1)
              < lax.broadcasted_iota(jnp.int32, (N_EXPERTS, N_EXPERTS), 0))
    seg_off = jnp.sum(jnp.where(before, seg_row, 0.0), axis=1, keepdims=True)
    earlier = (lax.broadcasted_iota(jnp.int32, (tm, tm), 0)
               < lax.broadcasted_iota(jnp.int32, (tm, tm), 1)).astype(BF16)
    pos = jnp.dot(chosen16, earlier, preferred_element_type=F32) + seg_off

    val_out = jnp.zeros(wt_ref.shape, F32)
    row_out = jnp.zeros(srow_ref.shape, F32)
    for k in range(TOP_K):
        val_out = jnp.where(slot == k, es[k] / tot, val_out)
        row_out = jnp.where(slot == k, jnp.sum(jnp.where(hits[k], pos, 0.0), axis=0, keepdims=True), row_out)
    wt_ref[...] = val_out
    srow_ref[...] = row_out
    cnt_ref[0] = jnp.broadcast_to(cnt_col, cnt_ref.shape[1:])


def _router(x1, g_ffn, w_router_t, b_router_col):
    n, d = x1.shape
    tm = min(MOE_TILE, n)
    return pl.pallas_call(
        _router_kernel,
        grid=(n // tm,),
        in_specs=[
            pl.BlockSpec((tm, d), lambda i: (i, 0)),
            pl.BlockSpec((1, d), lambda i: (0, 0)),
            pl.BlockSpec((N_EXPERTS, 3 * d), lambda i: (0, 0)),
            pl.BlockSpec((N_EXPERTS, 1), lambda i: (0, 0)),
        ],
        out_specs=[pl.BlockSpec((tm, d), lambda i: (i, 0)),
                   pl.BlockSpec((8, tm), lambda i: (0, i)),
                   pl.BlockSpec((8, tm), lambda i: (0, i)),
                   pl.BlockSpec((1, N_EXPERTS, LANES), lambda i: (i, 0, 0))],
        out_shape=[jax.ShapeDtypeStruct((n, d), BF16),
                   jax.ShapeDtypeStruct((8, n), F32),
                   jax.ShapeDtypeStruct((8, n), F32),
                   jax.ShapeDtypeStruct((n // tm, N_EXPERTS, LANES), F32)],
        compiler_params=_cparams(("arbitrary",)),
    )(x1, g_ffn, w_router_t, b_router_col)


def _routing_plan(cnt, n, tm, tb):
    nt = n // tm
    gran = MOE_GRAN
    rc = (cnt + gran - 1) // gran * gran
    covered = jnp.sum(rc, axis=0)
    region = (covered + tb - 1) // tb * tb
    pad_end = jnp.cumsum(region)
    pad_start = pad_end - region
    seg_start = pad_start[None, :] + jnp.cumsum(rc, axis=0) - rc
    n_used = pad_end[-1] // tb
    nb = (TOP_K * n + nt * N_EXPERTS * (gran - 1) + N_EXPERTS * (tb - 1) + tb - 1) // tb
    blk = jnp.arange(nb, dtype=jnp.int32)
    block_e = jnp.sum((pad_end[None, :] <= (jnp.minimum(blk, n_used - 1) * tb)[:, None]).astype(jnp.int32), axis=1)
    block_e = jnp.minimum(block_e, N_EXPERTS - 1)
    flat = lambda a: a.reshape(-1).astype(jnp.int32)
    tail_gran = (region - covered) // gran
    ngran = rc // gran
    g_end = jnp.cumsum(ngran, axis=1)
    gi = jnp.arange(_stage_rows(tm) // gran, dtype=jnp.int32)
    g_exp = jnp.minimum(jnp.sum((g_end[:, None, :] <= gi[None, :, None]).astype(jnp.int32), axis=2), N_EXPERTS - 1)
    mine = g_exp[:, :, None] == jnp.arange(N_EXPERTS, dtype=jnp.int32)[None, None, :]
    seg_base = seg_start - gran * (g_end - ngran)
    gran_dst = jnp.sum(jnp.where(mine, seg_base[:, None, :], 0), axis=2) + gran * gi[None, :]
    ids = jnp.arange(N_EXPERTS, dtype=jnp.int32)
    later = jnp.where((ids[None, :] > ids[:, None]) & (region[None, :] > 0), ids[None, :], N_EXPERTS)
    nxt = jnp.min(later, axis=1)
    next_expert = jnp.where(nxt < N_EXPERTS, nxt, ids)
    return dict(gran_dst=flat(gran_dst), tot_gran=flat(g_end[:, -1]), block_e=flat(block_e),
                next_expert=flat(next_expert),
                n_used=flat(n_used), nb=nb,
                tail_start=flat(pad_start + covered), tail_gran=flat(tail_gran),
                tot_tail=flat(jnp.sum(tail_gran)))


def _stage_rows(tm):
    rows = TOP_K * tm + N_EXPERTS * (MOE_GRAN - 1)
    return (rows + 2 * LANES - 1) // (2 * LANES) * (2 * LANES)


GRAN_UNROLL = 4


def _granule_copies(t, gdst_ref, totg_ref, make_copy, start):
    total = totg_ref[t]
    base = t * (gdst_ref.shape[0] // totg_ref.shape[0])
    full = lax.shift_right_logical(total, GRAN_UNROLL.bit_length() - 1)

    def one(i):
        make_copy(pl.multiple_of(i * MOE_GRAN, MOE_GRAN), pl.multiple_of(gdst_ref[base + i], MOE_GRAN),
                  MOE_GRAN).start()

    def group(q, c):
        if start:
            for u in range(GRAN_UNROLL):
                one(q * GRAN_UNROLL + u)
        else:
            make_copy(0, 0, GRAN_UNROLL * MOE_GRAN).wait()
        return c

    def rest(i, c):
        if start:
            one(i)
        else:
            make_copy(0, 0, MOE_GRAN).wait()
        return c

    lax.fori_loop(0, full, group, 0)
    lax.fori_loop(full * GRAN_UNROLL, total, rest, 0)


def _dispatch_kernel(gdst_ref, totg_ref, tstart_ref, tgran_ref, misc_ref,
                     h_ref, srow_ref, xs_hbm, stage, zbuf, sem, zsem, *, n_blocks):
    t = pl.program_id(0)
    slot = t % 2
    tm = h_ref.shape[0]
    rows = stage.shape[1]
    h = h_ref[...]
    srows = [srow_ref[k:k + 1, :] for k in range(TOP_K)]
    chunk = 2 * LANES
    row_id = lax.broadcasted_iota(jnp.int32, (chunk, tm), 0).astype(F32)
    for c in range(rows // chunk):
        perm = jnp.zeros((chunk, tm), F32)
        for srow in srows:
            perm = jnp.where(row_id == srow - float(c * chunk), 1.0, perm)
        stage[slot, c * chunk:(c + 1) * chunk, :] = jnp.dot(
            perm.astype(BF16), h, preferred_element_type=F32).astype(stage.dtype)

    def copy_from(s):
        def copy(stage_row, sorted_row, nrows):
            return pltpu.make_async_copy(stage.at[s, pl.ds(stage_row, nrows)],
                                         xs_hbm.at[pl.ds(sorted_row, nrows)], sem.at[s])
        return copy

    _granule_copies(t, gdst_ref, totg_ref, copy_from(slot), start=True)

    @pl.when(t > 0)
    def _():
        _granule_copies(jnp.maximum(t - 1, 0), gdst_ref, totg_ref, copy_from(1 - slot), start=False)

    @pl.when(t == pl.num_programs(0) - 1)
    def _():
        _granule_copies(t, gdst_ref, totg_ref, copy_from(slot), start=False)
        tb = zbuf.shape[0]
        n_used = misc_ref[0]
        zbuf[...] = jnp.zeros_like(zbuf)

        def zero_gran(row):
            return pltpu.make_async_copy(zbuf.at[pl.ds(0, MOE_GRAN)], xs_hbm.at[pl.ds(row, MOE_GRAN)], zsem)

        def zero_block(blk):
            return pltpu.make_async_copy(zbuf, xs_hbm.at[pl.ds(pl.multiple_of(blk * tb, tb), tb)], zsem)

        def tails(e, c):
            def one(g, c2):
                zero_gran(pl.multiple_of(tstart_ref[e] + g * MOE_GRAN, MOE_GRAN)).start()
                return c2
            lax.fori_loop(0, tgran_ref[e], one, 0)
            return c

        def start_block(blk, c):
            zero_block(blk).start()
            return c

        def wait_gran(g, c):
            zero_gran(0).wait()
            return c

        def wait_block(blk, c):
            zero_block(0).wait()
            return c

        lax.fori_loop(0, N_EXPERTS, tails, 0)
        lax.fori_loop(n_used, n_blocks, start_block, 0)
        lax.fori_loop(0, misc_ref[1], wait_gran, 0)
        lax.fori_loop(n_used, n_blocks, wait_block, 0)


def _dispatch(h2, srow_t, plan, tm):
    n, d = h2.shape
    nt = n // tm
    rows = _stage_rows(tm)
    misc = jnp.concatenate([plan["n_used"], plan["tot_tail"]])
    grid_spec = pltpu.PrefetchScalarGridSpec(
        num_scalar_prefetch=5,
        grid=(nt,),
        in_specs=[
            pl.BlockSpec((tm, d), lambda t, *_: (t, 0)),
            pl.BlockSpec((8, tm), lambda t, *_: (0, t)),
        ],
        out_specs=pl.BlockSpec(memory_space=pl.ANY),
        scratch_shapes=[pltpu.VMEM((2, rows, d), BF16), pltpu.VMEM((MOE_BLOCK, d), BF16),
                        pltpu.SemaphoreType.DMA((2,)), pltpu.SemaphoreType.DMA(())],
    )
    return pl.pallas_call(
        functools.partial(_dispatch_kernel, n_blocks=plan["nb"]),
        grid_spec=grid_spec,
        out_shape=jax.ShapeDtypeStruct((plan["nb"] * MOE_BLOCK, d), BF16),
        compiler_params=_cparams(("arbitrary",)),
    )(plan["gran_dst"], plan["tot_gran"], plan["tail_start"], plan["tail_gran"], misc, h2, srow_t)


def _expert_kernel(be_ref, nused_ref, nexte_ref, x_ref, w1_hbm, b1_ref, w2_hbm, b2_ref, o_ref,
                   w1f, w2f, w1b, w2b, sem):
    de = w2f.shape[0]
    j = pl.program_id(0)
    live = j < nused_ref[0]
    e = be_ref[j]
    new_expert = jnp.logical_or(j == 0, e != be_ref[jnp.maximum(j - 1, 0)])

    def fetch(ex):
        return (pltpu.make_async_copy(w1_hbm.at[ex], w1f, sem.at[0]),
                pltpu.make_async_copy(w2_hbm.at[ex], w2f, sem.at[1]))

    @pl.when(jnp.logical_and(live, j == 0))
    def _():
        for c in fetch(e):
            c.start()

    @pl.when(jnp.logical_and(live, new_expert))
    def _():
        for c in fetch(e):
            c.wait()
        w1b[...] = w1f[...].astype(BF16)
        w2b[...] = w2f[...].astype(BF16)
        nxt = nexte_ref[e]

        @pl.when(nxt != e)
        def _():
            for c in fetch(nxt):
                c.start()

    @pl.when(live)
    def _():
        hm = jnp.dot(x_ref[...], w1b[...], preferred_element_type=F32) + b1_ref[0]
        gate = jnp.minimum(hm[:, :de], SWIGLU_LIMIT)
        up = jnp.clip(hm[:, de:], -SWIGLU_LIMIT, SWIGLU_LIMIT)
        act = gate * _sigmoid(SWIGLU_ALPHA * gate) * (up + 1.0)
        y = jnp.dot(act.astype(BF16), w2b[...], preferred_element_type=F32) + b2_ref[0]
        o_ref[...] = y.astype(o_ref.dtype)

    @pl.when(jnp.logical_not(live))
    def _():
        o_ref[...] = jnp.zeros_like(o_ref)


def _experts(xs, plan, w1, b1, w2, b2):
    d = xs.shape[1]
    tb = MOE_BLOCK
    f2 = w1.shape[2]
    de = w2.shape[1]
    last = lambda j, nu: jnp.maximum(jnp.minimum(j, nu[0] - 1), 0)
    grid_spec = pltpu.PrefetchScalarGridSpec(
        num_scalar_prefetch=3,
        grid=(plan["nb"],),
        in_specs=[
            pl.BlockSpec((tb, d), lambda j, be, nu, ne: (last(j, nu), 0)),
            pl.BlockSpec(memory_space=pl.ANY),
            pl.BlockSpec((1, 1, f2), lambda j, be, nu, ne: (be[j], 0, 0)),
            pl.BlockSpec(memory_space=pl.ANY),
            pl.BlockSpec((1, 1, d), lambda j, be, nu, ne: (be[j], 0, 0)),
        ],
        out_specs=pl.BlockSpec((tb, d), lambda j, be, nu, ne: (j, 0)),
        scratch_shapes=[pltpu.VMEM((d, f2), F32), pltpu.VMEM((de, d), F32),
                        pltpu.VMEM((d, f2), BF16), pltpu.VMEM((de, d), BF16),
                        pltpu.SemaphoreType.DMA((2,))],
    )
    return pl.pallas_call(
        _expert_kernel,
        grid_spec=grid_spec,
        out_shape=jax.ShapeDtypeStruct(xs.shape, BF16),
        compiler_params=_cparams(("arbitrary",)),
    )(plan["block_e"], plan["n_used"], plan["next_expert"], xs, w1, b1, w2, b2)


def _combine_kernel(gdst_ref, totg_ref, x1_ref, srow_ref, wt_ref, g_ref, yb_hbm, o_ref, stage, sem):
    t = pl.program_id(0)
    slot = t % 2
    tm = x1_ref.shape[0]
    rows = stage.shape[1]

    def copy_into(s):
        def copy(stage_row, sorted_row, nrows):
            return pltpu.make_async_copy(yb_hbm.at[pl.ds(sorted_row, nrows)],
                                         stage.at[s, pl.ds(stage_row, nrows)], sem.at[s])
        return copy

    @pl.when(t == 0)
    def _():
        stage[...] = jnp.zeros_like(stage)
        _granule_copies(t, gdst_ref, totg_ref, copy_into(slot), start=True)

    @pl.when(t + 1 < pl.num_programs(0))
    def _():
        _granule_copies(t + 1, gdst_ref, totg_ref, copy_into(1 - slot), start=True)

    wt = wt_ref[...]
    srows = [srow_ref[:, k:k + 1] for k in range(TOP_K)]
    chunk = LANES
    row_id = lax.broadcasted_iota(jnp.int32, (chunk, rows), 1).astype(F32)
    for c in range(tm // chunk):
        sl = slice(c * chunk, (c + 1) * chunk)
        unsort = jnp.zeros((chunk, rows), F32)
        for k, srow in enumerate(srows):
            unsort = jnp.where(row_id == srow[sl], wt[sl, k:k + 1], unsort)
        if c == 0:
            _granule_copies(t, gdst_ref, totg_ref, copy_into(slot), start=False)
        y = jnp.dot(unsort.astype(BF16), stage[slot], preferred_element_type=F32)
        o_ref[sl, :] = _rms(x1_ref[sl, :] + y, g_ref[...])


def _combine(x1, yb, srow, top_w, plan, g_final, tm):
    n, d = x1.shape
    nt = n // tm
    rows = _stage_rows(tm)
    grid_spec = pltpu.PrefetchScalarGridSpec(
        num_scalar_prefetch=2,
        grid=(nt,),
        in_specs=[
            pl.BlockSpec((tm, d), lambda t, *_: (t, 0)),
            pl.BlockSpec((tm, LANES), lambda t, *_: (t, 0)),
            pl.BlockSpec((tm, LANES), lambda t, *_: (t, 0)),
            pl.BlockSpec((1, d), lambda t, *_: (0, 0)),
            pl.BlockSpec(memory_space=pl.ANY),
        ],
        out_specs=pl.BlockSpec((tm, d), lambda t, *_: (t, 0)),
        scratch_shapes=[pltpu.VMEM((2, rows, d), BF16), pltpu.SemaphoreType.DMA((2,))],
    )
    return pl.pallas_call(
        _combine_kernel,
        grid_spec=grid_spec,
        out_shape=jax.ShapeDtypeStruct((n, d), F32),
        compiler_params=_cparams(("arbitrary",)),
    )(plan["gran_dst"], plan["tot_gran"], x1, srow, top_w, g_final, yb)


def _rope_tables(seq):
    half = ATTN_HEAD_DIM // 2
    inv_freq = np.float32(ROPE_THETA) ** (-np.arange(0, half, 2, dtype=np.float32) / np.float32(half))
    ang_row = np.arange(seq // GRID_W, dtype=np.float32)[:, None] * inv_freq
    ang_col = np.arange(GRID_W, dtype=np.float32)[:, None] * inv_freq
    by_row = lambda t: jnp.repeat(jnp.asarray(t, F32), GRID_W, axis=0)
    by_col = lambda t: jnp.tile(jnp.asarray(t, F32), (seq // GRID_W, 1))
    cos_r, sin_r = by_row(np.cos(ang_row)), by_row(np.sin(ang_row))
    cos_c, sin_c = by_col(np.cos(ang_col)), by_col(np.sin(ang_col))
    cos_t = jnp.concatenate([cos_r, cos_c] * 2, axis=-1)
    sin_t = jnp.concatenate([-sin_r, -sin_c, sin_r, sin_c], axis=-1)
    return cos_t, sin_t


def _token_mixer(x2, batch, seq, g_mix, w_in, conv_w, conv_b, dt_bias_f, dt_bias_b, a_log_f, a_log_b, d_skip,
                 g_ssm, q_norm_g, k_norm_g, w_br_ssm, w_br_attn, w_out):
    n, d = x2.shape
    z_end = SSM_INNER
    xbc_end = z_end + CONV_CH
    dtf_end = xbc_end + SSM_HEADS
    dtb_end = dtf_end + SSM_HEADS
    q_end = dtb_end + ATTN_HEADS * ATTN_HEAD_DIM
    k_end = q_end + ATTN_KV_HEADS * ATTN_HEAD_DIM
    v_end = k_end + ATTN_KV_HEADS * ATTN_HEAD_DIM
    head_cols = lambda w: _rope_head_order(w.reshape(d, -1, ATTN_HEAD_DIM)).reshape(d, -1)
    w_main = jnp.concatenate([w_in[:, :z_end], head_cols(w_in[:, dtb_end:q_end]), w_in[:, v_end:],
                              w_in[:, z_end:xbc_end], head_cols(w_in[:, q_end:k_end]), w_in[:, k_end:v_end]],
                             axis=1).astype(BF16)
    w_dt = jnp.pad(w_in[:, xbc_end:dtb_end], ((0, 0), (0, LANES - 2 * SSM_HEADS))).astype(BF16)

    proj, dt = _in_proj(x2, g_mix.reshape(1, d), w_main, w_dt)
    proj3 = proj.reshape(batch, seq, PROJ_COLS)

    xbc = _conv(proj3, conv_w, conv_b.reshape(1, CONV_CH))

    dt3 = dt.reshape(batch, seq, LANES)
    dtt3 = jnp.swapaxes(dt3[:, :, :2 * SSM_HEADS], 1, 2)
    bias = jnp.concatenate([dt_bias_f, dt_bias_b])
    alog = jnp.concatenate([a_log_f, a_log_b])
    pad_row = lambda v: jnp.pad(v, (0, LANES - 2 * SSM_HEADS)).reshape(1, LANES)
    y_f, y_b = _ssd(xbc, dt3, dtt3, pad_row(bias), bias.reshape(-1, 1), pad_row(alog), alog.reshape(-1, 1))

    cos_t, sin_t = _rope_tables(seq)
    q_rot, k_rot = _qk_prep(proj, cos_t, sin_t, _rope_head_order(q_norm_g).reshape(1, -1),
                            _rope_head_order(k_norm_g).reshape(1, -1), seq)
    vt3 = jnp.swapaxes(proj3[:, :, V_OFF:V_OFF + ATTN_KV_HEADS * ATTN_HEAD_DIM], 1, 2)
    attn = _flash(q_rot.reshape(batch, seq, -1), k_rot.reshape(batch, seq, -1), vt3, q_norm_g, k_norm_g)

    return _merge(y_f.reshape(n, -1), y_b.reshape(n, -1), xbc.reshape(n, CONV_CH), proj, attn.reshape(n, -1),
                  x2, jnp.repeat(d_skip, SSM_HEAD_DIM).reshape(1, -1), g_ssm.reshape(1, -1),
                  w_br_ssm.astype(BF16), w_br_attn.astype(BF16), w_out.astype(BF16))


def _moe_and_final_norm(x1, g_ffn, w_router, b_router, w_mlp1, b_mlp1, w_mlp2, b_mlp2, g_final):
    n, d = x1.shape
    tm = min(MOE_TILE, n)
    w_t = w_router.T
    w_hi = w_t.astype(BF16)
    w_lo = (w_t - w_hi.astype(F32)).astype(BF16)
    g_ffn_row = g_ffn.reshape(1, d)
    h2, srow_t, wt_t, cnt = _router(x1, g_ffn_row, jnp.concatenate([w_hi, w_lo, w_hi], axis=1),
                                    b_router.reshape(N_EXPERTS, 1))
    plan = _routing_plan(cnt[:, :, 0].astype(jnp.int32), n, tm, MOE_BLOCK)
    lanes_0_3 = lambda a: jnp.pad(a[:TOP_K].T, ((0, 0), (0, LANES - TOP_K)))
    xs = _dispatch(h2, srow_t, plan, tm)
    yb = _experts(xs, plan, w_mlp1, b_mlp1[:, None, :], w_mlp2, b_mlp2[:, None, :])
    return _combine(x1, yb, lanes_0_3(srow_t), lanes_0_3(wt_t), plan, g_final.reshape(1, d), tm)


def kernel(x, g_mix, w_in, conv_w, conv_b, dt_bias_f, dt_bias_b, a_log_f, a_log_b, d_skip, g_ssm, q_norm_g,
           k_norm_g, w_br_ssm, w_br_attn, w_out, g_ffn, w_router, b_router, w_mlp1, b_mlp1, w_mlp2, b_mlp2,
           g_final):
    batch, seq, d = x.shape
    assert g_mix.shape[0] == 1, "single-layer model: the final rmsnorm is fused into the MoE combine"
    x2 = x.reshape(batch * seq, d)
    x1 = _token_mixer(x2, batch, seq, g_mix[0], w_in[0], conv_w[0], conv_b[0], dt_bias_f[0], dt_bias_b[0],
                      a_log_f[0], a_log_b[0], d_skip[0], g_ssm[0], q_norm_g[0], k_norm_g[0], w_br_ssm[0],
                      w_br_attn[0], w_out[0])
    out = _moe_and_final_norm(x1, g_ffn[0], w_router[0], b_router[0], w_mlp1[0], b_mlp1[0], w_mlp2[0],
                              b_mlp2[0], g_final)
    return out.reshape(batch, seq, d)
```

```python
import functools
import math

import jax
import jax.numpy as jnp
import numpy as np
from jax import lax
from jax.experimental import pallas as pl
from jax.experimental.pallas import tpu as pltpu

F32 = jnp.float32
BF16 = jnp.bfloat16

NORM_EPS = 1e-6
GRID_W = 64
SSM_HEADS = 16
SSM_HEAD_DIM = 64
SSM_INNER = SSM_HEADS * SSM_HEAD_DIM
SSM_GROUPS = 2
SSM_STATE = 128
SSM_CONV = 5
CONV_CH = SSM_INNER + 2 * SSM_GROUPS * SSM_STATE
ATTN_HEADS = 8
ATTN_KV_HEADS = 2
ATTN_HEAD_DIM = 128
ROPE_THETA = 10000.0
N_EXPERTS = 32
TOP_K = 4
SWIGLU_LIMIT = 7.0
SWIGLU_ALPHA = 1.702

LANES = 128
BF16_SUBLANES = 16
VMEM_LIMIT = 56 * 1024 * 1024

Z_OFF, Q_OFF, GATE_OFF, XBC_OFF = 0, 1024, 2048, 4096
K_OFF, V_OFF, PROJ_COLS = 5632, 5888, 6144

MOE_TILE = 512
MOE_BLOCK = 512
MOE_GRAN = BF16_SUBLANES


def _cparams(sem):
    return pltpu.CompilerParams(dimension_semantics=sem, vmem_limit_bytes=VMEM_LIMIT)


def _sigmoid(x):
    return 1.0 / (1.0 + jnp.exp(-x))


def _softplus(x):
    return jnp.maximum(x, 0.0) + jnp.log(1.0 + jnp.exp(-jnp.abs(x)))


def _rms(x, g):
    ms = jnp.mean(x * x, axis=-1, keepdims=True)
    return x * lax.rsqrt(ms + NORM_EPS) * g


def _inproj_kernel(x_ref, g_ref, w_ref, wdt_ref, o_ref, dt_ref, h_scr):
    @pl.when(pl.program_id(1) == 0)
    def _():
        hb = _rms(x_ref[...], g_ref[...]).astype(BF16)
        h_scr[...] = hb
        dt_ref[...] = jnp.dot(hb, wdt_ref[...], preferred_element_type=F32)

    o_ref[...] = jnp.dot(h_scr[...], w_ref[...], preferred_element_type=F32).astype(o_ref.dtype)


def _in_proj(x2, g_mix, w_main, w_dt):
    n, d = x2.shape
    tm = min(1024, n)
    tn = 2048
    return pl.pallas_call(
        _inproj_kernel,
        grid=(n // tm, PROJ_COLS // tn),
        in_specs=[
            pl.BlockSpec((tm, d), lambda i, j: (i, 0)),
            pl.BlockSpec((1, d), lambda i, j: (0, 0)),
            pl.BlockSpec((d, tn), lambda i, j: (0, j)),
            pl.BlockSpec((d, LANES), lambda i, j: (0, 0)),
        ],
        out_specs=[
            pl.BlockSpec((tm, tn), lambda i, j: (i, j)),
            pl.BlockSpec((tm, LANES), lambda i, j: (i, 0)),
        ],
        out_shape=[
            jax.ShapeDtypeStruct((n, PROJ_COLS), BF16),
            jax.ShapeDtypeStruct((n, LANES), F32),
        ],
        scratch_shapes=[pltpu.VMEM((tm, d), BF16)],
        compiler_params=_cparams(("arbitrary", "arbitrary")),
    )(x2, g_mix, w_main, w_dt)


CONV_HALO = 64
CONV_ROWS = 128


def _conv_kernel(prev_ref, cur_ref, next_ref, shift_ref, w_ref, b_ref, o_ref):
    s = pl.program_id(1)
    ts = cur_ref.shape[1]
    prev = prev_ref[0]
    nxt = next_ref[0]
    zero = jnp.zeros_like(prev)
    ext = jnp.concatenate([jnp.where(s == 0, zero, prev), cur_ref[0],
                           jnp.where(s == pl.num_programs(1) - 1, zero, nxt)], axis=0)
    shift = shift_ref[...]
    for rb in range(ts // CONV_ROWS):
        lo = rb * CONV_ROWS
        taps = jnp.dot(shift, ext[lo:lo + CONV_ROWS + 2 * CONV_HALO], preferred_element_type=F32)
        acc = b_ref[...] + w_ref[0:1, :] * taps[0:CONV_ROWS]
        for k in range(1, SSM_CONV):
            acc = acc + w_ref[k:k + 1, :] * taps[k * CONV_ROWS:(k + 1) * CONV_ROWS]
        o_ref[0, lo:lo + CONV_ROWS, :] = (acc * _sigmoid(acc)).astype(o_ref.dtype)


def _conv(proj3, conv_w, conv_b):
    b, s, _ = proj3.shape
    ts = min(1024, s)
    tc = 512
    halo = CONV_HALO
    hb = ts // halo
    col0 = XBC_OFF // tc
    pad = (SSM_CONV - 1) // 2
    out_row = np.arange(SSM_CONV * CONV_ROWS)
    src = out_row % CONV_ROWS + halo + out_row // CONV_ROWS - pad
    shift = jnp.asarray(src[:, None] == np.arange(CONV_ROWS + 2 * halo)[None, :], BF16)
    return pl.pallas_call(
        _conv_kernel,
        grid=(b, s // ts, CONV_CH // tc),
        in_specs=[
            pl.BlockSpec((1, halo, tc), lambda bi, si, ci: (bi, jnp.maximum(si * hb - 1, 0), col0 + ci)),
            pl.BlockSpec((1, ts, tc), lambda bi, si, ci: (bi, si, col0 + ci)),
            pl.BlockSpec((1, halo, tc),
                         lambda bi, si, ci: (bi, jnp.minimum((si + 1) * hb, s // halo - 1), col0 + ci)),
            pl.BlockSpec(shift.shape, lambda bi, si, ci: (0, 0)),
            pl.BlockSpec((SSM_CONV, tc), lambda bi, si, ci: (0, ci)),
            pl.BlockSpec((1, tc), lambda bi, si, ci: (0, ci)),
        ],
        out_specs=pl.BlockSpec((1, ts, tc), lambda bi, si, ci: (bi, si, ci)),
        out_shape=jax.ShapeDtypeStruct((b, s, CONV_CH), BF16),
        compiler_params=_cparams(("arbitrary", "arbitrary", "arbitrary")),
    )(proj3, proj3, proj3, shift, conv_w, conv_b)


SSM_CHUNKS_PER_STEP = 4


def _ssd_kernel(xf_ref, xb_ref, dtf_ref, dtb_ref, dttf_ref, dttb_ref, brow_ref, bcol_ref,
                arow_ref, acol_ref, yf_ref, yb_ref, st_ref):
    L = SSM_STATE
    nsub = xf_ref.shape[1] // L
    hg = SSM_HEADS // SSM_GROUPS
    pairs = hg // 2

    @pl.when(pl.program_id(1) == 0)
    def _():
        st_ref[...] = jnp.zeros_like(st_ref)

    rows = lax.broadcasted_iota(jnp.int32, (L, L), 0)
    cols = lax.broadcasted_iota(jnp.int32, (L, L), 1)
    lower = rows >= cols
    upper = rows <= cols
    ltri = lower.astype(BF16)
    utri = upper.astype(BF16)
    lane = lax.broadcasted_iota(jnp.int32, (L, LANES), 1)
    first_half = lane < SSM_HEAD_DIM
    lane1 = lax.broadcasted_iota(jnp.int32, (1, LANES), 1)
    log2e = math.log2(math.e)
    a_row = -jnp.exp(arow_ref[...]) * log2e
    a_col = -jnp.exp(acol_ref[...]) * log2e

    def split3(v):
        hi = v.astype(BF16)
        r1 = v - hi.astype(F32)
        mid = r1.astype(BF16)
        return hi, mid, (r1 - mid.astype(F32)).astype(BF16)

    def cumsum_cols(tri, v):
        return jnp.dot(jnp.concatenate([tri] * 3, axis=1), jnp.concatenate(split3(v), axis=0),
                       preferred_element_type=F32)

    def cumsum_rows(v, tri):
        return jnp.dot(jnp.concatenate(split3(v), axis=1), jnp.concatenate([tri] * 3, axis=0),
                       preferred_element_type=F32)

    for sub, d in [(sub, d) for sub in range(nsub) for d in range(2)]:
        x_ref, dt_ref, dtt_ref, y_ref = ((xf_ref, dtf_ref, dttf_ref, yf_ref) if d == 0
                                         else (xb_ref, dtb_ref, dttb_ref, yb_ref))
        r0 = (sub if d == 0 else nsub - 1 - sub) * L
        rs = slice(r0, r0 + L)
        a = _softplus(dt_ref[0, rs, :] + brow_ref[...]) * a_row
        dt_t = _softplus(dtt_ref[0, :, rs] + bcol_ref[...])
        a_t = dt_t * a_col
        if d == 0:
            cs_col = cumsum_cols(ltri, a)
            cs_row = cumsum_rows(a_t, utri)
            tot = cs_col[L - 1:L, :]
            tot_t = cs_row[:, L - 1:L]
            mask = lower
        else:
            cs_col = cumsum_cols(utri, a)
            cs_row = cumsum_rows(a_t, ltri)
            tot = cs_col[0:1, :]
            tot_t = cs_row[:, 0:1]
            mask = upper
        w_t = dt_t * jnp.exp2(tot_t - cs_row)
        src_t = cs_row - jnp.log2(dt_t)
        chunk_decay = jnp.exp2(tot)

        for g in range(SSM_GROUPS):
            boff = SSM_INNER + g * SSM_STATE
            coff = SSM_INNER + SSM_GROUPS * SSM_STATE + g * SSM_STATE
            bm = x_ref[0, rs, boff:boff + SSM_STATE]
            cm = x_ref[0, rs, coff:coff + SSM_STATE]
            cb = lax.dot_general(cm, bm, (((1,), (1,)), ((), ())), preferred_element_type=F32)
            bt = bm.astype(F32).T
            st = st_ref[d, g]
            y_off = jnp.dot(cm, st.astype(BF16), preferred_element_type=F32)
            for pr in range(pairs):
                h0 = d * SSM_HEADS + g * hg + 2 * pr
                xoff = (g * pairs + pr) * LANES
                xs = x_ref[0, rs, xoff:xoff + LANES]
                zero = jnp.zeros_like(xs)
                rhs = jnp.concatenate([jnp.where(first_half, xs, zero),
                                       jnp.where(first_half, zero, xs)], axis=0)
                ms, ws, dins = [], [], []
                for hh in (h0, h0 + 1):
                    cs_b = jnp.broadcast_to(cs_col[:, hh:hh + 1], (L, L))
                    seg = cs_b - src_t[hh:hh + 1, :]
                    m = cb * jnp.exp2(jnp.where(mask, seg, -jnp.inf))
                    ms.append(m.astype(BF16))
                    ws.append((bt * w_t[hh:hh + 1, :]).astype(BF16))
                    dins.append(jnp.exp2(cs_b))
                y = jnp.dot(jnp.concatenate(ms, axis=1), rhs, preferred_element_type=F32)
                y = y + y_off[:, pr * LANES:(pr + 1) * LANES] * jnp.where(first_half, dins[0], dins[1])
                y_ref[0, rs, xoff:xoff + LANES] = y.astype(y_ref.dtype)
                new_st = jnp.dot(jnp.concatenate(ws, axis=1), rhs, preferred_element_type=F32)
                cd = jnp.where(lane1 < SSM_HEAD_DIM, chunk_decay[:, h0:h0 + 1], chunk_decay[:, h0 + 1:h0 + 2])
                st_ref[d, g, :, pr * LANES:(pr + 1) * LANES] = st[:, pr * LANES:(pr + 1) * LANES] * cd + new_st


def _ssd(xbc, dt3, dtt3, bias_row, bias_col, alog_row, alog_col):
    b, s, _ = xbc.shape
    L = SSM_CHUNKS_PER_STEP * SSM_STATE
    nc = s // L
    hg = SSM_HEADS // SSM_GROUPS
    fwd = lambda bi, ci: (bi, ci, 0)
    bwd = lambda bi, ci: (bi, nc - 1 - ci, 0)
    fwd_t = lambda bi, ci: (bi, 0, ci)
    bwd_t = lambda bi, ci: (bi, 0, nc - 1 - ci)
    const = lambda bi, ci: (0, 0)
    return pl.pallas_call(
        _ssd_kernel,
        grid=(b, nc),
        in_specs=[
            pl.BlockSpec((1, L, CONV_CH), fwd),
            pl.BlockSpec((1, L, CONV_CH), bwd),
            pl.BlockSpec((1, L, LANES), fwd),
            pl.BlockSpec((1, L, LANES), bwd),
            pl.BlockSpec((1, 2 * SSM_HEADS, L), fwd_t),
            pl.BlockSpec((1, 2 * SSM_HEADS, L), bwd_t),
            pl.BlockSpec((1, LANES), const),
            pl.BlockSpec((2 * SSM_HEADS, 1), const),
            pl.BlockSpec((1, LANES), const),
            pl.BlockSpec((2 * SSM_HEADS, 1), const),
        ],
        out_specs=[
            pl.BlockSpec((1, L, SSM_INNER), fwd),
            pl.BlockSpec((1, L, SSM_INNER), bwd),
        ],
        out_shape=[jax.ShapeDtypeStruct((b, s, SSM_INNER), BF16)] * 2,
        scratch_shapes=[pltpu.VMEM((2, SSM_GROUPS, SSM_STATE, hg * SSM_HEAD_DIM), F32)],
        compiler_params=_cparams(("arbitrary", "arbitrary")),
    )(xbc, xbc, dt3, dt3, dtt3, dtt3, bias_row, bias_col, alog_row, alog_col)


def _rope_head_order(a):
    q4 = ATTN_HEAD_DIM // 4
    return jnp.concatenate([a[..., 0:q4], a[..., 2 * q4:3 * q4], a[..., q4:2 * q4], a[..., 3 * q4:]], axis=-1)


def _rope_norm(t, g, cos, sin_signed, ones):
    sq = t * t
    hi = sq.astype(BF16)
    lo = (sq - hi.astype(F32)).astype(BF16)
    ms = jnp.dot(jnp.concatenate([hi, lo], axis=1), ones, preferred_element_type=F32) * (1.0 / ATTN_HEAD_DIM)
    tn = t * lax.rsqrt(ms + NORM_EPS) * g
    return tn * cos + pltpu.roll(tn, ATTN_HEAD_DIM // 2, 1) * sin_signed


Q_SCALE = ATTN_HEAD_DIM ** -0.5 * math.log2(math.e)


def _qkprep_kernel(q_ref, k_ref, cos_ref, sin_ref, qg_ref, kg_ref, qo_ref, ko_ref):
    cos = cos_ref[...]
    sin = sin_ref[...]
    ones = jnp.ones((2 * ATTN_HEAD_DIM, ATTN_HEAD_DIM), BF16)

    def heads(src_ref, g_ref, dst_ref, n_heads, scale):
        for h in range(n_heads):
            sl = slice(h * ATTN_HEAD_DIM, (h + 1) * ATTN_HEAD_DIM)
            r = _rope_norm(src_ref[:, sl].astype(F32), g_ref[...], cos, sin, ones) * scale
            dst_ref[:, sl] = r.astype(dst_ref.dtype)

    heads(q_ref, qg_ref, qo_ref, ATTN_HEADS, Q_SCALE)
    heads(k_ref, kg_ref, ko_ref, ATTN_KV_HEADS, 1.0)


def _qk_prep(proj, cos_t, sin_t, q_norm_g, k_norm_g, seq):
    n = proj.shape[0]
    tm = min(1024, seq)
    qw = ATTN_HEADS * ATTN_HEAD_DIM
    kw = ATTN_KV_HEADS * ATTN_HEAD_DIM
    spt = seq // tm
    return pl.pallas_call(
        _qkprep_kernel,
        grid=(n // tm,),
        in_specs=[
            pl.BlockSpec((tm, qw), lambda i: (i, Q_OFF // qw)),
            pl.BlockSpec((tm, kw), lambda i: (i, K_OFF // kw)),
            pl.BlockSpec((tm, ATTN_HEAD_DIM), lambda i: (i % spt, 0)),
            pl.BlockSpec((tm, ATTN_HEAD_DIM), lambda i: (i % spt, 0)),
            pl.BlockSpec((1, ATTN_HEAD_DIM), lambda i: (0, 0)),
            pl.BlockSpec((1, ATTN_HEAD_DIM), lambda i: (0, 0)),
        ],
        out_specs=[
            pl.BlockSpec((tm, qw), lambda i: (i, 0)),
            pl.BlockSpec((tm, kw), lambda i: (i, 0)),
        ],
        out_shape=[jax.ShapeDtypeStruct((n, qw), BF16), jax.ShapeDtypeStruct((n, kw), BF16)],
        compiler_params=_cparams(("arbitrary",)),
    )(proj, proj, cos_t, sin_t, q_norm_g, k_norm_g)


def _flash_kernel(small_ref, q_ref, k_ref, vt_ref, o_ref, *, tk, th):
    q = q_ref[0]
    tq = q.shape[0]
    seq = k_ref.shape[1]
    nt = (((1,), (1,)), ((), ()))
    small = small_ref[0] != 0

    @pl.when(small)
    def _():
        l = acc_t = None
        for c in range(seq // th):
            ks = slice(c * th, (c + 1) * th)
            p_t = jnp.exp2(lax.dot_general(k_ref[0, ks, :], q, nt, preferred_element_type=F32))
            l_c = jnp.sum(p_t, axis=0, keepdims=True)
            a_c = jnp.dot(vt_ref[0, :, ks], p_t.astype(BF16), preferred_element_type=F32)
            l = l_c if l is None else l + l_c
            acc_t = a_c if acc_t is None else acc_t + a_c
        o_ref[0] = (acc_t / l).T.astype(o_ref.dtype)

    @pl.when(jnp.logical_not(small))
    def _():
        def body(i, carry):
            m, l, acc_t = carry
            off = pl.multiple_of(i * tk, tk)
            s_t = lax.dot_general(k_ref[0, pl.ds(off, tk), :], q, nt, preferred_element_type=F32)
            m_new = jnp.maximum(m, jnp.max(s_t, axis=0, keepdims=True))
            alpha = jnp.exp2(m - m_new)
            p_t = jnp.exp2(s_t - m_new)
            l = alpha * l + jnp.sum(p_t, axis=0, keepdims=True)
            acc_t = alpha * acc_t + jnp.dot(vt_ref[0, :, pl.ds(off, tk)], p_t.astype(BF16),
                                            preferred_element_type=F32)
            return m_new, l, acc_t

        init = (jnp.full((1, tq), -jnp.inf, F32), jnp.zeros((1, tq), F32),
                jnp.zeros((ATTN_HEAD_DIM, tq), F32))
        _, l, acc_t = lax.fori_loop(0, seq // tk, body, init)
        o_ref[0] = (acc_t / l).T.astype(o_ref.dtype)


SCORE_BOUND = 59.0


def _flash(q3, k3, vt3, q_norm_g, k_norm_g):
    b, s, _ = q3.shape
    tq = min(1024, s)
    tk = min(2048, s)
    th = min(4096, s)
    nq = s // tq
    grp = ATTN_HEADS // ATTN_KV_HEADS
    hd = ATTN_HEAD_DIM
    bound = hd * Q_SCALE * jnp.max(jnp.abs(q_norm_g)) * jnp.max(jnp.abs(k_norm_g)) * 1.02
    small = (bound <= SCORE_BOUND).astype(jnp.int32).reshape(1)
    grid_spec = pltpu.PrefetchScalarGridSpec(
        num_scalar_prefetch=1,
        grid=(b, ATTN_HEADS, nq),
        in_specs=[
            pl.BlockSpec((1, tq, hd), lambda bi, h, qi, sm: (bi, qi, h)),
            pl.BlockSpec((1, s, hd), lambda bi, h, qi, sm: (bi, 0, h // grp)),
            pl.BlockSpec((1, hd, s), lambda bi, h, qi, sm: (bi, h // grp, 0)),
        ],
        out_specs=pl.BlockSpec((1, tq, hd), lambda bi, h, qi, sm: (bi, qi, h)),
    )
    return pl.pallas_call(
        functools.partial(_flash_kernel, tk=tk, th=th),
        grid_spec=grid_spec,
        out_shape=jax.ShapeDtypeStruct((b, s, ATTN_HEADS * hd), BF16),
        compiler_params=_cparams(("arbitrary", "arbitrary", "arbitrary")),
    )(small, q3, k3, vt3)


def _merge_kernel(yf_ref, yb_ref, xs_ref, z_ref, gate_ref, attn_ref, x_ref, dskip_ref, gssm_ref,
                  wbs_ref, wba_ref, wo_ref, x1_ref):
    d = x_ref.shape[1]
    xs = xs_ref[...].astype(F32)
    y = yf_ref[...].astype(F32) + yb_ref[...].astype(F32) + xs * dskip_ref[...]
    z = z_ref[...].astype(F32)
    y = _rms(y * (z * _sigmoid(z)), gssm_ref[...])
    br_ssm = jnp.dot(y.astype(BF16), wbs_ref[...], preferred_element_type=F32)
    br_attn = jnp.dot(attn_ref[...], wba_ref[...], preferred_element_type=F32)
    g_s = _sigmoid(gate_ref[:, :d].astype(F32))
    g_a = _sigmoid(gate_ref[:, d:].astype(F32))
    merged = (g_s * br_ssm + g_a * br_attn).astype(BF16)
    x1_ref[...] = x_ref[...] + jnp.dot(merged, wo_ref[...], preferred_element_type=F32)


def _merge(y_f, y_b, xbc, proj, attn, x2, dskip_row, g_ssm, w_br_ssm, w_br_attn, w_out):
    n, d = x2.shape
    tm = min(512, n)
    row = lambda i: (i, 0)
    const = lambda i: (0, 0)
    return pl.pallas_call(
        _merge_kernel,
        grid=(n // tm,),
        in_specs=[
            pl.BlockSpec((tm, d), row),
            pl.BlockSpec((tm, d), row),
            pl.BlockSpec((tm, d), row),
            pl.BlockSpec((tm, d), lambda i: (i, Z_OFF // d)),
            pl.BlockSpec((tm, 2 * d), lambda i: (i, GATE_OFF // (2 * d))),
            pl.BlockSpec((tm, d), row),
            pl.BlockSpec((tm, d), row),
            pl.BlockSpec((1, d), const),
            pl.BlockSpec((1, d), const),
            pl.BlockSpec((d, d), const),
            pl.BlockSpec((d, d), const),
            pl.BlockSpec((d, d), const),
        ],
        out_specs=pl.BlockSpec((tm, d), row),
        out_shape=jax.ShapeDtypeStruct((n, d), F32),
        compiler_params=_cparams(("arbitrary",)),
    )(y_f, y_b, xbc, proj, proj, attn, x2, dskip_row, g_ssm, w_br_ssm, w_br_attn, w_out)


def _router_kernel(x1_ref, g_ref, w_ref, b_ref, h_ref, srow_ref, wt_ref, cnt_ref):
    tm = x1_ref.shape[0]
    h = _rms(x1_ref[...], g_ref[...])
    h_hi = h.astype(BF16)
    h_lo = (h - h_hi.astype(F32)).astype(BF16)
    h_ref[...] = h_hi
    logits = lax.dot_general(w_ref[...], jnp.concatenate([h_hi, h_hi, h_lo], axis=1), (((1,), (1,)), ((), ())),
                             preferred_element_type=F32) + b_ref[...]
    expert = lax.broadcasted_iota(jnp.int32, logits.shape, 0)
    slot = lax.broadcasted_iota(jnp.int32, wt_ref.shape, 0)
    chosen = jnp.zeros(logits.shape, F32)
    vals, hits = [], []
    for k in range(TOP_K):
        m = jnp.max(logits, axis=0, keepdims=True)
        idx = jnp.min(jnp.where(logits == m, expert, N_EXPERTS), axis=0, keepdims=True)
        vals.append(m)
        hit = expert == idx
        hits.append(hit)
        chosen = jnp.where(hit, 1.0, chosen)
        logits = jnp.where(hit, -jnp.inf, logits)
    es = [jnp.exp(v - vals[0]) for v in vals]
    tot = es[0] + es[1] + es[2] + es[3]

    chosen16 = chosen.astype(BF16)
    cnt_col = jnp.sum(chosen, axis=1, keepdims=True)
    cnt_row = lax.dot_general(jnp.ones((8, tm), BF16), chosen16, (((1,), (1,)), ((), ())),
                              preferred_element_type=F32)[0:1, :]
    seg_row = jnp.ceil(cnt_row * (1.0 / MOE_GRAN)) * MOE_GRAN
    before = (lax.broadcasted_iota(jnp.int32, (N_EXPERTS, N_EXPERTS), 1)
              < lax.broadcasted_iota(jnp.int32, (N_EXPERTS, N_EXPERTS), 0))
    seg_off = jnp.sum(jnp.where(before, seg_row, 0.0), axis=1, keepdims=True)
    earlier = (lax.broadcasted_iota(jnp.int32, (tm, tm), 0)
               < lax.broadcasted_iota(jnp.int32, (tm, tm), 1)).astype(BF16)
    pos = jnp.dot(chosen16, earlier, preferred_element_type=F32) + seg_off

    val_out = jnp.zeros(wt_ref.shape, F32)
    row_out = jnp.zeros(srow_ref.shape, F32)
    for k in range(TOP_K):
        val_out = jnp.where(slot == k, es[k] / tot, val_out)
        row_out = jnp.where(slot == k, jnp.sum(jnp.where(hits[k], pos, 0.0), axis=0, keepdims=True), row_out)
    wt_ref[...] = val_out
    srow_ref[...] = row_out
    cnt_ref[0] = jnp.broadcast_to(cnt_col, cnt_ref.shape[1:])


def _router(x1, g_ffn, w_router_t, b_router_col):
    n, d = x1.shape
    tm = min(MOE_TILE, n)
    return pl.pallas_call(
        _router_kernel,
        grid=(n // tm,),
        in_specs=[
            pl.BlockSpec((tm, d), lambda i: (i, 0)),
            pl.BlockSpec((1, d), lambda i: (0, 0)),
            pl.BlockSpec((N_EXPERTS, 3 * d), lambda i: (0, 0)),
            pl.BlockSpec((N_EXPERTS, 1), lambda i: (0, 0)),
        ],
        out_specs=[pl.BlockSpec((tm, d), lambda i: (i, 0)),
                   pl.BlockSpec((8, tm), lambda i: (0, i)),
                   pl.BlockSpec((8, tm), lambda i: (0, i)),
                   pl.BlockSpec((1, N_EXPERTS, LANES), lambda i: (i, 0, 0))],
        out_shape=[jax.ShapeDtypeStruct((n, d), BF16),
                   jax.ShapeDtypeStruct((8, n), F32),
                   jax.ShapeDtypeStruct((8, n), F32),
                   jax.ShapeDtypeStruct((n // tm, N_EXPERTS, LANES), F32)],
        compiler_params=_cparams(("arbitrary",)),
    )(x1, g_ffn, w_router_t, b_router_col)


def _routing_plan(cnt, n, tm, tb):
    nt = n // tm
    gran = MOE_GRAN
    rc = (cnt + gran - 1) // gran * gran
    covered = jnp.sum(rc, axis=0)
    region = (covered + tb - 1) // tb * tb
    pad_end = jnp.cumsum(region)
    pad_start = pad_end - region
    seg_start = pad_start[None, :] + jnp.cumsum(rc, axis=0) - rc
    n_used = pad_end[-1] // tb
    nb = (TOP_K * n + nt * N_EXPERTS * (gran - 1) + N_EXPERTS * (tb - 1) + tb - 1) // tb
    blk = jnp.arange(nb, dtype=jnp.int32)
    block_e = jnp.sum((pad_end[None, :] <= (jnp.minimum(blk, n_used - 1) * tb)[:, None]).astype(jnp.int32), axis=1)
    block_e = jnp.minimum(block_e, N_EXPERTS - 1)
    flat = lambda a: a.reshape(-1).astype(jnp.int32)
    tail_gran = (region - covered) // gran
    ngran = rc // gran
    g_end = jnp.cumsum(ngran, axis=1)
    gi = jnp.arange(_stage_rows(tm) // gran, dtype=jnp.int32)
    g_exp = jnp.minimum(jnp.sum((g_end[:, None, :] <= gi[None, :, None]).astype(jnp.int32), axis=2), N_EXPERTS - 1)
    mine = g_exp[:, :, None] == jnp.arange(N_EXPERTS, dtype=jnp.int32)[None, None, :]
    seg_base = seg_start - gran * (g_end - ngran)
    gran_dst = jnp.sum(jnp.where(mine, seg_base[:, None, :], 0), axis=2) + gran * gi[None, :]
    ids = jnp.arange(N_EXPERTS, dtype=jnp.int32)
    later = jnp.where((ids[None, :] > ids[:, None]) & (region[None, :] > 0), ids[None, :], N_EXPERTS)
    nxt = jnp.min(later, axis=1)
    next_expert = jnp.where(nxt < N_EXPERTS, nxt, ids)
    return dict(gran_dst=flat(gran_dst), tot_gran=flat(g_end[:, -1]), block_e=flat(block_e),
                next_expert=flat(next_expert),
                n_used=flat(n_used), nb=nb,
                tail_start=flat(pad_start + covered), tail_gran=flat(tail_gran),
                tot_tail=flat(jnp.sum(tail_gran)))


def _stage_rows(tm):
    rows = TOP_K * tm + N_EXPERTS * (MOE_GRAN - 1)
    return (rows + 2 * LANES - 1) // (2 * LANES) * (2 * LANES)


GRAN_UNROLL = 4


def _granule_copies(t, gdst_ref, totg_ref, make_copy, start):
    total = totg_ref[t]
    base = t * (gdst_ref.shape[0] // totg_ref.shape[0])
    full = lax.shift_right_logical(total, GRAN_UNROLL.bit_length() - 1)

    def one(i):
        make_copy(pl.multiple_of(i * MOE_GRAN, MOE_GRAN), pl.multiple_of(gdst_ref[base + i], MOE_GRAN),
                  MOE_GRAN).start()

    def group(q, c):
        if start:
            for u in range(GRAN_UNROLL):
                one(q * GRAN_UNROLL + u)
        else:
            make_copy(0, 0, GRAN_UNROLL * MOE_GRAN).wait()
        return c

    def rest(i, c):
        if start:
            one(i)
        else:
            make_copy(0, 0, MOE_GRAN).wait()
        return c

    lax.fori_loop(0, full, group, 0)
    lax.fori_loop(full * GRAN_UNROLL, total, rest, 0)


def _dispatch_kernel(gdst_ref, totg_ref, tstart_ref, tgran_ref, misc_ref,
                     h_ref, srow_ref, xs_hbm, stage, zbuf, sem, zsem, *, n_blocks):
    t = pl.program_id(0)
    slot = t % 2
    tm = h_ref.shape[0]
    rows = stage.shape[1]
    h = h_ref[...]
    srows = [srow_ref[k:k + 1, :] for k in range(TOP_K)]
    chunk = 2 * LANES
    row_id = lax.broadcasted_iota(jnp.int32, (chunk, tm), 0).astype(F32)
    for c in range(rows // chunk):
        perm = jnp.zeros((chunk, tm), F32)
        for srow in srows:
            perm = jnp.where(row_id == srow - float(c * chunk), 1.0, perm)
        stage[slot, c * chunk:(c + 1) * chunk, :] = jnp.dot(
            perm.astype(BF16), h, preferred_element_type=F32).astype(stage.dtype)

    def copy_from(s):
        def copy(stage_row, sorted_row, nrows):
            return pltpu.make_async_copy(stage.at[s, pl.ds(stage_row, nrows)],
                                         xs_hbm.at[pl.ds(sorted_row, nrows)], sem.at[s])
        return copy

    _granule_copies(t, gdst_ref, totg_ref, copy_from(slot), start=True)

    @pl.when(t > 0)
    def _():
        _granule_copies(jnp.maximum(t - 1, 0), gdst_ref, totg_ref, copy_from(1 - slot), start=False)

    @pl.when(t == pl.num_programs(0) - 1)
    def _():
        _granule_copies(t, gdst_ref, totg_ref, copy_from(slot), start=False)
        tb = zbuf.shape[0]
        n_used = misc_ref[0]
        zbuf[...] = jnp.zeros_like(zbuf)

        def zero_gran(row):
            return pltpu.make_async_copy(zbuf.at[pl.ds(0, MOE_GRAN)], xs_hbm.at[pl.ds(row, MOE_GRAN)], zsem)

        def zero_block(blk):
            return pltpu.make_async_copy(zbuf, xs_hbm.at[pl.ds(pl.multiple_of(blk * tb, tb), tb)], zsem)

        def tails(e, c):
            def one(g, c2):
                zero_gran(pl.multiple_of(tstart_ref[e] + g * MOE_GRAN, MOE_GRAN)).start()
                return c2
            lax.fori_loop(0, tgran_ref[e], one, 0)
            return c

        def start_block(blk, c):
            zero_block(blk).start()
            return c

        def wait_gran(g, c):
            zero_gran(0).wait()
            return c

        def wait_block(blk, c):
            zero_block(0).wait()
            return c

        lax.fori_loop(0, N_EXPERTS, tails, 0)
        lax.fori_loop(n_used, n_blocks, start_block, 0)
        lax.fori_loop(0, misc_ref[1], wait_gran, 0)
        lax.fori_loop(n_used, n_blocks, wait_block, 0)


def _dispatch(h2, srow_t, plan, tm):
    n, d = h2.shape
    nt = n // tm
    rows = _stage_rows(tm)
    misc = jnp.concatenate([plan["n_used"], plan["tot_tail"]])
    grid_spec = pltpu.PrefetchScalarGridSpec(
        num_scalar_prefetch=5,
        grid=(nt,),
        in_specs=[
            pl.BlockSpec((tm, d), lambda t, *_: (t, 0)),
            pl.BlockSpec((8, tm), lambda t, *_: (0, t)),
        ],
        out_specs=pl.BlockSpec(memory_space=pl.ANY),
        scratch_shapes=[pltpu.VMEM((2, rows, d), BF16), pltpu.VMEM((MOE_BLOCK, d), BF16),
                        pltpu.SemaphoreType.DMA((2,)), pltpu.SemaphoreType.DMA(())],
    )
    return pl.pallas_call(
        functools.partial(_dispatch_kernel, n_blocks=plan["nb"]),
        grid_spec=grid_spec,
        out_shape=jax.ShapeDtypeStruct((plan["nb"] * MOE_BLOCK, d), BF16),
        compiler_params=_cparams(("arbitrary",)),
    )(plan["gran_dst"], plan["tot_gran"], plan["tail_start"], plan["tail_gran"], misc, h2, srow_t)


def _expert_kernel(be_ref, nused_ref, nexte_ref, x_ref, w1_hbm, b1_ref, w2_hbm, b2_ref, o_ref,
                   w1f, w2f, w1b, w2b, sem):
    de = w2f.shape[0]
    j = pl.program_id(0)
    live = j < nused_ref[0]
    e = be_ref[j]
    new_expert = jnp.logical_or(j == 0, e != be_ref[jnp.maximum(j - 1, 0)])

    def fetch(ex):
        return (pltpu.make_async_copy(w1_hbm.at[ex], w1f, sem.at[0]),
                pltpu.make_async_copy(w2_hbm.at[ex], w2f, sem.at[1]))

    @pl.when(jnp.logical_and(live, j == 0))
    def _():
        for c in fetch(e):
            c.start()

    @pl.when(jnp.logical_and(live, new_expert))
    def _():
        for c in fetch(e):
            c.wait()
        w1b[...] = w1f[...].astype(BF16)
        w2b[...] = w2f[...].astype(BF16)
        nxt = nexte_ref[e]

        @pl.when(nxt != e)
        def _():
            for c in fetch(nxt):
                c.start()

    @pl.when(live)
    def _():
        hm = jnp.dot(x_ref[...], w1b[...], preferred_element_type=F32) + b1_ref[0]
        gate = jnp.minimum(hm[:, :de], SWIGLU_LIMIT)
        up = jnp.clip(hm[:, de:], -SWIGLU_LIMIT, SWIGLU_LIMIT)
        act = gate * _sigmoid(SWIGLU_ALPHA * gate) * (up + 1.0)
        y = jnp.dot(act.astype(BF16), w2b[...], preferred_element_type=F32) + b2_ref[0]
        o_ref[...] = y.astype(o_ref.dtype)

    @pl.when(jnp.logical_not(live))
    def _():
        o_ref[...] = jnp.zeros_like(o_ref)


def _experts(xs, plan, w1, b1, w2, b2):
    d = xs.shape[1]
    tb = MOE_BLOCK
    f2 = w1.shape[2]
    de = w2.shape[1]
    last = lambda j, nu: jnp.maximum(jnp.minimum(j, nu[0] - 1), 0)
    grid_spec = pltpu.PrefetchScalarGridSpec(
        num_scalar_prefetch=3,
        grid=(plan["nb"],),
        in_specs=[
            pl.BlockSpec((tb, d), lambda j, be, nu, ne: (last(j, nu), 0)),
            pl.BlockSpec(memory_space=pl.ANY),
            pl.BlockSpec((1, 1, f2), lambda j, be, nu, ne: (be[j], 0, 0)),
            pl.BlockSpec(memory_space=pl.ANY),
            pl.BlockSpec((1, 1, d), lambda j, be, nu, ne: (be[j], 0, 0)),
        ],
        out_specs=pl.BlockSpec((tb, d), lambda j, be, nu, ne: (j, 0)),
        scratch_shapes=[pltpu.VMEM((d, f2), F32), pltpu.VMEM((de, d), F32),
                        pltpu.VMEM((d, f2), BF16), pltpu.VMEM((de, d), BF16),
                        pltpu.SemaphoreType.DMA((2,))],
    )
    return pl.pallas_call(
        _expert_kernel,
        grid_spec=grid_spec,
        out_shape=jax.ShapeDtypeStruct(xs.shape, BF16),
        compiler_params=_cparams(("arbitrary",)),
    )(plan["block_e"], plan["n_used"], plan["next_expert"], xs, w1, b1, w2, b2)


def _combine_kernel(gdst_ref, totg_ref, x1_ref, srow_ref, wt_ref, g_ref, yb_hbm, o_ref, stage, sem):
    t = pl.program_id(0)
    slot = t % 2
    tm = x1_ref.shape[0]
    rows = stage.shape[1]

    def copy_into(s):
        def copy(stage_row, sorted_row, nrows):
            return pltpu.make_async_copy(yb_hbm.at[pl.ds(sorted_row, nrows)],
                                         stage.at[s, pl.ds(stage_row, nrows)], sem.at[s])
        return copy

    @pl.when(t == 0)
    def _():
        stage[...] = jnp.zeros_like(stage)
        _granule_copies(t, gdst_ref, totg_ref, copy_into(slot), start=True)

    @pl.when(t + 1 < pl.num_programs(0))
    def _():
        _granule_copies(t + 1, gdst_ref, totg_ref, copy_into(1 - slot), start=True)

    wt = wt_ref[...]
    srows = [srow_ref[:, k:k + 1] for k in range(TOP_K)]
    chunk = LANES
    row_id = lax.broadcasted_iota(jnp.int32, (chunk, rows), 1).astype(F32)
    for c in range(tm // chunk):
        sl = slice(c * chunk, (c + 1) * chunk)
        unsort = jnp.zeros((chunk, rows), F32)
        for k, srow in enumerate(srows):
            unsort = jnp.where(row_id == srow[sl], wt[sl, k:k + 1], unsort)
        if c == 0:
            _granule_copies(t, gdst_ref, totg_ref, copy_into(slot), start=False)
        y = jnp.dot(unsort.astype(BF16), stage[slot], preferred_element_type=F32)
        o_ref[sl, :] = _rms(x1_ref[sl, :] + y, g_ref[...])


def _combine(x1, yb, srow, top_w, plan, g_final, tm):
    n, d = x1.shape
    nt = n // tm
    rows = _stage_rows(tm)
    grid_spec = pltpu.PrefetchScalarGridSpec(
        num_scalar_prefetch=2,
        grid=(nt,),
        in_specs=[
            pl.BlockSpec((tm, d), lambda t, *_: (t, 0)),
            pl.BlockSpec((tm, LANES), lambda t, *_: (t, 0)),
            pl.BlockSpec((tm, LANES), lambda t, *_: (t, 0)),
            pl.BlockSpec((1, d), lambda t, *_: (0, 0)),
            pl.BlockSpec(memory_space=pl.ANY),
        ],
        out_specs=pl.BlockSpec((tm, d), lambda t, *_: (t, 0)),
        scratch_shapes=[pltpu.VMEM((2, rows, d), BF16), pltpu.SemaphoreType.DMA((2,))],
    )
    return pl.pallas_call(
        _combine_kernel,
        grid_spec=grid_spec,
        out_shape=jax.ShapeDtypeStruct((n, d), F32),
        compiler_params=_cparams(("arbitrary",)),
    )(plan["gran_dst"], plan["tot_gran"], x1, srow, top_w, g_final, yb)


def _rope_tables(seq):
    half = ATTN_HEAD_DIM // 2
    inv_freq = np.float32(ROPE_THETA) ** (-np.arange(0, half, 2, dtype=np.float32) / np.float32(half))
    ang_row = np.arange(seq // GRID_W, dtype=np.float32)[:, None] * inv_freq
    ang_col = np.arange(GRID_W, dtype=np.float32)[:, None] * inv_freq
    by_row = lambda t: jnp.repeat(jnp.asarray(t, F32), GRID_W, axis=0)
    by_col = lambda t: jnp.tile(jnp.asarray(t, F32), (seq // GRID_W, 1))
    cos_r, sin_r = by_row(np.cos(ang_row)), by_row(np.sin(ang_row))
    cos_c, sin_c = by_col(np.cos(ang_col)), by_col(np.sin(ang_col))
    cos_t = jnp.concatenate([cos_r, cos_c] * 2, axis=-1)
    sin_t = jnp.concatenate([-sin_r, -sin_c, sin_r, sin_c], axis=-1)
    return cos_t, sin_t


def _token_mixer(x2, batch, seq, g_mix, w_in, conv_w, conv_b, dt_bias_f, dt_bias_b, a_log_f, a_log_b, d_skip,
                 g_ssm, q_norm_g, k_norm_g, w_br_ssm, w_br_attn, w_out):
    n, d = x2.shape
    z_end = SSM_INNER
    xbc_end = z_end + CONV_CH
    dtf_end = xbc_end + SSM_HEADS
    dtb_end = dtf_end + SSM_HEADS
    q_end = dtb_end + ATTN_HEADS * ATTN_HEAD_DIM
    k_end = q_end + ATTN_KV_HEADS * ATTN_HEAD_DIM
    v_end = k_end + ATTN_KV_HEADS * ATTN_HEAD_DIM
    head_cols = lambda w: _rope_head_order(w.reshape(d, -1, ATTN_HEAD_DIM)).reshape(d, -1)
    w_main = jnp.concatenate([w_in[:, :z_end], head_cols(w_in[:, dtb_end:q_end]), w_in[:, v_end:],
                              w_in[:, z_end:xbc_end], head_cols(w_in[:, q_end:k_end]), w_in[:, k_end:v_end]],
                             axis=1).astype(BF16)
    w_dt = jnp.pad(w_in[:, xbc_end:dtb_end], ((0, 0), (0, LANES - 2 * SSM_HEADS))).astype(BF16)

    proj, dt = _in_proj(x2, g_mix.reshape(1, d), w_main, w_dt)
    proj3 = proj.reshape(batch, seq, PROJ_COLS)

    xbc = _conv(proj3, conv_w, conv_b.reshape(1, CONV_CH))

    dt3 = dt.reshape(batch, seq, LANES)
    dtt3 = jnp.swapaxes(dt3[:, :, :2 * SSM_HEADS], 1, 2)
    bias = jnp.concatenate([dt_bias_f, dt_bias_b])
    alog = jnp.concatenate([a_log_f, a_log_b])
    pad_row = lambda v: jnp.pad(v, (0, LANES - 2 * SSM_HEADS)).reshape(1, LANES)
    y_f, y_b = _ssd(xbc, dt3, dtt3, pad_row(bias), bias.reshape(-1, 1), pad_row(alog), alog.reshape(-1, 1))

    cos_t, sin_t = _rope_tables(seq)
    q_rot, k_rot = _qk_prep(proj, cos_t, sin_t, _rope_head_order(q_norm_g).reshape(1, -1),
                            _rope_head_order(k_norm_g).reshape(1, -1), seq)
    vt3 = jnp.swapaxes(proj3[:, :, V_OFF:V_OFF + ATTN_KV_HEADS * ATTN_HEAD_DIM], 1, 2)
    attn = _flash(q_rot.reshape(batch, seq, -1), k_rot.reshape(batch, seq, -1), vt3, q_norm_g, k_norm_g)

    return _merge(y_f.reshape(n, -1), y_b.reshape(n, -1), xbc.reshape(n, CONV_CH), proj, attn.reshape(n, -1),
                  x2, jnp.repeat(d_skip, SSM_HEAD_DIM).reshape(1, -1), g_ssm.reshape(1, -1),
                  w_br_ssm.astype(BF16), w_br_attn.astype(BF16), w_out.astype(BF16))


def _moe_and_final_norm(x1, g_ffn, w_router, b_router, w_mlp1, b_mlp1, w_mlp2, b_mlp2, g_final):
    n, d = x1.shape
    tm = min(MOE_TILE, n)
    w_t = w_router.T
    w_hi = w_t.astype(BF16)
    w_lo = (w_t - w_hi.astype(F32)).astype(BF16)
    g_ffn_row = g_ffn.reshape(1, d)
    h2, srow_t, wt_t, cnt = _router(x1, g_ffn_row, jnp.concatenate([w_hi, w_lo, w_hi], axis=1),
                                    b_router.reshape(N_EXPERTS, 1))
    plan = _routing_plan(cnt[:, :, 0].astype(jnp.int32), n, tm, MOE_BLOCK)
    lanes_0_3 = lambda a: jnp.pad(a[:TOP_K].T, ((0, 0), (0, LANES - TOP_K)))
    xs = _dispatch(h2, srow_t, plan, tm)
    yb = _experts(xs, plan, w_mlp1, b_mlp1[:, None, :], w_mlp2, b_mlp2[:, None, :])
    return _combine(x1, yb, lanes_0_3(srow_t), lanes_0_3(wt_t), plan, g_final.reshape(1, d), tm)


def kernel(x, g_mix, w_in, conv_w, conv_b, dt_bias_f, dt_bias_b, a_log_f, a_log_b, d_skip, g_ssm, q_norm_g,
           k_norm_g, w_br_ssm, w_br_attn, w_out, g_ffn, w_router, b_router, w_mlp1, b_mlp1, w_mlp2, b_mlp2,
           g_final):
    batch, seq, d = x.shape
    assert g_mix.shape[0] == 1, "single-layer model: the final rmsnorm is fused into the MoE combine"
    x2 = x.reshape(batch * seq, d)
    x1 = _token_mixer(x2, batch, seq, g_mix[0], w_in[0], conv_w[0], conv_b[0], dt_bias_f[0], dt_bias_b[0],
                      a_log_f[0], a_log_b[0], d_skip[0], g_ssm[0], q_norm_g[0], k_norm_g[0], w_br_ssm[0],
                      w_br_attn[0], w_out[0])
    out = _moe_and_final_norm(x1, g_ffn[0], w_router[0], b_router[0], w_mlp1[0], b_mlp1[0], w_mlp2[0],
                              b_mlp2[0], g_final)
    return out.reshape(batch, seq, d)
```

```python
import functools
import math

import jax
import jax.numpy as jnp
import numpy as np
from jax import lax
from jax.experimental import pallas as pl
from jax.experimental.pallas import tpu as pltpu

F32 = jnp.float32
BF16 = jnp.bfloat16

NORM_EPS = 1e-6
GRID_W = 64
SSM_HEADS = 16
SSM_HEAD_DIM = 64
SSM_INNER = SSM_HEADS * SSM_HEAD_DIM
SSM_GROUPS = 2
SSM_STATE = 128
SSM_CONV = 5
CONV_CH = SSM_INNER + 2 * SSM_GROUPS * SSM_STATE
ATTN_HEADS = 8
ATTN_KV_HEADS = 2
ATTN_HEAD_DIM = 128
ROPE_THETA = 10000.0
N_EXPERTS = 32
TOP_K = 4
SWIGLU_LIMIT = 7.0
SWIGLU_ALPHA = 1.702

LANES = 128
BF16_SUBLANES = 16
VMEM_LIMIT = 56 * 1024 * 1024

Z_OFF, Q_OFF, GATE_OFF, XBC_OFF = 0, 1024, 2048, 4096
K_OFF, V_OFF, PROJ_COLS = 5632, 5888, 6144

MOE_TILE = 512
MOE_BLOCK = 512
MOE_GRAN = BF16_SUBLANES


def _cparams(sem):
    return pltpu.CompilerParams(dimension_semantics=sem, vmem_limit_bytes=VMEM_LIMIT)


def _sigmoid(x):
    return 1.0 / (1.0 + jnp.exp(-x))


def _softplus(x):
    return jnp.maximum(x, 0.0) + jnp.log(1.0 + jnp.exp(-jnp.abs(x)))


def _rms(x, g):
    ms = jnp.mean(x * x, axis=-1, keepdims=True)
    return x * lax.rsqrt(ms + NORM_EPS) * g


def _inproj_kernel(x_ref, g_ref, w_ref, wdt_ref, o_ref, dt_ref, h_scr):
    @pl.when(pl.program_id(1) == 0)
    def _():
        hb = _rms(x_ref[...], g_ref[...]).astype(BF16)
        h_scr[...] = hb
        dt_ref[...] = jnp.dot(hb, wdt_ref[...], preferred_element_type=F32)

    o_ref[...] = jnp.dot(h_scr[...], w_ref[...], preferred_element_type=F32).astype(o_ref.dtype)


def _in_proj(x2, g_mix, w_main, w_dt):
    n, d = x2.shape
    tm = min(1024, n)
    tn = 2048
    return pl.pallas_call(
        _inproj_kernel,
        grid=(n // tm, PROJ_COLS // tn),
        in_specs=[
            pl.BlockSpec((tm, d), lambda i, j: (i, 0)),
            pl.BlockSpec((1, d), lambda i, j: (0, 0)),
            pl.BlockSpec((d, tn), lambda i, j: (0, j)),
            pl.BlockSpec((d, LANES), lambda i, j: (0, 0)),
        ],
        out_specs=[
            pl.BlockSpec((tm, tn), lambda i, j: (i, j)),
            pl.BlockSpec((tm, LANES), lambda i, j: (i, 0)),
        ],
        out_shape=[
            jax.ShapeDtypeStruct((n, PROJ_COLS), BF16),
            jax.ShapeDtypeStruct((n, LANES), F32),
        ],
        scratch_shapes=[pltpu.VMEM((tm, d), BF16)],
        compiler_params=_cparams(("arbitrary", "arbitrary")),
    )(x2, g_mix, w_main, w_dt)


CONV_HALO = 64
CONV_ROWS = 128


def _conv_kernel(prev_ref, cur_ref, next_ref, shift_ref, w_ref, b_ref, o_ref):
    s = pl.program_id(1)
    ts = cur_ref.shape[1]
    prev = prev_ref[0]
    nxt = next_ref[0]
    zero = jnp.zeros_like(prev)
    ext = jnp.concatenate([jnp.where(s == 0, zero, prev), cur_ref[0],
                           jnp.where(s == pl.num_programs(1) - 1, zero, nxt)], axis=0)
    shift = shift_ref[...]
    for rb in range(ts // CONV_ROWS):
        lo = rb * CONV_ROWS
        taps = jnp.dot(shift, ext[lo:lo + CONV_ROWS + 2 * CONV_HALO], preferred_element_type=F32)
        acc = b_ref[...] + w_ref[0:1, :] * taps[0:CONV_ROWS]
        for k in range(1, SSM_CONV):
            acc = acc + w_ref[k:k + 1, :] * taps[k * CONV_ROWS:(k + 1) * CONV_ROWS]
        o_ref[0, lo:lo + CONV_ROWS, :] = (acc * _sigmoid(acc)).astype(o_ref.dtype)


def _conv(proj3, conv_w, conv_b):
    b, s, _ = proj3.shape
    ts = min(2048, s)
    tc = 512
    halo = CONV_HALO
    hb = ts // halo
    col0 = XBC_OFF // tc
    pad = (SSM_CONV - 1) // 2
    out_row = np.arange(SSM_CONV * CONV_ROWS)
    src = out_row % CONV_ROWS + halo + out_row // CONV_ROWS - pad
    shift = jnp.asarray(src[:, None] == np.arange(CONV_ROWS + 2 * halo)[None, :], BF16)
    return pl.pallas_call(
        _conv_kernel,
        grid=(b, s // ts, CONV_CH // tc),
        in_specs=[
            pl.BlockSpec((1, halo, tc), lambda bi, si, ci: (bi, jnp.maximum(si * hb - 1, 0), col0 + ci)),
            pl.BlockSpec((1, ts, tc), lambda bi, si, ci: (bi, si, col0 + ci)),
            pl.BlockSpec((1, halo, tc),
                         lambda bi, si, ci: (bi, jnp.minimum((si + 1) * hb, s // halo - 1), col0 + ci)),
            pl.BlockSpec(shift.shape, lambda bi, si, ci: (0, 0)),
            pl.BlockSpec((SSM_CONV, tc), lambda bi, si, ci: (0, ci)),
            pl.BlockSpec((1, tc), lambda bi, si, ci: (0, ci)),
        ],
        out_specs=pl.BlockSpec((1, ts, tc), lambda bi, si, ci: (bi, si, ci)),
        out_shape=jax.ShapeDtypeStruct((b, s, CONV_CH), BF16),
        compiler_params=_cparams(("arbitrary", "arbitrary", "arbitrary")),
    )(proj3, proj3, proj3, shift, conv_w, conv_b)


SSM_CHUNKS_PER_STEP = 4


def _ssd_kernel(xf_ref, xb_ref, dtf_ref, dtb_ref, dttf_ref, dttb_ref, brow_ref, bcol_ref,
                arow_ref, acol_ref, yf_ref, yb_ref, st_ref):
    L = SSM_STATE
    nsub = xf_ref.shape[1] // L
    hg = SSM_HEADS // SSM_GROUPS
    pairs = hg // 2

    @pl.when(pl.program_id(1) == 0)
    def _():
        st_ref[...] = jnp.zeros_like(st_ref)

    rows = lax.broadcasted_iota(jnp.int32, (L, L), 0)
    cols = lax.broadcasted_iota(jnp.int32, (L, L), 1)
    lower = rows >= cols
    upper = rows <= cols
    ltri = lower.astype(BF16)
    utri = upper.astype(BF16)
    lane = lax.broadcasted_iota(jnp.int32, (L, LANES), 1)
    first_half = lane < SSM_HEAD_DIM
    lane1 = lax.broadcasted_iota(jnp.int32, (1, LANES), 1)
    log2e = math.log2(math.e)
    a_row = -jnp.exp(arow_ref[...]) * log2e
    a_col = -jnp.exp(acol_ref[...]) * log2e

    def split3(v):
        hi = v.astype(BF16)
        r1 = v - hi.astype(F32)
        mid = r1.astype(BF16)
        return hi, mid, (r1 - mid.astype(F32)).astype(BF16)

    def cumsum_cols(tri, v):
        return jnp.dot(jnp.concatenate([tri] * 3, axis=1), jnp.concatenate(split3(v), axis=0),
                       preferred_element_type=F32)

    def cumsum_rows(v, tri):
        return jnp.dot(jnp.concatenate(split3(v), axis=1), jnp.concatenate([tri] * 3, axis=0),
                       preferred_element_type=F32)

    for sub, d in [(sub, d) for sub in range(nsub) for d in range(2)]:
        x_ref, dt_ref, dtt_ref, y_ref = ((xf_ref, dtf_ref, dttf_ref, yf_ref) if d == 0
                                         else (xb_ref, dtb_ref, dttb_ref, yb_ref))
        r0 = (sub if d == 0 else nsub - 1 - sub) * L
        rs = slice(r0, r0 + L)
        a = _softplus(dt_ref[0, rs, :] + brow_ref[...]) * a_row
        dt_t = _softplus(dtt_ref[0, :, rs] + bcol_ref[...])
        a_t = dt_t * a_col
        if d == 0:
            cs_col = cumsum_cols(ltri, a)
            cs_row = cumsum_rows(a_t, utri)
            tot = cs_col[L - 1:L, :]
            tot_t = cs_row[:, L - 1:L]
            mask = lower
        else:
            cs_col = cumsum_cols(utri, a)
            cs_row = cumsum_rows(a_t, ltri)
            tot = cs_col[0:1, :]
            tot_t = cs_row[:, 0:1]
            mask = upper
        w_t = dt_t * jnp.exp2(tot_t - cs_row)
        src_t = cs_row - jnp.log2(dt_t)
        chunk_decay = jnp.exp2(tot)

        for g in range(SSM_GROUPS):
            boff = SSM_INNER + g * SSM_STATE
            coff = SSM_INNER + SSM_GROUPS * SSM_STATE + g * SSM_STATE
            bm = x_ref[0, rs, boff:boff + SSM_STATE]
            cm = x_ref[0, rs, coff:coff + SSM_STATE]
            cb = lax.dot_general(cm, bm, (((1,), (1,)), ((), ())), preferred_element_type=F32)
            bt = bm.astype(F32).T
            st = st_ref[d, g]
            y_off = jnp.dot(cm, st.astype(BF16), preferred_element_type=F32)
            for pr in range(pairs):
                h0 = d * SSM_HEADS + g * hg + 2 * pr
                xoff = (g * pairs + pr) * LANES
                xs = x_ref[0, rs, xoff:xoff + LANES]
                zero = jnp.zeros_like(xs)
                rhs = jnp.concatenate([jnp.where(first_half, xs, zero),
                                       jnp.where(first_half, zero, xs)], axis=0)
                ms, ws, dins = [], [], []
                for hh in (h0, h0 + 1):
                    cs_b = jnp.broadcast_to(cs_col[:, hh:hh + 1], (L, L))
                    seg = cs_b - src_t[hh:hh + 1, :]
                    m = cb * jnp.exp2(jnp.where(mask, seg, -jnp.inf))
                    ms.append(m.astype(BF16))
                    ws.append((bt * w_t[hh:hh + 1, :]).astype(BF16))
                    dins.append(jnp.exp2(cs_b))
                y = jnp.dot(jnp.concatenate(ms, axis=1), rhs, preferred_element_type=F32)
                y = y + y_off[:, pr * LANES:(pr + 1) * LANES] * jnp.where(first_half, dins[0], dins[1])
                y_ref[0, rs, xoff:xoff + LANES] = y.astype(y_ref.dtype)
                new_st = jnp.dot(jnp.concatenate(ws, axis=1), rhs, preferred_element_type=F32)
                cd = jnp.where(lane1 < SSM_HEAD_DIM, chunk_decay[:, h0:h0 + 1], chunk_decay[:, h0 + 1:h0 + 2])
                st_ref[d, g, :, pr * LANES:(pr + 1) * LANES] = st[:, pr * LANES:(pr + 1) * LANES] * cd + new_st


def _ssd(xbc, dt3, dtt3, bias_row, bias_col, alog_row, alog_col):
    b, s, _ = xbc.shape
    L = SSM_CHUNKS_PER_STEP * SSM_STATE
    nc = s // L
    hg = SSM_HEADS // SSM_GROUPS
    fwd = lambda bi, ci: (bi, ci, 0)
    bwd = lambda bi, ci: (bi, nc - 1 - ci, 0)
    fwd_t = lambda bi, ci: (bi, 0, ci)
    bwd_t = lambda bi, ci: (bi, 0, nc - 1 - ci)
    const = lambda bi, ci: (0, 0)
    return pl.pallas_call(
        _ssd_kernel,
        grid=(b, nc),
        in_specs=[
            pl.BlockSpec((1, L, CONV_CH), fwd),
            pl.BlockSpec((1, L, CONV_CH), bwd),
            pl.BlockSpec((1, L, LANES), fwd),
            pl.BlockSpec((1, L, LANES), bwd),
            pl.BlockSpec((1, 2 * SSM_HEADS, L), fwd_t),
            pl.BlockSpec((1, 2 * SSM_HEADS, L), bwd_t),
            pl.BlockSpec((1, LANES), const),
            pl.BlockSpec((2 * SSM_HEADS, 1), const),
            pl.BlockSpec((1, LANES), const),
            pl.BlockSpec((2 * SSM_HEADS, 1), const),
        ],
        out_specs=[
            pl.BlockSpec((1, L, SSM_INNER), fwd),
            pl.BlockSpec((1, L, SSM_INNER), bwd),
        ],
        out_shape=[jax.ShapeDtypeStruct((b, s, SSM_INNER), BF16)] * 2,
        scratch_shapes=[pltpu.VMEM((2, SSM_GROUPS, SSM_STATE, hg * SSM_HEAD_DIM), F32)],
        compiler_params=_cparams(("arbitrary", "arbitrary")),
    )(xbc, xbc, dt3, dt3, dtt3, dtt3, bias_row, bias_col, alog_row, alog_col)


def _rope_head_order(a):
    q4 = ATTN_HEAD_DIM // 4
    return jnp.concatenate([a[..., 0:q4], a[..., 2 * q4:3 * q4], a[..., q4:2 * q4], a[..., 3 * q4:]], axis=-1)


def _rope_norm(t, g, cos, sin_signed, ones):
    sq = t * t
    hi = sq.astype(BF16)
    lo = (sq - hi.astype(F32)).astype(BF16)
    ms = jnp.dot(jnp.concatenate([hi, lo], axis=1), ones, preferred_element_type=F32) * (1.0 / ATTN_HEAD_DIM)
    tn = t * lax.rsqrt(ms + NORM_EPS) * g
    return tn * cos + pltpu.roll(tn, ATTN_HEAD_DIM // 2, 1) * sin_signed


Q_SCALE = ATTN_HEAD_DIM ** -0.5 * math.log2(math.e)


def _qkprep_kernel(q_ref, k_ref, cos_ref, sin_ref, qg_ref, kg_ref, qo_ref, ko_ref):
    cos = cos_ref[...]
    sin = sin_ref[...]
    ones = jnp.ones((2 * ATTN_HEAD_DIM, ATTN_HEAD_DIM), BF16)

    def heads(src_ref, g_ref, dst_ref, n_heads, scale):
        for h in range(n_heads):
            sl = slice(h * ATTN_HEAD_DIM, (h + 1) * ATTN_HEAD_DIM)
            r = _rope_norm(src_ref[:, sl].astype(F32), g_ref[...], cos, sin, ones) * scale
            dst_ref[:, sl] = r.astype(dst_ref.dtype)

    heads(q_ref, qg_ref, qo_ref, ATTN_HEADS, Q_SCALE)
    heads(k_ref, kg_ref, ko_ref, ATTN_KV_HEADS, 1.0)


def _qk_prep(proj, cos_t, sin_t, q_norm_g, k_norm_g, seq):
    n = proj.shape[0]
    tm = min(1024, seq)
    qw = ATTN_HEADS * ATTN_HEAD_DIM
    kw = ATTN_KV_HEADS * ATTN_HEAD_DIM
    spt = seq // tm
    return pl.pallas_call(
        _qkprep_kernel,
        grid=(n // tm,),
        in_specs=[
            pl.BlockSpec((tm, qw), lambda i: (i, Q_OFF // qw)),
            pl.BlockSpec((tm, kw), lambda i: (i, K_OFF // kw)),
            pl.BlockSpec((tm, ATTN_HEAD_DIM), lambda i: (i % spt, 0)),
            pl.BlockSpec((tm, ATTN_HEAD_DIM), lambda i: (i % spt, 0)),
            pl.BlockSpec((1, ATTN_HEAD_DIM), lambda i: (0, 0)),
            pl.BlockSpec((1, ATTN_HEAD_DIM), lambda i: (0, 0)),
        ],
        out_specs=[
            pl.BlockSpec((tm, qw), lambda i: (i, 0)),
            pl.BlockSpec((tm, kw), lambda i: (i, 0)),
        ],
        out_shape=[jax.ShapeDtypeStruct((n, qw), BF16), jax.ShapeDtypeStruct((n, kw), BF16)],
        compiler_params=_cparams(("arbitrary",)),
    )(proj, proj, cos_t, sin_t, q_norm_g, k_norm_g)


def _flash_kernel(small_ref, q_ref, k_ref, vt_ref, o_ref, *, tk, th):
    q = q_ref[0]
    tq = q.shape[0]
    seq = k_ref.shape[1]
    nt = (((1,), (1,)), ((), ()))
    small = small_ref[0] != 0

    @pl.when(small)
    def _():
        l = acc_t = None
        for c in range(seq // th):
            ks = slice(c * th, (c + 1) * th)
            p_t = jnp.exp2(lax.dot_general(k_ref[0, ks, :], q, nt, preferred_element_type=F32))
            l_c = jnp.sum(p_t, axis=0, keepdims=True)
            a_c = jnp.dot(vt_ref[0, :, ks], p_t.astype(BF16), preferred_element_type=F32)
            l = l_c if l is None else l + l_c
            acc_t = a_c if acc_t is None else acc_t + a_c
        o_ref[0] = (acc_t / l).T.astype(o_ref.dtype)

    @pl.when(jnp.logical_not(small))
    def _():
        def body(i, carry):
            m, l, acc_t = carry
            off = pl.multiple_of(i * tk, tk)
            s_t = lax.dot_general(k_ref[0, pl.ds(off, tk), :], q, nt, preferred_element_type=F32)
            m_new = jnp.maximum(m, jnp.max(s_t, axis=0, keepdims=True))
            alpha = jnp.exp2(m - m_new)
            p_t = jnp.exp2(s_t - m_new)
            l = alpha * l + jnp.sum(p_t, axis=0, keepdims=True)
            acc_t = alpha * acc_t + jnp.dot(vt_ref[0, :, pl.ds(off, tk)], p_t.astype(BF16),
                                            preferred_element_type=F32)
            return m_new, l, acc_t

        init = (jnp.full((1, tq), -jnp.inf, F32), jnp.zeros((1, tq), F32),
                jnp.zeros((ATTN_HEAD_DIM, tq), F32))
        _, l, acc_t = lax.fori_loop(0, seq // tk, body, init)
        o_ref[0] = (acc_t / l).T.astype(o_ref.dtype)


SCORE_BOUND = 59.0


def _flash(q3, k3, vt3, q_norm_g, k_norm_g):
    b, s, _ = q3.shape
    tq = min(1024, s)
    tk = min(2048, s)
    th = min(4096, s)
    nq = s // tq
    grp = ATTN_HEADS // ATTN_KV_HEADS
    hd = ATTN_HEAD_DIM
    bound = hd * Q_SCALE * jnp.max(jnp.abs(q_norm_g)) * jnp.max(jnp.abs(k_norm_g)) * 1.02
    small = (bound <= SCORE_BOUND).astype(jnp.int32).reshape(1)
    grid_spec = pltpu.PrefetchScalarGridSpec(
        num_scalar_prefetch=1,
        grid=(b, ATTN_HEADS, nq),
        in_specs=[
            pl.BlockSpec((1, tq, hd), lambda bi, h, qi, sm: (bi, qi, h)),
            pl.BlockSpec((1, s, hd), lambda bi, h, qi, sm: (bi, 0, h // grp)),
            pl.BlockSpec((1, hd, s), lambda bi, h, qi, sm: (bi, h // grp, 0)),
        ],
        out_specs=pl.BlockSpec((1, tq, hd), lambda bi, h, qi, sm: (bi, qi, h)),
    )
    return pl.pallas_call(
        functools.partial(_flash_kernel, tk=tk, th=th),
        grid_spec=grid_spec,
        out_shape=jax.ShapeDtypeStruct((b, s, ATTN_HEADS * hd), BF16),
        compiler_params=_cparams(("arbitrary", "arbitrary", "arbitrary")),
    )(small, q3, k3, vt3)


def _merge_kernel(yf_ref, yb_ref, xs_ref, z_ref, gate_ref, attn_ref, x_ref, dskip_ref, gssm_ref,
                  wbs_ref, wba_ref, wo_ref, x1_ref):
    d = x_ref.shape[1]
    xs = xs_ref[...].astype(F32)
    y = yf_ref[...].astype(F32) + yb_ref[...].astype(F32) + xs * dskip_ref[...]
    z = z_ref[...].astype(F32)
    y = _rms(y * (z * _sigmoid(z)), gssm_ref[...])
    br_ssm = jnp.dot(y.astype(BF16), wbs_ref[...], preferred_element_type=F32)
    br_attn = jnp.dot(attn_ref[...], wba_ref[...], preferred_element_type=F32)
    g_s = _sigmoid(gate_ref[:, :d].astype(F32))
    g_a = _sigmoid(gate_ref[:, d:].astype(F32))
    merged = (g_s * br_ssm + g_a * br_attn).astype(BF16)
    x1_ref[...] = x_ref[...] + jnp.dot(merged, wo_ref[...], preferred_element_type=F32)


def _merge(y_f, y_b, xbc, proj, attn, x2, dskip_row, g_ssm, w_br_ssm, w_br_attn, w_out):
    n, d = x2.shape
    tm = min(512, n)
    row = lambda i: (i, 0)
    const = lambda i: (0, 0)
    return pl.pallas_call(
        _merge_kernel,
        grid=(n // tm,),
        in_specs=[
            pl.BlockSpec((tm, d), row),
            pl.BlockSpec((tm, d), row),
            pl.BlockSpec((tm, d), row),
            pl.BlockSpec((tm, d), lambda i: (i, Z_OFF // d)),
            pl.BlockSpec((tm, 2 * d), lambda i: (i, GATE_OFF // (2 * d))),
            pl.BlockSpec((tm, d), row),
            pl.BlockSpec((tm, d), row),
            pl.BlockSpec((1, d), const),
            pl.BlockSpec((1, d), const),
            pl.BlockSpec((d, d), const),
            pl.BlockSpec((d, d), const),
            pl.BlockSpec((d, d), const),
        ],
        out_specs=pl.BlockSpec((tm, d), row),
        out_shape=jax.ShapeDtypeStruct((n, d), F32),
        compiler_params=_cparams(("arbitrary",)),
    )(y_f, y_b, xbc, proj, proj, attn, x2, dskip_row, g_ssm, w_br_ssm, w_br_attn, w_out)


def _router_kernel(x1_ref, g_ref, w_ref, b_ref, h_ref, srow_ref, wt_ref, cnt_ref):
    tm = x1_ref.shape[0]
    h = _rms(x1_ref[...], g_ref[...])
    h_hi = h.astype(BF16)
    h_lo = (h - h_hi.astype(F32)).astype(BF16)
    h_ref[...] = h_hi
    logits = lax.dot_general(w_ref[...], jnp.concatenate([h_hi, h_hi, h_lo], axis=1), (((1,), (1,)), ((), ())),
                             preferred_element_type=F32) + b_ref[...]
    expert = lax.broadcasted_iota(jnp.int32, logits.shape, 0)
    slot = lax.broadcasted_iota(jnp.int32, wt_ref.shape, 0)
    chosen = jnp.zeros(logits.shape, F32)
    vals, hits = [], []
    for k in range(TOP_K):
        m = jnp.max(logits, axis=0, keepdims=True)
        idx = jnp.min(jnp.where(logits == m, expert, N_EXPERTS), axis=0, keepdims=True)
        vals.append(m)
        hit = expert == idx
        hits.append(hit)
        chosen = jnp.where(hit, 1.0, chosen)
        logits = jnp.where(hit, -jnp.inf, logits)
    es = [jnp.exp(v - vals[0]) for v in vals]
    tot = es[0] + es[1] + es[2] + es[3]

    chosen16 = chosen.astype(BF16)
    cnt_col = jnp.sum(chosen, axis=1, keepdims=True)
    cnt_row = lax.dot_general(jnp.ones((8, tm), BF16), chosen16, (((1,), (1,)), ((), ())),
                              preferred_element_type=F32)[0:1, :]
    seg_row = jnp.ceil(cnt_row * (1.0 / MOE_GRAN)) * MOE_GRAN
    before = (lax.broadcasted_iota(jnp.int32, (N_EXPERTS, N_EXPERTS), 1)
              < lax.broadcasted_iota(jnp.int32, (N_EXPERTS, N_EXPERTS), 0))
    seg_off = jnp.sum(jnp.where(before, seg_row, 0.0), axis=1, keepdims=True)
    earlier = (lax.broadcasted_iota(jnp.int32, (tm, tm), 0)
               < lax.broadcasted_iota(jnp.int32, (tm, tm), 1)).astype(BF16)
    pos = jnp.dot(chosen16, earlier, preferred_element_type=F32) + seg_off

    val_out = jnp.zeros(wt_ref.shape, F32)
    row_out = jnp.zeros(srow_ref.shape, F32)
    for k in range(TOP_K):
        val_out = jnp.where(slot == k, es[k] / tot, val_out)
        row_out = jnp.where(slot == k, jnp.sum(jnp.where(hits[k], pos, 0.0), axis=0, keepdims=True), row_out)
    wt_ref[...] = val_out
    srow_ref[...] = row_out
    cnt_ref[0] = jnp.broadcast_to(cnt_col, cnt_ref.shape[1:])


def _router(x1, g_ffn, w_router_t, b_router_col):
    n, d = x1.shape
    tm = min(MOE_TILE, n)
    return pl.pallas_call(
        _router_kernel,
        grid=(n // tm,),
        in_specs=[
            pl.BlockSpec((tm, d), lambda i: (i, 0)),
            pl.BlockSpec((1, d), lambda i: (0, 0)),
            pl.BlockSpec((N_EXPERTS, 3 * d), lambda i: (0, 0)),
            pl.BlockSpec((N_EXPERTS, 1), lambda i: (0, 0)),
        ],
        out_specs=[pl.BlockSpec((tm, d), lambda i: (i, 0)),
                   pl.BlockSpec((8, tm), lambda i: (0, i)),
                   pl.BlockSpec((8, tm), lambda i: (0, i)),
                   pl.BlockSpec((1, N_EXPERTS, LANES), lambda i: (i, 0, 0))],
        out_shape=[jax.ShapeDtypeStruct((n, d), BF16),
                   jax.ShapeDtypeStruct((8, n), F32),
                   jax.ShapeDtypeStruct((8, n), F32),
                   jax.ShapeDtypeStruct((n // tm, N_EXPERTS, LANES), F32)],
        compiler_params=_cparams(("arbitrary",)),
    )(x1, g_ffn, w_router_t, b_router_col)


def _routing_plan(cnt, n, tm, tb):
    nt = n // tm
    gran = MOE_GRAN
    rc = (cnt + gran - 1) // gran * gran
    covered = jnp.sum(rc, axis=0)
    region = (covered + tb - 1) // tb * tb
    pad_end = jnp.cumsum(region)
    pad_start = pad_end - region
    seg_start = pad_start[None, :] + jnp.cumsum(rc, axis=0) - rc
    n_used = pad_end[-1] // tb
    nb = (TOP_K * n + nt * N_EXPERTS * (gran - 1) + N_EXPERTS * (tb - 1) + tb - 1) // tb
    blk = jnp.arange(nb, dtype=jnp.int32)
    block_e = jnp.sum((pad_end[None, :] <= (jnp.minimum(blk, n_used - 1) * tb)[:, None]).astype(jnp.int32), axis=1)
    block_e = jnp.minimum(block_e, N_EXPERTS - 1)
    flat = lambda a: a.reshape(-1).astype(jnp.int32)
    tail_gran = (region - covered) // gran
    ngran = rc // gran
    g_end = jnp.cumsum(ngran, axis=1)
    gi = jnp.arange(_stage_rows(tm) // gran, dtype=jnp.int32)
    g_exp = jnp.minimum(jnp.sum((g_end[:, None, :] <= gi[None, :, None]).astype(jnp.int32), axis=2), N_EXPERTS - 1)
    mine = g_exp[:, :, None] == jnp.arange(N_EXPERTS, dtype=jnp.int32)[None, None, :]
    seg_base = seg_start - gran * (g_end - ngran)
    gran_dst = jnp.sum(jnp.where(mine, seg_base[:, None, :], 0), axis=2) + gran * gi[None, :]
    ids = jnp.arange(N_EXPERTS, dtype=jnp.int32)
    later = jnp.where((ids[None, :] > ids[:, None]) & (region[None, :] > 0), ids[None, :], N_EXPERTS)
    nxt = jnp.min(later, axis=1)
    next_expert = jnp.where(nxt < N_EXPERTS, nxt, ids)
    return dict(gran_dst=flat(gran_dst), tot_gran=flat(g_end[:, -1]), block_e=flat(block_e),
                next_expert=flat(next_expert),
                n_used=flat(n_used), nb=nb,
                tail_start=flat(pad_start + covered), tail_gran=flat(tail_gran),
                tot_tail=flat(jnp.sum(tail_gran)))


def _stage_rows(tm):
    rows = TOP_K * tm + N_EXPERTS * (MOE_GRAN - 1)
    return (rows + 2 * LANES - 1) // (2 * LANES) * (2 * LANES)


GRAN_UNROLL = 4


def _granule_copies(t, gdst_ref, totg_ref, make_copy, start):
    total = totg_ref[t]
    base = t * (gdst_ref.shape[0] // totg_ref.shape[0])
    full = lax.shift_right_logical(total, GRAN_UNROLL.bit_length() - 1)

    def one(i):
        make_copy(pl.multiple_of(i * MOE_GRAN, MOE_GRAN), pl.multiple_of(gdst_ref[base + i], MOE_GRAN),
                  MOE_GRAN).start()

    def group(q, c):
        if start:
            for u in range(GRAN_UNROLL):
                one(q * GRAN_UNROLL + u)
        else:
            make_copy(0, 0, GRAN_UNROLL * MOE_GRAN).wait()
        return c

    def rest(i, c):
        if start:
            one(i)
        else:
            make_copy(0, 0, MOE_GRAN).wait()
        return c

    lax.fori_loop(0, full, group, 0)
    lax.fori_loop(full * GRAN_UNROLL, total, rest, 0)


def _dispatch_kernel(gdst_ref, totg_ref, tstart_ref, tgran_ref, misc_ref,
                     h_ref, srow_ref, xs_hbm, stage, zbuf, sem, zsem, *, n_blocks):
    t = pl.program_id(0)
    slot = t % 2
    tm = h_ref.shape[0]
    rows = stage.shape[1]
    h = h_ref[...]
    srows = [srow_ref[k:k + 1, :] for k in range(TOP_K)]
    chunk = 2 * LANES
    row_id = lax.broadcasted_iota(jnp.int32, (chunk, tm), 0).astype(F32)
    for c in range(rows // chunk):
        perm = jnp.zeros((chunk, tm), F32)
        for srow in srows:
            perm = jnp.where(row_id == srow - float(c * chunk), 1.0, perm)
        stage[slot, c * chunk:(c + 1) * chunk, :] = jnp.dot(
            perm.astype(BF16), h, preferred_element_type=F32).astype(stage.dtype)

    def copy_from(s):
        def copy(stage_row, sorted_row, nrows):
            return pltpu.make_async_copy(stage.at[s, pl.ds(stage_row, nrows)],
                                         xs_hbm.at[pl.ds(sorted_row, nrows)], sem.at[s])
        return copy

    _granule_copies(t, gdst_ref, totg_ref, copy_from(slot), start=True)

    @pl.when(t > 0)
    def _():
        _granule_copies(jnp.maximum(t - 1, 0), gdst_ref, totg_ref, copy_from(1 - slot), start=False)

    @pl.when(t == pl.num_programs(0) - 1)
    def _():
        _granule_copies(t, gdst_ref, totg_ref, copy_from(slot), start=False)
        tb = zbuf.shape[0]
        n_used = misc_ref[0]
        zbuf[...] = jnp.zeros_like(zbuf)

        def zero_gran(row):
            return pltpu.make_async_copy(zbuf.at[pl.ds(0, MOE_GRAN)], xs_hbm.at[pl.ds(row, MOE_GRAN)], zsem)

        def zero_block(blk):
            return pltpu.make_async_copy(zbuf, xs_hbm.at[pl.ds(pl.multiple_of(blk * tb, tb), tb)], zsem)

        def tails(e, c):
            def one(g, c2):
                zero_gran(pl.multiple_of(tstart_ref[e] + g * MOE_GRAN, MOE_GRAN)).start()
                return c2
            lax.fori_loop(0, tgran_ref[e], one, 0)
            return c

        def start_block(blk, c):
            zero_block(blk).start()
            return c

        def wait_gran(g, c):
            zero_gran(0).wait()
            return c

        def wait_block(blk, c):
            zero_block(0).wait()
            return c

        lax.fori_loop(0, N_EXPERTS, tails, 0)
        lax.fori_loop(n_used, n_blocks, start_block, 0)
        lax.fori_loop(0, misc_ref[1], wait_gran, 0)
        lax.fori_loop(n_used, n_blocks, wait_block, 0)


def _dispatch(h2, srow_t, plan, tm):
    n, d = h2.shape
    nt = n // tm
    rows = _stage_rows(tm)
    misc = jnp.concatenate([plan["n_used"], plan["tot_tail"]])
    grid_spec = pltpu.PrefetchScalarGridSpec(
        num_scalar_prefetch=5,
        grid=(nt,),
        in_specs=[
            pl.BlockSpec((tm, d), lambda t, *_: (t, 0)),
            pl.BlockSpec((8, tm), lambda t, *_: (0, t)),
        ],
        out_specs=pl.BlockSpec(memory_space=pl.ANY),
        scratch_shapes=[pltpu.VMEM((2, rows, d), BF16), pltpu.VMEM((MOE_BLOCK, d), BF16),
                        pltpu.SemaphoreType.DMA((2,)), pltpu.SemaphoreType.DMA(())],
    )
    return pl.pallas_call(
        functools.partial(_dispatch_kernel, n_blocks=plan["nb"]),
        grid_spec=grid_spec,
        out_shape=jax.ShapeDtypeStruct((plan["nb"] * MOE_BLOCK, d), BF16),
        compiler_params=_cparams(("arbitrary",)),
    )(plan["gran_dst"], plan["tot_gran"], plan["tail_start"], plan["tail_gran"], misc, h2, srow_t)


def _expert_kernel(be_ref, nused_ref, nexte_ref, x_ref, w1_hbm, b1_ref, w2_hbm, b2_ref, o_ref,
                   w1f, w2f, w1b, w2b, sem):
    de = w2f.shape[0]
    j = pl.program_id(0)
    live = j < nused_ref[0]
    e = be_ref[j]
    new_expert = jnp.logical_or(j == 0, e != be_ref[jnp.maximum(j - 1, 0)])

    def fetch(ex):
        return (pltpu.make_async_copy(w1_hbm.at[ex], w1f, sem.at[0]),
                pltpu.make_async_copy(w2_hbm.at[ex], w2f, sem.at[1]))

    @pl.when(jnp.logical_and(live, j == 0))
    def _():
        for c in fetch(e):
            c.start()

    @pl.when(jnp.logical_and(live, new_expert))
    def _():
        for c in fetch(e):
            c.wait()
        w1b[...] = w1f[...].astype(BF16)
        w2b[...] = w2f[...].astype(BF16)
        nxt = nexte_ref[e]

        @pl.when(nxt != e)
        def _():
            for c in fetch(nxt):
                c.start()

    @pl.when(live)
    def _():
        hm = jnp.dot(x_ref[...], w1b[...], preferred_element_type=F32) + b1_ref[0]
        gate = jnp.minimum(hm[:, :de], SWIGLU_LIMIT)
        up = jnp.clip(hm[:, de:], -SWIGLU_LIMIT, SWIGLU_LIMIT)
        act = gate * _sigmoid(SWIGLU_ALPHA * gate) * (up + 1.0)
        y = jnp.dot(act.astype(BF16), w2b[...], preferred_element_type=F32) + b2_ref[0]
        o_ref[...] = y.astype(o_ref.dtype)

    @pl.when(jnp.logical_not(live))
    def _():
        o_ref[...] = jnp.zeros_like(o_ref)


def _experts(xs, plan, w1, b1, w2, b2):
    d = xs.shape[1]
    tb = MOE_BLOCK
    f2 = w1.shape[2]
    de = w2.shape[1]
    last = lambda j, nu: jnp.maximum(jnp.minimum(j, nu[0] - 1), 0)
    grid_spec = pltpu.PrefetchScalarGridSpec(
        num_scalar_prefetch=3,
        grid=(plan["nb"],),
        in_specs=[
            pl.BlockSpec((tb, d), lambda j, be, nu, ne: (last(j, nu), 0)),
            pl.BlockSpec(memory_space=pl.ANY),
            pl.BlockSpec((1, 1, f2), lambda j, be, nu, ne: (be[j], 0, 0)),
            pl.BlockSpec(memory_space=pl.ANY),
            pl.BlockSpec((1, 1, d), lambda j, be, nu, ne: (be[j], 0, 0)),
        ],
        out_specs=pl.BlockSpec((tb, d), lambda j, be, nu, ne: (j, 0)),
        scratch_shapes=[pltpu.VMEM((d, f2), F32), pltpu.VMEM((de, d), F32),
                        pltpu.VMEM((d, f2), BF16), pltpu.VMEM((de, d), BF16),
                        pltpu.SemaphoreType.DMA((2,))],
    )
    return pl.pallas_call(
        _expert_kernel,
        grid_spec=grid_spec,
        out_shape=jax.ShapeDtypeStruct(xs.shape, BF16),
        compiler_params=_cparams(("arbitrary",)),
    )(plan["block_e"], plan["n_used"], plan["next_expert"], xs, w1, b1, w2, b2)


def _combine_kernel(gdst_ref, totg_ref, x1_ref, srow_ref, wt_ref, g_ref, yb_hbm, o_ref, stage, sem):
    t = pl.program_id(0)
    slot = t % 2
    tm = x1_ref.shape[0]
    rows = stage.shape[1]

    def copy_into(s):
        def copy(stage_row, sorted_row, nrows):
            return pltpu.make_async_copy(yb_hbm.at[pl.ds(sorted_row, nrows)],
                                         stage.at[s, pl.ds(stage_row, nrows)], sem.at[s])
        return copy

    @pl.when(t == 0)
    def _():
        stage[...] = jnp.zeros_like(stage)
        _granule_copies(t, gdst_ref, totg_ref, copy_into(slot), start=True)

    @pl.when(t + 1 < pl.num_programs(0))
    def _():
        _granule_copies(t + 1, gdst_ref, totg_ref, copy_into(1 - slot), start=True)

    wt = wt_ref[...]
    srows = [srow_ref[:, k:k + 1] for k in range(TOP_K)]
    chunk = LANES
    row_id = lax.broadcasted_iota(jnp.int32, (chunk, rows), 1).astype(F32)
    for c in range(tm // chunk):
        sl = slice(c * chunk, (c + 1) * chunk)
        unsort = jnp.zeros((chunk, rows), F32)
        for k, srow in enumerate(srows):
            unsort = jnp.where(row_id == srow[sl], wt[sl, k:k + 1], unsort)
        if c == 0:
            _granule_copies(t, gdst_ref, totg_ref, copy_into(slot), start=False)
        y = jnp.dot(unsort.astype(BF16), stage[slot], preferred_element_type=F32)
        o_ref[sl, :] = _rms(x1_ref[sl, :] + y, g_ref[...])


def _combine(x1, yb, srow, top_w, plan, g_final, tm):
    n, d = x1.shape
    nt = n // tm
    rows = _stage_rows(tm)
    grid_spec = pltpu.PrefetchScalarGridSpec(
        num_scalar_prefetch=2,
        grid=(nt,),
        in_specs=[
            pl.BlockSpec((tm, d), lambda t, *_: (t, 0)),
            pl.BlockSpec((tm, LANES), lambda t, *_: (t, 0)),
            pl.BlockSpec((tm, LANES), lambda t, *_: (t, 0)),
            pl.BlockSpec((1, d), lambda t, *_: (0, 0)),
            pl.BlockSpec(memory_space=pl.ANY),
        ],
        out_specs=pl.BlockSpec((tm, d), lambda t, *_: (t, 0)),
        scratch_shapes=[pltpu.VMEM((2, rows, d), BF16), pltpu.SemaphoreType.DMA((2,))],
    )
    return pl.pallas_call(
        _combine_kernel,
        grid_spec=grid_spec,
        out_shape=jax.ShapeDtypeStruct((n, d), F32),
        compiler_params=_cparams(("arbitrary",)),
    )(plan["gran_dst"], plan["tot_gran"], x1, srow, top_w, g_final, yb)


def _rope_tables(seq):
    half = ATTN_HEAD_DIM // 2
    inv_freq = np.float32(ROPE_THETA) ** (-np.arange(0, half, 2, dtype=np.float32) / np.float32(half))
    ang_row = np.arange(seq // GRID_W, dtype=np.float32)[:, None] * inv_freq
    ang_col = np.arange(GRID_W, dtype=np.float32)[:, None] * inv_freq
    by_row = lambda t: jnp.repeat(jnp.asarray(t, F32), GRID_W, axis=0)
    by_col = lambda t: jnp.tile(jnp.asarray(t, F32), (seq // GRID_W, 1))
    cos_r, sin_r = by_row(np.cos(ang_row)), by_row(np.sin(ang_row))
    cos_c, sin_c = by_col(np.cos(ang_col)), by_col(np.sin(ang_col))
    cos_t = jnp.concatenate([cos_r, cos_c] * 2, axis=-1)
    sin_t = jnp.concatenate([-sin_r, -sin_c, sin_r, sin_c], axis=-1)
    return cos_t, sin_t


def _token_mixer(x2, batch, seq, g_mix, w_in, conv_w, conv_b, dt_bias_f, dt_bias_b, a_log_f, a_log_b, d_skip,
                 g_ssm, q_norm_g, k_norm_g, w_br_ssm, w_br_attn, w_out):
    n, d = x2.shape
    z_end = SSM_INNER
    xbc_end = z_end + CONV_CH
    dtf_end = xbc_end + SSM_HEADS
    dtb_end = dtf_end + SSM_HEADS
    q_end = dtb_end + ATTN_HEADS * ATTN_HEAD_DIM
    k_end = q_end + ATTN_KV_HEADS * ATTN_HEAD_DIM
    v_end = k_end + ATTN_KV_HEADS * ATTN_HEAD_DIM
    head_cols = lambda w: _rope_head_order(w.reshape(d, -1, ATTN_HEAD_DIM)).reshape(d, -1)
    w_main = jnp.concatenate([w_in[:, :z_end], head_cols(w_in[:, dtb_end:q_end]), w_in[:, v_end:],
                              w_in[:, z_end:xbc_end], head_cols(w_in[:, q_end:k_end]), w_in[:, k_end:v_end]],
                             axis=1).astype(BF16)
    w_dt = jnp.pad(w_in[:, xbc_end:dtb_end], ((0, 0), (0, LANES - 2 * SSM_HEADS))).astype(BF16)

    proj, dt = _in_proj(x2, g_mix.reshape(1, d), w_main, w_dt)
    proj3 = proj.reshape(batch, seq, PROJ_COLS)

    xbc = _conv(proj3, conv_w, conv_b.reshape(1, CONV_CH))

    dt3 = dt.reshape(batch, seq, LANES)
    dtt3 = jnp.swapaxes(dt3[:, :, :2 * SSM_HEADS], 1, 2)
    bias = jnp.concatenate([dt_bias_f, dt_bias_b])
    alog = jnp.concatenate([a_log_f, a_log_b])
    pad_row = lambda v: jnp.pad(v, (0, LANES - 2 * SSM_HEADS)).reshape(1, LANES)
    y_f, y_b = _ssd(xbc, dt3, dtt3, pad_row(bias), bias.reshape(-1, 1), pad_row(alog), alog.reshape(-1, 1))

    cos_t, sin_t = _rope_tables(seq)
    q_rot, k_rot = _qk_prep(proj, cos_t, sin_t, _rope_head_order(q_norm_g).reshape(1, -1),
                            _rope_head_order(k_norm_g).reshape(1, -1), seq)
    vt3 = jnp.swapaxes(proj3[:, :, V_OFF:V_OFF + ATTN_KV_HEADS * ATTN_HEAD_DIM], 1, 2)
    attn = _flash(q_rot.reshape(batch, seq, -1), k_rot.reshape(batch, seq, -1), vt3, q_norm_g, k_norm_g)

    return _merge(y_f.reshape(n, -1), y_b.reshape(n, -1), xbc.reshape(n, CONV_CH), proj, attn.reshape(n, -1),
                  x2, jnp.repeat(d_skip, SSM_HEAD_DIM).reshape(1, -1), g_ssm.reshape(1, -1),
                  w_br_ssm.astype(BF16), w_br_attn.astype(BF16), w_out.astype(BF16))


def _moe_and_final_norm(x1, g_ffn, w_router, b_router, w_mlp1, b_mlp1, w_mlp2, b_mlp2, g_final):
    n, d = x1.shape
    tm = min(MOE_TILE, n)
    w_t = w_router.T
    w_hi = w_t.astype(BF16)
    w_lo = (w_t - w_hi.astype(F32)).astype(BF16)
    g_ffn_row = g_ffn.reshape(1, d)
    h2, srow_t, wt_t, cnt = _router(x1, g_ffn_row, jnp.concatenate([w_hi, w_lo, w_hi], axis=1),
                                    b_router.reshape(N_EXPERTS, 1))
    plan = _routing_plan(cnt[:, :, 0].astype(jnp.int32), n, tm, MOE_BLOCK)
    lanes_0_3 = lambda a: jnp.pad(a[:TOP_K].T, ((0, 0), (0, LANES - TOP_K)))
    xs = _dispatch(h2, srow_t, plan, tm)
    yb = _experts(xs, plan, w_mlp1, b_mlp1[:, None, :], w_mlp2, b_mlp2[:, None, :])
    return _combine(x1, yb, lanes_0_3(srow_t), lanes_0_3(wt_t), plan, g_final.reshape(1, d), tm)


def kernel(x, g_mix, w_in, conv_w, conv_b, dt_bias_f, dt_bias_b, a_log_f, a_log_b, d_skip, g_ssm, q_norm_g,
           k_norm_g, w_br_ssm, w_br_attn, w_out, g_ffn, w_router, b_router, w_mlp1, b_mlp1, w_mlp2, b_mlp2,
           g_final):
    batch, seq, d = x.shape
    assert g_mix.shape[0] == 1, "single-layer model: the final rmsnorm is fused into the MoE combine"
    x2 = x.reshape(batch * seq, d)
    x1 = _token_mixer(x2, batch, seq, g_mix[0], w_in[0], conv_w[0], conv_b[0], dt_bias_f[0], dt_bias_b[0],
                      a_log_f[0], a_log_b[0], d_skip[0], g_ssm[0], q_norm_g[0], k_norm_g[0], w_br_ssm[0],
                      w_br_attn[0], w_out[0])
    out = _moe_and_final_norm(x1, g_ffn[0], w_router[0], b_router[0], w_mlp1[0], b_mlp1[0], w_mlp2[0],
                              b_mlp2[0], g_final)
    return out.reshape(batch, seq, d)
```

```python
import functools
import math

import jax
import jax.numpy as jnp
import numpy as np
from jax import lax
from jax.experimental import pallas as pl
from jax.experimental.pallas import tpu as pltpu

F32 = jnp.float32
BF16 = jnp.bfloat16

NORM_EPS = 1e-6
GRID_W = 64
SSM_HEADS = 16
SSM_HEAD_DIM = 64
SSM_INNER = SSM_HEADS * SSM_HEAD_DIM
SSM_GROUPS = 2
SSM_STATE = 128
SSM_CONV = 5
CONV_CH = SSM_INNER + 2 * SSM_GROUPS * SSM_STATE
ATTN_HEADS = 8
ATTN_KV_HEADS = 2
ATTN_HEAD_DIM = 128
ROPE_THETA = 10000.0
N_EXPERTS = 32
TOP_K = 4
SWIGLU_LIMIT = 7.0
SWIGLU_ALPHA = 1.702

LANES = 128
BF16_SUBLANES = 16
VMEM_LIMIT = 56 * 1024 * 1024

Z_OFF, Q_OFF, GATE_OFF, XBC_OFF = 0, 1024, 2048, 4096
K_OFF, V_OFF, PROJ_COLS = 5632, 5888, 6144

MOE_TILE = 512
MOE_BLOCK = 512
MOE_GRAN = BF16_SUBLANES


def _cparams(sem):
    return pltpu.CompilerParams(dimension_semantics=sem, vmem_limit_bytes=VMEM_LIMIT)


def _sigmoid(x):
    return 1.0 / (1.0 + jnp.exp(-x))


def _softplus(x):
    return jnp.maximum(x, 0.0) + jnp.log(1.0 + jnp.exp(-jnp.abs(x)))


def _rms(x, g):
    ms = jnp.mean(x * x, axis=-1, keepdims=True)
    return x * lax.rsqrt(ms + NORM_EPS) * g


def _inproj_kernel(x_ref, g_ref, w_ref, wdt_ref, o_ref, dt_ref, h_scr):
    @pl.when(pl.program_id(1) == 0)
    def _():
        hb = _rms(x_ref[...], g_ref[...]).astype(BF16)
        h_scr[...] = hb
        dt_ref[...] = jnp.dot(hb, wdt_ref[...], preferred_element_type=F32)

    o_ref[...] = jnp.dot(h_scr[...], w_ref[...], preferred_element_type=F32).astype(o_ref.dtype)


def _in_proj(x2, g_mix, w_main, w_dt):
    n, d = x2.shape
    tm = min(1024, n)
    tn = 2048
    return pl.pallas_call(
        _inproj_kernel,
        grid=(n // tm, PROJ_COLS // tn),
        in_specs=[
            pl.BlockSpec((tm, d), lambda i, j: (i, 0)),
            pl.BlockSpec((1, d), lambda i, j: (0, 0)),
            pl.BlockSpec((d, tn), lambda i, j: (0, j)),
            pl.BlockSpec((d, LANES), lambda i, j: (0, 0)),
        ],
        out_specs=[
            pl.BlockSpec((tm, tn), lambda i, j: (i, j)),
            pl.BlockSpec((tm, LANES), lambda i, j: (i, 0)),
        ],
        out_shape=[
            jax.ShapeDtypeStruct((n, PROJ_COLS), BF16),
            jax.ShapeDtypeStruct((n, LANES), F32),
        ],
        scratch_shapes=[pltpu.VMEM((tm, d), BF16)],
        compiler_params=_cparams(("arbitrary", "arbitrary")),
    )(x2, g_mix, w_main, w_dt)


CONV_HALO = 64
CONV_ROWS = 128


def _conv_kernel(prev_ref, cur_ref, next_ref, shift_ref, w_ref, b_ref, o_ref):
    s = pl.program_id(1)
    ts = cur_ref.shape[1]
    prev = prev_ref[0]
    nxt = next_ref[0]
    zero = jnp.zeros_like(prev)
    ext = jnp.concatenate([jnp.where(s == 0, zero, prev), cur_ref[0],
                           jnp.where(s == pl.num_programs(1) - 1, zero, nxt)], axis=0)
    shift = shift_ref[...]
    for rb in range(ts // CONV_ROWS):
        lo = rb * CONV_ROWS
        taps = jnp.dot(shift, ext[lo:lo + CONV_ROWS + 2 * CONV_HALO], preferred_element_type=F32)
        acc = b_ref[...] + w_ref[0:1, :] * taps[0:CONV_ROWS]
        for k in range(1, SSM_CONV):
            acc = acc + w_ref[k:k + 1, :] * taps[k * CONV_ROWS:(k + 1) * CONV_ROWS]
        o_ref[0, lo:lo + CONV_ROWS, :] = (acc * _sigmoid(acc)).astype(o_ref.dtype)


def _conv(proj3, conv_w, conv_b):
    b, s, _ = proj3.shape
    ts = min(2048, s)
    tc = 512
    halo = CONV_HALO
    hb = ts // halo
    col0 = XBC_OFF // tc
    pad = (SSM_CONV - 1) // 2
    out_row = np.arange(SSM_CONV * CONV_ROWS)
    src = out_row % CONV_ROWS + halo + out_row // CONV_ROWS - pad
    shift = jnp.asarray(src[:, None] == np.arange(CONV_ROWS + 2 * halo)[None, :], BF16)
    return pl.pallas_call(
        _conv_kernel,
        grid=(b, s // ts, CONV_CH // tc),
        in_specs=[
            pl.BlockSpec((1, halo, tc), lambda bi, si, ci: (bi, jnp.maximum(si * hb - 1, 0), col0 + ci)),
            pl.BlockSpec((1, ts, tc), lambda bi, si, ci: (bi, si, col0 + ci)),
            pl.BlockSpec((1, halo, tc),
                         lambda bi, si, ci: (bi, jnp.minimum((si + 1) * hb, s // halo - 1), col0 + ci)),
            pl.BlockSpec(shift.shape, lambda bi, si, ci: (0, 0)),
            pl.BlockSpec((SSM_CONV, tc), lambda bi, si, ci: (0, ci)),
            pl.BlockSpec((1, tc), lambda bi, si, ci: (0, ci)),
        ],
        out_specs=pl.BlockSpec((1, ts, tc), lambda bi, si, ci: (bi, si, ci)),
        out_shape=jax.ShapeDtypeStruct((b, s, CONV_CH), BF16),
        compiler_params=_cparams(("arbitrary", "arbitrary", "arbitrary")),
    )(proj3, proj3, proj3, shift, conv_w, conv_b)


SSM_CHUNKS_PER_STEP = 4


def _ssd_kernel(xf_ref, xb_ref, dtf_ref, dtb_ref, dttf_ref, dttb_ref, brow_ref, bcol_ref,
                arow_ref, acol_ref, yf_ref, yb_ref, st_ref):
    L = SSM_STATE
    nsub = xf_ref.shape[1] // L
    hg = SSM_HEADS // SSM_GROUPS
    pairs = hg // 2

    @pl.when(pl.program_id(1) == 0)
    def _():
        st_ref[...] = jnp.zeros_like(st_ref)

    rows = lax.broadcasted_iota(jnp.int32, (L, L), 0)
    cols = lax.broadcasted_iota(jnp.int32, (L, L), 1)
    lower = rows >= cols
    upper = rows <= cols
    ltri = lower.astype(BF16)
    utri = upper.astype(BF16)
    lane = lax.broadcasted_iota(jnp.int32, (L, LANES), 1)
    first_half = lane < SSM_HEAD_DIM
    lane1 = lax.broadcasted_iota(jnp.int32, (1, LANES), 1)
    log2e = math.log2(math.e)
    a_row = -jnp.exp(arow_ref[...]) * log2e
    a_col = -jnp.exp(acol_ref[...]) * log2e

    def split3(v):
        hi = v.astype(BF16)
        r1 = v - hi.astype(F32)
        mid = r1.astype(BF16)
        return hi, mid, (r1 - mid.astype(F32)).astype(BF16)

    def cumsum_cols(tri, v):
        return jnp.dot(jnp.concatenate([tri] * 3, axis=1), jnp.concatenate(split3(v), axis=0),
                       preferred_element_type=F32)

    def cumsum_rows(v, tri):
        return jnp.dot(jnp.concatenate(split3(v), axis=1), jnp.concatenate([tri] * 3, axis=0),
                       preferred_element_type=F32)

    for sub, d in [(sub, d) for sub in range(nsub) for d in range(2)]:
        x_ref, dt_ref, dtt_ref, y_ref = ((xf_ref, dtf_ref, dttf_ref, yf_ref) if d == 0
                                         else (xb_ref, dtb_ref, dttb_ref, yb_ref))
        r0 = (sub if d == 0 else nsub - 1 - sub) * L
        rs = slice(r0, r0 + L)
        a = _softplus(dt_ref[0, rs, :] + brow_ref[...]) * a_row
        dt_t = _softplus(dtt_ref[0, :, rs] + bcol_ref[...])
        a_t = dt_t * a_col
        if d == 0:
            cs_col = cumsum_cols(ltri, a)
            cs_row = cumsum_rows(a_t, utri)
            tot = cs_col[L - 1:L, :]
            tot_t = cs_row[:, L - 1:L]
            mask = lower
        else:
            cs_col = cumsum_cols(utri, a)
            cs_row = cumsum_rows(a_t, ltri)
            tot = cs_col[0:1, :]
            tot_t = cs_row[:, 0:1]
            mask = upper
        w_t = dt_t * jnp.exp2(tot_t - cs_row)
        src_t = cs_row - jnp.log2(dt_t)
        chunk_decay = jnp.exp2(tot)

        for g in range(SSM_GROUPS):
            boff = SSM_INNER + g * SSM_STATE
            coff = SSM_INNER + SSM_GROUPS * SSM_STATE + g * SSM_STATE
            bm = x_ref[0, rs, boff:boff + SSM_STATE]
            cm = x_ref[0, rs, coff:coff + SSM_STATE]
            cb = lax.dot_general(cm, bm, (((1,), (1,)), ((), ())), preferred_element_type=F32)
            bt = bm.astype(F32).T
            st = st_ref[d, g]
            y_off = jnp.dot(cm, st.astype(BF16), preferred_element_type=F32)
            for pr in range(pairs):
                h0 = d * SSM_HEADS + g * hg + 2 * pr
                xoff = (g * pairs + pr) * LANES
                xs = x_ref[0, rs, xoff:xoff + LANES]
                zero = jnp.zeros_like(xs)
                rhs = jnp.concatenate([jnp.where(first_half, xs, zero),
                                       jnp.where(first_half, zero, xs)], axis=0)
                ms, ws, dins = [], [], []
                for hh in (h0, h0 + 1):
                    cs_b = jnp.broadcast_to(cs_col[:, hh:hh + 1], (L, L))
                    seg = cs_b - src_t[hh:hh + 1, :]
                    m = cb * jnp.exp2(jnp.where(mask, seg, -jnp.inf))
                    ms.append(m.astype(BF16))
                    ws.append((bt * w_t[hh:hh + 1, :]).astype(BF16))
                    dins.append(jnp.exp2(cs_b))
                y = jnp.dot(jnp.concatenate(ms, axis=1), rhs, preferred_element_type=F32)
                y = y + y_off[:, pr * LANES:(pr + 1) * LANES] * jnp.where(first_half, dins[0], dins[1])
                y_ref[0, rs, xoff:xoff + LANES] = y.astype(y_ref.dtype)
                new_st = jnp.dot(jnp.concatenate(ws, axis=1), rhs, preferred_element_type=F32)
                cd = jnp.where(lane1 < SSM_HEAD_DIM, chunk_decay[:, h0:h0 + 1], chunk_decay[:, h0 + 1:h0 + 2])
                st_ref[d, g, :, pr * LANES:(pr + 1) * LANES] = st[:, pr * LANES:(pr + 1) * LANES] * cd + new_st


def _ssd(xbc, dt3, dtt3, bias_row, bias_col, alog_row, alog_col):
    b, s, _ = xbc.shape
    L = SSM_CHUNKS_PER_STEP * SSM_STATE
    nc = s // L
    hg = SSM_HEADS // SSM_GROUPS
    fwd = lambda bi, ci: (bi, ci, 0)
    bwd = lambda bi, ci: (bi, nc - 1 - ci, 0)
    fwd_t = lambda bi, ci: (bi, 0, ci)
    bwd_t = lambda bi, ci: (bi, 0, nc - 1 - ci)
    const = lambda bi, ci: (0, 0)
    return pl.pallas_call(
        _ssd_kernel,
        grid=(b, nc),
        in_specs=[
            pl.BlockSpec((1, L, CONV_CH), fwd),
            pl.BlockSpec((1, L, CONV_CH), bwd),
            pl.BlockSpec((1, L, LANES), fwd),
            pl.BlockSpec((1, L, LANES), bwd),
            pl.BlockSpec((1, 2 * SSM_HEADS, L), fwd_t),
            pl.BlockSpec((1, 2 * SSM_HEADS, L), bwd_t),
            pl.BlockSpec((1, LANES), const),
            pl.BlockSpec((2 * SSM_HEADS, 1), const),
            pl.BlockSpec((1, LANES), const),
            pl.BlockSpec((2 * SSM_HEADS, 1), const),
        ],
        out_specs=[
            pl.BlockSpec((1, L, SSM_INNER), fwd),
            pl.BlockSpec((1, L, SSM_INNER), bwd),
        ],
        out_shape=[jax.ShapeDtypeStruct((b, s, SSM_INNER), BF16)] * 2,
        scratch_shapes=[pltpu.VMEM((2, SSM_GROUPS, SSM_STATE, hg * SSM_HEAD_DIM), F32)],
        compiler_params=_cparams(("arbitrary", "arbitrary")),
    )(xbc, xbc, dt3, dt3, dtt3, dtt3, bias_row, bias_col, alog_row, alog_col)


def _rope_head_order(a):
    q4 = ATTN_HEAD_DIM // 4
    return jnp.concatenate([a[..., 0:q4], a[..., 2 * q4:3 * q4], a[..., q4:2 * q4], a[..., 3 * q4:]], axis=-1)


def _rope_norm(t, g, cos, sin_signed, ones):
    sq = t * t
    hi = sq.astype(BF16)
    lo = (sq - hi.astype(F32)).astype(BF16)
    ms = jnp.dot(jnp.concatenate([hi, lo], axis=1), ones, preferred_element_type=F32) * (1.0 / ATTN_HEAD_DIM)
    tn = t * lax.rsqrt(ms + NORM_EPS) * g
    return tn * cos + pltpu.roll(tn, ATTN_HEAD_DIM // 2, 1) * sin_signed


Q_SCALE = ATTN_HEAD_DIM ** -0.5 * math.log2(math.e)


def _qkprep_kernel(q_ref, k_ref, cos_ref, sin_ref, qg_ref, kg_ref, qo_ref, ko_ref):
    cos = cos_ref[...]
    sin = sin_ref[...]
    ones = jnp.ones((2 * ATTN_HEAD_DIM, ATTN_HEAD_DIM), BF16)

    def heads(src_ref, g_ref, dst_ref, n_heads, scale):
        for h in range(n_heads):
            sl = slice(h * ATTN_HEAD_DIM, (h + 1) * ATTN_HEAD_DIM)
            r = _rope_norm(src_ref[:, sl].astype(F32), g_ref[...], cos, sin, ones) * scale
            dst_ref[:, sl] = r.astype(dst_ref.dtype)

    heads(q_ref, qg_ref, qo_ref, ATTN_HEADS, Q_SCALE)
    heads(k_ref, kg_ref, ko_ref, ATTN_KV_HEADS, 1.0)


def _qk_prep(proj, cos_t, sin_t, q_norm_g, k_norm_g, seq):
    n = proj.shape[0]
    tm = min(1024, seq)
    qw = ATTN_HEADS * ATTN_HEAD_DIM
    kw = ATTN_KV_HEADS * ATTN_HEAD_DIM
    spt = seq // tm
    return pl.pallas_call(
        _qkprep_kernel,
        grid=(n // tm,),
        in_specs=[
            pl.BlockSpec((tm, qw), lambda i: (i, Q_OFF // qw)),
            pl.BlockSpec((tm, kw), lambda i: (i, K_OFF // kw)),
            pl.BlockSpec((tm, ATTN_HEAD_DIM), lambda i: (i % spt, 0)),
            pl.BlockSpec((tm, ATTN_HEAD_DIM), lambda i: (i % spt, 0)),
            pl.BlockSpec((1, ATTN_HEAD_DIM), lambda i: (0, 0)),
            pl.BlockSpec((1, ATTN_HEAD_DIM), lambda i: (0, 0)),
        ],
        out_specs=[
            pl.BlockSpec((tm, qw), lambda i: (i, 0)),
            pl.BlockSpec((tm, kw), lambda i: (i, 0)),
        ],
        out_shape=[jax.ShapeDtypeStruct((n, qw), BF16), jax.ShapeDtypeStruct((n, kw), BF16)],
        compiler_params=_cparams(("arbitrary",)),
    )(proj, proj, cos_t, sin_t, q_norm_g, k_norm_g)


def _flash_kernel(small_ref, q_ref, k_ref, vt_ref, o_ref, *, tk, th):
    q = q_ref[0]
    tq = q.shape[0]
    seq = k_ref.shape[1]
    nt = (((1,), (1,)), ((), ()))
    small = small_ref[0] != 0

    @pl.when(small)
    def _():
        l = acc_t = None
        for c in range(seq // th):
            ks = slice(c * th, (c + 1) * th)
            p_t = jnp.exp2(lax.dot_general(k_ref[0, ks, :], q, nt, preferred_element_type=F32))
            l_c = jnp.sum(p_t, axis=0, keepdims=True)
            a_c = jnp.dot(vt_ref[0, :, ks], p_t.astype(BF16), preferred_element_type=F32)
            l = l_c if l is None else l + l_c
            acc_t = a_c if acc_t is None else acc_t + a_c
        o_ref[0] = (acc_t / l).T.astype(o_ref.dtype)

    @pl.when(jnp.logical_not(small))
    def _():
        def body(i, carry):
            m, l, acc_t = carry
            off = pl.multiple_of(i * tk, tk)
            s_t = lax.dot_general(k_ref[0, pl.ds(off, tk), :], q, nt, preferred_element_type=F32)
            m_new = jnp.maximum(m, jnp.max(s_t, axis=0, keepdims=True))
            alpha = jnp.exp2(m - m_new)
            p_t = jnp.exp2(s_t - m_new)
            l = alpha * l + jnp.sum(p_t, axis=0, keepdims=True)
            acc_t = alpha * acc_t + jnp.dot(vt_ref[0, :, pl.ds(off, tk)], p_t.astype(BF16),
                                            preferred_element_type=F32)
            return m_new, l, acc_t

        init = (jnp.full((1, tq), -jnp.inf, F32), jnp.zeros((1, tq), F32),
                jnp.zeros((ATTN_HEAD_DIM, tq), F32))
        _, l, acc_t = lax.fori_loop(0, seq // tk, body, init)
        o_ref[0] = (acc_t / l).T.astype(o_ref.dtype)


SCORE_BOUND = 59.0


def _flash(q3, k3, vt3, q_norm_g, k_norm_g):
    b, s, _ = q3.shape
    tq = min(1024, s)
    tk = min(2048, s)
    th = min(4096, s)
    nq = s // tq
    grp = ATTN_HEADS // ATTN_KV_HEADS
    hd = ATTN_HEAD_DIM
    bound = hd * Q_SCALE * jnp.max(jnp.abs(q_norm_g)) * jnp.max(jnp.abs(k_norm_g)) * 1.02
    small = (bound <= SCORE_BOUND).astype(jnp.int32).reshape(1)
    grid_spec = pltpu.PrefetchScalarGridSpec(
        num_scalar_prefetch=1,
        grid=(b, ATTN_HEADS, nq),
        in_specs=[
            pl.BlockSpec((1, tq, hd), lambda bi, h, qi, sm: (bi, qi, h)),
            pl.BlockSpec((1, s, hd), lambda bi, h, qi, sm: (bi, 0, h // grp)),
            pl.BlockSpec((1, hd, s), lambda bi, h, qi, sm: (bi, h // grp, 0)),
        ],
        out_specs=pl.BlockSpec((1, tq, hd), lambda bi, h, qi, sm: (bi, qi, h)),
    )
    return pl.pallas_call(
        functools.partial(_flash_kernel, tk=tk, th=th),
        grid_spec=grid_spec,
        out_shape=jax.ShapeDtypeStruct((b, s, ATTN_HEADS * hd), BF16),
        compiler_params=_cparams(("arbitrary", "arbitrary", "arbitrary")),
    )(small, q3, k3, vt3)


def _merge_kernel(yf_ref, yb_ref, xs_ref, z_ref, gate_ref, attn_ref, x_ref, dskip_ref, gssm_ref,
                  wbs_ref, wba_ref, wo_ref, x1_ref):
    d = x_ref.shape[1]
    xs = xs_ref[...].astype(F32)
    y = yf_ref[...].astype(F32) + yb_ref[...].astype(F32) + xs * dskip_ref[...]
    z = z_ref[...].astype(F32)
    y = _rms(y * (z * _sigmoid(z)), gssm_ref[...])
    br_ssm = jnp.dot(y.astype(BF16), wbs_ref[...], preferred_element_type=F32)
    br_attn = jnp.dot(attn_ref[...], wba_ref[...], preferred_element_type=F32)
    g_s = _sigmoid(gate_ref[:, :d].astype(F32))
    g_a = _sigmoid(gate_ref[:, d:].astype(F32))
    merged = (g_s * br_ssm + g_a * br_attn).astype(BF16)
    x1_ref[...] = x_ref[...] + jnp.dot(merged, wo_ref[...], preferred_element_type=F32)


def _merge(y_f, y_b, xbc, proj, attn, x2, dskip_row, g_ssm, w_br_ssm, w_br_attn, w_out):
    n, d = x2.shape
    tm = min(512, n)
    row = lambda i: (i, 0)
    const = lambda i: (0, 0)
    return pl.pallas_call(
        _merge_kernel,
        grid=(n // tm,),
        in_specs=[
            pl.BlockSpec((tm, d), row),
            pl.BlockSpec((tm, d), row),
            pl.BlockSpec((tm, d), row),
            pl.BlockSpec((tm, d), lambda i: (i, Z_OFF // d)),
            pl.BlockSpec((tm, 2 * d), lambda i: (i, GATE_OFF // (2 * d))),
            pl.BlockSpec((tm, d), row),
            pl.BlockSpec((tm, d), row),
            pl.BlockSpec((1, d), const),
            pl.BlockSpec((1, d), const),
            pl.BlockSpec((d, d), const),
            pl.BlockSpec((d, d), const),
            pl.BlockSpec((d, d), const),
        ],
        out_specs=pl.BlockSpec((tm, d), row),
        out_shape=jax.ShapeDtypeStruct((n, d), F32),
        compiler_params=_cparams(("arbitrary",)),
    )(y_f, y_b, xbc, proj, proj, attn, x2, dskip_row, g_ssm, w_br_ssm, w_br_attn, w_out)


def _router_kernel(x1_ref, g_ref, w_ref, b_ref, h_ref, srow_ref, wt_ref, cnt_ref):
    tm = x1_ref.shape[0]
    h = _rms(x1_ref[...], g_ref[...])
    h_hi = h.astype(BF16)
    h_lo = (h - h_hi.astype(F32)).astype(BF16)
    h_ref[...] = h_hi
    logits = lax.dot_general(w_ref[...], jnp.concatenate([h_hi, h_hi, h_lo], axis=1), (((1,), (1,)), ((), ())),
                             preferred_element_type=F32) + b_ref[...]
    expert = lax.broadcasted_iota(jnp.int32, logits.shape, 0)
    slot = lax.broadcasted_iota(jnp.int32, wt_ref.shape, 0)
    chosen = jnp.zeros(logits.shape, F32)
    vals, hits = [], []
    for k in range(TOP_K):
        m = jnp.max(logits, axis=0, keepdims=True)
        idx = jnp.min(jnp.where(logits == m, expert, N_EXPERTS), axis=0, keepdims=True)
        vals.append(m)
        hit = expert == idx
        hits.append(hit)
        chosen = jnp.where(hit, 1.0, chosen)
        logits = jnp.where(hit, -jnp.inf, logits)
    es = [jnp.exp(v - vals[0]) for v in vals]
    tot = es[0] + es[1] + es[2] + es[3]

    chosen16 = chosen.astype(BF16)
    cnt_col = jnp.sum(chosen, axis=1, keepdims=True)
    cnt_row = lax.dot_general(jnp.ones((8, tm), BF16), chosen16, (((1,), (1,)), ((), ())),
                              preferred_element_type=F32)[0:1, :]
    seg_row = jnp.ceil(cnt_row * (1.0 / MOE_GRAN)) * MOE_GRAN
    before = (lax.broadcasted_iota(jnp.int32, (N_EXPERTS, N_EXPERTS), 1)
              < lax.broadcasted_iota(jnp.int32, (N_EXPERTS, N_EXPERTS), 0))
    seg_off = jnp.sum(jnp.where(before, seg_row, 0.0), axis=1, keepdims=True)
    earlier = (lax.broadcasted_iota(jnp.int32, (tm, tm), 0)
               < lax.broadcasted_iota(jnp.int32, (tm, tm), 1)).astype(BF16)
    pos = jnp.dot(chosen16, earlier, preferred_element_type=F32) + seg_off

    val_out = jnp.zeros(wt_ref.shape, F32)
    row_out = jnp.zeros(srow_ref.shape, F32)
    for k in range(TOP_K):
        val_out = jnp.where(slot == k, es[k] / tot, val_out)
        row_out = jnp.where(slot == k, jnp.sum(jnp.where(hits[k], pos, 0.0), axis=0, keepdims=True), row_out)
    wt_ref[...] = val_out
    srow_ref[...] = row_out
    cnt_ref[0] = jnp.broadcast_to(cnt_col, cnt_ref.shape[1:])


def _router(x1, g_ffn, w_router_t, b_router_col):
    n, d = x1.shape
    tm = min(MOE_TILE, n)
    return pl.pallas_call(
        _router_kernel,
        grid=(n // tm,),
        in_specs=[
            pl.BlockSpec((tm, d), lambda i: (i, 0)),
            pl.BlockSpec((1, d), lambda i: (0, 0)),
            pl.BlockSpec((N_EXPERTS, 3 * d), lambda i: (0, 0)),
            pl.BlockSpec((N_EXPERTS, 1), lambda i: (0, 0)),
        ],
        out_specs=[pl.BlockSpec((tm, d), lambda i: (i, 0)),
                   pl.BlockSpec((8, tm), lambda i: (0, i)),
                   pl.BlockSpec((8, tm), lambda i: (0, i)),
                   pl.BlockSpec((1, N_EXPERTS, LANES), lambda i: (i, 0, 0))],
        out_shape=[jax.ShapeDtypeStruct((n, d), BF16),
                   jax.ShapeDtypeStruct((8, n), F32),
                   jax.ShapeDtypeStruct((8, n), F32),
                   jax.ShapeDtypeStruct((n // tm, N_EXPERTS, LANES), F32)],
        compiler_params=_cparams(("arbitrary",)),
    )(x1, g_ffn, w_router_t, b_router_col)


def _routing_plan(cnt, n, tm, tb):
    nt = n // tm
    gran = MOE_GRAN
    rc = (cnt + gran - 1) // gran * gran
    covered = jnp.sum(rc, axis=0)
    region = (covered + tb - 1) // tb * tb
    pad_end = jnp.cumsum(region)
    pad_start = pad_end - region
    seg_start = pad_start[None, :] + jnp.cumsum(rc, axis=0) - rc
    n_used = pad_end[-1] // tb
    nb = (TOP_K * n + nt * N_EXPERTS * (gran - 1) + N_EXPERTS * (tb - 1) + tb - 1) // tb
    blk = jnp.arange(nb, dtype=jnp.int32)
    block_e = jnp.sum((pad_end[None, :] <= (jnp.minimum(blk, n_used - 1) * tb)[:, None]).astype(jnp.int32), axis=1)
    block_e = jnp.minimum(block_e, N_EXPERTS - 1)
    flat = lambda a: a.reshape(-1).astype(jnp.int32)
    tail_gran = (region - covered) // gran
    ngran = rc // gran
    g_end = jnp.cumsum(ngran, axis=1)
    gi = jnp.arange(_stage_rows(tm) // gran, dtype=jnp.int32)
    g_exp = jnp.minimum(jnp.sum((g_end[:, None, :] <= gi[None, :, None]).astype(jnp.int32), axis=2), N_EXPERTS - 1)
    mine = g_exp[:, :, None] == jnp.arange(N_EXPERTS, dtype=jnp.int32)[None, None, :]
    seg_base = seg_start - gran * (g_end - ngran)
    gran_dst = jnp.sum(jnp.where(mine, seg_base[:, None, :], 0), axis=2) + gran * gi[None, :]
    ids = jnp.arange(N_EXPERTS, dtype=jnp.int32)
    later = jnp.where((ids[None, :] > ids[:, None]) & (region[None, :] > 0), ids[None, :], N_EXPERTS)
    nxt = jnp.min(later, axis=1)
    next_expert = jnp.where(nxt < N_EXPERTS, nxt, ids)
    return dict(gran_dst=flat(gran_dst), tot_gran=flat(g_end[:, -1]), block_e=flat(block_e),
                next_expert=flat(next_expert),
                n_used=flat(n_used), nb=nb,
                tail_start=flat(pad_start + covered), tail_gran=flat(tail_gran),
                tot_tail=flat(jnp.sum(tail_gran)))


def _stage_rows(tm):
    rows = TOP_K * tm + N_EXPERTS * (MOE_GRAN - 1)
    return (rows + 2 * LANES - 1) // (2 * LANES) * (2 * LANES)


GRAN_UNROLL = 4


def _granule_copies(t, gdst_ref, totg_ref, make_copy, start):
    total = totg_ref[t]
    base = t * (gdst_ref.shape[0] // totg_ref.shape[0])
    full = lax.shift_right_logical(total, GRAN_UNROLL.bit_length() - 1)

    def one(i):
        make_copy(pl.multiple_of(i * MOE_GRAN, MOE_GRAN), pl.multiple_of(gdst_ref[base + i], MOE_GRAN),
                  MOE_GRAN).start()

    def group(q, c):
        if start:
            for u in range(GRAN_UNROLL):
                one(q * GRAN_UNROLL + u)
        else:
            make_copy(0, 0, GRAN_UNROLL * MOE_GRAN).wait()
        return c

    def rest(i, c):
        if start:
            one(i)
        else:
            make_copy(0, 0, MOE_GRAN).wait()
        return c

    lax.fori_loop(0, full, group, 0)
    lax.fori_loop(full * GRAN_UNROLL, total, rest, 0)


def _dispatch_kernel(gdst_ref, totg_ref, tstart_ref, tgran_ref, misc_ref,
                     h_ref, srow_ref, xs_hbm, stage, zbuf, sem, zsem, *, n_blocks):
    t = pl.program_id(0)
    slot = t % 2
    tm = h_ref.shape[0]
    rows = stage.shape[1]
    h = h_ref[...]
    srows = [srow_ref[k:k + 1, :] for k in range(TOP_K)]
    chunk = 2 * LANES
    row_id = lax.broadcasted_iota(jnp.int32, (chunk, tm), 0).astype(F32)
    for c in range(rows // chunk):
        perm = jnp.zeros((chunk, tm), F32)
        for srow in srows:
            perm = jnp.where(row_id == srow - float(c * chunk), 1.0, perm)
        stage[slot, c * chunk:(c + 1) * chunk, :] = jnp.dot(
            perm.astype(BF16), h, preferred_element_type=F32).astype(stage.dtype)

    def copy_from(s):
        def copy(stage_row, sorted_row, nrows):
            return pltpu.make_async_copy(stage.at[s, pl.ds(stage_row, nrows)],
                                         xs_hbm.at[pl.ds(sorted_row, nrows)], sem.at[s])
        return copy

    _granule_copies(t, gdst_ref, totg_ref, copy_from(slot), start=True)

    @pl.when(t > 0)
    def _():
        _granule_copies(jnp.maximum(t - 1, 0), gdst_ref, totg_ref, copy_from(1 - slot), start=False)

    @pl.when(t == pl.num_programs(0) - 1)
    def _():
        _granule_copies(t, gdst_ref, totg_ref, copy_from(slot), start=False)
        tb = zbuf.shape[0]
        n_used = misc_ref[0]
        zbuf[...] = jnp.zeros_like(zbuf)

        def zero_gran(row):
            return pltpu.make_async_copy(zbuf.at[pl.ds(0, MOE_GRAN)], xs_hbm.at[pl.ds(row, MOE_GRAN)], zsem)

        def zero_block(blk):
            return pltpu.make_async_copy(zbuf, xs_hbm.at[pl.ds(pl.multiple_of(blk * tb, tb), tb)], zsem)

        def tails(e, c):
            def one(g, c2):
                zero_gran(pl.multiple_of(tstart_ref[e] + g * MOE_GRAN, MOE_GRAN)).start()
                return c2
            lax.fori_loop(0, tgran_ref[e], one, 0)
            return c

        def start_block(blk, c):
            zero_block(blk).start()
            return c

        def wait_gran(g, c):
            zero_gran(0).wait()
            return c

        def wait_block(blk, c):
            zero_block(0).wait()
            return c

        lax.fori_loop(0, N_EXPERTS, tails, 0)
        lax.fori_loop(n_used, n_blocks, start_block, 0)
        lax.fori_loop(0, misc_ref[1], wait_gran, 0)
        lax.fori_loop(n_used, n_blocks, wait_block, 0)


def _dispatch(h2, srow_t, plan, tm):
    n, d = h2.shape
    nt = n // tm
    rows = _stage_rows(tm)
    misc = jnp.concatenate([plan["n_used"], plan["tot_tail"]])
    grid_spec = pltpu.PrefetchScalarGridSpec(
        num_scalar_prefetch=5,
        grid=(nt,),
        in_specs=[
            pl.BlockSpec((tm, d), lambda t, *_: (t, 0)),
            pl.BlockSpec((8, tm), lambda t, *_: (0, t)),
        ],
        out_specs=pl.BlockSpec(memory_space=pl.ANY),
        scratch_shapes=[pltpu.VMEM((2, rows, d), BF16), pltpu.VMEM((MOE_BLOCK, d), BF16),
                        pltpu.SemaphoreType.DMA((2,)), pltpu.SemaphoreType.DMA(())],
    )
    return pl.pallas_call(
        functools.partial(_dispatch_kernel, n_blocks=plan["nb"]),
        grid_spec=grid_spec,
        out_shape=jax.ShapeDtypeStruct((plan["nb"] * MOE_BLOCK, d), BF16),
        compiler_params=_cparams(("arbitrary",)),
    )(plan["gran_dst"], plan["tot_gran"], plan["tail_start"], plan["tail_gran"], misc, h2, srow_t)


def _expert_kernel(be_ref, nused_ref, nexte_ref, x_ref, w1_hbm, b1_ref, w2_hbm, b2_ref, o_ref,
                   w1f, w2f, w1b, w2b, sem):
    de = w2f.shape[0]
    j = pl.program_id(0)
    live = j < nused_ref[0]
    e = be_ref[j]
    new_expert = jnp.logical_or(j == 0, e != be_ref[jnp.maximum(j - 1, 0)])

    def fetch(ex):
        return (pltpu.make_async_copy(w1_hbm.at[ex], w1f, sem.at[0]),
                pltpu.make_async_copy(w2_hbm.at[ex], w2f, sem.at[1]))

    @pl.when(jnp.logical_and(live, j == 0))
    def _():
        for c in fetch(e):
            c.start()

    @pl.when(jnp.logical_and(live, new_expert))
    def _():
        for c in fetch(e):
            c.wait()
        w1b[...] = w1f[...].astype(BF16)
        w2b[...] = w2f[...].astype(BF16)
        nxt = nexte_ref[e]

        @pl.when(nxt != e)
        def _():
            for c in fetch(nxt):
                c.start()

    @pl.when(live)
    def _():
        hm = jnp.dot(x_ref[...], w1b[...], preferred_element_type=F32) + b1_ref[0]
        gate = jnp.minimum(hm[:, :de], SWIGLU_LIMIT)
        up = jnp.clip(hm[:, de:], -SWIGLU_LIMIT, SWIGLU_LIMIT)
        act = gate * _sigmoid(SWIGLU_ALPHA * gate) * (up + 1.0)
        y = jnp.dot(act.astype(BF16), w2b[...], preferred_element_type=F32) + b2_ref[0]
        o_ref[...] = y.astype(o_ref.dtype)

    @pl.when(jnp.logical_not(live))
    def _():
        o_ref[...] = jnp.zeros_like(o_ref)


def _experts(xs, plan, w1, b1, w2, b2):
    d = xs.shape[1]
    tb = MOE_BLOCK
    f2 = w1.shape[2]
    de = w2.shape[1]
    last = lambda j, nu: jnp.maximum(jnp.minimum(j, nu[0] - 1), 0)
    grid_spec = pltpu.PrefetchScalarGridSpec(
        num_scalar_prefetch=3,
        grid=(plan["nb"],),
        in_specs=[
            pl.BlockSpec((tb, d), lambda j, be, nu, ne: (last(j, nu), 0)),
            pl.BlockSpec(memory_space=pl.ANY),
            pl.BlockSpec((1, 1, f2), lambda j, be, nu, ne: (be[j], 0, 0)),
            pl.BlockSpec(memory_space=pl.ANY),
            pl.BlockSpec((1, 1, d), lambda j, be, nu, ne: (be[j], 0, 0)),
        ],
        out_specs=pl.BlockSpec((tb, d), lambda j, be, nu, ne: (j, 0)),
        scratch_shapes=[pltpu.VMEM((d, f2), F32), pltpu.VMEM((de, d), F32),
                        pltpu.VMEM((d, f2), BF16), pltpu.VMEM((de, d), BF16),
                        pltpu.SemaphoreType.DMA((2,))],
    )
    return pl.pallas_call(
        _expert_kernel,
        grid_spec=grid_spec,
        out_shape=jax.ShapeDtypeStruct(xs.shape, BF16),
        compiler_params=_cparams(("arbitrary",)),
    )(plan["block_e"], plan["n_used"], plan["next_expert"], xs, w1, b1, w2, b2)


COMBINE_TILES = 2


def _combine_kernel(gdst_ref, totg_ref, x1_ref, srow_ref, wt_ref, g_ref, yb_hbm, o_ref, stage, sem):
    s = pl.program_id(0)
    slot = s % 2
    nu = stage.shape[1]
    tm = x1_ref.shape[0] // nu
    rows = stage.shape[2]

    def copy_into(sl_, u):
        def copy(stage_row, sorted_row, nrows):
            return pltpu.make_async_copy(yb_hbm.at[pl.ds(sorted_row, nrows)],
                                         stage.at[sl_, u, pl.ds(stage_row, nrows)], sem.at[sl_, u])
        return copy

    def fetch(step, sl_, start):
        for u in range(nu):
            _granule_copies(step * nu + u, gdst_ref, totg_ref, copy_into(sl_, u), start=start)

    @pl.when(s == 0)
    def _():
        stage[...] = jnp.zeros_like(stage)
        fetch(s, slot, True)

    @pl.when(s + 1 < pl.num_programs(0))
    def _():
        fetch(s + 1, 1 - slot, True)

    fetch(s, slot, False)

    chunk = LANES
    row_id = lax.broadcasted_iota(jnp.int32, (chunk, rows), 1).astype(F32)
    for u in range(nu):
        for c in range(tm // chunk):
            sl = slice(u * tm + c * chunk, u * tm + (c + 1) * chunk)
            unsort = jnp.zeros((chunk, rows), F32)
            for k in range(TOP_K):
                unsort = jnp.where(row_id == srow_ref[sl, k:k + 1], wt_ref[sl, k:k + 1], unsort)
            y = jnp.dot(unsort.astype(BF16), stage[slot, u], preferred_element_type=F32)
            o_ref[sl, :] = _rms(x1_ref[sl, :] + y, g_ref[...])


def _combine(x1, yb, srow, top_w, plan, g_final, tm):
    n, d = x1.shape
    nu = COMBINE_TILES
    nt = n // (nu * tm)
    rows = _stage_rows(tm)
    grid_spec = pltpu.PrefetchScalarGridSpec(
        num_scalar_prefetch=2,
        grid=(nt,),
        in_specs=[
            pl.BlockSpec((nu * tm, d), lambda t, *_: (t, 0)),
            pl.BlockSpec((nu * tm, LANES), lambda t, *_: (t, 0)),
            pl.BlockSpec((nu * tm, LANES), lambda t, *_: (t, 0)),
            pl.BlockSpec((1, d), lambda t, *_: (0, 0)),
            pl.BlockSpec(memory_space=pl.ANY),
        ],
        out_specs=pl.BlockSpec((nu * tm, d), lambda t, *_: (t, 0)),
        scratch_shapes=[pltpu.VMEM((2, nu, rows, d), BF16), pltpu.SemaphoreType.DMA((2, nu))],
    )
    return pl.pallas_call(
        _combine_kernel,
        grid_spec=grid_spec,
        out_shape=jax.ShapeDtypeStruct((n, d), F32),
        compiler_params=_cparams(("arbitrary",)),
    )(plan["gran_dst"], plan["tot_gran"], x1, srow, top_w, g_final, yb)


def _rope_tables(seq):
    half = ATTN_HEAD_DIM // 2
    inv_freq = np.float32(ROPE_THETA) ** (-np.arange(0, half, 2, dtype=np.float32) / np.float32(half))
    ang_row = np.arange(seq // GRID_W, dtype=np.float32)[:, None] * inv_freq
    ang_col = np.arange(GRID_W, dtype=np.float32)[:, None] * inv_freq
    by_row = lambda t: jnp.repeat(jnp.asarray(t, F32), GRID_W, axis=0)
    by_col = lambda t: jnp.tile(jnp.asarray(t, F32), (seq // GRID_W, 1))
    cos_r, sin_r = by_row(np.cos(ang_row)), by_row(np.sin(ang_row))
    cos_c, sin_c = by_col(np.cos(ang_col)), by_col(np.sin(ang_col))
    cos_t = jnp.concatenate([cos_r, cos_c] * 2, axis=-1)
    sin_t = jnp.concatenate([-sin_r, -sin_c, sin_r, sin_c], axis=-1)
    return cos_t, sin_t


def _token_mixer(x2, batch, seq, g_mix, w_in, conv_w, conv_b, dt_bias_f, dt_bias_b, a_log_f, a_log_b, d_skip,
                 g_ssm, q_norm_g, k_norm_g, w_br_ssm, w_br_attn, w_out):
    n, d = x2.shape
    z_end = SSM_INNER
    xbc_end = z_end + CONV_CH
    dtf_end = xbc_end + SSM_HEADS
    dtb_end = dtf_end + SSM_HEADS
    q_end = dtb_end + ATTN_HEADS * ATTN_HEAD_DIM
    k_end = q_end + ATTN_KV_HEADS * ATTN_HEAD_DIM
    v_end = k_end + ATTN_KV_HEADS * ATTN_HEAD_DIM
    head_cols = lambda w: _rope_head_order(w.reshape(d, -1, ATTN_HEAD_DIM)).reshape(d, -1)
    w_main = jnp.concatenate([w_in[:, :z_end], head_cols(w_in[:, dtb_end:q_end]), w_in[:, v_end:],
                              w_in[:, z_end:xbc_end], head_cols(w_in[:, q_end:k_end]), w_in[:, k_end:v_end]],
                             axis=1).astype(BF16)
    w_dt = jnp.pad(w_in[:, xbc_end:dtb_end], ((0, 0), (0, LANES - 2 * SSM_HEADS))).astype(BF16)

    proj, dt = _in_proj(x2, g_mix.reshape(1, d), w_main, w_dt)
    proj3 = proj.reshape(batch, seq, PROJ_COLS)

    xbc = _conv(proj3, conv_w, conv_b.reshape(1, CONV_CH))

    dt3 = dt.reshape(batch, seq, LANES)
    dtt3 = jnp.swapaxes(dt3[:, :, :2 * SSM_HEADS], 1, 2)
    bias = jnp.concatenate([dt_bias_f, dt_bias_b])
    alog = jnp.concatenate([a_log_f, a_log_b])
    pad_row = lambda v: jnp.pad(v, (0, LANES - 2 * SSM_HEADS)).reshape(1, LANES)
    y_f, y_b = _ssd(xbc, dt3, dtt3, pad_row(bias), bias.reshape(-1, 1), pad_row(alog), alog.reshape(-1, 1))

    cos_t, sin_t = _rope_tables(seq)
    q_rot, k_rot = _qk_prep(proj, cos_t, sin_t, _rope_head_order(q_norm_g).reshape(1, -1),
                            _rope_head_order(k_norm_g).reshape(1, -1), seq)
    vt3 = jnp.swapaxes(proj3[:, :, V_OFF:V_OFF + ATTN_KV_HEADS * ATTN_HEAD_DIM], 1, 2)
    attn = _flash(q_rot.reshape(batch, seq, -1), k_rot.reshape(batch, seq, -1), vt3, q_norm_g, k_norm_g)

    return _merge(y_f.reshape(n, -1), y_b.reshape(n, -1), xbc.reshape(n, CONV_CH), proj, attn.reshape(n, -1),
                  x2, jnp.repeat(d_skip, SSM_HEAD_DIM).reshape(1, -1), g_ssm.reshape(1, -1),
                  w_br_ssm.astype(BF16), w_br_attn.astype(BF16), w_out.astype(BF16))


def _moe_and_final_norm(x1, g_ffn, w_router, b_router, w_mlp1, b_mlp1, w_mlp2, b_mlp2, g_final):
    n, d = x1.shape
    tm = min(MOE_TILE, n)
    w_t = w_router.T
    w_hi = w_t.astype(BF16)
    w_lo = (w_t - w_hi.astype(F32)).astype(BF16)
    g_ffn_row = g_ffn.reshape(1, d)
    h2, srow_t, wt_t, cnt = _router(x1, g_ffn_row, jnp.concatenate([w_hi, w_lo, w_hi], axis=1),
                                    b_router.reshape(N_EXPERTS, 1))
    plan = _routing_plan(cnt[:, :, 0].astype(jnp.int32), n, tm, MOE_BLOCK)
    lanes_0_3 = lambda a: jnp.pad(a[:TOP_K].T, ((0, 0), (0, LANES - TOP_K)))
    xs = _dispatch(h2, srow_t, plan, tm)
    yb = _experts(xs, plan, w_mlp1, b_mlp1[:, None, :], w_mlp2, b_mlp2[:, None, :])
    return _combine(x1, yb, lanes_0_3(srow_t), lanes_0_3(wt_t), plan, g_final.reshape(1, d), tm)


def kernel(x, g_mix, w_in, conv_w, conv_b, dt_bias_f, dt_bias_b, a_log_f, a_log_b, d_skip, g_ssm, q_norm_g,
           k_norm_g, w_br_ssm, w_br_attn, w_out, g_ffn, w_router, b_router, w_mlp1, b_mlp1, w_mlp2, b_mlp2,
           g_final):
    batch, seq, d = x.shape
    assert g_mix.shape[0] == 1, "single-layer model: the final rmsnorm is fused into the MoE combine"
    x2 = x.reshape(batch * seq, d)
    x1 = _token_mixer(x2, batch, seq, g_mix[0], w_in[0], conv_w[0], conv_b[0], dt_bias_f[0], dt_bias_b[0],
                      a_log_f[0], a_log_b[0], d_skip[0], g_ssm[0], q_norm_g[0], k_norm_g[0], w_br_ssm[0],
                      w_br_attn[0], w_out[0])
    out = _moe_and_final_norm(x1, g_ffn[0], w_router[0], b_router[0], w_mlp1[0], b_mlp1[0], w_mlp2[0],
                              b_mlp2[0], g_final)
    return out.reshape(batch, seq, d)
```

```python
import functools
import math

import jax
import jax.numpy as jnp
import numpy as np
from jax import lax
from jax.experimental import pallas as pl
from jax.experimental.pallas import tpu as pltpu

F32 = jnp.float32
BF16 = jnp.bfloat16

NORM_EPS = 1e-6
GRID_W = 64
SSM_HEADS = 16
SSM_HEAD_DIM = 64
SSM_INNER = SSM_HEADS * SSM_HEAD_DIM
SSM_GROUPS = 2
SSM_STATE = 128
SSM_CONV = 5
CONV_CH = SSM_INNER + 2 * SSM_GROUPS * SSM_STATE
ATTN_HEADS = 8
ATTN_KV_HEADS = 2
ATTN_HEAD_DIM = 128
ROPE_THETA = 10000.0
N_EXPERTS = 32
TOP_K = 4
SWIGLU_LIMIT = 7.0
SWIGLU_ALPHA = 1.702

LANES = 128
BF16_SUBLANES = 16
VMEM_LIMIT = 56 * 1024 * 1024

Z_OFF, Q_OFF, GATE_OFF, XBC_OFF = 0, 1024, 2048, 4096
K_OFF, V_OFF, PROJ_COLS = 5632, 5888, 6144

MOE_TILE = 512
MOE_BLOCK = 512
MOE_GRAN = BF16_SUBLANES
MOE_TILES_PER_STEP = 2


def _cparams(sem):
    return pltpu.CompilerParams(dimension_semantics=sem, vmem_limit_bytes=VMEM_LIMIT)


def _sigmoid(x):
    return 1.0 / (1.0 + jnp.exp(-x))


def _softplus(x):
    return jnp.maximum(x, 0.0) + jnp.log(1.0 + jnp.exp(-jnp.abs(x)))


def _rms(x, g):
    ms = jnp.mean(x * x, axis=-1, keepdims=True)
    return x * lax.rsqrt(ms + NORM_EPS) * g


def _inproj_kernel(x_ref, g_ref, w_ref, wdt_ref, o_ref, dt_ref, h_scr):
    @pl.when(pl.program_id(1) == 0)
    def _():
        hb = _rms(x_ref[...], g_ref[...]).astype(BF16)
        h_scr[...] = hb
        dt_ref[...] = jnp.dot(hb, wdt_ref[...], preferred_element_type=F32)

    o_ref[...] = jnp.dot(h_scr[...], w_ref[...], preferred_element_type=F32).astype(o_ref.dtype)


def _in_proj(x2, g_mix, w_main, w_dt):
    n, d = x2.shape
    tm = min(1024, n)
    tn = 2048
    return pl.pallas_call(
        _inproj_kernel,
        grid=(n // tm, PROJ_COLS // tn),
        in_specs=[
            pl.BlockSpec((tm, d), lambda i, j: (i, 0)),
            pl.BlockSpec((1, d), lambda i, j: (0, 0)),
            pl.BlockSpec((d, tn), lambda i, j: (0, j)),
            pl.BlockSpec((d, LANES), lambda i, j: (0, 0)),
        ],
        out_specs=[
            pl.BlockSpec((tm, tn), lambda i, j: (i, j)),
            pl.BlockSpec((tm, LANES), lambda i, j: (i, 0)),
        ],
        out_shape=[
            jax.ShapeDtypeStruct((n, PROJ_COLS), BF16),
            jax.ShapeDtypeStruct((n, LANES), F32),
        ],
        scratch_shapes=[pltpu.VMEM((tm, d), BF16)],
        compiler_params=_cparams(("arbitrary", "arbitrary")),
    )(x2, g_mix, w_main, w_dt)


CONV_HALO = 64
CONV_ROWS = 128


def _conv_kernel(prev_ref, cur_ref, next_ref, shift_ref, w_ref, b_ref, o_ref):
    s = pl.program_id(1)
    ts = cur_ref.shape[1]
    prev = prev_ref[0]
    nxt = next_ref[0]
    zero = jnp.zeros_like(prev)
    ext = jnp.concatenate([jnp.where(s == 0, zero, prev), cur_ref[0],
                           jnp.where(s == pl.num_programs(1) - 1, zero, nxt)], axis=0)
    shift = shift_ref[...]
    for rb in range(ts // CONV_ROWS):
        lo = rb * CONV_ROWS
        taps = jnp.dot(shift, ext[lo:lo + CONV_ROWS + 2 * CONV_HALO], preferred_element_type=F32)
        acc = b_ref[...] + w_ref[0:1, :] * taps[0:CONV_ROWS]
        for k in range(1, SSM_CONV):
            acc = acc + w_ref[k:k + 1, :] * taps[k * CONV_ROWS:(k + 1) * CONV_ROWS]
        o_ref[0, lo:lo + CONV_ROWS, :] = (acc * _sigmoid(acc)).astype(o_ref.dtype)


def _conv(proj3, conv_w, conv_b):
    b, s, _ = proj3.shape
    ts = min(2048, s)
    tc = 512
    halo = CONV_HALO
    hb = ts // halo
    col0 = XBC_OFF // tc
    pad = (SSM_CONV - 1) // 2
    out_row = np.arange(SSM_CONV * CONV_ROWS)
    src = out_row % CONV_ROWS + halo + out_row // CONV_ROWS - pad
    shift = jnp.asarray(src[:, None] == np.arange(CONV_ROWS + 2 * halo)[None, :], BF16)
    return pl.pallas_call(
        _conv_kernel,
        grid=(b, s // ts, CONV_CH // tc),
        in_specs=[
            pl.BlockSpec((1, halo, tc), lambda bi, si, ci: (bi, jnp.maximum(si * hb - 1, 0), col0 + ci)),
            pl.BlockSpec((1, ts, tc), lambda bi, si, ci: (bi, si, col0 + ci)),
            pl.BlockSpec((1, halo, tc),
                         lambda bi, si, ci: (bi, jnp.minimum((si + 1) * hb, s // halo - 1), col0 + ci)),
            pl.BlockSpec(shift.shape, lambda bi, si, ci: (0, 0)),
            pl.BlockSpec((SSM_CONV, tc), lambda bi, si, ci: (0, ci)),
            pl.BlockSpec((1, tc), lambda bi, si, ci: (0, ci)),
        ],
        out_specs=pl.BlockSpec((1, ts, tc), lambda bi, si, ci: (bi, si, ci)),
        out_shape=jax.ShapeDtypeStruct((b, s, CONV_CH), BF16),
        compiler_params=_cparams(("arbitrary", "arbitrary", "arbitrary")),
    )(proj3, proj3, proj3, shift, conv_w, conv_b)


SSM_CHUNKS_PER_STEP = 8


def _ssd_kernel(xf_ref, xb_ref, dtf_ref, dtb_ref, dttf_ref, dttb_ref, brow_ref, bcol_ref,
                arow_ref, acol_ref, yf_ref, yb_ref, st_ref):
    L = SSM_STATE
    nsub = xf_ref.shape[1] // L
    hg = SSM_HEADS // SSM_GROUPS
    pairs = hg // 2

    @pl.when(pl.program_id(1) == 0)
    def _():
        st_ref[...] = jnp.zeros_like(st_ref)

    rows = lax.broadcasted_iota(jnp.int32, (L, L), 0)
    cols = lax.broadcasted_iota(jnp.int32, (L, L), 1)
    lower = rows >= cols
    upper = rows <= cols
    ltri = lower.astype(BF16)
    utri = upper.astype(BF16)
    lane = lax.broadcasted_iota(jnp.int32, (L, LANES), 1)
    first_half = lane < SSM_HEAD_DIM
    lane1 = lax.broadcasted_iota(jnp.int32, (1, LANES), 1)
    log2e = math.log2(math.e)
    a_row = -jnp.exp(arow_ref[...]) * log2e
    a_col = -jnp.exp(acol_ref[...]) * log2e

    def split3(v):
        hi = v.astype(BF16)
        r1 = v - hi.astype(F32)
        mid = r1.astype(BF16)
        return hi, mid, (r1 - mid.astype(F32)).astype(BF16)

    def cumsum_cols(tri, v):
        return jnp.dot(jnp.concatenate([tri] * 3, axis=1), jnp.concatenate(split3(v), axis=0),
                       preferred_element_type=F32)

    def cumsum_rows(v, tri):
        return jnp.dot(jnp.concatenate(split3(v), axis=1), jnp.concatenate([tri] * 3, axis=0),
                       preferred_element_type=F32)

    for sub, d in [(sub, d) for sub in range(nsub) for d in range(2)]:
        x_ref, dt_ref, dtt_ref, y_ref = ((xf_ref, dtf_ref, dttf_ref, yf_ref) if d == 0
                                         else (xb_ref, dtb_ref, dttb_ref, yb_ref))
        r0 = (sub if d == 0 else nsub - 1 - sub) * L
        rs = slice(r0, r0 + L)
        a = _softplus(dt_ref[0, rs, :] + brow_ref[...]) * a_row
        dt_t = _softplus(dtt_ref[0, :, rs] + bcol_ref[...])
        a_t = dt_t * a_col
        if d == 0:
            cs_col = cumsum_cols(ltri, a)
            cs_row = cumsum_rows(a_t, utri)
            tot = cs_col[L - 1:L, :]
            tot_t = cs_row[:, L - 1:L]
            mask = lower
        else:
            cs_col = cumsum_cols(utri, a)
            cs_row = cumsum_rows(a_t, ltri)
            tot = cs_col[0:1, :]
            tot_t = cs_row[:, 0:1]
            mask = upper
        w_t = dt_t * jnp.exp2(tot_t - cs_row)
        src_t = cs_row - jnp.log2(dt_t)
        chunk_decay = jnp.exp2(tot)

        for g in range(SSM_GROUPS):
            boff = SSM_INNER + g * SSM_STATE
            coff = SSM_INNER + SSM_GROUPS * SSM_STATE + g * SSM_STATE
            bm = x_ref[0, rs, boff:boff + SSM_STATE]
            cm = x_ref[0, rs, coff:coff + SSM_STATE]
            cb = lax.dot_general(cm, bm, (((1,), (1,)), ((), ())), preferred_element_type=F32)
            bt = bm.astype(F32).T
            st = st_ref[d, g]
            y_off = jnp.dot(cm, st.astype(BF16), preferred_element_type=F32)
            for pr in range(pairs):
                h0 = d * SSM_HEADS + g * hg + 2 * pr
                xoff = (g * pairs + pr) * LANES
                xs = x_ref[0, rs, xoff:xoff + LANES]
                zero = jnp.zeros_like(xs)
                rhs = jnp.concatenate([jnp.where(first_half, xs, zero),
                                       jnp.where(first_half, zero, xs)], axis=0)
                ms, ws, dins = [], [], []
                for hh in (h0, h0 + 1):
                    cs_b = jnp.broadcast_to(cs_col[:, hh:hh + 1], (L, L))
                    seg = cs_b - src_t[hh:hh + 1, :]
                    m = cb * jnp.exp2(jnp.where(mask, seg, -jnp.inf))
                    ms.append(m.astype(BF16))
                    ws.append((bt * w_t[hh:hh + 1, :]).astype(BF16))
                    dins.append(jnp.exp2(cs_b))
                y = jnp.dot(jnp.concatenate(ms, axis=1), rhs, preferred_element_type=F32)
                y = y + y_off[:, pr * LANES:(pr + 1) * LANES] * jnp.where(first_half, dins[0], dins[1])
                y_ref[0, rs, xoff:xoff + LANES] = y.astype(y_ref.dtype)
                new_st = jnp.dot(jnp.concatenate(ws, axis=1), rhs, preferred_element_type=F32)
                cd = jnp.where(lane1 < SSM_HEAD_DIM, chunk_decay[:, h0:h0 + 1], chunk_decay[:, h0 + 1:h0 + 2])
                st_ref[d, g, :, pr * LANES:(pr + 1) * LANES] = st[:, pr * LANES:(pr + 1) * LANES] * cd + new_st


def _ssd(xbc, dt3, dtt3, bias_row, bias_col, alog_row, alog_col):
    b, s, _ = xbc.shape
    L = min(SSM_CHUNKS_PER_STEP * SSM_STATE, s)
    nc = s // L
    hg = SSM_HEADS // SSM_GROUPS
    fwd = lambda bi, ci: (bi, ci, 0)
    bwd = lambda bi, ci: (bi, nc - 1 - ci, 0)
    fwd_t = lambda bi, ci: (bi, 0, ci)
    bwd_t = lambda bi, ci: (bi, 0, nc - 1 - ci)
    const = lambda bi, ci: (0, 0)
    return pl.pallas_call(
        _ssd_kernel,
        grid=(b, nc),
        in_specs=[
            pl.BlockSpec((1, L, CONV_CH), fwd),
            pl.BlockSpec((1, L, CONV_CH), bwd),
            pl.BlockSpec((1, L, LANES), fwd),
            pl.BlockSpec((1, L, LANES), bwd),
            pl.BlockSpec((1, 2 * SSM_HEADS, L), fwd_t),
            pl.BlockSpec((1, 2 * SSM_HEADS, L), bwd_t),
            pl.BlockSpec((1, LANES), const),
            pl.BlockSpec((2 * SSM_HEADS, 1), const),
            pl.BlockSpec((1, LANES), const),
            pl.BlockSpec((2 * SSM_HEADS, 1), const),
        ],
        out_specs=[
            pl.BlockSpec((1, L, SSM_INNER), fwd),
            pl.BlockSpec((1, L, SSM_INNER), bwd),
        ],
        out_shape=[jax.ShapeDtypeStruct((b, s, SSM_INNER), BF16)] * 2,
        scratch_shapes=[pltpu.VMEM((2, SSM_GROUPS, SSM_STATE, hg * SSM_HEAD_DIM), F32)],
        compiler_params=_cparams(("arbitrary", "arbitrary")),
    )(xbc, xbc, dt3, dt3, dtt3, dtt3, bias_row, bias_col, alog_row, alog_col)


def _rope_head_order(a):
    q4 = ATTN_HEAD_DIM // 4
    return jnp.concatenate([a[..., 0:q4], a[..., 2 * q4:3 * q4], a[..., q4:2 * q4], a[..., 3 * q4:]], axis=-1)


def _rope_norm(t, g, cos, sin_signed, ones):
    sq = t * t
    hi = sq.astype(BF16)
    lo = (sq - hi.astype(F32)).astype(BF16)
    ms = jnp.dot(jnp.concatenate([hi, lo], axis=1), ones, preferred_element_type=F32) * (1.0 / ATTN_HEAD_DIM)
    tn = t * lax.rsqrt(ms + NORM_EPS) * g
    return tn * cos + pltpu.roll(tn, ATTN_HEAD_DIM // 2, 1) * sin_signed


Q_SCALE = ATTN_HEAD_DIM ** -0.5 * math.log2(math.e)


def _qkprep_kernel(q_ref, k_ref, cos_ref, sin_ref, qg_ref, kg_ref, qo_ref, ko_ref):
    cos = cos_ref[...]
    sin = sin_ref[...]
    ones = jnp.ones((2 * ATTN_HEAD_DIM, ATTN_HEAD_DIM), BF16)

    def heads(src_ref, g_ref, dst_ref, n_heads, scale):
        for h in range(n_heads):
            sl = slice(h * ATTN_HEAD_DIM, (h + 1) * ATTN_HEAD_DIM)
            r = _rope_norm(src_ref[:, sl].astype(F32), g_ref[...], cos, sin, ones) * scale
            dst_ref[:, sl] = r.astype(dst_ref.dtype)

    heads(q_ref, qg_ref, qo_ref, ATTN_HEADS, Q_SCALE)
    heads(k_ref, kg_ref, ko_ref, ATTN_KV_HEADS, 1.0)


def _qk_prep(proj, cos_t, sin_t, q_norm_g, k_norm_g, seq):
    n = proj.shape[0]
    tm = min(1024, seq)
    qw = ATTN_HEADS * ATTN_HEAD_DIM
    kw = ATTN_KV_HEADS * ATTN_HEAD_DIM
    spt = seq // tm
    return pl.pallas_call(
        _qkprep_kernel,
        grid=(n // tm,),
        in_specs=[
            pl.BlockSpec((tm, qw), lambda i: (i, Q_OFF // qw)),
            pl.BlockSpec((tm, kw), lambda i: (i, K_OFF // kw)),
            pl.BlockSpec((tm, ATTN_HEAD_DIM), lambda i: (i % spt, 0)),
            pl.BlockSpec((tm, ATTN_HEAD_DIM), lambda i: (i % spt, 0)),
            pl.BlockSpec((1, ATTN_HEAD_DIM), lambda i: (0, 0)),
            pl.BlockSpec((1, ATTN_HEAD_DIM), lambda i: (0, 0)),
        ],
        out_specs=[
            pl.BlockSpec((tm, qw), lambda i: (i, 0)),
            pl.BlockSpec((tm, kw), lambda i: (i, 0)),
        ],
        out_shape=[jax.ShapeDtypeStruct((n, qw), BF16), jax.ShapeDtypeStruct((n, kw), BF16)],
        compiler_params=_cparams(("arbitrary",)),
    )(proj, proj, cos_t, sin_t, q_norm_g, k_norm_g)


def _flash_kernel(small_ref, q_ref, k_ref, vt_ref, o_ref, *, tk, th):
    q = q_ref[0]
    tq = q.shape[0]
    seq = k_ref.shape[1]
    nt = (((1,), (1,)), ((), ()))
    small = small_ref[0] != 0

    @pl.when(small)
    def _():
        l = acc_t = None
        for c in range(seq // th):
            ks = slice(c * th, (c + 1) * th)
            p_t = jnp.exp2(lax.dot_general(k_ref[0, ks, :], q, nt, preferred_element_type=F32))
            l_c = jnp.sum(p_t, axis=0, keepdims=True)
            a_c = jnp.dot(vt_ref[0, :, ks], p_t.astype(BF16), preferred_element_type=F32)
            l = l_c if l is None else l + l_c
            acc_t = a_c if acc_t is None else acc_t + a_c
        o_ref[0] = (acc_t / l).T.astype(o_ref.dtype)

    @pl.when(jnp.logical_not(small))
    def _():
        def body(i, carry):
            m, l, acc_t = carry
            off = pl.multiple_of(i * tk, tk)
            s_t = lax.dot_general(k_ref[0, pl.ds(off, tk), :], q, nt, preferred_element_type=F32)
            m_new = jnp.maximum(m, jnp.max(s_t, axis=0, keepdims=True))
            alpha = jnp.exp2(m - m_new)
            p_t = jnp.exp2(s_t - m_new)
            l = alpha * l + jnp.sum(p_t, axis=0, keepdims=True)
            acc_t = alpha * acc_t + jnp.dot(vt_ref[0, :, pl.ds(off, tk)], p_t.astype(BF16),
                                            preferred_element_type=F32)
            return m_new, l, acc_t

        init = (jnp.full((1, tq), -jnp.inf, F32), jnp.zeros((1, tq), F32),
                jnp.zeros((ATTN_HEAD_DIM, tq), F32))
        _, l, acc_t = lax.fori_loop(0, seq // tk, body, init)
        o_ref[0] = (acc_t / l).T.astype(o_ref.dtype)


SCORE_BOUND = 59.0


def _flash(q3, k3, vt3, q_norm_g, k_norm_g):
    b, s, _ = q3.shape
    tq = min(1024, s)
    tk = min(2048, s)
    th = min(4096, s)
    nq = s // tq
    grp = ATTN_HEADS // ATTN_KV_HEADS
    hd = ATTN_HEAD_DIM
    bound = hd * Q_SCALE * jnp.max(jnp.abs(q_norm_g)) * jnp.max(jnp.abs(k_norm_g)) * 1.02
    small = (bound <= SCORE_BOUND).astype(jnp.int32).reshape(1)
    grid_spec = pltpu.PrefetchScalarGridSpec(
        num_scalar_prefetch=1,
        grid=(b, ATTN_HEADS, nq),
        in_specs=[
            pl.BlockSpec((1, tq, hd), lambda bi, h, qi, sm: (bi, qi, h)),
            pl.BlockSpec((1, s, hd), lambda bi, h, qi, sm: (bi, 0, h // grp)),
            pl.BlockSpec((1, hd, s), lambda bi, h, qi, sm: (bi, h // grp, 0)),
        ],
        out_specs=pl.BlockSpec((1, tq, hd), lambda bi, h, qi, sm: (bi, qi, h)),
    )
    return pl.pallas_call(
        functools.partial(_flash_kernel, tk=tk, th=th),
        grid_spec=grid_spec,
        out_shape=jax.ShapeDtypeStruct((b, s, ATTN_HEADS * hd), BF16),
        compiler_params=_cparams(("arbitrary", "arbitrary", "arbitrary")),
    )(small, q3, k3, vt3)


def _merge_kernel(yf_ref, yb_ref, xs_ref, z_ref, gate_ref, attn_ref, x_ref, dskip_ref, gssm_ref,
                  wbs_ref, wba_ref, wo_ref, x1_ref):
    d = x_ref.shape[1]
    xs = xs_ref[...].astype(F32)
    y = yf_ref[...].astype(F32) + yb_ref[...].astype(F32) + xs * dskip_ref[...]
    z = z_ref[...].astype(F32)
    y = _rms(y * (z * _sigmoid(z)), gssm_ref[...])
    br_ssm = jnp.dot(y.astype(BF16), wbs_ref[...], preferred_element_type=F32)
    br_attn = jnp.dot(attn_ref[...], wba_ref[...], preferred_element_type=F32)
    g_s = _sigmoid(gate_ref[:, :d].astype(F32))
    g_a = _sigmoid(gate_ref[:, d:].astype(F32))
    merged = (g_s * br_ssm + g_a * br_attn).astype(BF16)
    x1_ref[...] = x_ref[...] + jnp.dot(merged, wo_ref[...], preferred_element_type=F32)


def _merge(y_f, y_b, xbc, proj, attn, x2, dskip_row, g_ssm, w_br_ssm, w_br_attn, w_out):
    n, d = x2.shape
    tm = min(512, n)
    row = lambda i: (i, 0)
    const = lambda i: (0, 0)
    return pl.pallas_call(
        _merge_kernel,
        grid=(n // tm,),
        in_specs=[
            pl.BlockSpec((tm, d), row),
            pl.BlockSpec((tm, d), row),
            pl.BlockSpec((tm, d), row),
            pl.BlockSpec((tm, d), lambda i: (i, Z_OFF // d)),
            pl.BlockSpec((tm, 2 * d), lambda i: (i, GATE_OFF // (2 * d))),
            pl.BlockSpec((tm, d), row),
            pl.BlockSpec((tm, d), row),
            pl.BlockSpec((1, d), const),
            pl.BlockSpec((1, d), const),
            pl.BlockSpec((d, d), const),
            pl.BlockSpec((d, d), const),
            pl.BlockSpec((d, d), const),
        ],
        out_specs=pl.BlockSpec((tm, d), row),
        out_shape=jax.ShapeDtypeStruct((n, d), F32),
        compiler_params=_cparams(("arbitrary",)),
    )(y_f, y_b, xbc, proj, proj, attn, x2, dskip_row, g_ssm, w_br_ssm, w_br_attn, w_out)


def _router_kernel(x1_ref, g_ref, w_ref, b_ref, h_ref, srow_ref, wt_ref, cnt_ref):
    tm = x1_ref.shape[0]
    h = _rms(x1_ref[...], g_ref[...])
    h_hi = h.astype(BF16)
    h_lo = (h - h_hi.astype(F32)).astype(BF16)
    h_ref[...] = h_hi
    logits = lax.dot_general(w_ref[...], jnp.concatenate([h_hi, h_hi, h_lo], axis=1), (((1,), (1,)), ((), ())),
                             preferred_element_type=F32) + b_ref[...]
    expert = lax.broadcasted_iota(jnp.int32, logits.shape, 0)
    slot = lax.broadcasted_iota(jnp.int32, wt_ref.shape, 0)
    chosen = jnp.zeros(logits.shape, F32)
    vals, hits = [], []
    for k in range(TOP_K):
        m = jnp.max(logits, axis=0, keepdims=True)
        idx = jnp.min(jnp.where(logits == m, expert, N_EXPERTS), axis=0, keepdims=True)
        vals.append(m)
        hit = expert == idx
        hits.append(hit)
        chosen = jnp.where(hit, 1.0, chosen)
        logits = jnp.where(hit, -jnp.inf, logits)
    es = [jnp.exp(v - vals[0]) for v in vals]
    tot = es[0] + es[1] + es[2] + es[3]

    chosen16 = chosen.astype(BF16)
    cnt_col = jnp.sum(chosen, axis=1, keepdims=True)
    cnt_row = lax.dot_general(jnp.ones((8, tm), BF16), chosen16, (((1,), (1,)), ((), ())),
                              preferred_element_type=F32)[0:1, :]
    seg_row = jnp.ceil(cnt_row * (1.0 / MOE_GRAN)) * MOE_GRAN
    before = (lax.broadcasted_iota(jnp.int32, (N_EXPERTS, N_EXPERTS), 1)
              < lax.broadcasted_iota(jnp.int32, (N_EXPERTS, N_EXPERTS), 0))
    seg_off = jnp.sum(jnp.where(before, seg_row, 0.0), axis=1, keepdims=True)
    earlier = (lax.broadcasted_iota(jnp.int32, (tm, tm), 0)
               < lax.broadcasted_iota(jnp.int32, (tm, tm), 1)).astype(BF16)
    pos = jnp.dot(chosen16, earlier, preferred_element_type=F32) + seg_off

    val_out = jnp.zeros(wt_ref.shape, F32)
    row_out = jnp.zeros(srow_ref.shape, F32)
    for k in range(TOP_K):
        val_out = jnp.where(slot == k, es[k] / tot, val_out)
        row_out = jnp.where(slot == k, jnp.sum(jnp.where(hits[k], pos, 0.0), axis=0, keepdims=True), row_out)
    wt_ref[...] = val_out
    srow_ref[...] = row_out
    cnt_ref[0] = jnp.broadcast_to(cnt_col, cnt_ref.shape[1:])


def _router(x1, g_ffn, w_router_t, b_router_col):
    n, d = x1.shape
    tm = min(MOE_TILE, n)
    return pl.pallas_call(
        _router_kernel,
        grid=(n // tm,),
        in_specs=[
            pl.BlockSpec((tm, d), lambda i: (i, 0)),
            pl.BlockSpec((1, d), lambda i: (0, 0)),
            pl.BlockSpec((N_EXPERTS, 3 * d), lambda i: (0, 0)),
            pl.BlockSpec((N_EXPERTS, 1), lambda i: (0, 0)),
        ],
        out_specs=[pl.BlockSpec((tm, d), lambda i: (i, 0)),
                   pl.BlockSpec((8, tm), lambda i: (0, i)),
                   pl.BlockSpec((8, tm), lambda i: (0, i)),
                   pl.BlockSpec((1, N_EXPERTS, LANES), lambda i: (i, 0, 0))],
        out_shape=[jax.ShapeDtypeStruct((n, d), BF16),
                   jax.ShapeDtypeStruct((8, n), F32),
                   jax.ShapeDtypeStruct((8, n), F32),
                   jax.ShapeDtypeStruct((n // tm, N_EXPERTS, LANES), F32)],
        compiler_params=_cparams(("arbitrary",)),
    )(x1, g_ffn, w_router_t, b_router_col)


def _routing_plan(cnt, n, tm, tb):
    nt = n // tm
    gran = MOE_GRAN
    rc = (cnt + gran - 1) // gran * gran
    covered = jnp.sum(rc, axis=0)
    region = (covered + tb - 1) // tb * tb
    pad_end = jnp.cumsum(region)
    pad_start = pad_end - region
    seg_start = pad_start[None, :] + jnp.cumsum(rc, axis=0) - rc
    n_used = pad_end[-1] // tb
    nb = (TOP_K * n + nt * N_EXPERTS * (gran - 1) + N_EXPERTS * (tb - 1) + tb - 1) // tb
    blk = jnp.arange(nb, dtype=jnp.int32)
    block_e = jnp.sum((pad_end[None, :] <= (jnp.minimum(blk, n_used - 1) * tb)[:, None]).astype(jnp.int32), axis=1)
    block_e = jnp.minimum(block_e, N_EXPERTS - 1)
    flat = lambda a: a.reshape(-1).astype(jnp.int32)
    tail_gran = (region - covered) // gran
    ngran = rc // gran
    g_end = jnp.cumsum(ngran, axis=1)
    gi = jnp.arange(_stage_rows(tm) // gran, dtype=jnp.int32)
    g_exp = jnp.minimum(jnp.sum((g_end[:, None, :] <= gi[None, :, None]).astype(jnp.int32), axis=2), N_EXPERTS - 1)
    mine = g_exp[:, :, None] == jnp.arange(N_EXPERTS, dtype=jnp.int32)[None, None, :]
    seg_base = seg_start - gran * (g_end - ngran)
    gran_dst = jnp.sum(jnp.where(mine, seg_base[:, None, :], 0), axis=2) + gran * gi[None, :]
    ids = jnp.arange(N_EXPERTS, dtype=jnp.int32)
    later = jnp.where((ids[None, :] > ids[:, None]) & (region[None, :] > 0), ids[None, :], N_EXPERTS)
    nxt = jnp.min(later, axis=1)
    next_expert = jnp.where(nxt < N_EXPERTS, nxt, ids)
    return dict(gran_dst=flat(gran_dst), tot_gran=flat(g_end[:, -1]), block_e=flat(block_e),
                next_expert=flat(next_expert),
                n_used=flat(n_used), nb=nb,
                tail_start=flat(pad_start + covered), tail_gran=flat(tail_gran),
                tot_tail=flat(jnp.sum(tail_gran)))


def _stage_rows(tm):
    rows = TOP_K * tm + N_EXPERTS * (MOE_GRAN - 1)
    return (rows + 2 * LANES - 1) // (2 * LANES) * (2 * LANES)


GRAN_UNROLL = 4


def _granule_copies(t, gdst_ref, totg_ref, make_copy, start):
    total = totg_ref[t]
    base = t * (gdst_ref.shape[0] // totg_ref.shape[0])
    full = lax.shift_right_logical(total, GRAN_UNROLL.bit_length() - 1)

    def one(i):
        make_copy(pl.multiple_of(i * MOE_GRAN, MOE_GRAN), pl.multiple_of(gdst_ref[base + i], MOE_GRAN),
                  MOE_GRAN).start()

    def group(q, c):
        if start:
            for u in range(GRAN_UNROLL):
                one(q * GRAN_UNROLL + u)
        else:
            make_copy(0, 0, GRAN_UNROLL * MOE_GRAN).wait()
        return c

    def rest(i, c):
        if start:
            one(i)
        else:
            make_copy(0, 0, MOE_GRAN).wait()
        return c

    lax.fori_loop(0, full, group, 0)
    lax.fori_loop(full * GRAN_UNROLL, total, rest, 0)


def _dispatch_kernel(gdst_ref, totg_ref, tstart_ref, tgran_ref, misc_ref,
                     h_ref, srow_ref, xs_hbm, stage, zbuf, sem, zsem, *, n_blocks):
    t = pl.program_id(0)
    slot = t % 2
    nu = stage.shape[1]
    tm = h_ref.shape[0] // nu
    rows = stage.shape[2]
    chunk = 2 * LANES
    row_id = lax.broadcasted_iota(jnp.int32, (chunk, tm), 0).astype(F32)
    for u in range(nu):
        tok = slice(u * tm, (u + 1) * tm)
        h = h_ref[tok, :]
        srows = [srow_ref[k:k + 1, tok] for k in range(TOP_K)]
        for c in range(rows // chunk):
            perm = jnp.zeros((chunk, tm), F32)
            for srow in srows:
                perm = jnp.where(row_id == srow - float(c * chunk), 1.0, perm)
            stage[slot, u, c * chunk:(c + 1) * chunk, :] = jnp.dot(
                perm.astype(BF16), h, preferred_element_type=F32).astype(stage.dtype)

    def copy_from(s, u):
        def copy(stage_row, sorted_row, nrows):
            return pltpu.make_async_copy(stage.at[s, u, pl.ds(stage_row, nrows)],
                                         xs_hbm.at[pl.ds(sorted_row, nrows)], sem.at[s, u])
        return copy

    def move(step, s, start):
        for u in range(nu):
            _granule_copies(step * nu + u, gdst_ref, totg_ref, copy_from(s, u), start=start)

    move(t, slot, True)

    @pl.when(t > 0)
    def _():
        move(jnp.maximum(t - 1, 0), 1 - slot, False)

    @pl.when(t == pl.num_programs(0) - 1)
    def _():
        move(t, slot, False)
        tb = zbuf.shape[0]
        n_used = misc_ref[0]
        zbuf[...] = jnp.zeros_like(zbuf)

        def zero_gran(row):
            return pltpu.make_async_copy(zbuf.at[pl.ds(0, MOE_GRAN)], xs_hbm.at[pl.ds(row, MOE_GRAN)], zsem)

        def zero_block(blk):
            return pltpu.make_async_copy(zbuf, xs_hbm.at[pl.ds(pl.multiple_of(blk * tb, tb), tb)], zsem)

        def tails(e, c):
            def one(g, c2):
                zero_gran(pl.multiple_of(tstart_ref[e] + g * MOE_GRAN, MOE_GRAN)).start()
                return c2
            lax.fori_loop(0, tgran_ref[e], one, 0)
            return c

        def start_block(blk, c):
            zero_block(blk).start()
            return c

        def wait_gran(g, c):
            zero_gran(0).wait()
            return c

        def wait_block(blk, c):
            zero_block(0).wait()
            return c

        lax.fori_loop(0, N_EXPERTS, tails, 0)
        lax.fori_loop(n_used, n_blocks, start_block, 0)
        lax.fori_loop(0, misc_ref[1], wait_gran, 0)
        lax.fori_loop(n_used, n_blocks, wait_block, 0)


def _dispatch(h2, srow_t, plan, tm):
    n, d = h2.shape
    nu = MOE_TILES_PER_STEP
    nt = n // (nu * tm)
    rows = _stage_rows(tm)
    misc = jnp.concatenate([plan["n_used"], plan["tot_tail"]])
    grid_spec = pltpu.PrefetchScalarGridSpec(
        num_scalar_prefetch=5,
        grid=(nt,),
        in_specs=[
            pl.BlockSpec((nu * tm, d), lambda t, *_: (t, 0)),
            pl.BlockSpec((8, nu * tm), lambda t, *_: (0, t)),
        ],
        out_specs=pl.BlockSpec(memory_space=pl.ANY),
        scratch_shapes=[pltpu.VMEM((2, nu, rows, d), BF16), pltpu.VMEM((MOE_BLOCK, d), BF16),
                        pltpu.SemaphoreType.DMA((2, nu)), pltpu.SemaphoreType.DMA(())],
    )
    return pl.pallas_call(
        functools.partial(_dispatch_kernel, n_blocks=plan["nb"]),
        grid_spec=grid_spec,
        out_shape=jax.ShapeDtypeStruct((plan["nb"] * MOE_BLOCK, d), BF16),
        compiler_params=_cparams(("arbitrary",)),
    )(plan["gran_dst"], plan["tot_gran"], plan["tail_start"], plan["tail_gran"], misc, h2, srow_t)


def _expert_kernel(be_ref, nused_ref, nexte_ref, x_ref, w1_hbm, b1_ref, w2_hbm, b2_ref, o_ref,
                   w1f, w2f, w1b, w2b, sem):
    de = w2f.shape[0]
    j = pl.program_id(0)
    live = j < nused_ref[0]
    e = be_ref[j]
    new_expert = jnp.logical_or(j == 0, e != be_ref[jnp.maximum(j - 1, 0)])

    def fetch(ex):
        return (pltpu.make_async_copy(w1_hbm.at[ex], w1f, sem.at[0]),
                pltpu.make_async_copy(w2_hbm.at[ex], w2f, sem.at[1]))

    @pl.when(jnp.logical_and(live, j == 0))
    def _():
        for c in fetch(e):
            c.start()

    @pl.when(jnp.logical_and(live, new_expert))
    def _():
        for c in fetch(e):
            c.wait()
        w1b[...] = w1f[...].astype(BF16)
        w2b[...] = w2f[...].astype(BF16)
        nxt = nexte_ref[e]

        @pl.when(nxt != e)
        def _():
            for c in fetch(nxt):
                c.start()

    @pl.when(live)
    def _():
        hm = jnp.dot(x_ref[...], w1b[...], preferred_element_type=F32) + b1_ref[0]
        gate = jnp.minimum(hm[:, :de], SWIGLU_LIMIT)
        up = jnp.clip(hm[:, de:], -SWIGLU_LIMIT, SWIGLU_LIMIT)
        act = gate * _sigmoid(SWIGLU_ALPHA * gate) * (up + 1.0)
        y = jnp.dot(act.astype(BF16), w2b[...], preferred_element_type=F32) + b2_ref[0]
        o_ref[...] = y.astype(o_ref.dtype)

    @pl.when(jnp.logical_not(live))
    def _():
        o_ref[...] = jnp.zeros_like(o_ref)


def _experts(xs, plan, w1, b1, w2, b2):
    d = xs.shape[1]
    tb = MOE_BLOCK
    f2 = w1.shape[2]
    de = w2.shape[1]
    last = lambda j, nu: jnp.maximum(jnp.minimum(j, nu[0] - 1), 0)
    grid_spec = pltpu.PrefetchScalarGridSpec(
        num_scalar_prefetch=3,
        grid=(plan["nb"],),
        in_specs=[
            pl.BlockSpec((tb, d), lambda j, be, nu, ne: (last(j, nu), 0)),
            pl.BlockSpec(memory_space=pl.ANY),
            pl.BlockSpec((1, 1, f2), lambda j, be, nu, ne: (be[j], 0, 0)),
            pl.BlockSpec(memory_space=pl.ANY),
            pl.BlockSpec((1, 1, d), lambda j, be, nu, ne: (be[j], 0, 0)),
        ],
        out_specs=pl.BlockSpec((tb, d), lambda j, be, nu, ne: (j, 0)),
        scratch_shapes=[pltpu.VMEM((d, f2), F32), pltpu.VMEM((de, d), F32),
                        pltpu.VMEM((d, f2), BF16), pltpu.VMEM((de, d), BF16),
                        pltpu.SemaphoreType.DMA((2,))],
    )
    return pl.pallas_call(
        _expert_kernel,
        grid_spec=grid_spec,
        out_shape=jax.ShapeDtypeStruct(xs.shape, BF16),
        compiler_params=_cparams(("arbitrary",)),
    )(plan["block_e"], plan["n_used"], plan["next_expert"], xs, w1, b1, w2, b2)


def _combine_kernel(gdst_ref, totg_ref, x1_ref, srow_ref, wt_ref, g_ref, yb_hbm, o_ref, stage, sem):
    s = pl.program_id(0)
    slot = s % 2
    nu = stage.shape[1]
    tm = x1_ref.shape[0] // nu
    rows = stage.shape[2]

    def copy_into(sl_, u):
        def copy(stage_row, sorted_row, nrows):
            return pltpu.make_async_copy(yb_hbm.at[pl.ds(sorted_row, nrows)],
                                         stage.at[sl_, u, pl.ds(stage_row, nrows)], sem.at[sl_, u])
        return copy

    def fetch(step, sl_, start):
        for u in range(nu):
            _granule_copies(step * nu + u, gdst_ref, totg_ref, copy_into(sl_, u), start=start)

    @pl.when(s == 0)
    def _():
        stage[...] = jnp.zeros_like(stage)
        fetch(s, slot, True)

    @pl.when(s + 1 < pl.num_programs(0))
    def _():
        fetch(s + 1, 1 - slot, True)

    fetch(s, slot, False)

    chunk = LANES
    row_id = lax.broadcasted_iota(jnp.int32, (chunk, rows), 1).astype(F32)
    for u in range(nu):
        for c in range(tm // chunk):
            sl = slice(u * tm + c * chunk, u * tm + (c + 1) * chunk)
            unsort = jnp.zeros((chunk, rows), F32)
            for k in range(TOP_K):
                unsort = jnp.where(row_id == srow_ref[sl, k:k + 1], wt_ref[sl, k:k + 1], unsort)
            y = jnp.dot(unsort.astype(BF16), stage[slot, u], preferred_element_type=F32)
            o_ref[sl, :] = _rms(x1_ref[sl, :] + y, g_ref[...])


def _combine(x1, yb, srow, top_w, plan, g_final, tm):
    n, d = x1.shape
    nu = MOE_TILES_PER_STEP
    nt = n // (nu * tm)
    rows = _stage_rows(tm)
    grid_spec = pltpu.PrefetchScalarGridSpec(
        num_scalar_prefetch=2,
        grid=(nt,),
        in_specs=[
            pl.BlockSpec((nu * tm, d), lambda t, *_: (t, 0)),
            pl.BlockSpec((nu * tm, LANES), lambda t, *_: (t, 0)),
            pl.BlockSpec((nu * tm, LANES), lambda t, *_: (t, 0)),
            pl.BlockSpec((1, d), lambda t, *_: (0, 0)),
            pl.BlockSpec(memory_space=pl.ANY),
        ],
        out_specs=pl.BlockSpec((nu * tm, d), lambda t, *_: (t, 0)),
        scratch_shapes=[pltpu.VMEM((2, nu, rows, d), BF16), pltpu.SemaphoreType.DMA((2, nu))],
    )
    return pl.pallas_call(
        _combine_kernel,
        grid_spec=grid_spec,
        out_shape=jax.ShapeDtypeStruct((n, d), F32),
        compiler_params=_cparams(("arbitrary",)),
    )(plan["gran_dst"], plan["tot_gran"], x1, srow, top_w, g_final, yb)


def _rope_tables(seq):
    half = ATTN_HEAD_DIM // 2
    inv_freq = np.float32(ROPE_THETA) ** (-np.arange(0, half, 2, dtype=np.float32) / np.float32(half))
    ang_row = np.arange(seq // GRID_W, dtype=np.float32)[:, None] * inv_freq
    ang_col = np.arange(GRID_W, dtype=np.float32)[:, None] * inv_freq
    by_row = lambda t: jnp.repeat(jnp.asarray(t, F32), GRID_W, axis=0)
    by_col = lambda t: jnp.tile(jnp.asarray(t, F32), (seq // GRID_W, 1))
    cos_r, sin_r = by_row(np.cos(ang_row)), by_row(np.sin(ang_row))
    cos_c, sin_c = by_col(np.cos(ang_col)), by_col(np.sin(ang_col))
    cos_t = jnp.concatenate([cos_r, cos_c] * 2, axis=-1)
    sin_t = jnp.concatenate([-sin_r, -sin_c, sin_r, sin_c], axis=-1)
    return cos_t, sin_t


def _token_mixer(x2, batch, seq, g_mix, w_in, conv_w, conv_b, dt_bias_f, dt_bias_b, a_log_f, a_log_b, d_skip,
                 g_ssm, q_norm_g, k_norm_g, w_br_ssm, w_br_attn, w_out):
    n, d = x2.shape
    z_end = SSM_INNER
    xbc_end = z_end + CONV_CH
    dtf_end = xbc_end + SSM_HEADS
    dtb_end = dtf_end + SSM_HEADS
    q_end = dtb_end + ATTN_HEADS * ATTN_HEAD_DIM
    k_end = q_end + ATTN_KV_HEADS * ATTN_HEAD_DIM
    v_end = k_end + ATTN_KV_HEADS * ATTN_HEAD_DIM
    head_cols = lambda w: _rope_head_order(w.reshape(d, -1, ATTN_HEAD_DIM)).reshape(d, -1)
    w_main = jnp.concatenate([w_in[:, :z_end], head_cols(w_in[:, dtb_end:q_end]), w_in[:, v_end:],
                              w_in[:, z_end:xbc_end], head_cols(w_in[:, q_end:k_end]), w_in[:, k_end:v_end]],
                             axis=1).astype(BF16)
    w_dt = jnp.pad(w_in[:, xbc_end:dtb_end], ((0, 0), (0, LANES - 2 * SSM_HEADS))).astype(BF16)

    proj, dt = _in_proj(x2, g_mix.reshape(1, d), w_main, w_dt)
    proj3 = proj.reshape(batch, seq, PROJ_COLS)

    xbc = _conv(proj3, conv_w, conv_b.reshape(1, CONV_CH))

    dt3 = dt.reshape(batch, seq, LANES)
    dtt3 = jnp.swapaxes(dt3[:, :, :2 * SSM_HEADS], 1, 2)
    bias = jnp.concatenate([dt_bias_f, dt_bias_b])
    alog = jnp.concatenate([a_log_f, a_log_b])
    pad_row = lambda v: jnp.pad(v, (0, LANES - 2 * SSM_HEADS)).reshape(1, LANES)
    y_f, y_b = _ssd(xbc, dt3, dtt3, pad_row(bias), bias.reshape(-1, 1), pad_row(alog), alog.reshape(-1, 1))

    cos_t, sin_t = _rope_tables(seq)
    q_rot, k_rot = _qk_prep(proj, cos_t, sin_t, _rope_head_order(q_norm_g).reshape(1, -1),
                            _rope_head_order(k_norm_g).reshape(1, -1), seq)
    vt3 = jnp.swapaxes(proj3[:, :, V_OFF:V_OFF + ATTN_KV_HEADS * ATTN_HEAD_DIM], 1, 2)
    attn = _flash(q_rot.reshape(batch, seq, -1), k_rot.reshape(batch, seq, -1), vt3, q_norm_g, k_norm_g)

    return _merge(y_f.reshape(n, -1), y_b.reshape(n, -1), xbc.reshape(n, CONV_CH), proj, attn.reshape(n, -1),
                  x2, jnp.repeat(d_skip, SSM_HEAD_DIM).reshape(1, -1), g_ssm.reshape(1, -1),
                  w_br_ssm.astype(BF16), w_br_attn.astype(BF16), w_out.astype(BF16))


def _moe_and_final_norm(x1, g_ffn, w_router, b_router, w_mlp1, b_mlp1, w_mlp2, b_mlp2, g_final):
    n, d = x1.shape
    tm = min(MOE_TILE, n)
    w_t = w_router.T
    w_hi = w_t.astype(BF16)
    w_lo = (w_t - w_hi.astype(F32)).astype(BF16)
    g_ffn_row = g_ffn.reshape(1, d)
    h2, srow_t, wt_t, cnt = _router(x1, g_ffn_row, jnp.concatenate([w_hi, w_lo, w_hi], axis=1),
                                    b_router.reshape(N_EXPERTS, 1))
    plan = _routing_plan(cnt[:, :, 0].astype(jnp.int32), n, tm, MOE_BLOCK)
    lanes_0_3 = lambda a: jnp.pad(a[:TOP_K].T, ((0, 0), (0, LANES - TOP_K)))
    xs = _dispatch(h2, srow_t, plan, tm)
    yb = _experts(xs, plan, w_mlp1, b_mlp1[:, None, :], w_mlp2, b_mlp2[:, None, :])
    return _combine(x1, yb, lanes_0_3(srow_t), lanes_0_3(wt_t), plan, g_final.reshape(1, d), tm)


def kernel(x, g_mix, w_in, conv_w, conv_b, dt_bias_f, dt_bias_b, a_log_f, a_log_b, d_skip, g_ssm, q_norm_g,
           k_norm_g, w_br_ssm, w_br_attn, w_out, g_ffn, w_router, b_router, w_mlp1, b_mlp1, w_mlp2, b_mlp2,
           g_final):
    batch, seq, d = x.shape
    assert g_mix.shape[0] == 1, "single-layer model: the final rmsnorm is fused into the MoE combine"
    x2 = x.reshape(batch * seq, d)
    x1 = _token_mixer(x2, batch, seq, g_mix[0], w_in[0], conv_w[0], conv_b[0], dt_bias_f[0], dt_bias_b[0],
                      a_log_f[0], a_log_b[0], d_skip[0], g_ssm[0], q_norm_g[0], k_norm_g[0], w_br_ssm[0],
                      w_br_attn[0], w_out[0])
    out = _moe_and_final_norm(x1, g_ffn[0], w_router[0], b_router[0], w_mlp1[0], b_mlp1[0], w_mlp2[0],
                              b_mlp2[0], g_final)
    return out.reshape(batch, seq, d)
```

```python
import functools
import math

import jax
import jax.numpy as jnp
import numpy as np
from jax import lax
from jax.experimental import pallas as pl
from jax.experimental.pallas import tpu as pltpu

F32 = jnp.float32
BF16 = jnp.bfloat16

NORM_EPS = 1e-6
GRID_W = 64
SSM_HEADS = 16
SSM_HEAD_DIM = 64
SSM_INNER = SSM_HEADS * SSM_HEAD_DIM
SSM_GROUPS = 2
SSM_STATE = 128
SSM_CONV = 5
CONV_CH = SSM_INNER + 2 * SSM_GROUPS * SSM_STATE
ATTN_HEADS = 8
ATTN_KV_HEADS = 2
ATTN_HEAD_DIM = 128
ROPE_THETA = 10000.0
N_EXPERTS = 32
TOP_K = 4
SWIGLU_LIMIT = 7.0
SWIGLU_ALPHA = 1.702

LANES = 128
BF16_SUBLANES = 16
VMEM_LIMIT = 56 * 1024 * 1024

Z_OFF, Q_OFF, GATE_OFF, XBC_OFF = 0, 1024, 2048, 4096
K_OFF, V_OFF, PROJ_COLS = 5632, 5888, 6144

MOE_TILE = 512
MOE_BLOCK = 512
MOE_GRAN = BF16_SUBLANES


def _cparams(sem):
    return pltpu.CompilerParams(dimension_semantics=sem, vmem_limit_bytes=VMEM_LIMIT)


def _sigmoid(x):
    return 1.0 / (1.0 + jnp.exp(-x))


def _softplus(x):
    return jnp.maximum(x, 0.0) + jnp.log(1.0 + jnp.exp(-jnp.abs(x)))


def _rms(x, g):
    ms = jnp.mean(x * x, axis=-1, keepdims=True)
    return x * lax.rsqrt(ms + NORM_EPS) * g


def _inproj_kernel(x_ref, g_ref, w_ref, wdt_ref, o_ref, dt_ref, h_scr):
    @pl.when(pl.program_id(1) == 0)
    def _():
        hb = _rms(x_ref[...], g_ref[...]).astype(BF16)
        h_scr[...] = hb
        dt_ref[...] = jnp.dot(hb, wdt_ref[...], preferred_element_type=F32)

    o_ref[...] = jnp.dot(h_scr[...], w_ref[...], preferred_element_type=F32).astype(o_ref.dtype)


def _in_proj(x2, g_mix, w_main, w_dt):
    n, d = x2.shape
    tm = min(1024, n)
    tn = 2048
    return pl.pallas_call(
        _inproj_kernel,
        grid=(n // tm, PROJ_COLS // tn),
        in_specs=[
            pl.BlockSpec((tm, d), lambda i, j: (i, 0)),
            pl.BlockSpec((1, d), lambda i, j: (0, 0)),
            pl.BlockSpec((d, tn), lambda i, j: (0, j)),
            pl.BlockSpec((d, LANES), lambda i, j: (0, 0)),
        ],
        out_specs=[
            pl.BlockSpec((tm, tn), lambda i, j: (i, j)),
            pl.BlockSpec((tm, LANES), lambda i, j: (i, 0)),
        ],
        out_shape=[
            jax.ShapeDtypeStruct((n, PROJ_COLS), BF16),
            jax.ShapeDtypeStruct((n, LANES), F32),
        ],
        scratch_shapes=[pltpu.VMEM((tm, d), BF16)],
        compiler_params=_cparams(("arbitrary", "arbitrary")),
    )(x2, g_mix, w_main, w_dt)


CONV_HALO = 64
CONV_ROWS = 128


def _conv_kernel(prev_ref, cur_ref, next_ref, shift_ref, w_ref, b_ref, o_ref):
    s = pl.program_id(1)
    ts = cur_ref.shape[1]
    prev = prev_ref[0]
    nxt = next_ref[0]
    zero = jnp.zeros_like(prev)
    ext = jnp.concatenate([jnp.where(s == 0, zero, prev), cur_ref[0],
                           jnp.where(s == pl.num_programs(1) - 1, zero, nxt)], axis=0)
    shift = shift_ref[...]
    for rb in range(ts // CONV_ROWS):
        lo = rb * CONV_ROWS
        taps = jnp.dot(shift, ext[lo:lo + CONV_ROWS + 2 * CONV_HALO], preferred_element_type=F32)
        acc = b_ref[...] + w_ref[0:1, :] * taps[0:CONV_ROWS]
        for k in range(1, SSM_CONV):
            acc = acc + w_ref[k:k + 1, :] * taps[k * CONV_ROWS:(k + 1) * CONV_ROWS]
        o_ref[0, lo:lo + CONV_ROWS, :] = (acc * _sigmoid(acc)).astype(o_ref.dtype)


def _conv(proj3, conv_w, conv_b):
    b, s, _ = proj3.shape
    ts = min(2048, s)
    tc = 512
    halo = CONV_HALO
    hb = ts // halo
    col0 = XBC_OFF // tc
    pad = (SSM_CONV - 1) // 2
    out_row = np.arange(SSM_CONV * CONV_ROWS)
    src = out_row % CONV_ROWS + halo + out_row // CONV_ROWS - pad
    shift = jnp.asarray(src[:, None] == np.arange(CONV_ROWS + 2 * halo)[None, :], BF16)
    return pl.pallas_call(
        _conv_kernel,
        grid=(b, s // ts, CONV_CH // tc),
        in_specs=[
            pl.BlockSpec((1, halo, tc), lambda bi, si, ci: (bi, jnp.maximum(si * hb - 1, 0), col0 + ci)),
            pl.BlockSpec((1, ts, tc), lambda bi, si, ci: (bi, si, col0 + ci)),
            pl.BlockSpec((1, halo, tc),
                         lambda bi, si, ci: (bi, jnp.minimum((si + 1) * hb, s // halo - 1), col0 + ci)),
            pl.BlockSpec(shift.shape, lambda bi, si, ci: (0, 0)),
            pl.BlockSpec((SSM_CONV, tc), lambda bi, si, ci: (0, ci)),
            pl.BlockSpec((1, tc), lambda bi, si, ci: (0, ci)),
        ],
        out_specs=pl.BlockSpec((1, ts, tc), lambda bi, si, ci: (bi, si, ci)),
        out_shape=jax.ShapeDtypeStruct((b, s, CONV_CH), BF16),
        compiler_params=_cparams(("arbitrary", "arbitrary", "arbitrary")),
    )(proj3, proj3, proj3, shift, conv_w, conv_b)


SSM_CHUNKS_PER_STEP = 4


def _ssd_kernel(xf_ref, xb_ref, dtf_ref, dtb_ref, dttf_ref, dttb_ref, brow_ref, bcol_ref,
                arow_ref, acol_ref, yf_ref, yb_ref, st_ref):
    L = SSM_STATE
    nsub = xf_ref.shape[1] // L
    hg = SSM_HEADS // SSM_GROUPS
    pairs = hg // 2

    @pl.when(pl.program_id(1) == 0)
    def _():
        st_ref[...] = jnp.zeros_like(st_ref)

    rows = lax.broadcasted_iota(jnp.int32, (L, L), 0)
    cols = lax.broadcasted_iota(jnp.int32, (L, L), 1)
    lower = rows >= cols
    upper = rows <= cols
    ltri = lower.astype(BF16)
    utri = upper.astype(BF16)
    lane = lax.broadcasted_iota(jnp.int32, (L, LANES), 1)
    first_half = lane < SSM_HEAD_DIM
    lane1 = lax.broadcasted_iota(jnp.int32, (1, LANES), 1)
    log2e = math.log2(math.e)
    a_row = -jnp.exp(arow_ref[...]) * log2e
    a_col = -jnp.exp(acol_ref[...]) * log2e

    def split3(v):
        hi = v.astype(BF16)
        r1 = v - hi.astype(F32)
        mid = r1.astype(BF16)
        return hi, mid, (r1 - mid.astype(F32)).astype(BF16)

    def cumsum_cols(tri, v):
        return jnp.dot(jnp.concatenate([tri] * 3, axis=1), jnp.concatenate(split3(v), axis=0),
                       preferred_element_type=F32)

    def cumsum_rows(v, tri):
        return jnp.dot(jnp.concatenate(split3(v), axis=1), jnp.concatenate([tri] * 3, axis=0),
                       preferred_element_type=F32)

    for sub, d in [(sub, d) for sub in range(nsub) for d in range(2)]:
        x_ref, dt_ref, dtt_ref, y_ref = ((xf_ref, dtf_ref, dttf_ref, yf_ref) if d == 0
                                         else (xb_ref, dtb_ref, dttb_ref, yb_ref))
        r0 = (sub if d == 0 else nsub - 1 - sub) * L
        rs = slice(r0, r0 + L)
        a = _softplus(dt_ref[0, rs, :] + brow_ref[...]) * a_row
        dt_t = _softplus(dtt_ref[0, :, rs] + bcol_ref[...])
        a_t = dt_t * a_col
        if d == 0:
            cs_col = cumsum_cols(ltri, a)
            cs_row = cumsum_rows(a_t, utri)
            tot = cs_col[L - 1:L, :]
            tot_t = cs_row[:, L - 1:L]
            mask = lower
        else:
            cs_col = cumsum_cols(utri, a)
            cs_row = cumsum_rows(a_t, ltri)
            tot = cs_col[0:1, :]
            tot_t = cs_row[:, 0:1]
            mask = upper
        w_t = dt_t * jnp.exp2(tot_t - cs_row)
        src_t = cs_row - jnp.log2(dt_t)
        chunk_decay = jnp.exp2(tot)

        for g in range(SSM_GROUPS):
            boff = SSM_INNER + g * SSM_STATE
            coff = SSM_INNER + SSM_GROUPS * SSM_STATE + g * SSM_STATE
            bm = x_ref[0, rs, boff:boff + SSM_STATE]
            cm = x_ref[0, rs, coff:coff + SSM_STATE]
            cb = lax.dot_general(cm, bm, (((1,), (1,)), ((), ())), preferred_element_type=F32)
            bt = bm.astype(F32).T
            st = st_ref[d, g]
            y_off = jnp.dot(cm, st.astype(BF16), preferred_element_type=F32)
            for pr in range(pairs):
                h0 = d * SSM_HEADS + g * hg + 2 * pr
                xoff = (g * pairs + pr) * LANES
                xs = x_ref[0, rs, xoff:xoff + LANES]
                zero = jnp.zeros_like(xs)
                rhs = jnp.concatenate([jnp.where(first_half, xs, zero),
                                       jnp.where(first_half, zero, xs)], axis=0)
                ms, ws, dins = [], [], []
                for hh in (h0, h0 + 1):
                    cs_b = jnp.broadcast_to(cs_col[:, hh:hh + 1], (L, L))
                    seg = cs_b - src_t[hh:hh + 1, :]
                    m = cb * jnp.exp2(jnp.where(mask, seg, -jnp.inf))
                    ms.append(m.astype(BF16))
                    ws.append((bt * w_t[hh:hh + 1, :]).astype(BF16))
                    dins.append(jnp.exp2(cs_b))
                y = jnp.dot(jnp.concatenate(ms, axis=1), rhs, preferred_element_type=F32)
                y = y + y_off[:, pr * LANES:(pr + 1) * LANES] * jnp.where(first_half, dins[0], dins[1])
                y_ref[0, rs, xoff:xoff + LANES] = y.astype(y_ref.dtype)
                new_st = jnp.dot(jnp.concatenate(ws, axis=1), rhs, preferred_element_type=F32)
                cd = jnp.where(lane1 < SSM_HEAD_DIM, chunk_decay[:, h0:h0 + 1], chunk_decay[:, h0 + 1:h0 + 2])
                st_ref[d, g, :, pr * LANES:(pr + 1) * LANES] = st[:, pr * LANES:(pr + 1) * LANES] * cd + new_st


def _ssd(xbc, dt3, dtt3, bias_row, bias_col, alog_row, alog_col):
    b, s, _ = xbc.shape
    L = SSM_CHUNKS_PER_STEP * SSM_STATE
    nc = s // L
    hg = SSM_HEADS // SSM_GROUPS
    fwd = lambda bi, ci: (bi, ci, 0)
    bwd = lambda bi, ci: (bi, nc - 1 - ci, 0)
    fwd_t = lambda bi, ci: (bi, 0, ci)
    bwd_t = lambda bi, ci: (bi, 0, nc - 1 - ci)
    const = lambda bi, ci: (0, 0)
    return pl.pallas_call(
        _ssd_kernel,
        grid=(b, nc),
        in_specs=[
            pl.BlockSpec((1, L, CONV_CH), fwd),
            pl.BlockSpec((1, L, CONV_CH), bwd),
            pl.BlockSpec((1, L, LANES), fwd),
            pl.BlockSpec((1, L, LANES), bwd),
            pl.BlockSpec((1, 2 * SSM_HEADS, L), fwd_t),
            pl.BlockSpec((1, 2 * SSM_HEADS, L), bwd_t),
            pl.BlockSpec((1, LANES), const),
            pl.BlockSpec((2 * SSM_HEADS, 1), const),
            pl.BlockSpec((1, LANES), const),
            pl.BlockSpec((2 * SSM_HEADS, 1), const),
        ],
        out_specs=[
            pl.BlockSpec((1, L, SSM_INNER), fwd),
            pl.BlockSpec((1, L, SSM_INNER), bwd),
        ],
        out_shape=[jax.ShapeDtypeStruct((b, s, SSM_INNER), BF16)] * 2,
        scratch_shapes=[pltpu.VMEM((2, SSM_GROUPS, SSM_STATE, hg * SSM_HEAD_DIM), F32)],
        compiler_params=_cparams(("arbitrary", "arbitrary")),
    )(xbc, xbc, dt3, dt3, dtt3, dtt3, bias_row, bias_col, alog_row, alog_col)


def _rope_head_order(a):
    q4 = ATTN_HEAD_DIM // 4
    return jnp.concatenate([a[..., 0:q4], a[..., 2 * q4:3 * q4], a[..., q4:2 * q4], a[..., 3 * q4:]], axis=-1)


def _rope_norm(t, g, cos, sin_signed, ones):
    sq = t * t
    hi = sq.astype(BF16)
    lo = (sq - hi.astype(F32)).astype(BF16)
    ms = jnp.dot(jnp.concatenate([hi, lo], axis=1), ones, preferred_element_type=F32) * (1.0 / ATTN_HEAD_DIM)
    tn = t * lax.rsqrt(ms + NORM_EPS) * g
    return tn * cos + pltpu.roll(tn, ATTN_HEAD_DIM // 2, 1) * sin_signed


Q_SCALE = ATTN_HEAD_DIM ** -0.5 * math.log2(math.e)


def _qkprep_kernel(q_ref, k_ref, cos_ref, sin_ref, qg_ref, kg_ref, qo_ref, ko_ref):
    cos = cos_ref[...]
    sin = sin_ref[...]
    ones = jnp.ones((2 * ATTN_HEAD_DIM, ATTN_HEAD_DIM), BF16)

    def heads(src_ref, g_ref, dst_ref, n_heads, scale):
        for h in range(n_heads):
            sl = slice(h * ATTN_HEAD_DIM, (h + 1) * ATTN_HEAD_DIM)
            r = _rope_norm(src_ref[:, sl].astype(F32), g_ref[...], cos, sin, ones) * scale
            dst_ref[:, sl] = r.astype(dst_ref.dtype)

    heads(q_ref, qg_ref, qo_ref, ATTN_HEADS, Q_SCALE)
    heads(k_ref, kg_ref, ko_ref, ATTN_KV_HEADS, 1.0)


def _qk_prep(proj, cos_t, sin_t, q_norm_g, k_norm_g, seq):
    n = proj.shape[0]
    tm = min(1024, seq)
    qw = ATTN_HEADS * ATTN_HEAD_DIM
    kw = ATTN_KV_HEADS * ATTN_HEAD_DIM
    spt = seq // tm
    return pl.pallas_call(
        _qkprep_kernel,
        grid=(n // tm,),
        in_specs=[
            pl.BlockSpec((tm, qw), lambda i: (i, Q_OFF // qw)),
            pl.BlockSpec((tm, kw), lambda i: (i, K_OFF // kw)),
            pl.BlockSpec((tm, ATTN_HEAD_DIM), lambda i: (i % spt, 0)),
            pl.BlockSpec((tm, ATTN_HEAD_DIM), lambda i: (i % spt, 0)),
            pl.BlockSpec((1, ATTN_HEAD_DIM), lambda i: (0, 0)),
            pl.BlockSpec((1, ATTN_HEAD_DIM), lambda i: (0, 0)),
        ],
        out_specs=[
            pl.BlockSpec((tm, qw), lambda i: (i, 0)),
            pl.BlockSpec((tm, kw), lambda i: (i, 0)),
        ],
        out_shape=[jax.ShapeDtypeStruct((n, qw), BF16), jax.ShapeDtypeStruct((n, kw), BF16)],
        compiler_params=_cparams(("arbitrary",)),
    )(proj, proj, cos_t, sin_t, q_norm_g, k_norm_g)


def _flash_kernel(small_ref, q_ref, k_ref, vt_ref, o_ref, *, tk, th):
    q = q_ref[0]
    tq = q.shape[0]
    seq = k_ref.shape[1]
    nt = (((1,), (1,)), ((), ()))
    small = small_ref[0] != 0

    @pl.when(small)
    def _():
        l = acc_t = None
        for c in range(seq // th):
            ks = slice(c * th, (c + 1) * th)
            p_t = jnp.exp2(lax.dot_general(k_ref[0, ks, :], q, nt, preferred_element_type=F32))
            l_c = jnp.sum(p_t, axis=0, keepdims=True)
            a_c = jnp.dot(vt_ref[0, :, ks], p_t.astype(BF16), preferred_element_type=F32)
            l = l_c if l is None else l + l_c
            acc_t = a_c if acc_t is None else acc_t + a_c
        o_ref[0] = (acc_t / l).T.astype(o_ref.dtype)

    @pl.when(jnp.logical_not(small))
    def _():
        def body(i, carry):
            m, l, acc_t = carry
            off = pl.multiple_of(i * tk, tk)
            s_t = lax.dot_general(k_ref[0, pl.ds(off, tk), :], q, nt, preferred_element_type=F32)
            m_new = jnp.maximum(m, jnp.max(s_t, axis=0, keepdims=True))
            alpha = jnp.exp2(m - m_new)
            p_t = jnp.exp2(s_t - m_new)
            l = alpha * l + jnp.sum(p_t, axis=0, keepdims=True)
            acc_t = alpha * acc_t + jnp.dot(vt_ref[0, :, pl.ds(off, tk)], p_t.astype(BF16),
                                            preferred_element_type=F32)
            return m_new, l, acc_t

        init = (jnp.full((1, tq), -jnp.inf, F32), jnp.zeros((1, tq), F32),
                jnp.zeros((ATTN_HEAD_DIM, tq), F32))
        _, l, acc_t = lax.fori_loop(0, seq // tk, body, init)
        o_ref[0] = (acc_t / l).T.astype(o_ref.dtype)


SCORE_BOUND = 59.0


def _flash(q3, k3, vt3, q_norm_g, k_norm_g):
    b, s, _ = q3.shape
    tq = min(1024, s)
    tk = min(2048, s)
    th = min(4096, s)
    nq = s // tq
    grp = ATTN_HEADS // ATTN_KV_HEADS
    hd = ATTN_HEAD_DIM
    bound = hd * Q_SCALE * jnp.max(jnp.abs(q_norm_g)) * jnp.max(jnp.abs(k_norm_g)) * 1.02
    small = (bound <= SCORE_BOUND).astype(jnp.int32).reshape(1)
    grid_spec = pltpu.PrefetchScalarGridSpec(
        num_scalar_prefetch=1,
        grid=(b, ATTN_HEADS, nq),
        in_specs=[
            pl.BlockSpec((1, tq, hd), lambda bi, h, qi, sm: (bi, qi, h)),
            pl.BlockSpec((1, s, hd), lambda bi, h, qi, sm: (bi, 0, h // grp)),
            pl.BlockSpec((1, hd, s), lambda bi, h, qi, sm: (bi, h // grp, 0)),
        ],
        out_specs=pl.BlockSpec((1, tq, hd), lambda bi, h, qi, sm: (bi, qi, h)),
    )
    return pl.pallas_call(
        functools.partial(_flash_kernel, tk=tk, th=th),
        grid_spec=grid_spec,
        out_shape=jax.ShapeDtypeStruct((b, s, ATTN_HEADS * hd), BF16),
        compiler_params=_cparams(("arbitrary", "arbitrary", "arbitrary")),
    )(small, q3, k3, vt3)


def _merge_kernel(yf_ref, yb_ref, xs_ref, z_ref, gate_ref, attn_ref, x_ref, dskip_ref, gssm_ref,
                  wbs_ref, wba_ref, wo_ref, x1_ref):
    d = x_ref.shape[1]
    xs = xs_ref[...].astype(F32)
    y = yf_ref[...].astype(F32) + yb_ref[...].astype(F32) + xs * dskip_ref[...]
    z = z_ref[...].astype(F32)
    y = _rms(y * (z * _sigmoid(z)), gssm_ref[...])
    br_ssm = jnp.dot(y.astype(BF16), wbs_ref[...], preferred_element_type=F32)
    br_attn = jnp.dot(attn_ref[...], wba_ref[...], preferred_element_type=F32)
    g_s = _sigmoid(gate_ref[:, :d].astype(F32))
    g_a = _sigmoid(gate_ref[:, d:].astype(F32))
    merged = (g_s * br_ssm + g_a * br_attn).astype(BF16)
    x1_ref[...] = x_ref[...] + jnp.dot(merged, wo_ref[...], preferred_element_type=F32)


def _merge(y_f, y_b, xbc, proj, attn, x2, dskip_row, g_ssm, w_br_ssm, w_br_attn, w_out):
    n, d = x2.shape
    tm = min(512, n)
    row = lambda i: (i, 0)
    const = lambda i: (0, 0)
    return pl.pallas_call(
        _merge_kernel,
        grid=(n // tm,),
        in_specs=[
            pl.BlockSpec((tm, d), row),
            pl.BlockSpec((tm, d), row),
            pl.BlockSpec((tm, d), row),
            pl.BlockSpec((tm, d), lambda i: (i, Z_OFF // d)),
            pl.BlockSpec((tm, 2 * d), lambda i: (i, GATE_OFF // (2 * d))),
            pl.BlockSpec((tm, d), row),
            pl.BlockSpec((tm, d), row),
            pl.BlockSpec((1, d), const),
            pl.BlockSpec((1, d), const),
            pl.BlockSpec((d, d), const),
            pl.BlockSpec((d, d), const),
            pl.BlockSpec((d, d), const),
        ],
        out_specs=pl.BlockSpec((tm, d), row),
        out_shape=jax.ShapeDtypeStruct((n, d), F32),
        compiler_params=_cparams(("arbitrary",)),
    )(y_f, y_b, xbc, proj, proj, attn, x2, dskip_row, g_ssm, w_br_ssm, w_br_attn, w_out)


def _router_kernel(x1_ref, g_ref, w_ref, b_ref, h_ref, srow_ref, srow_l_ref, wt_l_ref, cnt_ref):
    tm = x1_ref.shape[0]
    h = _rms(x1_ref[...], g_ref[...])
    h_hi = h.astype(BF16)
    h_lo = (h - h_hi.astype(F32)).astype(BF16)
    h_ref[...] = h_hi
    logits = lax.dot_general(w_ref[...], jnp.concatenate([h_hi, h_hi, h_lo], axis=1), (((1,), (1,)), ((), ())),
                             preferred_element_type=F32) + b_ref[...]
    expert = lax.broadcasted_iota(jnp.int32, logits.shape, 0)
    slot = lax.broadcasted_iota(jnp.int32, srow_ref.shape, 0)
    chosen = jnp.zeros(logits.shape, F32)
    vals, hits = [], []
    for k in range(TOP_K):
        m = jnp.max(logits, axis=0, keepdims=True)
        idx = jnp.min(jnp.where(logits == m, expert, N_EXPERTS), axis=0, keepdims=True)
        vals.append(m)
        hit = expert == idx
        hits.append(hit)
        chosen = jnp.where(hit, 1.0, chosen)
        logits = jnp.where(hit, -jnp.inf, logits)
    es = [jnp.exp(v - vals[0]) for v in vals]
    tot = es[0] + es[1] + es[2] + es[3]

    chosen16 = chosen.astype(BF16)
    cnt_col = jnp.sum(chosen, axis=1, keepdims=True)
    cnt_row = lax.dot_general(jnp.ones((8, tm), BF16), chosen16, (((1,), (1,)), ((), ())),
                              preferred_element_type=F32)[0:1, :]
    seg_row = jnp.ceil(cnt_row * (1.0 / MOE_GRAN)) * MOE_GRAN
    before = (lax.broadcasted_iota(jnp.int32, (N_EXPERTS, N_EXPERTS), 1)
              < lax.broadcasted_iota(jnp.int32, (N_EXPERTS, N_EXPERTS), 0))
    seg_off = jnp.sum(jnp.where(before, seg_row, 0.0), axis=1, keepdims=True)
    earlier = (lax.broadcasted_iota(jnp.int32, (tm, tm), 0)
               < lax.broadcasted_iota(jnp.int32, (tm, tm), 1)).astype(BF16)
    pos = jnp.dot(chosen16, earlier, preferred_element_type=F32) + seg_off

    val_out = jnp.zeros(srow_ref.shape, F32)
    row_out = jnp.zeros(srow_ref.shape, F32)
    for k in range(TOP_K):
        val_out = jnp.where(slot == k, es[k] / tot, val_out)
        row_out = jnp.where(slot == k, jnp.sum(jnp.where(hits[k], pos, 0.0), axis=0, keepdims=True), row_out)
    srow_ref[...] = row_out
    fill = jnp.zeros((LANES - srow_ref.shape[0], tm), F32)
    srow_l_ref[...] = jnp.concatenate([row_out, fill], axis=0).T
    wt_l_ref[...] = jnp.concatenate([val_out, fill], axis=0).T
    cnt_ref[0] = jnp.broadcast_to(cnt_col, cnt_ref.shape[1:])


def _router(x1, g_ffn, w_router_t, b_router_col):
    n, d = x1.shape
    tm = min(MOE_TILE, n)
    return pl.pallas_call(
        _router_kernel,
        grid=(n // tm,),
        in_specs=[
            pl.BlockSpec((tm, d), lambda i: (i, 0)),
            pl.BlockSpec((1, d), lambda i: (0, 0)),
            pl.BlockSpec((N_EXPERTS, 3 * d), lambda i: (0, 0)),
            pl.BlockSpec((N_EXPERTS, 1), lambda i: (0, 0)),
        ],
        out_specs=[pl.BlockSpec((tm, d), lambda i: (i, 0)),
                   pl.BlockSpec((8, tm), lambda i: (0, i)),
                   pl.BlockSpec((tm, LANES), lambda i: (i, 0)),
                   pl.BlockSpec((tm, LANES), lambda i: (i, 0)),
                   pl.BlockSpec((1, N_EXPERTS, LANES), lambda i: (i, 0, 0))],
        out_shape=[jax.ShapeDtypeStruct((n, d), BF16),
                   jax.ShapeDtypeStruct((8, n), F32),
                   jax.ShapeDtypeStruct((n, LANES), F32),
                   jax.ShapeDtypeStruct((n, LANES), F32),
                   jax.ShapeDtypeStruct((n // tm, N_EXPERTS, LANES), F32)],
        compiler_params=_cparams(("arbitrary",)),
    )(x1, g_ffn, w_router_t, b_router_col)


def _routing_plan(cnt, n, tm, tb):
    nt = n // tm
    gran = MOE_GRAN
    rc = (cnt + gran - 1) // gran * gran
    covered = jnp.sum(rc, axis=0)
    region = (covered + tb - 1) // tb * tb
    pad_end = jnp.cumsum(region)
    pad_start = pad_end - region
    seg_start = pad_start[None, :] + jnp.cumsum(rc, axis=0) - rc
    n_used = pad_end[-1] // tb
    nb = (TOP_K * n + nt * N_EXPERTS * (gran - 1) + N_EXPERTS * (tb - 1) + tb - 1) // tb
    blk = jnp.arange(nb, dtype=jnp.int32)
    block_e = jnp.sum((pad_end[None, :] <= (jnp.minimum(blk, n_used - 1) * tb)[:, None]).astype(jnp.int32), axis=1)
    block_e = jnp.minimum(block_e, N_EXPERTS - 1)
    flat = lambda a: a.reshape(-1).astype(jnp.int32)
    tail_gran = (region - covered) // gran
    ngran = rc // gran
    g_end = jnp.cumsum(ngran, axis=1)
    gi = jnp.arange(_stage_rows(tm) // gran, dtype=jnp.int32)
    g_exp = jnp.minimum(jnp.sum((g_end[:, None, :] <= gi[None, :, None]).astype(jnp.int32), axis=2), N_EXPERTS - 1)
    mine = g_exp[:, :, None] == jnp.arange(N_EXPERTS, dtype=jnp.int32)[None, None, :]
    seg_base = seg_start - gran * (g_end - ngran)
    gran_dst = jnp.sum(jnp.where(mine, seg_base[:, None, :], 0), axis=2) + gran * gi[None, :]
    ids = jnp.arange(N_EXPERTS, dtype=jnp.int32)
    later = jnp.where((ids[None, :] > ids[:, None]) & (region[None, :] > 0), ids[None, :], N_EXPERTS)
    nxt = jnp.min(later, axis=1)
    next_expert = jnp.where(nxt < N_EXPERTS, nxt, ids)
    return dict(gran_dst=flat(gran_dst), tot_gran=flat(g_end[:, -1]), block_e=flat(block_e),
                next_expert=flat(next_expert),
                n_used=flat(n_used), nb=nb,
                tail_start=flat(pad_start + covered), tail_gran=flat(tail_gran),
                tot_tail=flat(jnp.sum(tail_gran)))


def _stage_rows(tm):
    rows = TOP_K * tm + N_EXPERTS * (MOE_GRAN - 1)
    return (rows + 2 * LANES - 1) // (2 * LANES) * (2 * LANES)


GRAN_UNROLL = 4


def _granule_copies(t, gdst_ref, totg_ref, make_copy, start):
    total = totg_ref[t]
    base = t * (gdst_ref.shape[0] // totg_ref.shape[0])
    full = lax.shift_right_logical(total, GRAN_UNROLL.bit_length() - 1)

    def one(i):
        make_copy(pl.multiple_of(i * MOE_GRAN, MOE_GRAN), pl.multiple_of(gdst_ref[base + i], MOE_GRAN),
                  MOE_GRAN).start()

    def group(q, c):
        if start:
            for u in range(GRAN_UNROLL):
                one(q * GRAN_UNROLL + u)
        else:
            make_copy(0, 0, GRAN_UNROLL * MOE_GRAN).wait()
        return c

    def rest(i, c):
        if start:
            one(i)
        else:
            make_copy(0, 0, MOE_GRAN).wait()
        return c

    lax.fori_loop(0, full, group, 0)
    lax.fori_loop(full * GRAN_UNROLL, total, rest, 0)


def _dispatch_kernel(gdst_ref, totg_ref, tstart_ref, tgran_ref, misc_ref,
                     h_ref, srow_ref, xs_hbm, stage, zbuf, sem, zsem, *, n_blocks):
    t = pl.program_id(0)
    slot = t % 2
    tm = h_ref.shape[0]
    rows = stage.shape[1]
    h = h_ref[...]
    srows = [srow_ref[k:k + 1, :] for k in range(TOP_K)]
    chunk = 2 * LANES
    row_id = lax.broadcasted_iota(jnp.int32, (chunk, tm), 0).astype(F32)
    for c in range(rows // chunk):
        perm = jnp.zeros((chunk, tm), F32)
        for srow in srows:
            perm = jnp.where(row_id == srow - float(c * chunk), 1.0, perm)
        stage[slot, c * chunk:(c + 1) * chunk, :] = jnp.dot(
            perm.astype(BF16), h, preferred_element_type=F32).astype(stage.dtype)

    def copy_from(s):
        def copy(stage_row, sorted_row, nrows):
            return pltpu.make_async_copy(stage.at[s, pl.ds(stage_row, nrows)],
                                         xs_hbm.at[pl.ds(sorted_row, nrows)], sem.at[s])
        return copy

    _granule_copies(t, gdst_ref, totg_ref, copy_from(slot), start=True)

    @pl.when(t > 0)
    def _():
        _granule_copies(jnp.maximum(t - 1, 0), gdst_ref, totg_ref, copy_from(1 - slot), start=False)

    @pl.when(t == pl.num_programs(0) - 1)
    def _():
        _granule_copies(t, gdst_ref, totg_ref, copy_from(slot), start=False)
        tb = zbuf.shape[0]
        n_used = misc_ref[0]
        zbuf[...] = jnp.zeros_like(zbuf)

        def zero_gran(row):
            return pltpu.make_async_copy(zbuf.at[pl.ds(0, MOE_GRAN)], xs_hbm.at[pl.ds(row, MOE_GRAN)], zsem)

        def zero_block(blk):
            return pltpu.make_async_copy(zbuf, xs_hbm.at[pl.ds(pl.multiple_of(blk * tb, tb), tb)], zsem)

        def tails(e, c):
            def one(g, c2):
                zero_gran(pl.multiple_of(tstart_ref[e] + g * MOE_GRAN, MOE_GRAN)).start()
                return c2
            lax.fori_loop(0, tgran_ref[e], one, 0)
            return c

        def start_block(blk, c):
            zero_block(blk).start()
            return c

        def wait_gran(g, c):
            zero_gran(0).wait()
            return c

        def wait_block(blk, c):
            zero_block(0).wait()
            return c

        lax.fori_loop(0, N_EXPERTS, tails, 0)
        lax.fori_loop(n_used, n_blocks, start_block, 0)
        lax.fori_loop(0, misc_ref[1], wait_gran, 0)
        lax.fori_loop(n_used, n_blocks, wait_block, 0)


def _dispatch(h2, srow_t, plan, tm):
    n, d = h2.shape
    nt = n // tm
    rows = _stage_rows(tm)
    misc = jnp.concatenate([plan["n_used"], plan["tot_tail"]])
    grid_spec = pltpu.PrefetchScalarGridSpec(
        num_scalar_prefetch=5,
        grid=(nt,),
        in_specs=[
            pl.BlockSpec((tm, d), lambda t, *_: (t, 0)),
            pl.BlockSpec((8, tm), lambda t, *_: (0, t)),
        ],
        out_specs=pl.BlockSpec(memory_space=pl.ANY),
        scratch_shapes=[pltpu.VMEM((2, rows, d), BF16), pltpu.VMEM((MOE_BLOCK, d), BF16),
                        pltpu.SemaphoreType.DMA((2,)), pltpu.SemaphoreType.DMA(())],
    )
    return pl.pallas_call(
        functools.partial(_dispatch_kernel, n_blocks=plan["nb"]),
        grid_spec=grid_spec,
        out_shape=jax.ShapeDtypeStruct((plan["nb"] * MOE_BLOCK, d), BF16),
        compiler_params=_cparams(("arbitrary",)),
    )(plan["gran_dst"], plan["tot_gran"], plan["tail_start"], plan["tail_gran"], misc, h2, srow_t)


def _expert_kernel(be_ref, nused_ref, nexte_ref, x_ref, w1_hbm, b1_ref, w2_hbm, b2_ref, o_ref,
                   w1f, w2f, w1b, w2b, sem):
    de = w2f.shape[0]
    j = pl.program_id(0)
    live = j < nused_ref[0]
    e = be_ref[j]
    new_expert = jnp.logical_or(j == 0, e != be_ref[jnp.maximum(j - 1, 0)])

    def fetch(ex):
        return (pltpu.make_async_copy(w1_hbm.at[ex], w1f, sem.at[0]),
                pltpu.make_async_copy(w2_hbm.at[ex], w2f, sem.at[1]))

    @pl.when(jnp.logical_and(live, j == 0))
    def _():
        for c in fetch(e):
            c.start()

    @pl.when(jnp.logical_and(live, new_expert))
    def _():
        for c in fetch(e):
            c.wait()
        w1b[...] = w1f[...].astype(BF16)
        w2b[...] = w2f[...].astype(BF16)
        nxt = nexte_ref[e]

        @pl.when(nxt != e)
        def _():
            for c in fetch(nxt):
                c.start()

    @pl.when(live)
    def _():
        hm = jnp.dot(x_ref[...], w1b[...], preferred_element_type=F32) + b1_ref[0]
        gate = jnp.minimum(hm[:, :de], SWIGLU_LIMIT)
        up = jnp.clip(hm[:, de:], -SWIGLU_LIMIT, SWIGLU_LIMIT)
        act = gate * _sigmoid(SWIGLU_ALPHA * gate) * (up + 1.0)
        y = jnp.dot(act.astype(BF16), w2b[...], preferred_element_type=F32) + b2_ref[0]
        o_ref[...] = y.astype(o_ref.dtype)

    @pl.when(jnp.logical_not(live))
    def _():
        o_ref[...] = jnp.zeros_like(o_ref)


def _experts(xs, plan, w1, b1, w2, b2):
    d = xs.shape[1]
    tb = MOE_BLOCK
    f2 = w1.shape[2]
    de = w2.shape[1]
    last = lambda j, nu: jnp.maximum(jnp.minimum(j, nu[0] - 1), 0)
    grid_spec = pltpu.PrefetchScalarGridSpec(
        num_scalar_prefetch=3,
        grid=(plan["nb"],),
        in_specs=[
            pl.BlockSpec((tb, d), lambda j, be, nu, ne: (last(j, nu), 0)),
            pl.BlockSpec(memory_space=pl.ANY),
            pl.BlockSpec((1, 1, f2), lambda j, be, nu, ne: (be[j], 0, 0)),
            pl.BlockSpec(memory_space=pl.ANY),
            pl.BlockSpec((1, 1, d), lambda j, be, nu, ne: (be[j], 0, 0)),
        ],
        out_specs=pl.BlockSpec((tb, d), lambda j, be, nu, ne: (j, 0)),
        scratch_shapes=[pltpu.VMEM((d, f2), F32), pltpu.VMEM((de, d), F32),
                        pltpu.VMEM((d, f2), BF16), pltpu.VMEM((de, d), BF16),
                        pltpu.SemaphoreType.DMA((2,))],
    )
    return pl.pallas_call(
        _expert_kernel,
        grid_spec=grid_spec,
        out_shape=jax.ShapeDtypeStruct(xs.shape, BF16),
        compiler_params=_cparams(("arbitrary",)),
    )(plan["block_e"], plan["n_used"], plan["next_expert"], xs, w1, b1, w2, b2)


COMBINE_TILES = 2


def _combine_kernel(gdst_ref, totg_ref, x1_ref, srow_ref, wt_ref, g_ref, yb_hbm, o_ref, stage, sem):
    s = pl.program_id(0)
    slot = s % 2
    nu = stage.shape[1]
    tm = x1_ref.shape[0] // nu
    rows = stage.shape[2]

    def copy_into(sl_, u):
        def copy(stage_row, sorted_row, nrows):
            return pltpu.make_async_copy(yb_hbm.at[pl.ds(sorted_row, nrows)],
                                         stage.at[sl_, u, pl.ds(stage_row, nrows)], sem.at[sl_, u])
        return copy

    def fetch(step, sl_, start):
        for u in range(nu):
            _granule_copies(step * nu + u, gdst_ref, totg_ref, copy_into(sl_, u), start=start)

    @pl.when(s == 0)
    def _():
        stage[...] = jnp.zeros_like(stage)
        fetch(s, slot, True)

    @pl.when(s + 1 < pl.num_programs(0))
    def _():
        fetch(s + 1, 1 - slot, True)

    fetch(s, slot, False)

    chunk = LANES
    row_id = lax.broadcasted_iota(jnp.int32, (chunk, rows), 1).astype(F32)
    for u in range(nu):
        for c in range(tm // chunk):
            sl = slice(u * tm + c * chunk, u * tm + (c + 1) * chunk)
            unsort = jnp.zeros((chunk, rows), F32)
            for k in range(TOP_K):
                unsort = jnp.where(row_id == srow_ref[sl, k:k + 1], wt_ref[sl, k:k + 1], unsort)
            y = jnp.dot(unsort.astype(BF16), stage[slot, u], preferred_element_type=F32)
            o_ref[sl, :] = _rms(x1_ref[sl, :] + y, g_ref[...])


def _combine(x1, yb, srow, top_w, plan, g_final, tm):
    n, d = x1.shape
    nu = COMBINE_TILES
    nt = n // (nu * tm)
    rows = _stage_rows(tm)
    grid_spec = pltpu.PrefetchScalarGridSpec(
        num_scalar_prefetch=2,
        grid=(nt,),
        in_specs=[
            pl.BlockSpec((nu * tm, d), lambda t, *_: (t, 0)),
            pl.BlockSpec((nu * tm, LANES), lambda t, *_: (t, 0)),
            pl.BlockSpec((nu * tm, LANES), lambda t, *_: (t, 0)),
            pl.BlockSpec((1, d), lambda t, *_: (0, 0)),
            pl.BlockSpec(memory_space=pl.ANY),
        ],
        out_specs=pl.BlockSpec((nu * tm, d), lambda t, *_: (t, 0)),
        scratch_shapes=[pltpu.VMEM((2, nu, rows, d), BF16), pltpu.SemaphoreType.DMA((2, nu))],
    )
    return pl.pallas_call(
        _combine_kernel,
        grid_spec=grid_spec,
        out_shape=jax.ShapeDtypeStruct((n, d), F32),
        compiler_params=_cparams(("arbitrary",)),
    )(plan["gran_dst"], plan["tot_gran"], x1, srow, top_w, g_final, yb)


def _rope_tables(seq):
    half = ATTN_HEAD_DIM // 2
    inv_freq = np.float32(ROPE_THETA) ** (-np.arange(0, half, 2, dtype=np.float32) / np.float32(half))
    pos = np.arange(seq)
    ang_r = (pos // GRID_W).astype(np.float32)[:, None] * inv_freq
    ang_c = (pos % GRID_W).astype(np.float32)[:, None] * inv_freq
    cos_t = np.concatenate([np.cos(ang_r), np.cos(ang_c)] * 2, axis=-1)
    sin_t = np.concatenate([-np.sin(ang_r), -np.sin(ang_c), np.sin(ang_r), np.sin(ang_c)], axis=-1)
    return jnp.asarray(cos_t, F32), jnp.asarray(sin_t, F32)


def _token_mixer(x2, batch, seq, g_mix, w_in, conv_w, conv_b, dt_bias_f, dt_bias_b, a_log_f, a_log_b, d_skip,
                 g_ssm, q_norm_g, k_norm_g, w_br_ssm, w_br_attn, w_out):
    n, d = x2.shape
    z_end = SSM_INNER
    xbc_end = z_end + CONV_CH
    dtf_end = xbc_end + SSM_HEADS
    dtb_end = dtf_end + SSM_HEADS
    q_end = dtb_end + ATTN_HEADS * ATTN_HEAD_DIM
    k_end = q_end + ATTN_KV_HEADS * ATTN_HEAD_DIM
    v_end = k_end + ATTN_KV_HEADS * ATTN_HEAD_DIM
    head_cols = lambda w: _rope_head_order(w.reshape(d, -1, ATTN_HEAD_DIM)).reshape(d, -1)
    w_main = jnp.concatenate([w_in[:, :z_end], head_cols(w_in[:, dtb_end:q_end]), w_in[:, v_end:],
                              w_in[:, z_end:xbc_end], head_cols(w_in[:, q_end:k_end]), w_in[:, k_end:v_end]],
                             axis=1).astype(BF16)
    w_dt = jnp.pad(w_in[:, xbc_end:dtb_end], ((0, 0), (0, LANES - 2 * SSM_HEADS))).astype(BF16)

    proj, dt = _in_proj(x2, g_mix.reshape(1, d), w_main, w_dt)
    proj3 = proj.reshape(batch, seq, PROJ_COLS)

    xbc = _conv(proj3, conv_w, conv_b.reshape(1, CONV_CH))

    dt3 = dt.reshape(batch, seq, LANES)
    dtt3 = jnp.swapaxes(dt3[:, :, :2 * SSM_HEADS], 1, 2)
    bias = jnp.concatenate([dt_bias_f, dt_bias_b])
    alog = jnp.concatenate([a_log_f, a_log_b])
    pad_row = lambda v: jnp.pad(v, (0, LANES - 2 * SSM_HEADS)).reshape(1, LANES)
    y_f, y_b = _ssd(xbc, dt3, dtt3, pad_row(bias), bias.reshape(-1, 1), pad_row(alog), alog.reshape(-1, 1))

    cos_t, sin_t = _rope_tables(seq)
    q_rot, k_rot = _qk_prep(proj, cos_t, sin_t, _rope_head_order(q_norm_g).reshape(1, -1),
                            _rope_head_order(k_norm_g).reshape(1, -1), seq)
    vt3 = jnp.swapaxes(proj3[:, :, V_OFF:V_OFF + ATTN_KV_HEADS * ATTN_HEAD_DIM], 1, 2)
    attn = _flash(q_rot.reshape(batch, seq, -1), k_rot.reshape(batch, seq, -1), vt3, q_norm_g, k_norm_g)

    return _merge(y_f.reshape(n, -1), y_b.reshape(n, -1), xbc.reshape(n, CONV_CH), proj, attn.reshape(n, -1),
                  x2, jnp.repeat(d_skip, SSM_HEAD_DIM).reshape(1, -1), g_ssm.reshape(1, -1),
                  w_br_ssm.astype(BF16), w_br_attn.astype(BF16), w_out.astype(BF16))


def _moe_and_final_norm(x1, g_ffn, w_router, b_router, w_mlp1, b_mlp1, w_mlp2, b_mlp2, g_final):
    n, d = x1.shape
    tm = min(MOE_TILE, n)
    w_t = w_router.T
    w_hi = w_t.astype(BF16)
    w_lo = (w_t - w_hi.astype(F32)).astype(BF16)
    g_ffn_row = g_ffn.reshape(1, d)
    h2, srow_t, srow, top_w, cnt = _router(x1, g_ffn_row, jnp.concatenate([w_hi, w_lo, w_hi], axis=1),
                                           b_router.reshape(N_EXPERTS, 1))
    plan = _routing_plan(cnt[:, :, 0].astype(jnp.int32), n, tm, MOE_BLOCK)
    xs = _dispatch(h2, srow_t, plan, tm)
    yb = _experts(xs, plan, w_mlp1, b_mlp1[:, None, :], w_mlp2, b_mlp2[:, None, :])
    return _combine(x1, yb, srow, top_w, plan, g_final.reshape(1, d), tm)


def kernel(x, g_mix, w_in, conv_w, conv_b, dt_bias_f, dt_bias_b, a_log_f, a_log_b, d_skip, g_ssm, q_norm_g,
           k_norm_g, w_br_ssm, w_br_attn, w_out, g_ffn, w_router, b_router, w_mlp1, b_mlp1, w_mlp2, b_mlp2,
           g_final):
    batch, seq, d = x.shape
    assert g_mix.shape[0] == 1, "single-layer model: the final rmsnorm is fused into the MoE combine"
    x2 = x.reshape(batch * seq, d)
    x1 = _token_mixer(x2, batch, seq, g_mix[0], w_in[0], conv_w[0], conv_b[0], dt_bias_f[0], dt_bias_b[0],
                      a_log_f[0], a_log_b[0], d_skip[0], g_ssm[0], q_norm_g[0], k_norm_g[0], w_br_ssm[0],
                      w_br_attn[0], w_out[0])
    out = _moe_and_final_norm(x1, g_ffn[0], w_router[0], b_router[0], w_mlp1[0], b_mlp1[0], w_mlp2[0],
                              b_mlp2[0], g_final)
    return out.reshape(batch, seq, d)
```

```python
import functools
import math

import jax
import jax.numpy as jnp
import numpy as np
from jax import lax
from jax.experimental import pallas as pl
from jax.experimental.pallas import tpu as pltpu

F32 = jnp.float32
BF16 = jnp.bfloat16

NORM_EPS = 1e-6
GRID_W = 64
SSM_HEADS = 16
SSM_HEAD_DIM = 64
SSM_INNER = SSM_HEADS * SSM_HEAD_DIM
SSM_GROUPS = 2
SSM_STATE = 128
SSM_CONV = 5
CONV_CH = SSM_INNER + 2 * SSM_GROUPS * SSM_STATE
ATTN_HEADS = 8
ATTN_KV_HEADS = 2
ATTN_HEAD_DIM = 128
ROPE_THETA = 10000.0
N_EXPERTS = 32
TOP_K = 4
SWIGLU_LIMIT = 7.0
SWIGLU_ALPHA = 1.702

LANES = 128
BF16_SUBLANES = 16
VMEM_LIMIT = 56 * 1024 * 1024

Z_OFF, Q_OFF, GATE_OFF, XBC_OFF = 0, 1024, 2048, 4096
K_OFF, V_OFF, PROJ_COLS = 5632, 5888, 6144

MOE_TILE = 512
MOE_BLOCK = 512
MOE_GRAN = BF16_SUBLANES


def _cparams(sem):
    return pltpu.CompilerParams(dimension_semantics=sem, vmem_limit_bytes=VMEM_LIMIT)


def _sigmoid(x):
    return 1.0 / (1.0 + jnp.exp(-x))


def _softplus(x):
    return jnp.maximum(x, 0.0) + jnp.log(1.0 + jnp.exp(-jnp.abs(x)))


def _rms(x, g):
    ms = jnp.mean(x * x, axis=-1, keepdims=True)
    return x * lax.rsqrt(ms + NORM_EPS) * g


def _inproj_kernel(x_ref, g_ref, w_ref, wdt_ref, o_ref, dt_ref, h_scr):
    @pl.when(pl.program_id(1) == 0)
    def _():
        hb = _rms(x_ref[...], g_ref[...]).astype(BF16)
        h_scr[...] = hb
        dt_ref[...] = jnp.dot(hb, wdt_ref[...], preferred_element_type=F32)

    o_ref[...] = jnp.dot(h_scr[...], w_ref[...], preferred_element_type=F32).astype(o_ref.dtype)


def _in_proj(x2, g_mix, w_main, w_dt):
    n, d = x2.shape
    tm = min(1024, n)
    tn = 2048
    return pl.pallas_call(
        _inproj_kernel,
        grid=(n // tm, PROJ_COLS // tn),
        in_specs=[
            pl.BlockSpec((tm, d), lambda i, j: (i, 0)),
            pl.BlockSpec((1, d), lambda i, j: (0, 0)),
            pl.BlockSpec((d, tn), lambda i, j: (0, j)),
            pl.BlockSpec((d, LANES), lambda i, j: (0, 0)),
        ],
        out_specs=[
            pl.BlockSpec((tm, tn), lambda i, j: (i, j)),
            pl.BlockSpec((tm, LANES), lambda i, j: (i, 0)),
        ],
        out_shape=[
            jax.ShapeDtypeStruct((n, PROJ_COLS), BF16),
            jax.ShapeDtypeStruct((n, LANES), F32),
        ],
        scratch_shapes=[pltpu.VMEM((tm, d), BF16)],
        compiler_params=_cparams(("arbitrary", "arbitrary")),
    )(x2, g_mix, w_main, w_dt)


CONV_HALO = 64
CONV_ROWS = 128


def _conv_kernel(prev_ref, cur_ref, next_ref, shift_ref, w_ref, b_ref, o_ref):
    s = pl.program_id(1)
    ts = cur_ref.shape[1]
    prev = prev_ref[0]
    nxt = next_ref[0]
    zero = jnp.zeros_like(prev)
    ext = jnp.concatenate([jnp.where(s == 0, zero, prev), cur_ref[0],
                           jnp.where(s == pl.num_programs(1) - 1, zero, nxt)], axis=0)
    shift = shift_ref[...]
    for rb in range(ts // CONV_ROWS):
        lo = rb * CONV_ROWS
        taps = jnp.dot(shift, ext[lo:lo + CONV_ROWS + 2 * CONV_HALO], preferred_element_type=F32)
        acc = b_ref[...] + w_ref[0:1, :] * taps[0:CONV_ROWS]
        for k in range(1, SSM_CONV):
            acc = acc + w_ref[k:k + 1, :] * taps[k * CONV_ROWS:(k + 1) * CONV_ROWS]
        o_ref[0, lo:lo + CONV_ROWS, :] = (acc * _sigmoid(acc)).astype(o_ref.dtype)


def _conv(proj3, conv_w, conv_b):
    b, s, _ = proj3.shape
    ts = min(2048, s)
    tc = 512
    halo = CONV_HALO
    hb = ts // halo
    col0 = XBC_OFF // tc
    pad = (SSM_CONV - 1) // 2
    out_row = np.arange(SSM_CONV * CONV_ROWS)
    src = out_row % CONV_ROWS + halo + out_row // CONV_ROWS - pad
    shift = jnp.asarray(src[:, None] == np.arange(CONV_ROWS + 2 * halo)[None, :], BF16)
    return pl.pallas_call(
        _conv_kernel,
        grid=(b, s // ts, CONV_CH // tc),
        in_specs=[
            pl.BlockSpec((1, halo, tc), lambda bi, si, ci: (bi, jnp.maximum(si * hb - 1, 0), col0 + ci)),
            pl.BlockSpec((1, ts, tc), lambda bi, si, ci: (bi, si, col0 + ci)),
            pl.BlockSpec((1, halo, tc),
                         lambda bi, si, ci: (bi, jnp.minimum((si + 1) * hb, s // halo - 1), col0 + ci)),
            pl.BlockSpec(shift.shape, lambda bi, si, ci: (0, 0)),
            pl.BlockSpec((SSM_CONV, tc), lambda bi, si, ci: (0, ci)),
            pl.BlockSpec((1, tc), lambda bi, si, ci: (0, ci)),
        ],
        out_specs=pl.BlockSpec((1, ts, tc), lambda bi, si, ci: (bi, si, ci)),
        out_shape=jax.ShapeDtypeStruct((b, s, CONV_CH), BF16),
        compiler_params=_cparams(("arbitrary", "arbitrary", "arbitrary")),
    )(proj3, proj3, proj3, shift, conv_w, conv_b)


SSM_CHUNKS_PER_STEP = 4


def _ssd_kernel(xf_ref, xb_ref, dtf_ref, dtb_ref, dttf_ref, dttb_ref, brow_ref, bcol_ref,
                arow_ref, acol_ref, yf_ref, yb_ref, st_ref):
    L = SSM_STATE
    nsub = xf_ref.shape[1] // L
    hg = SSM_HEADS // SSM_GROUPS
    pairs = hg // 2

    @pl.when(pl.program_id(1) == 0)
    def _():
        st_ref[...] = jnp.zeros_like(st_ref)

    rows = lax.broadcasted_iota(jnp.int32, (L, L), 0)
    cols = lax.broadcasted_iota(jnp.int32, (L, L), 1)
    lower = rows >= cols
    upper = rows <= cols
    ltri = lower.astype(BF16)
    utri = upper.astype(BF16)
    lane = lax.broadcasted_iota(jnp.int32, (L, LANES), 1)
    first_half = lane < SSM_HEAD_DIM
    lane1 = lax.broadcasted_iota(jnp.int32, (1, LANES), 1)
    log2e = math.log2(math.e)
    a_row = -jnp.exp(arow_ref[...]) * log2e
    a_col = -jnp.exp(acol_ref[...]) * log2e

    def split3(v):
        hi = v.astype(BF16)
        r1 = v - hi.astype(F32)
        mid = r1.astype(BF16)
        return hi, mid, (r1 - mid.astype(F32)).astype(BF16)

    def cumsum_cols(tri, v):
        return jnp.dot(jnp.concatenate([tri] * 3, axis=1), jnp.concatenate(split3(v), axis=0),
                       preferred_element_type=F32)

    def cumsum_rows(v, tri):
        return jnp.dot(jnp.concatenate(split3(v), axis=1), jnp.concatenate([tri] * 3, axis=0),
                       preferred_element_type=F32)

    for sub, d in [(sub, d) for sub in range(nsub) for d in range(2)]:
        x_ref, dt_ref, dtt_ref, y_ref = ((xf_ref, dtf_ref, dttf_ref, yf_ref) if d == 0
                                         else (xb_ref, dtb_ref, dttb_ref, yb_ref))
        r0 = (sub if d == 0 else nsub - 1 - sub) * L
        rs = slice(r0, r0 + L)
        a = _softplus(dt_ref[0, rs, :] + brow_ref[...]) * a_row
        dt_t = _softplus(dtt_ref[0, :, rs] + bcol_ref[...])
        a_t = dt_t * a_col
        if d == 0:
            cs_col = cumsum_cols(ltri, a)
            cs_row = cumsum_rows(a_t, utri)
            tot = cs_col[L - 1:L, :]
            tot_t = cs_row[:, L - 1:L]
            mask = lower
        else:
            cs_col = cumsum_cols(utri, a)
            cs_row = cumsum_rows(a_t, ltri)
            tot = cs_col[0:1, :]
            tot_t = cs_row[:, 0:1]
            mask = upper
        w_t = dt_t * jnp.exp2(tot_t - cs_row)
        src_t = cs_row - jnp.log2(dt_t)
        chunk_decay = jnp.exp2(tot)

        for g in range(SSM_GROUPS):
            boff = SSM_INNER + g * SSM_STATE
            coff = SSM_INNER + SSM_GROUPS * SSM_STATE + g * SSM_STATE
            bm = x_ref[0, rs, boff:boff + SSM_STATE]
            cm = x_ref[0, rs, coff:coff + SSM_STATE]
            cb = lax.dot_general(cm, bm, (((1,), (1,)), ((), ())), preferred_element_type=F32)
            bt = bm.astype(F32).T
            st = st_ref[d, g]
            y_off = jnp.dot(cm, st.astype(BF16), preferred_element_type=F32)
            for pr in range(pairs):
                h0 = d * SSM_HEADS + g * hg + 2 * pr
                xoff = (g * pairs + pr) * LANES
                xs = x_ref[0, rs, xoff:xoff + LANES]
                zero = jnp.zeros_like(xs)
                rhs = jnp.concatenate([jnp.where(first_half, xs, zero),
                                       jnp.where(first_half, zero, xs)], axis=0)
                ms, ws, dins = [], [], []
                for hh in (h0, h0 + 1):
                    cs_b = jnp.broadcast_to(cs_col[:, hh:hh + 1], (L, L))
                    seg = cs_b - src_t[hh:hh + 1, :]
                    m = cb * jnp.exp2(jnp.where(mask, seg, -jnp.inf))
                    ms.append(m.astype(BF16))
                    ws.append((bt * w_t[hh:hh + 1, :]).astype(BF16))
                    dins.append(jnp.exp2(cs_b))
                y = jnp.dot(jnp.concatenate(ms, axis=1), rhs, preferred_element_type=F32)
                y = y + y_off[:, pr * LANES:(pr + 1) * LANES] * jnp.where(first_half, dins[0], dins[1])
                y_ref[0, rs, xoff:xoff + LANES] = y.astype(y_ref.dtype)
                new_st = jnp.dot(jnp.concatenate(ws, axis=1), rhs, preferred_element_type=F32)
                cd = jnp.where(lane1 < SSM_HEAD_DIM, chunk_decay[:, h0:h0 + 1], chunk_decay[:, h0 + 1:h0 + 2])
                st_ref[d, g, :, pr * LANES:(pr + 1) * LANES] = st[:, pr * LANES:(pr + 1) * LANES] * cd + new_st


def _ssd(xbc, dt3, dtt3, bias_row, bias_col, alog_row, alog_col):
    b, s, _ = xbc.shape
    L = SSM_CHUNKS_PER_STEP * SSM_STATE
    nc = s // L
    hg = SSM_HEADS // SSM_GROUPS
    fwd = lambda bi, ci: (bi, ci, 0)
    bwd = lambda bi, ci: (bi, nc - 1 - ci, 0)
    fwd_t = lambda bi, ci: (bi, 0, ci)
    bwd_t = lambda bi, ci: (bi, 0, nc - 1 - ci)
    const = lambda bi, ci: (0, 0)
    return pl.pallas_call(
        _ssd_kernel,
        grid=(b, nc),
        in_specs=[
            pl.BlockSpec((1, L, CONV_CH), fwd),
            pl.BlockSpec((1, L, CONV_CH), bwd),
            pl.BlockSpec((1, L, LANES), fwd),
            pl.BlockSpec((1, L, LANES), bwd),
            pl.BlockSpec((1, 2 * SSM_HEADS, L), fwd_t),
            pl.BlockSpec((1, 2 * SSM_HEADS, L), bwd_t),
            pl.BlockSpec((1, LANES), const),
            pl.BlockSpec((2 * SSM_HEADS, 1), const),
            pl.BlockSpec((1, LANES), const),
            pl.BlockSpec((2 * SSM_HEADS, 1), const),
        ],
        out_specs=[
            pl.BlockSpec((1, L, SSM_INNER), fwd),
            pl.BlockSpec((1, L, SSM_INNER), bwd),
        ],
        out_shape=[jax.ShapeDtypeStruct((b, s, SSM_INNER), BF16)] * 2,
        scratch_shapes=[pltpu.VMEM((2, SSM_GROUPS, SSM_STATE, hg * SSM_HEAD_DIM), F32)],
        compiler_params=_cparams(("arbitrary", "arbitrary")),
    )(xbc, xbc, dt3, dt3, dtt3, dtt3, bias_row, bias_col, alog_row, alog_col)


def _rope_head_order(a):
    q4 = ATTN_HEAD_DIM // 4
    return jnp.concatenate([a[..., 0:q4], a[..., 2 * q4:3 * q4], a[..., q4:2 * q4], a[..., 3 * q4:]], axis=-1)


def _rope_norm(t, g, cos, sin_signed, ones):
    sq = t * t
    hi = sq.astype(BF16)
    lo = (sq - hi.astype(F32)).astype(BF16)
    ms = jnp.dot(jnp.concatenate([hi, lo], axis=1), ones, preferred_element_type=F32) * (1.0 / ATTN_HEAD_DIM)
    tn = t * lax.rsqrt(ms + NORM_EPS) * g
    return tn * cos + pltpu.roll(tn, ATTN_HEAD_DIM // 2, 1) * sin_signed


Q_SCALE = ATTN_HEAD_DIM ** -0.5 * math.log2(math.e)


def _qkprep_kernel(q_ref, k_ref, cos_ref, sin_ref, qg_ref, kg_ref, qo_ref, ko_ref):
    cos = cos_ref[...]
    sin = sin_ref[...]
    ones = jnp.ones((2 * ATTN_HEAD_DIM, ATTN_HEAD_DIM), BF16)

    def heads(src_ref, g_ref, dst_ref, n_heads, scale):
        for h in range(n_heads):
            sl = slice(h * ATTN_HEAD_DIM, (h + 1) * ATTN_HEAD_DIM)
            r = _rope_norm(src_ref[:, sl].astype(F32), g_ref[...], cos, sin, ones) * scale
            dst_ref[:, sl] = r.astype(dst_ref.dtype)

    heads(q_ref, qg_ref, qo_ref, ATTN_HEADS, Q_SCALE)
    heads(k_ref, kg_ref, ko_ref, ATTN_KV_HEADS, 1.0)


def _qk_prep(proj, cos_t, sin_t, q_norm_g, k_norm_g, seq):
    n = proj.shape[0]
    tm = min(1024, seq)
    qw = ATTN_HEADS * ATTN_HEAD_DIM
    kw = ATTN_KV_HEADS * ATTN_HEAD_DIM
    spt = seq // tm
    return pl.pallas_call(
        _qkprep_kernel,
        grid=(n // tm,),
        in_specs=[
            pl.BlockSpec((tm, qw), lambda i: (i, Q_OFF // qw)),
            pl.BlockSpec((tm, kw), lambda i: (i, K_OFF // kw)),
            pl.BlockSpec((tm, ATTN_HEAD_DIM), lambda i: (i % spt, 0)),
            pl.BlockSpec((tm, ATTN_HEAD_DIM), lambda i: (i % spt, 0)),
            pl.BlockSpec((1, ATTN_HEAD_DIM), lambda i: (0, 0)),
            pl.BlockSpec((1, ATTN_HEAD_DIM), lambda i: (0, 0)),
        ],
        out_specs=[
            pl.BlockSpec((tm, qw), lambda i: (i, 0)),
            pl.BlockSpec((tm, kw), lambda i: (i, 0)),
        ],
        out_shape=[jax.ShapeDtypeStruct((n, qw), BF16), jax.ShapeDtypeStruct((n, kw), BF16)],
        compiler_params=_cparams(("arbitrary",)),
    )(proj, proj, cos_t, sin_t, q_norm_g, k_norm_g)


def _flash_kernel(small_ref, q_ref, k_ref, vt_ref, o_ref, *, tk, th):
    q = q_ref[0]
    tq = q.shape[0]
    seq = k_ref.shape[1]
    nt = (((1,), (1,)), ((), ()))
    small = small_ref[0] != 0

    @pl.when(small)
    def _():
        l = acc_t = None
        for c in range(seq // th):
            ks = slice(c * th, (c + 1) * th)
            p_t = jnp.exp2(lax.dot_general(k_ref[0, ks, :], q, nt, preferred_element_type=F32))
            l_c = jnp.sum(p_t, axis=0, keepdims=True)
            a_c = jnp.dot(vt_ref[0, :, ks], p_t.astype(BF16), preferred_element_type=F32)
            l = l_c if l is None else l + l_c
            acc_t = a_c if acc_t is None else acc_t + a_c
        o_ref[0] = (acc_t / l).T.astype(o_ref.dtype)

    @pl.when(jnp.logical_not(small))
    def _():
        def body(i, carry):
            m, l, acc_t = carry
            off = pl.multiple_of(i * tk, tk)
            s_t = lax.dot_general(k_ref[0, pl.ds(off, tk), :], q, nt, preferred_element_type=F32)
            m_new = jnp.maximum(m, jnp.max(s_t, axis=0, keepdims=True))
            alpha = jnp.exp2(m - m_new)
            p_t = jnp.exp2(s_t - m_new)
            l = alpha * l + jnp.sum(p_t, axis=0, keepdims=True)
            acc_t = alpha * acc_t + jnp.dot(vt_ref[0, :, pl.ds(off, tk)], p_t.astype(BF16),
                                            preferred_element_type=F32)
            return m_new, l, acc_t

        init = (jnp.full((1, tq), -jnp.inf, F32), jnp.zeros((1, tq), F32),
                jnp.zeros((ATTN_HEAD_DIM, tq), F32))
        _, l, acc_t = lax.fori_loop(0, seq // tk, body, init)
        o_ref[0] = (acc_t / l).T.astype(o_ref.dtype)


SCORE_BOUND = 59.0


def _flash(q3, k3, vt3, q_norm_g, k_norm_g):
    b, s, _ = q3.shape
    tq = min(1024, s)
    tk = min(2048, s)
    th = min(4096, s)
    nq = s // tq
    grp = ATTN_HEADS // ATTN_KV_HEADS
    hd = ATTN_HEAD_DIM
    bound = hd * Q_SCALE * jnp.max(jnp.abs(q_norm_g)) * jnp.max(jnp.abs(k_norm_g)) * 1.02
    small = (bound <= SCORE_BOUND).astype(jnp.int32).reshape(1)
    grid_spec = pltpu.PrefetchScalarGridSpec(
        num_scalar_prefetch=1,
        grid=(b, ATTN_HEADS, nq),
        in_specs=[
            pl.BlockSpec((1, tq, hd), lambda bi, h, qi, sm: (bi, qi, h)),
            pl.BlockSpec((1, s, hd), lambda bi, h, qi, sm: (bi, 0, h // grp)),
            pl.BlockSpec((1, hd, s), lambda bi, h, qi, sm: (bi, h // grp, 0)),
        ],
        out_specs=pl.BlockSpec((1, tq, hd), lambda bi, h, qi, sm: (bi, qi, h)),
    )
    return pl.pallas_call(
        functools.partial(_flash_kernel, tk=tk, th=th),
        grid_spec=grid_spec,
        out_shape=jax.ShapeDtypeStruct((b, s, ATTN_HEADS * hd), BF16),
        compiler_params=_cparams(("arbitrary", "arbitrary", "arbitrary")),
    )(small, q3, k3, vt3)


def _merge_kernel(yf_ref, yb_ref, xs_ref, z_ref, gate_ref, attn_ref, x_ref, dskip_ref, gssm_ref,
                  wbs_ref, wba_ref, wo_ref, x1_ref):
    d = x_ref.shape[1]
    xs = xs_ref[...].astype(F32)
    y = yf_ref[...].astype(F32) + yb_ref[...].astype(F32) + xs * dskip_ref[...]
    z = z_ref[...].astype(F32)
    y = _rms(y * (z * _sigmoid(z)), gssm_ref[...])
    br_ssm = jnp.dot(y.astype(BF16), wbs_ref[...], preferred_element_type=F32)
    br_attn = jnp.dot(attn_ref[...], wba_ref[...], preferred_element_type=F32)
    g_s = _sigmoid(gate_ref[:, :d].astype(F32))
    g_a = _sigmoid(gate_ref[:, d:].astype(F32))
    merged = (g_s * br_ssm + g_a * br_attn).astype(BF16)
    x1_ref[...] = x_ref[...] + jnp.dot(merged, wo_ref[...], preferred_element_type=F32)


def _merge(y_f, y_b, xbc, proj, attn, x2, dskip_row, g_ssm, w_br_ssm, w_br_attn, w_out):
    n, d = x2.shape
    tm = min(512, n)
    row = lambda i: (i, 0)
    const = lambda i: (0, 0)
    return pl.pallas_call(
        _merge_kernel,
        grid=(n // tm,),
        in_specs=[
            pl.BlockSpec((tm, d), row),
            pl.BlockSpec((tm, d), row),
            pl.BlockSpec((tm, d), row),
            pl.BlockSpec((tm, d), lambda i: (i, Z_OFF // d)),
            pl.BlockSpec((tm, 2 * d), lambda i: (i, GATE_OFF // (2 * d))),
            pl.BlockSpec((tm, d), row),
            pl.BlockSpec((tm, d), row),
            pl.BlockSpec((1, d), const),
            pl.BlockSpec((1, d), const),
            pl.BlockSpec((d, d), const),
            pl.BlockSpec((d, d), const),
            pl.BlockSpec((d, d), const),
        ],
        out_specs=pl.BlockSpec((tm, d), row),
        out_shape=jax.ShapeDtypeStruct((n, d), F32),
        compiler_params=_cparams(("arbitrary",)),
    )(y_f, y_b, xbc, proj, proj, attn, x2, dskip_row, g_ssm, w_br_ssm, w_br_attn, w_out)


def _router_kernel(x1_ref, g_ref, w_ref, b_ref, h_ref, srow_ref, srow_l_ref, wt_l_ref, cnt_ref):
    tm = x1_ref.shape[0]
    h = _rms(x1_ref[...], g_ref[...])
    h_hi = h.astype(BF16)
    h_lo = (h - h_hi.astype(F32)).astype(BF16)
    h_ref[...] = h_hi
    logits = lax.dot_general(w_ref[...], jnp.concatenate([h_hi, h_hi, h_lo], axis=1), (((1,), (1,)), ((), ())),
                             preferred_element_type=F32) + b_ref[...]
    expert = lax.broadcasted_iota(jnp.int32, logits.shape, 0)
    slot = lax.broadcasted_iota(jnp.int32, srow_ref.shape, 0)
    chosen = jnp.zeros(logits.shape, F32)
    vals, hits = [], []
    for k in range(TOP_K):
        m = jnp.max(logits, axis=0, keepdims=True)
        idx = jnp.min(jnp.where(logits == m, expert, N_EXPERTS), axis=0, keepdims=True)
        vals.append(m)
        hit = expert == idx
        hits.append(hit)
        chosen = jnp.where(hit, 1.0, chosen)
        logits = jnp.where(hit, -jnp.inf, logits)
    es = [jnp.exp(v - vals[0]) for v in vals]
    tot = es[0] + es[1] + es[2] + es[3]

    chosen16 = chosen.astype(BF16)
    cnt_col = jnp.sum(chosen, axis=1, keepdims=True)
    cnt_row = lax.dot_general(jnp.ones((8, tm), BF16), chosen16, (((1,), (1,)), ((), ())),
                              preferred_element_type=F32)[0:1, :]
    seg_row = jnp.ceil(cnt_row * (1.0 / MOE_GRAN)) * MOE_GRAN
    before = (lax.broadcasted_iota(jnp.int32, (N_EXPERTS, N_EXPERTS), 1)
              < lax.broadcasted_iota(jnp.int32, (N_EXPERTS, N_EXPERTS), 0))
    seg_off = jnp.sum(jnp.where(before, seg_row, 0.0), axis=1, keepdims=True)
    earlier = (lax.broadcasted_iota(jnp.int32, (tm, tm), 0)
               < lax.broadcasted_iota(jnp.int32, (tm, tm), 1)).astype(BF16)
    pos = jnp.dot(chosen16, earlier, preferred_element_type=F32) + seg_off

    val_out = jnp.zeros(srow_ref.shape, F32)
    row_out = jnp.zeros(srow_ref.shape, F32)
    for k in range(TOP_K):
        val_out = jnp.where(slot == k, es[k] / tot, val_out)
        row_out = jnp.where(slot == k, jnp.sum(jnp.where(hits[k], pos, 0.0), axis=0, keepdims=True), row_out)
    srow_ref[...] = row_out
    fill = jnp.zeros((LANES - srow_ref.shape[0], tm), F32)
    srow_l_ref[...] = jnp.concatenate([row_out, fill], axis=0).T
    wt_l_ref[...] = jnp.concatenate([val_out, fill], axis=0).T
    cnt_ref[0] = jnp.broadcast_to(cnt_col, cnt_ref.shape[1:])


def _router(x1, g_ffn, w_router_t, b_router_col):
    n, d = x1.shape
    tm = min(MOE_TILE, n)
    return pl.pallas_call(
        _router_kernel,
        grid=(n // tm,),
        in_specs=[
            pl.BlockSpec((tm, d), lambda i: (i, 0)),
            pl.BlockSpec((1, d), lambda i: (0, 0)),
            pl.BlockSpec((N_EXPERTS, 3 * d), lambda i: (0, 0)),
            pl.BlockSpec((N_EXPERTS, 1), lambda i: (0, 0)),
        ],
        out_specs=[pl.BlockSpec((tm, d), lambda i: (i, 0)),
                   pl.BlockSpec((8, tm), lambda i: (0, i)),
                   pl.BlockSpec((tm, LANES), lambda i: (i, 0)),
                   pl.BlockSpec((tm, LANES), lambda i: (i, 0)),
                   pl.BlockSpec((1, N_EXPERTS, LANES), lambda i: (i, 0, 0))],
        out_shape=[jax.ShapeDtypeStruct((n, d), BF16),
                   jax.ShapeDtypeStruct((8, n), F32),
                   jax.ShapeDtypeStruct((n, LANES), F32),
                   jax.ShapeDtypeStruct((n, LANES), F32),
                   jax.ShapeDtypeStruct((n // tm, N_EXPERTS, LANES), F32)],
        compiler_params=_cparams(("arbitrary",)),
    )(x1, g_ffn, w_router_t, b_router_col)


def _routing_plan(cnt, n, tm, tb):
    nt = n // tm
    gran = MOE_GRAN
    rc = (cnt + gran - 1) // gran * gran
    covered = jnp.sum(rc, axis=0)
    region = (covered + tb - 1) // tb * tb
    pad_end = jnp.cumsum(region)
    pad_start = pad_end - region
    seg_start = pad_start[None, :] + jnp.cumsum(rc, axis=0) - rc
    n_used = pad_end[-1] // tb
    nb = (TOP_K * n + nt * N_EXPERTS * (gran - 1) + N_EXPERTS * (tb - 1) + tb - 1) // tb
    blk = jnp.arange(nb, dtype=jnp.int32)
    block_e = jnp.sum((pad_end[None, :] <= (jnp.minimum(blk, n_used - 1) * tb)[:, None]).astype(jnp.int32), axis=1)
    block_e = jnp.minimum(block_e, N_EXPERTS - 1)
    flat = lambda a: a.reshape(-1).astype(jnp.int32)
    tail_gran = (region - covered) // gran
    ngran = rc // gran
    g_end = jnp.cumsum(ngran, axis=1)
    gi = jnp.arange(_stage_rows(tm) // gran, dtype=jnp.int32)
    g_exp = jnp.minimum(jnp.sum((g_end[:, None, :] <= gi[None, :, None]).astype(jnp.int32), axis=2), N_EXPERTS - 1)
    mine = g_exp[:, :, None] == jnp.arange(N_EXPERTS, dtype=jnp.int32)[None, None, :]
    seg_base = seg_start - gran * (g_end - ngran)
    gran_dst = jnp.sum(jnp.where(mine, seg_base[:, None, :], 0), axis=2) + gran * gi[None, :]
    ids = jnp.arange(N_EXPERTS, dtype=jnp.int32)
    later = jnp.where((ids[None, :] > ids[:, None]) & (region[None, :] > 0), ids[None, :], N_EXPERTS)
    nxt = jnp.min(later, axis=1)
    next_expert = jnp.where(nxt < N_EXPERTS, nxt, ids)
    return dict(gran_dst=flat(gran_dst), tot_gran=flat(g_end[:, -1]), block_e=flat(block_e),
                next_expert=flat(next_expert),
                n_used=flat(n_used), nb=nb,
                tail_start=flat(pad_start + covered), tail_gran=flat(tail_gran),
                tot_tail=flat(jnp.sum(tail_gran)))


def _stage_rows(tm):
    rows = TOP_K * tm + N_EXPERTS * (MOE_GRAN - 1)
    return (rows + 2 * LANES - 1) // (2 * LANES) * (2 * LANES)


GRAN_UNROLL = 4


def _granule_copies(t, gdst_ref, totg_ref, make_copy, start):
    total = totg_ref[t]
    base = t * (gdst_ref.shape[0] // totg_ref.shape[0])
    full = lax.shift_right_logical(total, GRAN_UNROLL.bit_length() - 1)

    def one(i, priority=0):
        make_copy(pl.multiple_of(i * MOE_GRAN, MOE_GRAN), pl.multiple_of(gdst_ref[base + i], MOE_GRAN),
                  MOE_GRAN).start(priority=priority)

    def group(q, c):
        if start:
            for u in range(GRAN_UNROLL):
                one(q * GRAN_UNROLL + u, u % 2)
        else:
            make_copy(0, 0, GRAN_UNROLL * MOE_GRAN).wait()
        return c

    def rest(i, c):
        if start:
            one(i)
        else:
            make_copy(0, 0, MOE_GRAN).wait()
        return c

    lax.fori_loop(0, full, group, 0)
    lax.fori_loop(full * GRAN_UNROLL, total, rest, 0)


def _dispatch_kernel(gdst_ref, totg_ref, tstart_ref, tgran_ref, misc_ref,
                     h_ref, srow_ref, xs_hbm, stage, zbuf, sem, zsem, *, n_blocks):
    t = pl.program_id(0)
    slot = t % 2
    tm = h_ref.shape[0]
    rows = stage.shape[1]
    h = h_ref[...]
    srows = [srow_ref[k:k + 1, :] for k in range(TOP_K)]
    chunk = 2 * LANES
    row_id = lax.broadcasted_iota(jnp.int32, (chunk, tm), 0).astype(F32)
    for c in range(rows // chunk):
        perm = jnp.zeros((chunk, tm), F32)
        for srow in srows:
            perm = jnp.where(row_id == srow - float(c * chunk), 1.0, perm)
        stage[slot, c * chunk:(c + 1) * chunk, :] = jnp.dot(
            perm.astype(BF16), h, preferred_element_type=F32).astype(stage.dtype)

    def copy_from(s):
        def copy(stage_row, sorted_row, nrows):
            return pltpu.make_async_copy(stage.at[s, pl.ds(stage_row, nrows)],
                                         xs_hbm.at[pl.ds(sorted_row, nrows)], sem.at[s])
        return copy

    _granule_copies(t, gdst_ref, totg_ref, copy_from(slot), start=True)

    @pl.when(t > 0)
    def _():
        _granule_copies(jnp.maximum(t - 1, 0), gdst_ref, totg_ref, copy_from(1 - slot), start=False)

    @pl.when(t == pl.num_programs(0) - 1)
    def _():
        _granule_copies(t, gdst_ref, totg_ref, copy_from(slot), start=False)
        tb = zbuf.shape[0]
        n_used = misc_ref[0]
        zbuf[...] = jnp.zeros_like(zbuf)

        def zero_gran(row):
            return pltpu.make_async_copy(zbuf.at[pl.ds(0, MOE_GRAN)], xs_hbm.at[pl.ds(row, MOE_GRAN)], zsem)

        def zero_block(blk):
            return pltpu.make_async_copy(zbuf, xs_hbm.at[pl.ds(pl.multiple_of(blk * tb, tb), tb)], zsem)

        def tails(e, c):
            def one(g, c2):
                zero_gran(pl.multiple_of(tstart_ref[e] + g * MOE_GRAN, MOE_GRAN)).start()
                return c2
            lax.fori_loop(0, tgran_ref[e], one, 0)
            return c

        def start_block(blk, c):
            zero_block(blk).start()
            return c

        def wait_gran(g, c):
            zero_gran(0).wait()
            return c

        def wait_block(blk, c):
            zero_block(0).wait()
            return c

        lax.fori_loop(0, N_EXPERTS, tails, 0)
        lax.fori_loop(n_used, n_blocks, start_block, 0)
        lax.fori_loop(0, misc_ref[1], wait_gran, 0)
        lax.fori_loop(n_used, n_blocks, wait_block, 0)


def _dispatch(h2, srow_t, plan, tm):
    n, d = h2.shape
    nt = n // tm
    rows = _stage_rows(tm)
    misc = jnp.concatenate([plan["n_used"], plan["tot_tail"]])
    grid_spec = pltpu.PrefetchScalarGridSpec(
        num_scalar_prefetch=5,
        grid=(nt,),
        in_specs=[
            pl.BlockSpec((tm, d), lambda t, *_: (t, 0)),
            pl.BlockSpec((8, tm), lambda t, *_: (0, t)),
        ],
        out_specs=pl.BlockSpec(memory_space=pl.ANY),
        scratch_shapes=[pltpu.VMEM((2, rows, d), BF16), pltpu.VMEM((MOE_BLOCK, d), BF16),
                        pltpu.SemaphoreType.DMA((2,)), pltpu.SemaphoreType.DMA(())],
    )
    return pl.pallas_call(
        functools.partial(_dispatch_kernel, n_blocks=plan["nb"]),
        grid_spec=grid_spec,
        out_shape=jax.ShapeDtypeStruct((plan["nb"] * MOE_BLOCK, d), BF16),
        compiler_params=_cparams(("arbitrary",)),
    )(plan["gran_dst"], plan["tot_gran"], plan["tail_start"], plan["tail_gran"], misc, h2, srow_t)


def _expert_kernel(be_ref, nused_ref, nexte_ref, x_ref, w1_hbm, b1_ref, w2_hbm, b2_ref, o_ref,
                   w1f, w2f, w1b, w2b, sem):
    de = w2f.shape[0]
    j = pl.program_id(0)
    live = j < nused_ref[0]
    e = be_ref[j]
    new_expert = jnp.logical_or(j == 0, e != be_ref[jnp.maximum(j - 1, 0)])

    def fetch(ex):
        return (pltpu.make_async_copy(w1_hbm.at[ex], w1f, sem.at[0]),
                pltpu.make_async_copy(w2_hbm.at[ex], w2f, sem.at[1]))

    @pl.when(jnp.logical_and(live, j == 0))
    def _():
        for c in fetch(e):
            c.start()

    @pl.when(jnp.logical_and(live, new_expert))
    def _():
        for c in fetch(e):
            c.wait()
        w1b[...] = w1f[...].astype(BF16)
        w2b[...] = w2f[...].astype(BF16)
        nxt = nexte_ref[e]

        @pl.when(nxt != e)
        def _():
            for c in fetch(nxt):
                c.start()

    @pl.when(live)
    def _():
        hm = jnp.dot(x_ref[...], w1b[...], preferred_element_type=F32) + b1_ref[0]
        gate = jnp.minimum(hm[:, :de], SWIGLU_LIMIT)
        up = jnp.clip(hm[:, de:], -SWIGLU_LIMIT, SWIGLU_LIMIT)
        act = gate * _sigmoid(SWIGLU_ALPHA * gate) * (up + 1.0)
        y = jnp.dot(act.astype(BF16), w2b[...], preferred_element_type=F32) + b2_ref[0]
        o_ref[...] = y.astype(o_ref.dtype)

    @pl.when(jnp.logical_not(live))
    def _():
        o_ref[...] = jnp.zeros_like(o_ref)


def _experts(xs, plan, w1, b1, w2, b2):
    d = xs.shape[1]
    tb = MOE_BLOCK
    f2 = w1.shape[2]
    de = w2.shape[1]
    last = lambda j, nu: jnp.maximum(jnp.minimum(j, nu[0] - 1), 0)
    grid_spec = pltpu.PrefetchScalarGridSpec(
        num_scalar_prefetch=3,
        grid=(plan["nb"],),
        in_specs=[
            pl.BlockSpec((tb, d), lambda j, be, nu, ne: (last(j, nu), 0)),
            pl.BlockSpec(memory_space=pl.ANY),
            pl.BlockSpec((1, 1, f2), lambda j, be, nu, ne: (be[j], 0, 0)),
            pl.BlockSpec(memory_space=pl.ANY),
            pl.BlockSpec((1, 1, d), lambda j, be, nu, ne: (be[j], 0, 0)),
        ],
        out_specs=pl.BlockSpec((tb, d), lambda j, be, nu, ne: (j, 0)),
        scratch_shapes=[pltpu.VMEM((d, f2), F32), pltpu.VMEM((de, d), F32),
                        pltpu.VMEM((d, f2), BF16), pltpu.VMEM((de, d), BF16),
                        pltpu.SemaphoreType.DMA((2,))],
    )
    return pl.pallas_call(
        _expert_kernel,
        grid_spec=grid_spec,
        out_shape=jax.ShapeDtypeStruct(xs.shape, BF16),
        compiler_params=_cparams(("arbitrary",)),
    )(plan["block_e"], plan["n_used"], plan["next_expert"], xs, w1, b1, w2, b2)


COMBINE_TILES = 2


def _combine_kernel(gdst_ref, totg_ref, x1_ref, srow_ref, wt_ref, g_ref, yb_hbm, o_ref, stage, sem):
    s = pl.program_id(0)
    slot = s % 2
    nu = stage.shape[1]
    tm = x1_ref.shape[0] // nu
    rows = stage.shape[2]

    def copy_into(sl_, u):
        def copy(stage_row, sorted_row, nrows):
            return pltpu.make_async_copy(yb_hbm.at[pl.ds(sorted_row, nrows)],
                                         stage.at[sl_, u, pl.ds(stage_row, nrows)], sem.at[sl_, u])
        return copy

    def fetch(step, sl_, start):
        for u in range(nu):
            _granule_copies(step * nu + u, gdst_ref, totg_ref, copy_into(sl_, u), start=start)

    @pl.when(s == 0)
    def _():
        stage[...] = jnp.zeros_like(stage)
        fetch(s, slot, True)

    @pl.when(s + 1 < pl.num_programs(0))
    def _():
        fetch(s + 1, 1 - slot, True)

    fetch(s, slot, False)

    chunk = LANES
    row_id = lax.broadcasted_iota(jnp.int32, (chunk, rows), 1).astype(F32)
    for u in range(nu):
        for c in range(tm // chunk):
            sl = slice(u * tm + c * chunk, u * tm + (c + 1) * chunk)
            unsort = jnp.zeros((chunk, rows), F32)
            for k in range(TOP_K):
                unsort = jnp.where(row_id == srow_ref[sl, k:k + 1], wt_ref[sl, k:k + 1], unsort)
            y = jnp.dot(unsort.astype(BF16), stage[slot, u], preferred_element_type=F32)
            o_ref[sl, :] = _rms(x1_ref[sl, :] + y, g_ref[...])


def _combine(x1, yb, srow, top_w, plan, g_final, tm):
    n, d = x1.shape
    nu = COMBINE_TILES
    nt = n // (nu * tm)
    rows = _stage_rows(tm)
    grid_spec = pltpu.PrefetchScalarGridSpec(
        num_scalar_prefetch=2,
        grid=(nt,),
        in_specs=[
            pl.BlockSpec((nu * tm, d), lambda t, *_: (t, 0)),
            pl.BlockSpec((nu * tm, LANES), lambda t, *_: (t, 0)),
            pl.BlockSpec((nu * tm, LANES), lambda t, *_: (t, 0)),
            pl.BlockSpec((1, d), lambda t, *_: (0, 0)),
            pl.BlockSpec(memory_space=pl.ANY),
        ],
        out_specs=pl.BlockSpec((nu * tm, d), lambda t, *_: (t, 0)),
        scratch_shapes=[pltpu.VMEM((2, nu, rows, d), BF16), pltpu.SemaphoreType.DMA((2, nu))],
    )
    return pl.pallas_call(
        _combine_kernel,
        grid_spec=grid_spec,
        out_shape=jax.ShapeDtypeStruct((n, d), F32),
        compiler_params=_cparams(("arbitrary",)),
    )(plan["gran_dst"], plan["tot_gran"], x1, srow, top_w, g_final, yb)


def _rope_tables(seq):
    half = ATTN_HEAD_DIM // 2
    inv_freq = np.float32(ROPE_THETA) ** (-np.arange(0, half, 2, dtype=np.float32) / np.float32(half))
    pos = np.arange(seq)
    ang_r = (pos // GRID_W).astype(np.float32)[:, None] * inv_freq
    ang_c = (pos % GRID_W).astype(np.float32)[:, None] * inv_freq
    cos_t = np.concatenate([np.cos(ang_r), np.cos(ang_c)] * 2, axis=-1)
    sin_t = np.concatenate([-np.sin(ang_r), -np.sin(ang_c), np.sin(ang_r), np.sin(ang_c)], axis=-1)
    return jnp.asarray(cos_t, F32), jnp.asarray(sin_t, F32)


def _token_mixer(x2, batch, seq, g_mix, w_in, conv_w, conv_b, dt_bias_f, dt_bias_b, a_log_f, a_log_b, d_skip,
                 g_ssm, q_norm_g, k_norm_g, w_br_ssm, w_br_attn, w_out):
    n, d = x2.shape
    z_end = SSM_INNER
    xbc_end = z_end + CONV_CH
    dtf_end = xbc_end + SSM_HEADS
    dtb_end = dtf_end + SSM_HEADS
    q_end = dtb_end + ATTN_HEADS * ATTN_HEAD_DIM
    k_end = q_end + ATTN_KV_HEADS * ATTN_HEAD_DIM
    v_end = k_end + ATTN_KV_HEADS * ATTN_HEAD_DIM
    head_cols = lambda w: _rope_head_order(w.reshape(d, -1, ATTN_HEAD_DIM)).reshape(d, -1)
    w_main = jnp.concatenate([w_in[:, :z_end], head_cols(w_in[:, dtb_end:q_end]), w_in[:, v_end:],
                              w_in[:, z_end:xbc_end], head_cols(w_in[:, q_end:k_end]), w_in[:, k_end:v_end]],
                             axis=1).astype(BF16)
    w_dt = jnp.pad(w_in[:, xbc_end:dtb_end], ((0, 0), (0, LANES - 2 * SSM_HEADS))).astype(BF16)

    proj, dt = _in_proj(x2, g_mix.reshape(1, d), w_main, w_dt)
    proj3 = proj.reshape(batch, seq, PROJ_COLS)

    xbc = _conv(proj3, conv_w, conv_b.reshape(1, CONV_CH))

    dt3 = dt.reshape(batch, seq, LANES)
    dtt3 = jnp.swapaxes(dt3[:, :, :2 * SSM_HEADS], 1, 2)
    bias = jnp.concatenate([dt_bias_f, dt_bias_b])
    alog = jnp.concatenate([a_log_f, a_log_b])
    pad_row = lambda v: jnp.pad(v, (0, LANES - 2 * SSM_HEADS)).reshape(1, LANES)
    y_f, y_b = _ssd(xbc, dt3, dtt3, pad_row(bias), bias.reshape(-1, 1), pad_row(alog), alog.reshape(-1, 1))

    cos_t, sin_t = _rope_tables(seq)
    q_rot, k_rot = _qk_prep(proj, cos_t, sin_t, _rope_head_order(q_norm_g).reshape(1, -1),
                            _rope_head_order(k_norm_g).reshape(1, -1), seq)
    vt3 = jnp.swapaxes(proj3[:, :, V_OFF:V_OFF + ATTN_KV_HEADS * ATTN_HEAD_DIM], 1, 2)
    attn = _flash(q_rot.reshape(batch, seq, -1), k_rot.reshape(batch, seq, -1), vt3, q_norm_g, k_norm_g)

    return _merge(y_f.reshape(n, -1), y_b.reshape(n, -1), xbc.reshape(n, CONV_CH), proj, attn.reshape(n, -1),
                  x2, jnp.repeat(d_skip, SSM_HEAD_DIM).reshape(1, -1), g_ssm.reshape(1, -1),
                  w_br_ssm.astype(BF16), w_br_attn.astype(BF16), w_out.astype(BF16))


def _moe_and_final_norm(x1, g_ffn, w_router, b_router, w_mlp1, b_mlp1, w_mlp2, b_mlp2, g_final):
    n, d = x1.shape
    tm = min(MOE_TILE, n)
    w_t = w_router.T
    w_hi = w_t.astype(BF16)
    w_lo = (w_t - w_hi.astype(F32)).astype(BF16)
    g_ffn_row = g_ffn.reshape(1, d)
    h2, srow_t, srow, top_w, cnt = _router(x1, g_ffn_row, jnp.concatenate([w_hi, w_lo, w_hi], axis=1),
                                           b_router.reshape(N_EXPERTS, 1))
    plan = _routing_plan(cnt[:, :, 0].astype(jnp.int32), n, tm, MOE_BLOCK)
    xs = _dispatch(h2, srow_t, plan, tm)
    yb = _experts(xs, plan, w_mlp1, b_mlp1[:, None, :], w_mlp2, b_mlp2[:, None, :])
    return _combine(x1, yb, srow, top_w, plan, g_final.reshape(1, d), tm)


def kernel(x, g_mix, w_in, conv_w, conv_b, dt_bias_f, dt_bias_b, a_log_f, a_log_b, d_skip, g_ssm, q_norm_g,
           k_norm_g, w_br_ssm, w_br_attn, w_out, g_ffn, w_router, b_router, w_mlp1, b_mlp1, w_mlp2, b_mlp2,
           g_final):
    batch, seq, d = x.shape
    assert g_mix.shape[0] == 1, "single-layer model: the final rmsnorm is fused into the MoE combine"
    x2 = x.reshape(batch * seq, d)
    x1 = _token_mixer(x2, batch, seq, g_mix[0], w_in[0], conv_w[0], conv_b[0], dt_bias_f[0], dt_bias_b[0],
                      a_log_f[0], a_log_b[0], d_skip[0], g_ssm[0], q_norm_g[0], k_norm_g[0], w_br_ssm[0],
                      w_br_attn[0], w_out[0])
    out = _moe_and_final_norm(x1, g_ffn[0], w_router[0], b_router[0], w_mlp1[0], b_mlp1[0], w_mlp2[0],
                              b_mlp2[0], g_final)
    return out.reshape(batch, seq, d)
```

```python
import functools
import math

import jax
import jax.numpy as jnp
import numpy as np
from jax import lax
from jax.experimental import pallas as pl
from jax.experimental.pallas import tpu as pltpu

F32 = jnp.float32
BF16 = jnp.bfloat16

NORM_EPS = 1e-6
GRID_W = 64
SSM_HEADS = 16
SSM_HEAD_DIM = 64
SSM_INNER = SSM_HEADS * SSM_HEAD_DIM
SSM_GROUPS = 2
SSM_STATE = 128
SSM_CONV = 5
CONV_CH = SSM_INNER + 2 * SSM_GROUPS * SSM_STATE
ATTN_HEADS = 8
ATTN_KV_HEADS = 2
ATTN_HEAD_DIM = 128
ROPE_THETA = 10000.0
N_EXPERTS = 32
TOP_K = 4
SWIGLU_LIMIT = 7.0
SWIGLU_ALPHA = 1.702

LANES = 128
BF16_SUBLANES = 16
VMEM_LIMIT = 56 * 1024 * 1024

Z_OFF, Q_OFF, GATE_OFF, XBC_OFF = 0, 1024, 2048, 4096
K_OFF, V_OFF, PROJ_COLS = 5632, 5888, 6144

MOE_TILE = 512
MOE_BLOCK = 512
MOE_GRAN = BF16_SUBLANES


def _cparams(sem):
    return pltpu.CompilerParams(dimension_semantics=sem, vmem_limit_bytes=VMEM_LIMIT)


def _sigmoid(x):
    return 1.0 / (1.0 + jnp.exp(-x))


def _softplus(x):
    return jnp.maximum(x, 0.0) + jnp.log(1.0 + jnp.exp(-jnp.abs(x)))


def _rms(x, g):
    ms = jnp.mean(x * x, axis=-1, keepdims=True)
    return x * lax.rsqrt(ms + NORM_EPS) * g


def _inproj_kernel(x_ref, g_ref, w_ref, wdt_ref, o_ref, dt_ref, dtt_ref, h_scr):
    @pl.when(pl.program_id(1) == 0)
    def _():
        hb = _rms(x_ref[...], g_ref[...]).astype(BF16)
        h_scr[...] = hb
        dt = jnp.dot(hb, wdt_ref[...], preferred_element_type=F32)
        dt_ref[...] = dt
        dtt_ref[...] = dt.T

    o_ref[...] = jnp.dot(h_scr[...], w_ref[...], preferred_element_type=F32).astype(o_ref.dtype)


def _in_proj(x2, g_mix, w_main, w_dt):
    n, d = x2.shape
    tm = min(1024, n)
    tn = 2048
    return pl.pallas_call(
        _inproj_kernel,
        grid=(n // tm, PROJ_COLS // tn),
        in_specs=[
            pl.BlockSpec((tm, d), lambda i, j: (i, 0)),
            pl.BlockSpec((1, d), lambda i, j: (0, 0)),
            pl.BlockSpec((d, tn), lambda i, j: (0, j)),
            pl.BlockSpec((d, LANES), lambda i, j: (0, 0)),
        ],
        out_specs=[
            pl.BlockSpec((tm, tn), lambda i, j: (i, j)),
            pl.BlockSpec((tm, LANES), lambda i, j: (i, 0)),
            pl.BlockSpec((LANES, tm), lambda i, j: (0, i)),
        ],
        out_shape=[
            jax.ShapeDtypeStruct((n, PROJ_COLS), BF16),
            jax.ShapeDtypeStruct((n, LANES), F32),
            jax.ShapeDtypeStruct((LANES, n), F32),
        ],
        scratch_shapes=[pltpu.VMEM((tm, d), BF16)],
        compiler_params=_cparams(("arbitrary", "arbitrary")),
    )(x2, g_mix, w_main, w_dt)


CONV_HALO = 64
CONV_ROWS = 128


def _conv_kernel(prev_ref, cur_ref, next_ref, shift_ref, w_ref, b_ref, o_ref):
    s = pl.program_id(1)
    ts = cur_ref.shape[1]
    prev = prev_ref[0]
    nxt = next_ref[0]
    zero = jnp.zeros_like(prev)
    ext = jnp.concatenate([jnp.where(s == 0, zero, prev), cur_ref[0],
                           jnp.where(s == pl.num_programs(1) - 1, zero, nxt)], axis=0)
    shift = shift_ref[...]
    for rb in range(ts // CONV_ROWS):
        lo = rb * CONV_ROWS
        taps = jnp.dot(shift, ext[lo:lo + CONV_ROWS + 2 * CONV_HALO], preferred_element_type=F32)
        acc = b_ref[...] + w_ref[0:1, :] * taps[0:CONV_ROWS]
        for k in range(1, SSM_CONV):
            acc = acc + w_ref[k:k + 1, :] * taps[k * CONV_ROWS:(k + 1) * CONV_ROWS]
        o_ref[0, lo:lo + CONV_ROWS, :] = (acc * _sigmoid(acc)).astype(o_ref.dtype)


def _conv(proj3, conv_w, conv_b):
    b, s, _ = proj3.shape
    ts = min(2048, s)
    tc = 512
    halo = CONV_HALO
    hb = ts // halo
    col0 = XBC_OFF // tc
    pad = (SSM_CONV - 1) // 2
    out_row = np.arange(SSM_CONV * CONV_ROWS)
    src = out_row % CONV_ROWS + halo + out_row // CONV_ROWS - pad
    shift = jnp.asarray(src[:, None] == np.arange(CONV_ROWS + 2 * halo)[None, :], BF16)
    return pl.pallas_call(
        _conv_kernel,
        grid=(b, s // ts, CONV_CH // tc),
        in_specs=[
            pl.BlockSpec((1, halo, tc), lambda bi, si, ci: (bi, jnp.maximum(si * hb - 1, 0), col0 + ci)),
            pl.BlockSpec((1, ts, tc), lambda bi, si, ci: (bi, si, col0 + ci)),
            pl.BlockSpec((1, halo, tc),
                         lambda bi, si, ci: (bi, jnp.minimum((si + 1) * hb, s // halo - 1), col0 + ci)),
            pl.BlockSpec(shift.shape, lambda bi, si, ci: (0, 0)),
            pl.BlockSpec((SSM_CONV, tc), lambda bi, si, ci: (0, ci)),
            pl.BlockSpec((1, tc), lambda bi, si, ci: (0, ci)),
        ],
        out_specs=pl.BlockSpec((1, ts, tc), lambda bi, si, ci: (bi, si, ci)),
        out_shape=jax.ShapeDtypeStruct((b, s, CONV_CH), BF16),
        compiler_params=_cparams(("arbitrary", "arbitrary", "arbitrary")),
    )(proj3, proj3, proj3, shift, conv_w, conv_b)


SSM_CHUNKS_PER_STEP = 4


def _ssd_kernel(xf_ref, xb_ref, dtf_ref, dtb_ref, dttf_ref, dttb_ref, brow_ref, bcol_ref,
                arow_ref, acol_ref, yf_ref, yb_ref, st_ref):
    L = SSM_STATE
    nsub = xf_ref.shape[1] // L
    hg = SSM_HEADS // SSM_GROUPS
    pairs = hg // 2

    @pl.when(pl.program_id(1) == 0)
    def _():
        st_ref[...] = jnp.zeros_like(st_ref)

    rows = lax.broadcasted_iota(jnp.int32, (L, L), 0)
    cols = lax.broadcasted_iota(jnp.int32, (L, L), 1)
    lower = rows >= cols
    upper = rows <= cols
    ltri = lower.astype(BF16)
    utri = upper.astype(BF16)
    lane = lax.broadcasted_iota(jnp.int32, (L, LANES), 1)
    first_half = lane < SSM_HEAD_DIM
    lane1 = lax.broadcasted_iota(jnp.int32, (1, LANES), 1)
    log2e = math.log2(math.e)
    a_row = -jnp.exp(arow_ref[...]) * log2e
    a_col = -jnp.exp(acol_ref[...]) * log2e

    def split3(v):
        hi = v.astype(BF16)
        r1 = v - hi.astype(F32)
        mid = r1.astype(BF16)
        return hi, mid, (r1 - mid.astype(F32)).astype(BF16)

    def cumsum_cols(tri, v):
        return jnp.dot(jnp.concatenate([tri] * 3, axis=1), jnp.concatenate(split3(v), axis=0),
                       preferred_element_type=F32)

    def cumsum_rows(v, tri):
        return jnp.dot(jnp.concatenate(split3(v), axis=1), jnp.concatenate([tri] * 3, axis=0),
                       preferred_element_type=F32)

    for sub, d in [(sub, d) for sub in range(nsub) for d in range(2)]:
        x_ref, dt_ref, dtt_ref, y_ref = ((xf_ref, dtf_ref, dttf_ref, yf_ref) if d == 0
                                         else (xb_ref, dtb_ref, dttb_ref, yb_ref))
        r0 = (sub if d == 0 else nsub - 1 - sub) * L
        rs = slice(r0, r0 + L)
        a = _softplus(dt_ref[0, rs, :] + brow_ref[...]) * a_row
        dt_t = _softplus(dtt_ref[:, rs] + bcol_ref[...])
        a_t = dt_t * a_col
        if d == 0:
            cs_col = cumsum_cols(ltri, a)
            cs_row = cumsum_rows(a_t, utri)
            tot = cs_col[L - 1:L, :]
            tot_t = cs_row[:, L - 1:L]
            mask = lower
        else:
            cs_col = cumsum_cols(utri, a)
            cs_row = cumsum_rows(a_t, ltri)
            tot = cs_col[0:1, :]
            tot_t = cs_row[:, 0:1]
            mask = upper
        w_t = dt_t * jnp.exp2(tot_t - cs_row)
        src_t = cs_row - jnp.log2(dt_t)
        chunk_decay = jnp.exp2(tot)

        for g in range(SSM_GROUPS):
            boff = SSM_INNER + g * SSM_STATE
            coff = SSM_INNER + SSM_GROUPS * SSM_STATE + g * SSM_STATE
            bm = x_ref[0, rs, boff:boff + SSM_STATE]
            cm = x_ref[0, rs, coff:coff + SSM_STATE]
            cb = lax.dot_general(cm, bm, (((1,), (1,)), ((), ())), preferred_element_type=F32)
            bt = bm.astype(F32).T
            st = st_ref[d, g]
            y_off = jnp.dot(cm, st.astype(BF16), preferred_element_type=F32)
            for pr in range(pairs):
                h0 = d * SSM_HEADS + g * hg + 2 * pr
                xoff = (g * pairs + pr) * LANES
                xs = x_ref[0, rs, xoff:xoff + LANES]
                zero = jnp.zeros_like(xs)
                rhs = jnp.concatenate([jnp.where(first_half, xs, zero),
                                       jnp.where(first_half, zero, xs)], axis=0)
                ms, ws, dins = [], [], []
                for hh in (h0, h0 + 1):
                    cs_b = jnp.broadcast_to(cs_col[:, hh:hh + 1], (L, L))
                    seg = cs_b - src_t[hh:hh + 1, :]
                    m = cb * jnp.exp2(jnp.where(mask, seg, -jnp.inf))
                    ms.append(m.astype(BF16))
                    ws.append((bt * w_t[hh:hh + 1, :]).astype(BF16))
                    dins.append(jnp.exp2(cs_b))
                y = jnp.dot(jnp.concatenate(ms, axis=1), rhs, preferred_element_type=F32)
                y = y + y_off[:, pr * LANES:(pr + 1) * LANES] * jnp.where(first_half, dins[0], dins[1])
                y_ref[0, rs, xoff:xoff + LANES] = y.astype(y_ref.dtype)
                new_st = jnp.dot(jnp.concatenate(ws, axis=1), rhs, preferred_element_type=F32)
                cd = jnp.where(lane1 < SSM_HEAD_DIM, chunk_decay[:, h0:h0 + 1], chunk_decay[:, h0 + 1:h0 + 2])
                st_ref[d, g, :, pr * LANES:(pr + 1) * LANES] = st[:, pr * LANES:(pr + 1) * LANES] * cd + new_st


def _ssd(xbc, dt3, dtt3, bias_row, bias_col, alog_row, alog_col):
    b, s, _ = xbc.shape
    L = SSM_CHUNKS_PER_STEP * SSM_STATE
    nc = s // L
    hg = SSM_HEADS // SSM_GROUPS
    fwd = lambda bi, ci: (bi, ci, 0)
    bwd = lambda bi, ci: (bi, nc - 1 - ci, 0)
    fwd_t = lambda bi, ci: (0, bi * nc + ci)
    bwd_t = lambda bi, ci: (0, bi * nc + nc - 1 - ci)
    const = lambda bi, ci: (0, 0)
    return pl.pallas_call(
        _ssd_kernel,
        grid=(b, nc),
        in_specs=[
            pl.BlockSpec((1, L, CONV_CH), fwd),
            pl.BlockSpec((1, L, CONV_CH), bwd),
            pl.BlockSpec((1, L, LANES), fwd),
            pl.BlockSpec((1, L, LANES), bwd),
            pl.BlockSpec((2 * SSM_HEADS, L), fwd_t),
            pl.BlockSpec((2 * SSM_HEADS, L), bwd_t),
            pl.BlockSpec((1, LANES), const),
            pl.BlockSpec((2 * SSM_HEADS, 1), const),
            pl.BlockSpec((1, LANES), const),
            pl.BlockSpec((2 * SSM_HEADS, 1), const),
        ],
        out_specs=[
            pl.BlockSpec((1, L, SSM_INNER), fwd),
            pl.BlockSpec((1, L, SSM_INNER), bwd),
        ],
        out_shape=[jax.ShapeDtypeStruct((b, s, SSM_INNER), BF16)] * 2,
        scratch_shapes=[pltpu.VMEM((2, SSM_GROUPS, SSM_STATE, hg * SSM_HEAD_DIM), F32)],
        compiler_params=_cparams(("arbitrary", "arbitrary")),
    )(xbc, xbc, dt3, dt3, dtt3, dtt3, bias_row, bias_col, alog_row, alog_col)


def _rope_head_order(a):
    q4 = ATTN_HEAD_DIM // 4
    return jnp.concatenate([a[..., 0:q4], a[..., 2 * q4:3 * q4], a[..., q4:2 * q4], a[..., 3 * q4:]], axis=-1)


def _rope_norm(t, g, cos, sin_signed, ones):
    sq = t * t
    hi = sq.astype(BF16)
    lo = (sq - hi.astype(F32)).astype(BF16)
    ms = jnp.dot(jnp.concatenate([hi, lo], axis=1), ones, preferred_element_type=F32) * (1.0 / ATTN_HEAD_DIM)
    tn = t * lax.rsqrt(ms + NORM_EPS) * g
    return tn * cos + pltpu.roll(tn, ATTN_HEAD_DIM // 2, 1) * sin_signed


Q_SCALE = ATTN_HEAD_DIM ** -0.5 * math.log2(math.e)


def _qkprep_kernel(q_ref, k_ref, cos_ref, sin_ref, qg_ref, kg_ref, qo_ref, ko_ref):
    cos = cos_ref[...]
    sin = sin_ref[...]
    ones = jnp.ones((2 * ATTN_HEAD_DIM, ATTN_HEAD_DIM), BF16)

    def heads(src_ref, g_ref, dst_ref, n_heads, scale):
        for h in range(n_heads):
            sl = slice(h * ATTN_HEAD_DIM, (h + 1) * ATTN_HEAD_DIM)
            r = _rope_norm(src_ref[:, sl].astype(F32), g_ref[...], cos, sin, ones) * scale
            dst_ref[:, sl] = r.astype(dst_ref.dtype)

    heads(q_ref, qg_ref, qo_ref, ATTN_HEADS, Q_SCALE)
    heads(k_ref, kg_ref, ko_ref, ATTN_KV_HEADS, 1.0)


def _qk_prep(proj, cos_t, sin_t, q_norm_g, k_norm_g, seq):
    n = proj.shape[0]
    tm = min(1024, seq)
    qw = ATTN_HEADS * ATTN_HEAD_DIM
    kw = ATTN_KV_HEADS * ATTN_HEAD_DIM
    spt = seq // tm
    return pl.pallas_call(
        _qkprep_kernel,
        grid=(n // tm,),
        in_specs=[
            pl.BlockSpec((tm, qw), lambda i: (i, Q_OFF // qw)),
            pl.BlockSpec((tm, kw), lambda i: (i, K_OFF // kw)),
            pl.BlockSpec((tm, ATTN_HEAD_DIM), lambda i: (i % spt, 0)),
            pl.BlockSpec((tm, ATTN_HEAD_DIM), lambda i: (i % spt, 0)),
            pl.BlockSpec((1, ATTN_HEAD_DIM), lambda i: (0, 0)),
            pl.BlockSpec((1, ATTN_HEAD_DIM), lambda i: (0, 0)),
        ],
        out_specs=[
            pl.BlockSpec((tm, qw), lambda i: (i, 0)),
            pl.BlockSpec((tm, kw), lambda i: (i, 0)),
        ],
        out_shape=[jax.ShapeDtypeStruct((n, qw), BF16), jax.ShapeDtypeStruct((n, kw), BF16)],
        compiler_params=_cparams(("arbitrary",)),
    )(proj, proj, cos_t, sin_t, q_norm_g, k_norm_g)


def _flash_kernel(small_ref, q_ref, k_ref, vt_ref, o_ref, *, tk, th):
    q = q_ref[0]
    tq = q.shape[0]
    seq = k_ref.shape[1]
    nt = (((1,), (1,)), ((), ()))
    small = small_ref[0] != 0

    @pl.when(small)
    def _():
        l = acc_t = None
        for c in range(seq // th):
            ks = slice(c * th, (c + 1) * th)
            p_t = jnp.exp2(lax.dot_general(k_ref[0, ks, :], q, nt, preferred_element_type=F32))
            l_c = jnp.sum(p_t, axis=0, keepdims=True)
            a_c = jnp.dot(vt_ref[0, :, ks], p_t.astype(BF16), preferred_element_type=F32)
            l = l_c if l is None else l + l_c
            acc_t = a_c if acc_t is None else acc_t + a_c
        o_ref[0] = (acc_t / l).T.astype(o_ref.dtype)

    @pl.when(jnp.logical_not(small))
    def _():
        def body(i, carry):
            m, l, acc_t = carry
            off = pl.multiple_of(i * tk, tk)
            s_t = lax.dot_general(k_ref[0, pl.ds(off, tk), :], q, nt, preferred_element_type=F32)
            m_new = jnp.maximum(m, jnp.max(s_t, axis=0, keepdims=True))
            alpha = jnp.exp2(m - m_new)
            p_t = jnp.exp2(s_t - m_new)
            l = alpha * l + jnp.sum(p_t, axis=0, keepdims=True)
            acc_t = alpha * acc_t + jnp.dot(vt_ref[0, :, pl.ds(off, tk)], p_t.astype(BF16),
                                            preferred_element_type=F32)
            return m_new, l, acc_t

        init = (jnp.full((1, tq), -jnp.inf, F32), jnp.zeros((1, tq), F32),
                jnp.zeros((ATTN_HEAD_DIM, tq), F32))
        _, l, acc_t = lax.fori_loop(0, seq // tk, body, init)
        o_ref[0] = (acc_t / l).T.astype(o_ref.dtype)


SCORE_BOUND = 59.0


def _flash(q3, k3, vt3, q_norm_g, k_norm_g):
    b, s, _ = q3.shape
    tq = min(1024, s)
    tk = min(2048, s)
    th = min(4096, s)
    nq = s // tq
    grp = ATTN_HEADS // ATTN_KV_HEADS
    hd = ATTN_HEAD_DIM
    bound = hd * Q_SCALE * jnp.max(jnp.abs(q_norm_g)) * jnp.max(jnp.abs(k_norm_g)) * 1.02
    small = (bound <= SCORE_BOUND).astype(jnp.int32).reshape(1)
    grid_spec = pltpu.PrefetchScalarGridSpec(
        num_scalar_prefetch=1,
        grid=(b, ATTN_HEADS, nq),
        in_specs=[
            pl.BlockSpec((1, tq, hd), lambda bi, h, qi, sm: (bi, qi, h)),
            pl.BlockSpec((1, s, hd), lambda bi, h, qi, sm: (bi, 0, h // grp)),
            pl.BlockSpec((1, hd, s), lambda bi, h, qi, sm: (bi, h // grp, 0)),
        ],
        out_specs=pl.BlockSpec((1, tq, hd), lambda bi, h, qi, sm: (bi, qi, h)),
    )
    return pl.pallas_call(
        functools.partial(_flash_kernel, tk=tk, th=th),
        grid_spec=grid_spec,
        out_shape=jax.ShapeDtypeStruct((b, s, ATTN_HEADS * hd), BF16),
        compiler_params=_cparams(("arbitrary", "arbitrary", "arbitrary")),
    )(small, q3, k3, vt3)


def _merge_kernel(yf_ref, yb_ref, xs_ref, z_ref, gate_ref, attn_ref, x_ref, dskip_ref, gssm_ref,
                  wbs_ref, wba_ref, wo_ref, x1_ref):
    d = x_ref.shape[1]
    xs = xs_ref[...].astype(F32)
    y = yf_ref[...].astype(F32) + yb_ref[...].astype(F32) + xs * dskip_ref[...]
    z = z_ref[...].astype(F32)
    y = _rms(y * (z * _sigmoid(z)), gssm_ref[...])
    br_ssm = jnp.dot(y.astype(BF16), wbs_ref[...], preferred_element_type=F32)
    br_attn = jnp.dot(attn_ref[...], wba_ref[...], preferred_element_type=F32)
    g_s = _sigmoid(gate_ref[:, :d].astype(F32))
    g_a = _sigmoid(gate_ref[:, d:].astype(F32))
    merged = (g_s * br_ssm + g_a * br_attn).astype(BF16)
    x1_ref[...] = x_ref[...] + jnp.dot(merged, wo_ref[...], preferred_element_type=F32)


def _merge(y_f, y_b, xbc, proj, attn, x2, dskip_row, g_ssm, w_br_ssm, w_br_attn, w_out):
    n, d = x2.shape
    tm = min(512, n)
    row = lambda i: (i, 0)
    const = lambda i: (0, 0)
    return pl.pallas_call(
        _merge_kernel,
        grid=(n // tm,),
        in_specs=[
            pl.BlockSpec((tm, d), row),
            pl.BlockSpec((tm, d), row),
            pl.BlockSpec((tm, d), row),
            pl.BlockSpec((tm, d), lambda i: (i, Z_OFF // d)),
            pl.BlockSpec((tm, 2 * d), lambda i: (i, GATE_OFF // (2 * d))),
            pl.BlockSpec((tm, d), row),
            pl.BlockSpec((tm, d), row),
            pl.BlockSpec((1, d), const),
            pl.BlockSpec((1, d), const),
            pl.BlockSpec((d, d), const),
            pl.BlockSpec((d, d), const),
            pl.BlockSpec((d, d), const),
        ],
        out_specs=pl.BlockSpec((tm, d), row),
        out_shape=jax.ShapeDtypeStruct((n, d), F32),
        compiler_params=_cparams(("arbitrary",)),
    )(y_f, y_b, xbc, proj, proj, attn, x2, dskip_row, g_ssm, w_br_ssm, w_br_attn, w_out)


def _router_kernel(x1_ref, g_ref, w_ref, b_ref, h_ref, srow_ref, srow_l_ref, wt_l_ref, cnt_ref):
    tm = x1_ref.shape[0]
    h = _rms(x1_ref[...], g_ref[...])
    h_hi = h.astype(BF16)
    h_lo = (h - h_hi.astype(F32)).astype(BF16)
    h_ref[...] = h_hi
    logits = lax.dot_general(w_ref[...], jnp.concatenate([h_hi, h_hi, h_lo], axis=1), (((1,), (1,)), ((), ())),
                             preferred_element_type=F32) + b_ref[...]
    expert = lax.broadcasted_iota(jnp.int32, logits.shape, 0)
    slot = lax.broadcasted_iota(jnp.int32, srow_ref.shape, 0)
    chosen = jnp.zeros(logits.shape, F32)
    vals, hits = [], []
    for k in range(TOP_K):
        m = jnp.max(logits, axis=0, keepdims=True)
        idx = jnp.min(jnp.where(logits == m, expert, N_EXPERTS), axis=0, keepdims=True)
        vals.append(m)
        hit = expert == idx
        hits.append(hit)
        chosen = jnp.where(hit, 1.0, chosen)
        logits = jnp.where(hit, -jnp.inf, logits)
    es = [jnp.exp(v - vals[0]) for v in vals]
    tot = es[0] + es[1] + es[2] + es[3]

    chosen16 = chosen.astype(BF16)
    cnt_col = jnp.sum(chosen, axis=1, keepdims=True)
    cnt_row = lax.dot_general(jnp.ones((8, tm), BF16), chosen16, (((1,), (1,)), ((), ())),
                              preferred_element_type=F32)[0:1, :]
    seg_row = jnp.ceil(cnt_row * (1.0 / MOE_GRAN)) * MOE_GRAN
    before = (lax.broadcasted_iota(jnp.int32, (N_EXPERTS, N_EXPERTS), 1)
              < lax.broadcasted_iota(jnp.int32, (N_EXPERTS, N_EXPERTS), 0))
    seg_off = jnp.sum(jnp.where(before, seg_row, 0.0), axis=1, keepdims=True)
    earlier = (lax.broadcasted_iota(jnp.int32, (tm, tm), 0)
               < lax.broadcasted_iota(jnp.int32, (tm, tm), 1)).astype(BF16)
    pos = jnp.dot(chosen16, earlier, preferred_element_type=F32) + seg_off

    val_out = jnp.zeros(srow_ref.shape, F32)
    row_out = jnp.zeros(srow_ref.shape, F32)
    for k in range(TOP_K):
        val_out = jnp.where(slot == k, es[k] / tot, val_out)
        row_out = jnp.where(slot == k, jnp.sum(jnp.where(hits[k], pos, 0.0), axis=0, keepdims=True), row_out)
    srow_ref[...] = row_out
    fill = jnp.zeros((LANES - srow_ref.shape[0], tm), F32)
    srow_l_ref[...] = jnp.concatenate([row_out, fill], axis=0).T
    wt_l_ref[...] = jnp.concatenate([val_out, fill], axis=0).T
    cnt_ref[0] = jnp.broadcast_to(cnt_col, cnt_ref.shape[1:])


def _router(x1, g_ffn, w_router_t, b_router_col):
    n, d = x1.shape
    tm = min(MOE_TILE, n)
    return pl.pallas_call(
        _router_kernel,
        grid=(n // tm,),
        in_specs=[
            pl.BlockSpec((tm, d), lambda i: (i, 0)),
            pl.BlockSpec((1, d), lambda i: (0, 0)),
            pl.BlockSpec((N_EXPERTS, 3 * d), lambda i: (0, 0)),
            pl.BlockSpec((N_EXPERTS, 1), lambda i: (0, 0)),
        ],
        out_specs=[pl.BlockSpec((tm, d), lambda i: (i, 0)),
                   pl.BlockSpec((8, tm), lambda i: (0, i)),
                   pl.BlockSpec((tm, LANES), lambda i: (i, 0)),
                   pl.BlockSpec((tm, LANES), lambda i: (i, 0)),
                   pl.BlockSpec((1, N_EXPERTS, LANES), lambda i: (i, 0, 0))],
        out_shape=[jax.ShapeDtypeStruct((n, d), BF16),
                   jax.ShapeDtypeStruct((8, n), F32),
                   jax.ShapeDtypeStruct((n, LANES), F32),
                   jax.ShapeDtypeStruct((n, LANES), F32),
                   jax.ShapeDtypeStruct((n // tm, N_EXPERTS, LANES), F32)],
        compiler_params=_cparams(("arbitrary",)),
    )(x1, g_ffn, w_router_t, b_router_col)


def _routing_plan(cnt, n, tm, tb):
    nt = n // tm
    gran = MOE_GRAN
    rc = (cnt + gran - 1) // gran * gran
    covered = jnp.sum(rc, axis=0)
    region = (covered + tb - 1) // tb * tb
    pad_end = jnp.cumsum(region)
    pad_start = pad_end - region
    seg_start = pad_start[None, :] + jnp.cumsum(rc, axis=0) - rc
    n_used = pad_end[-1] // tb
    nb = (TOP_K * n + nt * N_EXPERTS * (gran - 1) + N_EXPERTS * (tb - 1) + tb - 1) // tb
    blk = jnp.arange(nb, dtype=jnp.int32)
    block_e = jnp.sum((pad_end[None, :] <= (jnp.minimum(blk, n_used - 1) * tb)[:, None]).astype(jnp.int32), axis=1)
    block_e = jnp.minimum(block_e, N_EXPERTS - 1)
    flat = lambda a: a.reshape(-1).astype(jnp.int32)
    tail_gran = (region - covered) // gran
    ngran = rc // gran
    g_end = jnp.cumsum(ngran, axis=1)
    gi = jnp.arange(_stage_rows(tm) // gran, dtype=jnp.int32)
    g_exp = jnp.minimum(jnp.sum((g_end[:, None, :] <= gi[None, :, None]).astype(jnp.int32), axis=2), N_EXPERTS - 1)
    mine = g_exp[:, :, None] == jnp.arange(N_EXPERTS, dtype=jnp.int32)[None, None, :]
    seg_base = seg_start - gran * (g_end - ngran)
    gran_dst = jnp.sum(jnp.where(mine, seg_base[:, None, :], 0), axis=2) + gran * gi[None, :]
    ids = jnp.arange(N_EXPERTS, dtype=jnp.int32)
    later = jnp.where((ids[None, :] > ids[:, None]) & (region[None, :] > 0), ids[None, :], N_EXPERTS)
    nxt = jnp.min(later, axis=1)
    next_expert = jnp.where(nxt < N_EXPERTS, nxt, ids)
    return dict(gran_dst=flat(gran_dst), tot_gran=flat(g_end[:, -1]), block_e=flat(block_e),
                next_expert=flat(next_expert),
                n_used=flat(n_used), nb=nb,
                tail_start=flat(pad_start + covered), tail_gran=flat(tail_gran),
                tot_tail=flat(jnp.sum(tail_gran)))


def _stage_rows(tm):
    rows = TOP_K * tm + N_EXPERTS * (MOE_GRAN - 1)
    return (rows + 2 * LANES - 1) // (2 * LANES) * (2 * LANES)


GRAN_UNROLL = 4


def _granule_copies(t, gdst_ref, totg_ref, make_copy, start):
    total = totg_ref[t]
    base = t * (gdst_ref.shape[0] // totg_ref.shape[0])
    full = lax.shift_right_logical(total, GRAN_UNROLL.bit_length() - 1)

    def one(i):
        make_copy(pl.multiple_of(i * MOE_GRAN, MOE_GRAN), pl.multiple_of(gdst_ref[base + i], MOE_GRAN),
                  MOE_GRAN).start()

    def group(q, c):
        if start:
            for u in range(GRAN_UNROLL):
                one(q * GRAN_UNROLL + u)
        else:
            make_copy(0, 0, GRAN_UNROLL * MOE_GRAN).wait()
        return c

    def rest(i, c):
        if start:
            one(i)
        else:
            make_copy(0, 0, MOE_GRAN).wait()
        return c

    lax.fori_loop(0, full, group, 0)
    lax.fori_loop(full * GRAN_UNROLL, total, rest, 0)


def _dispatch_kernel(gdst_ref, totg_ref, tstart_ref, tgran_ref, misc_ref,
                     h_ref, srow_ref, xs_hbm, stage, zbuf, sem, zsem, *, n_blocks):
    t = pl.program_id(0)
    slot = t % 2
    tm = h_ref.shape[0]
    rows = stage.shape[1]
    h = h_ref[...]
    srows = [srow_ref[k:k + 1, :] for k in range(TOP_K)]
    chunk = 2 * LANES
    row_id = lax.broadcasted_iota(jnp.int32, (chunk, tm), 0).astype(F32)
    for c in range(rows // chunk):
        perm = jnp.zeros((chunk, tm), F32)
        for srow in srows:
            perm = jnp.where(row_id == srow - float(c * chunk), 1.0, perm)
        stage[slot, c * chunk:(c + 1) * chunk, :] = jnp.dot(
            perm.astype(BF16), h, preferred_element_type=F32).astype(stage.dtype)

    def copy_from(s):
        def copy(stage_row, sorted_row, nrows):
            return pltpu.make_async_copy(stage.at[s, pl.ds(stage_row, nrows)],
                                         xs_hbm.at[pl.ds(sorted_row, nrows)], sem.at[s])
        return copy

    _granule_copies(t, gdst_ref, totg_ref, copy_from(slot), start=True)

    @pl.when(t > 0)
    def _():
        _granule_copies(jnp.maximum(t - 1, 0), gdst_ref, totg_ref, copy_from(1 - slot), start=False)

    @pl.when(t == pl.num_programs(0) - 1)
    def _():
        _granule_copies(t, gdst_ref, totg_ref, copy_from(slot), start=False)
        tb = zbuf.shape[0]
        n_used = misc_ref[0]
        zbuf[...] = jnp.zeros_like(zbuf)

        def zero_gran(row):
            return pltpu.make_async_copy(zbuf.at[pl.ds(0, MOE_GRAN)], xs_hbm.at[pl.ds(row, MOE_GRAN)], zsem)

        def zero_block(blk):
            return pltpu.make_async_copy(zbuf, xs_hbm.at[pl.ds(pl.multiple_of(blk * tb, tb), tb)], zsem)

        def tails(e, c):
            def one(g, c2):
                zero_gran(pl.multiple_of(tstart_ref[e] + g * MOE_GRAN, MOE_GRAN)).start()
                return c2
            lax.fori_loop(0, tgran_ref[e], one, 0)
            return c

        def start_block(blk, c):
            zero_block(blk).start()
            return c

        def wait_gran(g, c):
            zero_gran(0).wait()
            return c

        def wait_block(blk, c):
            zero_block(0).wait()
            return c

        lax.fori_loop(0, N_EXPERTS, tails, 0)
        lax.fori_loop(n_used, n_blocks, start_block, 0)
        lax.fori_loop(0, misc_ref[1], wait_gran, 0)
        lax.fori_loop(n_used, n_blocks, wait_block, 0)


def _dispatch(h2, srow_t, plan, tm):
    n, d = h2.shape
    nt = n // tm
    rows = _stage_rows(tm)
    misc = jnp.concatenate([plan["n_used"], plan["tot_tail"]])
    grid_spec = pltpu.PrefetchScalarGridSpec(
        num_scalar_prefetch=5,
        grid=(nt,),
        in_specs=[
            pl.BlockSpec((tm, d), lambda t, *_: (t, 0)),
            pl.BlockSpec((8, tm), lambda t, *_: (0, t)),
        ],
        out_specs=pl.BlockSpec(memory_space=pl.ANY),
        scratch_shapes=[pltpu.VMEM((2, rows, d), BF16), pltpu.VMEM((MOE_BLOCK, d), BF16),
                        pltpu.SemaphoreType.DMA((2,)), pltpu.SemaphoreType.DMA(())],
    )
    return pl.pallas_call(
        functools.partial(_dispatch_kernel, n_blocks=plan["nb"]),
        grid_spec=grid_spec,
        out_shape=jax.ShapeDtypeStruct((plan["nb"] * MOE_BLOCK, d), BF16),
        compiler_params=_cparams(("arbitrary",)),
    )(plan["gran_dst"], plan["tot_gran"], plan["tail_start"], plan["tail_gran"], misc, h2, srow_t)


def _expert_kernel(be_ref, nused_ref, nexte_ref, x_ref, w1_hbm, b1_ref, w2_hbm, b2_ref, o_ref,
                   w1f, w2f, w1b, w2b, sem):
    de = w2f.shape[0]
    j = pl.program_id(0)
    live = j < nused_ref[0]
    e = be_ref[j]
    new_expert = jnp.logical_or(j == 0, e != be_ref[jnp.maximum(j - 1, 0)])

    def fetch(ex):
        return (pltpu.make_async_copy(w1_hbm.at[ex], w1f, sem.at[0]),
                pltpu.make_async_copy(w2_hbm.at[ex], w2f, sem.at[1]))

    @pl.when(jnp.logical_and(live, j == 0))
    def _():
        for c in fetch(e):
            c.start()

    @pl.when(jnp.logical_and(live, new_expert))
    def _():
        for c in fetch(e):
            c.wait()
        w1b[...] = w1f[...].astype(BF16)
        w2b[...] = w2f[...].astype(BF16)
        nxt = nexte_ref[e]

        @pl.when(nxt != e)
        def _():
            for c in fetch(nxt):
                c.start()

    @pl.when(live)
    def _():
        hm = jnp.dot(x_ref[...], w1b[...], preferred_element_type=F32) + b1_ref[0]
        gate = jnp.minimum(hm[:, :de], SWIGLU_LIMIT)
        up = jnp.clip(hm[:, de:], -SWIGLU_LIMIT, SWIGLU_LIMIT)
        act = gate * _sigmoid(SWIGLU_ALPHA * gate) * (up + 1.0)
        y = jnp.dot(act.astype(BF16), w2b[...], preferred_element_type=F32) + b2_ref[0]
        o_ref[...] = y.astype(o_ref.dtype)

    @pl.when(jnp.logical_not(live))
    def _():
        o_ref[...] = jnp.zeros_like(o_ref)


def _experts(xs, plan, w1, b1, w2, b2):
    d = xs.shape[1]
    tb = MOE_BLOCK
    f2 = w1.shape[2]
    de = w2.shape[1]
    last = lambda j, nu: jnp.maximum(jnp.minimum(j, nu[0] - 1), 0)
    grid_spec = pltpu.PrefetchScalarGridSpec(
        num_scalar_prefetch=3,
        grid=(plan["nb"],),
        in_specs=[
            pl.BlockSpec((tb, d), lambda j, be, nu, ne: (last(j, nu), 0)),
            pl.BlockSpec(memory_space=pl.ANY),
            pl.BlockSpec((1, 1, f2), lambda j, be, nu, ne: (be[j], 0, 0)),
            pl.BlockSpec(memory_space=pl.ANY),
            pl.BlockSpec((1, 1, d), lambda j, be, nu, ne: (be[j], 0, 0)),
        ],
        out_specs=pl.BlockSpec((tb, d), lambda j, be, nu, ne: (j, 0)),
        scratch_shapes=[pltpu.VMEM((d, f2), F32), pltpu.VMEM((de, d), F32),
                        pltpu.VMEM((d, f2), BF16), pltpu.VMEM((de, d), BF16),
                        pltpu.SemaphoreType.DMA((2,))],
    )
    return pl.pallas_call(
        _expert_kernel,
        grid_spec=grid_spec,
        out_shape=jax.ShapeDtypeStruct(xs.shape, BF16),
        compiler_params=_cparams(("arbitrary",)),
    )(plan["block_e"], plan["n_used"], plan["next_expert"], xs, w1, b1, w2, b2)


COMBINE_TILES = 2


def _combine_kernel(gdst_ref, totg_ref, x1_ref, srow_ref, wt_ref, g_ref, yb_hbm, o_ref, stage, sem):
    s = pl.program_id(0)
    slot = s % 2
    nu = stage.shape[1]
    tm = x1_ref.shape[0] // nu
    rows = stage.shape[2]

    def copy_into(sl_, u):
        def copy(stage_row, sorted_row, nrows):
            return pltpu.make_async_copy(yb_hbm.at[pl.ds(sorted_row, nrows)],
                                         stage.at[sl_, u, pl.ds(stage_row, nrows)], sem.at[sl_, u])
        return copy

    def fetch(step, sl_, start):
        for u in range(nu):
            _granule_copies(step * nu + u, gdst_ref, totg_ref, copy_into(sl_, u), start=start)

    @pl.when(s == 0)
    def _():
        stage[...] = jnp.zeros_like(stage)
        fetch(s, slot, True)

    @pl.when(s + 1 < pl.num_programs(0))
    def _():
        fetch(s + 1, 1 - slot, True)

    fetch(s, slot, False)

    chunk = LANES
    row_id = lax.broadcasted_iota(jnp.int32, (chunk, rows), 1).astype(F32)
    for u in range(nu):
        for c in range(tm // chunk):
            sl = slice(u * tm + c * chunk, u * tm + (c + 1) * chunk)
            unsort = jnp.zeros((chunk, rows), F32)
            for k in range(TOP_K):
                unsort = jnp.where(row_id == srow_ref[sl, k:k + 1], wt_ref[sl, k:k + 1], unsort)
            y = jnp.dot(unsort.astype(BF16), stage[slot, u], preferred_element_type=F32)
            o_ref[sl, :] = _rms(x1_ref[sl, :] + y, g_ref[...])


def _combine(x1, yb, srow, top_w, plan, g_final, tm):
    n, d = x1.shape
    nu = COMBINE_TILES
    nt = n // (nu * tm)
    rows = _stage_rows(tm)
    grid_spec = pltpu.PrefetchScalarGridSpec(
        num_scalar_prefetch=2,
        grid=(nt,),
        in_specs=[
            pl.BlockSpec((nu * tm, d), lambda t, *_: (t, 0)),
            pl.BlockSpec((nu * tm, LANES), lambda t, *_: (t, 0)),
            pl.BlockSpec((nu * tm, LANES), lambda t, *_: (t, 0)),
            pl.BlockSpec((1, d), lambda t, *_: (0, 0)),
            pl.BlockSpec(memory_space=pl.ANY),
        ],
        out_specs=pl.BlockSpec((nu * tm, d), lambda t, *_: (t, 0)),
        scratch_shapes=[pltpu.VMEM((2, nu, rows, d), BF16), pltpu.SemaphoreType.DMA((2, nu))],
    )
    return pl.pallas_call(
        _combine_kernel,
        grid_spec=grid_spec,
        out_shape=jax.ShapeDtypeStruct((n, d), F32),
        compiler_params=_cparams(("arbitrary",)),
    )(plan["gran_dst"], plan["tot_gran"], x1, srow, top_w, g_final, yb)


def _rope_tables(seq):
    half = ATTN_HEAD_DIM // 2
    inv_freq = np.float32(ROPE_THETA) ** (-np.arange(0, half, 2, dtype=np.float32) / np.float32(half))
    pos = np.arange(seq)
    ang_r = (pos // GRID_W).astype(np.float32)[:, None] * inv_freq
    ang_c = (pos % GRID_W).astype(np.float32)[:, None] * inv_freq
    cos_t = np.concatenate([np.cos(ang_r), np.cos(ang_c)] * 2, axis=-1)
    sin_t = np.concatenate([-np.sin(ang_r), -np.sin(ang_c), np.sin(ang_r), np.sin(ang_c)], axis=-1)
    return jnp.asarray(cos_t, F32), jnp.asarray(sin_t, F32)


def _token_mixer(x2, batch, seq, g_mix, w_in, conv_w, conv_b, dt_bias_f, dt_bias_b, a_log_f, a_log_b, d_skip,
                 g_ssm, q_norm_g, k_norm_g, w_br_ssm, w_br_attn, w_out):
    n, d = x2.shape
    z_end = SSM_INNER
    xbc_end = z_end + CONV_CH
    dtf_end = xbc_end + SSM_HEADS
    dtb_end = dtf_end + SSM_HEADS
    q_end = dtb_end + ATTN_HEADS * ATTN_HEAD_DIM
    k_end = q_end + ATTN_KV_HEADS * ATTN_HEAD_DIM
    v_end = k_end + ATTN_KV_HEADS * ATTN_HEAD_DIM
    head_cols = lambda w: _rope_head_order(w.reshape(d, -1, ATTN_HEAD_DIM)).reshape(d, -1)
    w_main = jnp.concatenate([w_in[:, :z_end], head_cols(w_in[:, dtb_end:q_end]), w_in[:, v_end:],
                              w_in[:, z_end:xbc_end], head_cols(w_in[:, q_end:k_end]), w_in[:, k_end:v_end]],
                             axis=1).astype(BF16)
    w_dt = jnp.pad(w_in[:, xbc_end:dtb_end], ((0, 0), (0, LANES - 2 * SSM_HEADS))).astype(BF16)

    proj, dt, dtt = _in_proj(x2, g_mix.reshape(1, d), w_main, w_dt)
    proj3 = proj.reshape(batch, seq, PROJ_COLS)

    xbc = _conv(proj3, conv_w, conv_b.reshape(1, CONV_CH))

    dt3 = dt.reshape(batch, seq, LANES)
    bias = jnp.concatenate([dt_bias_f, dt_bias_b])
    alog = jnp.concatenate([a_log_f, a_log_b])
    pad_row = lambda v: jnp.pad(v, (0, LANES - 2 * SSM_HEADS)).reshape(1, LANES)
    y_f, y_b = _ssd(xbc, dt3, dtt, pad_row(bias), bias.reshape(-1, 1), pad_row(alog), alog.reshape(-1, 1))

    cos_t, sin_t = _rope_tables(seq)
    q_rot, k_rot = _qk_prep(proj, cos_t, sin_t, _rope_head_order(q_norm_g).reshape(1, -1),
                            _rope_head_order(k_norm_g).reshape(1, -1), seq)
    vt3 = jnp.swapaxes(proj3[:, :, V_OFF:V_OFF + ATTN_KV_HEADS * ATTN_HEAD_DIM], 1, 2)
    attn = _flash(q_rot.reshape(batch, seq, -1), k_rot.reshape(batch, seq, -1), vt3, q_norm_g, k_norm_g)

    return _merge(y_f.reshape(n, -1), y_b.reshape(n, -1), xbc.reshape(n, CONV_CH), proj, attn.reshape(n, -1),
                  x2, jnp.repeat(d_skip, SSM_HEAD_DIM).reshape(1, -1), g_ssm.reshape(1, -1),
                  w_br_ssm.astype(BF16), w_br_attn.astype(BF16), w_out.astype(BF16))


def _moe_and_final_norm(x1, g_ffn, w_router, b_router, w_mlp1, b_mlp1, w_mlp2, b_mlp2, g_final):
    n, d = x1.shape
    tm = min(MOE_TILE, n)
    w_t = w_router.T
    w_hi = w_t.astype(BF16)
    w_lo = (w_t - w_hi.astype(F32)).astype(BF16)
    g_ffn_row = g_ffn.reshape(1, d)
    h2, srow_t, srow, top_w, cnt = _router(x1, g_ffn_row, jnp.concatenate([w_hi, w_lo, w_hi], axis=1),
                                           b_router.reshape(N_EXPERTS, 1))
    plan = _routing_plan(cnt[:, :, 0].astype(jnp.int32), n, tm, MOE_BLOCK)
    xs = _dispatch(h2, srow_t, plan, tm)
    yb = _experts(xs, plan, w_mlp1, b_mlp1[:, None, :], w_mlp2, b_mlp2[:, None, :])
    return _combine(x1, yb, srow, top_w, plan, g_final.reshape(1, d), tm)


def kernel(x, g_mix, w_in, conv_w, conv_b, dt_bias_f, dt_bias_b, a_log_f, a_log_b, d_skip, g_ssm, q_norm_g,
           k_norm_g, w_br_ssm, w_br_attn, w_out, g_ffn, w_router, b_router, w_mlp1, b_mlp1, w_mlp2, b_mlp2,
           g_final):
    batch, seq, d = x.shape
    assert g_mix.shape[0] == 1, "single-layer model: the final rmsnorm is fused into the MoE combine"
    x2 = x.reshape(batch * seq, d)
    x1 = _token_mixer(x2, batch, seq, g_mix[0], w_in[0], conv_w[0], conv_b[0], dt_bias_f[0], dt_bias_b[0],
                      a_log_f[0], a_log_b[0], d_skip[0], g_ssm[0], q_norm_g[0], k_norm_g[0], w_br_ssm[0],
                      w_br_attn[0], w_out[0])
    out = _moe_and_final_norm(x1, g_ffn[0], w_router[0], b_router[0], w_mlp1[0], b_mlp1[0], w_mlp2[0],
                              b_mlp2[0], g_final)
    return out.reshape(batch, seq, d)
```

```python
import functools
import math

import jax
import jax.numpy as jnp
import numpy as np
from jax import lax
from jax.experimental import pallas as pl
from jax.experimental.pallas import tpu as pltpu

F32 = jnp.float32
BF16 = jnp.bfloat16

NORM_EPS = 1e-6
GRID_W = 64
SSM_HEADS = 16
SSM_HEAD_DIM = 64
SSM_INNER = SSM_HEADS * SSM_HEAD_DIM
SSM_GROUPS = 2
SSM_STATE = 128
SSM_CONV = 5
CONV_CH = SSM_INNER + 2 * SSM_GROUPS * SSM_STATE
ATTN_HEADS = 8
ATTN_KV_HEADS = 2
ATTN_HEAD_DIM = 128
ROPE_THETA = 10000.0
N_EXPERTS = 32
TOP_K = 4
SWIGLU_LIMIT = 7.0
SWIGLU_ALPHA = 1.702

LANES = 128
BF16_SUBLANES = 16
VMEM_LIMIT = 56 * 1024 * 1024

Z_OFF, Q_OFF, GATE_OFF, XBC_OFF = 0, 1024, 2048, 4096
K_OFF, V_OFF, PROJ_COLS = 5632, 5888, 6144

MOE_TILE = 512
MOE_BLOCK = 512
MOE_GRAN = BF16_SUBLANES


def _cparams(sem):
    return pltpu.CompilerParams(dimension_semantics=sem, vmem_limit_bytes=VMEM_LIMIT)


def _sigmoid(x):
    return 1.0 / (1.0 + jnp.exp(-x))


def _softplus(x):
    return jnp.maximum(x, 0.0) + jnp.log(1.0 + jnp.exp(-jnp.abs(x)))


def _rms(x, g):
    ms = jnp.mean(x * x, axis=-1, keepdims=True)
    return x * lax.rsqrt(ms + NORM_EPS) * g


def _inproj_kernel(x_ref, g_ref, w_ref, wdt_ref, o_ref, dt_ref, dtt_ref, h_scr):
    @pl.when(pl.program_id(1) == 0)
    def _():
        hb = _rms(x_ref[...], g_ref[...]).astype(BF16)
        h_scr[...] = hb
        dt = jnp.dot(hb, wdt_ref[...], preferred_element_type=F32)
        dt_ref[...] = dt
        dtt_ref[...] = dt.T

    o_ref[...] = jnp.dot(h_scr[...], w_ref[...], preferred_element_type=F32).astype(o_ref.dtype)


def _in_proj(x2, g_mix, w_main, w_dt):
    n, d = x2.shape
    tm = min(1024, n)
    tn = 2048
    return pl.pallas_call(
        _inproj_kernel,
        grid=(n // tm, PROJ_COLS // tn),
        in_specs=[
            pl.BlockSpec((tm, d), lambda i, j: (i, 0)),
            pl.BlockSpec((1, d), lambda i, j: (0, 0)),
            pl.BlockSpec((d, tn), lambda i, j: (0, j)),
            pl.BlockSpec((d, LANES), lambda i, j: (0, 0)),
        ],
        out_specs=[
            pl.BlockSpec((tm, tn), lambda i, j: (i, j)),
            pl.BlockSpec((tm, LANES), lambda i, j: (i, 0)),
            pl.BlockSpec((LANES, tm), lambda i, j: (0, i)),
        ],
        out_shape=[
            jax.ShapeDtypeStruct((n, PROJ_COLS), BF16),
            jax.ShapeDtypeStruct((n, LANES), F32),
            jax.ShapeDtypeStruct((LANES, n), F32),
        ],
        scratch_shapes=[pltpu.VMEM((tm, d), BF16)],
        compiler_params=_cparams(("arbitrary", "arbitrary")),
    )(x2, g_mix, w_main, w_dt)


CONV_HALO = 64
CONV_ROWS = 128


def _conv_kernel(prev_ref, cur_ref, next_ref, shift_ref, w_ref, b_ref, o_ref):
    s = pl.program_id(1)
    ts = cur_ref.shape[1]
    prev = prev_ref[0]
    nxt = next_ref[0]
    zero = jnp.zeros_like(prev)
    ext = jnp.concatenate([jnp.where(s == 0, zero, prev), cur_ref[0],
                           jnp.where(s == pl.num_programs(1) - 1, zero, nxt)], axis=0)
    shift = shift_ref[...]
    for rb in range(ts // CONV_ROWS):
        lo = rb * CONV_ROWS
        taps = jnp.dot(shift, ext[lo:lo + CONV_ROWS + 2 * CONV_HALO], preferred_element_type=F32)
        acc = b_ref[...] + w_ref[0:1, :] * taps[0:CONV_ROWS]
        for k in range(1, SSM_CONV):
            acc = acc + w_ref[k:k + 1, :] * taps[k * CONV_ROWS:(k + 1) * CONV_ROWS]
        o_ref[0, lo:lo + CONV_ROWS, :] = (acc * _sigmoid(acc)).astype(o_ref.dtype)


def _conv(proj3, conv_w, conv_b):
    b, s, _ = proj3.shape
    ts = min(2048, s)
    tc = 512
    halo = CONV_HALO
    hb = ts // halo
    col0 = XBC_OFF // tc
    pad = (SSM_CONV - 1) // 2
    out_row = np.arange(SSM_CONV * CONV_ROWS)
    src = out_row % CONV_ROWS + halo + out_row // CONV_ROWS - pad
    shift = jnp.asarray(src[:, None] == np.arange(CONV_ROWS + 2 * halo)[None, :], BF16)
    return pl.pallas_call(
        _conv_kernel,
        grid=(b, s // ts, CONV_CH // tc),
        in_specs=[
            pl.BlockSpec((1, halo, tc), lambda bi, si, ci: (bi, jnp.maximum(si * hb - 1, 0), col0 + ci)),
            pl.BlockSpec((1, ts, tc), lambda bi, si, ci: (bi, si, col0 + ci)),
            pl.BlockSpec((1, halo, tc),
                         lambda bi, si, ci: (bi, jnp.minimum((si + 1) * hb, s // halo - 1), col0 + ci)),
            pl.BlockSpec(shift.shape, lambda bi, si, ci: (0, 0)),
            pl.BlockSpec((SSM_CONV, tc), lambda bi, si, ci: (0, ci)),
            pl.BlockSpec((1, tc), lambda bi, si, ci: (0, ci)),
        ],
        out_specs=pl.BlockSpec((1, ts, tc), lambda bi, si, ci: (bi, si, ci)),
        out_shape=jax.ShapeDtypeStruct((b, s, CONV_CH), BF16),
        compiler_params=_cparams(("arbitrary", "arbitrary", "arbitrary")),
    )(proj3, proj3, proj3, shift, conv_w, conv_b)


SSM_CHUNKS_PER_STEP = 4


def _ssd_kernel(xf_ref, xb_ref, dtf_ref, dtb_ref, dttf_ref, dttb_ref, brow_ref, bcol_ref,
                arow_ref, acol_ref, yf_ref, yb_ref, st_ref):
    L = SSM_STATE
    nsub = xf_ref.shape[1] // L
    hg = SSM_HEADS // SSM_GROUPS
    pairs = hg // 2

    @pl.when(pl.program_id(1) == 0)
    def _():
        st_ref[...] = jnp.zeros_like(st_ref)

    rows = lax.broadcasted_iota(jnp.int32, (L, L), 0)
    cols = lax.broadcasted_iota(jnp.int32, (L, L), 1)
    lower = rows >= cols
    upper = rows <= cols
    ltri = lower.astype(BF16)
    utri = upper.astype(BF16)
    lane = lax.broadcasted_iota(jnp.int32, (L, LANES), 1)
    first_half = lane < SSM_HEAD_DIM
    lane1 = lax.broadcasted_iota(jnp.int32, (1, LANES), 1)
    log2e = math.log2(math.e)
    a_row = -jnp.exp(arow_ref[...]) * log2e
    a_col = -jnp.exp(acol_ref[...]) * log2e

    def split3(v):
        hi = v.astype(BF16)
        r1 = v - hi.astype(F32)
        mid = r1.astype(BF16)
        return hi, mid, (r1 - mid.astype(F32)).astype(BF16)

    def cumsum_cols(tri, v):
        return jnp.dot(jnp.concatenate([tri] * 3, axis=1), jnp.concatenate(split3(v), axis=0),
                       preferred_element_type=F32)

    def cumsum_rows(v, tri):
        return jnp.dot(jnp.concatenate(split3(v), axis=1), jnp.concatenate([tri] * 3, axis=0),
                       preferred_element_type=F32)

    for sub, d in [(sub, d) for sub in range(nsub) for d in range(2)]:
        x_ref, dt_ref, dtt_ref, y_ref = ((xf_ref, dtf_ref, dttf_ref, yf_ref) if d == 0
                                         else (xb_ref, dtb_ref, dttb_ref, yb_ref))
        r0 = (sub if d == 0 else nsub - 1 - sub) * L
        rs = slice(r0, r0 + L)
        a = _softplus(dt_ref[0, rs, :] + brow_ref[...]) * a_row
        dt_t = _softplus(dtt_ref[:, rs] + bcol_ref[...])
        a_t = dt_t * a_col
        if d == 0:
            cs_col = cumsum_cols(ltri, a)
            cs_row = cumsum_rows(a_t, utri)
            tot = cs_col[L - 1:L, :]
            tot_t = cs_row[:, L - 1:L]
            mask = lower
        else:
            cs_col = cumsum_cols(utri, a)
            cs_row = cumsum_rows(a_t, ltri)
            tot = cs_col[0:1, :]
            tot_t = cs_row[:, 0:1]
            mask = upper
        w_t = dt_t * jnp.exp2(tot_t - cs_row)
        src_t = cs_row - jnp.log2(dt_t)
        chunk_decay = jnp.exp2(tot)

        for g in range(SSM_GROUPS):
            boff = SSM_INNER + g * SSM_STATE
            coff = SSM_INNER + SSM_GROUPS * SSM_STATE + g * SSM_STATE
            bm = x_ref[0, rs, boff:boff + SSM_STATE]
            cm = x_ref[0, rs, coff:coff + SSM_STATE]
            cb = lax.dot_general(cm, bm, (((1,), (1,)), ((), ())), preferred_element_type=F32)
            bt = bm.astype(F32).T
            st = st_ref[d, g]
            y_off = jnp.dot(cm, st.astype(BF16), preferred_element_type=F32)
            for pr in range(pairs):
                h0 = d * SSM_HEADS + g * hg + 2 * pr
                xoff = (g * pairs + pr) * LANES
                xs = x_ref[0, rs, xoff:xoff + LANES]
                zero = jnp.zeros_like(xs)
                rhs = jnp.concatenate([jnp.where(first_half, xs, zero),
                                       jnp.where(first_half, zero, xs)], axis=0)
                ms, ws, dins = [], [], []
                for hh in (h0, h0 + 1):
                    cs_b = jnp.broadcast_to(cs_col[:, hh:hh + 1], (L, L))
                    seg = cs_b - src_t[hh:hh + 1, :]
                    m = cb * jnp.exp2(jnp.where(mask, seg, -jnp.inf))
                    ms.append(m.astype(BF16))
                    ws.append((bt * w_t[hh:hh + 1, :]).astype(BF16))
                    dins.append(jnp.exp2(cs_b))
                y = jnp.dot(jnp.concatenate(ms, axis=1), rhs, preferred_element_type=F32)
                y = y + y_off[:, pr * LANES:(pr + 1) * LANES] * jnp.where(first_half, dins[0], dins[1])
                y_ref[0, rs, xoff:xoff + LANES] = y.astype(y_ref.dtype)
                new_st = jnp.dot(jnp.concatenate(ws, axis=1), rhs, preferred_element_type=F32)
                cd = jnp.where(lane1 < SSM_HEAD_DIM, chunk_decay[:, h0:h0 + 1], chunk_decay[:, h0 + 1:h0 + 2])
                st_ref[d, g, :, pr * LANES:(pr + 1) * LANES] = st[:, pr * LANES:(pr + 1) * LANES] * cd + new_st


def _ssd(xbc, dt3, dtt3, bias_row, bias_col, alog_row, alog_col):
    b, s, _ = xbc.shape
    L = SSM_CHUNKS_PER_STEP * SSM_STATE
    nc = s // L
    hg = SSM_HEADS // SSM_GROUPS
    fwd = lambda bi, ci: (bi, ci, 0)
    bwd = lambda bi, ci: (bi, nc - 1 - ci, 0)
    fwd_t = lambda bi, ci: (0, bi * nc + ci)
    bwd_t = lambda bi, ci: (0, bi * nc + nc - 1 - ci)
    const = lambda bi, ci: (0, 0)
    return pl.pallas_call(
        _ssd_kernel,
        grid=(b, nc),
        in_specs=[
            pl.BlockSpec((1, L, CONV_CH), fwd),
            pl.BlockSpec((1, L, CONV_CH), bwd),
            pl.BlockSpec((1, L, LANES), fwd),
            pl.BlockSpec((1, L, LANES), bwd),
            pl.BlockSpec((2 * SSM_HEADS, L), fwd_t),
            pl.BlockSpec((2 * SSM_HEADS, L), bwd_t),
            pl.BlockSpec((1, LANES), const),
            pl.BlockSpec((2 * SSM_HEADS, 1), const),
            pl.BlockSpec((1, LANES), const),
            pl.BlockSpec((2 * SSM_HEADS, 1), const),
        ],
        out_specs=[
            pl.BlockSpec((1, L, SSM_INNER), fwd),
            pl.BlockSpec((1, L, SSM_INNER), bwd),
        ],
        out_shape=[jax.ShapeDtypeStruct((b, s, SSM_INNER), BF16)] * 2,
        scratch_shapes=[pltpu.VMEM((2, SSM_GROUPS, SSM_STATE, hg * SSM_HEAD_DIM), F32)],
        compiler_params=_cparams(("arbitrary", "arbitrary")),
    )(xbc, xbc, dt3, dt3, dtt3, dtt3, bias_row, bias_col, alog_row, alog_col)


def _rope_head_order(a):
    q4 = ATTN_HEAD_DIM // 4
    return jnp.concatenate([a[..., 0:q4], a[..., 2 * q4:3 * q4], a[..., q4:2 * q4], a[..., 3 * q4:]], axis=-1)


def _rope_norm(t, g, cos, sin_signed, ones):
    sq = t * t
    hi = sq.astype(BF16)
    lo = (sq - hi.astype(F32)).astype(BF16)
    ms = jnp.dot(jnp.concatenate([hi, lo], axis=1), ones, preferred_element_type=F32) * (1.0 / ATTN_HEAD_DIM)
    tn = t * lax.rsqrt(ms + NORM_EPS) * g
    return tn * cos + pltpu.roll(tn, ATTN_HEAD_DIM // 2, 1) * sin_signed


Q_SCALE = ATTN_HEAD_DIM ** -0.5 * math.log2(math.e)


def _qkprep_kernel(q_ref, k_ref, v_ref, cos_ref, sin_ref, qg_ref, kg_ref, qo_ref, ko_ref, vt_ref):
    cos = cos_ref[...]
    sin = sin_ref[...]
    ones = jnp.ones((2 * ATTN_HEAD_DIM, ATTN_HEAD_DIM), BF16)

    def heads(src_ref, g_ref, dst_ref, n_heads, scale):
        for h in range(n_heads):
            sl = slice(h * ATTN_HEAD_DIM, (h + 1) * ATTN_HEAD_DIM)
            r = _rope_norm(src_ref[:, sl].astype(F32), g_ref[...], cos, sin, ones) * scale
            dst_ref[:, sl] = r.astype(dst_ref.dtype)

    heads(q_ref, qg_ref, qo_ref, ATTN_HEADS, Q_SCALE)
    heads(k_ref, kg_ref, ko_ref, ATTN_KV_HEADS, 1.0)
    vt_ref[...] = v_ref[...].astype(F32).T.astype(vt_ref.dtype)


def _qk_prep(proj, cos_t, sin_t, q_norm_g, k_norm_g, seq):
    n = proj.shape[0]
    tm = min(1024, seq)
    qw = ATTN_HEADS * ATTN_HEAD_DIM
    kw = ATTN_KV_HEADS * ATTN_HEAD_DIM
    spt = seq // tm
    return pl.pallas_call(
        _qkprep_kernel,
        grid=(n // tm,),
        in_specs=[
            pl.BlockSpec((tm, qw), lambda i: (i, Q_OFF // qw)),
            pl.BlockSpec((tm, kw), lambda i: (i, K_OFF // kw)),
            pl.BlockSpec((tm, kw), lambda i: (i, V_OFF // kw)),
            pl.BlockSpec((tm, ATTN_HEAD_DIM), lambda i: (i % spt, 0)),
            pl.BlockSpec((tm, ATTN_HEAD_DIM), lambda i: (i % spt, 0)),
            pl.BlockSpec((1, ATTN_HEAD_DIM), lambda i: (0, 0)),
            pl.BlockSpec((1, ATTN_HEAD_DIM), lambda i: (0, 0)),
        ],
        out_specs=[
            pl.BlockSpec((tm, qw), lambda i: (i, 0)),
            pl.BlockSpec((tm, kw), lambda i: (i, 0)),
            pl.BlockSpec((kw, tm), lambda i: (0, i)),
        ],
        out_shape=[jax.ShapeDtypeStruct((n, qw), BF16), jax.ShapeDtypeStruct((n, kw), BF16),
                   jax.ShapeDtypeStruct((kw, n), BF16)],
        compiler_params=_cparams(("arbitrary",)),
    )(proj, proj, proj, cos_t, sin_t, q_norm_g, k_norm_g)


def _flash_kernel(small_ref, q_ref, k_ref, vt_ref, o_ref, *, tk, th):
    q = q_ref[0]
    tq = q.shape[0]
    seq = k_ref.shape[1]
    nt = (((1,), (1,)), ((), ()))
    small = small_ref[0] != 0

    @pl.when(small)
    def _():
        l = acc_t = None
        for c in range(seq // th):
            ks = slice(c * th, (c + 1) * th)
            p_t = jnp.exp2(lax.dot_general(k_ref[0, ks, :], q, nt, preferred_element_type=F32))
            l_c = jnp.sum(p_t, axis=0, keepdims=True)
            a_c = jnp.dot(vt_ref[:, ks], p_t.astype(BF16), preferred_element_type=F32)
            l = l_c if l is None else l + l_c
            acc_t = a_c if acc_t is None else acc_t + a_c
        o_ref[0] = (acc_t / l).T.astype(o_ref.dtype)

    @pl.when(jnp.logical_not(small))
    def _():
        def body(i, carry):
            m, l, acc_t = carry
            off = pl.multiple_of(i * tk, tk)
            s_t = lax.dot_general(k_ref[0, pl.ds(off, tk), :], q, nt, preferred_element_type=F32)
            m_new = jnp.maximum(m, jnp.max(s_t, axis=0, keepdims=True))
            alpha = jnp.exp2(m - m_new)
            p_t = jnp.exp2(s_t - m_new)
            l = alpha * l + jnp.sum(p_t, axis=0, keepdims=True)
            acc_t = alpha * acc_t + jnp.dot(vt_ref[:, pl.ds(off, tk)], p_t.astype(BF16),
                                            preferred_element_type=F32)
            return m_new, l, acc_t

        init = (jnp.full((1, tq), -jnp.inf, F32), jnp.zeros((1, tq), F32),
                jnp.zeros((ATTN_HEAD_DIM, tq), F32))
        _, l, acc_t = lax.fori_loop(0, seq // tk, body, init)
        o_ref[0] = (acc_t / l).T.astype(o_ref.dtype)


SCORE_BOUND = 59.0


def _flash(q3, k3, vt3, q_norm_g, k_norm_g):
    b, s, _ = q3.shape
    tq = min(1024, s)
    tk = min(2048, s)
    th = min(4096, s)
    nq = s // tq
    grp = ATTN_HEADS // ATTN_KV_HEADS
    hd = ATTN_HEAD_DIM
    bound = hd * Q_SCALE * jnp.max(jnp.abs(q_norm_g)) * jnp.max(jnp.abs(k_norm_g)) * 1.02
    small = (bound <= SCORE_BOUND).astype(jnp.int32).reshape(1)
    grid_spec = pltpu.PrefetchScalarGridSpec(
        num_scalar_prefetch=1,
        grid=(b, ATTN_HEADS, nq),
        in_specs=[
            pl.BlockSpec((1, tq, hd), lambda bi, h, qi, sm: (bi, qi, h)),
            pl.BlockSpec((1, s, hd), lambda bi, h, qi, sm: (bi, 0, h // grp)),
            pl.BlockSpec((hd, s), lambda bi, h, qi, sm: (h // grp, bi)),
        ],
        out_specs=pl.BlockSpec((1, tq, hd), lambda bi, h, qi, sm: (bi, qi, h)),
    )
    return pl.pallas_call(
        functools.partial(_flash_kernel, tk=tk, th=th),
        grid_spec=grid_spec,
        out_shape=jax.ShapeDtypeStruct((b, s, ATTN_HEADS * hd), BF16),
        compiler_params=_cparams(("arbitrary", "arbitrary", "arbitrary")),
    )(small, q3, k3, vt3)


def _merge_kernel(yf_ref, yb_ref, xs_ref, z_ref, gate_ref, attn_ref, x_ref, dskip_ref, gssm_ref,
                  wbs_ref, wba_ref, wo_ref, x1_ref):
    d = x_ref.shape[1]
    xs = xs_ref[...].astype(F32)
    y = yf_ref[...].astype(F32) + yb_ref[...].astype(F32) + xs * dskip_ref[...]
    z = z_ref[...].astype(F32)
    y = _rms(y * (z * _sigmoid(z)), gssm_ref[...])
    br_ssm = jnp.dot(y.astype(BF16), wbs_ref[...], preferred_element_type=F32)
    br_attn = jnp.dot(attn_ref[...], wba_ref[...], preferred_element_type=F32)
    g_s = _sigmoid(gate_ref[:, :d].astype(F32))
    g_a = _sigmoid(gate_ref[:, d:].astype(F32))
    merged = (g_s * br_ssm + g_a * br_attn).astype(BF16)
    x1_ref[...] = x_ref[...] + jnp.dot(merged, wo_ref[...], preferred_element_type=F32)


def _merge(y_f, y_b, xbc, proj, attn, x2, dskip_row, g_ssm, w_br_ssm, w_br_attn, w_out):
    n, d = x2.shape
    tm = min(512, n)
    row = lambda i: (i, 0)
    const = lambda i: (0, 0)
    return pl.pallas_call(
        _merge_kernel,
        grid=(n // tm,),
        in_specs=[
            pl.BlockSpec((tm, d), row),
            pl.BlockSpec((tm, d), row),
            pl.BlockSpec((tm, d), row),
            pl.BlockSpec((tm, d), lambda i: (i, Z_OFF // d)),
            pl.BlockSpec((tm, 2 * d), lambda i: (i, GATE_OFF // (2 * d))),
            pl.BlockSpec((tm, d), row),
            pl.BlockSpec((tm, d), row),
            pl.BlockSpec((1, d), const),
            pl.BlockSpec((1, d), const),
            pl.BlockSpec((d, d), const),
            pl.BlockSpec((d, d), const),
            pl.BlockSpec((d, d), const),
        ],
        out_specs=pl.BlockSpec((tm, d), row),
        out_shape=jax.ShapeDtypeStruct((n, d), F32),
        compiler_params=_cparams(("arbitrary",)),
    )(y_f, y_b, xbc, proj, proj, attn, x2, dskip_row, g_ssm, w_br_ssm, w_br_attn, w_out)


def _router_kernel(x1_ref, g_ref, w_ref, b_ref, h_ref, srow_ref, srow_l_ref, wt_l_ref, cnt_ref):
    tm = x1_ref.shape[0]
    h = _rms(x1_ref[...], g_ref[...])
    h_hi = h.astype(BF16)
    h_lo = (h - h_hi.astype(F32)).astype(BF16)
    h_ref[...] = h_hi
    logits = lax.dot_general(w_ref[...], jnp.concatenate([h_hi, h_hi, h_lo], axis=1), (((1,), (1,)), ((), ())),
                             preferred_element_type=F32) + b_ref[...]
    expert = lax.broadcasted_iota(jnp.int32, logits.shape, 0)
    slot = lax.broadcasted_iota(jnp.int32, srow_ref.shape, 0)
    chosen = jnp.zeros(logits.shape, F32)
    vals, hits = [], []
    for k in range(TOP_K):
        m = jnp.max(logits, axis=0, keepdims=True)
        idx = jnp.min(jnp.where(logits == m, expert, N_EXPERTS), axis=0, keepdims=True)
        vals.append(m)
        hit = expert == idx
        hits.append(hit)
        chosen = jnp.where(hit, 1.0, chosen)
        logits = jnp.where(hit, -jnp.inf, logits)
    es = [jnp.exp(v - vals[0]) for v in vals]
    tot = es[0] + es[1] + es[2] + es[3]

    chosen16 = chosen.astype(BF16)
    cnt_col = jnp.sum(chosen, axis=1, keepdims=True)
    cnt_row = lax.dot_general(jnp.ones((8, tm), BF16), chosen16, (((1,), (1,)), ((), ())),
                              preferred_element_type=F32)[0:1, :]
    seg_row = jnp.ceil(cnt_row * (1.0 / MOE_GRAN)) * MOE_GRAN
    before = (lax.broadcasted_iota(jnp.int32, (N_EXPERTS, N_EXPERTS), 1)
              < lax.broadcasted_iota(jnp.int32, (N_EXPERTS, N_EXPERTS), 0))
    seg_off = jnp.sum(jnp.where(before, seg_row, 0.0), axis=1, keepdims=True)
    earlier = (lax.broadcasted_iota(jnp.int32, (tm, tm), 0)
               < lax.broadcasted_iota(jnp.int32, (tm, tm), 1)).astype(BF16)
    pos = jnp.dot(chosen16, earlier, preferred_element_type=F32) + seg_off

    val_out = jnp.zeros(srow_ref.shape, F32)
    row_out = jnp.zeros(srow_ref.shape, F32)
    for k in range(TOP_K):
        val_out = jnp.where(slot == k, es[k] / tot, val_out)
        row_out = jnp.where(slot == k, jnp.sum(jnp.where(hits[k], pos, 0.0), axis=0, keepdims=True), row_out)
    srow_ref[...] = row_out
    fill = jnp.zeros((LANES - srow_ref.shape[0], tm), F32)
    srow_l_ref[...] = jnp.concatenate([row_out, fill], axis=0).T
    wt_l_ref[...] = jnp.concatenate([val_out, fill], axis=0).T
    cnt_ref[0] = jnp.broadcast_to(cnt_col, cnt_ref.shape[1:])


def _router(x1, g_ffn, w_router_t, b_router_col):
    n, d = x1.shape
    tm = min(MOE_TILE, n)
    return pl.pallas_call(
        _router_kernel,
        grid=(n // tm,),
        in_specs=[
            pl.BlockSpec((tm, d), lambda i: (i, 0)),
            pl.BlockSpec((1, d), lambda i: (0, 0)),
            pl.BlockSpec((N_EXPERTS, 3 * d), lambda i: (0, 0)),
            pl.BlockSpec((N_EXPERTS, 1), lambda i: (0, 0)),
        ],
        out_specs=[pl.BlockSpec((tm, d), lambda i: (i, 0)),
                   pl.BlockSpec((8, tm), lambda i: (0, i)),
                   pl.BlockSpec((tm, LANES), lambda i: (i, 0)),
                   pl.BlockSpec((tm, LANES), lambda i: (i, 0)),
                   pl.BlockSpec((1, N_EXPERTS, LANES), lambda i: (i, 0, 0))],
        out_shape=[jax.ShapeDtypeStruct((n, d), BF16),
                   jax.ShapeDtypeStruct((8, n), F32),
                   jax.ShapeDtypeStruct((n, LANES), F32),
                   jax.ShapeDtypeStruct((n, LANES), F32),
                   jax.ShapeDtypeStruct((n // tm, N_EXPERTS, LANES), F32)],
        compiler_params=_cparams(("arbitrary",)),
    )(x1, g_ffn, w_router_t, b_router_col)


def _routing_plan(cnt, n, tm, tb):
    nt = n // tm
    gran = MOE_GRAN
    rc = (cnt + gran - 1) // gran * gran
    covered = jnp.sum(rc, axis=0)
    region = (covered + tb - 1) // tb * tb
    pad_end = jnp.cumsum(region)
    pad_start = pad_end - region
    seg_start = pad_start[None, :] + jnp.cumsum(rc, axis=0) - rc
    n_used = pad_end[-1] // tb
    nb = (TOP_K * n + nt * N_EXPERTS * (gran - 1) + N_EXPERTS * (tb - 1) + tb - 1) // tb
    blk = jnp.arange(nb, dtype=jnp.int32)
    block_e = jnp.sum((pad_end[None, :] <= (jnp.minimum(blk, n_used - 1) * tb)[:, None]).astype(jnp.int32), axis=1)
    block_e = jnp.minimum(block_e, N_EXPERTS - 1)
    flat = lambda a: a.reshape(-1).astype(jnp.int32)
    tail_gran = (region - covered) // gran
    ngran = rc // gran
    g_end = jnp.cumsum(ngran, axis=1)
    gi = jnp.arange(_stage_rows(tm) // gran, dtype=jnp.int32)
    g_exp = jnp.minimum(jnp.sum((g_end[:, None, :] <= gi[None, :, None]).astype(jnp.int32), axis=2), N_EXPERTS - 1)
    mine = g_exp[:, :, None] == jnp.arange(N_EXPERTS, dtype=jnp.int32)[None, None, :]
    seg_base = seg_start - gran * (g_end - ngran)
    gran_dst = jnp.sum(jnp.where(mine, seg_base[:, None, :], 0), axis=2) + gran * gi[None, :]
    ids = jnp.arange(N_EXPERTS, dtype=jnp.int32)
    later = jnp.where((ids[None, :] > ids[:, None]) & (region[None, :] > 0), ids[None, :], N_EXPERTS)
    nxt = jnp.min(later, axis=1)
    next_expert = jnp.where(nxt < N_EXPERTS, nxt, ids)
    return dict(gran_dst=flat(gran_dst), tot_gran=flat(g_end[:, -1]), block_e=flat(block_e),
                next_expert=flat(next_expert),
                n_used=flat(n_used), nb=nb,
                tail_start=flat(pad_start + covered), tail_gran=flat(tail_gran),
                tot_tail=flat(jnp.sum(tail_gran)))


def _stage_rows(tm):
    rows = TOP_K * tm + N_EXPERTS * (MOE_GRAN - 1)
    return (rows + 2 * LANES - 1) // (2 * LANES) * (2 * LANES)


GRAN_UNROLL = 4


def _granule_copies(t, gdst_ref, totg_ref, make_copy, start):
    total = totg_ref[t]
    base = t * (gdst_ref.shape[0] // totg_ref.shape[0])
    full = lax.shift_right_logical(total, GRAN_UNROLL.bit_length() - 1)

    def one(i):
        make_copy(pl.multiple_of(i * MOE_GRAN, MOE_GRAN), pl.multiple_of(gdst_ref[base + i], MOE_GRAN),
                  MOE_GRAN).start()

    def group(q, c):
        if start:
            for u in range(GRAN_UNROLL):
                one(q * GRAN_UNROLL + u)
        else:
            make_copy(0, 0, GRAN_UNROLL * MOE_GRAN).wait()
        return c

    def rest(i, c):
        if start:
            one(i)
        else:
            make_copy(0, 0, MOE_GRAN).wait()
        return c

    lax.fori_loop(0, full, group, 0)
    lax.fori_loop(full * GRAN_UNROLL, total, rest, 0)


def _dispatch_kernel(gdst_ref, totg_ref, tstart_ref, tgran_ref, misc_ref,
                     h_ref, srow_ref, xs_hbm, stage, zbuf, sem, zsem, *, n_blocks):
    t = pl.program_id(0)
    slot = t % 2
    tm = h_ref.shape[0]
    rows = stage.shape[1]
    h = h_ref[...]
    srows = [srow_ref[k:k + 1, :] for k in range(TOP_K)]
    chunk = 2 * LANES
    row_id = lax.broadcasted_iota(jnp.int32, (chunk, tm), 0).astype(F32)
    for c in range(rows // chunk):
        perm = jnp.zeros((chunk, tm), F32)
        for srow in srows:
            perm = jnp.where(row_id == srow - float(c * chunk), 1.0, perm)
        stage[slot, c * chunk:(c + 1) * chunk, :] = jnp.dot(
            perm.astype(BF16), h, preferred_element_type=F32).astype(stage.dtype)

    def copy_from(s):
        def copy(stage_row, sorted_row, nrows):
            return pltpu.make_async_copy(stage.at[s, pl.ds(stage_row, nrows)],
                                         xs_hbm.at[pl.ds(sorted_row, nrows)], sem.at[s])
        return copy

    _granule_copies(t, gdst_ref, totg_ref, copy_from(slot), start=True)

    @pl.when(t > 0)
    def _():
        _granule_copies(jnp.maximum(t - 1, 0), gdst_ref, totg_ref, copy_from(1 - slot), start=False)

    @pl.when(t == pl.num_programs(0) - 1)
    def _():
        _granule_copies(t, gdst_ref, totg_ref, copy_from(slot), start=False)
        tb = zbuf.shape[0]
        n_used = misc_ref[0]
        zbuf[...] = jnp.zeros_like(zbuf)

        def zero_gran(row):
            return pltpu.make_async_copy(zbuf.at[pl.ds(0, MOE_GRAN)], xs_hbm.at[pl.ds(row, MOE_GRAN)], zsem)

        def zero_block(blk):
            return pltpu.make_async_copy(zbuf, xs_hbm.at[pl.ds(pl.multiple_of(blk * tb, tb), tb)], zsem)

        def tails(e, c):
            def one(g, c2):
                zero_gran(pl.multiple_of(tstart_ref[e] + g * MOE_GRAN, MOE_GRAN)).start()
                return c2
            lax.fori_loop(0, tgran_ref[e], one, 0)
            return c

        def start_block(blk, c):
            zero_block(blk).start()
            return c

        def wait_gran(g, c):
            zero_gran(0).wait()
            return c

        def wait_block(blk, c):
            zero_block(0).wait()
            return c

        lax.fori_loop(0, N_EXPERTS, tails, 0)
        lax.fori_loop(n_used, n_blocks, start_block, 0)
        lax.fori_loop(0, misc_ref[1], wait_gran, 0)
        lax.fori_loop(n_used, n_blocks, wait_block, 0)


def _dispatch(h2, srow_t, plan, tm):
    n, d = h2.shape
    nt = n // tm
    rows = _stage_rows(tm)
    misc = jnp.concatenate([plan["n_used"], plan["tot_tail"]])
    grid_spec = pltpu.PrefetchScalarGridSpec(
        num_scalar_prefetch=5,
        grid=(nt,),
        in_specs=[
            pl.BlockSpec((tm, d), lambda t, *_: (t, 0)),
            pl.BlockSpec((8, tm), lambda t, *_: (0, t)),
        ],
        out_specs=pl.BlockSpec(memory_space=pl.ANY),
        scratch_shapes=[pltpu.VMEM((2, rows, d), BF16), pltpu.VMEM((MOE_BLOCK, d), BF16),
                        pltpu.SemaphoreType.DMA((2,)), pltpu.SemaphoreType.DMA(())],
    )
    return pl.pallas_call(
        functools.partial(_dispatch_kernel, n_blocks=plan["nb"]),
        grid_spec=grid_spec,
        out_shape=jax.ShapeDtypeStruct((plan["nb"] * MOE_BLOCK, d), BF16),
        compiler_params=_cparams(("arbitrary",)),
    )(plan["gran_dst"], plan["tot_gran"], plan["tail_start"], plan["tail_gran"], misc, h2, srow_t)


def _expert_kernel(be_ref, nused_ref, nexte_ref, x_ref, w1_hbm, b1_ref, w2_hbm, b2_ref, o_ref,
                   w1f, w2f, w1b, w2b, sem):
    de = w2f.shape[0]
    j = pl.program_id(0)
    live = j < nused_ref[0]
    e = be_ref[j]
    new_expert = jnp.logical_or(j == 0, e != be_ref[jnp.maximum(j - 1, 0)])

    def fetch(ex):
        return (pltpu.make_async_copy(w1_hbm.at[ex], w1f, sem.at[0]),
                pltpu.make_async_copy(w2_hbm.at[ex], w2f, sem.at[1]))

    @pl.when(jnp.logical_and(live, j == 0))
    def _():
        for c in fetch(e):
            c.start()

    @pl.when(jnp.logical_and(live, new_expert))
    def _():
        for c in fetch(e):
            c.wait()
        w1b[...] = w1f[...].astype(BF16)
        w2b[...] = w2f[...].astype(BF16)
        nxt = nexte_ref[e]

        @pl.when(nxt != e)
        def _():
            for c in fetch(nxt):
                c.start()

    @pl.when(live)
    def _():
        hm = jnp.dot(x_ref[...], w1b[...], preferred_element_type=F32) + b1_ref[0]
        gate = jnp.minimum(hm[:, :de], SWIGLU_LIMIT)
        up = jnp.clip(hm[:, de:], -SWIGLU_LIMIT, SWIGLU_LIMIT)
        act = gate * _sigmoid(SWIGLU_ALPHA * gate) * (up + 1.0)
        y = jnp.dot(act.astype(BF16), w2b[...], preferred_element_type=F32) + b2_ref[0]
        o_ref[...] = y.astype(o_ref.dtype)

    @pl.when(jnp.logical_not(live))
    def _():
        o_ref[...] = jnp.zeros_like(o_ref)


def _experts(xs, plan, w1, b1, w2, b2):
    d = xs.shape[1]
    tb = MOE_BLOCK
    f2 = w1.shape[2]
    de = w2.shape[1]
    last = lambda j, nu: jnp.maximum(jnp.minimum(j, nu[0] - 1), 0)
    grid_spec = pltpu.PrefetchScalarGridSpec(
        num_scalar_prefetch=3,
        grid=(plan["nb"],),
        in_specs=[
            pl.BlockSpec((tb, d), lambda j, be, nu, ne: (last(j, nu), 0)),
            pl.BlockSpec(memory_space=pl.ANY),
            pl.BlockSpec((1, 1, f2), lambda j, be, nu, ne: (be[j], 0, 0)),
            pl.BlockSpec(memory_space=pl.ANY),
            pl.BlockSpec((1, 1, d), lambda j, be, nu, ne: (be[j], 0, 0)),
        ],
        out_specs=pl.BlockSpec((tb, d), lambda j, be, nu, ne: (j, 0)),
        scratch_shapes=[pltpu.VMEM((d, f2), F32), pltpu.VMEM((de, d), F32),
                        pltpu.VMEM((d, f2), BF16), pltpu.VMEM((de, d), BF16),
                        pltpu.SemaphoreType.DMA((2,))],
    )
    return pl.pallas_call(
        _expert_kernel,
        grid_spec=grid_spec,
        out_shape=jax.ShapeDtypeStruct(xs.shape, BF16),
        compiler_params=_cparams(("arbitrary",)),
    )(plan["block_e"], plan["n_used"], plan["next_expert"], xs, w1, b1, w2, b2)


COMBINE_TILES = 2


def _combine_kernel(gdst_ref, totg_ref, x1_ref, srow_ref, wt_ref, g_ref, yb_hbm, o_ref, stage, sem):
    s = pl.program_id(0)
    slot = s % 2
    nu = stage.shape[1]
    tm = x1_ref.shape[0] // nu
    rows = stage.shape[2]

    def copy_into(sl_, u):
        def copy(stage_row, sorted_row, nrows):
            return pltpu.make_async_copy(yb_hbm.at[pl.ds(sorted_row, nrows)],
                                         stage.at[sl_, u, pl.ds(stage_row, nrows)], sem.at[sl_, u])
        return copy

    def fetch(step, sl_, start):
        for u in range(nu):
            _granule_copies(step * nu + u, gdst_ref, totg_ref, copy_into(sl_, u), start=start)

    @pl.when(s == 0)
    def _():
        stage[...] = jnp.zeros_like(stage)
        fetch(s, slot, True)

    @pl.when(s + 1 < pl.num_programs(0))
    def _():
        fetch(s + 1, 1 - slot, True)

    fetch(s, slot, False)

    chunk = LANES
    row_id = lax.broadcasted_iota(jnp.int32, (chunk, rows), 1).astype(F32)
    for u in range(nu):
        for c in range(tm // chunk):
            sl = slice(u * tm + c * chunk, u * tm + (c + 1) * chunk)
            unsort = jnp.zeros((chunk, rows), F32)
            for k in range(TOP_K):
                unsort = jnp.where(row_id == srow_ref[sl, k:k + 1], wt_ref[sl, k:k + 1], unsort)
            y = jnp.dot(unsort.astype(BF16), stage[slot, u], preferred_element_type=F32)
            o_ref[sl, :] = _rms(x1_ref[sl, :] + y, g_ref[...])


def _combine(x1, yb, srow, top_w, plan, g_final, tm):
    n, d = x1.shape
    nu = COMBINE_TILES
    nt = n // (nu * tm)
    rows = _stage_rows(tm)
    grid_spec = pltpu.PrefetchScalarGridSpec(
        num_scalar_prefetch=2,
        grid=(nt,),
        in_specs=[
            pl.BlockSpec((nu * tm, d), lambda t, *_: (t, 0)),
            pl.BlockSpec((nu * tm, LANES), lambda t, *_: (t, 0)),
            pl.BlockSpec((nu * tm, LANES), lambda t, *_: (t, 0)),
            pl.BlockSpec((1, d), lambda t, *_: (0, 0)),
            pl.BlockSpec(memory_space=pl.ANY),
        ],
        out_specs=pl.BlockSpec((nu * tm, d), lambda t, *_: (t, 0)),
        scratch_shapes=[pltpu.VMEM((2, nu, rows, d), BF16), pltpu.SemaphoreType.DMA((2, nu))],
    )
    return pl.pallas_call(
        _combine_kernel,
        grid_spec=grid_spec,
        out_shape=jax.ShapeDtypeStruct((n, d), F32),
        compiler_params=_cparams(("arbitrary",)),
    )(plan["gran_dst"], plan["tot_gran"], x1, srow, top_w, g_final, yb)


def _rope_tables(seq):
    half = ATTN_HEAD_DIM // 2
    inv_freq = np.float32(ROPE_THETA) ** (-np.arange(0, half, 2, dtype=np.float32) / np.float32(half))
    pos = np.arange(seq)
    ang_r = (pos // GRID_W).astype(np.float32)[:, None] * inv_freq
    ang_c = (pos % GRID_W).astype(np.float32)[:, None] * inv_freq
    cos_t = np.concatenate([np.cos(ang_r), np.cos(ang_c)] * 2, axis=-1)
    sin_t = np.concatenate([-np.sin(ang_r), -np.sin(ang_c), np.sin(ang_r), np.sin(ang_c)], axis=-1)
    return jnp.asarray(cos_t, F32), jnp.asarray(sin_t, F32)


def _token_mixer(x2, batch, seq, g_mix, w_in, conv_w, conv_b, dt_bias_f, dt_bias_b, a_log_f, a_log_b, d_skip,
                 g_ssm, q_norm_g, k_norm_g, w_br_ssm, w_br_attn, w_out):
    n, d = x2.shape
    z_end = SSM_INNER
    xbc_end = z_end + CONV_CH
    dtf_end = xbc_end + SSM_HEADS
    dtb_end = dtf_end + SSM_HEADS
    q_end = dtb_end + ATTN_HEADS * ATTN_HEAD_DIM
    k_end = q_end + ATTN_KV_HEADS * ATTN_HEAD_DIM
    v_end = k_end + ATTN_KV_HEADS * ATTN_HEAD_DIM
    head_cols = lambda w: _rope_head_order(w.reshape(d, -1, ATTN_HEAD_DIM)).reshape(d, -1)
    w_main = jnp.concatenate([w_in[:, :z_end], head_cols(w_in[:, dtb_end:q_end]), w_in[:, v_end:],
                              w_in[:, z_end:xbc_end], head_cols(w_in[:, q_end:k_end]), w_in[:, k_end:v_end]],
                             axis=1).astype(BF16)
    w_dt = jnp.pad(w_in[:, xbc_end:dtb_end], ((0, 0), (0, LANES - 2 * SSM_HEADS))).astype(BF16)

    proj, dt, dtt = _in_proj(x2, g_mix.reshape(1, d), w_main, w_dt)
    proj3 = proj.reshape(batch, seq, PROJ_COLS)

    xbc = _conv(proj3, conv_w, conv_b.reshape(1, CONV_CH))

    dt3 = dt.reshape(batch, seq, LANES)
    bias = jnp.concatenate([dt_bias_f, dt_bias_b])
    alog = jnp.concatenate([a_log_f, a_log_b])
    pad_row = lambda v: jnp.pad(v, (0, LANES - 2 * SSM_HEADS)).reshape(1, LANES)
    y_f, y_b = _ssd(xbc, dt3, dtt, pad_row(bias), bias.reshape(-1, 1), pad_row(alog), alog.reshape(-1, 1))

    cos_t, sin_t = _rope_tables(seq)
    q_rot, k_rot, vt = _qk_prep(proj, cos_t, sin_t, _rope_head_order(q_norm_g).reshape(1, -1),
                            _rope_head_order(k_norm_g).reshape(1, -1), seq)
    attn = _flash(q_rot.reshape(batch, seq, -1), k_rot.reshape(batch, seq, -1), vt, q_norm_g, k_norm_g)

    return _merge(y_f.reshape(n, -1), y_b.reshape(n, -1), xbc.reshape(n, CONV_CH), proj, attn.reshape(n, -1),
                  x2, jnp.repeat(d_skip, SSM_HEAD_DIM).reshape(1, -1), g_ssm.reshape(1, -1),
                  w_br_ssm.astype(BF16), w_br_attn.astype(BF16), w_out.astype(BF16))


def _moe_and_final_norm(x1, g_ffn, w_router, b_router, w_mlp1, b_mlp1, w_mlp2, b_mlp2, g_final):
    n, d = x1.shape
    tm = min(MOE_TILE, n)
    w_t = w_router.T
    w_hi = w_t.astype(BF16)
    w_lo = (w_t - w_hi.astype(F32)).astype(BF16)
    g_ffn_row = g_ffn.reshape(1, d)
    h2, srow_t, srow, top_w, cnt = _router(x1, g_ffn_row, jnp.concatenate([w_hi, w_lo, w_hi], axis=1),
                                           b_router.reshape(N_EXPERTS, 1))
    plan = _routing_plan(cnt[:, :, 0].astype(jnp.int32), n, tm, MOE_BLOCK)
    xs = _dispatch(h2, srow_t, plan, tm)
    yb = _experts(xs, plan, w_mlp1, b_mlp1[:, None, :], w_mlp2, b_mlp2[:, None, :])
    return _combine(x1, yb, srow, top_w, plan, g_final.reshape(1, d), tm)


def kernel(x, g_mix, w_in, conv_w, conv_b, dt_bias_f, dt_bias_b, a_log_f, a_log_b, d_skip, g_ssm, q_norm_g,
           k_norm_g, w_br_ssm, w_br_attn, w_out, g_ffn, w_router, b_router, w_mlp1, b_mlp1, w_mlp2, b_mlp2,
           g_final):
    batch, seq, d = x.shape
    assert g_mix.shape[0] == 1, "single-layer model: the final rmsnorm is fused into the MoE combine"
    x2 = x.reshape(batch * seq, d)
    x1 = _token_mixer(x2, batch, seq, g_mix[0], w_in[0], conv_w[0], conv_b[0], dt_bias_f[0], dt_bias_b[0],
                      a_log_f[0], a_log_b[0], d_skip[0], g_ssm[0], q_norm_g[0], k_norm_g[0], w_br_ssm[0],
                      w_br_attn[0], w_out[0])
    out = _moe_and_final_norm(x1, g_ffn[0], w_router[0], b_router[0], w_mlp1[0], b_mlp1[0], w_mlp2[0],
                              b_mlp2[0], g_final)
    return out.reshape(batch, seq, d)
```
